```python
import math
import jax
import jax.numpy as jnp
from jax import lax
import numpy as np

D_MODEL = 2048
BATCH = 1
SEQ = 16384
DEPTH = 2

GRID_W = 64
CTX_LEN = 256
HEAD_DIM = 128
MIX_WIDTH = D_MODEL
MLSTM_WIDTH = MIX_WIDTH // 2
NA_WIDTH = MIX_WIDTH - MLSTM_WIDTH
N_MLSTM_HEADS = MLSTM_WIDTH // HEAD_DIM
N_NA_HEADS = NA_WIDTH // HEAD_DIM
HGRN_WIDTH = MIX_WIDTH // 2
S5_WIDTH = MIX_WIDTH - HGRN_WIDTH
N_HGRN_HEADS = HGRN_WIDTH // HEAD_DIM
S5_GROUP = 16
S5_GROUPS = S5_WIDTH // S5_GROUP
S5_STATE = 64
NA_WIN_ROWS = 8
NA_WIN_COLS = 16
CHUNK = 64
ROPE_BASE = 10000.0
PEER_HEADS = 8
PEER_N_KEYS = 128
PEER_N_EXPERTS = PEER_N_KEYS * PEER_N_KEYS
PEER_TOPK = 16
PEER_QDIM = 256
PEER_BLOCK = 128
LN_EPS = 1e-5
DEEPNORM_ALPHA = (2.0 * DEPTH) ** 0.25
DEEPNORM_BETA = (8.0 * DEPTH) ** -0.25
N_EVEN = (DEPTH + 1) // 2
N_ODD = DEPTH // 2
W_AB = 4 * MLSTM_WIDTH + 4 * N_MLSTM_HEADS + 3 * NA_WIDTH
W_CD = 5 * HGRN_WIDTH + S5_WIDTH

kernel_name = 'hybrid_mlstm_natten_hgrn2_s5_peer_dit'


def layer_norm(x, g, b):
    xf = x.astype(jnp.float32)
    mu = xf.mean(-1, keepdims=True)
    var = jnp.square(xf - mu).mean(-1, keepdims=True)
    return ((xf - mu) * lax.rsqrt(var + LN_EPS) * g + b).astype(x.dtype)


def head_norm(x):
    xf = x.astype(jnp.float32)
    mu = xf.mean(-1, keepdims=True)
    return (xf - mu) * lax.rsqrt(jnp.square(xf - mu).mean(-1, keepdims=True) + LN_EPS)


def to_heads(a):
    return a.reshape(a.shape[:-1] + (-1, HEAD_DIM))


def split_cols(p, sizes):
    offsets = np.cumsum(sizes)[:-1].tolist()
    return jnp.split(p, offsets, axis=-1)


def adaln(cvec, w, b):
    m = jax.nn.silu(cvec) @ w + b
    return jnp.split(m, 6, axis=-1)


def _rotate(x, pos):
    n = x.shape[-1] // 2
    freqs = ROPE_BASE ** (-jnp.arange(n, dtype=jnp.float32) / n)
    ang = pos.astype(jnp.float32)[:, None] * freqs
    cos, sin = jnp.cos(ang)[None, :, None, :], jnp.sin(ang)[None, :, None, :]
    x1, x2 = x[..., :n], x[..., n:]
    return jnp.concatenate([x1 * cos - x2 * sin, x1 * sin + x2 * cos], axis=-1)


def axial_rope(x, row_pos, col_pos):
    half = x.shape[-1] // 2
    return jnp.concatenate([_rotate(x[..., :half], row_pos), _rotate(x[..., half:], col_pos)], axis=-1)


def _chunks(a):
    B, T = a.shape[:2]
    return jnp.swapaxes(a.reshape((B, T // CHUNK, CHUNK) + a.shape[2:]), 0, 1)


def _unchunks(a):
    nc, B, L = a.shape[:3]
    return jnp.swapaxes(a, 0, 1).reshape((B, nc * L) + a.shape[3:])


def run_direction(scan_fn, ctx_in, lat_in, init, reverse):
    if reverse:
        ctx_in = tuple(jnp.flip(a, 1) for a in ctx_in)
        lat_in = tuple(jnp.flip(a, 1) for a in lat_in)
    ctx_out, ctx_state = scan_fn(*ctx_in, init)
    lat_out, _ = scan_fn(*lat_in, ctx_state)
    if reverse:
        ctx_out, lat_out = jnp.flip(ctx_out, 1), jnp.flip(lat_out, 1)
    return lat_out, ctx_out


def mlstm_scan(q, k, v, i_pre, logf, state):
    L = CHUNK
    causal = jnp.tril(jnp.ones((L, L), dtype=bool))
    xs = tuple(_chunks(a.astype(jnp.float32)) for a in (q, k, v, i_pre, logf))

    def step(carry, inp):
        C, n, m = carry
        qc, kc, vc, ic, fc = inp
        b = jnp.cumsum(fc, axis=1).swapaxes(1, 2)
        ih = ic.swapaxes(1, 2)
        logw = jnp.where(causal, b[..., :, None] - b[..., None, :] + ih[..., None, :], -jnp.inf)
        inter = b + m[..., None]
        m_t = jnp.maximum(inter, logw.max(-1))
        w_inter = jnp.exp(inter - m_t)
        s = jnp.einsum('bthd,bshd->bhts', qc, kc) * jnp.exp(logw - m_t[..., None])
        num = jnp.einsum('bhts,bshe->bthe', s, vc) + jnp.einsum('bthd,bhde->bthe', qc, C) * w_inter.swapaxes(1, 2)[..., None]
        den = s.sum(-1) + jnp.einsum('bthd,bhd->bht', qc, n) * w_inter
        den = jnp.maximum(jnp.abs(den), jnp.exp(-m_t))
        h = num / den.swapaxes(1, 2)[..., None]
        m_new = m_t[..., -1]
        w_old = jnp.exp(b[..., -1] + m - m_new)
        w_s = jnp.exp(b[..., -1:] - b + ih - m_new[..., None])
        C = w_old[..., None, None] * C + jnp.einsum('bhs,bshd,bshe->bhde', w_s, kc, vc)
        n = w_old[..., None] * n + jnp.einsum('bhs,bshd->bhd', w_s, kc)
        return (C, n, m_new), h

    state, hs = lax.scan(step, state, xs)
    return _unchunks(hs), state


def gla_scan(q, k, v, logf, state):
    L = CHUNK
    causal = jnp.tril(jnp.ones((L, L), dtype=bool))[None, :, :, None, None]
    xs = tuple(_chunks(a.astype(jnp.float32)) for a in (q, k, v, logf))

    def step(S, inp):
        qc, kc, vc, fc = inp
        b = jnp.cumsum(fc, axis=1)
        decay = jnp.exp(jnp.where(causal, b[:, :, None] - b[:, None], -jnp.inf))
        att = jnp.einsum('btshk,bshk->bhts', qc[:, :, None] * decay, kc)
        o = jnp.einsum('bhts,bshv->bthv', att, vc) + jnp.einsum('bthk,bhkv->bthv', qc * jnp.exp(b), S)
        bL = b[:, -1]
        S = jnp.exp(bL)[..., None] * S + jnp.einsum('bshk,bshv->bhkv', kc * jnp.exp(bL[:, None] - b), vc)
        return S, o

    state, os = lax.scan(step, state, xs)
    return _unchunks(os), state


def zoh(a_re, a_im, log_dt, b_re, b_im):
    dt = jnp.exp(log_dt)[:, None]
    mag = jnp.exp(a_re * dt)
    abar_re, abar_im = mag * jnp.cos(a_im * dt), mag * jnp.sin(a_im * dt)
    nr, ni = abar_re - 1.0, abar_im
    den = jnp.square(a_re) + jnp.square(a_im)
    cr = (nr * a_re + ni * a_im) / den
    ci = (ni * a_re - nr * a_im) / den
    bbar_re = cr[..., None] * b_re - ci[..., None] * b_im
    bbar_im = cr[..., None] * b_im + ci[..., None] * b_re
    return abar_re, abar_im, bbar_re, bbar_im


def _complex_affine_combine(e1, e2):
    a1r, a1i, b1r, b1i = e1
    a2r, a2i, b2r, b2i = e2
    return (a2r * a1r - a2i * a1i, a2r * a1i + a2i * a1r,
            a2r * b1r - a2i * b1i + b2r, a2r * b1i + a2i * b1r + b2i)


def s5_scan(u, state, abar_re, abar_im, bbar_re, bbar_im, c_re, c_im):
    bu_re = jnp.einsum('btgi,gpi->btgp', u, bbar_re)
    bu_im = jnp.einsum('btgi,gpi->btgp', u, bbar_im)
    a_re = jnp.broadcast_to(abar_re, bu_re.shape)
    a_im = jnp.broadcast_to(abar_im, bu_re.shape)
    p_re, p_im, h_re, h_im = lax.associative_scan(_complex_affine_combine, (a_re, a_im, bu_re, bu_im), axis=1)
    h0r, h0i = state[0][:, None], state[1][:, None]
    h_re, h_im = h_re + p_re * h0r - p_im * h0i, h_im + p_re * h0i + p_im * h0r
    y = jnp.einsum('btgp,gip->btgi', h_re, c_re) - jnp.einsum('btgp,gip->btgi', h_im, c_im)
    return y, (h_re[:, -1], h_im[:, -1])


def neighbourhood_attention(q, k, v, k_ctx, v_ctx, rpb):
    B, T, H, d = q.shape
    rows = T // GRID_W
    wr, wc = min(NA_WIN_ROWS, rows), NA_WIN_COLS
    qg = q.reshape(B, rows, GRID_W, H, d)
    kg = k.reshape(B, rows, GRID_W, H, d)
    vg = v.reshape(B, rows, GRID_W, H, d)
    col = jnp.arange(GRID_W)
    col_idx = jnp.clip(col - wc // 2, 0, GRID_W - wc)[:, None] + jnp.arange(wc)
    dc = col_idx - col[:, None]

    def row_block(r):
        r0 = jnp.clip(r - wr // 2, 0, rows - wr)
        kq = lax.dynamic_slice_in_dim(kg, r0, wr, axis=1)[:, :, col_idx]
        vq = lax.dynamic_slice_in_dim(vg, r0, wr, axis=1)[:, :, col_idx]
        qr = lax.dynamic_index_in_dim(qg, r, axis=1, keepdims=False)
        dr = r0 + jnp.arange(wr) - r
        bias = rpb[:, (dr + NA_WIN_ROWS - 1)[None, :, None], (dc + NA_WIN_COLS - 1)[:, None, :]]
        s_loc = jnp.einsum('bqhd,brqchd->bhqrc', qr, kq).astype(jnp.float32) + bias[None]
        s_ctx = jnp.einsum('bqhd,bkhd->bhqk', qr, k_ctx).astype(jnp.float32)
        p = jax.nn.softmax(jnp.concatenate([s_loc.reshape(B, H, GRID_W, wr * wc), s_ctx], -1), axis=-1)
        p_loc = p[..., :wr * wc].reshape(B, H, GRID_W, wr, wc).astype(v.dtype)
        p_ctx = p[..., wr * wc:].astype(v.dtype)
        return jnp.einsum('bhqrc,brqchd->bqhd', p_loc, vq) + jnp.einsum('bhqk,bkhd->bqhd', p_ctx, v_ctx)

    out = lax.map(row_block, jnp.arange(rows))
    return jnp.moveaxis(out, 0, 1).reshape(B, T, H, d)


def context_attention(q, k, v):
    p = jax.nn.softmax(jnp.einsum('bqhd,bkhd->bhqk', q, k).astype(jnp.float32), axis=-1)
    return jnp.einsum('bhqk,bkhd->bqhd', p.astype(v.dtype), v)


def mixer_ab(h, hc, w_in, w_out, igate_b, fgate_b, rpb, need_ctx):
    B, T, _ = h.shape
    f32 = jnp.float32
    pos = jnp.arange(T)
    row_pos, col_pos = pos // GRID_W, pos % GRID_W
    sizes = (MLSTM_WIDTH,) * 4 + (2 * N_MLSTM_HEADS,) * 2 + (NA_WIDTH,) * 3
    lat = split_cols(h @ w_in, sizes)
    cx = split_cols(hc @ w_in, sizes)

    def mlstm_inputs(p, rotary):
        q, k, v = (to_heads(a).astype(f32) for a in p[:3])
        if rotary:
            q, k = axial_rope(q, row_pos, col_pos), axial_rope(k, row_pos, col_pos)
        gshape = p[4].shape[:2] + (2, N_MLSTM_HEADS)
        i_pre = p[4].astype(f32).reshape(gshape) + igate_b
        logf = jax.nn.log_sigmoid(p[5].astype(f32).reshape(gshape) + fgate_b)
        return q * HEAD_DIM ** -0.5, k, v, i_pre, logf

    ql, kl, vl, il, fl = mlstm_inputs(lat, True)
    qc, kc, vc, ic, fc = mlstm_inputs(cx, False)
    init = (jnp.zeros((B, N_MLSTM_HEADS, HEAD_DIM, HEAD_DIM), f32),
            jnp.zeros((B, N_MLSTM_HEADS, HEAD_DIM), f32),
            jnp.zeros((B, N_MLSTM_HEADS), f32))
    dirs = [run_direction(mlstm_scan, (qc, kc, vc, ic[:, :, dd], fc[:, :, dd]),
                          (ql, kl, vl, il[:, :, dd], fl[:, :, dd]), init, dd == 1) for dd in range(2)]

    def mlstm_out(hsum, o):
        return (head_norm(hsum) * jax.nn.sigmoid(to_heads(o).astype(f32))).reshape(o.shape).astype(h.dtype)

    scale = HEAD_DIM ** -0.5
    qn, kn, vn = (to_heads(a) for a in lat[6:9])
    qnc, knc, vnc = (to_heads(a) for a in cx[6:9])
    y_na = neighbourhood_attention(qn * scale, kn, vn, knc, vnc, rpb).reshape(B, T, NA_WIDTH).astype(h.dtype)
    y_lat = jnp.concatenate([mlstm_out(dirs[0][0] + dirs[1][0], lat[3]), y_na], axis=-1) @ w_out
    if not need_ctx:
        return y_lat, None
    y_na_c = context_attention(qnc * scale, knc, vnc).reshape(hc.shape[:2] + (NA_WIDTH,)).astype(h.dtype)
    y_ctx = jnp.concatenate([mlstm_out(dirs[0][1] + dirs[1][1], cx[3]), y_na_c], axis=-1) @ w_out
    return y_lat, y_ctx


def mixer_cd(h, hc, w_in, w_out, lb, s5p, s5_d, glu_w, glu_b, need_ctx):
    B, T, _ = h.shape
    f32 = jnp.float32
    sizes = (HGRN_WIDTH,) * 5 + (S5_WIDTH,)
    lat = split_cols(h @ w_in, sizes)
    cx = split_cols(hc @ w_in, sizes)
    lb_h = lb.reshape(N_HGRN_HEADS, HEAD_DIM)

    def gate(f):
        logf = jnp.logaddexp(jnp.log(lb_h), jnp.log1p(-lb_h) + jax.nn.log_sigmoid(to_heads(f).astype(f32)))
        return -jnp.expm1(logf), logf

    def hgrn_inputs(p):
        q = to_heads(jax.nn.silu(p[0])).astype(f32)
        i = to_heads(p[1]).astype(f32)
        return q, i, gate(p[2]), gate(p[3])

    ql, il, (kfl, lfl), (kbl, lbl) = hgrn_inputs(lat)
    qc, ic, (kfc, lfc), (kbc, lbc) = hgrn_inputs(cx)
    s0 = jnp.zeros((B, N_HGRN_HEADS, HEAD_DIM, HEAD_DIM), f32)
    hg_fw = run_direction(gla_scan, (qc, kfc, ic, lfc), (ql, kfl, il, lfl), s0, False)
    hg_bw = run_direction(gla_scan, (qc, kbc, ic, lbc), (ql, kbl, il, lbl), s0, True)

    def hgrn_out(o, g):
        return (head_norm(o) * jax.nn.silu(to_heads(g).astype(f32))).reshape(g.shape).astype(h.dtype)

    u_l = lat[5].astype(f32).reshape(B, T, S5_GROUPS, S5_GROUP)
    u_c = cx[5].astype(f32).reshape(B, cx[5].shape[1], S5_GROUPS, S5_GROUP)
    h0 = (jnp.zeros((B, S5_GROUPS, S5_STATE), f32), jnp.zeros((B, S5_GROUPS, S5_STATE), f32))

    def s5_direction(dd):
        a_re, a_im, log_dt, b_re, b_im, c_re, c_im = (p[dd].astype(f32) for p in s5p)
        abar_re, abar_im, bbar_re, bbar_im = zoh(a_re, a_im, log_dt, b_re, b_im)
        fn = lambda u, s: s5_scan(u, s, abar_re, abar_im, bbar_re, bbar_im, c_re, c_im)
        return run_direction(fn, (u_c,), (u_l,), h0, dd == 1)

    s5_fw, s5_bw = s5_direction(0), s5_direction(1)
    d_skip = s5_d.astype(f32).reshape(S5_GROUPS, S5_GROUP)

    def s5_out(u, y):
        y = jax.nn.gelu((y + d_skip * u).reshape(u.shape[:2] + (S5_WIDTH,)))
        return (y * jax.nn.sigmoid(y @ glu_w + glu_b)).astype(h.dtype)

    y_lat = jnp.concatenate([hgrn_out(hg_fw[0] + hg_bw[0], lat[4]), s5_out(u_l, s5_fw[0] + s5_bw[0])], axis=-1) @ w_out
    if not need_ctx:
        return y_lat, None
    y_ctx = jnp.concatenate([hgrn_out(hg_fw[1] + hg_bw[1], cx[4]), s5_out(u_c, s5_fw[1] + s5_bw[1])], axis=-1) @ w_out
    return y_lat, y_ctx


def peer(h, w_q, sub_keys, u_tab, v_tab):
    B, T, D = h.shape
    K = PEER_TOPK

    def block(xb):
        n = xb.shape[0]
        q = (xb @ w_q).reshape(n, PEER_HEADS, 2, PEER_QDIM // 2)
        s = jnp.einsum('thpc,hpnc->thpn', q, sub_keys).astype(jnp.float32)
        top_s, top_i = lax.top_k(s, K)
        cand_s = (top_s[:, :, 0, :, None] + top_s[:, :, 1, None, :]).reshape(n, PEER_HEADS, K * K)
        cand_e = (top_i[:, :, 0, :, None] * PEER_N_KEYS + top_i[:, :, 1, None, :]).reshape(n, PEER_HEADS, K * K)
        best_s, best_j = lax.top_k(cand_s, K)
        experts = jnp.take_along_axis(cand_e, best_j, axis=-1).reshape(n, PEER_HEADS * K)
        gates = jax.nn.softmax(best_s, axis=-1).reshape(n, PEER_HEADS * K)
        act = jax.nn.gelu(jnp.einsum('td,ted->te', xb, u_tab[experts]).astype(jnp.float32)) * gates
        return jnp.einsum('te,ted->td', act, v_tab[experts].astype(jnp.float32)).astype(h.dtype)

    out = lax.map(block, h.reshape(-1, PEER_BLOCK, D))
    return out.reshape(B, T, D)


def setup_inputs(seed: int = 0) -> dict:
    key = jax.random.key(seed)
    ks = iter(jax.random.split(key, 40))
    f32 = jnp.float32

    def nrm(shape, scale):
        return scale * jax.random.normal(next(ks), shape, f32)

    D = D_MODEL
    beta = DEEPNORM_BETA
    n_idx = jnp.arange(S5_STATE, dtype=f32)
    return {
        'x': nrm((BATCH, SEQ, D), 1.0),
        'c': nrm((BATCH, D), 1.0),
        'ctx': nrm((BATCH, CTX_LEN, D), 1.0),
        'c_ctx': nrm((D,), 1.0),
        'ada_w': nrm((DEPTH, D, 6 * D), 0.5 * D ** -0.5),
        'ada_b': nrm((DEPTH, 6 * D), 0.01),
        'ln_g': 1.0 + nrm((DEPTH, 2, D), 0.02),
        'ln_b': nrm((DEPTH, 2, D), 0.02),
        'ab_w_in': nrm((N_EVEN, D, W_AB), D ** -0.5),
        'ab_w_out': nrm((N_EVEN, MIX_WIDTH, D), beta * MIX_WIDTH ** -0.5),
        'mlstm_igate_b': nrm((N_EVEN, 2, N_MLSTM_HEADS), 0.1),
        'mlstm_fgate_b': jnp.linspace(3.0, 6.0, N_MLSTM_HEADS, dtype=f32) + nrm((N_EVEN, 2, N_MLSTM_HEADS), 0.1),
        'na_rpb': nrm((N_EVEN, N_NA_HEADS, 2 * NA_WIN_ROWS - 1, 2 * NA_WIN_COLS - 1), 0.1),
        'cd_w_in': nrm((N_ODD, D, W_CD), D ** -0.5),
        'cd_w_out': nrm((N_ODD, MIX_WIDTH, D), beta * MIX_WIDTH ** -0.5),
        'hgrn_lb_logits': nrm((DEPTH, HGRN_WIDTH), 1.0),
        's5_a_re': -0.5 + nrm((N_ODD, 2, S5_GROUPS, S5_STATE), 0.01),
        's5_a_im': math.pi * n_idx + nrm((N_ODD, 2, S5_GROUPS, S5_STATE), 0.01),
        's5_log_dt': jax.random.uniform(next(ks), (N_ODD, 2, S5_GROUPS), f32, math.log(1e-3), math.log(1e-1)),
        's5_b_re': nrm((N_ODD, 2, S5_GROUPS, S5_STATE, S5_GROUP), (2 * S5_GROUP) ** -0.5),
        's5_b_im': nrm((N_ODD, 2, S5_GROUPS, S5_STATE, S5_GROUP), (2 * S5_GROUP) ** -0.5),
        's5_c_re': nrm((N_ODD, 2, S5_GROUPS, S5_GROUP, S5_STATE), (2 * S5_STATE) ** -0.5),
        's5_c_im': nrm((N_ODD, 2, S5_GROUPS, S5_GROUP, S5_STATE), (2 * S5_STATE) ** -0.5),
        's5_d': nrm((N_ODD, S5_WIDTH), 1.0),
        's5_glu_w': nrm((N_ODD, S5_WIDTH, S5_WIDTH), S5_WIDTH ** -0.5),
        's5_glu_b': nrm((N_ODD, S5_WIDTH), 0.01),
        'peer_w_q': nrm((DEPTH, D, PEER_HEADS * PEER_QDIM), D ** -0.5),
        'peer_sub_keys': nrm((DEPTH, PEER_HEADS, 2, PEER_N_KEYS, PEER_QDIM // 2), (PEER_QDIM // 2) ** -0.5),
        'peer_u': nrm((DEPTH, PEER_N_EXPERTS, D), D ** -0.5),
        'peer_v': nrm((DEPTH, PEER_N_EXPERTS, D), beta),
    }


def reference(x, c, ctx, c_ctx, ada_w, ada_b, ln_g, ln_b,
              ab_w_in, ab_w_out, mlstm_igate_b, mlstm_fgate_b, na_rpb,
              cd_w_in, cd_w_out, hgrn_lb_logits,
              s5_a_re, s5_a_im, s5_log_dt, s5_b_re, s5_b_im, s5_c_re, s5_c_im,
              s5_d, s5_glu_w, s5_glu_b,
              peer_w_q, peer_sub_keys, peer_u, peer_v):
    lb_soft = jax.nn.softmax(hgrn_lb_logits.astype(jnp.float32), axis=0)
    lower_bounds = jnp.cumsum(lb_soft, axis=0) - lb_soft[0]
    alpha = DEEPNORM_ALPHA
    for l in range(DEPTH):
        need_ctx = l < DEPTH - 1
        j = l // 2
        sh1, sc1, g1, sh2, sc2, g2 = (m[:, None] for m in adaln(c, ada_w[l], ada_b[l]))
        csh1, csc1, cg1, csh2, csc2, cg2 = adaln(c_ctx, ada_w[l], ada_b[l])
        h = x * (1 + sc1) + sh1
        hc = ctx * (1 + csc1) + csh1
        if l % 2 == 0:
            y, yc = mixer_ab(h, hc, ab_w_in[j], ab_w_out[j], mlstm_igate_b[j], mlstm_fgate_b[j], na_rpb[j], need_ctx)
        else:
            s5p = (s5_a_re[j], s5_a_im[j], s5_log_dt[j], s5_b_re[j], s5_b_im[j], s5_c_re[j], s5_c_im[j])
            y, yc = mixer_cd(h, hc, cd_w_in[j], cd_w_out[j], lower_bounds[l], s5p, s5_d[j], s5_glu_w[j], s5_glu_b[j], need_ctx)
        x = layer_norm(alpha * x + g1 * y, ln_g[l, 0], ln_b[l, 0])
        x = layer_norm(alpha * x + g2 * peer(x * (1 + sc2) + sh2, peer_w_q[l], peer_sub_keys[l], peer_u[l], peer_v[l]),
                       ln_g[l, 1], ln_b[l, 1])
        if need_ctx:
            ctx = layer_norm(alpha * ctx + cg1 * yc, ln_g[l, 0], ln_b[l, 0])
            ctx = layer_norm(alpha * ctx + cg2 * peer(ctx * (1 + csc2) + csh2, peer_w_q[l], peer_sub_keys[l], peer_u[l], peer_v[l]),
                             ln_g[l, 1], ln_b[l, 1])
    return x
```

```python
import functools
import math

import numpy as np
import jax
import jax.numpy as jnp
from jax import lax
from jax.experimental import pallas as pl
from jax.experimental.pallas import tpu as pltpu

F32 = jnp.float32
BF16 = jnp.bfloat16

HEAD_DIM = 128
GRID_W = 64
N_HEADS = 8
HALF = N_HEADS * HEAD_DIM
NA_WIN_ROWS = 8
NA_WIN_COLS = 16
ROPE_BASE = 10000.0
S5_GROUP = 16
S5_STATE = 64
S5_GROUPS = HALF // S5_GROUP
PEER_HEADS = 8
PEER_N_KEYS = 128
PEER_TOPK = 16
PEER_TOP_ROWS = 24
LN_EPS = 1e-5
DEPTH = 2
ALPHA = (2.0 * DEPTH) ** 0.25
QK_SCALE = HEAD_DIM ** -0.5

ROW_TILE = 256
SCAN_CHUNK = 128
S5_CHUNK = 32
S5_SCAN_BLOCK = 8
NA_QROWS = 8
PEER_P1_TILE = 256
PEER_TOK_TILE = 640
PEER_EXP_TILE = 512
NEG = -1e30
GLA_SAFE_LOG = -80.0
VMEM_LIMIT = 56 * 1024 * 1024


def _cparams(n_axes):
    return pltpu.CompilerParams(dimension_semantics=("arbitrary",) * n_axes,
                                vmem_limit_bytes=VMEM_LIMIT)


def _nt(a, b):
    return lax.dot_general(a, b, (((1,), (1,)), ((), ())), preferred_element_type=F32)


def _tn(a, b):
    return lax.dot_general(a, b, (((0,), (0,)), ((), ())), preferred_element_type=F32)


def _nn(a, b):
    return jnp.dot(a, b, preferred_element_type=F32)


def _split3(x):
    hi = x.astype(BF16)
    r = x - hi.astype(F32)
    mid = r.astype(BF16)
    lo = (r - mid.astype(F32)).astype(BF16)
    return hi, mid, lo


def _log_sigmoid(x):
    return jnp.minimum(x, 0.0) - jnp.log1p(jnp.exp(-jnp.abs(x)))


def _sigmoid(x):
    return 1.0 / (1.0 + jnp.exp(-x))


def _gelu_tanh(x):
    return 0.5 * x * (1.0 + jnp.tanh(math.sqrt(2.0 / math.pi) * (x + 0.044715 * (x * x * x))))


def _head_norm(x):
    mu = jnp.mean(x, axis=-1, keepdims=True)
    xc = x - mu
    return xc * lax.rsqrt(jnp.mean(xc * xc, axis=-1, keepdims=True) + LN_EPS)


def _adaln_kernel(ct_ref, w_ref, b_ref, o_ref):
    ct = ct_ref[...]
    s = ct * _sigmoid(ct)
    w = w_ref[...]
    o_ref[0:1, :] = jnp.sum(s[:, 0:1] * w, axis=0, keepdims=True) + b_ref[...]
    o_ref[1:2, :] = jnp.sum(s[:, 1:2] * w, axis=0, keepdims=True) + b_ref[...]


def _adaln(c_lat, c_ctx, w, b):
    d = w.shape[0]
    n = w.shape[1]
    tn = n // 8
    ct = jnp.stack([c_lat.reshape(d), c_ctx.reshape(d)], axis=1)
    out = pl.pallas_call(
        _adaln_kernel,
        grid=(n // tn,),
        in_specs=[pl.BlockSpec((d, 2), lambda j: (0, 0)),
                  pl.BlockSpec((d, tn), lambda j: (0, j)),
                  pl.BlockSpec((1, tn), lambda j: (0, j))],
        out_specs=pl.BlockSpec((2, tn), lambda j: (0, j)),
        out_shape=jax.ShapeDtypeStruct((2, n), F32),
        compiler_params=_cparams(1),
    )(ct, w, b.reshape(1, n))
    return jnp.pad(out.reshape(2, 6, d), ((0, 0), (0, 2), (0, 0)))


def _inproj_kernel(x_ref, m_ref, w_ref, o_ref, xs_ref):
    @pl.when(pl.program_id(1) == 0)
    def _():
        xs_ref[...] = (x_ref[...] * (1.0 + m_ref[1:2, :]) + m_ref[0:1, :]).astype(BF16)

    o_ref[...] = _nn(xs_ref[...], w_ref[...])


def _inproj(x_all, mods, w_bf, n_lat_tiles, tn):
    t_all, d = x_all.shape
    n = w_bf.shape[1]
    return pl.pallas_call(
        _inproj_kernel,
        grid=(t_all // ROW_TILE, n // tn),
        in_specs=[pl.BlockSpec((ROW_TILE, d), lambda i, j: (i, 0)),
                  pl.BlockSpec((None, 8, d), lambda i, j: (jnp.where(i < n_lat_tiles, 0, 1), 0, 0)),
                  pl.BlockSpec((d, tn), lambda i, j: (0, j))],
        out_specs=pl.BlockSpec((ROW_TILE, tn), lambda i, j: (i, j)),
        out_shape=jax.ShapeDtypeStruct((t_all, n), F32),
        scratch_shapes=[pltpu.VMEM((ROW_TILE, d), BF16)],
        compiler_params=_cparams(2),
    )(x_all, mods, w_bf)


def _ln_rows(z, g, b):
    mu = jnp.mean(z, axis=-1, keepdims=True)
    zc = z - mu
    return zc * lax.rsqrt(jnp.mean(zc * zc, axis=-1, keepdims=True) + LN_EPS) * g + b


def _outproj_kernel(*refs, glu):
    if glu:
        (a1_ref, y5_ref, u5_ref, ds_ref, gw_ref, gb_ref, w1_ref, w2_ref, x_ref, m_ref, g_ref, b_ref,
         x1_ref, h2_ref) = refs
        y = _gelu_tanh(y5_ref[...] + ds_ref[...] * u5_ref[...])
        a2 = y * _sigmoid(_nn(y.astype(BF16), gw_ref[...]) + gb_ref[...])
    else:
        a1_ref, a2_ref, w1_ref, w2_ref, x_ref, m_ref, g_ref, b_ref, x1_ref, h2_ref = refs
        a2 = a2_ref[...]
    y = _nn(a1_ref[...].astype(BF16), w1_ref[...]) + _nn(a2.astype(BF16), w2_ref[...])
    x1 = _ln_rows(ALPHA * x_ref[...] + m_ref[2:3, :] * y, g_ref[...], b_ref[...])
    x1_ref[...] = x1
    h2_ref[...] = (x1 * (1.0 + m_ref[4:5, :]) + m_ref[3:4, :]).astype(BF16)


def _outproj(a1, a2_parts, w_out_bf, x_all, mods, ln_g, ln_b, n_lat_tiles, glu_params=None):
    t_all, d = x_all.shape
    half = a1.shape[1]
    row = lambda i: (i, 0)
    const = lambda i: (0, 0)
    act = pl.BlockSpec((ROW_TILE, half), row)
    if glu_params is None:
        a2_specs, a2_args = [act], list(a2_parts)
    else:
        d_skip, glu_w_bf, glu_b = glu_params
        a2_specs = [act, act, pl.BlockSpec((1, half), const), pl.BlockSpec((half, half), const),
                    pl.BlockSpec((1, half), const)]
        a2_args = list(a2_parts) + [d_skip.reshape(1, half), glu_w_bf, glu_b.reshape(1, half)]
    return pl.pallas_call(
        functools.partial(_outproj_kernel, glu=glu_params is not None),
        grid=(t_all // ROW_TILE,),
        in_specs=[act] + a2_specs + [
            pl.BlockSpec((half, d), const), pl.BlockSpec((half, d), const),
            pl.BlockSpec((ROW_TILE, d), row),
            pl.BlockSpec((None, 8, d), lambda i: (jnp.where(i < n_lat_tiles, 0, 1), 0, 0)),
            pl.BlockSpec((1, d), const), pl.BlockSpec((1, d), const)],
        out_specs=[pl.BlockSpec((ROW_TILE, d), row), pl.BlockSpec((ROW_TILE, d), row)],
        out_shape=[jax.ShapeDtypeStruct((t_all, d), F32), jax.ShapeDtypeStruct((t_all, d), BF16)],
        compiler_params=_cparams(1),
    )(a1, *a2_args, w_out_bf[:half], w_out_bf[half:], x_all, mods, ln_g.reshape(1, d), ln_b.reshape(1, d))


def _ln2_kernel(x_ref, p_ref, m_ref, g_ref, b_ref, o_ref):
    o_ref[...] = _ln_rows(ALPHA * x_ref[...] + m_ref[5:6, :] * p_ref[...], g_ref[...], b_ref[...])


def _ln2(x1, peer_out, mods, ln_g, ln_b, n_lat_tiles):
    t_all, d = x1.shape
    row = lambda i: (i, 0)
    const = lambda i: (0, 0)
    return pl.pallas_call(
        _ln2_kernel,
        grid=(t_all // ROW_TILE,),
        in_specs=[pl.BlockSpec((ROW_TILE, d), row), pl.BlockSpec((ROW_TILE, d), row),
                  pl.BlockSpec((None, 8, d), lambda i: (jnp.where(i < n_lat_tiles, 0, 1), 0, 0)),
                  pl.BlockSpec((1, d), const), pl.BlockSpec((1, d), const)],
        out_specs=pl.BlockSpec((ROW_TILE, d), row),
        out_shape=jax.ShapeDtypeStruct((t_all, d), F32),
        compiler_params=_cparams(1),
    )(x1, peer_out, mods, ln_g.reshape(1, d), ln_b.reshape(1, d))


def _scan_block(rev, n_chunks, n_lat_chunks):
    if rev:
        return lambda s: n_chunks - 1 - s
    n_ctx = n_chunks - n_lat_chunks
    return lambda s: jnp.where(s < n_ctx, s + n_lat_chunks, s - n_ctx)


def _scan_mask(rev, n):
    row = lax.broadcasted_iota(jnp.int32, (n, n), 0)
    col = lax.broadcasted_iota(jnp.int32, (n, n), 1)
    return (col >= row) if rev else (col <= row)


def _rope(x, cos, sin):
    lane = lax.broadcasted_iota(jnp.int32, x.shape, 1)
    partner = jnp.where((lane % 64) < 32, pltpu.roll(x, 96, 1), pltpu.roll(x, 32, 1))
    return x * cos + partner * sin


def _mlstm_kernel(*refs, rev):
    if rev:
        (q_ref, k_ref, v_ref, g_ref, gt_ref, cos_ref, sin_ref, bias_ref, biast_ref, hf_ref, o_ref,
         out_ref, c_ref, n_ref, m_ref) = refs
    else:
        (q_ref, k_ref, v_ref, g_ref, gt_ref, cos_ref, sin_ref, bias_ref, biast_ref,
         out_ref, c_ref, n_ref, m_ref) = refs
    n = q_ref.shape[0]

    @pl.when(pl.program_id(0) == 0)
    def _():
        c_ref[...] = jnp.zeros_like(c_ref)
        n_ref[...] = jnp.zeros_like(n_ref)
        m_ref[...] = jnp.zeros_like(m_ref)

    mask = _scan_mask(rev, n)
    tri = mask.astype(BF16)
    pre_c = g_ref[...] + bias_ref[...]
    pre_r = gt_ref[...] + biast_ref[...]
    b_c = sum(_nn(tri, p) for p in _split3(_log_sigmoid(pre_c)))
    b_r = sum(_nt(p, tri) for p in _split3(_log_sigmoid(pre_r)))
    cos = cos_ref[...]
    sin = sin_ref[...]
    last = 0 if rev else n - 1
    off = N_HEADS if rev else 0

    for h in range(N_HEADS):
        hs = slice(h * HEAD_DIM, (h + 1) * HEAD_DIM)
        ci, cf = off + h, 2 * N_HEADS + off + h
        bc, br = b_c[:, cf:cf + 1], b_r[cf:cf + 1, :]
        ic, ir = pre_c[:, ci:ci + 1], pre_r[ci:ci + 1, :]
        m_prev = m_ref[h:h + 1, 0:1]
        logw = jnp.where(mask, bc - br + ir, NEG)
        inter = bc + m_prev
        m_t = jnp.maximum(inter, jnp.max(logw, axis=1, keepdims=True))
        w_inter = jnp.exp(inter - m_t)
        q = _rope(q_ref[:, hs], cos, sin) * QK_SCALE
        k = _rope(k_ref[:, hs], cos, sin)
        qb, kb, vb = q.astype(BF16), k.astype(BF16), v_ref[:, hs].astype(BF16)
        s = _nt(qb, kb) * jnp.exp(logw - m_t)
        c_old = c_ref[h]
        n_old = n_ref[h:h + 1, :]
        num = _nn(s.astype(BF16), vb) + _nn(qb, c_old.astype(BF16)) * w_inter
        den = jnp.sum(s, axis=1, keepdims=True) + jnp.sum(q * n_old, axis=1, keepdims=True) * w_inter
        den = jnp.maximum(jnp.abs(den), jnp.exp(-m_t))
        hout = num / den
        m_new = m_t[last:last + 1, :]
        b_last = bc[last:last + 1, :]
        w_old = jnp.exp(b_last + m_prev - m_new)
        kw = k * jnp.exp(b_last - bc + ic - m_new)
        c_ref[h] = w_old * c_old + _tn(kw.astype(BF16), vb)
        n_ref[h:h + 1, :] = w_old * n_old + jnp.sum(kw, axis=0, keepdims=True)
        m_ref[h:h + 1, :] = jnp.broadcast_to(m_new, (1, HEAD_DIM))
        if rev:
            out_ref[:, hs] = _head_norm(hf_ref[:, hs] + hout) * _sigmoid(o_ref[:, hs])
        else:
            out_ref[:, hs] = hout


def _mlstm(proj, gates, gates_t, cos_t, sin_t, igate_b, fgate_b, n_lat_chunks):
    t_all = proj.shape[0]
    n = SCAN_CHUNK
    n_chunks = t_all // n
    bias = jnp.zeros((1, 128), F32).at[0, :4 * N_HEADS].set(
        jnp.concatenate([igate_b.reshape(-1), fgate_b.reshape(-1)]))

    def call(rev, extra_in, extra_args):
        blk = _scan_block(rev, n_chunks, n_lat_chunks)
        colblk = lambda c: pl.BlockSpec((n, HALF), lambda s, c=c: (blk(s), c))
        tok = pl.BlockSpec((n, 128), lambda s: (blk(s), 0))
        return pl.pallas_call(
            functools.partial(_mlstm_kernel, rev=rev),
            grid=(n_chunks,),
            in_specs=[colblk(0), colblk(1), colblk(2), tok,
                      pl.BlockSpec((128, n), lambda s: (0, blk(s))), tok, tok,
                      pl.BlockSpec((1, 128), lambda s: (0, 0)),
                      pl.BlockSpec((128, 1), lambda s: (0, 0))] + extra_in,
            out_specs=pl.BlockSpec((n, HALF), lambda s: (blk(s), 0)),
            out_shape=jax.ShapeDtypeStruct((t_all, HALF), F32),
            scratch_shapes=[pltpu.VMEM((N_HEADS, HEAD_DIM, HEAD_DIM), F32),
                            pltpu.VMEM((N_HEADS, HEAD_DIM), F32),
                            pltpu.VMEM((N_HEADS, HEAD_DIM), F32)],
            compiler_params=_cparams(1),
        )(proj, proj, proj, gates, gates_t, cos_t, sin_t, bias, bias.reshape(128, 1), *extra_args)

    h_fw = call(False, [], [])
    blk_r = _scan_block(True, n_chunks, n_lat_chunks)
    return call(True, [pl.BlockSpec((n, HALF), lambda s: (blk_r(s), 0)),
                       pl.BlockSpec((n, HALF), lambda s: (blk_r(s), 3))], [h_fw, proj])


def _na_kernel(q_ref, kp_ref, kc_ref, kn_ref, vp_ref, vc_ref, vn_ref, kx_ref, vx_ref, bias_ref, o_ref):
    qb = (q_ref[...] * QK_SCALE).astype(BF16)
    kk = jnp.concatenate([kp_ref[...], kc_ref[...], kn_ref[...]], axis=0).astype(BF16)
    vv = jnp.concatenate([vp_ref[...], vc_ref[...], vn_ref[...]], axis=0).astype(BF16)
    s_loc = _nt(qb, kk) + bias_ref[...]
    s_ctx = _nt(qb, kx_ref[...].astype(BF16))
    m = jnp.maximum(jnp.max(s_loc, axis=1, keepdims=True), jnp.max(s_ctx, axis=1, keepdims=True))
    p_loc = jnp.exp(s_loc - m)
    p_ctx = jnp.exp(s_ctx - m)
    l = jnp.sum(p_loc, axis=1, keepdims=True) + jnp.sum(p_ctx, axis=1, keepdims=True)
    o_ref[...] = (_nn(p_loc.astype(BF16), vv) + _nn(p_ctx.astype(BF16), vx_ref[...].astype(BF16))) / l


def _na_bias(rpb, rows):
    qn = NA_QROWS * GRID_W
    kn = 2 * qn
    ql = np.arange(qn)
    kl = np.arange(kn)
    out = []
    nqb = rows // NA_QROWS
    for qb in (0, 1, nqb - 1):
        qr = NA_QROWS * qb + ql // GRID_W
        qc = ql % GRID_W
        kr = NA_QROWS * qb - NA_QROWS // 2 + kl // GRID_W
        kc = kl % GRID_W
        r0 = np.clip(qr - NA_WIN_ROWS // 2, 0, rows - NA_WIN_ROWS)
        c0 = np.clip(qc - NA_WIN_COLS // 2, 0, GRID_W - NA_WIN_COLS)
        ok = ((kr[None, :] >= r0[:, None]) & (kr[None, :] < r0[:, None] + NA_WIN_ROWS)
              & (kc[None, :] >= c0[:, None]) & (kc[None, :] < c0[:, None] + NA_WIN_COLS)
              & (kr[None, :] >= 0) & (kr[None, :] < rows))
        dr = np.clip(kr[None, :] - qr[:, None] + NA_WIN_ROWS - 1, 0, 2 * NA_WIN_ROWS - 2)
        dc = np.clip(kc[None, :] - qc[:, None] + NA_WIN_COLS - 1, 0, 2 * NA_WIN_COLS - 2)
        out.append(jnp.where(jnp.asarray(ok)[None], rpb[:, dr, dc], NEG))
    return jnp.stack(out)


def _na(proj, rpb, t_lat, t_ctx):
    rows = t_lat // GRID_W
    qn = NA_QROWS * GRID_W
    hn = qn // 2
    nqb = rows // NA_QROWS
    assert nqb >= 2 and t_ctx == hn
    bias = _na_bias(rpb.astype(F32), rows)
    last_half = 2 * nqb - 1
    ctx_blk = t_lat // hn
    cls = lambda b: jnp.where(b == 0, 0, jnp.where(b == nqb - 1, 2, 1))

    def kv_specs(col0):
        return [pl.BlockSpec((hn, HEAD_DIM), lambda h, b: (jnp.maximum(2 * b - 1, 0), col0 + h)),
                pl.BlockSpec((qn, HEAD_DIM), lambda h, b: (b, col0 + h)),
                pl.BlockSpec((hn, HEAD_DIM), lambda h, b: (jnp.minimum(2 * b + 2, last_half), col0 + h))]

    kcol, vcol = 5 * N_HEADS, 6 * N_HEADS
    return pl.pallas_call(
        _na_kernel,
        grid=(N_HEADS, nqb),
        in_specs=[pl.BlockSpec((qn, HEAD_DIM), lambda h, b: (b, 4 * N_HEADS + h))]
        + kv_specs(kcol) + kv_specs(vcol)
        + [pl.BlockSpec((hn, HEAD_DIM), lambda h, b: (ctx_blk, kcol + h)),
           pl.BlockSpec((hn, HEAD_DIM), lambda h, b: (ctx_blk, vcol + h)),
           pl.BlockSpec((None, None, qn, 2 * qn), lambda h, b: (cls(b), h, 0, 0))],
        out_specs=pl.BlockSpec((qn, HEAD_DIM), lambda h, b: (b, h)),
        out_shape=jax.ShapeDtypeStruct((t_lat, HALF), F32),
        compiler_params=_cparams(2),
    )(proj, proj, proj, proj, proj, proj, proj, proj, proj, bias)


def _ctx_attn_kernel(q_ref, k_ref, v_ref, o_ref):
    qb = (q_ref[...] * QK_SCALE).astype(BF16)
    s = _nt(qb, k_ref[...].astype(BF16))
    p = jnp.exp(s - jnp.max(s, axis=1, keepdims=True))
    o_ref[...] = _nn(p.astype(BF16), v_ref[...].astype(BF16)) / jnp.sum(p, axis=1, keepdims=True)


def _ctx_attn(proj, t_lat, t_ctx):
    blk = t_lat // t_ctx
    spec = lambda c: pl.BlockSpec((t_ctx, HEAD_DIM), lambda h, c=c: (blk, c * N_HEADS + h))
    return pl.pallas_call(
        _ctx_attn_kernel,
        grid=(N_HEADS,),
        in_specs=[spec(4), spec(5), spec(6)],
        out_specs=pl.BlockSpec((t_ctx, HEAD_DIM), lambda h: (0, h)),
        out_shape=jax.ShapeDtypeStruct((t_ctx, HALF), F32),
        compiler_params=_cparams(1),
    )(proj, proj, proj)


def _gla_kernel(*refs, rev):
    if rev:
        q_ref, i_ref, f_ref, lb_ref, of_ref, g_ref, out_ref, st_ref, o_scr, k_scr, b_scr = refs
    else:
        q_ref, i_ref, f_ref, lb_ref, out_ref, st_ref, o_scr, k_scr, b_scr = refs
    n = q_ref.shape[0]

    @pl.when(pl.program_id(0) == 0)
    def _():
        st_ref[...] = jnp.zeros_like(st_ref)

    mask = _scan_mask(rev, n)
    tri = mask.astype(BF16)
    last = 0 if rev else n - 1
    x = f_ref[...]
    key = (1.0 - lb_ref[...]) * _sigmoid(-x)
    b = sum(_nn(tri, p) for p in _split3(jnp.log1p(-key)))
    b_end = b[last:last + 1, :]
    qx = q_ref[...]
    q = qx * _sigmoid(qx)
    qt = q * jnp.exp(b)
    kdec = key * jnp.exp(b_end - b)
    w_end = jnp.exp(b_end)
    k_scr[...] = key
    b_scr[...] = b

    for h in range(N_HEADS):
        hs = slice(h * HEAD_DIM, (h + 1) * HEAD_DIM)
        st = st_ref[h]
        vb = i_ref[:, hs].astype(BF16)
        o_scr[:, hs] = _nt(qt[:, hs].astype(BF16), st.astype(BF16))
        st_ref[h] = st * w_end[:, hs] + _tn(vb, kdec[:, hs].astype(BF16))

    safe = jnp.min(b_end) >= GLA_SAFE_LOG

    @pl.when(safe)
    def _():
        kt = k_scr[...] * jnp.exp(-b_scr[...])
        for h in range(N_HEADS):
            hs = slice(h * HEAD_DIM, (h + 1) * HEAD_DIM)
            att = jnp.where(mask, _nt(qt[:, hs].astype(BF16), kt[:, hs].astype(BF16)), 0.0)
            o_scr[:, hs] += _nn(att.astype(BF16), i_ref[:, hs].astype(BF16))

    @pl.when(jnp.logical_not(safe))
    def _():
        t_idx = lax.broadcasted_iota(jnp.int32, (n, 1), 0)
        for h in range(N_HEADS):
            hs = slice(h * HEAD_DIM, (h + 1) * HEAD_DIM)
            q_h = q[:, hs]
            b_h = b[:, hs]

            def body(grp, acc):
                base = pl.multiple_of(grp * 8, 8)
                b8, k8, v8 = b_scr[pl.ds(base, 8), hs], k_scr[pl.ds(base, 8), hs], i_ref[pl.ds(base, 8), hs]
                for r in range(8):
                    s = base + r
                    e = jnp.exp(jnp.minimum(b_h - b8[r:r + 1, :], 0.0))
                    a = jnp.sum(q_h * k8[r:r + 1, :] * e, axis=1, keepdims=True)
                    a = jnp.where((t_idx <= s) if rev else (t_idx >= s), a, 0.0)
                    acc = acc + a * v8[r:r + 1, :]
                return acc

            o_scr[:, hs] += lax.fori_loop(0, n // 8, body, jnp.zeros((n, HEAD_DIM), F32))

    if rev:
        for h in range(N_HEADS):
            hs = slice(h * HEAD_DIM, (h + 1) * HEAD_DIM)
            gx = g_ref[:, hs]
            out_ref[:, hs] = _head_norm(of_ref[:, hs] + o_scr[:, hs]) * (gx * _sigmoid(gx))
    else:
        out_ref[...] = o_scr[...]


def _gla(proj, lb, n_lat_chunks):
    t_all = proj.shape[0]
    n = SCAN_CHUNK
    n_chunks = t_all // n

    def call(rev, extra_in, extra_args):
        blk = _scan_block(rev, n_chunks, n_lat_chunks)
        colblk = lambda c: pl.BlockSpec((n, HALF), lambda s, c=c: (blk(s), c))
        return pl.pallas_call(
            functools.partial(_gla_kernel, rev=rev),
            grid=(n_chunks,),
            in_specs=[colblk(0), colblk(1), colblk(3 if rev else 2),
                      pl.BlockSpec((1, HALF), lambda s: (0, 0))] + extra_in,
            out_specs=pl.BlockSpec((n, HALF), lambda s: (blk(s), 0)),
            out_shape=jax.ShapeDtypeStruct((t_all, HALF), F32),
            scratch_shapes=[pltpu.VMEM((N_HEADS, HEAD_DIM, HEAD_DIM), F32),
                            pltpu.VMEM((n, HALF), F32), pltpu.VMEM((n, HALF), F32),
                            pltpu.VMEM((n, HALF), F32)],
            compiler_params=_cparams(1),
        )(proj, proj, proj, lb.reshape(1, HALF), *extra_args)

    o_fw = call(False, [], [])
    blk_r = _scan_block(True, n_chunks, n_lat_chunks)
    return call(True, [pl.BlockSpec((n, HALF), lambda s: (blk_r(s), 0)),
                       pl.BlockSpec((n, HALF), lambda s: (blk_r(s), 4))], [o_fw, proj])


def _s5_mats(a_re, a_im, log_dt, b_re, b_im, c_re, c_im):
    hp = lax.Precision.HIGHEST
    n = S5_CHUNK
    dt = jnp.exp(log_dt)[..., None]
    la_re, la_im = a_re * dt, a_im * dt
    mag = jnp.exp(la_re)
    ab_re, ab_im = mag * jnp.cos(la_im), mag * jnp.sin(la_im)
    nr, ni = ab_re - 1.0, ab_im
    den = jnp.square(a_re) + jnp.square(a_im)
    cr = (nr * a_re + ni * a_im) / den
    ci = (ni * a_re - nr * a_im) / den
    bb_re = cr[..., None] * b_re - ci[..., None] * b_im
    bb_im = cr[..., None] * b_im + ci[..., None] * b_re

    def apow(tau):
        tau = jnp.asarray(tau, F32)[None, None, :, None]
        m = jnp.exp(tau * la_re[:, :, None, :])
        return m * jnp.cos(tau * la_im[:, :, None, :]), m * jnp.sin(tau * la_im[:, :, None, :])

    def c_apow(tau):
        pr, pi = apow(tau)
        return (c_re[:, :, None] * pr[:, :, :, None] - c_im[:, :, None] * pi[:, :, :, None],
                c_re[:, :, None] * pi[:, :, :, None] + c_im[:, :, None] * pr[:, :, :, None])

    ca_re, ca_im = c_apow(np.arange(n))
    kern = (jnp.einsum('dgtip,dgpj->dgtij', ca_re, bb_re, precision=hp)
            - jnp.einsum('dgtip,dgpj->dgtij', ca_im, bb_im, precision=hp))
    s_idx, t_idx = np.arange(n)[:, None], np.arange(n)[None, :]
    g = a_re.shape[1]
    lp = n * S5_GROUP

    def toeplitz(kd, lag, ok):
        m = jnp.where(jnp.asarray(ok)[None, :, :, None, None], kd[:, np.clip(lag, 0, n - 1)], 0.0)
        return m.transpose(0, 1, 4, 2, 3).reshape(g, lp, lp)

    kmat = toeplitz(kern[0], t_idx - s_idx, t_idx >= s_idx) + toeplitz(kern[1], s_idx - t_idx, s_idx >= t_idx)

    def emat(d, tau):
        pr, pi = apow(tau)
        pr, pi = pr[d][:, :, None, :], pi[d][:, :, None, :]
        br, bi = bb_re[d].transpose(0, 2, 1)[:, None], bb_im[d].transpose(0, 2, 1)[:, None]
        return jnp.concatenate([pr * br - pi * bi, pr * bi + pi * br], axis=-1).reshape(g, lp, 2 * S5_STATE)

    def fmat(d, tau):
        fr, fi = c_apow(tau)
        fr, fi = fr[d], fi[d]
        return jnp.concatenate([fr, -fi], axis=-1).transpose(0, 3, 1, 2).reshape(g, 2 * S5_STATE, lp)

    e_all = jnp.concatenate([emat(0, n - 1 - np.arange(n)), emat(1, np.arange(n))], axis=-1)
    f_all = jnp.concatenate([fmat(0, np.arange(n) + 1), fmat(1, n - np.arange(n))], axis=1)
    pr, pi = apow(np.array([n]))
    pr, pi = pr[:, :, 0], pi[:, :, 0]
    ar2 = jnp.concatenate([pr[0], pr[0], pr[1], pr[1]], axis=-1)
    ai2 = jnp.concatenate([-pi[0], pi[0], -pi[1], pi[1]], axis=-1)
    return kmat.astype(BF16), e_all.astype(BF16), f_all.astype(BF16), ar2, ai2


def _s5_state_kernel(u_ref, e_ref, s_ref):
    s_ref[...] = _nn(u_ref[...], e_ref[...])


def _s5_scan_kernel(sf_ref, sb_ref, ar_ref, ai_ref, hf_ref, hb_ref, stf_ref, stb_ref):
    nb = sf_ref.shape[0]
    w = 2 * S5_STATE
    ar, ai = ar_ref[...], ai_ref[...]

    @pl.when(pl.program_id(0) == 0)
    def _():
        stf_ref[...] = jnp.zeros_like(stf_ref)
        stb_ref[...] = jnp.zeros_like(stb_ref)

    def step(h, s, lo):
        return ar[:, lo:lo + w] * h + ai[:, lo:lo + w] * pltpu.roll(h, S5_STATE, 1) + s

    hf, hb = stf_ref[...], stb_ref[...]
    for j in range(nb):
        hf_ref[j] = hf
        hf = step(hf, sf_ref[j], 0)
        hb_ref[nb - 1 - j] = hb
        hb = step(hb, sb_ref[nb - 1 - j], w)
    stf_ref[...] = hf
    stb_ref[...] = hb


def _s5_out_kernel(u_ref, k_ref, h_ref, f_ref, y_ref):
    y_ref[...] = _nn(u_ref[...], k_ref[...]) + _nn(h_ref[...].astype(BF16), f_ref[...])


def _s5(u, mats, n_lat_chunks):
    kmat, emat, fmat, ar2, ai2 = mats
    t_all = u.shape[0]
    n = S5_CHUNK
    nc = t_all // n
    g = S5_GROUPS
    lp = n * S5_GROUP
    w4 = 4 * S5_STATE
    ug = u.astype(BF16).reshape(nc, n, g, S5_GROUP).transpose(2, 0, 1, 3).reshape(g, nc, lp)
    grp = lambda a, b: pl.BlockSpec((None, a, b), lambda i: (i, 0, 0))
    s_end = pl.pallas_call(
        _s5_state_kernel, grid=(g,),
        in_specs=[grp(nc, lp), grp(lp, w4)], out_specs=grp(nc, w4),
        out_shape=jax.ShapeDtypeStruct((g, nc, w4), F32), compiler_params=_cparams(1),
    )(ug, emat)
    nb = S5_SCAN_BLOCK
    w2 = 2 * S5_STATE
    fw_blk = _scan_block(False, nc // nb, n_lat_chunks // nb)
    bw_blk = _scan_block(True, nc // nb, n_lat_chunks // nb)
    s_t = s_end.transpose(1, 0, 2)
    coef = pl.BlockSpec((g, w4), lambda i: (0, 0))
    h_fw, h_bw = pl.pallas_call(
        _s5_scan_kernel, grid=(nc // nb,),
        in_specs=[pl.BlockSpec((nb, g, w2), lambda i: (fw_blk(i), 0, 0)),
                  pl.BlockSpec((nb, g, w2), lambda i: (bw_blk(i), 0, 1)), coef, coef],
        out_specs=[pl.BlockSpec((nb, g, w2), lambda i: (fw_blk(i), 0, 0)),
                   pl.BlockSpec((nb, g, w2), lambda i: (bw_blk(i), 0, 0))],
        out_shape=[jax.ShapeDtypeStruct((nc, g, w2), F32)] * 2,
        scratch_shapes=[pltpu.VMEM((g, w2), F32)] * 2,
        compiler_params=_cparams(1),
    )(s_t, s_t, ar2, ai2)
    h_in = jnp.concatenate([h_fw, h_bw], axis=-1).transpose(1, 0, 2)
    y = pl.pallas_call(
        _s5_out_kernel, grid=(g,),
        in_specs=[grp(nc, lp), grp(lp, lp), grp(nc, w4), grp(w4, lp)], out_specs=grp(nc, lp),
        out_shape=jax.ShapeDtypeStruct((g, nc, lp), F32), compiler_params=_cparams(1),
    )(ug, kmat, h_in, fmat)
    return y.reshape(g, nc, n, S5_GROUP).transpose(1, 2, 0, 3).reshape(t_all, HALF)


def _top_rows(x, k):
    rows = []
    for _ in range(k):
        mx = jnp.max(x, axis=0, keepdims=True)
        rows.append(mx)
        x = jnp.where(x == mx, NEG, x)
    return rows


def _peer_route_kernel(h_ref, wq_ref, keys_ref, s1_ref, e1_ref, d0_ref, e0_ref, top_scr):
    kk = PEER_TOPK
    qt = _nt(wq_ref[...], h_ref[...])
    for hp in range(2 * PEER_HEADS):
        sc = _nn(keys_ref[hp], qt[hp * HEAD_DIM:(hp + 1) * HEAD_DIM, :].astype(BF16))
        h, p = divmod(hp, 2)
        if p == 0:
            d0_ref[h] = sc
        else:
            s1_ref[h] = sc
        for r, mx in enumerate(_top_rows(sc, kk + 1)):
            top_scr[hp, r:r + 1, :] = mx
        top_scr[hp, kk + 1:, :] = jnp.full((PEER_TOP_ROWS - kk - 1, sc.shape[1]), NEG, F32)

    row = lax.broadcasted_iota(jnp.int32, (PEER_TOP_ROWS, 1), 0)
    for h in range(PEER_HEADS):
        a = top_scr[2 * h]
        b = top_scr[2 * h + 1]
        parts = [a[0:1] + b, a[1:2] + b[0:8], a[2:3] + b[0:8], a[3:4] + b[0:8]]
        parts += [jnp.where(row >= 4, a + b[j:j + 1], NEG) for j in range(3)]
        cand = jnp.maximum(jnp.concatenate(parts, axis=0), NEG)
        top = _top_rows(cand, kk + 1)
        thr = 0.5 * (top[kk - 1] + top[kk])
        z = jnp.sum(jnp.where(cand >= thr, jnp.exp(cand - top[0]), 0.0), axis=0, keepdims=True)
        s0 = d0_ref[h]
        e0_ref[h] = jnp.exp(s0 - a[0:1]) / z
        d0_ref[h] = thr - s0
        e1_ref[h] = jnp.exp(s1_ref[h] - b[0:1])


def _peer_expert_kernel(h_ref, s1_ref, e1_ref, d0_ref, e0_ref, u_ref, v_ref, o_ref, g_scr):
    e = pl.program_id(1)
    ni = u_ref.shape[0] // PEER_N_KEYS

    @pl.when(e == 0)
    def _():
        o_ref[...] = jnp.zeros_like(o_ref)

    at = _nt(u_ref[...], h_ref[...])
    for ii in range(ni):
        i = e * ni + ii
        w = None
        for h in range(PEER_HEADS):
            t = jnp.where(s1_ref[h] >= d0_ref[h, pl.ds(i, 1), :], e1_ref[h] * e0_ref[h, pl.ds(i, 1), :], 0.0)
            w = t if w is None else w + t
        rs = slice(ii * PEER_N_KEYS, (ii + 1) * PEER_N_KEYS)
        g_scr[rs, :] = (_gelu_tanh(at[rs, :]) * w).astype(BF16)
    o_ref[...] += _tn(g_scr[...], v_ref[...])


def _tok3(t):
    return pl.BlockSpec((PEER_HEADS, PEER_N_KEYS, t), lambda i, *_: (0, 0, i))


def _peer_route(h2, wq_t_bf, keys_bf):
    t_all, d = h2.shape
    nh, nk = PEER_HEADS, PEER_N_KEYS
    t1 = PEER_P1_TILE
    tok3 = _tok3
    route_shape = jax.ShapeDtypeStruct((nh, nk, t_all), F32)
    return pl.pallas_call(
        _peer_route_kernel,
        grid=(t_all // t1,),
        in_specs=[pl.BlockSpec((t1, d), lambda i: (i, 0)),
                  pl.BlockSpec(wq_t_bf.shape, lambda i: (0, 0)),
                  pl.BlockSpec(keys_bf.shape, lambda i: (0, 0, 0))],
        out_specs=[tok3(t1)] * 4,
        out_shape=[route_shape] * 4,
        scratch_shapes=[pltpu.VMEM((2 * nh, PEER_TOP_ROWS, t1), F32)],
        compiler_params=_cparams(1),
    )(h2, wq_t_bf, keys_bf)


def _peer(h2, wq_t_bf, keys_bf, u_bf, v_bf):
    t_all, d = h2.shape
    tok3 = _tok3
    s1, e1, d0, e0 = _peer_route(h2, wq_t_bf, keys_bf)
    tt = PEER_TOK_TILE if t_all % PEER_TOK_TILE == 0 else PEER_P1_TILE
    et = PEER_EXP_TILE
    n_exp = u_bf.shape[0]
    return pl.pallas_call(
        _peer_expert_kernel,
        grid=(t_all // tt, n_exp // et),
        in_specs=[pl.BlockSpec((tt, d), lambda i, e: (i, 0))] + [tok3(tt)] * 4
        + [pl.BlockSpec((et, d), lambda i, e: (e, 0)), pl.BlockSpec((et, d), lambda i, e: (e, 0))],
        out_specs=pl.BlockSpec((tt, d), lambda i, e: (i, 0)),
        out_shape=jax.ShapeDtypeStruct((t_all, d), F32),
        scratch_shapes=[pltpu.VMEM((et, tt), BF16)],
        compiler_params=_cparams(2),
    )(h2, s1, e1, d0, e0, u_bf, v_bf)


def _rope_tables(t_lat, t_ctx):
    pos = np.arange(t_lat)
    nfreq = HEAD_DIM // 4
    freqs = ROPE_BASE ** (-jnp.arange(nfreq, dtype=F32) / nfreq)
    ang_r = jnp.asarray(pos // GRID_W, F32)[:, None] * freqs
    ang_c = jnp.asarray(pos % GRID_W, F32)[:, None] * freqs
    cos = jnp.concatenate([jnp.cos(ang_r), jnp.cos(ang_r), jnp.cos(ang_c), jnp.cos(ang_c)], axis=1)
    sin = jnp.concatenate([-jnp.sin(ang_r), jnp.sin(ang_r), -jnp.sin(ang_c), jnp.sin(ang_c)], axis=1)
    cos = jnp.concatenate([cos, jnp.ones((t_ctx, HEAD_DIM), F32)], axis=0)
    sin = jnp.concatenate([sin, jnp.zeros((t_ctx, HEAD_DIM), F32)], axis=0)
    return cos, sin


def _peer_and_norm(x1, h2, mods, ln_g, ln_b, n_lat_tiles, w_q, sub_keys, u_tab, v_tab):
    keys = sub_keys.reshape(2 * PEER_HEADS, PEER_N_KEYS, -1).astype(BF16)
    p = _peer(h2, w_q.T.astype(BF16), keys, u_tab.astype(BF16), v_tab.astype(BF16))
    return _ln2(x1, p, mods, ln_g, ln_b, n_lat_tiles)


def _layer_ab(x_all, mods, t_lat, t_ctx, w_in, w_out, igate_b, fgate_b, rpb, ln_g, ln_b):
    n_lat_tiles = t_lat // ROW_TILE
    ng = 4 * N_HEADS
    w_main = jnp.concatenate([w_in[:, :4 * HALF], w_in[:, 4 * HALF + ng:]], axis=1).astype(BF16)
    w_gate = jnp.pad(w_in[:, 4 * HALF:4 * HALF + ng], ((0, 0), (0, 128 - ng))).astype(BF16)
    proj = _inproj(x_all, mods, w_main, n_lat_tiles, 512)
    gates = _inproj(x_all, mods, w_gate, n_lat_tiles, 128)
    cos_t, sin_t = _rope_tables(t_lat, t_ctx)
    y_m = _mlstm(proj, gates, gates.T, cos_t, sin_t, igate_b, fgate_b, t_lat // SCAN_CHUNK)
    y_na = jnp.concatenate([_na(proj, rpb, t_lat, t_ctx), _ctx_attn(proj, t_lat, t_ctx)], axis=0)
    return _outproj(y_m, [y_na], w_out.astype(BF16), x_all, mods, ln_g, ln_b, n_lat_tiles)


def _layer_cd(x_all, mods, t_lat, w_in, w_out, lb, s5p, s5_d, glu_w, glu_b, ln_g, ln_b):
    n_lat_tiles = t_lat // ROW_TILE
    proj = _inproj(x_all, mods, w_in.astype(BF16), n_lat_tiles, 512)
    y_h = _gla(proj, lb, t_lat // SCAN_CHUNK)
    u5 = proj[:, 5 * HALF:]
    y5 = _s5(u5, _s5_mats(*[p.astype(F32) for p in s5p]), t_lat // S5_CHUNK)
    return _outproj(y_h, [y5, u5], w_out.astype(BF16), x_all, mods, ln_g, ln_b, n_lat_tiles,
                    glu_params=(s5_d.astype(F32), glu_w.astype(BF16), glu_b.astype(F32)))


def kernel(x, c, ctx, c_ctx, ada_w, ada_b, ln_g, ln_b, ab_w_in, ab_w_out, mlstm_igate_b, mlstm_fgate_b, na_rpb, cd_w_in, cd_w_out, hgrn_lb_logits, s5_a_re, s5_a_im, s5_log_dt, s5_b_re, s5_b_im, s5_c_re, s5_c_im, s5_d, s5_glu_w, s5_glu_b, peer_w_q, peer_sub_keys, peer_u, peer_v):
    t_lat, t_ctx = x.shape[1], ctx.shape[1]
    assert x.shape[0] == 1 and t_ctx == ROW_TILE and t_lat % (NA_QROWS * GRID_W) == 0
    depth = ada_w.shape[0]
    lb_soft = jax.nn.softmax(hgrn_lb_logits.astype(F32), axis=0)
    lower_bounds = jnp.cumsum(lb_soft, axis=0) - lb_soft[0]
    x_all = jnp.concatenate([x[0], ctx[0]], axis=0).astype(F32)
    n_lat_tiles = t_lat // ROW_TILE
    for l in range(depth):
        j = l // 2
        mods = _adaln(c, c_ctx, ada_w[l], ada_b[l])
        if l % 2 == 0:
            x1, h2 = _layer_ab(x_all, mods, t_lat, t_ctx, ab_w_in[j], ab_w_out[j], mlstm_igate_b[j],
                               mlstm_fgate_b[j], na_rpb[j], ln_g[l, 0], ln_b[l, 0])
        else:
            s5p = (s5_a_re[j], s5_a_im[j], s5_log_dt[j], s5_b_re[j], s5_b_im[j], s5_c_re[j], s5_c_im[j])
            x1, h2 = _layer_cd(x_all, mods, t_lat, cd_w_in[j], cd_w_out[j], lower_bounds[l], s5p, s5_d[j],
                               s5_glu_w[j], s5_glu_b[j], ln_g[l, 0], ln_b[l, 0])
        x_all = _peer_and_norm(x1, h2, mods, ln_g[l, 1], ln_b[l, 1], n_lat_tiles, peer_w_q[l],
                               peer_sub_keys[l], peer_u[l], peer_v[l])
    return x_all[:t_lat][None].astype(x.dtype)
```

```python
import functools
import math

import numpy as np
import jax
import jax.numpy as jnp
from jax import lax
from jax.experimental import pallas as pl
from jax.experimental.pallas import tpu as pltpu

F32 = jnp.float32
BF16 = jnp.bfloat16

HEAD_DIM = 128
GRID_W = 64
N_HEADS = 8
HALF = N_HEADS * HEAD_DIM
NA_WIN_ROWS = 8
NA_WIN_COLS = 16
ROPE_BASE = 10000.0
S5_GROUP = 16
S5_STATE = 64
S5_GROUPS = HALF // S5_GROUP
PEER_HEADS = 8
PEER_N_KEYS = 128
PEER_TOPK = 16
PEER_TOP_ROWS = 24
LN_EPS = 1e-5
DEPTH = 2
ALPHA = (2.0 * DEPTH) ** 0.25
QK_SCALE = HEAD_DIM ** -0.5

ROW_TILE = 256
SCAN_CHUNK = 128
GLA_CHUNK = 64
S5_CHUNK = 32
S5_SCAN_BLOCK = 8
NA_QROWS = 8
PEER_P1_TILE = 256
PEER_TOK_TILE = 640
PEER_EXP_TILE = 512
NEG = -1e30
GLA_SAFE_LOG = -80.0
VMEM_LIMIT = 56 * 1024 * 1024


def _cparams(n_axes):
    return pltpu.CompilerParams(dimension_semantics=("arbitrary",) * n_axes,
                                vmem_limit_bytes=VMEM_LIMIT)


def _nt(a, b):
    return lax.dot_general(a, b, (((1,), (1,)), ((), ())), preferred_element_type=F32)


def _tn(a, b):
    return lax.dot_general(a, b, (((0,), (0,)), ((), ())), preferred_element_type=F32)


def _nn(a, b):
    return jnp.dot(a, b, preferred_element_type=F32)


def _split3(x):
    hi = x.astype(BF16)
    r = x - hi.astype(F32)
    mid = r.astype(BF16)
    lo = (r - mid.astype(F32)).astype(BF16)
    return hi, mid, lo


def _log_sigmoid(x):
    return jnp.minimum(x, 0.0) - jnp.log1p(jnp.exp(-jnp.abs(x)))


def _sigmoid(x):
    return 1.0 / (1.0 + jnp.exp(-x))


def _gelu_tanh(x):
    return 0.5 * x * (1.0 + jnp.tanh(math.sqrt(2.0 / math.pi) * (x + 0.044715 * (x * x * x))))


def _head_norm(x):
    mu = jnp.mean(x, axis=-1, keepdims=True)
    xc = x - mu
    return xc * lax.rsqrt(jnp.mean(xc * xc, axis=-1, keepdims=True) + LN_EPS)


def _adaln_kernel(ct_ref, w_ref, b_ref, o_ref):
    ct = ct_ref[...]
    s = ct * _sigmoid(ct)
    w = w_ref[...]
    o_ref[0:1, :] = jnp.sum(s[:, 0:1] * w, axis=0, keepdims=True) + b_ref[...]
    o_ref[1:2, :] = jnp.sum(s[:, 1:2] * w, axis=0, keepdims=True) + b_ref[...]


def _adaln(c_lat, c_ctx, w, b):
    d = w.shape[0]
    n = w.shape[1]
    tn = n // 8
    ct = jnp.stack([c_lat.reshape(d), c_ctx.reshape(d)], axis=1)
    out = pl.pallas_call(
        _adaln_kernel,
        grid=(n // tn,),
        in_specs=[pl.BlockSpec((d, 2), lambda j: (0, 0)),
                  pl.BlockSpec((d, tn), lambda j: (0, j)),
                  pl.BlockSpec((1, tn), lambda j: (0, j))],
        out_specs=pl.BlockSpec((2, tn), lambda j: (0, j)),
        out_shape=jax.ShapeDtypeStruct((2, n), F32),
        compiler_params=_cparams(1),
    )(ct, w, b.reshape(1, n))
    return jnp.pad(out.reshape(2, 6, d), ((0, 0), (0, 2), (0, 0)))


def _inproj_kernel(x_ref, m_ref, w_ref, o_ref, xs_ref):
    @pl.when(pl.program_id(1) == 0)
    def _():
        xs_ref[...] = (x_ref[...] * (1.0 + m_ref[1:2, :]) + m_ref[0:1, :]).astype(BF16)

    o_ref[...] = _nn(xs_ref[...], w_ref[...])


def _inproj(x_all, mods, w_bf, n_lat_tiles, tn):
    t_all, d = x_all.shape
    n = w_bf.shape[1]
    return pl.pallas_call(
        _inproj_kernel,
        grid=(t_all // ROW_TILE, n // tn),
        in_specs=[pl.BlockSpec((ROW_TILE, d), lambda i, j: (i, 0)),
                  pl.BlockSpec((None, 8, d), lambda i, j: (jnp.where(i < n_lat_tiles, 0, 1), 0, 0)),
                  pl.BlockSpec((d, tn), lambda i, j: (0, j))],
        out_specs=pl.BlockSpec((ROW_TILE, tn), lambda i, j: (i, j)),
        out_shape=jax.ShapeDtypeStruct((t_all, n), F32),
        scratch_shapes=[pltpu.VMEM((ROW_TILE, d), BF16)],
        compiler_params=_cparams(2),
    )(x_all, mods, w_bf)


def _ln_rows(z, g, b):
    mu = jnp.mean(z, axis=-1, keepdims=True)
    zc = z - mu
    return zc * lax.rsqrt(jnp.mean(zc * zc, axis=-1, keepdims=True) + LN_EPS) * g + b


def _outproj_kernel(*refs, glu):
    if glu:
        (a1_ref, y5_ref, u5_ref, ds_ref, gw_ref, gb_ref, w1_ref, w2_ref, x_ref, m_ref, g_ref, b_ref,
         x1_ref, h2_ref) = refs
        y = _gelu_tanh(y5_ref[...] + ds_ref[...] * u5_ref[...])
        a2 = y * _sigmoid(_nn(y.astype(BF16), gw_ref[...]) + gb_ref[...])
    else:
        a1_ref, a2_ref, w1_ref, w2_ref, x_ref, m_ref, g_ref, b_ref, x1_ref, h2_ref = refs
        a2 = a2_ref[...]
    y = _nn(a1_ref[...].astype(BF16), w1_ref[...]) + _nn(a2.astype(BF16), w2_ref[...])
    x1 = _ln_rows(ALPHA * x_ref[...] + m_ref[2:3, :] * y, g_ref[...], b_ref[...])
    x1_ref[...] = x1
    h2_ref[...] = (x1 * (1.0 + m_ref[4:5, :]) + m_ref[3:4, :]).astype(BF16)


def _outproj(a1, a2_parts, w_out_bf, x_all, mods, ln_g, ln_b, n_lat_tiles, glu_params=None):
    t_all, d = x_all.shape
    half = a1.shape[1]
    row = lambda i: (i, 0)
    const = lambda i: (0, 0)
    act = pl.BlockSpec((ROW_TILE, half), row)
    if glu_params is None:
        a2_specs, a2_args = [act], list(a2_parts)
    else:
        d_skip, glu_w_bf, glu_b = glu_params
        a2_specs = [act, act, pl.BlockSpec((1, half), const), pl.BlockSpec((half, half), const),
                    pl.BlockSpec((1, half), const)]
        a2_args = list(a2_parts) + [d_skip.reshape(1, half), glu_w_bf, glu_b.reshape(1, half)]
    return pl.pallas_call(
        functools.partial(_outproj_kernel, glu=glu_params is not None),
        grid=(t_all // ROW_TILE,),
        in_specs=[act] + a2_specs + [
            pl.BlockSpec((half, d), const), pl.BlockSpec((half, d), const),
            pl.BlockSpec((ROW_TILE, d), row),
            pl.BlockSpec((None, 8, d), lambda i: (jnp.where(i < n_lat_tiles, 0, 1), 0, 0)),
            pl.BlockSpec((1, d), const), pl.BlockSpec((1, d), const)],
        out_specs=[pl.BlockSpec((ROW_TILE, d), row), pl.BlockSpec((ROW_TILE, d), row)],
        out_shape=[jax.ShapeDtypeStruct((t_all, d), F32), jax.ShapeDtypeStruct((t_all, d), BF16)],
        compiler_params=_cparams(1),
    )(a1, *a2_args, w_out_bf[:half], w_out_bf[half:], x_all, mods, ln_g.reshape(1, d), ln_b.reshape(1, d))


def _ln2_kernel(x_ref, p_ref, m_ref, g_ref, b_ref, o_ref):
    o_ref[...] = _ln_rows(ALPHA * x_ref[...] + m_ref[5:6, :] * p_ref[...], g_ref[...], b_ref[...])


def _ln2(x1, peer_out, mods, ln_g, ln_b, n_lat_tiles):
    t_all, d = x1.shape
    row = lambda i: (i, 0)
    const = lambda i: (0, 0)
    return pl.pallas_call(
        _ln2_kernel,
        grid=(t_all // ROW_TILE,),
        in_specs=[pl.BlockSpec((ROW_TILE, d), row), pl.BlockSpec((ROW_TILE, d), row),
                  pl.BlockSpec((None, 8, d), lambda i: (jnp.where(i < n_lat_tiles, 0, 1), 0, 0)),
                  pl.BlockSpec((1, d), const), pl.BlockSpec((1, d), const)],
        out_specs=pl.BlockSpec((ROW_TILE, d), row),
        out_shape=jax.ShapeDtypeStruct((t_all, d), F32),
        compiler_params=_cparams(1),
    )(x1, peer_out, mods, ln_g.reshape(1, d), ln_b.reshape(1, d))


def _scan_block(rev, n_chunks, n_lat_chunks):
    if rev:
        return lambda s: n_chunks - 1 - s
    n_ctx = n_chunks - n_lat_chunks
    return lambda s: jnp.where(s < n_ctx, s + n_lat_chunks, s - n_ctx)


def _scan_mask(rev, n):
    row = lax.broadcasted_iota(jnp.int32, (n, n), 0)
    col = lax.broadcasted_iota(jnp.int32, (n, n), 1)
    return (col >= row) if rev else (col <= row)


def _rope(x, cos, sin):
    lane = lax.broadcasted_iota(jnp.int32, x.shape, 1)
    partner = jnp.where((lane % 64) < 32, pltpu.roll(x, 96, 1), pltpu.roll(x, 32, 1))
    return x * cos + partner * sin


def _mlstm_kernel(*refs, rev):
    if rev:
        (q_ref, k_ref, v_ref, g_ref, gt_ref, cos_ref, sin_ref, bias_ref, biast_ref, hf_ref, o_ref,
         out_ref, c_ref, n_ref, m_ref) = refs
    else:
        (q_ref, k_ref, v_ref, g_ref, gt_ref, cos_ref, sin_ref, bias_ref, biast_ref,
         out_ref, c_ref, n_ref, m_ref) = refs
    n = q_ref.shape[0]

    @pl.when(pl.program_id(0) == 0)
    def _():
        c_ref[...] = jnp.zeros_like(c_ref)
        n_ref[...] = jnp.zeros_like(n_ref)
        m_ref[...] = jnp.zeros_like(m_ref)

    mask = _scan_mask(rev, n)
    tri = mask.astype(BF16)
    pre_c = g_ref[...] + bias_ref[...]
    pre_r = gt_ref[...] + biast_ref[...]
    b_c = sum(_nn(tri, p) for p in _split3(_log_sigmoid(pre_c)))
    b_r = sum(_nt(p, tri) for p in _split3(_log_sigmoid(pre_r)))
    cos = cos_ref[...]
    sin = sin_ref[...]
    last = 0 if rev else n - 1
    off = N_HEADS if rev else 0

    for h in range(N_HEADS):
        hs = slice(h * HEAD_DIM, (h + 1) * HEAD_DIM)
        ci, cf = off + h, 2 * N_HEADS + off + h
        bc, br = b_c[:, cf:cf + 1], b_r[cf:cf + 1, :]
        ic, ir = pre_c[:, ci:ci + 1], pre_r[ci:ci + 1, :]
        m_prev = m_ref[h:h + 1, 0:1]
        logw = jnp.where(mask, bc - br + ir, NEG)
        inter = bc + m_prev
        m_t = jnp.maximum(inter, jnp.max(logw, axis=1, keepdims=True))
        w_inter = jnp.exp(inter - m_t)
        q = _rope(q_ref[:, hs], cos, sin) * QK_SCALE
        k = _rope(k_ref[:, hs], cos, sin)
        qb, kb, vb = q.astype(BF16), k.astype(BF16), v_ref[:, hs].astype(BF16)
        s = _nt(qb, kb) * jnp.exp(logw - m_t)
        c_old = c_ref[h]
        n_old = n_ref[h:h + 1, :]
        num = _nn(s.astype(BF16), vb) + _nn(qb, c_old.astype(BF16)) * w_inter
        den = jnp.sum(s, axis=1, keepdims=True) + jnp.sum(q * n_old, axis=1, keepdims=True) * w_inter
        den = jnp.maximum(jnp.abs(den), jnp.exp(-m_t))
        hout = num / den
        m_new = m_t[last:last + 1, :]
        b_last = bc[last:last + 1, :]
        w_old = jnp.exp(b_last + m_prev - m_new)
        kw = k * jnp.exp(b_last - bc + ic - m_new)
        c_ref[h] = w_old * c_old + _tn(kw.astype(BF16), vb)
        n_ref[h:h + 1, :] = w_old * n_old + jnp.sum(kw, axis=0, keepdims=True)
        m_ref[h:h + 1, :] = jnp.broadcast_to(m_new, (1, HEAD_DIM))
        if rev:
            out_ref[:, hs] = _head_norm(hf_ref[:, hs] + hout) * _sigmoid(o_ref[:, hs])
        else:
            out_ref[:, hs] = hout


def _mlstm(proj, gates, gates_t, cos_t, sin_t, igate_b, fgate_b, n_lat_chunks):
    t_all = proj.shape[0]
    n = SCAN_CHUNK
    n_chunks = t_all // n
    bias = jnp.zeros((1, 128), F32).at[0, :4 * N_HEADS].set(
        jnp.concatenate([igate_b.reshape(-1), fgate_b.reshape(-1)]))

    def call(rev, extra_in, extra_args):
        blk = _scan_block(rev, n_chunks, n_lat_chunks)
        colblk = lambda c: pl.BlockSpec((n, HALF), lambda s, c=c: (blk(s), c))
        tok = pl.BlockSpec((n, 128), lambda s: (blk(s), 0))
        return pl.pallas_call(
            functools.partial(_mlstm_kernel, rev=rev),
            grid=(n_chunks,),
            in_specs=[colblk(0), colblk(1), colblk(2), tok,
                      pl.BlockSpec((128, n), lambda s: (0, blk(s))), tok, tok,
                      pl.BlockSpec((1, 128), lambda s: (0, 0)),
                      pl.BlockSpec((128, 1), lambda s: (0, 0))] + extra_in,
            out_specs=pl.BlockSpec((n, HALF), lambda s: (blk(s), 0)),
            out_shape=jax.ShapeDtypeStruct((t_all, HALF), F32),
            scratch_shapes=[pltpu.VMEM((N_HEADS, HEAD_DIM, HEAD_DIM), F32),
                            pltpu.VMEM((N_HEADS, HEAD_DIM), F32),
                            pltpu.VMEM((N_HEADS, HEAD_DIM), F32)],
            compiler_params=_cparams(1),
        )(proj, proj, proj, gates, gates_t, cos_t, sin_t, bias, bias.reshape(128, 1), *extra_args)

    h_fw = call(False, [], [])
    blk_r = _scan_block(True, n_chunks, n_lat_chunks)
    return call(True, [pl.BlockSpec((n, HALF), lambda s: (blk_r(s), 0)),
                       pl.BlockSpec((n, HALF), lambda s: (blk_r(s), 3))], [h_fw, proj])


def _na_kernel(q_ref, kp_ref, kc_ref, kn_ref, vp_ref, vc_ref, vn_ref, kx_ref, vx_ref, bias_ref, o_ref):
    qb = (q_ref[...] * QK_SCALE).astype(BF16)
    kk = jnp.concatenate([kp_ref[...], kc_ref[...], kn_ref[...]], axis=0).astype(BF16)
    vv = jnp.concatenate([vp_ref[...], vc_ref[...], vn_ref[...]], axis=0).astype(BF16)
    s_loc = _nt(qb, kk) + bias_ref[...]
    s_ctx = _nt(qb, kx_ref[...].astype(BF16))
    m = jnp.maximum(jnp.max(s_loc, axis=1, keepdims=True), jnp.max(s_ctx, axis=1, keepdims=True))
    p_loc = jnp.exp(s_loc - m)
    p_ctx = jnp.exp(s_ctx - m)
    l = jnp.sum(p_loc, axis=1, keepdims=True) + jnp.sum(p_ctx, axis=1, keepdims=True)
    o_ref[...] = (_nn(p_loc.astype(BF16), vv) + _nn(p_ctx.astype(BF16), vx_ref[...].astype(BF16))) / l


def _na_bias(rpb, rows):
    qn = NA_QROWS * GRID_W
    kn = 2 * qn
    ql = np.arange(qn)
    kl = np.arange(kn)
    out = []
    nqb = rows // NA_QROWS
    lr, krl = np.arange(NA_QROWS)[:, None], np.arange(2 * NA_QROWS)[None, :]
    dr = np.clip(krl - NA_QROWS // 2 - lr + NA_WIN_ROWS - 1, 0, 2 * NA_WIN_ROWS - 2)
    col = np.arange(GRID_W)
    dc = np.clip(col[None, :] - col[:, None] + NA_WIN_COLS - 1, 0, 2 * NA_WIN_COLS - 2)
    oh_r = jnp.asarray(dr[..., None] == np.arange(2 * NA_WIN_ROWS - 1), F32)
    oh_c = jnp.asarray(dc[..., None] == np.arange(2 * NA_WIN_COLS - 1), F32)
    table = jnp.einsum('hab,lka,qcb->hlqkc', rpb, oh_r, oh_c,
                       precision=lax.Precision.HIGHEST).reshape(rpb.shape[0], qn, kn)
    for qb in (0, 1, nqb - 1):
        qr = NA_QROWS * qb + ql // GRID_W
        qc = ql % GRID_W
        kr = NA_QROWS * qb - NA_QROWS // 2 + kl // GRID_W
        kc = kl % GRID_W
        r0 = np.clip(qr - NA_WIN_ROWS // 2, 0, rows - NA_WIN_ROWS)
        c0 = np.clip(qc - NA_WIN_COLS // 2, 0, GRID_W - NA_WIN_COLS)
        ok = ((kr[None, :] >= r0[:, None]) & (kr[None, :] < r0[:, None] + NA_WIN_ROWS)
              & (kc[None, :] >= c0[:, None]) & (kc[None, :] < c0[:, None] + NA_WIN_COLS)
              & (kr[None, :] >= 0) & (kr[None, :] < rows))
        out.append(jnp.where(jnp.asarray(ok)[None], table, NEG))
    return jnp.stack(out)


def _na(proj, rpb, t_lat, t_ctx):
    rows = t_lat // GRID_W
    qn = NA_QROWS * GRID_W
    hn = qn // 2
    nqb = rows // NA_QROWS
    assert nqb >= 2 and t_ctx == hn
    bias = _na_bias(rpb.astype(F32), rows)
    last_half = 2 * nqb - 1
    ctx_blk = t_lat // hn
    cls = lambda b: jnp.where(b == 0, 0, jnp.where(b == nqb - 1, 2, 1))

    def kv_specs(col0):
        return [pl.BlockSpec((hn, HEAD_DIM), lambda h, b: (jnp.maximum(2 * b - 1, 0), col0 + h)),
                pl.BlockSpec((qn, HEAD_DIM), lambda h, b: (b, col0 + h)),
                pl.BlockSpec((hn, HEAD_DIM), lambda h, b: (jnp.minimum(2 * b + 2, last_half), col0 + h))]

    kcol, vcol = 5 * N_HEADS, 6 * N_HEADS
    return pl.pallas_call(
        _na_kernel,
        grid=(N_HEADS, nqb),
        in_specs=[pl.BlockSpec((qn, HEAD_DIM), lambda h, b: (b, 4 * N_HEADS + h))]
        + kv_specs(kcol) + kv_specs(vcol)
        + [pl.BlockSpec((hn, HEAD_DIM), lambda h, b: (ctx_blk, kcol + h)),
           pl.BlockSpec((hn, HEAD_DIM), lambda h, b: (ctx_blk, vcol + h)),
           pl.BlockSpec((None, None, qn, 2 * qn), lambda h, b: (cls(b), h, 0, 0))],
        out_specs=pl.BlockSpec((qn, HEAD_DIM), lambda h, b: (b, h)),
        out_shape=jax.ShapeDtypeStruct((t_lat, HALF), F32),
        compiler_params=_cparams(2),
    )(proj, proj, proj, proj, proj, proj, proj, proj, proj, bias)


def _ctx_attn_kernel(q_ref, k_ref, v_ref, o_ref):
    qb = (q_ref[...] * QK_SCALE).astype(BF16)
    s = _nt(qb, k_ref[...].astype(BF16))
    p = jnp.exp(s - jnp.max(s, axis=1, keepdims=True))
    o_ref[...] = _nn(p.astype(BF16), v_ref[...].astype(BF16)) / jnp.sum(p, axis=1, keepdims=True)


def _ctx_attn(proj, t_lat, t_ctx):
    blk = t_lat // t_ctx
    spec = lambda c: pl.BlockSpec((t_ctx, HEAD_DIM), lambda h, c=c: (blk, c * N_HEADS + h))
    return pl.pallas_call(
        _ctx_attn_kernel,
        grid=(N_HEADS,),
        in_specs=[spec(4), spec(5), spec(6)],
        out_specs=pl.BlockSpec((t_ctx, HEAD_DIM), lambda h: (0, h)),
        out_shape=jax.ShapeDtypeStruct((t_ctx, HALF), F32),
        compiler_params=_cparams(1),
    )(proj, proj, proj)


def _gla_kernel(*refs, rev):
    if rev:
        q_ref, i_ref, f_ref, lb_ref, of_ref, g_ref, out_ref, st_ref, o_scr, k_scr, b_scr = refs
    else:
        q_ref, i_ref, f_ref, lb_ref, out_ref, st_ref, o_scr, k_scr, b_scr = refs
    n = q_ref.shape[0]

    @pl.when(pl.program_id(0) == 0)
    def _():
        st_ref[...] = jnp.zeros_like(st_ref)

    mask = _scan_mask(rev, n)
    tri = mask.astype(BF16)
    last = 0 if rev else n - 1
    x = f_ref[...]
    key = (1.0 - lb_ref[...]) * _sigmoid(-x)
    b = sum(_nn(tri, p) for p in _split3(jnp.log1p(-key)))
    b_end = b[last:last + 1, :]
    qx = q_ref[...]
    q = qx * _sigmoid(qx)
    qt = q * jnp.exp(b)
    kdec = key * jnp.exp(b_end - b)
    w_end = jnp.exp(b_end)
    k_scr[...] = key
    b_scr[...] = b

    for h in range(N_HEADS):
        hs = slice(h * HEAD_DIM, (h + 1) * HEAD_DIM)
        st = st_ref[h]
        vb = i_ref[:, hs].astype(BF16)
        o_scr[:, hs] = _nt(qt[:, hs].astype(BF16), st.astype(BF16))
        st_ref[h] = st * w_end[:, hs] + _tn(vb, kdec[:, hs].astype(BF16))

    safe = jnp.min(b_end) >= 2.0 * GLA_SAFE_LOG

    @pl.when(safe)
    def _():
        mid = 0.5 * b_end
        qm = q * jnp.exp(b_scr[...] - mid)
        km = k_scr[...] * jnp.exp(mid - b_scr[...])
        for h in range(N_HEADS):
            hs = slice(h * HEAD_DIM, (h + 1) * HEAD_DIM)
            att = jnp.where(mask, _nt(qm[:, hs].astype(BF16), km[:, hs].astype(BF16)), 0.0)
            o_scr[:, hs] += _nn(att.astype(BF16), i_ref[:, hs].astype(BF16))

    @pl.when(jnp.logical_not(safe))
    def _():
        t_idx = lax.broadcasted_iota(jnp.int32, (n, 1), 0)
        for h in range(N_HEADS):
            hs = slice(h * HEAD_DIM, (h + 1) * HEAD_DIM)
            q_h = q[:, hs]
            b_h = b[:, hs]

            def body(grp, acc):
                base = pl.multiple_of(grp * 8, 8)
                b8, k8, v8 = b_scr[pl.ds(base, 8), hs], k_scr[pl.ds(base, 8), hs], i_ref[pl.ds(base, 8), hs]
                for r in range(8):
                    s = base + r
                    e = jnp.exp(jnp.minimum(b_h - b8[r:r + 1, :], 0.0))
                    a = jnp.sum(q_h * k8[r:r + 1, :] * e, axis=1, keepdims=True)
                    a = jnp.where((t_idx <= s) if rev else (t_idx >= s), a, 0.0)
                    acc = acc + a * v8[r:r + 1, :]
                return acc

            o_scr[:, hs] += lax.fori_loop(0, n // 8, body, jnp.zeros((n, HEAD_DIM), F32))

    if rev:
        for h in range(N_HEADS):
            hs = slice(h * HEAD_DIM, (h + 1) * HEAD_DIM)
            gx = g_ref[:, hs]
            out_ref[:, hs] = _head_norm(of_ref[:, hs] + o_scr[:, hs]) * (gx * _sigmoid(gx))
    else:
        out_ref[...] = o_scr[...]


def _gla(proj, lb, n_lat_chunks):
    t_all = proj.shape[0]
    n = GLA_CHUNK
    n_chunks = t_all // n

    def call(rev, extra_in, extra_args):
        blk = _scan_block(rev, n_chunks, n_lat_chunks)
        colblk = lambda c: pl.BlockSpec((n, HALF), lambda s, c=c: (blk(s), c))
        return pl.pallas_call(
            functools.partial(_gla_kernel, rev=rev),
            grid=(n_chunks,),
            in_specs=[colblk(0), colblk(1), colblk(3 if rev else 2),
                      pl.BlockSpec((1, HALF), lambda s: (0, 0))] + extra_in,
            out_specs=pl.BlockSpec((n, HALF), lambda s: (blk(s), 0)),
            out_shape=jax.ShapeDtypeStruct((t_all, HALF), F32),
            scratch_shapes=[pltpu.VMEM((N_HEADS, HEAD_DIM, HEAD_DIM), F32),
                            pltpu.VMEM((n, HALF), F32), pltpu.VMEM((n, HALF), F32),
                            pltpu.VMEM((n, HALF), F32)],
            compiler_params=_cparams(1),
        )(proj, proj, proj, lb.reshape(1, HALF), *extra_args)

    o_fw = call(False, [], [])
    blk_r = _scan_block(True, n_chunks, n_lat_chunks)
    return call(True, [pl.BlockSpec((n, HALF), lambda s: (blk_r(s), 0)),
                       pl.BlockSpec((n, HALF), lambda s: (blk_r(s), 4))], [o_fw, proj])


def _s5_mats(a_re, a_im, log_dt, b_re, b_im, c_re, c_im):
    hp = lax.Precision.HIGHEST
    n = S5_CHUNK
    dt = jnp.exp(log_dt)[..., None]
    la_re, la_im = a_re * dt, a_im * dt
    mag = jnp.exp(la_re)
    ab_re, ab_im = mag * jnp.cos(la_im), mag * jnp.sin(la_im)
    nr, ni = ab_re - 1.0, ab_im
    den = jnp.square(a_re) + jnp.square(a_im)
    cr = (nr * a_re + ni * a_im) / den
    ci = (ni * a_re - nr * a_im) / den
    bb_re = cr[..., None] * b_re - ci[..., None] * b_im
    bb_im = cr[..., None] * b_im + ci[..., None] * b_re

    def apow(tau):
        tau = jnp.asarray(tau, F32)[None, None, :, None]
        m = jnp.exp(tau * la_re[:, :, None, :])
        return m * jnp.cos(tau * la_im[:, :, None, :]), m * jnp.sin(tau * la_im[:, :, None, :])

    def c_apow(tau):
        pr, pi = apow(tau)
        return (c_re[:, :, None] * pr[:, :, :, None] - c_im[:, :, None] * pi[:, :, :, None],
                c_re[:, :, None] * pi[:, :, :, None] + c_im[:, :, None] * pr[:, :, :, None])

    ca_re, ca_im = c_apow(np.arange(n))
    kern = (jnp.einsum('dgtip,dgpj->dgtij', ca_re, bb_re, precision=hp)
            - jnp.einsum('dgtip,dgpj->dgtij', ca_im, bb_im, precision=hp))
    s_idx, t_idx = np.arange(n)[:, None], np.arange(n)[None, :]
    g = a_re.shape[1]
    lp = n * S5_GROUP

    def toeplitz(kd, lag):
        onehot = jnp.asarray(lag[None] == np.arange(n)[:, None, None], F32)
        return jnp.einsum('lst,glij->gsjti', onehot, kd, precision=hp).reshape(g, lp, lp)

    kmat = toeplitz(kern[0], t_idx - s_idx) + toeplitz(kern[1], s_idx - t_idx)

    def emat(d, tau):
        pr, pi = apow(tau)
        pr, pi = pr[d][:, :, None, :], pi[d][:, :, None, :]
        br, bi = bb_re[d].transpose(0, 2, 1)[:, None], bb_im[d].transpose(0, 2, 1)[:, None]
        return jnp.concatenate([pr * br - pi * bi, pr * bi + pi * br], axis=-1).reshape(g, lp, 2 * S5_STATE)

    def fmat(d, tau):
        fr, fi = c_apow(tau)
        fr, fi = fr[d], fi[d]
        return jnp.concatenate([fr, -fi], axis=-1).transpose(0, 3, 1, 2).reshape(g, 2 * S5_STATE, lp)

    e_all = jnp.concatenate([emat(0, n - 1 - np.arange(n)), emat(1, np.arange(n))], axis=-1)
    f_all = jnp.concatenate([fmat(0, np.arange(n) + 1), fmat(1, n - np.arange(n))], axis=1)
    pr, pi = apow(np.array([n]))
    pr, pi = pr[:, :, 0], pi[:, :, 0]
    ar2 = jnp.concatenate([pr[0], pr[0], pr[1], pr[1]], axis=-1)
    ai2 = jnp.concatenate([-pi[0], pi[0], -pi[1], pi[1]], axis=-1)
    return kmat.astype(BF16), e_all.astype(BF16), f_all.astype(BF16), ar2, ai2


def _s5_state_kernel(u_ref, e_ref, s_ref):
    s_ref[...] = _nn(u_ref[...], e_ref[...])


def _s5_scan_kernel(sf_ref, sb_ref, ar_ref, ai_ref, hf_ref, hb_ref, stf_ref, stb_ref):
    nb = sf_ref.shape[0]
    w = 2 * S5_STATE
    ar, ai = ar_ref[...], ai_ref[...]

    @pl.when(pl.program_id(0) == 0)
    def _():
        stf_ref[...] = jnp.zeros_like(stf_ref)
        stb_ref[...] = jnp.zeros_like(stb_ref)

    def step(h, s, lo):
        return ar[:, lo:lo + w] * h + ai[:, lo:lo + w] * pltpu.roll(h, S5_STATE, 1) + s

    hf, hb = stf_ref[...], stb_ref[...]
    for j in range(nb):
        hf_ref[j] = hf
        hf = step(hf, sf_ref[j], 0)
        hb_ref[nb - 1 - j] = hb
        hb = step(hb, sb_ref[nb - 1 - j], w)
    stf_ref[...] = hf
    stb_ref[...] = hb


def _s5_out_kernel(u_ref, k_ref, h_ref, f_ref, y_ref):
    y_ref[...] = _nn(u_ref[...], k_ref[...]) + _nn(h_ref[...].astype(BF16), f_ref[...])


def _s5(u, mats, n_lat_chunks):
    kmat, emat, fmat, ar2, ai2 = mats
    t_all = u.shape[0]
    n = S5_CHUNK
    nc = t_all // n
    g = S5_GROUPS
    lp = n * S5_GROUP
    w4 = 4 * S5_STATE
    ug = u.astype(BF16).reshape(nc, n, g, S5_GROUP).transpose(2, 0, 1, 3).reshape(g, nc, lp)
    grp = lambda a, b: pl.BlockSpec((None, a, b), lambda i: (i, 0, 0))
    s_end = pl.pallas_call(
        _s5_state_kernel, grid=(g,),
        in_specs=[grp(nc, lp), grp(lp, w4)], out_specs=grp(nc, w4),
        out_shape=jax.ShapeDtypeStruct((g, nc, w4), F32), compiler_params=_cparams(1),
    )(ug, emat)
    nb = S5_SCAN_BLOCK
    w2 = 2 * S5_STATE
    fw_blk = _scan_block(False, nc // nb, n_lat_chunks // nb)
    bw_blk = _scan_block(True, nc // nb, n_lat_chunks // nb)
    s_t = s_end.transpose(1, 0, 2)
    coef = pl.BlockSpec((g, w4), lambda i: (0, 0))
    h_fw, h_bw = pl.pallas_call(
        _s5_scan_kernel, grid=(nc // nb,),
        in_specs=[pl.BlockSpec((nb, g, w2), lambda i: (fw_blk(i), 0, 0)),
                  pl.BlockSpec((nb, g, w2), lambda i: (bw_blk(i), 0, 1)), coef, coef],
        out_specs=[pl.BlockSpec((nb, g, w2), lambda i: (fw_blk(i), 0, 0)),
                   pl.BlockSpec((nb, g, w2), lambda i: (bw_blk(i), 0, 0))],
        out_shape=[jax.ShapeDtypeStruct((nc, g, w2), F32)] * 2,
        scratch_shapes=[pltpu.VMEM((g, w2), F32)] * 2,
        compiler_params=_cparams(1),
    )(s_t, s_t, ar2, ai2)
    h_in = jnp.concatenate([h_fw, h_bw], axis=-1).transpose(1, 0, 2)
    y = pl.pallas_call(
        _s5_out_kernel, grid=(g,),
        in_specs=[grp(nc, lp), grp(lp, lp), grp(nc, w4), grp(w4, lp)], out_specs=grp(nc, lp),
        out_shape=jax.ShapeDtypeStruct((g, nc, lp), F32), compiler_params=_cparams(1),
    )(ug, kmat, h_in, fmat)
    return y.reshape(g, nc, n, S5_GROUP).transpose(1, 2, 0, 3).reshape(t_all, HALF)


def _top_rows(x, k):
    rows = []
    for _ in range(k):
        mx = jnp.max(x, axis=0, keepdims=True)
        rows.append(mx)
        x = jnp.where(x == mx, NEG, x)
    return rows


def _peer_route_kernel(h_ref, wq_ref, keys_ref, s1_ref, e1_ref, d0_ref, e0_ref, top_scr):
    kk = PEER_TOPK
    qt = _nt(wq_ref[...], h_ref[...])
    for hp in range(2 * PEER_HEADS):
        sc = _nn(keys_ref[hp], qt[hp * HEAD_DIM:(hp + 1) * HEAD_DIM, :].astype(BF16))
        h, p = divmod(hp, 2)
        if p == 0:
            d0_ref[h] = sc
        else:
            s1_ref[h] = sc
        for r, mx in enumerate(_top_rows(sc, kk + 1)):
            top_scr[hp, r:r + 1, :] = mx
        top_scr[hp, kk + 1:, :] = jnp.full((PEER_TOP_ROWS - kk - 1, sc.shape[1]), NEG, F32)

    row = lax.broadcasted_iota(jnp.int32, (PEER_TOP_ROWS, 1), 0)
    for h in range(PEER_HEADS):
        a = top_scr[2 * h]
        b = top_scr[2 * h + 1]
        parts = [a[0:1] + b, a[1:2] + b[0:8], a[2:3] + b[0:8], a[3:4] + b[0:8]]
        parts += [jnp.where(row >= 4, a + b[j:j + 1], NEG) for j in range(3)]
        cand = jnp.maximum(jnp.concatenate(parts, axis=0), NEG)
        top = _top_rows(cand, kk + 1)
        thr = 0.5 * (top[kk - 1] + top[kk])
        z = jnp.sum(jnp.where(cand >= thr, jnp.exp(cand - top[0]), 0.0), axis=0, keepdims=True)
        s0 = d0_ref[h]
        e0_ref[h] = jnp.exp(s0 - a[0:1]) / z
        d0_ref[h] = thr - s0
        e1_ref[h] = jnp.exp(s1_ref[h] - b[0:1])


def _peer_expert_kernel(h_ref, s1_ref, e1_ref, d0_ref, e0_ref, u_ref, v_ref, o_ref, g_scr):
    e = pl.program_id(1)
    ni = u_ref.shape[0] // PEER_N_KEYS

    @pl.when(e == 0)
    def _():
        o_ref[...] = jnp.zeros_like(o_ref)

    at = _nt(u_ref[...], h_ref[...])
    for ii in range(ni):
        i = e * ni + ii
        w = None
        for h in range(PEER_HEADS):
            t = jnp.where(s1_ref[h] >= d0_ref[h, pl.ds(i, 1), :], e1_ref[h] * e0_ref[h, pl.ds(i, 1), :], 0.0)
            w = t if w is None else w + t
        rs = slice(ii * PEER_N_KEYS, (ii + 1) * PEER_N_KEYS)
        g_scr[rs, :] = (_gelu_tanh(at[rs, :]) * w).astype(BF16)
    o_ref[...] += _tn(g_scr[...], v_ref[...])


def _tok3(t):
    return pl.BlockSpec((PEER_HEADS, PEER_N_KEYS, t), lambda i, *_: (0, 0, i))


def _peer_route(h2, wq_t_bf, keys_bf):
    t_all, d = h2.shape
    nh, nk = PEER_HEADS, PEER_N_KEYS
    t1 = PEER_P1_TILE
    tok3 = _tok3
    route_shape = jax.ShapeDtypeStruct((nh, nk, t_all), F32)
    return pl.pallas_call(
        _peer_route_kernel,
        grid=(t_all // t1,),
        in_specs=[pl.BlockSpec((t1, d), lambda i: (i, 0)),
                  pl.BlockSpec(wq_t_bf.shape, lambda i: (0, 0)),
                  pl.BlockSpec(keys_bf.shape, lambda i: (0, 0, 0))],
        out_specs=[tok3(t1)] * 4,
        out_shape=[route_shape] * 4,
        scratch_shapes=[pltpu.VMEM((2 * nh, PEER_TOP_ROWS, t1), F32)],
        compiler_params=_cparams(1),
    )(h2, wq_t_bf, keys_bf)


def _peer(h2, wq_t_bf, keys_bf, u_bf, v_bf):
    t_all, d = h2.shape
    tok3 = _tok3
    s1, e1, d0, e0 = _peer_route(h2, wq_t_bf, keys_bf)
    tt = PEER_TOK_TILE if t_all % PEER_TOK_TILE == 0 else PEER_P1_TILE
    et = PEER_EXP_TILE
    n_exp = u_bf.shape[0]
    return pl.pallas_call(
        _peer_expert_kernel,
        grid=(t_all // tt, n_exp // et),
        in_specs=[pl.BlockSpec((tt, d), lambda i, e: (i, 0))] + [tok3(tt)] * 4
        + [pl.BlockSpec((et, d), lambda i, e: (e, 0)), pl.BlockSpec((et, d), lambda i, e: (e, 0))],
        out_specs=pl.BlockSpec((tt, d), lambda i, e: (i, 0)),
        out_shape=jax.ShapeDtypeStruct((t_all, d), F32),
        scratch_shapes=[pltpu.VMEM((et, tt), BF16)],
        compiler_params=_cparams(2),
    )(h2, s1, e1, d0, e0, u_bf, v_bf)


def _rope_tables(t_lat, t_ctx):
    pos = np.arange(t_lat)
    nfreq = HEAD_DIM // 4
    freqs = ROPE_BASE ** (-jnp.arange(nfreq, dtype=F32) / nfreq)
    ang_r = jnp.asarray(pos // GRID_W, F32)[:, None] * freqs
    ang_c = jnp.asarray(pos % GRID_W, F32)[:, None] * freqs
    cos = jnp.concatenate([jnp.cos(ang_r), jnp.cos(ang_r), jnp.cos(ang_c), jnp.cos(ang_c)], axis=1)
    sin = jnp.concatenate([-jnp.sin(ang_r), jnp.sin(ang_r), -jnp.sin(ang_c), jnp.sin(ang_c)], axis=1)
    cos = jnp.concatenate([cos, jnp.ones((t_ctx, HEAD_DIM), F32)], axis=0)
    sin = jnp.concatenate([sin, jnp.zeros((t_ctx, HEAD_DIM), F32)], axis=0)
    return cos, sin


def _peer_and_norm(x1, h2, mods, ln_g, ln_b, n_lat_tiles, w_q, sub_keys, u_tab, v_tab):
    keys = sub_keys.reshape(2 * PEER_HEADS, PEER_N_KEYS, -1).astype(BF16)
    p = _peer(h2, w_q.T.astype(BF16), keys, u_tab.astype(BF16), v_tab.astype(BF16))
    return _ln2(x1, p, mods, ln_g, ln_b, n_lat_tiles)


def _layer_ab(x_all, mods, t_lat, t_ctx, w_in, w_out, igate_b, fgate_b, rpb, ln_g, ln_b):
    n_lat_tiles = t_lat // ROW_TILE
    ng = 4 * N_HEADS
    w_main = jnp.concatenate([w_in[:, :4 * HALF], w_in[:, 4 * HALF + ng:]], axis=1).astype(BF16)
    w_gate = jnp.pad(w_in[:, 4 * HALF:4 * HALF + ng], ((0, 0), (0, 128 - ng))).astype(BF16)
    proj = _inproj(x_all, mods, w_main, n_lat_tiles, 512)
    gates = _inproj(x_all, mods, w_gate, n_lat_tiles, 128)
    cos_t, sin_t = _rope_tables(t_lat, t_ctx)
    y_m = _mlstm(proj, gates, gates.T, cos_t, sin_t, igate_b, fgate_b, t_lat // SCAN_CHUNK)
    y_na = jnp.concatenate([_na(proj, rpb, t_lat, t_ctx), _ctx_attn(proj, t_lat, t_ctx)], axis=0)
    return _outproj(y_m, [y_na], w_out.astype(BF16), x_all, mods, ln_g, ln_b, n_lat_tiles)


def _layer_cd(x_all, mods, t_lat, w_in, w_out, lb, s5p, s5_d, glu_w, glu_b, ln_g, ln_b):
    n_lat_tiles = t_lat // ROW_TILE
    proj = _inproj(x_all, mods, w_in.astype(BF16), n_lat_tiles, 512)
    y_h = _gla(proj, lb, t_lat // GLA_CHUNK)
    u5 = proj[:, 5 * HALF:]
    y5 = _s5(u5, _s5_mats(*[p.astype(F32) for p in s5p]), t_lat // S5_CHUNK)
    return _outproj(y_h, [y5, u5], w_out.astype(BF16), x_all, mods, ln_g, ln_b, n_lat_tiles,
                    glu_params=(s5_d.astype(F32), glu_w.astype(BF16), glu_b.astype(F32)))


def kernel(x, c, ctx, c_ctx, ada_w, ada_b, ln_g, ln_b, ab_w_in, ab_w_out, mlstm_igate_b, mlstm_fgate_b, na_rpb, cd_w_in, cd_w_out, hgrn_lb_logits, s5_a_re, s5_a_im, s5_log_dt, s5_b_re, s5_b_im, s5_c_re, s5_c_im, s5_d, s5_glu_w, s5_glu_b, peer_w_q, peer_sub_keys, peer_u, peer_v):
    t_lat, t_ctx = x.shape[1], ctx.shape[1]
    assert x.shape[0] == 1 and t_ctx == ROW_TILE and t_lat % (NA_QROWS * GRID_W) == 0
    depth = ada_w.shape[0]
    lb_soft = jax.nn.softmax(hgrn_lb_logits.astype(F32), axis=0)
    lower_bounds = jnp.cumsum(lb_soft, axis=0) - lb_soft[0]
    x_all = jnp.concatenate([x[0], ctx[0]], axis=0).astype(F32)
    n_lat_tiles = t_lat // ROW_TILE
    for l in range(depth):
        j = l // 2
        mods = _adaln(c, c_ctx, ada_w[l], ada_b[l])
        if l % 2 == 0:
            x1, h2 = _layer_ab(x_all, mods, t_lat, t_ctx, ab_w_in[j], ab_w_out[j], mlstm_igate_b[j],
                               mlstm_fgate_b[j], na_rpb[j], ln_g[l, 0], ln_b[l, 0])
        else:
            s5p = (s5_a_re[j], s5_a_im[j], s5_log_dt[j], s5_b_re[j], s5_b_im[j], s5_c_re[j], s5_c_im[j])
            x1, h2 = _layer_cd(x_all, mods, t_lat, cd_w_in[j], cd_w_out[j], lower_bounds[l], s5p, s5_d[j],
                               s5_glu_w[j], s5_glu_b[j], ln_g[l, 0], ln_b[l, 0])
        x_all = _peer_and_norm(x1, h2, mods, ln_g[l, 1], ln_b[l, 1], n_lat_tiles, peer_w_q[l],
                               peer_sub_keys[l], peer_u[l], peer_v[l])
    return x_all[:t_lat][None].astype(x.dtype)
```

```python
import functools
import math

import numpy as np
import jax
import jax.numpy as jnp
from jax import lax
from jax.experimental import pallas as pl
from jax.experimental.pallas import tpu as pltpu

F32 = jnp.float32
BF16 = jnp.bfloat16

HEAD_DIM = 128
GRID_W = 64
N_HEADS = 8
HALF = N_HEADS * HEAD_DIM
NA_WIN_ROWS = 8
NA_WIN_COLS = 16
ROPE_BASE = 10000.0
S5_GROUP = 16
S5_STATE = 64
S5_GROUPS = HALF // S5_GROUP
PEER_HEADS = 8
PEER_N_KEYS = 128
PEER_TOPK = 16
PEER_TOP_ROWS = 24
LN_EPS = 1e-5
DEPTH = 2
ALPHA = (2.0 * DEPTH) ** 0.25
QK_SCALE = HEAD_DIM ** -0.5

ROW_TILE = 256
MM_ROW_TILE = 1280
MM_COL_TILE = 1024
SCAN_CHUNK = 128
GLA_CHUNK = 64
S5_CHUNK = 32
S5_SCAN_BLOCK = 8
NA_QROWS = 8
PEER_P1_TILE = 256
PEER_TOK_TILE = 640
PEER_EXP_TILE = 512
NEG = -1e30
GLA_SAFE_LOG = -80.0
VMEM_LIMIT = 56 * 1024 * 1024


def _cparams(n_axes):
    return pltpu.CompilerParams(dimension_semantics=("arbitrary",) * n_axes,
                                vmem_limit_bytes=VMEM_LIMIT)


def _nt(a, b):
    return lax.dot_general(a, b, (((1,), (1,)), ((), ())), preferred_element_type=F32)


def _tn(a, b):
    return lax.dot_general(a, b, (((0,), (0,)), ((), ())), preferred_element_type=F32)


def _nn(a, b):
    return jnp.dot(a, b, preferred_element_type=F32)


def _split3(x):
    hi = x.astype(BF16)
    r = x - hi.astype(F32)
    mid = r.astype(BF16)
    lo = (r - mid.astype(F32)).astype(BF16)
    return hi, mid, lo


def _log_sigmoid(x):
    return jnp.minimum(x, 0.0) - jnp.log1p(jnp.exp(-jnp.abs(x)))


def _sigmoid(x):
    return 1.0 / (1.0 + jnp.exp(-x))


def _gelu_tanh(x):
    return 0.5 * x * (1.0 + jnp.tanh(math.sqrt(2.0 / math.pi) * (x + 0.044715 * (x * x * x))))


def _head_norm(x):
    mu = jnp.mean(x, axis=-1, keepdims=True)
    xc = x - mu
    return xc * lax.rsqrt(jnp.mean(xc * xc, axis=-1, keepdims=True) + LN_EPS)


def _adaln_kernel(ct_ref, w_ref, b_ref, o_ref):
    ct = ct_ref[...]
    s = ct * _sigmoid(ct)
    w = w_ref[...]
    o_ref[0:1, :] = jnp.sum(s[:, 0:1] * w, axis=0, keepdims=True) + b_ref[...]
    o_ref[1:2, :] = jnp.sum(s[:, 1:2] * w, axis=0, keepdims=True) + b_ref[...]


def _adaln(c_lat, c_ctx, w, b):
    d = w.shape[0]
    n = w.shape[1]
    tn = n // 8
    ct = jnp.stack([c_lat.reshape(d), c_ctx.reshape(d)], axis=1)
    out = pl.pallas_call(
        _adaln_kernel,
        grid=(n // tn,),
        in_specs=[pl.BlockSpec((d, 2), lambda j: (0, 0)),
                  pl.BlockSpec((d, tn), lambda j: (0, j)),
                  pl.BlockSpec((1, tn), lambda j: (0, j))],
        out_specs=pl.BlockSpec((2, tn), lambda j: (0, j)),
        out_shape=jax.ShapeDtypeStruct((2, n), F32),
        compiler_params=_cparams(1),
    )(ct, w, b.reshape(1, n))
    return jnp.pad(out.reshape(2, 6, d), ((0, 0), (0, 2), (0, 0)))


def _modulated(x, m_ref):
    return (x * (1.0 + m_ref[1:2, :]) + m_ref[0:1, :]).astype(BF16)


def _modulate_kernel(x_ref, m_ref, o_ref):
    o_ref[...] = _modulated(x_ref[...], m_ref)


def _mods_spec(d, n_lat_tiles):
    return pl.BlockSpec((None, 8, d), lambda i: (jnp.where(i < n_lat_tiles, 0, 1), 0, 0))


def _modulate(x_all, mods, n_lat_tiles):
    t_all, d = x_all.shape
    row = pl.BlockSpec((ROW_TILE, d), lambda i: (i, 0))
    return pl.pallas_call(
        _modulate_kernel,
        grid=(t_all // ROW_TILE,),
        in_specs=[row, _mods_spec(d, n_lat_tiles)],
        out_specs=row,
        out_shape=jax.ShapeDtypeStruct((t_all, d), BF16),
        compiler_params=_cparams(1),
    )(x_all, mods)


def _matmul_kernel(x_ref, w_ref, o_ref):
    o_ref[...] = _nn(x_ref[...], w_ref[...])


def _inproj(xm, w_bf):
    t_all, d = xm.shape
    n = w_bf.shape[1]
    tm = MM_ROW_TILE if t_all % MM_ROW_TILE == 0 else ROW_TILE
    tn = min(n, MM_COL_TILE)
    return pl.pallas_call(
        _matmul_kernel,
        grid=(t_all // tm, n // tn),
        in_specs=[pl.BlockSpec((tm, d), lambda i, j: (i, 0)),
                  pl.BlockSpec((d, tn), lambda i, j: (0, j))],
        out_specs=pl.BlockSpec((tm, tn), lambda i, j: (i, j)),
        out_shape=jax.ShapeDtypeStruct((t_all, n), F32),
        compiler_params=_cparams(2),
    )(xm, w_bf)


def _ln_rows(z, g, b):
    mu = jnp.mean(z, axis=-1, keepdims=True)
    zc = z - mu
    return zc * lax.rsqrt(jnp.mean(zc * zc, axis=-1, keepdims=True) + LN_EPS) * g + b


def _outproj_kernel(*refs, glu):
    if glu:
        (a1_ref, y5_ref, u5_ref, ds_ref, gw_ref, gb_ref, w1_ref, w2_ref, x_ref, m_ref, g_ref, b_ref,
         x1_ref, h2_ref) = refs
        y = _gelu_tanh(y5_ref[...] + ds_ref[...] * u5_ref[...])
        a2 = y * _sigmoid(_nn(y.astype(BF16), gw_ref[...]) + gb_ref[...])
    else:
        a1_ref, a2_ref, w1_ref, w2_ref, x_ref, m_ref, g_ref, b_ref, x1_ref, h2_ref = refs
        a2 = a2_ref[...]
    y = _nn(a1_ref[...].astype(BF16), w1_ref[...]) + _nn(a2.astype(BF16), w2_ref[...])
    x1 = _ln_rows(ALPHA * x_ref[...] + m_ref[2:3, :] * y, g_ref[...], b_ref[...])
    x1_ref[...] = x1
    h2_ref[...] = (x1 * (1.0 + m_ref[4:5, :]) + m_ref[3:4, :]).astype(BF16)


def _outproj(a1, a2_parts, w_out_bf, x_all, mods, ln_g, ln_b, n_lat_tiles, glu_params=None):
    t_all, d = x_all.shape
    half = a1.shape[1]
    row = lambda i: (i, 0)
    const = lambda i: (0, 0)
    act = pl.BlockSpec((ROW_TILE, half), row)
    a2_specs = [pl.BlockSpec((ROW_TILE, half), lambda i, c=c: (i, c)) for _, c in a2_parts]
    a2_args = [a for a, _ in a2_parts]
    if glu_params is not None:
        d_skip, glu_w_bf, glu_b = glu_params
        a2_specs += [pl.BlockSpec((1, half), const), pl.BlockSpec((half, half), const),
                     pl.BlockSpec((1, half), const)]
        a2_args += [d_skip.reshape(1, half), glu_w_bf, glu_b.reshape(1, half)]
    return pl.pallas_call(
        functools.partial(_outproj_kernel, glu=glu_params is not None),
        grid=(t_all // ROW_TILE,),
        in_specs=[act] + a2_specs + [
            pl.BlockSpec((half, d), const), pl.BlockSpec((half, d), const),
            pl.BlockSpec((ROW_TILE, d), row),
            pl.BlockSpec((None, 8, d), lambda i: (jnp.where(i < n_lat_tiles, 0, 1), 0, 0)),
            pl.BlockSpec((1, d), const), pl.BlockSpec((1, d), const)],
        out_specs=[pl.BlockSpec((ROW_TILE, d), row), pl.BlockSpec((ROW_TILE, d), row)],
        out_shape=[jax.ShapeDtypeStruct((t_all, d), F32), jax.ShapeDtypeStruct((t_all, d), BF16)],
        compiler_params=_cparams(1),
    )(a1, *a2_args, w_out_bf[:half], w_out_bf[half:], x_all, mods, ln_g.reshape(1, d), ln_b.reshape(1, d))


def _ln2_kernel(x_ref, p_ref, m_ref, g_ref, b_ref, *rest):
    x2 = _ln_rows(ALPHA * x_ref[...] + m_ref[5:6, :] * p_ref[...], g_ref[...], b_ref[...])
    if len(rest) == 1:
        rest[0][...] = x2
    else:
        mn_ref, o_ref, xm_ref = rest
        o_ref[...] = x2
        xm_ref[...] = _modulated(x2, mn_ref)


def _ln2(x1, peer_out, mods, ln_g, ln_b, n_lat_tiles, mods_next=None):
    t_all, d = x1.shape
    row = pl.BlockSpec((ROW_TILE, d), lambda i: (i, 0))
    vec = pl.BlockSpec((1, d), lambda i: (0, 0))
    nxt = mods_next is not None
    return pl.pallas_call(
        _ln2_kernel,
        grid=(t_all // ROW_TILE,),
        in_specs=[row, row, _mods_spec(d, n_lat_tiles), vec, vec] + ([_mods_spec(d, n_lat_tiles)] if nxt else []),
        out_specs=[row, row] if nxt else row,
        out_shape=([jax.ShapeDtypeStruct((t_all, d), F32), jax.ShapeDtypeStruct((t_all, d), BF16)] if nxt
                   else jax.ShapeDtypeStruct((t_all, d), F32)),
        compiler_params=_cparams(1),
    )(x1, peer_out, mods, ln_g.reshape(1, d), ln_b.reshape(1, d), *([mods_next] if nxt else []))


def _scan_block(rev, n_chunks, n_lat_chunks):
    if rev:
        return lambda s: n_chunks - 1 - s
    n_ctx = n_chunks - n_lat_chunks
    return lambda s: jnp.where(s < n_ctx, s + n_lat_chunks, s - n_ctx)


def _scan_mask(rev, n):
    row = lax.broadcasted_iota(jnp.int32, (n, n), 0)
    col = lax.broadcasted_iota(jnp.int32, (n, n), 1)
    return (col >= row) if rev else (col <= row)


def _rope(x, cos, sin):
    lane = lax.broadcasted_iota(jnp.int32, x.shape, 1)
    partner = jnp.where((lane % 64) < 32, pltpu.roll(x, 96, 1), pltpu.roll(x, 32, 1))
    return x * cos + partner * sin


def _mlstm_kernel(*refs, rev):
    if rev:
        (q_ref, k_ref, v_ref, g_ref, gt_ref, cos_ref, sin_ref, bias_ref, biast_ref, hf_ref, o_ref,
         out_ref, c_ref, n_ref, m_ref) = refs
    else:
        (q_ref, k_ref, v_ref, g_ref, gt_ref, cos_ref, sin_ref, bias_ref, biast_ref,
         out_ref, c_ref, n_ref, m_ref) = refs
    n = q_ref.shape[0]

    @pl.when(pl.program_id(0) == 0)
    def _():
        c_ref[...] = jnp.zeros_like(c_ref)
        n_ref[...] = jnp.zeros_like(n_ref)
        m_ref[...] = jnp.zeros_like(m_ref)

    mask = _scan_mask(rev, n)
    tri = mask.astype(BF16)
    pre_c = g_ref[...] + bias_ref[...]
    pre_r = gt_ref[...] + biast_ref[...]
    b_c = sum(_nn(tri, p) for p in _split3(_log_sigmoid(pre_c)))
    b_r = sum(_nt(p, tri) for p in _split3(_log_sigmoid(pre_r)))
    cos = cos_ref[...]
    sin = sin_ref[...]
    last = 0 if rev else n - 1
    off = N_HEADS if rev else 0

    for h in range(N_HEADS):
        hs = slice(h * HEAD_DIM, (h + 1) * HEAD_DIM)
        ci, cf = off + h, 2 * N_HEADS + off + h
        bc, br = b_c[:, cf:cf + 1], b_r[cf:cf + 1, :]
        ic, ir = pre_c[:, ci:ci + 1], pre_r[ci:ci + 1, :]
        m_prev = m_ref[h:h + 1, 0:1]
        logw = jnp.where(mask, bc - br + ir, NEG)
        inter = bc + m_prev
        m_t = jnp.maximum(inter, jnp.max(logw, axis=1, keepdims=True))
        w_inter = jnp.exp(inter - m_t)
        q = _rope(q_ref[:, hs], cos, sin) * QK_SCALE
        k = _rope(k_ref[:, hs], cos, sin)
        qb, kb, vb = q.astype(BF16), k.astype(BF16), v_ref[:, hs].astype(BF16)
        s = _nt(qb, kb) * jnp.exp(logw - m_t)
        c_old = c_ref[h]
        n_old = n_ref[h:h + 1, :]
        num = _nn(s.astype(BF16), vb) + _nn(qb, c_old.astype(BF16)) * w_inter
        den = jnp.sum(s, axis=1, keepdims=True) + jnp.sum(q * n_old, axis=1, keepdims=True) * w_inter
        den = jnp.maximum(jnp.abs(den), jnp.exp(-m_t))
        hout = num / den
        m_new = m_t[last:last + 1, :]
        b_last = bc[last:last + 1, :]
        w_old = jnp.exp(b_last + m_prev - m_new)
        kw = k * jnp.exp(b_last - bc + ic - m_new)
        c_ref[h] = w_old * c_old + _tn(kw.astype(BF16), vb)
        n_ref[h:h + 1, :] = w_old * n_old + jnp.sum(kw, axis=0, keepdims=True)
        m_ref[h:h + 1, :] = jnp.broadcast_to(m_new, (1, HEAD_DIM))
        if rev:
            out_ref[:, hs] = _head_norm(hf_ref[:, hs] + hout) * _sigmoid(o_ref[:, hs])
        else:
            out_ref[:, hs] = hout


def _mlstm(proj, gates, gates_t, cos_t, sin_t, igate_b, fgate_b, n_lat_chunks):
    t_all = proj.shape[0]
    n = SCAN_CHUNK
    n_chunks = t_all // n
    bias = jnp.zeros((1, 128), F32).at[0, :4 * N_HEADS].set(
        jnp.concatenate([igate_b.reshape(-1), fgate_b.reshape(-1)]))

    def call(rev, extra_in, extra_args):
        blk = _scan_block(rev, n_chunks, n_lat_chunks)
        colblk = lambda c: pl.BlockSpec((n, HALF), lambda s, c=c: (blk(s), c))
        tok = pl.BlockSpec((n, 128), lambda s: (blk(s), 0))
        return pl.pallas_call(
            functools.partial(_mlstm_kernel, rev=rev),
            grid=(n_chunks,),
            in_specs=[colblk(0), colblk(1), colblk(2), tok,
                      pl.BlockSpec((128, n), lambda s: (0, blk(s))), tok, tok,
                      pl.BlockSpec((1, 128), lambda s: (0, 0)),
                      pl.BlockSpec((128, 1), lambda s: (0, 0))] + extra_in,
            out_specs=pl.BlockSpec((n, HALF), lambda s: (blk(s), 0)),
            out_shape=jax.ShapeDtypeStruct((t_all, HALF), F32),
            scratch_shapes=[pltpu.VMEM((N_HEADS, HEAD_DIM, HEAD_DIM), F32),
                            pltpu.VMEM((N_HEADS, HEAD_DIM), F32),
                            pltpu.VMEM((N_HEADS, HEAD_DIM), F32)],
            compiler_params=_cparams(1),
        )(proj, proj, proj, gates, gates_t, cos_t, sin_t, bias, bias.reshape(128, 1), *extra_args)

    h_fw = call(False, [], [])
    blk_r = _scan_block(True, n_chunks, n_lat_chunks)
    return call(True, [pl.BlockSpec((n, HALF), lambda s: (blk_r(s), 0)),
                       pl.BlockSpec((n, HALF), lambda s: (blk_r(s), 3))], [h_fw, proj])


def _na_kernel(q_ref, kp_ref, kc_ref, kn_ref, vp_ref, vc_ref, vn_ref, kx_ref, vx_ref, bias_ref, o_ref):
    qb = (q_ref[...] * QK_SCALE).astype(BF16)
    kk = jnp.concatenate([kp_ref[...], kc_ref[...], kn_ref[...]], axis=0).astype(BF16)
    vv = jnp.concatenate([vp_ref[...], vc_ref[...], vn_ref[...]], axis=0).astype(BF16)
    s_loc = _nt(qb, kk) + bias_ref[...]
    s_ctx = _nt(qb, kx_ref[...].astype(BF16))
    m = jnp.maximum(jnp.max(s_loc, axis=1, keepdims=True), jnp.max(s_ctx, axis=1, keepdims=True))
    p_loc = jnp.exp(s_loc - m)
    p_ctx = jnp.exp(s_ctx - m)
    l = jnp.sum(p_loc, axis=1, keepdims=True) + jnp.sum(p_ctx, axis=1, keepdims=True)
    o_ref[...] = (_nn(p_loc.astype(BF16), vv) + _nn(p_ctx.astype(BF16), vx_ref[...].astype(BF16))) / l


def _na_bias(rpb, rows):
    qn = NA_QROWS * GRID_W
    kn = 2 * qn
    ql = np.arange(qn)
    kl = np.arange(kn)
    out = []
    nqb = rows // NA_QROWS
    lr, krl = np.arange(NA_QROWS)[:, None], np.arange(2 * NA_QROWS)[None, :]
    dr = np.clip(krl - NA_QROWS // 2 - lr + NA_WIN_ROWS - 1, 0, 2 * NA_WIN_ROWS - 2)
    col = np.arange(GRID_W)
    dc = np.clip(col[None, :] - col[:, None] + NA_WIN_COLS - 1, 0, 2 * NA_WIN_COLS - 2)
    oh_r = jnp.asarray(dr[..., None] == np.arange(2 * NA_WIN_ROWS - 1), F32)
    oh_c = jnp.asarray(dc[..., None] == np.arange(2 * NA_WIN_COLS - 1), F32)
    table = jnp.einsum('hab,lka,qcb->hlqkc', rpb, oh_r, oh_c,
                       precision=lax.Precision.HIGHEST).reshape(rpb.shape[0], qn, kn)
    for qb in (0, 1, nqb - 1):
        qr = NA_QROWS * qb + ql // GRID_W
        qc = ql % GRID_W
        kr = NA_QROWS * qb - NA_QROWS // 2 + kl // GRID_W
        kc = kl % GRID_W
        r0 = np.clip(qr - NA_WIN_ROWS // 2, 0, rows - NA_WIN_ROWS)
        c0 = np.clip(qc - NA_WIN_COLS // 2, 0, GRID_W - NA_WIN_COLS)
        ok = ((kr[None, :] >= r0[:, None]) & (kr[None, :] < r0[:, None] + NA_WIN_ROWS)
              & (kc[None, :] >= c0[:, None]) & (kc[None, :] < c0[:, None] + NA_WIN_COLS)
              & (kr[None, :] >= 0) & (kr[None, :] < rows))
        out.append(jnp.where(jnp.asarray(ok)[None], table, NEG))
    return jnp.stack(out)


def _na(proj, rpb, t_lat, t_ctx):
    rows = t_lat // GRID_W
    qn = NA_QROWS * GRID_W
    hn = qn // 2
    nqb = rows // NA_QROWS
    assert nqb >= 2 and t_ctx == hn
    bias = _na_bias(rpb.astype(F32), rows)
    last_half = 2 * nqb - 1
    ctx_blk = t_lat // hn
    cls = lambda b: jnp.where(b == 0, 0, jnp.where(b == nqb - 1, 2, 1))

    def kv_specs(col0):
        return [pl.BlockSpec((hn, HEAD_DIM), lambda h, b: (jnp.maximum(2 * b - 1, 0), col0 + h)),
                pl.BlockSpec((qn, HEAD_DIM), lambda h, b: (b, col0 + h)),
                pl.BlockSpec((hn, HEAD_DIM), lambda h, b: (jnp.minimum(2 * b + 2, last_half), col0 + h))]

    kcol, vcol = 5 * N_HEADS, 6 * N_HEADS
    return pl.pallas_call(
        _na_kernel,
        grid=(N_HEADS, nqb),
        in_specs=[pl.BlockSpec((qn, HEAD_DIM), lambda h, b: (b, 4 * N_HEADS + h))]
        + kv_specs(kcol) + kv_specs(vcol)
        + [pl.BlockSpec((hn, HEAD_DIM), lambda h, b: (ctx_blk, kcol + h)),
           pl.BlockSpec((hn, HEAD_DIM), lambda h, b: (ctx_blk, vcol + h)),
           pl.BlockSpec((None, None, qn, 2 * qn), lambda h, b: (cls(b), h, 0, 0))],
        out_specs=pl.BlockSpec((qn, HEAD_DIM), lambda h, b: (b, h)),
        out_shape=jax.ShapeDtypeStruct((t_lat, HALF), F32),
        compiler_params=_cparams(2),
    )(proj, proj, proj, proj, proj, proj, proj, proj, proj, bias)


def _ctx_attn_kernel(q_ref, k_ref, v_ref, o_ref):
    qb = (q_ref[...] * QK_SCALE).astype(BF16)
    s = _nt(qb, k_ref[...].astype(BF16))
    p = jnp.exp(s - jnp.max(s, axis=1, keepdims=True))
    o_ref[...] = _nn(p.astype(BF16), v_ref[...].astype(BF16)) / jnp.sum(p, axis=1, keepdims=True)


def _ctx_attn(proj, t_lat, t_ctx):
    blk = t_lat // t_ctx
    spec = lambda c: pl.BlockSpec((t_ctx, HEAD_DIM), lambda h, c=c: (blk, c * N_HEADS + h))
    return pl.pallas_call(
        _ctx_attn_kernel,
        grid=(N_HEADS,),
        in_specs=[spec(4), spec(5), spec(6)],
        out_specs=pl.BlockSpec((t_ctx, HEAD_DIM), lambda h: (0, h)),
        out_shape=jax.ShapeDtypeStruct((t_ctx, HALF), F32),
        compiler_params=_cparams(1),
    )(proj, proj, proj)


def _gla_kernel(*refs, rev):
    if rev:
        q_ref, i_ref, f_ref, lb_ref, of_ref, g_ref, out_ref, st_ref, o_scr, k_scr, b_scr = refs
    else:
        q_ref, i_ref, f_ref, lb_ref, out_ref, st_ref, o_scr, k_scr, b_scr = refs
    n = q_ref.shape[0]

    @pl.when(pl.program_id(0) == 0)
    def _():
        st_ref[...] = jnp.zeros_like(st_ref)

    mask = _scan_mask(rev, n)
    tri = mask.astype(BF16)
    last = 0 if rev else n - 1
    x = f_ref[...]
    key = (1.0 - lb_ref[...]) * _sigmoid(-x)
    b = sum(_nn(tri, p) for p in _split3(jnp.log1p(-key)))
    b_end = b[last:last + 1, :]
    qx = q_ref[...]
    q = qx * _sigmoid(qx)
    qt = q * jnp.exp(b)
    kdec = key * jnp.exp(b_end - b)
    w_end = jnp.exp(b_end)
    k_scr[...] = key
    b_scr[...] = b

    for h in range(N_HEADS):
        hs = slice(h * HEAD_DIM, (h + 1) * HEAD_DIM)
        st = st_ref[h]
        vb = i_ref[:, hs].astype(BF16)
        o_scr[:, hs] = _nt(qt[:, hs].astype(BF16), st.astype(BF16))
        st_ref[h] = st * w_end[:, hs] + _tn(vb, kdec[:, hs].astype(BF16))

    safe = jnp.min(b_end) >= 2.0 * GLA_SAFE_LOG

    @pl.when(safe)
    def _():
        mid = 0.5 * b_end
        qm = q * jnp.exp(b_scr[...] - mid)
        km = k_scr[...] * jnp.exp(mid - b_scr[...])
        for h in range(N_HEADS):
            hs = slice(h * HEAD_DIM, (h + 1) * HEAD_DIM)
            att = jnp.where(mask, _nt(qm[:, hs].astype(BF16), km[:, hs].astype(BF16)), 0.0)
            o_scr[:, hs] += _nn(att.astype(BF16), i_ref[:, hs].astype(BF16))

    @pl.when(jnp.logical_not(safe))
    def _():
        t_idx = lax.broadcasted_iota(jnp.int32, (n, 1), 0)
        for h in range(N_HEADS):
            hs = slice(h * HEAD_DIM, (h + 1) * HEAD_DIM)
            q_h = q[:, hs]
            b_h = b[:, hs]

            def body(grp, acc):
                base = pl.multiple_of(grp * 8, 8)
                b8, k8, v8 = b_scr[pl.ds(base, 8), hs], k_scr[pl.ds(base, 8), hs], i_ref[pl.ds(base, 8), hs]
                for r in range(8):
                    s = base + r
                    e = jnp.exp(jnp.minimum(b_h - b8[r:r + 1, :], 0.0))
                    a = jnp.sum(q_h * k8[r:r + 1, :] * e, axis=1, keepdims=True)
                    a = jnp.where((t_idx <= s) if rev else (t_idx >= s), a, 0.0)
                    acc = acc + a * v8[r:r + 1, :]
                return acc

            o_scr[:, hs] += lax.fori_loop(0, n // 8, body, jnp.zeros((n, HEAD_DIM), F32))

    if rev:
        for h in range(N_HEADS):
            hs = slice(h * HEAD_DIM, (h + 1) * HEAD_DIM)
            gx = g_ref[:, hs]
            out_ref[:, hs] = _head_norm(of_ref[:, hs] + o_scr[:, hs]) * (gx * _sigmoid(gx))
    else:
        out_ref[...] = o_scr[...]


def _gla(proj, lb, n_lat_chunks):
    t_all = proj.shape[0]
    n = GLA_CHUNK
    n_chunks = t_all // n

    def call(rev, extra_in, extra_args):
        blk = _scan_block(rev, n_chunks, n_lat_chunks)
        colblk = lambda c: pl.BlockSpec((n, HALF), lambda s, c=c: (blk(s), c))
        return pl.pallas_call(
            functools.partial(_gla_kernel, rev=rev),
            grid=(n_chunks,),
            in_specs=[colblk(0), colblk(1), colblk(3 if rev else 2),
                      pl.BlockSpec((1, HALF), lambda s: (0, 0))] + extra_in,
            out_specs=pl.BlockSpec((n, HALF), lambda s: (blk(s), 0)),
            out_shape=jax.ShapeDtypeStruct((t_all, HALF), F32),
            scratch_shapes=[pltpu.VMEM((N_HEADS, HEAD_DIM, HEAD_DIM), F32),
                            pltpu.VMEM((n, HALF), F32), pltpu.VMEM((n, HALF), F32),
                            pltpu.VMEM((n, HALF), F32)],
            compiler_params=_cparams(1),
        )(proj, proj, proj, lb.reshape(1, HALF), *extra_args)

    o_fw = call(False, [], [])
    blk_r = _scan_block(True, n_chunks, n_lat_chunks)
    return call(True, [pl.BlockSpec((n, HALF), lambda s: (blk_r(s), 0)),
                       pl.BlockSpec((n, HALF), lambda s: (blk_r(s), 4))], [o_fw, proj])


def _s5_mats(a_re, a_im, log_dt, b_re, b_im, c_re, c_im):
    hp = lax.Precision.HIGHEST
    n = S5_CHUNK
    dt = jnp.exp(log_dt)[..., None]
    la_re, la_im = a_re * dt, a_im * dt
    mag = jnp.exp(la_re)
    ab_re, ab_im = mag * jnp.cos(la_im), mag * jnp.sin(la_im)
    nr, ni = ab_re - 1.0, ab_im
    den = jnp.square(a_re) + jnp.square(a_im)
    cr = (nr * a_re + ni * a_im) / den
    ci = (ni * a_re - nr * a_im) / den
    bb_re = cr[..., None] * b_re - ci[..., None] * b_im
    bb_im = cr[..., None] * b_im + ci[..., None] * b_re

    def apow(tau):
        tau = jnp.asarray(tau, F32)[None, None, :, None]
        m = jnp.exp(tau * la_re[:, :, None, :])
        return m * jnp.cos(tau * la_im[:, :, None, :]), m * jnp.sin(tau * la_im[:, :, None, :])

    def c_apow(tau):
        pr, pi = apow(tau)
        return (c_re[:, :, None] * pr[:, :, :, None] - c_im[:, :, None] * pi[:, :, :, None],
                c_re[:, :, None] * pi[:, :, :, None] + c_im[:, :, None] * pr[:, :, :, None])

    ca_re, ca_im = c_apow(np.arange(n))
    kern = (jnp.einsum('dgtip,dgpj->dgtij', ca_re, bb_re, precision=hp)
            - jnp.einsum('dgtip,dgpj->dgtij', ca_im, bb_im, precision=hp))
    s_idx, t_idx = np.arange(n)[:, None], np.arange(n)[None, :]
    g = a_re.shape[1]
    lp = n * S5_GROUP

    def toeplitz(kd, lag):
        onehot = jnp.asarray(lag[None] == np.arange(n)[:, None, None], F32)
        return jnp.einsum('lst,glij->gsjti', onehot, kd, precision=hp).reshape(g, lp, lp)

    kmat = toeplitz(kern[0], t_idx - s_idx) + toeplitz(kern[1], s_idx - t_idx)

    def emat(d, tau):
        pr, pi = apow(tau)
        pr, pi = pr[d][:, :, None, :], pi[d][:, :, None, :]
        br, bi = bb_re[d].transpose(0, 2, 1)[:, None], bb_im[d].transpose(0, 2, 1)[:, None]
        return jnp.concatenate([pr * br - pi * bi, pr * bi + pi * br], axis=-1).reshape(g, lp, 2 * S5_STATE)

    def fmat(d, tau):
        fr, fi = c_apow(tau)
        fr, fi = fr[d], fi[d]
        return jnp.concatenate([fr, -fi], axis=-1).transpose(0, 3, 1, 2).reshape(g, 2 * S5_STATE, lp)

    e_all = jnp.concatenate([emat(0, n - 1 - np.arange(n)), emat(1, np.arange(n))], axis=-1)
    f_all = jnp.concatenate([fmat(0, np.arange(n) + 1), fmat(1, n - np.arange(n))], axis=1)
    pr, pi = apow(np.array([n]))
    pr, pi = pr[:, :, 0], pi[:, :, 0]
    ar2 = jnp.concatenate([pr[0], pr[0], pr[1], pr[1]], axis=-1)
    ai2 = jnp.concatenate([-pi[0], pi[0], -pi[1], pi[1]], axis=-1)
    return kmat.astype(BF16), e_all.astype(BF16), f_all.astype(BF16), ar2, ai2


def _s5_state_kernel(u_ref, e_ref, s_ref):
    s_ref[...] = _nn(u_ref[...], e_ref[...])


def _s5_scan_kernel(sf_ref, sb_ref, ar_ref, ai_ref, hf_ref, hb_ref, stf_ref, stb_ref):
    nb = sf_ref.shape[0]
    w = 2 * S5_STATE
    ar, ai = ar_ref[...], ai_ref[...]

    @pl.when(pl.program_id(0) == 0)
    def _():
        stf_ref[...] = jnp.zeros_like(stf_ref)
        stb_ref[...] = jnp.zeros_like(stb_ref)

    def step(h, s, lo):
        return ar[:, lo:lo + w] * h + ai[:, lo:lo + w] * pltpu.roll(h, S5_STATE, 1) + s

    hf, hb = stf_ref[...], stb_ref[...]
    for j in range(nb):
        hf_ref[j] = hf
        hf = step(hf, sf_ref[j], 0)
        hb_ref[nb - 1 - j] = hb
        hb = step(hb, sb_ref[nb - 1 - j], w)
    stf_ref[...] = hf
    stb_ref[...] = hb


def _s5_out_kernel(u_ref, k_ref, h_ref, f_ref, y_ref):
    y_ref[...] = _nn(u_ref[...], k_ref[...]) + _nn(h_ref[...].astype(BF16), f_ref[...])


def _s5(u, mats, n_lat_chunks):
    kmat, emat, fmat, ar2, ai2 = mats
    t_all = u.shape[0]
    n = S5_CHUNK
    nc = t_all // n
    g = S5_GROUPS
    lp = n * S5_GROUP
    w4 = 4 * S5_STATE
    ug = u.astype(BF16).reshape(nc, n, g, S5_GROUP).transpose(2, 0, 1, 3).reshape(g, nc, lp)
    grp = lambda a, b: pl.BlockSpec((None, a, b), lambda i: (i, 0, 0))
    s_end = pl.pallas_call(
        _s5_state_kernel, grid=(g,),
        in_specs=[grp(nc, lp), grp(lp, w4)], out_specs=grp(nc, w4),
        out_shape=jax.ShapeDtypeStruct((g, nc, w4), F32), compiler_params=_cparams(1),
    )(ug, emat)
    nb = S5_SCAN_BLOCK
    w2 = 2 * S5_STATE
    fw_blk = _scan_block(False, nc // nb, n_lat_chunks // nb)
    bw_blk = _scan_block(True, nc // nb, n_lat_chunks // nb)
    s_t = s_end.transpose(1, 0, 2)
    coef = pl.BlockSpec((g, w4), lambda i: (0, 0))
    h_fw, h_bw = pl.pallas_call(
        _s5_scan_kernel, grid=(nc // nb,),
        in_specs=[pl.BlockSpec((nb, g, w2), lambda i: (fw_blk(i), 0, 0)),
                  pl.BlockSpec((nb, g, w2), lambda i: (bw_blk(i), 0, 1)), coef, coef],
        out_specs=[pl.BlockSpec((nb, g, w2), lambda i: (fw_blk(i), 0, 0)),
                   pl.BlockSpec((nb, g, w2), lambda i: (bw_blk(i), 0, 0))],
        out_shape=[jax.ShapeDtypeStruct((nc, g, w2), F32)] * 2,
        scratch_shapes=[pltpu.VMEM((g, w2), F32)] * 2,
        compiler_params=_cparams(1),
    )(s_t, s_t, ar2, ai2)
    h_in = jnp.concatenate([h_fw, h_bw], axis=-1).transpose(1, 0, 2)
    y = pl.pallas_call(
        _s5_out_kernel, grid=(g,),
        in_specs=[grp(nc, lp), grp(lp, lp), grp(nc, w4), grp(w4, lp)], out_specs=grp(nc, lp),
        out_shape=jax.ShapeDtypeStruct((g, nc, lp), F32), compiler_params=_cparams(1),
    )(ug, kmat, h_in, fmat)
    return y.reshape(g, nc, n, S5_GROUP).transpose(1, 2, 0, 3).reshape(t_all, HALF)


def _top_rows(x, k):
    rows = []
    for _ in range(k):
        mx = jnp.max(x, axis=0, keepdims=True)
        rows.append(mx)
        x = jnp.where(x == mx, NEG, x)
    return rows


def _peer_route_kernel(h_ref, wq_ref, keys_ref, s1_ref, e1_ref, d0_ref, e0_ref, top_scr):
    kk = PEER_TOPK
    qt = _nt(wq_ref[...], h_ref[...])
    for hp in range(2 * PEER_HEADS):
        sc = _nn(keys_ref[hp], qt[hp * HEAD_DIM:(hp + 1) * HEAD_DIM, :].astype(BF16))
        h, p = divmod(hp, 2)
        if p == 0:
            d0_ref[h] = sc
        else:
            s1_ref[h] = sc
        for r, mx in enumerate(_top_rows(sc, kk + 1)):
            top_scr[hp, r:r + 1, :] = mx
        top_scr[hp, kk + 1:, :] = jnp.full((PEER_TOP_ROWS - kk - 1, sc.shape[1]), NEG, F32)

    row = lax.broadcasted_iota(jnp.int32, (PEER_TOP_ROWS, 1), 0)
    for h in range(PEER_HEADS):
        a = top_scr[2 * h]
        b = top_scr[2 * h + 1]
        parts = [a[0:1] + b, a[1:2] + b[0:8], a[2:3] + b[0:8], a[3:4] + b[0:8]]
        parts += [jnp.where(row >= 4, a + b[j:j + 1], NEG) for j in range(3)]
        cand = jnp.maximum(jnp.concatenate(parts, axis=0), NEG)
        top = _top_rows(cand, kk + 1)
        thr = 0.5 * (top[kk - 1] + top[kk])
        z = jnp.sum(jnp.where(cand >= thr, jnp.exp(cand - top[0]), 0.0), axis=0, keepdims=True)
        s0 = d0_ref[h]
        e0_ref[h] = jnp.exp(s0 - a[0:1]) / z
        d0_ref[h] = thr - s0
        e1_ref[h] = jnp.exp(s1_ref[h] - b[0:1])


def _peer_expert_kernel(h_ref, s1_ref, e1_ref, d0_ref, e0_ref, u_ref, v_ref, o_ref,
                        at0_scr, at1_scr, g0_scr, g1_scr):
    s = pl.program_id(1)
    n_tiles = pl.num_programs(1) - 2
    ni = u_ref.shape[0] // PEER_N_KEYS

    @pl.when(s == 0)
    def _():
        o_ref[...] = jnp.zeros_like(o_ref)
        for scr in (at0_scr, at1_scr, g0_scr, g1_scr):
            scr[...] = jnp.zeros_like(scr)

    tile = jnp.clip(s - 1, 0, n_tiles - 1)

    def stages(at_new, at_old, g_new, g_old):
        at_new[...] = _nt(u_ref[...], h_ref[...])
        for ii in range(ni):
            i = tile * ni + ii
            w = None
            for h in range(PEER_HEADS):
                t = jnp.where(s1_ref[h] >= d0_ref[h, pl.ds(i, 1), :],
                              e1_ref[h] * e0_ref[h, pl.ds(i, 1), :], 0.0)
                w = t if w is None else w + t
            rs = slice(ii * PEER_N_KEYS, (ii + 1) * PEER_N_KEYS)
            g_new[rs, :] = (_gelu_tanh(at_old[rs, :]) * w).astype(BF16)
        o_ref[...] += _tn(g_old[...], v_ref[...])

    @pl.when(s % 2 == 0)
    def _():
        stages(at0_scr, at1_scr, g1_scr, g0_scr)

    @pl.when(s % 2 == 1)
    def _():
        stages(at1_scr, at0_scr, g0_scr, g1_scr)


def _tok3(t):
    return pl.BlockSpec((PEER_HEADS, PEER_N_KEYS, t), lambda i, *_: (0, 0, i))


def _peer_route(h2, wq_t_bf, keys_bf):
    t_all, d = h2.shape
    nh, nk = PEER_HEADS, PEER_N_KEYS
    t1 = PEER_P1_TILE
    tok3 = _tok3
    route_shape = jax.ShapeDtypeStruct((nh, nk, t_all), F32)
    return pl.pallas_call(
        _peer_route_kernel,
        grid=(t_all // t1,),
        in_specs=[pl.BlockSpec((t1, d), lambda i: (i, 0)),
                  pl.BlockSpec(wq_t_bf.shape, lambda i: (0, 0)),
                  pl.BlockSpec(keys_bf.shape, lambda i: (0, 0, 0))],
        out_specs=[tok3(t1)] * 4,
        out_shape=[route_shape] * 4,
        scratch_shapes=[pltpu.VMEM((2 * nh, PEER_TOP_ROWS, t1), F32)],
        compiler_params=_cparams(1),
    )(h2, wq_t_bf, keys_bf)


def _peer(h2, wq_t_bf, keys_bf, u_bf, v_bf):
    t_all, d = h2.shape
    tok3 = _tok3
    s1, e1, d0, e0 = _peer_route(h2, wq_t_bf, keys_bf)
    tt = PEER_TOK_TILE if t_all % PEER_TOK_TILE == 0 else PEER_P1_TILE
    et = PEER_EXP_TILE
    n_tiles = u_bf.shape[0] // et
    return pl.pallas_call(
        _peer_expert_kernel,
        grid=(t_all // tt, n_tiles + 2),
        in_specs=[pl.BlockSpec((tt, d), lambda i, s: (i, 0))] + [tok3(tt)] * 4
        + [pl.BlockSpec((et, d), lambda i, s: (jnp.minimum(s, n_tiles - 1), 0)),
           pl.BlockSpec((et, d), lambda i, s: (jnp.clip(s - 2, 0, n_tiles - 1), 0))],
        out_specs=pl.BlockSpec((tt, d), lambda i, s: (i, 0)),
        out_shape=jax.ShapeDtypeStruct((t_all, d), F32),
        scratch_shapes=[pltpu.VMEM((et, tt), F32)] * 2 + [pltpu.VMEM((et, tt), BF16)] * 2,
        compiler_params=_cparams(2),
    )(h2, s1, e1, d0, e0, u_bf, v_bf)


def _rope_tables(t_lat, t_ctx):
    pos = np.arange(t_lat)
    nfreq = HEAD_DIM // 4
    freqs = ROPE_BASE ** (-jnp.arange(nfreq, dtype=F32) / nfreq)
    ang_r = jnp.asarray(pos // GRID_W, F32)[:, None] * freqs
    ang_c = jnp.asarray(pos % GRID_W, F32)[:, None] * freqs
    cos = jnp.concatenate([jnp.cos(ang_r), jnp.cos(ang_r), jnp.cos(ang_c), jnp.cos(ang_c)], axis=1)
    sin = jnp.concatenate([-jnp.sin(ang_r), jnp.sin(ang_r), -jnp.sin(ang_c), jnp.sin(ang_c)], axis=1)
    cos = jnp.concatenate([cos, jnp.ones((t_ctx, HEAD_DIM), F32)], axis=0)
    sin = jnp.concatenate([sin, jnp.zeros((t_ctx, HEAD_DIM), F32)], axis=0)
    return cos, sin


def _peer_and_norm(x1, h2, mods, ln_g, ln_b, n_lat_tiles, w_q, sub_keys, u_tab, v_tab, mods_next=None):
    keys = sub_keys.reshape(2 * PEER_HEADS, PEER_N_KEYS, -1).astype(BF16)
    p = _peer(h2, w_q.T.astype(BF16), keys, u_tab.astype(BF16), v_tab.astype(BF16))
    return _ln2(x1, p, mods, ln_g, ln_b, n_lat_tiles, mods_next)


def _layer_ab(x_all, mods, t_lat, t_ctx, w_in, w_out, igate_b, fgate_b, rpb, ln_g, ln_b, xm=None):
    n_lat_tiles = t_lat // ROW_TILE
    if xm is None:
        xm = _modulate(x_all, mods, n_lat_tiles)
    ng = 4 * N_HEADS
    w_main = jnp.concatenate([w_in[:, :4 * HALF], w_in[:, 4 * HALF + ng:]], axis=1).astype(BF16)
    w_gate = jnp.pad(w_in[:, 4 * HALF:4 * HALF + ng], ((0, 0), (0, 128 - ng))).astype(BF16)
    proj = _inproj(xm, w_main)
    gates = _inproj(xm, w_gate)
    cos_t, sin_t = _rope_tables(t_lat, t_ctx)
    y_m = _mlstm(proj, gates, gates.T, cos_t, sin_t, igate_b, fgate_b, t_lat // SCAN_CHUNK)
    y_na = jnp.concatenate([_na(proj, rpb, t_lat, t_ctx), _ctx_attn(proj, t_lat, t_ctx)], axis=0)
    return _outproj(y_m, [(y_na, 0)], w_out.astype(BF16), x_all, mods, ln_g, ln_b, n_lat_tiles)


def _layer_cd(x_all, mods, t_lat, w_in, w_out, lb, s5p, s5_d, glu_w, glu_b, ln_g, ln_b, xm=None):
    n_lat_tiles = t_lat // ROW_TILE
    if xm is None:
        xm = _modulate(x_all, mods, n_lat_tiles)
    proj = _inproj(xm, w_in.astype(BF16))
    y_h = _gla(proj, lb, t_lat // GLA_CHUNK)
    y5 = _s5(proj[:, 5 * HALF:], _s5_mats(*[p.astype(F32) for p in s5p]), t_lat // S5_CHUNK)
    return _outproj(y_h, [(y5, 0), (proj, 5)], w_out.astype(BF16), x_all, mods, ln_g, ln_b, n_lat_tiles,
                    glu_params=(s5_d.astype(F32), glu_w.astype(BF16), glu_b.astype(F32)))


def kernel(x, c, ctx, c_ctx, ada_w, ada_b, ln_g, ln_b, ab_w_in, ab_w_out, mlstm_igate_b, mlstm_fgate_b, na_rpb, cd_w_in, cd_w_out, hgrn_lb_logits, s5_a_re, s5_a_im, s5_log_dt, s5_b_re, s5_b_im, s5_c_re, s5_c_im, s5_d, s5_glu_w, s5_glu_b, peer_w_q, peer_sub_keys, peer_u, peer_v):
    t_lat, t_ctx = x.shape[1], ctx.shape[1]
    assert x.shape[0] == 1 and t_ctx == ROW_TILE and t_lat % (NA_QROWS * GRID_W) == 0
    depth = ada_w.shape[0]
    lb_soft = jax.nn.softmax(hgrn_lb_logits.astype(F32), axis=0)
    lower_bounds = jnp.cumsum(lb_soft, axis=0) - lb_soft[0]
    x_all = jnp.concatenate([x[0], ctx[0]], axis=0).astype(F32)
    n_lat_tiles = t_lat // ROW_TILE
    mods_all = [_adaln(c, c_ctx, ada_w[l], ada_b[l]) for l in range(depth)]
    xm = _modulate(x_all, mods_all[0], n_lat_tiles)
    for l in range(depth):
        j = l // 2
        mods = mods_all[l]
        if l % 2 == 0:
            x1, h2 = _layer_ab(x_all, mods, t_lat, t_ctx, ab_w_in[j], ab_w_out[j], mlstm_igate_b[j],
                               mlstm_fgate_b[j], na_rpb[j], ln_g[l, 0], ln_b[l, 0], xm=xm)
        else:
            s5p = (s5_a_re[j], s5_a_im[j], s5_log_dt[j], s5_b_re[j], s5_b_im[j], s5_c_re[j], s5_c_im[j])
            x1, h2 = _layer_cd(x_all, mods, t_lat, cd_w_in[j], cd_w_out[j], lower_bounds[l], s5p, s5_d[j],
                               s5_glu_w[j], s5_glu_b[j], ln_g[l, 0], ln_b[l, 0], xm=xm)
        peer_args = (x1, h2, mods, ln_g[l, 1], ln_b[l, 1], n_lat_tiles, peer_w_q[l], peer_sub_keys[l],
                     peer_u[l], peer_v[l])
        if l + 1 < depth:
            x_all, xm = _peer_and_norm(*peer_args, mods_next=mods_all[l + 1])
        else:
            x_all = _peer_and_norm(*peer_args)
    return x_all[:t_lat][None].astype(x.dtype)
```

```python
import functools
import math

import numpy as np
import jax
import jax.numpy as jnp
from jax import lax
from jax.experimental import pallas as pl
from jax.experimental.pallas import tpu as pltpu

F32 = jnp.float32
BF16 = jnp.bfloat16

HEAD_DIM = 128
GRID_W = 64
N_HEADS = 8
HALF = N_HEADS * HEAD_DIM
NA_WIN_ROWS = 8
NA_WIN_COLS = 16
ROPE_BASE = 10000.0
S5_GROUP = 16
S5_STATE = 64
S5_GROUPS = HALF // S5_GROUP
PEER_HEADS = 8
PEER_N_KEYS = 128
PEER_TOPK = 16
PEER_TOP_ROWS = 24
LN_EPS = 1e-5
DEPTH = 2
ALPHA = (2.0 * DEPTH) ** 0.25
QK_SCALE = HEAD_DIM ** -0.5

ROW_TILE = 256
MM_ROW_TILE = 1280
MM_COL_TILE = 1024
SCAN_CHUNK = 128
GLA_CHUNK = 64
S5_CHUNK = 32
S5_SCAN_BLOCK = 8
NA_QROWS = 8
PEER_P1_TILE = 256
PEER_TOK_TILE = 640
PEER_EXP_TILE = 512
PEER_GATE_ROWS = 64
PEER_U_SPLIT = 4
NEG = -1e30
GLA_SAFE_LOG = -80.0
VMEM_LIMIT = 56 * 1024 * 1024


def _cparams(n_axes):
    return pltpu.CompilerParams(dimension_semantics=("arbitrary",) * n_axes,
                                vmem_limit_bytes=VMEM_LIMIT)


def _nt(a, b):
    return lax.dot_general(a, b, (((1,), (1,)), ((), ())), preferred_element_type=F32)


def _tn(a, b):
    return lax.dot_general(a, b, (((0,), (0,)), ((), ())), preferred_element_type=F32)


def _nn(a, b):
    return jnp.dot(a, b, preferred_element_type=F32)


def _split3(x):
    hi = x.astype(BF16)
    r = x - hi.astype(F32)
    mid = r.astype(BF16)
    lo = (r - mid.astype(F32)).astype(BF16)
    return hi, mid, lo


def _log_sigmoid(x):
    return jnp.minimum(x, 0.0) - jnp.log1p(jnp.exp(-jnp.abs(x)))


def _sigmoid(x):
    return 1.0 / (1.0 + jnp.exp(-x))


def _gelu_tanh(x):
    return 0.5 * x * (1.0 + jnp.tanh(math.sqrt(2.0 / math.pi) * (x + 0.044715 * (x * x * x))))


def _head_norm(x):
    mu = jnp.mean(x, axis=-1, keepdims=True)
    xc = x - mu
    return xc * lax.rsqrt(jnp.mean(xc * xc, axis=-1, keepdims=True) + LN_EPS)


def _adaln_kernel(ct_ref, w_ref, b_ref, o_ref):
    ct = ct_ref[...]
    s = ct * _sigmoid(ct)
    w = w_ref[...]
    o_ref[0:1, :] = jnp.sum(s[:, 0:1] * w, axis=0, keepdims=True) + b_ref[...]
    o_ref[1:2, :] = jnp.sum(s[:, 1:2] * w, axis=0, keepdims=True) + b_ref[...]


def _adaln(c_lat, c_ctx, w, b):
    d = w.shape[0]
    n = w.shape[1]
    tn = n // 8
    ct = jnp.stack([c_lat.reshape(d), c_ctx.reshape(d)], axis=1)
    out = pl.pallas_call(
        _adaln_kernel,
        grid=(n // tn,),
        in_specs=[pl.BlockSpec((d, 2), lambda j: (0, 0)),
                  pl.BlockSpec((d, tn), lambda j: (0, j)),
                  pl.BlockSpec((1, tn), lambda j: (0, j))],
        out_specs=pl.BlockSpec((2, tn), lambda j: (0, j)),
        out_shape=jax.ShapeDtypeStruct((2, n), F32),
        compiler_params=_cparams(1),
    )(ct, w, b.reshape(1, n))
    return jnp.pad(out.reshape(2, 6, d), ((0, 0), (0, 2), (0, 0)))


def _modulated(x, m_ref):
    return (x * (1.0 + m_ref[1:2, :]) + m_ref[0:1, :]).astype(BF16)


def _modulate_kernel(x_ref, m_ref, o_ref):
    o_ref[...] = _modulated(x_ref[...], m_ref)


def _mods_spec(d, n_lat_tiles):
    return pl.BlockSpec((None, 8, d), lambda i: (jnp.where(i < n_lat_tiles, 0, 1), 0, 0))


def _modulate(x_all, mods, n_lat_tiles):
    t_all, d = x_all.shape
    row = pl.BlockSpec((ROW_TILE, d), lambda i: (i, 0))
    return pl.pallas_call(
        _modulate_kernel,
        grid=(t_all // ROW_TILE,),
        in_specs=[row, _mods_spec(d, n_lat_tiles)],
        out_specs=row,
        out_shape=jax.ShapeDtypeStruct((t_all, d), BF16),
        compiler_params=_cparams(1),
    )(x_all, mods)


def _matmul_kernel(x_ref, w_ref, o_ref):
    o_ref[...] = _nn(x_ref[...], w_ref[...])


def _inproj(xm, w_bf):
    t_all, d = xm.shape
    n = w_bf.shape[1]
    tm = MM_ROW_TILE if t_all % MM_ROW_TILE == 0 else ROW_TILE
    tn = min(n, MM_COL_TILE)
    return pl.pallas_call(
        _matmul_kernel,
        grid=(t_all // tm, n // tn),
        in_specs=[pl.BlockSpec((tm, d), lambda i, j: (i, 0)),
                  pl.BlockSpec((d, tn), lambda i, j: (0, j))],
        out_specs=pl.BlockSpec((tm, tn), lambda i, j: (i, j)),
        out_shape=jax.ShapeDtypeStruct((t_all, n), F32),
        compiler_params=_cparams(2),
    )(xm, w_bf)


def _ln_rows(z, g, b):
    mu = jnp.mean(z, axis=-1, keepdims=True)
    zc = z - mu
    return zc * lax.rsqrt(jnp.mean(zc * zc, axis=-1, keepdims=True) + LN_EPS) * g + b


def _outproj_kernel(*refs, glu):
    if glu:
        (a1_ref, y5_ref, u5_ref, ds_ref, gw_ref, gb_ref, w1_ref, w2_ref, x_ref, m_ref, g_ref, b_ref,
         x1_ref, h2_ref) = refs
        y = _gelu_tanh(y5_ref[...] + ds_ref[...] * u5_ref[...])
        a2 = y * _sigmoid(_nn(y.astype(BF16), gw_ref[...]) + gb_ref[...])
    else:
        a1_ref, a2_ref, w1_ref, w2_ref, x_ref, m_ref, g_ref, b_ref, x1_ref, h2_ref = refs
        a2 = a2_ref[...]
    y = _nn(a1_ref[...].astype(BF16), w1_ref[...]) + _nn(a2.astype(BF16), w2_ref[...])
    x1 = _ln_rows(ALPHA * x_ref[...] + m_ref[2:3, :] * y, g_ref[...], b_ref[...])
    x1_ref[...] = x1
    h2_ref[...] = (x1 * (1.0 + m_ref[4:5, :]) + m_ref[3:4, :]).astype(BF16)


def _outproj(a1, a2_parts, w_out_bf, x_all, mods, ln_g, ln_b, n_lat_tiles, glu_params=None):
    t_all, d = x_all.shape
    half = a1.shape[1]
    row = lambda i: (i, 0)
    const = lambda i: (0, 0)
    act = pl.BlockSpec((ROW_TILE, half), row)
    a2_specs = [pl.BlockSpec((ROW_TILE, half), lambda i, c=c: (i, c)) for _, c in a2_parts]
    a2_args = [a for a, _ in a2_parts]
    if glu_params is not None:
        d_skip, glu_w_bf, glu_b = glu_params
        a2_specs += [pl.BlockSpec((1, half), const), pl.BlockSpec((half, half), const),
                     pl.BlockSpec((1, half), const)]
        a2_args += [d_skip.reshape(1, half), glu_w_bf, glu_b.reshape(1, half)]
    return pl.pallas_call(
        functools.partial(_outproj_kernel, glu=glu_params is not None),
        grid=(t_all // ROW_TILE,),
        in_specs=[act] + a2_specs + [
            pl.BlockSpec((half, d), const), pl.BlockSpec((half, d), const),
            pl.BlockSpec((ROW_TILE, d), row),
            pl.BlockSpec((None, 8, d), lambda i: (jnp.where(i < n_lat_tiles, 0, 1), 0, 0)),
            pl.BlockSpec((1, d), const), pl.BlockSpec((1, d), const)],
        out_specs=[pl.BlockSpec((ROW_TILE, d), row), pl.BlockSpec((ROW_TILE, d), row)],
        out_shape=[jax.ShapeDtypeStruct((t_all, d), F32), jax.ShapeDtypeStruct((t_all, d), BF16)],
        compiler_params=_cparams(1),
    )(a1, *a2_args, w_out_bf[:half], w_out_bf[half:], x_all, mods, ln_g.reshape(1, d), ln_b.reshape(1, d))


def _ln2_kernel(x_ref, p_ref, m_ref, g_ref, b_ref, *rest):
    x2 = _ln_rows(ALPHA * x_ref[...] + m_ref[5:6, :] * p_ref[...].T, g_ref[...], b_ref[...])
    if len(rest) == 1:
        rest[0][...] = x2
    else:
        mn_ref, o_ref, xm_ref = rest
        o_ref[...] = x2
        xm_ref[...] = _modulated(x2, mn_ref)


def _ln2(x1, peer_out_t, mods, ln_g, ln_b, n_lat_tiles, mods_next=None):
    t_all, d = x1.shape
    row = pl.BlockSpec((ROW_TILE, d), lambda i: (i, 0))
    col = pl.BlockSpec((d, ROW_TILE), lambda i: (0, i))
    vec = pl.BlockSpec((1, d), lambda i: (0, 0))
    nxt = mods_next is not None
    return pl.pallas_call(
        _ln2_kernel,
        grid=(t_all // ROW_TILE,),
        in_specs=[row, col, _mods_spec(d, n_lat_tiles), vec, vec] + ([_mods_spec(d, n_lat_tiles)] if nxt else []),
        out_specs=[row, row] if nxt else row,
        out_shape=([jax.ShapeDtypeStruct((t_all, d), F32), jax.ShapeDtypeStruct((t_all, d), BF16)] if nxt
                   else jax.ShapeDtypeStruct((t_all, d), F32)),
        compiler_params=_cparams(1),
    )(x1, peer_out_t, mods, ln_g.reshape(1, d), ln_b.reshape(1, d), *([mods_next] if nxt else []))


def _scan_block(rev, n_chunks, n_lat_chunks):
    if rev:
        return lambda s: n_chunks - 1 - s
    n_ctx = n_chunks - n_lat_chunks
    return lambda s: jnp.where(s < n_ctx, s + n_lat_chunks, s - n_ctx)


def _scan_mask(rev, n):
    row = lax.broadcasted_iota(jnp.int32, (n, n), 0)
    col = lax.broadcasted_iota(jnp.int32, (n, n), 1)
    return (col >= row) if rev else (col <= row)


def _rope(x, cos, sin):
    lane = lax.broadcasted_iota(jnp.int32, x.shape, 1)
    partner = jnp.where((lane % 64) < 32, pltpu.roll(x, 96, 1), pltpu.roll(x, 32, 1))
    return x * cos + partner * sin


def _mlstm_kernel(*refs, rev):
    if rev:
        (q_ref, k_ref, v_ref, g_ref, gt_ref, cos_ref, sin_ref, bias_ref, biast_ref, hf_ref, o_ref,
         out_ref, c_ref, n_ref, m_ref) = refs
    else:
        (q_ref, k_ref, v_ref, g_ref, gt_ref, cos_ref, sin_ref, bias_ref, biast_ref,
         out_ref, c_ref, n_ref, m_ref) = refs
    n = q_ref.shape[0]

    @pl.when(pl.program_id(0) == 0)
    def _():
        c_ref[...] = jnp.zeros_like(c_ref)
        n_ref[...] = jnp.zeros_like(n_ref)
        m_ref[...] = jnp.zeros_like(m_ref)

    mask = _scan_mask(rev, n)
    tri = mask.astype(BF16)
    pre_c = g_ref[...] + bias_ref[...]
    pre_r = gt_ref[...] + biast_ref[...]
    b_c = sum(_nn(tri, p) for p in _split3(_log_sigmoid(pre_c)))
    b_r = sum(_nt(p, tri) for p in _split3(_log_sigmoid(pre_r)))
    cos = cos_ref[...]
    sin = sin_ref[...]
    last = 0 if rev else n - 1
    off = N_HEADS if rev else 0

    for h in range(N_HEADS):
        hs = slice(h * HEAD_DIM, (h + 1) * HEAD_DIM)
        ci, cf = off + h, 2 * N_HEADS + off + h
        bc, br = b_c[:, cf:cf + 1], b_r[cf:cf + 1, :]
        ic, ir = pre_c[:, ci:ci + 1], pre_r[ci:ci + 1, :]
        m_prev = m_ref[h:h + 1, 0:1]
        logw = jnp.where(mask, bc - br + ir, NEG)
        inter = bc + m_prev
        m_t = jnp.maximum(inter, jnp.max(logw, axis=1, keepdims=True))
        w_inter = jnp.exp(inter - m_t)
        q = _rope(q_ref[:, hs], cos, sin) * QK_SCALE
        k = _rope(k_ref[:, hs], cos, sin)
        qb, kb, vb = q.astype(BF16), k.astype(BF16), v_ref[:, hs].astype(BF16)
        s = _nt(qb, kb) * jnp.exp(logw - m_t)
        c_old = c_ref[h]
        n_old = n_ref[h:h + 1, :]
        num = _nn(s.astype(BF16), vb) + _nn(qb, c_old.astype(BF16)) * w_inter
        den = jnp.sum(s, axis=1, keepdims=True) + jnp.sum(q * n_old, axis=1, keepdims=True) * w_inter
        den = jnp.maximum(jnp.abs(den), jnp.exp(-m_t))
        hout = num / den
        m_new = m_t[last:last + 1, :]
        b_last = bc[last:last + 1, :]
        w_old = jnp.exp(b_last + m_prev - m_new)
        kw = k * jnp.exp(b_last - bc + ic - m_new)
        c_ref[h] = w_old * c_old + _tn(kw.astype(BF16), vb)
        n_ref[h:h + 1, :] = w_old * n_old + jnp.sum(kw, axis=0, keepdims=True)
        m_ref[h:h + 1, :] = jnp.broadcast_to(m_new, (1, HEAD_DIM))
        if rev:
            out_ref[:, hs] = _head_norm(hf_ref[:, hs] + hout) * _sigmoid(o_ref[:, hs])
        else:
            out_ref[:, hs] = hout


def _mlstm(proj, gates, gates_t, cos_t, sin_t, igate_b, fgate_b, n_lat_chunks):
    t_all = proj.shape[0]
    n = SCAN_CHUNK
    n_chunks = t_all // n
    bias = jnp.zeros((1, 128), F32).at[0, :4 * N_HEADS].set(
        jnp.concatenate([igate_b.reshape(-1), fgate_b.reshape(-1)]))

    def call(rev, extra_in, extra_args):
        blk = _scan_block(rev, n_chunks, n_lat_chunks)
        colblk = lambda c: pl.BlockSpec((n, HALF), lambda s, c=c: (blk(s), c))
        tok = pl.BlockSpec((n, 128), lambda s: (blk(s), 0))
        return pl.pallas_call(
            functools.partial(_mlstm_kernel, rev=rev),
            grid=(n_chunks,),
            in_specs=[colblk(0), colblk(1), colblk(2), tok,
                      pl.BlockSpec((128, n), lambda s: (0, blk(s))), tok, tok,
                      pl.BlockSpec((1, 128), lambda s: (0, 0)),
                      pl.BlockSpec((128, 1), lambda s: (0, 0))] + extra_in,
            out_specs=pl.BlockSpec((n, HALF), lambda s: (blk(s), 0)),
            out_shape=jax.ShapeDtypeStruct((t_all, HALF), F32),
            scratch_shapes=[pltpu.VMEM((N_HEADS, HEAD_DIM, HEAD_DIM), F32),
                            pltpu.VMEM((N_HEADS, HEAD_DIM), F32),
                            pltpu.VMEM((N_HEADS, HEAD_DIM), F32)],
            compiler_params=_cparams(1),
        )(proj, proj, proj, gates, gates_t, cos_t, sin_t, bias, bias.reshape(128, 1), *extra_args)

    h_fw = call(False, [], [])
    blk_r = _scan_block(True, n_chunks, n_lat_chunks)
    return call(True, [pl.BlockSpec((n, HALF), lambda s: (blk_r(s), 0)),
                       pl.BlockSpec((n, HALF), lambda s: (blk_r(s), 3))], [h_fw, proj])


def _na_kernel(q_ref, kp_ref, kc_ref, kn_ref, vp_ref, vc_ref, vn_ref, kx_ref, vx_ref, bias_ref, o_ref):
    qb = (q_ref[...] * QK_SCALE).astype(BF16)
    kk = jnp.concatenate([kp_ref[...], kc_ref[...], kn_ref[...]], axis=0).astype(BF16)
    vv = jnp.concatenate([vp_ref[...], vc_ref[...], vn_ref[...]], axis=0).astype(BF16)
    s_loc = _nt(qb, kk) + bias_ref[...]
    s_ctx = _nt(qb, kx_ref[...].astype(BF16))
    m = jnp.maximum(jnp.max(s_loc, axis=1, keepdims=True), jnp.max(s_ctx, axis=1, keepdims=True))
    p_loc = jnp.exp(s_loc - m)
    p_ctx = jnp.exp(s_ctx - m)
    l = jnp.sum(p_loc, axis=1, keepdims=True) + jnp.sum(p_ctx, axis=1, keepdims=True)
    o_ref[...] = (_nn(p_loc.astype(BF16), vv) + _nn(p_ctx.astype(BF16), vx_ref[...].astype(BF16))) / l


def _na_bias(rpb, rows):
    qn = NA_QROWS * GRID_W
    kn = 2 * qn
    ql = np.arange(qn)
    kl = np.arange(kn)
    out = []
    nqb = rows // NA_QROWS
    lr, krl = np.arange(NA_QROWS)[:, None], np.arange(2 * NA_QROWS)[None, :]
    dr = np.clip(krl - NA_QROWS // 2 - lr + NA_WIN_ROWS - 1, 0, 2 * NA_WIN_ROWS - 2)
    col = np.arange(GRID_W)
    dc = np.clip(col[None, :] - col[:, None] + NA_WIN_COLS - 1, 0, 2 * NA_WIN_COLS - 2)
    oh_r = jnp.asarray(dr[..., None] == np.arange(2 * NA_WIN_ROWS - 1), F32)
    oh_c = jnp.asarray(dc[..., None] == np.arange(2 * NA_WIN_COLS - 1), F32)
    table = jnp.einsum('hab,lka,qcb->hlqkc', rpb, oh_r, oh_c,
                       precision=lax.Precision.HIGHEST).reshape(rpb.shape[0], qn, kn)
    for qb in (0, 1, nqb - 1):
        qr = NA_QROWS * qb + ql // GRID_W
        qc = ql % GRID_W
        kr = NA_QROWS * qb - NA_QROWS // 2 + kl // GRID_W
        kc = kl % GRID_W
        r0 = np.clip(qr - NA_WIN_ROWS // 2, 0, rows - NA_WIN_ROWS)
        c0 = np.clip(qc - NA_WIN_COLS // 2, 0, GRID_W - NA_WIN_COLS)
        ok = ((kr[None, :] >= r0[:, None]) & (kr[None, :] < r0[:, None] + NA_WIN_ROWS)
              & (kc[None, :] >= c0[:, None]) & (kc[None, :] < c0[:, None] + NA_WIN_COLS)
              & (kr[None, :] >= 0) & (kr[None, :] < rows))
        out.append(jnp.where(jnp.asarray(ok)[None], table, NEG))
    return jnp.stack(out)


def _na(proj, rpb, t_lat, t_ctx):
    rows = t_lat // GRID_W
    qn = NA_QROWS * GRID_W
    hn = qn // 2
    nqb = rows // NA_QROWS
    assert nqb >= 2 and t_ctx == hn
    bias = _na_bias(rpb.astype(F32), rows)
    last_half = 2 * nqb - 1
    ctx_blk = t_lat // hn
    cls = lambda b: jnp.where(b == 0, 0, jnp.where(b == nqb - 1, 2, 1))

    def kv_specs(col0):
        return [pl.BlockSpec((hn, HEAD_DIM), lambda h, b: (jnp.maximum(2 * b - 1, 0), col0 + h)),
                pl.BlockSpec((qn, HEAD_DIM), lambda h, b: (b, col0 + h)),
                pl.BlockSpec((hn, HEAD_DIM), lambda h, b: (jnp.minimum(2 * b + 2, last_half), col0 + h))]

    kcol, vcol = 5 * N_HEADS, 6 * N_HEADS
    return pl.pallas_call(
        _na_kernel,
        grid=(N_HEADS, nqb),
        in_specs=[pl.BlockSpec((qn, HEAD_DIM), lambda h, b: (b, 4 * N_HEADS + h))]
        + kv_specs(kcol) + kv_specs(vcol)
        + [pl.BlockSpec((hn, HEAD_DIM), lambda h, b: (ctx_blk, kcol + h)),
           pl.BlockSpec((hn, HEAD_DIM), lambda h, b: (ctx_blk, vcol + h)),
           pl.BlockSpec((None, None, qn, 2 * qn), lambda h, b: (cls(b), h, 0, 0))],
        out_specs=pl.BlockSpec((qn, HEAD_DIM), lambda h, b: (b, h)),
        out_shape=jax.ShapeDtypeStruct((t_lat, HALF), F32),
        compiler_params=_cparams(2),
    )(proj, proj, proj, proj, proj, proj, proj, proj, proj, bias)


def _ctx_attn_kernel(q_ref, k_ref, v_ref, o_ref):
    qb = (q_ref[...] * QK_SCALE).astype(BF16)
    s = _nt(qb, k_ref[...].astype(BF16))
    p = jnp.exp(s - jnp.max(s, axis=1, keepdims=True))
    o_ref[...] = _nn(p.astype(BF16), v_ref[...].astype(BF16)) / jnp.sum(p, axis=1, keepdims=True)


def _ctx_attn(proj, t_lat, t_ctx):
    blk = t_lat // t_ctx
    spec = lambda c: pl.BlockSpec((t_ctx, HEAD_DIM), lambda h, c=c: (blk, c * N_HEADS + h))
    return pl.pallas_call(
        _ctx_attn_kernel,
        grid=(N_HEADS,),
        in_specs=[spec(4), spec(5), spec(6)],
        out_specs=pl.BlockSpec((t_ctx, HEAD_DIM), lambda h: (0, h)),
        out_shape=jax.ShapeDtypeStruct((t_ctx, HALF), F32),
        compiler_params=_cparams(1),
    )(proj, proj, proj)


def _gla_kernel(*refs, rev):
    if rev:
        q_ref, i_ref, f_ref, lb_ref, of_ref, g_ref, out_ref, st_ref, o_scr, k_scr, b_scr = refs
    else:
        q_ref, i_ref, f_ref, lb_ref, out_ref, st_ref, o_scr, k_scr, b_scr = refs
    n = q_ref.shape[0]

    @pl.when(pl.program_id(0) == 0)
    def _():
        st_ref[...] = jnp.zeros_like(st_ref)

    mask = _scan_mask(rev, n)
    tri = mask.astype(BF16)
    last = 0 if rev else n - 1
    x = f_ref[...]
    key = (1.0 - lb_ref[...]) * _sigmoid(-x)
    b = sum(_nn(tri, p) for p in _split3(jnp.log1p(-key)))
    b_end = b[last:last + 1, :]
    qx = q_ref[...]
    q = qx * _sigmoid(qx)
    qt = q * jnp.exp(b)
    kdec = key * jnp.exp(b_end - b)
    w_end = jnp.exp(b_end)
    k_scr[...] = key
    b_scr[...] = b

    for h in range(N_HEADS):
        hs = slice(h * HEAD_DIM, (h + 1) * HEAD_DIM)
        st = st_ref[h]
        vb = i_ref[:, hs].astype(BF16)
        o_scr[:, hs] = _nt(qt[:, hs].astype(BF16), st.astype(BF16))
        st_ref[h] = st * w_end[:, hs] + _tn(vb, kdec[:, hs].astype(BF16))

    safe = jnp.min(b_end) >= 2.0 * GLA_SAFE_LOG

    @pl.when(safe)
    def _():
        mid = 0.5 * b_end
        qm = q * jnp.exp(b_scr[...] - mid)
        km = k_scr[...] * jnp.exp(mid - b_scr[...])
        for h in range(N_HEADS):
            hs = slice(h * HEAD_DIM, (h + 1) * HEAD_DIM)
            att = jnp.where(mask, _nt(qm[:, hs].astype(BF16), km[:, hs].astype(BF16)), 0.0)
            o_scr[:, hs] += _nn(att.astype(BF16), i_ref[:, hs].astype(BF16))

    @pl.when(jnp.logical_not(safe))
    def _():
        t_idx = lax.broadcasted_iota(jnp.int32, (n, 1), 0)
        for h in range(N_HEADS):
            hs = slice(h * HEAD_DIM, (h + 1) * HEAD_DIM)
            q_h = q[:, hs]
            b_h = b[:, hs]

            def body(grp, acc):
                base = pl.multiple_of(grp * 8, 8)
                b8, k8, v8 = b_scr[pl.ds(base, 8), hs], k_scr[pl.ds(base, 8), hs], i_ref[pl.ds(base, 8), hs]
                for r in range(8):
                    s = base + r
                    e = jnp.exp(jnp.minimum(b_h - b8[r:r + 1, :], 0.0))
                    a = jnp.sum(q_h * k8[r:r + 1, :] * e, axis=1, keepdims=True)
                    a = jnp.where((t_idx <= s) if rev else (t_idx >= s), a, 0.0)
                    acc = acc + a * v8[r:r + 1, :]
                return acc

            o_scr[:, hs] += lax.fori_loop(0, n // 8, body, jnp.zeros((n, HEAD_DIM), F32))

    if rev:
        for h in range(N_HEADS):
            hs = slice(h * HEAD_DIM, (h + 1) * HEAD_DIM)
            gx = g_ref[:, hs]
            out_ref[:, hs] = _head_norm(of_ref[:, hs] + o_scr[:, hs]) * (gx * _sigmoid(gx))
    else:
        out_ref[...] = o_scr[...]


def _gla(proj, lb, n_lat_chunks):
    t_all = proj.shape[0]
    n = GLA_CHUNK
    n_chunks = t_all // n

    def call(rev, extra_in, extra_args):
        blk = _scan_block(rev, n_chunks, n_lat_chunks)
        colblk = lambda c: pl.BlockSpec((n, HALF), lambda s, c=c: (blk(s), c))
        return pl.pallas_call(
            functools.partial(_gla_kernel, rev=rev),
            grid=(n_chunks,),
            in_specs=[colblk(0), colblk(1), colblk(3 if rev else 2),
                      pl.BlockSpec((1, HALF), lambda s: (0, 0))] + extra_in,
            out_specs=pl.BlockSpec((n, HALF), lambda s: (blk(s), 0)),
            out_shape=jax.ShapeDtypeStruct((t_all, HALF), F32),
            scratch_shapes=[pltpu.VMEM((N_HEADS, HEAD_DIM, HEAD_DIM), F32),
                            pltpu.VMEM((n, HALF), F32), pltpu.VMEM((n, HALF), F32),
                            pltpu.VMEM((n, HALF), F32)],
            compiler_params=_cparams(1),
        )(proj, proj, proj, lb.reshape(1, HALF), *extra_args)

    o_fw = call(False, [], [])
    blk_r = _scan_block(True, n_chunks, n_lat_chunks)
    return call(True, [pl.BlockSpec((n, HALF), lambda s: (blk_r(s), 0)),
                       pl.BlockSpec((n, HALF), lambda s: (blk_r(s), 4))], [o_fw, proj])


def _s5_mats(a_re, a_im, log_dt, b_re, b_im, c_re, c_im):
    hp = lax.Precision.HIGHEST
    n = S5_CHUNK
    dt = jnp.exp(log_dt)[..., None]
    la_re, la_im = a_re * dt, a_im * dt
    mag = jnp.exp(la_re)
    ab_re, ab_im = mag * jnp.cos(la_im), mag * jnp.sin(la_im)
    nr, ni = ab_re - 1.0, ab_im
    den = jnp.square(a_re) + jnp.square(a_im)
    cr = (nr * a_re + ni * a_im) / den
    ci = (ni * a_re - nr * a_im) / den
    bb_re = cr[..., None] * b_re - ci[..., None] * b_im
    bb_im = cr[..., None] * b_im + ci[..., None] * b_re

    def apow(tau):
        tau = jnp.asarray(tau, F32)[None, None, :, None]
        m = jnp.exp(tau * la_re[:, :, None, :])
        return m * jnp.cos(tau * la_im[:, :, None, :]), m * jnp.sin(tau * la_im[:, :, None, :])

    def c_apow(tau):
        pr, pi = apow(tau)
        return (c_re[:, :, None] * pr[:, :, :, None] - c_im[:, :, None] * pi[:, :, :, None],
                c_re[:, :, None] * pi[:, :, :, None] + c_im[:, :, None] * pr[:, :, :, None])

    ca_re, ca_im = c_apow(np.arange(n))
    kern = (jnp.einsum('dgtip,dgpj->dgtij', ca_re, bb_re, precision=hp)
            - jnp.einsum('dgtip,dgpj->dgtij', ca_im, bb_im, precision=hp))
    s_idx, t_idx = np.arange(n)[:, None], np.arange(n)[None, :]
    g = a_re.shape[1]
    lp = n * S5_GROUP

    def toeplitz(kd, lag):
        onehot = jnp.asarray(lag[None] == np.arange(n)[:, None, None], F32)
        return jnp.einsum('lst,glij->gsjti', onehot, kd, precision=hp).reshape(g, lp, lp)

    kmat = toeplitz(kern[0], t_idx - s_idx) + toeplitz(kern[1], s_idx - t_idx)

    def emat(d, tau):
        pr, pi = apow(tau)
        pr, pi = pr[d][:, :, None, :], pi[d][:, :, None, :]
        br, bi = bb_re[d].transpose(0, 2, 1)[:, None], bb_im[d].transpose(0, 2, 1)[:, None]
        return jnp.concatenate([pr * br - pi * bi, pr * bi + pi * br], axis=-1).reshape(g, lp, 2 * S5_STATE)

    def fmat(d, tau):
        fr, fi = c_apow(tau)
        fr, fi = fr[d], fi[d]
        return jnp.concatenate([fr, -fi], axis=-1).transpose(0, 3, 1, 2).reshape(g, 2 * S5_STATE, lp)

    e_all = jnp.concatenate([emat(0, n - 1 - np.arange(n)), emat(1, np.arange(n))], axis=-1)
    f_all = jnp.concatenate([fmat(0, np.arange(n) + 1), fmat(1, n - np.arange(n))], axis=1)
    pr, pi = apow(np.array([n]))
    pr, pi = pr[:, :, 0], pi[:, :, 0]
    ar2 = jnp.concatenate([pr[0], pr[0], pr[1], pr[1]], axis=-1)
    ai2 = jnp.concatenate([-pi[0], pi[0], -pi[1], pi[1]], axis=-1)
    return kmat.astype(BF16), e_all.astype(BF16), f_all.astype(BF16), ar2, ai2


def _s5_state_kernel(u_ref, e_ref, s_ref):
    s_ref[...] = _nn(u_ref[...], e_ref[...])


def _s5_scan_kernel(sf_ref, sb_ref, ar_ref, ai_ref, hf_ref, hb_ref, stf_ref, stb_ref):
    nb = sf_ref.shape[0]
    w = 2 * S5_STATE
    ar, ai = ar_ref[...], ai_ref[...]

    @pl.when(pl.program_id(0) == 0)
    def _():
        stf_ref[...] = jnp.zeros_like(stf_ref)
        stb_ref[...] = jnp.zeros_like(stb_ref)

    def step(h, s, lo):
        return ar[:, lo:lo + w] * h + ai[:, lo:lo + w] * pltpu.roll(h, S5_STATE, 1) + s

    hf, hb = stf_ref[...], stb_ref[...]
    for j in range(nb):
        hf_ref[j] = hf
        hf = step(hf, sf_ref[j], 0)
        hb_ref[nb - 1 - j] = hb
        hb = step(hb, sb_ref[nb - 1 - j], w)
    stf_ref[...] = hf
    stb_ref[...] = hb


def _s5_out_kernel(u_ref, k_ref, h_ref, f_ref, y_ref):
    y_ref[...] = _nn(u_ref[...], k_ref[...]) + _nn(h_ref[...].astype(BF16), f_ref[...])


def _s5(u, mats, n_lat_chunks):
    kmat, emat, fmat, ar2, ai2 = mats
    t_all = u.shape[0]
    n = S5_CHUNK
    nc = t_all // n
    g = S5_GROUPS
    lp = n * S5_GROUP
    w4 = 4 * S5_STATE
    ug = u.astype(BF16).reshape(nc, n, g, S5_GROUP).transpose(2, 0, 1, 3).reshape(g, nc, lp)
    grp = lambda a, b: pl.BlockSpec((None, a, b), lambda i: (i, 0, 0))
    s_end = pl.pallas_call(
        _s5_state_kernel, grid=(g,),
        in_specs=[grp(nc, lp), grp(lp, w4)], out_specs=grp(nc, w4),
        out_shape=jax.ShapeDtypeStruct((g, nc, w4), F32), compiler_params=_cparams(1),
    )(ug, emat)
    nb = S5_SCAN_BLOCK
    w2 = 2 * S5_STATE
    fw_blk = _scan_block(False, nc // nb, n_lat_chunks // nb)
    bw_blk = _scan_block(True, nc // nb, n_lat_chunks // nb)
    s_t = s_end.transpose(1, 0, 2)
    coef = pl.BlockSpec((g, w4), lambda i: (0, 0))
    h_fw, h_bw = pl.pallas_call(
        _s5_scan_kernel, grid=(nc // nb,),
        in_specs=[pl.BlockSpec((nb, g, w2), lambda i: (fw_blk(i), 0, 0)),
                  pl.BlockSpec((nb, g, w2), lambda i: (bw_blk(i), 0, 1)), coef, coef],
        out_specs=[pl.BlockSpec((nb, g, w2), lambda i: (fw_blk(i), 0, 0)),
                   pl.BlockSpec((nb, g, w2), lambda i: (bw_blk(i), 0, 0))],
        out_shape=[jax.ShapeDtypeStruct((nc, g, w2), F32)] * 2,
        scratch_shapes=[pltpu.VMEM((g, w2), F32)] * 2,
        compiler_params=_cparams(1),
    )(s_t, s_t, ar2, ai2)
    h_in = jnp.concatenate([h_fw, h_bw], axis=-1).transpose(1, 0, 2)
    y = pl.pallas_call(
        _s5_out_kernel, grid=(g,),
        in_specs=[grp(nc, lp), grp(lp, lp), grp(nc, w4), grp(w4, lp)], out_specs=grp(nc, lp),
        out_shape=jax.ShapeDtypeStruct((g, nc, lp), F32), compiler_params=_cparams(1),
    )(ug, kmat, h_in, fmat)
    return y.reshape(g, nc, n, S5_GROUP).transpose(1, 2, 0, 3).reshape(t_all, HALF)


def _top_rows(x, k):
    rows = []
    for _ in range(k):
        mx = jnp.max(x, axis=0, keepdims=True)
        rows.append(mx)
        x = jnp.where(x == mx, NEG, x)
    return rows


def _peer_route_kernel(ht_ref, wq_ref, keys_ref, e1_ref, e0_ref, th_ref, top_scr):
    kk = PEER_TOPK
    qt = _nn(wq_ref[...], ht_ref[...])
    for hp in range(2 * PEER_HEADS):
        sc = _nn(keys_ref[hp], qt[hp * HEAD_DIM:(hp + 1) * HEAD_DIM, :].astype(BF16))
        h, p = divmod(hp, 2)
        if p == 0:
            e0_ref[h] = sc
        else:
            e1_ref[h] = sc
        for r, mx in enumerate(_top_rows(sc, kk + 1)):
            top_scr[hp, r:r + 1, :] = mx
        top_scr[hp, kk + 1:, :] = jnp.full((PEER_TOP_ROWS - kk - 1, sc.shape[1]), NEG, F32)

    row = lax.broadcasted_iota(jnp.int32, (PEER_TOP_ROWS, 1), 0)
    for h in range(PEER_HEADS):
        a = top_scr[2 * h]
        b = top_scr[2 * h + 1]
        parts = [a[0:1] + b, a[1:2] + b[0:8], a[2:3] + b[0:8], a[3:4] + b[0:8]]
        parts += [jnp.where(row >= 4, a + b[j:j + 1], NEG) for j in range(3)]
        cand = jnp.maximum(jnp.concatenate(parts, axis=0), NEG)
        top = _top_rows(cand, kk + 1)
        thr = 0.5 * (top[kk - 1] + top[kk])
        z = jnp.sum(jnp.where(cand >= thr, jnp.exp(cand - top[0]), 0.0), axis=0, keepdims=True)
        e0_ref[h] = jnp.exp(e0_ref[h] - a[0:1]) / z
        e1_ref[h] = jnp.exp(e1_ref[h] - b[0:1])
        th_ref[h:h + 1, :] = jnp.exp(thr - top[0]) / z


def _peer_expert_kernel(ht_ref, e1_ref, e0_ref, th_ref, *rest):
    u_refs = rest[:PEER_U_SPLIT]
    vt_ref, o_ref, at0_scr, at1_scr, g0_scr, g1_scr = rest[PEER_U_SPLIT:]
    s = pl.program_id(1)
    n_tiles = pl.num_programs(1) - 2
    ni = u_refs[0].shape[0] // PEER_N_KEYS
    kc = u_refs[0].shape[1]

    @pl.when(s == 0)
    def _():
        o_ref[...] = jnp.zeros_like(o_ref)
        for scr in (at0_scr, at1_scr, g0_scr, g1_scr):
            scr[...] = jnp.zeros_like(scr)

    tile = jnp.clip(s - 1, 0, n_tiles - 1)

    def stages(at_new, at_old, g_new, g_old):
        at_new[...] = sum(_nn(u_k[...], ht_ref[k * kc:(k + 1) * kc, :]) for k, u_k in enumerate(u_refs))
        nj, nc = PEER_GATE_ROWS, 128
        for ii in range(ni):
            i = tile * ni + ii
            e0_rows = [e0_ref[h, pl.ds(i, 1), :] for h in range(PEER_HEADS)]
            for c in range(at_old.shape[1] // nc):
                cs = slice(c * nc, (c + 1) * nc)
                for jb in range(PEER_N_KEYS // nj):
                    js = slice(jb * nj, (jb + 1) * nj)
                    w = None
                    for h in range(PEER_HEADS):
                        p = e1_ref[h, js, cs] * e0_rows[h][:, cs]
                        t = jnp.where(p >= th_ref[h:h + 1, cs], p, 0.0)
                        w = t if w is None else w + t
                    rs = slice(ii * PEER_N_KEYS + jb * nj, ii * PEER_N_KEYS + (jb + 1) * nj)
                    g_new[rs, cs] = (_gelu_tanh(at_old[rs, cs]) * w).astype(BF16)
        o_ref[...] += _nn(vt_ref[...], g_old[...])

    @pl.when(s % 2 == 0)
    def _():
        stages(at0_scr, at1_scr, g1_scr, g0_scr)

    @pl.when(s % 2 == 1)
    def _():
        stages(at1_scr, at0_scr, g0_scr, g1_scr)


def _tok3(t):
    return pl.BlockSpec((PEER_HEADS, PEER_N_KEYS, t), lambda i, *_: (0, 0, i))


def _peer_route(h2t, wq_t_bf, keys_bf):
    d, t_all = h2t.shape
    nh, nk = PEER_HEADS, PEER_N_KEYS
    t1 = PEER_P1_TILE
    return pl.pallas_call(
        _peer_route_kernel,
        grid=(t_all // t1,),
        in_specs=[pl.BlockSpec((d, t1), lambda i: (0, i)),
                  pl.BlockSpec(wq_t_bf.shape, lambda i: (0, 0)),
                  pl.BlockSpec(keys_bf.shape, lambda i: (0, 0, 0))],
        out_specs=[_tok3(t1), _tok3(t1), pl.BlockSpec((nh, t1), lambda i: (0, i))],
        out_shape=[jax.ShapeDtypeStruct((nh, nk, t_all), F32)] * 2 + [jax.ShapeDtypeStruct((nh, t_all), F32)],
        scratch_shapes=[pltpu.VMEM((2 * nh, PEER_TOP_ROWS, t1), F32)],
        compiler_params=_cparams(1),
    )(h2t, wq_t_bf, keys_bf)


def _peer(h2t, wq_t_bf, keys_bf, u_bf, vt_bf):
    return _peer_experts(h2t, *_peer_route(h2t, wq_t_bf, keys_bf), u_bf, vt_bf)


def _peer_experts(h2t, e1, e0, th, u_bf, vt_bf):
    d, t_all = h2t.shape
    tt = PEER_TOK_TILE if t_all % PEER_TOK_TILE == 0 else PEER_P1_TILE
    et = PEER_EXP_TILE
    n_tiles = u_bf.shape[0] // et
    tok = pl.BlockSpec((d, tt), lambda i, s: (0, i))
    return pl.pallas_call(
        _peer_expert_kernel,
        grid=(t_all // tt, n_tiles + 2),
        in_specs=[tok, _tok3(tt), _tok3(tt), pl.BlockSpec((PEER_HEADS, tt), lambda i, s: (0, i))]
        + [pl.BlockSpec((et, d // PEER_U_SPLIT), lambda i, s, k=k: (jnp.minimum(s, n_tiles - 1), k))
           for k in range(PEER_U_SPLIT)]
        + [pl.BlockSpec((d, et), lambda i, s: (0, jnp.clip(s - 2, 0, n_tiles - 1)))],
        out_specs=tok,
        out_shape=jax.ShapeDtypeStruct((d, t_all), F32),
        scratch_shapes=[pltpu.VMEM((et, tt), F32)] * 2 + [pltpu.VMEM((et, tt), BF16)] * 2,
        compiler_params=_cparams(2),
    )(h2t, e1, e0, th, *([u_bf] * PEER_U_SPLIT), vt_bf)


def _rope_tables(t_lat, t_ctx):
    pos = np.arange(t_lat)
    nfreq = HEAD_DIM // 4
    freqs = ROPE_BASE ** (-jnp.arange(nfreq, dtype=F32) / nfreq)
    ang_r = jnp.asarray(pos // GRID_W, F32)[:, None] * freqs
    ang_c = jnp.asarray(pos % GRID_W, F32)[:, None] * freqs
    cos = jnp.concatenate([jnp.cos(ang_r), jnp.cos(ang_r), jnp.cos(ang_c), jnp.cos(ang_c)], axis=1)
    sin = jnp.concatenate([-jnp.sin(ang_r), jnp.sin(ang_r), -jnp.sin(ang_c), jnp.sin(ang_c)], axis=1)
    cos = jnp.concatenate([cos, jnp.ones((t_ctx, HEAD_DIM), F32)], axis=0)
    sin = jnp.concatenate([sin, jnp.zeros((t_ctx, HEAD_DIM), F32)], axis=0)
    return cos, sin


def _peer_and_norm(x1, h2, mods, ln_g, ln_b, n_lat_tiles, w_q, sub_keys, u_tab, v_tab, mods_next=None):
    keys = sub_keys.reshape(2 * PEER_HEADS, PEER_N_KEYS, -1).astype(BF16)
    pt = _peer(h2.T, w_q.T.astype(BF16), keys, u_tab.astype(BF16), v_tab.T.astype(BF16))
    return _ln2(x1, pt, mods, ln_g, ln_b, n_lat_tiles, mods_next)


def _layer_ab(x_all, mods, t_lat, t_ctx, w_in, w_out, igate_b, fgate_b, rpb, ln_g, ln_b, xm=None):
    n_lat_tiles = t_lat // ROW_TILE
    if xm is None:
        xm = _modulate(x_all, mods, n_lat_tiles)
    ng = 4 * N_HEADS
    w_main = jnp.concatenate([w_in[:, :4 * HALF], w_in[:, 4 * HALF + ng:]], axis=1).astype(BF16)
    w_gate = jnp.pad(w_in[:, 4 * HALF:4 * HALF + ng], ((0, 0), (0, 128 - ng))).astype(BF16)
    proj = _inproj(xm, w_main)
    gates = _inproj(xm, w_gate)
    cos_t, sin_t = _rope_tables(t_lat, t_ctx)
    y_m = _mlstm(proj, gates, gates.T, cos_t, sin_t, igate_b, fgate_b, t_lat // SCAN_CHUNK)
    y_na = jnp.concatenate([_na(proj, rpb, t_lat, t_ctx), _ctx_attn(proj, t_lat, t_ctx)], axis=0)
    return _outproj(y_m, [(y_na, 0)], w_out.astype(BF16), x_all, mods, ln_g, ln_b, n_lat_tiles)


def _layer_cd(x_all, mods, t_lat, w_in, w_out, lb, s5p, s5_d, glu_w, glu_b, ln_g, ln_b, xm=None):
    n_lat_tiles = t_lat // ROW_TILE
    if xm is None:
        xm = _modulate(x_all, mods, n_lat_tiles)
    proj = _inproj(xm, w_in.astype(BF16))
    y_h = _gla(proj, lb, t_lat // GLA_CHUNK)
    y5 = _s5(proj[:, 5 * HALF:], _s5_mats(*[p.astype(F32) for p in s5p]), t_lat // S5_CHUNK)
    return _outproj(y_h, [(y5, 0), (proj, 5)], w_out.astype(BF16), x_all, mods, ln_g, ln_b, n_lat_tiles,
                    glu_params=(s5_d.astype(F32), glu_w.astype(BF16), glu_b.astype(F32)))


def kernel(x, c, ctx, c_ctx, ada_w, ada_b, ln_g, ln_b, ab_w_in, ab_w_out, mlstm_igate_b, mlstm_fgate_b, na_rpb, cd_w_in, cd_w_out, hgrn_lb_logits, s5_a_re, s5_a_im, s5_log_dt, s5_b_re, s5_b_im, s5_c_re, s5_c_im, s5_d, s5_glu_w, s5_glu_b, peer_w_q, peer_sub_keys, peer_u, peer_v):
    t_lat, t_ctx = x.shape[1], ctx.shape[1]
    assert x.shape[0] == 1 and t_ctx == ROW_TILE and t_lat % (NA_QROWS * GRID_W) == 0
    depth = ada_w.shape[0]
    lb_soft = jax.nn.softmax(hgrn_lb_logits.astype(F32), axis=0)
    lower_bounds = jnp.cumsum(lb_soft, axis=0) - lb_soft[0]
    x_all = jnp.concatenate([x[0], ctx[0]], axis=0).astype(F32)
    n_lat_tiles = t_lat // ROW_TILE
    mods_all = [_adaln(c, c_ctx, ada_w[l], ada_b[l]) for l in range(depth)]
    xm = _modulate(x_all, mods_all[0], n_lat_tiles)
    for l in range(depth):
        j = l // 2
        mods = mods_all[l]
        if l % 2 == 0:
            x1, h2 = _layer_ab(x_all, mods, t_lat, t_ctx, ab_w_in[j], ab_w_out[j], mlstm_igate_b[j],
                               mlstm_fgate_b[j], na_rpb[j], ln_g[l, 0], ln_b[l, 0], xm=xm)
        else:
            s5p = (s5_a_re[j], s5_a_im[j], s5_log_dt[j], s5_b_re[j], s5_b_im[j], s5_c_re[j], s5_c_im[j])
            x1, h2 = _layer_cd(x_all, mods, t_lat, cd_w_in[j], cd_w_out[j], lower_bounds[l], s5p, s5_d[j],
                               s5_glu_w[j], s5_glu_b[j], ln_g[l, 0], ln_b[l, 0], xm=xm)
        peer_args = (x1, h2, mods, ln_g[l, 1], ln_b[l, 1], n_lat_tiles, peer_w_q[l], peer_sub_keys[l],
                     peer_u[l], peer_v[l])
        if l + 1 < depth:
            x_all, xm = _peer_and_norm(*peer_args, mods_next=mods_all[l + 1])
        else:
            x_all = _peer_and_norm(*peer_args)
    return x_all[:t_lat][None].astype(x.dtype)
```

```python
import functools
import math

import numpy as np
import jax
import jax.numpy as jnp
from jax import lax
from jax.experimental import pallas as pl
from jax.experimental.pallas import tpu as pltpu

F32 = jnp.float32
BF16 = jnp.bfloat16

HEAD_DIM = 128
GRID_W = 64
N_HEADS = 8
HALF = N_HEADS * HEAD_DIM
NA_WIN_ROWS = 8
NA_WIN_COLS = 16
ROPE_BASE = 10000.0
S5_GROUP = 16
S5_STATE = 64
S5_GROUPS = HALF // S5_GROUP
PEER_HEADS = 8
PEER_N_KEYS = 128
PEER_TOPK = 16
PEER_TOP_ROWS = 24
LN_EPS = 1e-5
DEPTH = 2
ALPHA = (2.0 * DEPTH) ** 0.25
QK_SCALE = HEAD_DIM ** -0.5

ROW_TILE = 256
MM_ROW_TILE = 1280
MM_COL_TILE = 1024
SCAN_CHUNK = 128
GLA_CHUNK = 64
S5_CHUNK = 32
S5_SCAN_BLOCK = 8
NA_QROWS = 8
PEER_P1_TILE = 256
PEER_TOK_TILE = 512
PEER_EXP_TILE = 512
PEER_GATE_ROWS = 64
PEER_U_SPLIT = 4
NEG = -1e30
GLA_SAFE_LOG = -80.0
VMEM_LIMIT = 56 * 1024 * 1024


def _cparams(n_axes):
    return pltpu.CompilerParams(dimension_semantics=("arbitrary",) * n_axes,
                                vmem_limit_bytes=VMEM_LIMIT)


def _nt(a, b):
    return lax.dot_general(a, b, (((1,), (1,)), ((), ())), preferred_element_type=F32)


def _tn(a, b):
    return lax.dot_general(a, b, (((0,), (0,)), ((), ())), preferred_element_type=F32)


def _nn(a, b):
    return jnp.dot(a, b, preferred_element_type=F32)


def _split3(x):
    hi = x.astype(BF16)
    r = x - hi.astype(F32)
    mid = r.astype(BF16)
    lo = (r - mid.astype(F32)).astype(BF16)
    return hi, mid, lo


def _log_sigmoid(x):
    return jnp.minimum(x, 0.0) - jnp.log1p(jnp.exp(-jnp.abs(x)))


def _sigmoid(x):
    return 1.0 / (1.0 + jnp.exp(-x))


def _gelu_tanh(x):
    return 0.5 * x * (1.0 + jnp.tanh(math.sqrt(2.0 / math.pi) * (x + 0.044715 * (x * x * x))))


def _head_norm(x):
    mu = jnp.mean(x, axis=-1, keepdims=True)
    xc = x - mu
    return xc * lax.rsqrt(jnp.mean(xc * xc, axis=-1, keepdims=True) + LN_EPS)


def _adaln_kernel(ct_ref, w_ref, b_ref, o_ref):
    ct = ct_ref[...]
    s = ct * _sigmoid(ct)
    w = w_ref[...]
    o_ref[0:1, :] = jnp.sum(s[:, 0:1] * w, axis=0, keepdims=True) + b_ref[...]
    o_ref[1:2, :] = jnp.sum(s[:, 1:2] * w, axis=0, keepdims=True) + b_ref[...]


def _adaln(c_lat, c_ctx, w, b):
    d = w.shape[0]
    n = w.shape[1]
    tn = n // 8
    ct = jnp.stack([c_lat.reshape(d), c_ctx.reshape(d)], axis=1)
    out = pl.pallas_call(
        _adaln_kernel,
        grid=(n // tn,),
        in_specs=[pl.BlockSpec((d, 2), lambda j: (0, 0)),
                  pl.BlockSpec((d, tn), lambda j: (0, j)),
                  pl.BlockSpec((1, tn), lambda j: (0, j))],
        out_specs=pl.BlockSpec((2, tn), lambda j: (0, j)),
        out_shape=jax.ShapeDtypeStruct((2, n), F32),
        compiler_params=_cparams(1),
    )(ct, w, b.reshape(1, n))
    return jnp.pad(out.reshape(2, 6, d), ((0, 0), (0, 2), (0, 0)))


def _modulated(x, m_ref):
    return (x * (1.0 + m_ref[1:2, :]) + m_ref[0:1, :]).astype(BF16)


def _modulate_kernel(x_ref, m_ref, o_ref):
    o_ref[...] = _modulated(x_ref[...], m_ref)


def _mods_spec(d, n_lat_tiles):
    return pl.BlockSpec((None, 8, d), lambda i: (jnp.where(i < n_lat_tiles, 0, 1), 0, 0))


def _modulate(x_all, mods, n_lat_tiles):
    t_all, d = x_all.shape
    row = pl.BlockSpec((ROW_TILE, d), lambda i: (i, 0))
    return pl.pallas_call(
        _modulate_kernel,
        grid=(t_all // ROW_TILE,),
        in_specs=[row, _mods_spec(d, n_lat_tiles)],
        out_specs=row,
        out_shape=jax.ShapeDtypeStruct((t_all, d), BF16),
        compiler_params=_cparams(1),
    )(x_all, mods)


def _matmul_kernel(x_ref, w_ref, o_ref):
    o_ref[...] = _nn(x_ref[...], w_ref[...])


def _inproj(xm, w_bf):
    t_all, d = xm.shape
    n = w_bf.shape[1]
    tm = MM_ROW_TILE if t_all % MM_ROW_TILE == 0 else ROW_TILE
    tn = min(n, MM_COL_TILE)
    return pl.pallas_call(
        _matmul_kernel,
        grid=(t_all // tm, n // tn),
        in_specs=[pl.BlockSpec((tm, d), lambda i, j: (i, 0)),
                  pl.BlockSpec((d, tn), lambda i, j: (0, j))],
        out_specs=pl.BlockSpec((tm, tn), lambda i, j: (i, j)),
        out_shape=jax.ShapeDtypeStruct((t_all, n), F32),
        compiler_params=_cparams(2),
    )(xm, w_bf)


def _ln_rows(z, g, b):
    mu = jnp.mean(z, axis=-1, keepdims=True)
    zc = z - mu
    return zc * lax.rsqrt(jnp.mean(zc * zc, axis=-1, keepdims=True) + LN_EPS) * g + b


def _outproj_kernel(*refs, glu):
    if glu:
        (a1_ref, y5_ref, u5_ref, ds_ref, gw_ref, gb_ref, w1_ref, w2_ref, x_ref, m_ref, g_ref, b_ref,
         x1_ref, h2_ref) = refs
        y = _gelu_tanh(y5_ref[...] + ds_ref[...] * u5_ref[...])
        a2 = y * _sigmoid(_nn(y.astype(BF16), gw_ref[...]) + gb_ref[...])
    else:
        a1_ref, a2_ref, w1_ref, w2_ref, x_ref, m_ref, g_ref, b_ref, x1_ref, h2_ref = refs
        a2 = a2_ref[...]
    y = _nn(a1_ref[...].astype(BF16), w1_ref[...]) + _nn(a2.astype(BF16), w2_ref[...])
    x1 = _ln_rows(ALPHA * x_ref[...] + m_ref[2:3, :] * y, g_ref[...], b_ref[...])
    x1_ref[...] = x1
    h2_ref[...] = (x1 * (1.0 + m_ref[4:5, :]) + m_ref[3:4, :]).astype(BF16)


def _outproj(a1, a2_parts, w_out_bf, x_all, mods, ln_g, ln_b, n_lat_tiles, glu_params=None):
    t_all, d = x_all.shape
    half = a1.shape[1]
    row = lambda i: (i, 0)
    const = lambda i: (0, 0)
    act = pl.BlockSpec((ROW_TILE, half), row)
    a2_specs = [pl.BlockSpec((ROW_TILE, half), lambda i, c=c: (i, c)) for _, c in a2_parts]
    a2_args = [a for a, _ in a2_parts]
    if glu_params is not None:
        d_skip, glu_w_bf, glu_b = glu_params
        a2_specs += [pl.BlockSpec((1, half), const), pl.BlockSpec((half, half), const),
                     pl.BlockSpec((1, half), const)]
        a2_args += [d_skip.reshape(1, half), glu_w_bf, glu_b.reshape(1, half)]
    return pl.pallas_call(
        functools.partial(_outproj_kernel, glu=glu_params is not None),
        grid=(t_all // ROW_TILE,),
        in_specs=[act] + a2_specs + [
            pl.BlockSpec((half, d), const), pl.BlockSpec((half, d), const),
            pl.BlockSpec((ROW_TILE, d), row),
            pl.BlockSpec((None, 8, d), lambda i: (jnp.where(i < n_lat_tiles, 0, 1), 0, 0)),
            pl.BlockSpec((1, d), const), pl.BlockSpec((1, d), const)],
        out_specs=[pl.BlockSpec((ROW_TILE, d), row), pl.BlockSpec((ROW_TILE, d), row)],
        out_shape=[jax.ShapeDtypeStruct((t_all, d), F32), jax.ShapeDtypeStruct((t_all, d), BF16)],
        compiler_params=_cparams(1),
    )(a1, *a2_args, w_out_bf[:half], w_out_bf[half:], x_all, mods, ln_g.reshape(1, d), ln_b.reshape(1, d))


def _ln2_kernel(x_ref, p_ref, m_ref, g_ref, b_ref, *rest):
    x2 = _ln_rows(ALPHA * x_ref[...] + m_ref[5:6, :] * p_ref[...].T, g_ref[...], b_ref[...])
    if len(rest) == 1:
        rest[0][...] = x2
    else:
        mn_ref, o_ref, xm_ref = rest
        o_ref[...] = x2
        xm_ref[...] = _modulated(x2, mn_ref)


def _ln2(x1, peer_out_t, mods, ln_g, ln_b, n_lat_tiles, mods_next=None):
    t_all, d = x1.shape
    row = pl.BlockSpec((ROW_TILE, d), lambda i: (i, 0))
    col = pl.BlockSpec((d, ROW_TILE), lambda i: (0, i))
    vec = pl.BlockSpec((1, d), lambda i: (0, 0))
    nxt = mods_next is not None
    return pl.pallas_call(
        _ln2_kernel,
        grid=(t_all // ROW_TILE,),
        in_specs=[row, col, _mods_spec(d, n_lat_tiles), vec, vec] + ([_mods_spec(d, n_lat_tiles)] if nxt else []),
        out_specs=[row, row] if nxt else row,
        out_shape=([jax.ShapeDtypeStruct((t_all, d), F32), jax.ShapeDtypeStruct((t_all, d), BF16)] if nxt
                   else jax.ShapeDtypeStruct((t_all, d), F32)),
        compiler_params=_cparams(1),
    )(x1, peer_out_t, mods, ln_g.reshape(1, d), ln_b.reshape(1, d), *([mods_next] if nxt else []))


def _scan_block(rev, n_chunks, n_lat_chunks):
    if rev:
        return lambda s: n_chunks - 1 - s
    n_ctx = n_chunks - n_lat_chunks
    return lambda s: jnp.where(s < n_ctx, s + n_lat_chunks, s - n_ctx)


def _scan_mask(rev, n):
    row = lax.broadcasted_iota(jnp.int32, (n, n), 0)
    col = lax.broadcasted_iota(jnp.int32, (n, n), 1)
    return (col >= row) if rev else (col <= row)


def _rope(x, cos, sin):
    lane = lax.broadcasted_iota(jnp.int32, x.shape, 1)
    partner = jnp.where((lane % 64) < 32, pltpu.roll(x, 96, 1), pltpu.roll(x, 32, 1))
    return x * cos + partner * sin


def _mlstm_kernel(*refs, rev):
    if rev:
        (q_ref, k_ref, v_ref, g_ref, gt_ref, cos_ref, sin_ref, bias_ref, biast_ref, hf_ref, o_ref,
         out_ref, c_ref, n_ref, m_ref) = refs
    else:
        (q_ref, k_ref, v_ref, g_ref, gt_ref, cos_ref, sin_ref, bias_ref, biast_ref,
         out_ref, c_ref, n_ref, m_ref) = refs
    n = q_ref.shape[0]

    @pl.when(pl.program_id(0) == 0)
    def _():
        c_ref[...] = jnp.zeros_like(c_ref)
        n_ref[...] = jnp.zeros_like(n_ref)
        m_ref[...] = jnp.zeros_like(m_ref)

    mask = _scan_mask(rev, n)
    tri = mask.astype(BF16)
    pre_c = g_ref[...] + bias_ref[...]
    pre_r = gt_ref[...] + biast_ref[...]
    b_c = sum(_nn(tri, p) for p in _split3(_log_sigmoid(pre_c)))
    b_r = sum(_nt(p, tri) for p in _split3(_log_sigmoid(pre_r)))
    cos = cos_ref[...]
    sin = sin_ref[...]
    last = 0 if rev else n - 1
    off = N_HEADS if rev else 0

    for h in range(N_HEADS):
        hs = slice(h * HEAD_DIM, (h + 1) * HEAD_DIM)
        ci, cf = off + h, 2 * N_HEADS + off + h
        bc, br = b_c[:, cf:cf + 1], b_r[cf:cf + 1, :]
        ic, ir = pre_c[:, ci:ci + 1], pre_r[ci:ci + 1, :]
        m_prev = m_ref[h:h + 1, 0:1]
        logw = jnp.where(mask, bc - br + ir, NEG)
        inter = bc + m_prev
        m_t = jnp.maximum(inter, jnp.max(logw, axis=1, keepdims=True))
        w_inter = jnp.exp(inter - m_t)
        q = _rope(q_ref[:, hs], cos, sin) * QK_SCALE
        k = _rope(k_ref[:, hs], cos, sin)
        qb, kb, vb = q.astype(BF16), k.astype(BF16), v_ref[:, hs].astype(BF16)
        s = _nt(qb, kb) * jnp.exp(logw - m_t)
        c_old = c_ref[h]
        n_old = n_ref[h:h + 1, :]
        num = _nn(s.astype(BF16), vb) + _nn(qb, c_old.astype(BF16)) * w_inter
        den = jnp.sum(s, axis=1, keepdims=True) + jnp.sum(q * n_old, axis=1, keepdims=True) * w_inter
        den = jnp.maximum(jnp.abs(den), jnp.exp(-m_t))
        hout = num / den
        m_new = m_t[last:last + 1, :]
        b_last = bc[last:last + 1, :]
        w_old = jnp.exp(b_last + m_prev - m_new)
        kw = k * jnp.exp(b_last - bc + ic - m_new)
        c_ref[h] = w_old * c_old + _tn(kw.astype(BF16), vb)
        n_ref[h:h + 1, :] = w_old * n_old + jnp.sum(kw, axis=0, keepdims=True)
        m_ref[h:h + 1, :] = jnp.broadcast_to(m_new, (1, HEAD_DIM))
        if rev:
            out_ref[:, hs] = _head_norm(hf_ref[:, hs] + hout) * _sigmoid(o_ref[:, hs])
        else:
            out_ref[:, hs] = hout


def _mlstm(proj, gates, gates_t, cos_t, sin_t, igate_b, fgate_b, n_lat_chunks):
    t_all = proj.shape[0]
    n = SCAN_CHUNK
    n_chunks = t_all // n
    bias = jnp.zeros((1, 128), F32).at[0, :4 * N_HEADS].set(
        jnp.concatenate([igate_b.reshape(-1), fgate_b.reshape(-1)]))

    def call(rev, extra_in, extra_args):
        blk = _scan_block(rev, n_chunks, n_lat_chunks)
        colblk = lambda c: pl.BlockSpec((n, HALF), lambda s, c=c: (blk(s), c))
        tok = pl.BlockSpec((n, 128), lambda s: (blk(s), 0))
        return pl.pallas_call(
            functools.partial(_mlstm_kernel, rev=rev),
            grid=(n_chunks,),
            in_specs=[colblk(0), colblk(1), colblk(2), tok,
                      pl.BlockSpec((128, n), lambda s: (0, blk(s))), tok, tok,
                      pl.BlockSpec((1, 128), lambda s: (0, 0)),
                      pl.BlockSpec((128, 1), lambda s: (0, 0))] + extra_in,
            out_specs=pl.BlockSpec((n, HALF), lambda s: (blk(s), 0)),
            out_shape=jax.ShapeDtypeStruct((t_all, HALF), F32),
            scratch_shapes=[pltpu.VMEM((N_HEADS, HEAD_DIM, HEAD_DIM), F32),
                            pltpu.VMEM((N_HEADS, HEAD_DIM), F32),
                            pltpu.VMEM((N_HEADS, HEAD_DIM), F32)],
            compiler_params=_cparams(1),
        )(proj, proj, proj, gates, gates_t, cos_t, sin_t, bias, bias.reshape(128, 1), *extra_args)

    h_fw = call(False, [], [])
    blk_r = _scan_block(True, n_chunks, n_lat_chunks)
    return call(True, [pl.BlockSpec((n, HALF), lambda s: (blk_r(s), 0)),
                       pl.BlockSpec((n, HALF), lambda s: (blk_r(s), 3))], [h_fw, proj])


def _na_kernel(q_ref, kp_ref, kc_ref, kn_ref, vp_ref, vc_ref, vn_ref, kx_ref, vx_ref, bias_ref, o_ref):
    qb = (q_ref[...] * QK_SCALE).astype(BF16)
    kk = jnp.concatenate([kp_ref[...], kc_ref[...], kn_ref[...]], axis=0).astype(BF16)
    vv = jnp.concatenate([vp_ref[...], vc_ref[...], vn_ref[...]], axis=0).astype(BF16)
    s_loc = _nt(qb, kk) + bias_ref[...]
    s_ctx = _nt(qb, kx_ref[...].astype(BF16))
    m = jnp.maximum(jnp.max(s_loc, axis=1, keepdims=True), jnp.max(s_ctx, axis=1, keepdims=True))
    p_loc = jnp.exp(s_loc - m)
    p_ctx = jnp.exp(s_ctx - m)
    l = jnp.sum(p_loc, axis=1, keepdims=True) + jnp.sum(p_ctx, axis=1, keepdims=True)
    o_ref[...] = (_nn(p_loc.astype(BF16), vv) + _nn(p_ctx.astype(BF16), vx_ref[...].astype(BF16))) / l


def _na_bias(rpb, rows):
    qn = NA_QROWS * GRID_W
    kn = 2 * qn
    ql = np.arange(qn)
    kl = np.arange(kn)
    out = []
    nqb = rows // NA_QROWS
    lr, krl = np.arange(NA_QROWS)[:, None], np.arange(2 * NA_QROWS)[None, :]
    dr = np.clip(krl - NA_QROWS // 2 - lr + NA_WIN_ROWS - 1, 0, 2 * NA_WIN_ROWS - 2)
    col = np.arange(GRID_W)
    dc = np.clip(col[None, :] - col[:, None] + NA_WIN_COLS - 1, 0, 2 * NA_WIN_COLS - 2)
    oh_r = jnp.asarray(dr[..., None] == np.arange(2 * NA_WIN_ROWS - 1), F32)
    oh_c = jnp.asarray(dc[..., None] == np.arange(2 * NA_WIN_COLS - 1), F32)
    table = jnp.einsum('hab,lka,qcb->hlqkc', rpb, oh_r, oh_c,
                       precision=lax.Precision.HIGHEST).reshape(rpb.shape[0], qn, kn)
    for qb in (0, 1, nqb - 1):
        qr = NA_QROWS * qb + ql // GRID_W
        qc = ql % GRID_W
        kr = NA_QROWS * qb - NA_QROWS // 2 + kl // GRID_W
        kc = kl % GRID_W
        r0 = np.clip(qr - NA_WIN_ROWS // 2, 0, rows - NA_WIN_ROWS)
        c0 = np.clip(qc - NA_WIN_COLS // 2, 0, GRID_W - NA_WIN_COLS)
        ok = ((kr[None, :] >= r0[:, None]) & (kr[None, :] < r0[:, None] + NA_WIN_ROWS)
              & (kc[None, :] >= c0[:, None]) & (kc[None, :] < c0[:, None] + NA_WIN_COLS)
              & (kr[None, :] >= 0) & (kr[None, :] < rows))
        out.append(jnp.where(jnp.asarray(ok)[None], table, NEG))
    return jnp.stack(out)


def _na(proj, rpb, t_lat, t_ctx):
    rows = t_lat // GRID_W
    qn = NA_QROWS * GRID_W
    hn = qn // 2
    nqb = rows // NA_QROWS
    assert nqb >= 2 and t_ctx == hn
    bias = _na_bias(rpb.astype(F32), rows)
    last_half = 2 * nqb - 1
    ctx_blk = t_lat // hn
    cls = lambda b: jnp.where(b == 0, 0, jnp.where(b == nqb - 1, 2, 1))

    def kv_specs(col0):
        return [pl.BlockSpec((hn, HEAD_DIM), lambda h, b: (jnp.maximum(2 * b - 1, 0), col0 + h)),
                pl.BlockSpec((qn, HEAD_DIM), lambda h, b: (b, col0 + h)),
                pl.BlockSpec((hn, HEAD_DIM), lambda h, b: (jnp.minimum(2 * b + 2, last_half), col0 + h))]

    kcol, vcol = 5 * N_HEADS, 6 * N_HEADS
    return pl.pallas_call(
        _na_kernel,
        grid=(N_HEADS, nqb),
        in_specs=[pl.BlockSpec((qn, HEAD_DIM), lambda h, b: (b, 4 * N_HEADS + h))]
        + kv_specs(kcol) + kv_specs(vcol)
        + [pl.BlockSpec((hn, HEAD_DIM), lambda h, b: (ctx_blk, kcol + h)),
           pl.BlockSpec((hn, HEAD_DIM), lambda h, b: (ctx_blk, vcol + h)),
           pl.BlockSpec((None, None, qn, 2 * qn), lambda h, b: (cls(b), h, 0, 0))],
        out_specs=pl.BlockSpec((qn, HEAD_DIM), lambda h, b: (b, h)),
        out_shape=jax.ShapeDtypeStruct((t_lat, HALF), F32),
        compiler_params=_cparams(2),
    )(proj, proj, proj, proj, proj, proj, proj, proj, proj, bias)


def _ctx_attn_kernel(q_ref, k_ref, v_ref, o_ref):
    qb = (q_ref[...] * QK_SCALE).astype(BF16)
    s = _nt(qb, k_ref[...].astype(BF16))
    p = jnp.exp(s - jnp.max(s, axis=1, keepdims=True))
    o_ref[...] = _nn(p.astype(BF16), v_ref[...].astype(BF16)) / jnp.sum(p, axis=1, keepdims=True)


def _ctx_attn(proj, t_lat, t_ctx):
    blk = t_lat // t_ctx
    spec = lambda c: pl.BlockSpec((t_ctx, HEAD_DIM), lambda h, c=c: (blk, c * N_HEADS + h))
    return pl.pallas_call(
        _ctx_attn_kernel,
        grid=(N_HEADS,),
        in_specs=[spec(4), spec(5), spec(6)],
        out_specs=pl.BlockSpec((t_ctx, HEAD_DIM), lambda h: (0, h)),
        out_shape=jax.ShapeDtypeStruct((t_ctx, HALF), F32),
        compiler_params=_cparams(1),
    )(proj, proj, proj)


def _gla_kernel(*refs, rev):
    if rev:
        q_ref, i_ref, f_ref, lb_ref, of_ref, g_ref, out_ref, st_ref, o_scr, k_scr, b_scr = refs
    else:
        q_ref, i_ref, f_ref, lb_ref, out_ref, st_ref, o_scr, k_scr, b_scr = refs
    n = q_ref.shape[0]

    @pl.when(pl.program_id(0) == 0)
    def _():
        st_ref[...] = jnp.zeros_like(st_ref)

    mask = _scan_mask(rev, n)
    tri = mask.astype(BF16)
    last = 0 if rev else n - 1
    x = f_ref[...]
    key = (1.0 - lb_ref[...]) * _sigmoid(-x)
    b = sum(_nn(tri, p) for p in _split3(jnp.log1p(-key)))
    b_end = b[last:last + 1, :]
    qx = q_ref[...]
    q = qx * _sigmoid(qx)
    qt = q * jnp.exp(b)
    kdec = key * jnp.exp(b_end - b)
    w_end = jnp.exp(b_end)
    k_scr[...] = key
    b_scr[...] = b

    for h in range(N_HEADS):
        hs = slice(h * HEAD_DIM, (h + 1) * HEAD_DIM)
        st = st_ref[h]
        vb = i_ref[:, hs].astype(BF16)
        o_scr[:, hs] = _nt(qt[:, hs].astype(BF16), st.astype(BF16))
        st_ref[h] = st * w_end[:, hs] + _tn(vb, kdec[:, hs].astype(BF16))

    safe = jnp.min(b_end) >= 2.0 * GLA_SAFE_LOG

    @pl.when(safe)
    def _():
        mid = 0.5 * b_end
        qm = q * jnp.exp(b_scr[...] - mid)
        km = k_scr[...] * jnp.exp(mid - b_scr[...])
        for h in range(N_HEADS):
            hs = slice(h * HEAD_DIM, (h + 1) * HEAD_DIM)
            att = jnp.where(mask, _nt(qm[:, hs].astype(BF16), km[:, hs].astype(BF16)), 0.0)
            o_scr[:, hs] += _nn(att.astype(BF16), i_ref[:, hs].astype(BF16))

    @pl.when(jnp.logical_not(safe))
    def _():
        t_idx = lax.broadcasted_iota(jnp.int32, (n, 1), 0)
        for h in range(N_HEADS):
            hs = slice(h * HEAD_DIM, (h + 1) * HEAD_DIM)
            q_h = q[:, hs]
            b_h = b[:, hs]

            def body(grp, acc):
                base = pl.multiple_of(grp * 8, 8)
                b8, k8, v8 = b_scr[pl.ds(base, 8), hs], k_scr[pl.ds(base, 8), hs], i_ref[pl.ds(base, 8), hs]
                for r in range(8):
                    s = base + r
                    e = jnp.exp(jnp.minimum(b_h - b8[r:r + 1, :], 0.0))
                    a = jnp.sum(q_h * k8[r:r + 1, :] * e, axis=1, keepdims=True)
                    a = jnp.where((t_idx <= s) if rev else (t_idx >= s), a, 0.0)
                    acc = acc + a * v8[r:r + 1, :]
                return acc

            o_scr[:, hs] += lax.fori_loop(0, n // 8, body, jnp.zeros((n, HEAD_DIM), F32))

    if rev:
        for h in range(N_HEADS):
            hs = slice(h * HEAD_DIM, (h + 1) * HEAD_DIM)
            gx = g_ref[:, hs]
            out_ref[:, hs] = _head_norm(of_ref[:, hs] + o_scr[:, hs]) * (gx * _sigmoid(gx))
    else:
        out_ref[...] = o_scr[...]


def _gla(proj, lb, n_lat_chunks):
    t_all = proj.shape[0]
    n = GLA_CHUNK
    n_chunks = t_all // n

    def call(rev, extra_in, extra_args):
        blk = _scan_block(rev, n_chunks, n_lat_chunks)
        colblk = lambda c: pl.BlockSpec((n, HALF), lambda s, c=c: (blk(s), c))
        return pl.pallas_call(
            functools.partial(_gla_kernel, rev=rev),
            grid=(n_chunks,),
            in_specs=[colblk(0), colblk(1), colblk(3 if rev else 2),
                      pl.BlockSpec((1, HALF), lambda s: (0, 0))] + extra_in,
            out_specs=pl.BlockSpec((n, HALF), lambda s: (blk(s), 0)),
            out_shape=jax.ShapeDtypeStruct((t_all, HALF), F32),
            scratch_shapes=[pltpu.VMEM((N_HEADS, HEAD_DIM, HEAD_DIM), F32),
                            pltpu.VMEM((n, HALF), F32), pltpu.VMEM((n, HALF), F32),
                            pltpu.VMEM((n, HALF), F32)],
            compiler_params=_cparams(1),
        )(proj, proj, proj, lb.reshape(1, HALF), *extra_args)

    o_fw = call(False, [], [])
    blk_r = _scan_block(True, n_chunks, n_lat_chunks)
    return call(True, [pl.BlockSpec((n, HALF), lambda s: (blk_r(s), 0)),
                       pl.BlockSpec((n, HALF), lambda s: (blk_r(s), 4))], [o_fw, proj])


def _s5_mats(a_re, a_im, log_dt, b_re, b_im, c_re, c_im):
    hp = lax.Precision.HIGHEST
    n = S5_CHUNK
    dt = jnp.exp(log_dt)[..., None]
    la_re, la_im = a_re * dt, a_im * dt
    mag = jnp.exp(la_re)
    ab_re, ab_im = mag * jnp.cos(la_im), mag * jnp.sin(la_im)
    nr, ni = ab_re - 1.0, ab_im
    den = jnp.square(a_re) + jnp.square(a_im)
    cr = (nr * a_re + ni * a_im) / den
    ci = (ni * a_re - nr * a_im) / den
    bb_re = cr[..., None] * b_re - ci[..., None] * b_im
    bb_im = cr[..., None] * b_im + ci[..., None] * b_re

    def apow(tau):
        tau = jnp.asarray(tau, F32)[None, None, :, None]
        m = jnp.exp(tau * la_re[:, :, None, :])
        return m * jnp.cos(tau * la_im[:, :, None, :]), m * jnp.sin(tau * la_im[:, :, None, :])

    def c_apow(tau):
        pr, pi = apow(tau)
        return (c_re[:, :, None] * pr[:, :, :, None] - c_im[:, :, None] * pi[:, :, :, None],
                c_re[:, :, None] * pi[:, :, :, None] + c_im[:, :, None] * pr[:, :, :, None])

    ca_re, ca_im = c_apow(np.arange(n))
    kern = (jnp.einsum('dgtip,dgpj->dgtij', ca_re, bb_re, precision=hp)
            - jnp.einsum('dgtip,dgpj->dgtij', ca_im, bb_im, precision=hp))
    s_idx, t_idx = np.arange(n)[:, None], np.arange(n)[None, :]
    g = a_re.shape[1]
    lp = n * S5_GROUP

    def toeplitz(kd, lag):
        onehot = jnp.asarray(lag[None] == np.arange(n)[:, None, None], F32)
        return jnp.einsum('lst,glij->gsjti', onehot, kd, precision=hp).reshape(g, lp, lp)

    kmat = toeplitz(kern[0], t_idx - s_idx) + toeplitz(kern[1], s_idx - t_idx)

    def emat(d, tau):
        pr, pi = apow(tau)
        pr, pi = pr[d][:, :, None, :], pi[d][:, :, None, :]
        br, bi = bb_re[d].transpose(0, 2, 1)[:, None], bb_im[d].transpose(0, 2, 1)[:, None]
        return jnp.concatenate([pr * br - pi * bi, pr * bi + pi * br], axis=-1).reshape(g, lp, 2 * S5_STATE)

    def fmat(d, tau):
        fr, fi = c_apow(tau)
        fr, fi = fr[d], fi[d]
        return jnp.concatenate([fr, -fi], axis=-1).transpose(0, 3, 1, 2).reshape(g, 2 * S5_STATE, lp)

    e_all = jnp.concatenate([emat(0, n - 1 - np.arange(n)), emat(1, np.arange(n))], axis=-1)
    f_all = jnp.concatenate([fmat(0, np.arange(n) + 1), fmat(1, n - np.arange(n))], axis=1)
    pr, pi = apow(np.array([n]))
    pr, pi = pr[:, :, 0], pi[:, :, 0]
    ar2 = jnp.concatenate([pr[0], pr[0], pr[1], pr[1]], axis=-1)
    ai2 = jnp.concatenate([-pi[0], pi[0], -pi[1], pi[1]], axis=-1)
    return kmat.astype(BF16), e_all.astype(BF16), f_all.astype(BF16), ar2, ai2


def _s5_state_kernel(u_ref, e_ref, s_ref):
    s_ref[...] = _nn(u_ref[...], e_ref[...])


def _s5_scan_kernel(sf_ref, sb_ref, ar_ref, ai_ref, hf_ref, hb_ref, stf_ref, stb_ref):
    nb = sf_ref.shape[0]
    w = 2 * S5_STATE
    ar, ai = ar_ref[...], ai_ref[...]

    @pl.when(pl.program_id(0) == 0)
    def _():
        stf_ref[...] = jnp.zeros_like(stf_ref)
        stb_ref[...] = jnp.zeros_like(stb_ref)

    def step(h, s, lo):
        return ar[:, lo:lo + w] * h + ai[:, lo:lo + w] * pltpu.roll(h, S5_STATE, 1) + s

    hf, hb = stf_ref[...], stb_ref[...]
    for j in range(nb):
        hf_ref[j] = hf
        hf = step(hf, sf_ref[j], 0)
        hb_ref[nb - 1 - j] = hb
        hb = step(hb, sb_ref[nb - 1 - j], w)
    stf_ref[...] = hf
    stb_ref[...] = hb


def _s5_out_kernel(u_ref, k_ref, h_ref, f_ref, y_ref):
    y_ref[...] = _nn(u_ref[...], k_ref[...]) + _nn(h_ref[...].astype(BF16), f_ref[...])


def _s5(u, mats, n_lat_chunks):
    kmat, emat, fmat, ar2, ai2 = mats
    t_all = u.shape[0]
    n = S5_CHUNK
    nc = t_all // n
    g = S5_GROUPS
    lp = n * S5_GROUP
    w4 = 4 * S5_STATE
    ug = u.astype(BF16).reshape(nc, n, g, S5_GROUP).transpose(2, 0, 1, 3).reshape(g, nc, lp)
    grp = lambda a, b: pl.BlockSpec((None, a, b), lambda i: (i, 0, 0))
    s_end = pl.pallas_call(
        _s5_state_kernel, grid=(g,),
        in_specs=[grp(nc, lp), grp(lp, w4)], out_specs=grp(nc, w4),
        out_shape=jax.ShapeDtypeStruct((g, nc, w4), F32), compiler_params=_cparams(1),
    )(ug, emat)
    nb = S5_SCAN_BLOCK
    w2 = 2 * S5_STATE
    fw_blk = _scan_block(False, nc // nb, n_lat_chunks // nb)
    bw_blk = _scan_block(True, nc // nb, n_lat_chunks // nb)
    s_t = s_end.transpose(1, 0, 2)
    coef = pl.BlockSpec((g, w4), lambda i: (0, 0))
    h_fw, h_bw = pl.pallas_call(
        _s5_scan_kernel, grid=(nc // nb,),
        in_specs=[pl.BlockSpec((nb, g, w2), lambda i: (fw_blk(i), 0, 0)),
                  pl.BlockSpec((nb, g, w2), lambda i: (bw_blk(i), 0, 1)), coef, coef],
        out_specs=[pl.BlockSpec((nb, g, w2), lambda i: (fw_blk(i), 0, 0)),
                   pl.BlockSpec((nb, g, w2), lambda i: (bw_blk(i), 0, 0))],
        out_shape=[jax.ShapeDtypeStruct((nc, g, w2), F32)] * 2,
        scratch_shapes=[pltpu.VMEM((g, w2), F32)] * 2,
        compiler_params=_cparams(1),
    )(s_t, s_t, ar2, ai2)
    h_in = jnp.concatenate([h_fw, h_bw], axis=-1).transpose(1, 0, 2)
    y = pl.pallas_call(
        _s5_out_kernel, grid=(g,),
        in_specs=[grp(nc, lp), grp(lp, lp), grp(nc, w4), grp(w4, lp)], out_specs=grp(nc, lp),
        out_shape=jax.ShapeDtypeStruct((g, nc, lp), F32), compiler_params=_cparams(1),
    )(ug, kmat, h_in, fmat)
    return y.reshape(g, nc, n, S5_GROUP).transpose(1, 2, 0, 3).reshape(t_all, HALF)


def _top_rows(x, k):
    rows = []
    for _ in range(k):
        mx = jnp.max(x, axis=0, keepdims=True)
        rows.append(mx)
        x = jnp.where(x == mx, NEG, x)
    return rows


def _peer_route_kernel(ht_ref, wq_ref, keys_ref, e1_ref, e0_ref, th_ref, top_scr):
    kk = PEER_TOPK
    qt = _nn(wq_ref[...], ht_ref[...])
    for hp in range(2 * PEER_HEADS):
        sc = _nn(keys_ref[hp], qt[hp * HEAD_DIM:(hp + 1) * HEAD_DIM, :].astype(BF16))
        h, p = divmod(hp, 2)
        if p == 0:
            e0_ref[h] = sc
        else:
            e1_ref[h] = sc
        for r, mx in enumerate(_top_rows(sc, kk + 1)):
            top_scr[hp, r:r + 1, :] = mx
        top_scr[hp, kk + 1:, :] = jnp.full((PEER_TOP_ROWS - kk - 1, sc.shape[1]), NEG, F32)

    row = lax.broadcasted_iota(jnp.int32, (PEER_TOP_ROWS, 1), 0)
    for h in range(PEER_HEADS):
        a = top_scr[2 * h]
        b = top_scr[2 * h + 1]
        parts = [a[0:1] + b, a[1:2] + b[0:8], a[2:3] + b[0:8], a[3:4] + b[0:8]]
        parts += [jnp.where(row >= 4, a + b[j:j + 1], NEG) for j in range(3)]
        cand = jnp.maximum(jnp.concatenate(parts, axis=0), NEG)
        top = _top_rows(cand, kk + 1)
        thr = 0.5 * (top[kk - 1] + top[kk])
        z = jnp.sum(jnp.where(cand >= thr, jnp.exp(cand - top[0]), 0.0), axis=0, keepdims=True)
        e0_ref[h] = jnp.exp(e0_ref[h] - a[0:1]) / z
        e1_ref[h] = jnp.exp(e1_ref[h] - b[0:1])
        th_ref[h:h + 1, :] = jnp.exp(thr - top[0]) / z


def _peer_expert_kernel(ht_ref, e1_ref, e0_ref, th_ref, *rest):
    u_refs = rest[:PEER_U_SPLIT]
    vt_ref, o_ref, at0_scr, at1_scr, g0_scr, g1_scr = rest[PEER_U_SPLIT:]
    s = pl.program_id(1)
    n_tiles = pl.num_programs(1) - 2
    ni = u_refs[0].shape[0] // PEER_N_KEYS
    kc = u_refs[0].shape[1]

    @pl.when(s == 0)
    def _():
        o_ref[...] = jnp.zeros_like(o_ref)
        for scr in (at0_scr, at1_scr, g0_scr, g1_scr):
            scr[...] = jnp.zeros_like(scr)

    tile = jnp.clip(s - 1, 0, n_tiles - 1)

    def stages(at_new, at_old, g_new, g_old):
        at_new[...] = sum(_nn(u_k[...], ht_ref[k * kc:(k + 1) * kc, :]) for k, u_k in enumerate(u_refs))
        nj, nc = PEER_GATE_ROWS, 128
        for ii in range(ni):
            i = tile * ni + ii
            e0_rows = [e0_ref[h, pl.ds(i, 1), :] for h in range(PEER_HEADS)]
            for c in range(at_old.shape[1] // nc):
                cs = slice(c * nc, (c + 1) * nc)
                for jb in range(PEER_N_KEYS // nj):
                    js = slice(jb * nj, (jb + 1) * nj)
                    w = None
                    for h in range(PEER_HEADS):
                        p = e1_ref[h, js, cs] * e0_rows[h][:, cs]
                        t = jnp.where(p >= th_ref[h:h + 1, cs], p, 0.0)
                        w = t if w is None else w + t
                    rs = slice(ii * PEER_N_KEYS + jb * nj, ii * PEER_N_KEYS + (jb + 1) * nj)
                    g_new[rs, cs] = (_gelu_tanh(at_old[rs, cs]) * w).astype(BF16)
        o_ref[...] += _nn(vt_ref[...], g_old[...])

    @pl.when(s % 2 == 0)
    def _():
        stages(at0_scr, at1_scr, g1_scr, g0_scr)

    @pl.when(s % 2 == 1)
    def _():
        stages(at1_scr, at0_scr, g0_scr, g1_scr)


def _tok3(t):
    return pl.BlockSpec((PEER_HEADS, PEER_N_KEYS, t), lambda i, *_: (0, 0, i))


def _peer_route(h2t, wq_t_bf, keys_bf):
    d, t_all = h2t.shape
    nh, nk = PEER_HEADS, PEER_N_KEYS
    t1 = PEER_P1_TILE
    return pl.pallas_call(
        _peer_route_kernel,
        grid=(t_all // t1,),
        in_specs=[pl.BlockSpec((d, t1), lambda i: (0, i)),
                  pl.BlockSpec(wq_t_bf.shape, lambda i: (0, 0)),
                  pl.BlockSpec(keys_bf.shape, lambda i: (0, 0, 0))],
        out_specs=[_tok3(t1), _tok3(t1), pl.BlockSpec((nh, t1), lambda i: (0, i))],
        out_shape=[jax.ShapeDtypeStruct((nh, nk, t_all), F32)] * 2 + [jax.ShapeDtypeStruct((nh, t_all), F32)],
        scratch_shapes=[pltpu.VMEM((2 * nh, PEER_TOP_ROWS, t1), F32)],
        compiler_params=_cparams(1),
    )(h2t, wq_t_bf, keys_bf)


def _peer(h2t, wq_t_bf, keys_bf, u_bf, vt_bf):
    t_all = h2t.shape[1]
    h2t = jnp.pad(h2t, ((0, 0), (0, -t_all % PEER_TOK_TILE)))
    return _peer_experts(h2t, *_peer_route(h2t, wq_t_bf, keys_bf), u_bf, vt_bf)


def _peer_experts(h2t, e1, e0, th, u_bf, vt_bf):
    d, t_all = h2t.shape
    tt = PEER_TOK_TILE
    et = PEER_EXP_TILE
    n_tiles = u_bf.shape[0] // et
    tok = pl.BlockSpec((d, tt), lambda i, s: (0, i))
    return pl.pallas_call(
        _peer_expert_kernel,
        grid=(t_all // tt, n_tiles + 2),
        in_specs=[tok, _tok3(tt), _tok3(tt), pl.BlockSpec((PEER_HEADS, tt), lambda i, s: (0, i))]
        + [pl.BlockSpec((et, d // PEER_U_SPLIT), lambda i, s, k=k: (jnp.minimum(s, n_tiles - 1), k))
           for k in range(PEER_U_SPLIT)]
        + [pl.BlockSpec((d, et), lambda i, s: (0, jnp.clip(s - 2, 0, n_tiles - 1)))],
        out_specs=tok,
        out_shape=jax.ShapeDtypeStruct((d, t_all), F32),
        scratch_shapes=[pltpu.VMEM((et, tt), F32)] * 2 + [pltpu.VMEM((et, tt), BF16)] * 2,
        compiler_params=_cparams(2),
    )(h2t, e1, e0, th, *([u_bf] * PEER_U_SPLIT), vt_bf)


def _rope_tables(t_lat, t_ctx):
    pos = np.arange(t_lat)
    nfreq = HEAD_DIM // 4
    freqs = ROPE_BASE ** (-jnp.arange(nfreq, dtype=F32) / nfreq)
    ang_r = jnp.asarray(pos // GRID_W, F32)[:, None] * freqs
    ang_c = jnp.asarray(pos % GRID_W, F32)[:, None] * freqs
    cos = jnp.concatenate([jnp.cos(ang_r), jnp.cos(ang_r), jnp.cos(ang_c), jnp.cos(ang_c)], axis=1)
    sin = jnp.concatenate([-jnp.sin(ang_r), jnp.sin(ang_r), -jnp.sin(ang_c), jnp.sin(ang_c)], axis=1)
    cos = jnp.concatenate([cos, jnp.ones((t_ctx, HEAD_DIM), F32)], axis=0)
    sin = jnp.concatenate([sin, jnp.zeros((t_ctx, HEAD_DIM), F32)], axis=0)
    return cos, sin


def _peer_and_norm(x1, h2, mods, ln_g, ln_b, n_lat_tiles, w_q, sub_keys, u_tab, v_tab, mods_next=None):
    keys = sub_keys.reshape(2 * PEER_HEADS, PEER_N_KEYS, -1).astype(BF16)
    pt = _peer(h2.T, w_q.T.astype(BF16), keys, u_tab.astype(BF16), v_tab.T.astype(BF16))
    return _ln2(x1, pt, mods, ln_g, ln_b, n_lat_tiles, mods_next)


def _layer_ab(x_all, mods, t_lat, t_ctx, w_in, w_out, igate_b, fgate_b, rpb, ln_g, ln_b, xm=None):
    n_lat_tiles = t_lat // ROW_TILE
    if xm is None:
        xm = _modulate(x_all, mods, n_lat_tiles)
    ng = 4 * N_HEADS
    w_main = jnp.concatenate([w_in[:, :4 * HALF], w_in[:, 4 * HALF + ng:]], axis=1).astype(BF16)
    w_gate = jnp.pad(w_in[:, 4 * HALF:4 * HALF + ng], ((0, 0), (0, 128 - ng))).astype(BF16)
    proj = _inproj(xm, w_main)
    gates = _inproj(xm, w_gate)
    cos_t, sin_t = _rope_tables(t_lat, t_ctx)
    y_m = _mlstm(proj, gates, gates.T, cos_t, sin_t, igate_b, fgate_b, t_lat // SCAN_CHUNK)
    y_na = jnp.concatenate([_na(proj, rpb, t_lat, t_ctx), _ctx_attn(proj, t_lat, t_ctx)], axis=0)
    return _outproj(y_m, [(y_na, 0)], w_out.astype(BF16), x_all, mods, ln_g, ln_b, n_lat_tiles)


def _layer_cd(x_all, mods, t_lat, w_in, w_out, lb, s5p, s5_d, glu_w, glu_b, ln_g, ln_b, xm=None):
    n_lat_tiles = t_lat // ROW_TILE
    if xm is None:
        xm = _modulate(x_all, mods, n_lat_tiles)
    proj = _inproj(xm, w_in.astype(BF16))
    y_h = _gla(proj, lb, t_lat // GLA_CHUNK)
    y5 = _s5(proj[:, 5 * HALF:], _s5_mats(*[p.astype(F32) for p in s5p]), t_lat // S5_CHUNK)
    return _outproj(y_h, [(y5, 0), (proj, 5)], w_out.astype(BF16), x_all, mods, ln_g, ln_b, n_lat_tiles,
                    glu_params=(s5_d.astype(F32), glu_w.astype(BF16), glu_b.astype(F32)))


def kernel(x, c, ctx, c_ctx, ada_w, ada_b, ln_g, ln_b, ab_w_in, ab_w_out, mlstm_igate_b, mlstm_fgate_b, na_rpb, cd_w_in, cd_w_out, hgrn_lb_logits, s5_a_re, s5_a_im, s5_log_dt, s5_b_re, s5_b_im, s5_c_re, s5_c_im, s5_d, s5_glu_w, s5_glu_b, peer_w_q, peer_sub_keys, peer_u, peer_v):
    t_lat, t_ctx = x.shape[1], ctx.shape[1]
    assert x.shape[0] == 1 and t_ctx == ROW_TILE and t_lat % (NA_QROWS * GRID_W) == 0
    depth = ada_w.shape[0]
    lb_soft = jax.nn.softmax(hgrn_lb_logits.astype(F32), axis=0)
    lower_bounds = jnp.cumsum(lb_soft, axis=0) - lb_soft[0]
    x_all = jnp.concatenate([x[0], ctx[0]], axis=0).astype(F32)
    n_lat_tiles = t_lat // ROW_TILE
    mods_all = [_adaln(c, c_ctx, ada_w[l], ada_b[l]) for l in range(depth)]
    xm = _modulate(x_all, mods_all[0], n_lat_tiles)
    for l in range(depth):
        j = l // 2
        mods = mods_all[l]
        if l % 2 == 0:
            x1, h2 = _layer_ab(x_all, mods, t_lat, t_ctx, ab_w_in[j], ab_w_out[j], mlstm_igate_b[j],
                               mlstm_fgate_b[j], na_rpb[j], ln_g[l, 0], ln_b[l, 0], xm=xm)
        else:
            s5p = (s5_a_re[j], s5_a_im[j], s5_log_dt[j], s5_b_re[j], s5_b_im[j], s5_c_re[j], s5_c_im[j])
            x1, h2 = _layer_cd(x_all, mods, t_lat, cd_w_in[j], cd_w_out[j], lower_bounds[l], s5p, s5_d[j],
                               s5_glu_w[j], s5_glu_b[j], ln_g[l, 0], ln_b[l, 0], xm=xm)
        peer_args = (x1, h2, mods, ln_g[l, 1], ln_b[l, 1], n_lat_tiles, peer_w_q[l], peer_sub_keys[l],
                     peer_u[l], peer_v[l])
        if l + 1 < depth:
            x_all, xm = _peer_and_norm(*peer_args, mods_next=mods_all[l + 1])
        else:
            x_all = _peer_and_norm(*peer_args)
    return x_all[:t_lat][None].astype(x.dtype)
```

```python
import functools
import math

import numpy as np
import jax
import jax.numpy as jnp
from jax import lax
from jax.experimental import pallas as pl
from jax.experimental.pallas import tpu as pltpu

F32 = jnp.float32
BF16 = jnp.bfloat16

HEAD_DIM = 128
GRID_W = 64
N_HEADS = 8
HALF = N_HEADS * HEAD_DIM
NA_WIN_ROWS = 8
NA_WIN_COLS = 16
ROPE_BASE = 10000.0
S5_GROUP = 16
S5_STATE = 64
S5_GROUPS = HALF // S5_GROUP
PEER_HEADS = 8
PEER_N_KEYS = 128
PEER_TOPK = 16
PEER_TOP_ROWS = 24
LN_EPS = 1e-5
DEPTH = 2
ALPHA = (2.0 * DEPTH) ** 0.25
QK_SCALE = HEAD_DIM ** -0.5

ROW_TILE = 256
MM_ROW_TILE = 1280
MM_COL_TILE = 1024
SCAN_CHUNK = 128
GLA_CHUNK = 64
S5_CHUNK = 32
S5_SCAN_BLOCK = 8
NA_QROWS = 8
PEER_P1_TILE = 256
PEER_TOK_TILE = 512
PEER_EXP_TILE = 512
PEER_GATE_ROWS = 64
PEER_U_SPLIT = 4
NEG = -1e30
GLA_SAFE_LOG = -80.0
VMEM_LIMIT = 56 * 1024 * 1024


def _cparams(n_axes):
    return pltpu.CompilerParams(dimension_semantics=("arbitrary",) * n_axes,
                                vmem_limit_bytes=VMEM_LIMIT)


def _nt(a, b):
    return lax.dot_general(a, b, (((1,), (1,)), ((), ())), preferred_element_type=F32)


def _tn(a, b):
    return lax.dot_general(a, b, (((0,), (0,)), ((), ())), preferred_element_type=F32)


def _nn(a, b):
    return jnp.dot(a, b, preferred_element_type=F32)


def _split3(x):
    hi = x.astype(BF16)
    r = x - hi.astype(F32)
    mid = r.astype(BF16)
    lo = (r - mid.astype(F32)).astype(BF16)
    return hi, mid, lo


def _log_sigmoid(x):
    return jnp.minimum(x, 0.0) - jnp.log1p(jnp.exp(-jnp.abs(x)))


def _sigmoid(x):
    return 1.0 / (1.0 + jnp.exp(-x))


def _gelu_tanh(x):
    return 0.5 * x * (1.0 + jnp.tanh(math.sqrt(2.0 / math.pi) * (x + 0.044715 * (x * x * x))))


def _head_norm(x):
    mu = jnp.mean(x, axis=-1, keepdims=True)
    xc = x - mu
    return xc * lax.rsqrt(jnp.mean(xc * xc, axis=-1, keepdims=True) + LN_EPS)


def _adaln_kernel(ct_ref, w_ref, b_ref, o_ref):
    ct = ct_ref[...]
    s = ct * _sigmoid(ct)
    w = w_ref[...]
    o_ref[0:1, :] = jnp.sum(s[:, 0:1] * w, axis=0, keepdims=True) + b_ref[...]
    o_ref[1:2, :] = jnp.sum(s[:, 1:2] * w, axis=0, keepdims=True) + b_ref[...]


def _adaln(c_lat, c_ctx, w, b):
    d = w.shape[0]
    n = w.shape[1]
    tn = n // 8
    ct = jnp.stack([c_lat.reshape(d), c_ctx.reshape(d)], axis=1)
    out = pl.pallas_call(
        _adaln_kernel,
        grid=(n // tn,),
        in_specs=[pl.BlockSpec((d, 2), lambda j: (0, 0)),
                  pl.BlockSpec((d, tn), lambda j: (0, j)),
                  pl.BlockSpec((1, tn), lambda j: (0, j))],
        out_specs=pl.BlockSpec((2, tn), lambda j: (0, j)),
        out_shape=jax.ShapeDtypeStruct((2, n), F32),
        compiler_params=_cparams(1),
    )(ct, w, b.reshape(1, n))
    return jnp.pad(out.reshape(2, 6, d), ((0, 0), (0, 2), (0, 0)))


def _modulated(x, m_ref):
    return (x * (1.0 + m_ref[1:2, :]) + m_ref[0:1, :]).astype(BF16)


def _modulate_kernel(x_ref, m_ref, o_ref):
    o_ref[...] = _modulated(x_ref[...], m_ref)


def _mods_spec(d, n_lat_tiles):
    return pl.BlockSpec((None, 8, d), lambda i: (jnp.where(i < n_lat_tiles, 0, 1), 0, 0))


def _modulate(x_all, mods, n_lat_tiles):
    t_all, d = x_all.shape
    row = pl.BlockSpec((ROW_TILE, d), lambda i: (i, 0))
    return pl.pallas_call(
        _modulate_kernel,
        grid=(t_all // ROW_TILE,),
        in_specs=[row, _mods_spec(d, n_lat_tiles)],
        out_specs=row,
        out_shape=jax.ShapeDtypeStruct((t_all, d), BF16),
        compiler_params=_cparams(1),
    )(x_all, mods)


def _matmul_kernel(x_ref, w_ref, o_ref):
    o_ref[...] = _nn(x_ref[...], w_ref[...])


def _inproj(xm, w_bf):
    t_all, d = xm.shape
    n = w_bf.shape[1]
    tm = MM_ROW_TILE if t_all % MM_ROW_TILE == 0 else ROW_TILE
    tn = min(n, MM_COL_TILE)
    return pl.pallas_call(
        _matmul_kernel,
        grid=(t_all // tm, n // tn),
        in_specs=[pl.BlockSpec((tm, d), lambda i, j: (i, 0)),
                  pl.BlockSpec((d, tn), lambda i, j: (0, j))],
        out_specs=pl.BlockSpec((tm, tn), lambda i, j: (i, j)),
        out_shape=jax.ShapeDtypeStruct((t_all, n), F32),
        compiler_params=_cparams(2),
    )(xm, w_bf)


def _ln_rows(z, g, b):
    mu = jnp.mean(z, axis=-1, keepdims=True)
    zc = z - mu
    return zc * lax.rsqrt(jnp.mean(zc * zc, axis=-1, keepdims=True) + LN_EPS) * g + b


def _outproj_kernel(*refs, glu):
    hf_ref, hb_ref, gate_ref = refs[:3]
    refs = refs[3:]
    if glu:
        (y5_ref, u5_ref, ds_ref, gw_ref, gb_ref, w1_ref, w2_ref, x_ref, m_ref, g_ref, b_ref,
         x1_ref, h2_ref) = refs
        y = _gelu_tanh(y5_ref[...] + ds_ref[...] * u5_ref[...])
        a2 = y * _sigmoid(_nn(y.astype(BF16), gw_ref[...]) + gb_ref[...])
    else:
        a2_ref, w1_ref, w2_ref, x_ref, m_ref, g_ref, b_ref, x1_ref, h2_ref = refs
        a2 = a2_ref[...]
    heads = []
    for h in range(N_HEADS):
        hs = slice(h * HEAD_DIM, (h + 1) * HEAD_DIM)
        gx = gate_ref[:, hs]
        gate = gx * _sigmoid(gx) if glu else _sigmoid(gx)
        heads.append((_head_norm(hf_ref[:, hs] + hb_ref[:, hs]) * gate).astype(BF16))
    a1 = jnp.concatenate(heads, axis=1)
    y = _nn(a1, w1_ref[...]) + _nn(a2.astype(BF16), w2_ref[...])
    x1 = _ln_rows(ALPHA * x_ref[...] + m_ref[2:3, :] * y, g_ref[...], b_ref[...])
    x1_ref[...] = x1
    h2_ref[...] = (x1 * (1.0 + m_ref[4:5, :]) + m_ref[3:4, :]).astype(BF16)


def _outproj(scan_parts, a2_parts, w_out_bf, x_all, mods, ln_g, ln_b, n_lat_tiles, glu_params=None):
    t_all, d = x_all.shape
    half = HALF
    row = lambda i: (i, 0)
    const = lambda i: (0, 0)
    a2_parts = list(scan_parts) + list(a2_parts)
    a2_specs = [pl.BlockSpec((ROW_TILE, half), lambda i, c=c: (i, c)) for _, c in a2_parts]
    a2_args = [a for a, _ in a2_parts]
    if glu_params is not None:
        d_skip, glu_w_bf, glu_b = glu_params
        a2_specs += [pl.BlockSpec((1, half), const), pl.BlockSpec((half, half), const),
                     pl.BlockSpec((1, half), const)]
        a2_args += [d_skip.reshape(1, half), glu_w_bf, glu_b.reshape(1, half)]
    return pl.pallas_call(
        functools.partial(_outproj_kernel, glu=glu_params is not None),
        grid=(t_all // ROW_TILE,),
        in_specs=a2_specs + [
            pl.BlockSpec((half, d), const), pl.BlockSpec((half, d), const),
            pl.BlockSpec((ROW_TILE, d), row),
            pl.BlockSpec((None, 8, d), lambda i: (jnp.where(i < n_lat_tiles, 0, 1), 0, 0)),
            pl.BlockSpec((1, d), const), pl.BlockSpec((1, d), const)],
        out_specs=[pl.BlockSpec((ROW_TILE, d), row), pl.BlockSpec((ROW_TILE, d), row)],
        out_shape=[jax.ShapeDtypeStruct((t_all, d), F32), jax.ShapeDtypeStruct((t_all, d), BF16)],
        compiler_params=_cparams(1),
    )(*a2_args, w_out_bf[:half], w_out_bf[half:], x_all, mods, ln_g.reshape(1, d), ln_b.reshape(1, d))


def _ln2_kernel(x_ref, p_ref, m_ref, g_ref, b_ref, *rest):
    x2 = _ln_rows(ALPHA * x_ref[...] + m_ref[5:6, :] * p_ref[...].T, g_ref[...], b_ref[...])
    if len(rest) == 1:
        rest[0][...] = x2
    else:
        mn_ref, o_ref, xm_ref = rest
        o_ref[...] = x2
        xm_ref[...] = _modulated(x2, mn_ref)


def _ln2(x1, peer_out_t, mods, ln_g, ln_b, n_lat_tiles, mods_next=None):
    t_all, d = x1.shape
    row = pl.BlockSpec((ROW_TILE, d), lambda i: (i, 0))
    col = pl.BlockSpec((d, ROW_TILE), lambda i: (0, i))
    vec = pl.BlockSpec((1, d), lambda i: (0, 0))
    nxt = mods_next is not None
    return pl.pallas_call(
        _ln2_kernel,
        grid=(t_all // ROW_TILE,),
        in_specs=[row, col, _mods_spec(d, n_lat_tiles), vec, vec] + ([_mods_spec(d, n_lat_tiles)] if nxt else []),
        out_specs=[row, row] if nxt else row,
        out_shape=([jax.ShapeDtypeStruct((t_all, d), F32), jax.ShapeDtypeStruct((t_all, d), BF16)] if nxt
                   else jax.ShapeDtypeStruct((t_all, d), F32)),
        compiler_params=_cparams(1),
    )(x1, peer_out_t, mods, ln_g.reshape(1, d), ln_b.reshape(1, d), *([mods_next] if nxt else []))


def _scan_block(rev, n_chunks, n_lat_chunks):
    if rev:
        return lambda s: n_chunks - 1 - s
    n_ctx = n_chunks - n_lat_chunks
    return lambda s: jnp.where(s < n_ctx, s + n_lat_chunks, s - n_ctx)


def _scan_mask(rev, n):
    row = lax.broadcasted_iota(jnp.int32, (n, n), 0)
    col = lax.broadcasted_iota(jnp.int32, (n, n), 1)
    return (col >= row) if rev else (col <= row)


def _rope(x, cos, sin):
    lane = lax.broadcasted_iota(jnp.int32, x.shape, 1)
    partner = jnp.where((lane % 64) < 32, pltpu.roll(x, 96, 1), pltpu.roll(x, 32, 1))
    return x * cos + partner * sin


def _mlstm_kernel(*refs):
    n_in = 7
    fw_in, bw_in = refs[:n_in], refs[n_in:2 * n_in]
    bias_ref, biast_ref, out_f, out_b = refs[2 * n_in:2 * n_in + 4]
    state = refs[2 * n_in + 4:]

    @pl.when(pl.program_id(0) == 0)
    def _():
        for r in state:
            r[...] = jnp.zeros_like(r)

    _mlstm_chunk(*fw_in, bias_ref, biast_ref, out_f, *state[:3], rev=False)
    _mlstm_chunk(*bw_in, bias_ref, biast_ref, out_b, *state[3:], rev=True)


def _mlstm_chunk(q_ref, k_ref, v_ref, g_ref, gt_ref, cos_ref, sin_ref, bias_ref, biast_ref,
                 out_ref, c_ref, n_ref, m_ref, *, rev):
    n = q_ref.shape[0]
    mask = _scan_mask(rev, n)
    tri = mask.astype(BF16)
    pre_c = g_ref[...] + bias_ref[...]
    pre_r = gt_ref[...] + biast_ref[...]
    b_c = sum(_nn(tri, p) for p in _split3(_log_sigmoid(pre_c)))
    b_r = sum(_nt(p, tri) for p in _split3(_log_sigmoid(pre_r)))
    cos = cos_ref[...]
    sin = sin_ref[...]
    last = 0 if rev else n - 1
    off = N_HEADS if rev else 0

    for h in range(N_HEADS):
        hs = slice(h * HEAD_DIM, (h + 1) * HEAD_DIM)
        ci, cf = off + h, 2 * N_HEADS + off + h
        bc, br = b_c[:, cf:cf + 1], b_r[cf:cf + 1, :]
        ic, ir = pre_c[:, ci:ci + 1], pre_r[ci:ci + 1, :]
        m_prev = m_ref[h:h + 1, 0:1]
        logw = jnp.where(mask, bc - br + ir, NEG)
        inter = bc + m_prev
        m_t = jnp.maximum(inter, jnp.max(logw, axis=1, keepdims=True))
        w_inter = jnp.exp(inter - m_t)
        q = _rope(q_ref[:, hs], cos, sin) * QK_SCALE
        k = _rope(k_ref[:, hs], cos, sin)
        qb, kb, vb = q.astype(BF16), k.astype(BF16), v_ref[:, hs].astype(BF16)
        s = _nt(qb, kb) * jnp.exp(logw - m_t)
        c_old = c_ref[h]
        n_old = n_ref[h:h + 1, :]
        num = _nn(s.astype(BF16), vb) + _nn(qb, c_old.astype(BF16)) * w_inter
        den = jnp.sum(s, axis=1, keepdims=True) + jnp.sum(q * n_old, axis=1, keepdims=True) * w_inter
        den = jnp.maximum(jnp.abs(den), jnp.exp(-m_t))
        hout = num / den
        m_new = m_t[last:last + 1, :]
        b_last = bc[last:last + 1, :]
        w_old = jnp.exp(b_last + m_prev - m_new)
        kw = k * jnp.exp(b_last - bc + ic - m_new)
        c_ref[h] = w_old * c_old + _tn(kw.astype(BF16), vb)
        n_ref[h:h + 1, :] = w_old * n_old + jnp.sum(kw, axis=0, keepdims=True)
        m_ref[h:h + 1, :] = jnp.broadcast_to(m_new, (1, HEAD_DIM))
        out_ref[:, hs] = hout


def _mlstm(proj, gates, gates_t, cos_t, sin_t, igate_b, fgate_b, n_lat_chunks):
    t_all = proj.shape[0]
    n = SCAN_CHUNK
    n_chunks = t_all // n
    bias = jnp.zeros((1, 128), F32).at[0, :4 * N_HEADS].set(
        jnp.concatenate([igate_b.reshape(-1), fgate_b.reshape(-1)]))

    def dir_specs(rev):
        blk = _scan_block(rev, n_chunks, n_lat_chunks)
        colblk = lambda c: pl.BlockSpec((n, HALF), lambda s, c=c: (blk(s), c))
        tok = pl.BlockSpec((n, 128), lambda s: (blk(s), 0))
        ins = [colblk(0), colblk(1), colblk(2), tok, pl.BlockSpec((128, n), lambda s: (0, blk(s))), tok, tok]
        return ins, pl.BlockSpec((n, HALF), lambda s: (blk(s), 0))

    (in_f, out_f), (in_b, out_b) = dir_specs(False), dir_specs(True)
    dir_args = [proj, proj, proj, gates, gates_t, cos_t, sin_t]
    state = [pltpu.VMEM((N_HEADS, HEAD_DIM, HEAD_DIM), F32), pltpu.VMEM((N_HEADS, HEAD_DIM), F32),
             pltpu.VMEM((N_HEADS, HEAD_DIM), F32)]
    return pl.pallas_call(
        _mlstm_kernel,
        grid=(n_chunks,),
        in_specs=in_f + in_b + [pl.BlockSpec((1, 128), lambda s: (0, 0)), pl.BlockSpec((128, 1), lambda s: (0, 0))],
        out_specs=[out_f, out_b],
        out_shape=[jax.ShapeDtypeStruct((t_all, HALF), F32)] * 2,
        scratch_shapes=state + state,
        compiler_params=_cparams(1),
    )(*dir_args, *dir_args, bias, bias.reshape(128, 1))


def _na_kernel(q_ref, kp_ref, kc_ref, kn_ref, vp_ref, vc_ref, vn_ref, kx_ref, vx_ref, bias_ref, o_ref):
    qb = (q_ref[...] * QK_SCALE).astype(BF16)
    kk = jnp.concatenate([kp_ref[...], kc_ref[...], kn_ref[...]], axis=0).astype(BF16)
    vv = jnp.concatenate([vp_ref[...], vc_ref[...], vn_ref[...]], axis=0).astype(BF16)
    s_loc = _nt(qb, kk) + bias_ref[...]
    s_ctx = _nt(qb, kx_ref[...].astype(BF16))
    m = jnp.maximum(jnp.max(s_loc, axis=1, keepdims=True), jnp.max(s_ctx, axis=1, keepdims=True))
    p_loc = jnp.exp(s_loc - m)
    p_ctx = jnp.exp(s_ctx - m)
    l = jnp.sum(p_loc, axis=1, keepdims=True) + jnp.sum(p_ctx, axis=1, keepdims=True)
    o_ref[...] = (_nn(p_loc.astype(BF16), vv) + _nn(p_ctx.astype(BF16), vx_ref[...].astype(BF16))) / l


def _na_bias(rpb, rows):
    qn = NA_QROWS * GRID_W
    kn = 2 * qn
    ql = np.arange(qn)
    kl = np.arange(kn)
    out = []
    nqb = rows // NA_QROWS
    lr, krl = np.arange(NA_QROWS)[:, None], np.arange(2 * NA_QROWS)[None, :]
    dr = np.clip(krl - NA_QROWS // 2 - lr + NA_WIN_ROWS - 1, 0, 2 * NA_WIN_ROWS - 2)
    col = np.arange(GRID_W)
    dc = np.clip(col[None, :] - col[:, None] + NA_WIN_COLS - 1, 0, 2 * NA_WIN_COLS - 2)
    oh_r = jnp.asarray(dr[..., None] == np.arange(2 * NA_WIN_ROWS - 1), F32)
    oh_c = jnp.asarray(dc[..., None] == np.arange(2 * NA_WIN_COLS - 1), F32)
    table = jnp.einsum('hab,lka,qcb->hlqkc', rpb, oh_r, oh_c,
                       precision=lax.Precision.HIGHEST).reshape(rpb.shape[0], qn, kn)
    for qb in (0, 1, nqb - 1):
        qr = NA_QROWS * qb + ql // GRID_W
        qc = ql % GRID_W
        kr = NA_QROWS * qb - NA_QROWS // 2 + kl // GRID_W
        kc = kl % GRID_W
        r0 = np.clip(qr - NA_WIN_ROWS // 2, 0, rows - NA_WIN_ROWS)
        c0 = np.clip(qc - NA_WIN_COLS // 2, 0, GRID_W - NA_WIN_COLS)
        ok = ((kr[None, :] >= r0[:, None]) & (kr[None, :] < r0[:, None] + NA_WIN_ROWS)
              & (kc[None, :] >= c0[:, None]) & (kc[None, :] < c0[:, None] + NA_WIN_COLS)
              & (kr[None, :] >= 0) & (kr[None, :] < rows))
        out.append(jnp.where(jnp.asarray(ok)[None], table, NEG))
    return jnp.stack(out)


def _na(proj, rpb, t_lat, t_ctx):
    rows = t_lat // GRID_W
    qn = NA_QROWS * GRID_W
    hn = qn // 2
    nqb = rows // NA_QROWS
    assert nqb >= 2 and t_ctx == hn
    bias = _na_bias(rpb.astype(F32), rows)
    last_half = 2 * nqb - 1
    ctx_blk = t_lat // hn
    cls = lambda b: jnp.where(b == 0, 0, jnp.where(b == nqb - 1, 2, 1))

    def kv_specs(col0):
        return [pl.BlockSpec((hn, HEAD_DIM), lambda h, b: (jnp.maximum(2 * b - 1, 0), col0 + h)),
                pl.BlockSpec((qn, HEAD_DIM), lambda h, b: (b, col0 + h)),
                pl.BlockSpec((hn, HEAD_DIM), lambda h, b: (jnp.minimum(2 * b + 2, last_half), col0 + h))]

    kcol, vcol = 5 * N_HEADS, 6 * N_HEADS
    return pl.pallas_call(
        _na_kernel,
        grid=(N_HEADS, nqb),
        in_specs=[pl.BlockSpec((qn, HEAD_DIM), lambda h, b: (b, 4 * N_HEADS + h))]
        + kv_specs(kcol) + kv_specs(vcol)
        + [pl.BlockSpec((hn, HEAD_DIM), lambda h, b: (ctx_blk, kcol + h)),
           pl.BlockSpec((hn, HEAD_DIM), lambda h, b: (ctx_blk, vcol + h)),
           pl.BlockSpec((None, None, qn, 2 * qn), lambda h, b: (cls(b), h, 0, 0))],
        out_specs=pl.BlockSpec((qn, HEAD_DIM), lambda h, b: (b, h)),
        out_shape=jax.ShapeDtypeStruct((t_lat, HALF), F32),
        compiler_params=_cparams(2),
    )(proj, proj, proj, proj, proj, proj, proj, proj, proj, bias)


def _ctx_attn_kernel(q_ref, k_ref, v_ref, o_ref):
    qb = (q_ref[...] * QK_SCALE).astype(BF16)
    s = _nt(qb, k_ref[...].astype(BF16))
    p = jnp.exp(s - jnp.max(s, axis=1, keepdims=True))
    o_ref[...] = _nn(p.astype(BF16), v_ref[...].astype(BF16)) / jnp.sum(p, axis=1, keepdims=True)


def _ctx_attn(proj, t_lat, t_ctx):
    blk = t_lat // t_ctx
    spec = lambda c: pl.BlockSpec((t_ctx, HEAD_DIM), lambda h, c=c: (blk, c * N_HEADS + h))
    return pl.pallas_call(
        _ctx_attn_kernel,
        grid=(N_HEADS,),
        in_specs=[spec(4), spec(5), spec(6)],
        out_specs=pl.BlockSpec((t_ctx, HEAD_DIM), lambda h: (0, h)),
        out_shape=jax.ShapeDtypeStruct((t_ctx, HALF), F32),
        compiler_params=_cparams(1),
    )(proj, proj, proj)


def _gla_inter(q_ref, i_ref, f_ref, lb_ref, out_ref, st_ref, k_scr, b_scr, *, rev):
    n = q_ref.shape[0]
    mask = _scan_mask(rev, n)
    tri = mask.astype(BF16)
    last = 0 if rev else n - 1
    x = f_ref[...]
    key = (1.0 - lb_ref[...]) * _sigmoid(-x)
    b = sum(_nn(tri, p) for p in _split3(jnp.log1p(-key)))
    b_end = b[last:last + 1, :]
    qx = q_ref[...]
    q = qx * _sigmoid(qx)
    qt = q * jnp.exp(b)
    kdec = key * jnp.exp(b_end - b)
    w_end = jnp.exp(b_end)
    k_scr[...] = key
    b_scr[...] = b
    for h in range(N_HEADS):
        hs = slice(h * HEAD_DIM, (h + 1) * HEAD_DIM)
        st = st_ref[h]
        vb = i_ref[:, hs].astype(BF16)
        out_ref[:, hs] = _nt(qt[:, hs].astype(BF16), st.astype(BF16))
        st_ref[h] = st * w_end[:, hs] + _tn(vb, kdec[:, hs].astype(BF16))
    return mask, q, b_end


def _gla_intra_factored(mask, q, b_end, i_ref, out_ref, k_scr, b_scr):
    mid = 0.5 * b_end
    qm = q * jnp.exp(b_scr[...] - mid)
    km = k_scr[...] * jnp.exp(mid - b_scr[...])
    for h in range(N_HEADS):
        hs = slice(h * HEAD_DIM, (h + 1) * HEAD_DIM)
        att = jnp.where(mask, _nt(qm[:, hs].astype(BF16), km[:, hs].astype(BF16)), 0.0)
        out_ref[:, hs] += _nn(att.astype(BF16), i_ref[:, hs].astype(BF16))


def _gla_intra_exact(q, i_ref, out_ref, k_scr, b_scr, *, rev):
    n = q.shape[0]
    t_idx = lax.broadcasted_iota(jnp.int32, (n, 1), 0)
    for h in range(N_HEADS):
        hs = slice(h * HEAD_DIM, (h + 1) * HEAD_DIM)
        q_h = q[:, hs]
        b_h = b_scr[:, hs]

        def body(grp, acc):
            base = pl.multiple_of(grp * 8, 8)
            b8, k8, v8 = b_scr[pl.ds(base, 8), hs], k_scr[pl.ds(base, 8), hs], i_ref[pl.ds(base, 8), hs]
            for r in range(8):
                s = base + r
                e = jnp.exp(jnp.minimum(b_h - b8[r:r + 1, :], 0.0))
                a = jnp.sum(q_h * k8[r:r + 1, :] * e, axis=1, keepdims=True)
                a = jnp.where((t_idx <= s) if rev else (t_idx >= s), a, 0.0)
                acc = acc + a * v8[r:r + 1, :]
            return acc

        out_ref[:, hs] += lax.fori_loop(0, n // 8, body, jnp.zeros((n, HEAD_DIM), F32))


def _gla_kernel(*refs):
    fw_in, bw_in, lb_ref = refs[0:3], refs[3:6], refs[6]
    out_f, out_b = refs[7:9]
    st_f, k_f, b_f, st_b, k_b, b_b = refs[9:]

    @pl.when(pl.program_id(0) == 0)
    def _():
        st_f[...] = jnp.zeros_like(st_f)
        st_b[...] = jnp.zeros_like(st_b)

    mask_f, q_f, end_f = _gla_inter(*fw_in, lb_ref, out_f, st_f, k_f, b_f, rev=False)
    mask_b, q_b, end_b = _gla_inter(*bw_in, lb_ref, out_b, st_b, k_b, b_b, rev=True)
    safe = jnp.minimum(jnp.min(end_f), jnp.min(end_b)) >= 2.0 * GLA_SAFE_LOG

    @pl.when(safe)
    def _():
        _gla_intra_factored(mask_f, q_f, end_f, fw_in[1], out_f, k_f, b_f)
        _gla_intra_factored(mask_b, q_b, end_b, bw_in[1], out_b, k_b, b_b)

    @pl.when(jnp.logical_not(safe))
    def _():
        _gla_intra_exact(q_f, fw_in[1], out_f, k_f, b_f, rev=False)
        _gla_intra_exact(q_b, bw_in[1], out_b, k_b, b_b, rev=True)


def _gla(proj, lb, n_lat_chunks):
    t_all = proj.shape[0]
    n = GLA_CHUNK
    n_chunks = t_all // n

    def dir_specs(rev):
        blk = _scan_block(rev, n_chunks, n_lat_chunks)
        colblk = lambda c: pl.BlockSpec((n, HALF), lambda s, c=c: (blk(s), c))
        return [colblk(0), colblk(1), colblk(3 if rev else 2)], pl.BlockSpec((n, HALF), lambda s: (blk(s), 0))

    (in_f, out_f), (in_b, out_b) = dir_specs(False), dir_specs(True)
    state = [pltpu.VMEM((N_HEADS, HEAD_DIM, HEAD_DIM), F32), pltpu.VMEM((n, HALF), F32),
             pltpu.VMEM((n, HALF), F32)]
    return pl.pallas_call(
        _gla_kernel,
        grid=(n_chunks,),
        in_specs=in_f + in_b + [pl.BlockSpec((1, HALF), lambda s: (0, 0))],
        out_specs=[out_f, out_b],
        out_shape=[jax.ShapeDtypeStruct((t_all, HALF), F32)] * 2,
        scratch_shapes=state + state,
        compiler_params=_cparams(1),
    )(*([proj] * 6), lb.reshape(1, HALF))


def _s5_mats(a_re, a_im, log_dt, b_re, b_im, c_re, c_im):
    hp = lax.Precision.HIGHEST
    n = S5_CHUNK
    dt = jnp.exp(log_dt)[..., None]
    la_re, la_im = a_re * dt, a_im * dt
    mag = jnp.exp(la_re)
    ab_re, ab_im = mag * jnp.cos(la_im), mag * jnp.sin(la_im)
    nr, ni = ab_re - 1.0, ab_im
    den = jnp.square(a_re) + jnp.square(a_im)
    cr = (nr * a_re + ni * a_im) / den
    ci = (ni * a_re - nr * a_im) / den
    bb_re = cr[..., None] * b_re - ci[..., None] * b_im
    bb_im = cr[..., None] * b_im + ci[..., None] * b_re

    def apow(tau):
        tau = jnp.asarray(tau, F32)[None, None, :, None]
        m = jnp.exp(tau * la_re[:, :, None, :])
        return m * jnp.cos(tau * la_im[:, :, None, :]), m * jnp.sin(tau * la_im[:, :, None, :])

    def c_apow(tau):
        pr, pi = apow(tau)
        return (c_re[:, :, None] * pr[:, :, :, None] - c_im[:, :, None] * pi[:, :, :, None],
                c_re[:, :, None] * pi[:, :, :, None] + c_im[:, :, None] * pr[:, :, :, None])

    ca_re, ca_im = c_apow(np.arange(n))
    kern = (jnp.einsum('dgtip,dgpj->dgtij', ca_re, bb_re, precision=hp)
            - jnp.einsum('dgtip,dgpj->dgtij', ca_im, bb_im, precision=hp))
    s_idx, t_idx = np.arange(n)[:, None], np.arange(n)[None, :]
    g = a_re.shape[1]
    lp = n * S5_GROUP

    def toeplitz(kd, lag):
        onehot = jnp.asarray(lag[None] == np.arange(n)[:, None, None], F32)
        return jnp.einsum('lst,glij->gsjti', onehot, kd, precision=hp).reshape(g, lp, lp)

    kmat = toeplitz(kern[0], t_idx - s_idx) + toeplitz(kern[1], s_idx - t_idx)

    def emat(d, tau):
        pr, pi = apow(tau)
        pr, pi = pr[d][:, :, None, :], pi[d][:, :, None, :]
        br, bi = bb_re[d].transpose(0, 2, 1)[:, None], bb_im[d].transpose(0, 2, 1)[:, None]
        return jnp.concatenate([pr * br - pi * bi, pr * bi + pi * br], axis=-1).reshape(g, lp, 2 * S5_STATE)

    def fmat(d, tau):
        fr, fi = c_apow(tau)
        fr, fi = fr[d], fi[d]
        return jnp.concatenate([fr, -fi], axis=-1).transpose(0, 3, 1, 2).reshape(g, 2 * S5_STATE, lp)

    e_all = jnp.concatenate([emat(0, n - 1 - np.arange(n)), emat(1, np.arange(n))], axis=-1)
    f_all = jnp.concatenate([fmat(0, np.arange(n) + 1), fmat(1, n - np.arange(n))], axis=1)
    pr, pi = apow(np.array([n]))
    pr, pi = pr[:, :, 0], pi[:, :, 0]
    ar2 = jnp.concatenate([pr[0], pr[0], pr[1], pr[1]], axis=-1)
    ai2 = jnp.concatenate([-pi[0], pi[0], -pi[1], pi[1]], axis=-1)
    return kmat.astype(BF16), e_all.astype(BF16), f_all.astype(BF16), ar2, ai2


def _s5_state_kernel(u_ref, e_ref, s_ref):
    s_ref[...] = _nn(u_ref[...], e_ref[...])


def _s5_scan_kernel(sf_ref, sb_ref, ar_ref, ai_ref, hf_ref, hb_ref, stf_ref, stb_ref):
    nb = sf_ref.shape[0]
    w = 2 * S5_STATE
    ar, ai = ar_ref[...], ai_ref[...]

    @pl.when(pl.program_id(0) == 0)
    def _():
        stf_ref[...] = jnp.zeros_like(stf_ref)
        stb_ref[...] = jnp.zeros_like(stb_ref)

    def step(h, s, lo):
        return ar[:, lo:lo + w] * h + ai[:, lo:lo + w] * pltpu.roll(h, S5_STATE, 1) + s

    hf, hb = stf_ref[...], stb_ref[...]
    for j in range(nb):
        hf_ref[j] = hf
        hf = step(hf, sf_ref[j], 0)
        hb_ref[nb - 1 - j] = hb
        hb = step(hb, sb_ref[nb - 1 - j], w)
    stf_ref[...] = hf
    stb_ref[...] = hb


def _s5_out_kernel(u_ref, k_ref, h_ref, f_ref, y_ref):
    y_ref[...] = _nn(u_ref[...], k_ref[...]) + _nn(h_ref[...].astype(BF16), f_ref[...])


def _s5(u, mats, n_lat_chunks):
    kmat, emat, fmat, ar2, ai2 = mats
    t_all = u.shape[0]
    n = S5_CHUNK
    nc = t_all // n
    g = S5_GROUPS
    lp = n * S5_GROUP
    w4 = 4 * S5_STATE
    ug = u.astype(BF16).reshape(nc, n, g, S5_GROUP).transpose(2, 0, 1, 3).reshape(g, nc, lp)
    grp = lambda a, b: pl.BlockSpec((None, a, b), lambda i: (i, 0, 0))
    s_end = pl.pallas_call(
        _s5_state_kernel, grid=(g,),
        in_specs=[grp(nc, lp), grp(lp, w4)], out_specs=grp(nc, w4),
        out_shape=jax.ShapeDtypeStruct((g, nc, w4), F32), compiler_params=_cparams(1),
    )(ug, emat)
    nb = S5_SCAN_BLOCK
    w2 = 2 * S5_STATE
    fw_blk = _scan_block(False, nc // nb, n_lat_chunks // nb)
    bw_blk = _scan_block(True, nc // nb, n_lat_chunks // nb)
    s_t = s_end.transpose(1, 0, 2)
    coef = pl.BlockSpec((g, w4), lambda i: (0, 0))
    h_fw, h_bw = pl.pallas_call(
        _s5_scan_kernel, grid=(nc // nb,),
        in_specs=[pl.BlockSpec((nb, g, w2), lambda i: (fw_blk(i), 0, 0)),
                  pl.BlockSpec((nb, g, w2), lambda i: (bw_blk(i), 0, 1)), coef, coef],
        out_specs=[pl.BlockSpec((nb, g, w2), lambda i: (fw_blk(i), 0, 0)),
                   pl.BlockSpec((nb, g, w2), lambda i: (bw_blk(i), 0, 0))],
        out_shape=[jax.ShapeDtypeStruct((nc, g, w2), F32)] * 2,
        scratch_shapes=[pltpu.VMEM((g, w2), F32)] * 2,
        compiler_params=_cparams(1),
    )(s_t, s_t, ar2, ai2)
    h_in = jnp.concatenate([h_fw, h_bw], axis=-1).transpose(1, 0, 2)
    y = pl.pallas_call(
        _s5_out_kernel, grid=(g,),
        in_specs=[grp(nc, lp), grp(lp, lp), grp(nc, w4), grp(w4, lp)], out_specs=grp(nc, lp),
        out_shape=jax.ShapeDtypeStruct((g, nc, lp), F32), compiler_params=_cparams(1),
    )(ug, kmat, h_in, fmat)
    return y.reshape(g, nc, n, S5_GROUP).transpose(1, 2, 0, 3).reshape(t_all, HALF)


def _top_rows(x, k):
    rows = []
    for _ in range(k):
        mx = jnp.max(x, axis=0, keepdims=True)
        rows.append(mx)
        x = jnp.where(x == mx, NEG, x)
    return rows


def _peer_route_kernel(ht_ref, wq_ref, keys_ref, e1_ref, e0_ref, th_ref, top_scr):
    kk = PEER_TOPK
    qt = _nn(wq_ref[...], ht_ref[...])
    for hp in range(2 * PEER_HEADS):
        sc = _nn(keys_ref[hp], qt[hp * HEAD_DIM:(hp + 1) * HEAD_DIM, :].astype(BF16))
        h, p = divmod(hp, 2)
        if p == 0:
            e0_ref[h] = sc
        else:
            e1_ref[h] = sc
        for r, mx in enumerate(_top_rows(sc, kk + 1)):
            top_scr[hp, r:r + 1, :] = mx
        top_scr[hp, kk + 1:, :] = jnp.full((PEER_TOP_ROWS - kk - 1, sc.shape[1]), NEG, F32)

    row = lax.broadcasted_iota(jnp.int32, (PEER_TOP_ROWS, 1), 0)
    for h in range(PEER_HEADS):
        a = top_scr[2 * h]
        b = top_scr[2 * h + 1]
        parts = [a[0:1] + b, a[1:2] + b[0:8], a[2:3] + b[0:8], a[3:4] + b[0:8]]
        parts += [jnp.where(row >= 4, a + b[j:j + 1], NEG) for j in range(3)]
        cand = jnp.maximum(jnp.concatenate(parts, axis=0), NEG)
        top = _top_rows(cand, kk + 1)
        thr = 0.5 * (top[kk - 1] + top[kk])
        z = jnp.sum(jnp.where(cand >= thr, jnp.exp(cand - top[0]), 0.0), axis=0, keepdims=True)
        e0_ref[h] = jnp.exp(e0_ref[h] - a[0:1]) / z
        e1_ref[h] = jnp.exp(e1_ref[h] - b[0:1])
        th_ref[h:h + 1, :] = jnp.exp(thr - top[0]) / z


def _peer_expert_kernel(ht_ref, e1_ref, e0_ref, th_ref, *rest):
    u_refs = rest[:PEER_U_SPLIT]
    vt_ref, o_ref, at0_scr, at1_scr, g0_scr, g1_scr = rest[PEER_U_SPLIT:]
    s = pl.program_id(1)
    n_tiles = pl.num_programs(1) - 2
    ni = u_refs[0].shape[0] // PEER_N_KEYS
    kc = u_refs[0].shape[1]

    @pl.when(s == 0)
    def _():
        o_ref[...] = jnp.zeros_like(o_ref)
        for scr in (at0_scr, at1_scr, g0_scr, g1_scr):
            scr[...] = jnp.zeros_like(scr)

    tile = jnp.clip(s - 1, 0, n_tiles - 1)

    def stages(at_new, at_old, g_new, g_old):
        at_new[...] = sum(_nn(u_k[...], ht_ref[k * kc:(k + 1) * kc, :]) for k, u_k in enumerate(u_refs))
        nj, nc = PEER_GATE_ROWS, 128
        for ii in range(ni):
            i = tile * ni + ii
            e0_rows = [e0_ref[h, pl.ds(i, 1), :] for h in range(PEER_HEADS)]
            for c in range(at_old.shape[1] // nc):
                cs = slice(c * nc, (c + 1) * nc)
                for jb in range(PEER_N_KEYS // nj):
                    js = slice(jb * nj, (jb + 1) * nj)
                    w = None
                    for h in range(PEER_HEADS):
                        p = e1_ref[h, js, cs] * e0_rows[h][:, cs]
                        t = jnp.where(p >= th_ref[h:h + 1, cs], p, 0.0)
                        w = t if w is None else w + t
                    rs = slice(ii * PEER_N_KEYS + jb * nj, ii * PEER_N_KEYS + (jb + 1) * nj)
                    g_new[rs, cs] = (_gelu_tanh(at_old[rs, cs]) * w).astype(BF16)
        o_ref[...] += _nn(vt_ref[...], g_old[...])

    @pl.when(s % 2 == 0)
    def _():
        stages(at0_scr, at1_scr, g1_scr, g0_scr)

    @pl.when(s % 2 == 1)
    def _():
        stages(at1_scr, at0_scr, g0_scr, g1_scr)


def _tok3(t):
    return pl.BlockSpec((PEER_HEADS, PEER_N_KEYS, t), lambda i, *_: (0, 0, i))


def _peer_route(h2t, wq_t_bf, keys_bf):
    d, t_all = h2t.shape
    nh, nk = PEER_HEADS, PEER_N_KEYS
    t1 = PEER_P1_TILE
    return pl.pallas_call(
        _peer_route_kernel,
        grid=(t_all // t1,),
        in_specs=[pl.BlockSpec((d, t1), lambda i: (0, i)),
                  pl.BlockSpec(wq_t_bf.shape, lambda i: (0, 0)),
                  pl.BlockSpec(keys_bf.shape, lambda i: (0, 0, 0))],
        out_specs=[_tok3(t1), _tok3(t1), pl.BlockSpec((nh, t1), lambda i: (0, i))],
        out_shape=[jax.ShapeDtypeStruct((nh, nk, t_all), F32)] * 2 + [jax.ShapeDtypeStruct((nh, t_all), F32)],
        scratch_shapes=[pltpu.VMEM((2 * nh, PEER_TOP_ROWS, t1), F32)],
        compiler_params=_cparams(1),
    )(h2t, wq_t_bf, keys_bf)


def _peer(h2t, wq_t_bf, keys_bf, u_bf, vt_bf):
    t_all = h2t.shape[1]
    h2t = jnp.pad(h2t, ((0, 0), (0, -t_all % PEER_TOK_TILE)))
    return _peer_experts(h2t, *_peer_route(h2t, wq_t_bf, keys_bf), u_bf, vt_bf)


def _peer_tables(u_tab, v_tab):
    n_exp, d = u_tab.shape
    et = PEER_EXP_TILE
    u_r = u_tab.astype(BF16).reshape(n_exp // et, et, PEER_U_SPLIT, d // PEER_U_SPLIT).transpose(0, 2, 1, 3)
    vt_r = v_tab.astype(BF16).reshape(n_exp // et, et, d).transpose(0, 2, 1)
    return u_r, vt_r


def _peer_experts(h2t, e1, e0, th, u_r, vt_r):
    d, t_all = h2t.shape
    tt = PEER_TOK_TILE
    n_tiles, _, et, kc = u_r.shape
    tok = pl.BlockSpec((d, tt), lambda i, s: (0, i))
    return pl.pallas_call(
        _peer_expert_kernel,
        grid=(t_all // tt, n_tiles + 2),
        in_specs=[tok, _tok3(tt), _tok3(tt), pl.BlockSpec((PEER_HEADS, tt), lambda i, s: (0, i))]
        + [pl.BlockSpec((None, None, et, kc), lambda i, s, k=k: (jnp.minimum(s, n_tiles - 1), k, 0, 0))
           for k in range(PEER_U_SPLIT)]
        + [pl.BlockSpec((None, d, et), lambda i, s: (jnp.clip(s - 2, 0, n_tiles - 1), 0, 0))],
        out_specs=tok,
        out_shape=jax.ShapeDtypeStruct((d, t_all), F32),
        scratch_shapes=[pltpu.VMEM((et, tt), F32)] * 2 + [pltpu.VMEM((et, tt), BF16)] * 2,
        compiler_params=_cparams(2),
    )(h2t, e1, e0, th, *([u_r] * PEER_U_SPLIT), vt_r)


def _rope_tables(t_lat, t_ctx):
    pos = np.arange(t_lat)
    nfreq = HEAD_DIM // 4
    freqs = ROPE_BASE ** (-jnp.arange(nfreq, dtype=F32) / nfreq)
    ang_r = jnp.asarray(pos // GRID_W, F32)[:, None] * freqs
    ang_c = jnp.asarray(pos % GRID_W, F32)[:, None] * freqs
    cos = jnp.concatenate([jnp.cos(ang_r), jnp.cos(ang_r), jnp.cos(ang_c), jnp.cos(ang_c)], axis=1)
    sin = jnp.concatenate([-jnp.sin(ang_r), jnp.sin(ang_r), -jnp.sin(ang_c), jnp.sin(ang_c)], axis=1)
    cos = jnp.concatenate([cos, jnp.ones((t_ctx, HEAD_DIM), F32)], axis=0)
    sin = jnp.concatenate([sin, jnp.zeros((t_ctx, HEAD_DIM), F32)], axis=0)
    return cos, sin


def _peer_and_norm(x1, h2, mods, ln_g, ln_b, n_lat_tiles, w_q, sub_keys, u_tab, v_tab, mods_next=None):
    keys = sub_keys.reshape(2 * PEER_HEADS, PEER_N_KEYS, -1).astype(BF16)
    pt = _peer(h2.T, w_q.T.astype(BF16), keys, *_peer_tables(u_tab, v_tab))
    return _ln2(x1, pt, mods, ln_g, ln_b, n_lat_tiles, mods_next)


def _layer_ab(x_all, mods, t_lat, t_ctx, w_in, w_out, igate_b, fgate_b, rpb, ln_g, ln_b, xm=None):
    n_lat_tiles = t_lat // ROW_TILE
    if xm is None:
        xm = _modulate(x_all, mods, n_lat_tiles)
    ng = 4 * N_HEADS
    w_main = jnp.concatenate([w_in[:, :4 * HALF], w_in[:, 4 * HALF + ng:]], axis=1).astype(BF16)
    w_gate = jnp.pad(w_in[:, 4 * HALF:4 * HALF + ng], ((0, 0), (0, 128 - ng))).astype(BF16)
    proj = _inproj(xm, w_main)
    gates = _inproj(xm, w_gate)
    cos_t, sin_t = _rope_tables(t_lat, t_ctx)
    h_fw, h_bw = _mlstm(proj, gates, gates.T, cos_t, sin_t, igate_b, fgate_b, t_lat // SCAN_CHUNK)
    y_na = jnp.concatenate([_na(proj, rpb, t_lat, t_ctx), _ctx_attn(proj, t_lat, t_ctx)], axis=0)
    return _outproj([(h_fw, 0), (h_bw, 0), (proj, 3)], [(y_na, 0)], w_out.astype(BF16), x_all, mods,
                    ln_g, ln_b, n_lat_tiles)


def _layer_cd(x_all, mods, t_lat, w_in, w_out, lb, s5p, s5_d, glu_w, glu_b, ln_g, ln_b, xm=None):
    n_lat_tiles = t_lat // ROW_TILE
    if xm is None:
        xm = _modulate(x_all, mods, n_lat_tiles)
    proj = _inproj(xm, w_in.astype(BF16))
    o_fw, o_bw = _gla(proj, lb, t_lat // GLA_CHUNK)
    y5 = _s5(proj[:, 5 * HALF:], _s5_mats(*[p.astype(F32) for p in s5p]), t_lat // S5_CHUNK)
    return _outproj([(o_fw, 0), (o_bw, 0), (proj, 4)], [(y5, 0), (proj, 5)], w_out.astype(BF16), x_all, mods,
                    ln_g, ln_b, n_lat_tiles,
                    glu_params=(s5_d.astype(F32), glu_w.astype(BF16), glu_b.astype(F32)))


def kernel(x, c, ctx, c_ctx, ada_w, ada_b, ln_g, ln_b, ab_w_in, ab_w_out, mlstm_igate_b, mlstm_fgate_b, na_rpb, cd_w_in, cd_w_out, hgrn_lb_logits, s5_a_re, s5_a_im, s5_log_dt, s5_b_re, s5_b_im, s5_c_re, s5_c_im, s5_d, s5_glu_w, s5_glu_b, peer_w_q, peer_sub_keys, peer_u, peer_v):
    t_lat, t_ctx = x.shape[1], ctx.shape[1]
    assert x.shape[0] == 1 and t_ctx == ROW_TILE and t_lat % (NA_QROWS * GRID_W) == 0
    depth = ada_w.shape[0]
    lb_soft = jax.nn.softmax(hgrn_lb_logits.astype(F32), axis=0)
    lower_bounds = jnp.cumsum(lb_soft, axis=0) - lb_soft[0]
    x_all = jnp.concatenate([x[0], ctx[0]], axis=0).astype(F32)
    n_lat_tiles = t_lat // ROW_TILE
    mods_all = [_adaln(c, c_ctx, ada_w[l], ada_b[l]) for l in range(depth)]
    xm = _modulate(x_all, mods_all[0], n_lat_tiles)
    for l in range(depth):
        j = l // 2
        mods = mods_all[l]
        if l % 2 == 0:
            x1, h2 = _layer_ab(x_all, mods, t_lat, t_ctx, ab_w_in[j], ab_w_out[j], mlstm_igate_b[j],
                               mlstm_fgate_b[j], na_rpb[j], ln_g[l, 0], ln_b[l, 0], xm=xm)
        else:
            s5p = (s5_a_re[j], s5_a_im[j], s5_log_dt[j], s5_b_re[j], s5_b_im[j], s5_c_re[j], s5_c_im[j])
            x1, h2 = _layer_cd(x_all, mods, t_lat, cd_w_in[j], cd_w_out[j], lower_bounds[l], s5p, s5_d[j],
                               s5_glu_w[j], s5_glu_b[j], ln_g[l, 0], ln_b[l, 0], xm=xm)
        peer_args = (x1, h2, mods, ln_g[l, 1], ln_b[l, 1], n_lat_tiles, peer_w_q[l], peer_sub_keys[l],
                     peer_u[l], peer_v[l])
        if l + 1 < depth:
            x_all, xm = _peer_and_norm(*peer_args, mods_next=mods_all[l + 1])
        else:
            x_all = _peer_and_norm(*peer_args)
    return x_all[:t_lat][None].astype(x.dtype)
```

```python
import functools
import math

import numpy as np
import jax
import jax.numpy as jnp
from jax import lax
from jax.experimental import pallas as pl
from jax.experimental.pallas import tpu as pltpu

F32 = jnp.float32
BF16 = jnp.bfloat16

HEAD_DIM = 128
GRID_W = 64
N_HEADS = 8
HALF = N_HEADS * HEAD_DIM
NA_WIN_ROWS = 8
NA_WIN_COLS = 16
ROPE_BASE = 10000.0
S5_GROUP = 16
S5_STATE = 64
S5_GROUPS = HALF // S5_GROUP
PEER_HEADS = 8
PEER_N_KEYS = 128
PEER_TOPK = 16
PEER_TOP_ROWS = 24
LN_EPS = 1e-5
DEPTH = 2
ALPHA = (2.0 * DEPTH) ** 0.25
QK_SCALE = HEAD_DIM ** -0.5

ROW_TILE = 256
MM_ROW_TILE = 1280
MM_COL_TILE = 1024
SCAN_CHUNK = 128
GLA_CHUNK = 64
S5_CHUNK = 32
S5_SCAN_BLOCK = 8
NA_QROWS = 8
PEER_P1_TILE = 256
PEER_TOK_TILE = 512
PEER_EXP_TILE = 512
PEER_GATE_ROWS = 64
PEER_U_SPLIT = 4
NEG = -1e30
GLA_SAFE_LOG = -80.0
VMEM_LIMIT = 56 * 1024 * 1024


def _cparams(n_axes):
    return pltpu.CompilerParams(dimension_semantics=("arbitrary",) * n_axes,
                                vmem_limit_bytes=VMEM_LIMIT)


def _nt(a, b):
    return lax.dot_general(a, b, (((1,), (1,)), ((), ())), preferred_element_type=F32)


def _tn(a, b):
    return lax.dot_general(a, b, (((0,), (0,)), ((), ())), preferred_element_type=F32)


def _nn(a, b):
    return jnp.dot(a, b, preferred_element_type=F32)


def _split3(x):
    hi = x.astype(BF16)
    r = x - hi.astype(F32)
    mid = r.astype(BF16)
    lo = (r - mid.astype(F32)).astype(BF16)
    return hi, mid, lo


def _log_sigmoid(x):
    return jnp.minimum(x, 0.0) - jnp.log1p(jnp.exp(-jnp.abs(x)))


def _sigmoid(x):
    return 1.0 / (1.0 + jnp.exp(-x))


def _gelu_tanh(x):
    return 0.5 * x * (1.0 + jnp.tanh(math.sqrt(2.0 / math.pi) * (x + 0.044715 * (x * x * x))))


def _head_norm(x):
    mu = jnp.mean(x, axis=-1, keepdims=True)
    xc = x - mu
    return xc * lax.rsqrt(jnp.mean(xc * xc, axis=-1, keepdims=True) + LN_EPS)


def _adaln_kernel(ct_ref, w_ref, b_ref, o_ref):
    ct = ct_ref[...]
    s = ct * _sigmoid(ct)
    w = w_ref[...]
    o_ref[0:1, :] = jnp.sum(s[:, 0:1] * w, axis=0, keepdims=True) + b_ref[...]
    o_ref[1:2, :] = jnp.sum(s[:, 1:2] * w, axis=0, keepdims=True) + b_ref[...]


def _adaln(c_lat, c_ctx, w_all, b_all, layer):
    n_layers, d, n = w_all.shape
    tn = n // 8
    ct = jnp.stack([c_lat.reshape(d), c_ctx.reshape(d)], axis=1)
    out = pl.pallas_call(
        _adaln_kernel,
        grid=(n // tn,),
        in_specs=[pl.BlockSpec((d, 2), lambda j: (0, 0)),
                  pl.BlockSpec((None, d, tn), lambda j: (layer, 0, j)),
                  pl.BlockSpec((None, 1, tn), lambda j: (layer, 0, j))],
        out_specs=pl.BlockSpec((2, tn), lambda j: (0, j)),
        out_shape=jax.ShapeDtypeStruct((2, n), F32),
        compiler_params=_cparams(1),
    )(ct, w_all, b_all.reshape(n_layers, 1, n))
    return jnp.pad(out.reshape(2, 6, d), ((0, 0), (0, 2), (0, 0)))


def _modulated(x, m_ref):
    return (x * (1.0 + m_ref[1:2, :]) + m_ref[0:1, :]).astype(BF16)


def _modulate_kernel(x_ref, m_ref, o_ref):
    o_ref[...] = _modulated(x_ref[...], m_ref)


def _mods_spec(d, n_lat_tiles):
    return pl.BlockSpec((None, 8, d), lambda i: (jnp.where(i < n_lat_tiles, 0, 1), 0, 0))


def _modulate(x_all, mods, n_lat_tiles):
    t_all, d = x_all.shape
    row = pl.BlockSpec((ROW_TILE, d), lambda i: (i, 0))
    return pl.pallas_call(
        _modulate_kernel,
        grid=(t_all // ROW_TILE,),
        in_specs=[row, _mods_spec(d, n_lat_tiles)],
        out_specs=row,
        out_shape=jax.ShapeDtypeStruct((t_all, d), BF16),
        compiler_params=_cparams(1),
    )(x_all, mods)


def _matmul_kernel(x_ref, w_ref, o_ref):
    o_ref[...] = _nn(x_ref[...], w_ref[...])


def _inproj(xm, w_bf):
    t_all, d = xm.shape
    n = w_bf.shape[1]
    tm = MM_ROW_TILE if t_all % MM_ROW_TILE == 0 else ROW_TILE
    tn = min(n, MM_COL_TILE)
    return pl.pallas_call(
        _matmul_kernel,
        grid=(t_all // tm, n // tn),
        in_specs=[pl.BlockSpec((tm, d), lambda i, j: (i, 0)),
                  pl.BlockSpec((d, tn), lambda i, j: (0, j))],
        out_specs=pl.BlockSpec((tm, tn), lambda i, j: (i, j)),
        out_shape=jax.ShapeDtypeStruct((t_all, n), F32),
        compiler_params=_cparams(2),
    )(xm, w_bf)


def _ln_rows(z, g, b):
    mu = jnp.mean(z, axis=-1, keepdims=True)
    zc = z - mu
    return zc * lax.rsqrt(jnp.mean(zc * zc, axis=-1, keepdims=True) + LN_EPS) * g + b


def _outproj_kernel(*refs, glu):
    hf_ref, hb_ref, gate_ref = refs[:3]
    refs = refs[3:]
    if glu:
        (y5_ref, u5_ref, ds_ref, gw_ref, gb_ref, w1_ref, w2_ref, x_ref, m_ref, g_ref, b_ref,
         x1_ref, h2_ref) = refs
        y = _gelu_tanh(y5_ref[...] + ds_ref[...] * u5_ref[...])
        a2 = y * _sigmoid(_nn(y.astype(BF16), gw_ref[...]) + gb_ref[...])
    else:
        a2_ref, w1_ref, w2_ref, x_ref, m_ref, g_ref, b_ref, x1_ref, h2_ref = refs
        a2 = a2_ref[...]
    heads = []
    for h in range(N_HEADS):
        hs = slice(h * HEAD_DIM, (h + 1) * HEAD_DIM)
        gx = gate_ref[:, hs]
        gate = gx * _sigmoid(gx) if glu else _sigmoid(gx)
        heads.append((_head_norm(hf_ref[:, hs] + hb_ref[:, hs]) * gate).astype(BF16))
    a1 = jnp.concatenate(heads, axis=1)
    y = _nn(a1, w1_ref[...]) + _nn(a2.astype(BF16), w2_ref[...])
    x1 = _ln_rows(ALPHA * x_ref[...] + m_ref[2:3, :] * y, g_ref[...], b_ref[...])
    x1_ref[...] = x1
    h2_ref[...] = (x1 * (1.0 + m_ref[4:5, :]) + m_ref[3:4, :]).astype(BF16)


def _outproj(scan_parts, a2_parts, w_out_bf, x_all, mods, ln_g, ln_b, n_lat_tiles, glu_params=None):
    t_all, d = x_all.shape
    half = HALF
    row = lambda i: (i, 0)
    const = lambda i: (0, 0)
    a2_parts = list(scan_parts) + list(a2_parts)
    a2_specs = [pl.BlockSpec((ROW_TILE, half), lambda i, c=c: (i, c)) for _, c in a2_parts]
    a2_args = [a for a, _ in a2_parts]
    if glu_params is not None:
        d_skip, glu_w_bf, glu_b = glu_params
        a2_specs += [pl.BlockSpec((1, half), const), pl.BlockSpec((half, half), const),
                     pl.BlockSpec((1, half), const)]
        a2_args += [d_skip.reshape(1, half), glu_w_bf, glu_b.reshape(1, half)]
    return pl.pallas_call(
        functools.partial(_outproj_kernel, glu=glu_params is not None),
        grid=(t_all // ROW_TILE,),
        in_specs=a2_specs + [
            pl.BlockSpec((half, d), const), pl.BlockSpec((half, d), const),
            pl.BlockSpec((ROW_TILE, d), row),
            pl.BlockSpec((None, 8, d), lambda i: (jnp.where(i < n_lat_tiles, 0, 1), 0, 0)),
            pl.BlockSpec((1, d), const), pl.BlockSpec((1, d), const)],
        out_specs=[pl.BlockSpec((ROW_TILE, d), row), pl.BlockSpec((ROW_TILE, d), row)],
        out_shape=[jax.ShapeDtypeStruct((t_all, d), F32), jax.ShapeDtypeStruct((t_all, d), BF16)],
        compiler_params=_cparams(1),
    )(*a2_args, w_out_bf[:half], w_out_bf[half:], x_all, mods, ln_g.reshape(1, d), ln_b.reshape(1, d))


def _ln2_kernel(x_ref, p_ref, m_ref, g_ref, b_ref, *rest):
    x2 = _ln_rows(ALPHA * x_ref[...] + m_ref[5:6, :] * p_ref[...].T, g_ref[...], b_ref[...])
    if len(rest) == 1:
        rest[0][...] = x2
    else:
        mn_ref, o_ref, xm_ref = rest
        o_ref[...] = x2
        xm_ref[...] = _modulated(x2, mn_ref)


def _ln2(x1, peer_out_t, mods, ln_g, ln_b, n_lat_tiles, mods_next=None):
    t_all, d = x1.shape
    row = pl.BlockSpec((ROW_TILE, d), lambda i: (i, 0))
    col = pl.BlockSpec((d, ROW_TILE), lambda i: (0, i))
    vec = pl.BlockSpec((1, d), lambda i: (0, 0))
    nxt = mods_next is not None
    n_rows = t_all if nxt else n_lat_tiles * ROW_TILE
    return pl.pallas_call(
        _ln2_kernel,
        grid=(n_rows // ROW_TILE,),
        in_specs=[row, col, _mods_spec(d, n_lat_tiles), vec, vec] + ([_mods_spec(d, n_lat_tiles)] if nxt else []),
        out_specs=[row, row] if nxt else row,
        out_shape=([jax.ShapeDtypeStruct((t_all, d), F32), jax.ShapeDtypeStruct((t_all, d), BF16)] if nxt
                   else jax.ShapeDtypeStruct((n_rows, d), F32)),
        compiler_params=_cparams(1),
    )(x1, peer_out_t, mods, ln_g.reshape(1, d), ln_b.reshape(1, d), *([mods_next] if nxt else []))


def _scan_block(rev, n_chunks, n_lat_chunks):
    if rev:
        return lambda s: n_chunks - 1 - s
    n_ctx = n_chunks - n_lat_chunks
    return lambda s: jnp.where(s < n_ctx, s + n_lat_chunks, s - n_ctx)


def _scan_mask(rev, n):
    row = lax.broadcasted_iota(jnp.int32, (n, n), 0)
    col = lax.broadcasted_iota(jnp.int32, (n, n), 1)
    return (col >= row) if rev else (col <= row)


def _rope(x, cos, sin):
    lane = lax.broadcasted_iota(jnp.int32, x.shape, 1)
    partner = jnp.where((lane % 64) < 32, pltpu.roll(x, 96, 1), pltpu.roll(x, 32, 1))
    return x * cos + partner * sin


def _mlstm_kernel(*refs):
    n_in = 7
    fw_in, bw_in = refs[:n_in], refs[n_in:2 * n_in]
    bias_ref, biast_ref, out_f, out_b = refs[2 * n_in:2 * n_in + 4]
    state = refs[2 * n_in + 4:]

    @pl.when(pl.program_id(0) == 0)
    def _():
        for r in state:
            r[...] = jnp.zeros_like(r)

    _mlstm_chunk(*fw_in, bias_ref, biast_ref, out_f, *state[:3], rev=False)
    _mlstm_chunk(*bw_in, bias_ref, biast_ref, out_b, *state[3:], rev=True)


def _mlstm_chunk(q_ref, k_ref, v_ref, g_ref, gt_ref, cos_ref, sin_ref, bias_ref, biast_ref,
                 out_ref, c_ref, n_ref, m_ref, *, rev):
    n = q_ref.shape[0]
    mask = _scan_mask(rev, n)
    tri = mask.astype(BF16)
    pre_c = g_ref[...] + bias_ref[...]
    pre_r = gt_ref[...] + biast_ref[...]
    b_c = sum(_nn(tri, p) for p in _split3(_log_sigmoid(pre_c)))
    b_r = sum(_nt(p, tri) for p in _split3(_log_sigmoid(pre_r)))
    cos = cos_ref[...]
    sin = sin_ref[...]
    last = 0 if rev else n - 1
    off = N_HEADS if rev else 0

    for h in range(N_HEADS):
        hs = slice(h * HEAD_DIM, (h + 1) * HEAD_DIM)
        ci, cf = off + h, 2 * N_HEADS + off + h
        bc, br = b_c[:, cf:cf + 1], b_r[cf:cf + 1, :]
        ic, ir = pre_c[:, ci:ci + 1], pre_r[ci:ci + 1, :]
        m_prev = m_ref[h:h + 1, 0:1]
        logw = jnp.where(mask, bc - br + ir, NEG)
        inter = bc + m_prev
        m_t = jnp.maximum(inter, jnp.max(logw, axis=1, keepdims=True))
        w_inter = jnp.exp(inter - m_t)
        q = _rope(q_ref[:, hs], cos, sin) * QK_SCALE
        k = _rope(k_ref[:, hs], cos, sin)
        qb, kb, vb = q.astype(BF16), k.astype(BF16), v_ref[:, hs].astype(BF16)
        s = _nt(qb, kb) * jnp.exp(logw - m_t)
        c_old = c_ref[h]
        n_old = n_ref[h:h + 1, :]
        num = _nn(s.astype(BF16), vb) + _nn(qb, c_old.astype(BF16)) * w_inter
        den = jnp.sum(s, axis=1, keepdims=True) + jnp.sum(q * n_old, axis=1, keepdims=True) * w_inter
        den = jnp.maximum(jnp.abs(den), jnp.exp(-m_t))
        hout = num / den
        m_new = m_t[last:last + 1, :]
        b_last = bc[last:last + 1, :]
        w_old = jnp.exp(b_last + m_prev - m_new)
        kw = k * jnp.exp(b_last - bc + ic - m_new)
        c_ref[h] = w_old * c_old + _tn(kw.astype(BF16), vb)
        n_ref[h:h + 1, :] = w_old * n_old + jnp.sum(kw, axis=0, keepdims=True)
        m_ref[h:h + 1, :] = jnp.broadcast_to(m_new, (1, HEAD_DIM))
        out_ref[:, hs] = hout


def _mlstm(proj, gates, gates_t, cos_t, sin_t, igate_b, fgate_b, n_lat_chunks):
    t_all = proj.shape[0]
    n = SCAN_CHUNK
    n_chunks = t_all // n
    bias = jnp.zeros((1, 128), F32).at[0, :4 * N_HEADS].set(
        jnp.concatenate([igate_b.reshape(-1), fgate_b.reshape(-1)]))

    def dir_specs(rev):
        blk = _scan_block(rev, n_chunks, n_lat_chunks)
        colblk = lambda c: pl.BlockSpec((n, HALF), lambda s, c=c: (blk(s), c))
        tok = pl.BlockSpec((n, 128), lambda s: (blk(s), 0))
        ins = [colblk(0), colblk(1), colblk(2), tok, pl.BlockSpec((128, n), lambda s: (0, blk(s))), tok, tok]
        return ins, pl.BlockSpec((n, HALF), lambda s: (blk(s), 0))

    (in_f, out_f), (in_b, out_b) = dir_specs(False), dir_specs(True)
    dir_args = [proj, proj, proj, gates, gates_t, cos_t, sin_t]
    state = [pltpu.VMEM((N_HEADS, HEAD_DIM, HEAD_DIM), F32), pltpu.VMEM((N_HEADS, HEAD_DIM), F32),
             pltpu.VMEM((N_HEADS, HEAD_DIM), F32)]
    return pl.pallas_call(
        _mlstm_kernel,
        grid=(n_chunks,),
        in_specs=in_f + in_b + [pl.BlockSpec((1, 128), lambda s: (0, 0)), pl.BlockSpec((128, 1), lambda s: (0, 0))],
        out_specs=[out_f, out_b],
        out_shape=[jax.ShapeDtypeStruct((t_all, HALF), F32)] * 2,
        scratch_shapes=state + state,
        compiler_params=_cparams(1),
    )(*dir_args, *dir_args, bias, bias.reshape(128, 1))


def _na_kernel(q_ref, kp_ref, kc_ref, kn_ref, vp_ref, vc_ref, vn_ref, kx_ref, vx_ref, bias_ref, o_ref):
    qb = (q_ref[...] * QK_SCALE).astype(BF16)
    kk = jnp.concatenate([kp_ref[...], kc_ref[...], kn_ref[...]], axis=0).astype(BF16)
    vv = jnp.concatenate([vp_ref[...], vc_ref[...], vn_ref[...]], axis=0).astype(BF16)
    s_loc = _nt(qb, kk) + bias_ref[...]
    s_ctx = _nt(qb, kx_ref[...].astype(BF16))
    m = jnp.maximum(jnp.max(s_loc, axis=1, keepdims=True), jnp.max(s_ctx, axis=1, keepdims=True))
    p_loc = jnp.exp(s_loc - m)
    p_ctx = jnp.exp(s_ctx - m)
    l = jnp.sum(p_loc, axis=1, keepdims=True) + jnp.sum(p_ctx, axis=1, keepdims=True)
    o_ref[...] = (_nn(p_loc.astype(BF16), vv) + _nn(p_ctx.astype(BF16), vx_ref[...].astype(BF16))) / l


def _na_bias(rpb, rows):
    qn = NA_QROWS * GRID_W
    kn = 2 * qn
    ql = np.arange(qn)
    kl = np.arange(kn)
    out = []
    nqb = rows // NA_QROWS
    lr, krl = np.arange(NA_QROWS)[:, None], np.arange(2 * NA_QROWS)[None, :]
    dr = np.clip(krl - NA_QROWS // 2 - lr + NA_WIN_ROWS - 1, 0, 2 * NA_WIN_ROWS - 2)
    col = np.arange(GRID_W)
    dc = np.clip(col[None, :] - col[:, None] + NA_WIN_COLS - 1, 0, 2 * NA_WIN_COLS - 2)
    oh_r = jnp.asarray(dr[..., None] == np.arange(2 * NA_WIN_ROWS - 1), F32)
    oh_c = jnp.asarray(dc[..., None] == np.arange(2 * NA_WIN_COLS - 1), F32)
    table = jnp.einsum('hab,lka,qcb->hlqkc', rpb, oh_r, oh_c,
                       precision=lax.Precision.HIGHEST).reshape(rpb.shape[0], qn, kn)
    for qb in (0, 1, nqb - 1):
        qr = NA_QROWS * qb + ql // GRID_W
        qc = ql % GRID_W
        kr = NA_QROWS * qb - NA_QROWS // 2 + kl // GRID_W
        kc = kl % GRID_W
        r0 = np.clip(qr - NA_WIN_ROWS // 2, 0, rows - NA_WIN_ROWS)
        c0 = np.clip(qc - NA_WIN_COLS // 2, 0, GRID_W - NA_WIN_COLS)
        ok = ((kr[None, :] >= r0[:, None]) & (kr[None, :] < r0[:, None] + NA_WIN_ROWS)
              & (kc[None, :] >= c0[:, None]) & (kc[None, :] < c0[:, None] + NA_WIN_COLS)
              & (kr[None, :] >= 0) & (kr[None, :] < rows))
        out.append(jnp.where(jnp.asarray(ok)[None], table, NEG))
    return jnp.stack(out)


def _na(proj, rpb, t_lat, t_ctx):
    rows = t_lat // GRID_W
    qn = NA_QROWS * GRID_W
    hn = qn // 2
    nqb = rows // NA_QROWS
    assert nqb >= 2 and t_ctx == hn
    bias = _na_bias(rpb.astype(F32), rows)
    last_half = 2 * nqb - 1
    ctx_blk = t_lat // hn
    cls = lambda b: jnp.where(b == 0, 0, jnp.where(b == nqb - 1, 2, 1))

    def kv_specs(col0):
        return [pl.BlockSpec((hn, HEAD_DIM), lambda h, b: (jnp.maximum(2 * b - 1, 0), col0 + h)),
                pl.BlockSpec((qn, HEAD_DIM), lambda h, b: (b, col0 + h)),
                pl.BlockSpec((hn, HEAD_DIM), lambda h, b: (jnp.minimum(2 * b + 2, last_half), col0 + h))]

    kcol, vcol = 5 * N_HEADS, 6 * N_HEADS
    return pl.pallas_call(
        _na_kernel,
        grid=(N_HEADS, nqb),
        in_specs=[pl.BlockSpec((qn, HEAD_DIM), lambda h, b: (b, 4 * N_HEADS + h))]
        + kv_specs(kcol) + kv_specs(vcol)
        + [pl.BlockSpec((hn, HEAD_DIM), lambda h, b: (ctx_blk, kcol + h)),
           pl.BlockSpec((hn, HEAD_DIM), lambda h, b: (ctx_blk, vcol + h)),
           pl.BlockSpec((None, None, qn, 2 * qn), lambda h, b: (cls(b), h, 0, 0))],
        out_specs=pl.BlockSpec((qn, HEAD_DIM), lambda h, b: (b, h)),
        out_shape=jax.ShapeDtypeStruct((t_lat, HALF), F32),
        compiler_params=_cparams(2),
    )(proj, proj, proj, proj, proj, proj, proj, proj, proj, bias)


def _ctx_attn_kernel(q_ref, k_ref, v_ref, o_ref):
    qb = (q_ref[...] * QK_SCALE).astype(BF16)
    s = _nt(qb, k_ref[...].astype(BF16))
    p = jnp.exp(s - jnp.max(s, axis=1, keepdims=True))
    o_ref[...] = _nn(p.astype(BF16), v_ref[...].astype(BF16)) / jnp.sum(p, axis=1, keepdims=True)


def _ctx_attn(proj, t_lat, t_ctx):
    blk = t_lat // t_ctx
    spec = lambda c: pl.BlockSpec((t_ctx, HEAD_DIM), lambda h, c=c: (blk, c * N_HEADS + h))
    return pl.pallas_call(
        _ctx_attn_kernel,
        grid=(N_HEADS,),
        in_specs=[spec(4), spec(5), spec(6)],
        out_specs=pl.BlockSpec((t_ctx, HEAD_DIM), lambda h: (0, h)),
        out_shape=jax.ShapeDtypeStruct((t_ctx, HALF), F32),
        compiler_params=_cparams(1),
    )(proj, proj, proj)


def _gla_inter(q_ref, i_ref, f_ref, lb_ref, out_ref, st_ref, k_scr, b_scr, *, rev):
    n = q_ref.shape[0]
    mask = _scan_mask(rev, n)
    tri = mask.astype(BF16)
    last = 0 if rev else n - 1
    x = f_ref[...]
    key = lb_ref[0:1, :] * _sigmoid(-x)
    la, lc = lb_ref[1:2, :], lb_ref[2:3, :] + _log_sigmoid(x)
    logf = jnp.maximum(la, lc) + jnp.log1p(jnp.exp(-jnp.abs(la - lc)))
    b = sum(_nn(tri, p) for p in _split3(logf))
    b_end = b[last:last + 1, :]
    qx = q_ref[...]
    q = qx * _sigmoid(qx)
    qt = q * jnp.exp(b)
    kdec = key * jnp.exp(b_end - b)
    w_end = jnp.exp(b_end)
    k_scr[...] = key
    b_scr[...] = b
    for h in range(N_HEADS):
        hs = slice(h * HEAD_DIM, (h + 1) * HEAD_DIM)
        st = st_ref[h]
        vb = i_ref[:, hs].astype(BF16)
        out_ref[:, hs] = _nt(qt[:, hs].astype(BF16), st.astype(BF16))
        st_ref[h] = st * w_end[:, hs] + _tn(vb, kdec[:, hs].astype(BF16))
    return mask, q, b_end


def _gla_intra_factored(mask, q, b_end, i_ref, out_ref, k_scr, b_scr):
    mid = 0.5 * b_end
    qm = q * jnp.exp(b_scr[...] - mid)
    km = k_scr[...] * jnp.exp(mid - b_scr[...])
    for h in range(N_HEADS):
        hs = slice(h * HEAD_DIM, (h + 1) * HEAD_DIM)
        att = jnp.where(mask, _nt(qm[:, hs].astype(BF16), km[:, hs].astype(BF16)), 0.0)
        out_ref[:, hs] += _nn(att.astype(BF16), i_ref[:, hs].astype(BF16))


def _gla_intra_exact(q, i_ref, out_ref, k_scr, b_scr, *, rev):
    n = q.shape[0]
    t_idx = lax.broadcasted_iota(jnp.int32, (n, 1), 0)
    for h in range(N_HEADS):
        hs = slice(h * HEAD_DIM, (h + 1) * HEAD_DIM)
        q_h = q[:, hs]
        b_h = b_scr[:, hs]

        def body(grp, acc):
            base = pl.multiple_of(grp * 8, 8)
            b8, k8, v8 = b_scr[pl.ds(base, 8), hs], k_scr[pl.ds(base, 8), hs], i_ref[pl.ds(base, 8), hs]
            for r in range(8):
                s = base + r
                e = jnp.exp(jnp.minimum(b_h - b8[r:r + 1, :], 0.0))
                a = jnp.sum(q_h * k8[r:r + 1, :] * e, axis=1, keepdims=True)
                a = jnp.where((t_idx <= s) if rev else (t_idx >= s), a, 0.0)
                acc = acc + a * v8[r:r + 1, :]
            return acc

        out_ref[:, hs] += lax.fori_loop(0, n // 8, body, jnp.zeros((n, HEAD_DIM), F32))


def _gla_kernel(*refs):
    fw_in, bw_in, lb_ref = refs[0:3], refs[3:6], refs[6]
    out_f, out_b = refs[7:9]
    st_f, k_f, b_f, st_b, k_b, b_b = refs[9:]

    @pl.when(pl.program_id(0) == 0)
    def _():
        st_f[...] = jnp.zeros_like(st_f)
        st_b[...] = jnp.zeros_like(st_b)

    mask_f, q_f, end_f = _gla_inter(*fw_in, lb_ref, out_f, st_f, k_f, b_f, rev=False)
    mask_b, q_b, end_b = _gla_inter(*bw_in, lb_ref, out_b, st_b, k_b, b_b, rev=True)
    safe = jnp.minimum(jnp.min(end_f), jnp.min(end_b)) >= 2.0 * GLA_SAFE_LOG

    @pl.when(safe)
    def _():
        _gla_intra_factored(mask_f, q_f, end_f, fw_in[1], out_f, k_f, b_f)
        _gla_intra_factored(mask_b, q_b, end_b, bw_in[1], out_b, k_b, b_b)

    @pl.when(jnp.logical_not(safe))
    def _():
        _gla_intra_exact(q_f, fw_in[1], out_f, k_f, b_f, rev=False)
        _gla_intra_exact(q_b, bw_in[1], out_b, k_b, b_b, rev=True)


def _gla(proj, lb, n_lat_chunks):
    t_all = proj.shape[0]
    n = GLA_CHUNK
    n_chunks = t_all // n

    def dir_specs(rev):
        blk = _scan_block(rev, n_chunks, n_lat_chunks)
        colblk = lambda c: pl.BlockSpec((n, HALF), lambda s, c=c: (blk(s), c))
        return [colblk(0), colblk(1), colblk(3 if rev else 2)], pl.BlockSpec((n, HALF), lambda s: (blk(s), 0))

    (in_f, out_f), (in_b, out_b) = dir_specs(False), dir_specs(True)
    state = [pltpu.VMEM((N_HEADS, HEAD_DIM, HEAD_DIM), F32), pltpu.VMEM((n, HALF), F32),
             pltpu.VMEM((n, HALF), F32)]
    lb = lb.reshape(1, HALF)
    lb_rows = jnp.concatenate([1.0 - lb, jnp.log(lb), jnp.log1p(-lb)], axis=0)
    return pl.pallas_call(
        _gla_kernel,
        grid=(n_chunks,),
        in_specs=in_f + in_b + [pl.BlockSpec((3, HALF), lambda s: (0, 0))],
        out_specs=[out_f, out_b],
        out_shape=[jax.ShapeDtypeStruct((t_all, HALF), F32)] * 2,
        scratch_shapes=state + state,
        compiler_params=_cparams(1),
    )(*([proj] * 6), lb_rows)


def _s5_mats(a_re, a_im, log_dt, b_re, b_im, c_re, c_im):
    hp = lax.Precision.HIGHEST
    n = S5_CHUNK
    dt = jnp.exp(log_dt)[..., None]
    la_re, la_im = a_re * dt, a_im * dt
    mag = jnp.exp(la_re)
    ab_re, ab_im = mag * jnp.cos(la_im), mag * jnp.sin(la_im)
    nr, ni = ab_re - 1.0, ab_im
    den = jnp.square(a_re) + jnp.square(a_im)
    cr = (nr * a_re + ni * a_im) / den
    ci = (ni * a_re - nr * a_im) / den
    bb_re = cr[..., None] * b_re - ci[..., None] * b_im
    bb_im = cr[..., None] * b_im + ci[..., None] * b_re

    def apow(tau):
        tau = jnp.asarray(tau, F32)[None, None, :, None]
        m = jnp.exp(tau * la_re[:, :, None, :])
        return m * jnp.cos(tau * la_im[:, :, None, :]), m * jnp.sin(tau * la_im[:, :, None, :])

    def c_apow(tau):
        pr, pi = apow(tau)
        return (c_re[:, :, None] * pr[:, :, :, None] - c_im[:, :, None] * pi[:, :, :, None],
                c_re[:, :, None] * pi[:, :, :, None] + c_im[:, :, None] * pr[:, :, :, None])

    ca_re, ca_im = c_apow(np.arange(n))
    kern = jnp.einsum('dgtip,dgpj->dgtij', jnp.concatenate([ca_re, -ca_im], axis=-1),
                      jnp.concatenate([bb_re, bb_im], axis=2), precision=hp)
    s_idx, t_idx = np.arange(n)[:, None], np.arange(n)[None, :]
    g = a_re.shape[1]
    lp = n * S5_GROUP

    def toeplitz(kd, lag):
        onehot = jnp.asarray(lag[None] == np.arange(n)[:, None, None], F32)
        return jnp.einsum('lst,glij->gsjti', onehot, kd, precision=hp).reshape(g, lp, lp)

    kmat = toeplitz(kern[0], t_idx - s_idx) + toeplitz(kern[1], s_idx - t_idx)

    def emat(d, tau):
        pr, pi = apow(tau)
        pr, pi = pr[d][:, :, None, :], pi[d][:, :, None, :]
        br, bi = bb_re[d].transpose(0, 2, 1)[:, None], bb_im[d].transpose(0, 2, 1)[:, None]
        return jnp.concatenate([pr * br - pi * bi, pr * bi + pi * br], axis=-1).reshape(g, lp, 2 * S5_STATE)

    def fmat(d, tau):
        fr, fi = c_apow(tau)
        fr, fi = fr[d], fi[d]
        return jnp.concatenate([fr, -fi], axis=-1).transpose(0, 3, 1, 2).reshape(g, 2 * S5_STATE, lp)

    e_all = jnp.concatenate([emat(0, n - 1 - np.arange(n)), emat(1, np.arange(n))], axis=-1)
    f_all = jnp.concatenate([fmat(0, np.arange(n) + 1), fmat(1, n - np.arange(n))], axis=1)
    pr, pi = apow(np.array([n]))
    pr, pi = pr[:, :, 0], pi[:, :, 0]
    ar2 = jnp.concatenate([pr[0], pr[0], pr[1], pr[1]], axis=-1)
    ai2 = jnp.concatenate([-pi[0], pi[0], -pi[1], pi[1]], axis=-1)
    return kmat.astype(BF16), e_all.astype(BF16), f_all.astype(BF16), ar2, ai2


def _s5_state_kernel(u_ref, e_ref, s_ref):
    s_ref[...] = _nn(u_ref[...], e_ref[...])


def _s5_scan_kernel(sf_ref, sb_ref, ar_ref, ai_ref, hf_ref, hb_ref, stf_ref, stb_ref):
    nb = sf_ref.shape[0]
    w = 2 * S5_STATE
    ar, ai = ar_ref[...], ai_ref[...]

    @pl.when(pl.program_id(0) == 0)
    def _():
        stf_ref[...] = jnp.zeros_like(stf_ref)
        stb_ref[...] = jnp.zeros_like(stb_ref)

    def step(h, s, lo):
        return ar[:, lo:lo + w] * h + ai[:, lo:lo + w] * pltpu.roll(h, S5_STATE, 1) + s

    hf, hb = stf_ref[...], stb_ref[...]
    for j in range(nb):
        hf_ref[j] = hf
        hf = step(hf, sf_ref[j], 0)
        hb_ref[nb - 1 - j] = hb
        hb = step(hb, sb_ref[nb - 1 - j], w)
    stf_ref[...] = hf
    stb_ref[...] = hb


def _s5_out_kernel(u_ref, k_ref, h_ref, f_ref, y_ref):
    y_ref[...] = _nn(u_ref[...], k_ref[...]) + _nn(h_ref[...].astype(BF16), f_ref[...])


def _s5(u, mats, n_lat_chunks):
    kmat, emat, fmat, ar2, ai2 = mats
    t_all = u.shape[0]
    n = S5_CHUNK
    nc = t_all // n
    g = S5_GROUPS
    lp = n * S5_GROUP
    w4 = 4 * S5_STATE
    ug = u.astype(BF16).reshape(nc, n, g, S5_GROUP).transpose(2, 0, 1, 3).reshape(g, nc, lp)
    grp = lambda a, b: pl.BlockSpec((None, a, b), lambda i: (i, 0, 0))
    s_end = pl.pallas_call(
        _s5_state_kernel, grid=(g,),
        in_specs=[grp(nc, lp), grp(lp, w4)], out_specs=grp(nc, w4),
        out_shape=jax.ShapeDtypeStruct((g, nc, w4), F32), compiler_params=_cparams(1),
    )(ug, emat)
    nb = S5_SCAN_BLOCK
    w2 = 2 * S5_STATE
    fw_blk = _scan_block(False, nc // nb, n_lat_chunks // nb)
    bw_blk = _scan_block(True, nc // nb, n_lat_chunks // nb)
    s_t = s_end.transpose(1, 0, 2)
    coef = pl.BlockSpec((g, w4), lambda i: (0, 0))
    h_fw, h_bw = pl.pallas_call(
        _s5_scan_kernel, grid=(nc // nb,),
        in_specs=[pl.BlockSpec((nb, g, w2), lambda i: (fw_blk(i), 0, 0)),
                  pl.BlockSpec((nb, g, w2), lambda i: (bw_blk(i), 0, 1)), coef, coef],
        out_specs=[pl.BlockSpec((nb, g, w2), lambda i: (fw_blk(i), 0, 0)),
                   pl.BlockSpec((nb, g, w2), lambda i: (bw_blk(i), 0, 0))],
        out_shape=[jax.ShapeDtypeStruct((nc, g, w2), F32)] * 2,
        scratch_shapes=[pltpu.VMEM((g, w2), F32)] * 2,
        compiler_params=_cparams(1),
    )(s_t, s_t, ar2, ai2)
    h_in = jnp.concatenate([h_fw, h_bw], axis=-1).transpose(1, 0, 2)
    y = pl.pallas_call(
        _s5_out_kernel, grid=(g,),
        in_specs=[grp(nc, lp), grp(lp, lp), grp(nc, w4), grp(w4, lp)], out_specs=grp(nc, lp),
        out_shape=jax.ShapeDtypeStruct((g, nc, lp), F32), compiler_params=_cparams(1),
    )(ug, kmat, h_in, fmat)
    return y.reshape(g, nc, n, S5_GROUP).transpose(1, 2, 0, 3).reshape(t_all, HALF)


def _top_rows(x, k):
    rows = []
    for _ in range(k):
        mx = jnp.max(x, axis=0, keepdims=True)
        rows.append(mx)
        x = jnp.where(x == mx, NEG, x)
    return rows


def _peer_route_kernel(ht_ref, wq_ref, keys_ref, e1_ref, e0_ref, th_ref, top_scr):
    kk = PEER_TOPK
    qt = _nn(wq_ref[...], ht_ref[...])
    for hp in range(2 * PEER_HEADS):
        sc = _nn(keys_ref[hp], qt[hp * HEAD_DIM:(hp + 1) * HEAD_DIM, :].astype(BF16))
        h, p = divmod(hp, 2)
        if p == 0:
            e0_ref[h] = sc
        else:
            e1_ref[h] = sc
        for r, mx in enumerate(_top_rows(sc, kk + 1)):
            top_scr[hp, r:r + 1, :] = mx
        top_scr[hp, kk + 1:, :] = jnp.full((PEER_TOP_ROWS - kk - 1, sc.shape[1]), NEG, F32)

    row = lax.broadcasted_iota(jnp.int32, (PEER_TOP_ROWS, 1), 0)
    for h in range(PEER_HEADS):
        a = top_scr[2 * h]
        b = top_scr[2 * h + 1]
        parts = [a[0:1] + b, a[1:2] + b[0:8], a[2:3] + b[0:8], a[3:4] + b[0:8]]
        parts += [jnp.where(row >= 4, a + b[j:j + 1], NEG) for j in range(3)]
        cand = jnp.maximum(jnp.concatenate(parts, axis=0), NEG)
        top = _top_rows(cand, kk + 1)
        thr = 0.5 * (top[kk - 1] + top[kk])
        z = jnp.sum(jnp.where(cand >= thr, jnp.exp(cand - top[0]), 0.0), axis=0, keepdims=True)
        e0_ref[h] = jnp.exp(e0_ref[h] - a[0:1]) / z
        e1_ref[h] = jnp.exp(e1_ref[h] - b[0:1])
        th_ref[h:h + 1, :] = jnp.exp(thr - top[0]) / z


def _peer_expert_kernel(ht_ref, e1_ref, e0_ref, th_ref, *rest):
    u_refs = rest[:PEER_U_SPLIT]
    vt_ref, o_ref, at0_scr, at1_scr, g0_scr, g1_scr = rest[PEER_U_SPLIT:]
    s = pl.program_id(1)
    n_tiles = pl.num_programs(1) - 2
    ni = u_refs[0].shape[0] // PEER_N_KEYS
    kc = u_refs[0].shape[1]

    @pl.when(s == 0)
    def _():
        o_ref[...] = jnp.zeros_like(o_ref)
        for scr in (at0_scr, at1_scr, g0_scr, g1_scr):
            scr[...] = jnp.zeros_like(scr)

    tile = jnp.clip(s - 1, 0, n_tiles - 1)

    def stages(at_new, at_old, g_new, g_old):
        at_new[...] = sum(_nn(u_k[...], ht_ref[k * kc:(k + 1) * kc, :]) for k, u_k in enumerate(u_refs))
        nj, nc = PEER_GATE_ROWS, 128
        for ii in range(ni):
            i = tile * ni + ii
            e0_rows = [e0_ref[h, pl.ds(i, 1), :] for h in range(PEER_HEADS)]
            for c in range(at_old.shape[1] // nc):
                cs = slice(c * nc, (c + 1) * nc)
                for jb in range(PEER_N_KEYS // nj):
                    js = slice(jb * nj, (jb + 1) * nj)
                    w = None
                    for h in range(PEER_HEADS):
                        p = e1_ref[h, js, cs] * e0_rows[h][:, cs]
                        t = jnp.where(p >= th_ref[h:h + 1, cs], p, 0.0)
                        w = t if w is None else w + t
                    rs = slice(ii * PEER_N_KEYS + jb * nj, ii * PEER_N_KEYS + (jb + 1) * nj)
                    g_new[rs, cs] = (_gelu_tanh(at_old[rs, cs]) * w).astype(BF16)
        o_ref[...] += _nn(vt_ref[...], g_old[...])

    @pl.when(s % 2 == 0)
    def _():
        stages(at0_scr, at1_scr, g1_scr, g0_scr)

    @pl.when(s % 2 == 1)
    def _():
        stages(at1_scr, at0_scr, g0_scr, g1_scr)


def _tok3(t):
    return pl.BlockSpec((PEER_HEADS, PEER_N_KEYS, t), lambda i, *_: (0, 0, i))


def _peer_route(h2t, wq_t_bf, keys_bf):
    d, t_all = h2t.shape
    nh, nk = PEER_HEADS, PEER_N_KEYS
    t1 = PEER_P1_TILE
    return pl.pallas_call(
        _peer_route_kernel,
        grid=(t_all // t1,),
        in_specs=[pl.BlockSpec((d, t1), lambda i: (0, i)),
                  pl.BlockSpec(wq_t_bf.shape, lambda i: (0, 0)),
                  pl.BlockSpec(keys_bf.shape, lambda i: (0, 0, 0))],
        out_specs=[_tok3(t1), _tok3(t1), pl.BlockSpec((nh, t1), lambda i: (0, i))],
        out_shape=[jax.ShapeDtypeStruct((nh, nk, t_all), F32)] * 2 + [jax.ShapeDtypeStruct((nh, t_all), F32)],
        scratch_shapes=[pltpu.VMEM((2 * nh, PEER_TOP_ROWS, t1), F32)],
        compiler_params=_cparams(1),
    )(h2t, wq_t_bf, keys_bf)


def _peer(h2t, wq_t_bf, keys_bf, u_bf, vt_bf):
    t_all = h2t.shape[1]
    h2t = jnp.pad(h2t, ((0, 0), (0, -t_all % PEER_TOK_TILE)))
    return _peer_experts(h2t, *_peer_route(h2t, wq_t_bf, keys_bf), u_bf, vt_bf)


def _peer_tables(u_tab, v_tab):
    return u_tab.astype(BF16), v_tab.T.astype(BF16)


def _peer_experts(h2t, e1, e0, th, u_r, vt_r):
    d, t_all = h2t.shape
    tt = PEER_TOK_TILE
    et = PEER_EXP_TILE
    n_tiles = u_r.shape[0] // et
    kc = d // PEER_U_SPLIT
    tok = pl.BlockSpec((d, tt), lambda i, s: (0, i))
    return pl.pallas_call(
        _peer_expert_kernel,
        grid=(t_all // tt, n_tiles + 2),
        in_specs=[tok, _tok3(tt), _tok3(tt), pl.BlockSpec((PEER_HEADS, tt), lambda i, s: (0, i))]
        + [pl.BlockSpec((et, kc), lambda i, s, k=k: (jnp.minimum(s, n_tiles - 1), k))
           for k in range(PEER_U_SPLIT)]
        + [pl.BlockSpec((d, et), lambda i, s: (0, jnp.clip(s - 2, 0, n_tiles - 1)))],
        out_specs=tok,
        out_shape=jax.ShapeDtypeStruct((d, t_all), F32),
        scratch_shapes=[pltpu.VMEM((et, tt), F32)] * 2 + [pltpu.VMEM((et, tt), BF16)] * 2,
        compiler_params=_cparams(2),
    )(h2t, e1, e0, th, *([u_r] * PEER_U_SPLIT), vt_r)


def _rope_tables(t_lat, t_ctx):
    pos = np.arange(t_lat)
    nfreq = HEAD_DIM // 4
    freqs = ROPE_BASE ** (-jnp.arange(nfreq, dtype=F32) / nfreq)
    ang_r = jnp.asarray(pos // GRID_W, F32)[:, None] * freqs
    ang_c = jnp.asarray(pos % GRID_W, F32)[:, None] * freqs
    cos = jnp.concatenate([jnp.cos(ang_r), jnp.cos(ang_r), jnp.cos(ang_c), jnp.cos(ang_c)], axis=1)
    sin = jnp.concatenate([-jnp.sin(ang_r), jnp.sin(ang_r), -jnp.sin(ang_c), jnp.sin(ang_c)], axis=1)
    cos = jnp.concatenate([cos, jnp.ones((t_ctx, HEAD_DIM), F32)], axis=0)
    sin = jnp.concatenate([sin, jnp.zeros((t_ctx, HEAD_DIM), F32)], axis=0)
    return cos, sin


def _peer_and_norm(x1, h2, mods, ln_g, ln_b, n_lat_tiles, w_q, sub_keys, u_tab, v_tab, mods_next=None):
    keys = sub_keys.reshape(2 * PEER_HEADS, PEER_N_KEYS, -1).astype(BF16)
    pt = _peer(h2.T, w_q.T.astype(BF16), keys, *_peer_tables(u_tab, v_tab))
    return _ln2(x1, pt, mods, ln_g, ln_b, n_lat_tiles, mods_next)


def _layer_ab(x_all, mods, t_lat, t_ctx, w_in, w_out, igate_b, fgate_b, rpb, ln_g, ln_b, xm=None):
    n_lat_tiles = t_lat // ROW_TILE
    if xm is None:
        xm = _modulate(x_all, mods, n_lat_tiles)
    ng = 4 * N_HEADS
    w_main = jnp.concatenate([w_in[:, :4 * HALF], w_in[:, 4 * HALF + ng:]], axis=1).astype(BF16)
    w_gate = jnp.pad(w_in[:, 4 * HALF:4 * HALF + ng], ((0, 0), (0, 128 - ng))).astype(BF16)
    proj = _inproj(xm, w_main)
    gates = _inproj(xm, w_gate)
    cos_t, sin_t = _rope_tables(t_lat, t_ctx)
    h_fw, h_bw = _mlstm(proj, gates, gates.T, cos_t, sin_t, igate_b, fgate_b, t_lat // SCAN_CHUNK)
    y_na = jnp.concatenate([_na(proj, rpb, t_lat, t_ctx), _ctx_attn(proj, t_lat, t_ctx)], axis=0)
    return _outproj([(h_fw, 0), (h_bw, 0), (proj, 3)], [(y_na, 0)], w_out.astype(BF16), x_all, mods,
                    ln_g, ln_b, n_lat_tiles)


def _layer_cd(x_all, mods, t_lat, w_in, w_out, lb, s5p, s5_d, glu_w, glu_b, ln_g, ln_b, xm=None):
    n_lat_tiles = t_lat // ROW_TILE
    if xm is None:
        xm = _modulate(x_all, mods, n_lat_tiles)
    proj = _inproj(xm, w_in.astype(BF16))
    o_fw, o_bw = _gla(proj, lb, t_lat // GLA_CHUNK)
    y5 = _s5(proj[:, 5 * HALF:], _s5_mats(*[p.astype(F32) for p in s5p]), t_lat // S5_CHUNK)
    return _outproj([(o_fw, 0), (o_bw, 0), (proj, 4)], [(y5, 0), (proj, 5)], w_out.astype(BF16), x_all, mods,
                    ln_g, ln_b, n_lat_tiles,
                    glu_params=(s5_d.astype(F32), glu_w.astype(BF16), glu_b.astype(F32)))


def kernel(x, c, ctx, c_ctx, ada_w, ada_b, ln_g, ln_b, ab_w_in, ab_w_out, mlstm_igate_b, mlstm_fgate_b, na_rpb, cd_w_in, cd_w_out, hgrn_lb_logits, s5_a_re, s5_a_im, s5_log_dt, s5_b_re, s5_b_im, s5_c_re, s5_c_im, s5_d, s5_glu_w, s5_glu_b, peer_w_q, peer_sub_keys, peer_u, peer_v):
    t_lat, t_ctx = x.shape[1], ctx.shape[1]
    assert x.shape[0] == 1 and t_ctx == ROW_TILE and t_lat % (NA_QROWS * GRID_W) == 0
    depth = ada_w.shape[0]
    lb_soft = jax.nn.softmax(hgrn_lb_logits.astype(F32), axis=0)
    lower_bounds = jnp.cumsum(lb_soft, axis=0) - lb_soft[0]
    x_all = jnp.concatenate([x[0], ctx[0]], axis=0).astype(F32)
    n_lat_tiles = t_lat // ROW_TILE
    mods_all = [_adaln(c, c_ctx, ada_w, ada_b, l) for l in range(depth)]
    xm = _modulate(x_all, mods_all[0], n_lat_tiles)
    for l in range(depth):
        j = l // 2
        mods = mods_all[l]
        if l % 2 == 0:
            x1, h2 = _layer_ab(x_all, mods, t_lat, t_ctx, ab_w_in[j], ab_w_out[j], mlstm_igate_b[j],
                               mlstm_fgate_b[j], na_rpb[j], ln_g[l, 0], ln_b[l, 0], xm=xm)
        else:
            s5p = (s5_a_re[j], s5_a_im[j], s5_log_dt[j], s5_b_re[j], s5_b_im[j], s5_c_re[j], s5_c_im[j])
            x1, h2 = _layer_cd(x_all, mods, t_lat, cd_w_in[j], cd_w_out[j], lower_bounds[l], s5p, s5_d[j],
                               s5_glu_w[j], s5_glu_b[j], ln_g[l, 0], ln_b[l, 0], xm=xm)
        peer_args = (x1, h2, mods, ln_g[l, 1], ln_b[l, 1], n_lat_tiles, peer_w_q[l], peer_sub_keys[l],
                     peer_u[l], peer_v[l])
        if l + 1 < depth:
            x_all, xm = _peer_and_norm(*peer_args, mods_next=mods_all[l + 1])
        else:
            x_all = _peer_and_norm(*peer_args)
    return x_all[None].astype(x.dtype)
```

```python
import functools
import math

import numpy as np
import jax
import jax.numpy as jnp
from jax import lax
from jax.experimental import pallas as pl
from jax.experimental.pallas import tpu as pltpu

F32 = jnp.float32
BF16 = jnp.bfloat16

HEAD_DIM = 128
GRID_W = 64
N_HEADS = 8
HALF = N_HEADS * HEAD_DIM
NA_WIN_ROWS = 8
NA_WIN_COLS = 16
ROPE_BASE = 10000.0
S5_GROUP = 16
S5_STATE = 64
S5_GROUPS = HALF // S5_GROUP
PEER_HEADS = 8
PEER_N_KEYS = 128
PEER_TOPK = 16
PEER_TOP_ROWS = 24
LN_EPS = 1e-5
DEPTH = 2
ALPHA = (2.0 * DEPTH) ** 0.25
QK_SCALE = HEAD_DIM ** -0.5

ROW_TILE = 256
MM_ROW_TILE = 1280
MM_COL_TILE = 1024
SCAN_CHUNK = 128
GLA_CHUNK = 64
S5_CHUNK = 32
S5_SCAN_BLOCK = 8
NA_QROWS = 8
PEER_P1_TILE = 256
PEER_TOK_TILE = 512
PEER_EXP_TILE = 512
PEER_GATE_ROWS = 64
PEER_U_SPLIT = 4
NEG = -1e30
GLA_SAFE_LOG = -80.0
VMEM_LIMIT = 56 * 1024 * 1024


def _cparams(n_axes):
    return pltpu.CompilerParams(dimension_semantics=("arbitrary",) * n_axes,
                                vmem_limit_bytes=VMEM_LIMIT)


def _nt(a, b):
    return lax.dot_general(a, b, (((1,), (1,)), ((), ())), preferred_element_type=F32)


def _tn(a, b):
    return lax.dot_general(a, b, (((0,), (0,)), ((), ())), preferred_element_type=F32)


def _nn(a, b):
    return jnp.dot(a, b, preferred_element_type=F32)


def _split3(x):
    hi = x.astype(BF16)
    r = x - hi.astype(F32)
    mid = r.astype(BF16)
    lo = (r - mid.astype(F32)).astype(BF16)
    return hi, mid, lo


def _log_sigmoid(x):
    return jnp.minimum(x, 0.0) - jnp.log1p(jnp.exp(-jnp.abs(x)))


def _sigmoid(x):
    return 1.0 / (1.0 + jnp.exp(-x))


def _gelu_tanh(x):
    return 0.5 * x * (1.0 + jnp.tanh(math.sqrt(2.0 / math.pi) * (x + 0.044715 * (x * x * x))))


def _head_norm(x):
    mu = jnp.mean(x, axis=-1, keepdims=True)
    xc = x - mu
    return xc * lax.rsqrt(jnp.mean(xc * xc, axis=-1, keepdims=True) + LN_EPS)


def _adaln_kernel(ct_ref, w_ref, b_ref, o_ref):
    ct = ct_ref[...]
    s = ct * _sigmoid(ct)
    w = w_ref[...]
    o_ref[0:1, :] = jnp.sum(s[:, 0:1] * w, axis=0, keepdims=True) + b_ref[...]
    o_ref[1:2, :] = jnp.sum(s[:, 1:2] * w, axis=0, keepdims=True) + b_ref[...]


def _adaln(c_lat, c_ctx, w_all, b_all, layer):
    n_layers, d, n = w_all.shape
    tn = n // 8
    ct = jnp.stack([c_lat.reshape(d), c_ctx.reshape(d)], axis=1)
    out = pl.pallas_call(
        _adaln_kernel,
        grid=(n // tn,),
        in_specs=[pl.BlockSpec((d, 2), lambda j: (0, 0)),
                  pl.BlockSpec((None, d, tn), lambda j: (layer, 0, j)),
                  pl.BlockSpec((None, 1, tn), lambda j: (layer, 0, j))],
        out_specs=pl.BlockSpec((2, tn), lambda j: (0, j)),
        out_shape=jax.ShapeDtypeStruct((2, n), F32),
        compiler_params=_cparams(1),
    )(ct, w_all, b_all.reshape(n_layers, 1, n))
    return jnp.pad(out.reshape(2, 6, d), ((0, 0), (0, 2), (0, 0)))


def _modulated(x, m_ref):
    return (x * (1.0 + m_ref[1:2, :]) + m_ref[0:1, :]).astype(BF16)


def _modulate_kernel(x_ref, m_ref, o_ref):
    o_ref[...] = _modulated(x_ref[...], m_ref)


def _mods_spec(d, n_lat_tiles):
    return pl.BlockSpec((None, 8, d), lambda i: (jnp.where(i < n_lat_tiles, 0, 1), 0, 0))


def _modulate(x_all, mods, n_lat_tiles):
    t_all, d = x_all.shape
    row = pl.BlockSpec((ROW_TILE, d), lambda i: (i, 0))
    return pl.pallas_call(
        _modulate_kernel,
        grid=(t_all // ROW_TILE,),
        in_specs=[row, _mods_spec(d, n_lat_tiles)],
        out_specs=row,
        out_shape=jax.ShapeDtypeStruct((t_all, d), BF16),
        compiler_params=_cparams(1),
    )(x_all, mods)


def _matmul_kernel(x_ref, w_ref, o_ref):
    o_ref[...] = _nn(x_ref[...], w_ref[...])


def _inproj(xm, w_bf):
    t_all, d = xm.shape
    n = w_bf.shape[1]
    tm = MM_ROW_TILE if t_all % MM_ROW_TILE == 0 else ROW_TILE
    tn = min(n, MM_COL_TILE)
    return pl.pallas_call(
        _matmul_kernel,
        grid=(t_all // tm, n // tn),
        in_specs=[pl.BlockSpec((tm, d), lambda i, j: (i, 0)),
                  pl.BlockSpec((d, tn), lambda i, j: (0, j))],
        out_specs=pl.BlockSpec((tm, tn), lambda i, j: (i, j)),
        out_shape=jax.ShapeDtypeStruct((t_all, n), F32),
        compiler_params=_cparams(2),
    )(xm, w_bf)


def _ln_rows(z, g, b):
    mu = jnp.mean(z, axis=-1, keepdims=True)
    zc = z - mu
    return zc * lax.rsqrt(jnp.mean(zc * zc, axis=-1, keepdims=True) + LN_EPS) * g + b


def _outproj_kernel(*refs, glu):
    hf_ref, hb_ref, gate_ref = refs[:3]
    refs = refs[3:]
    if glu:
        (y5_ref, u5_ref, ds_ref, gw_ref, gb_ref, w1_ref, w2_ref, x_ref, m_ref, g_ref, b_ref,
         x1_ref, h2_ref) = refs
        y = _gelu_tanh(y5_ref[...] + ds_ref[...] * u5_ref[...])
        a2 = y * _sigmoid(_nn(y.astype(BF16), gw_ref[...]) + gb_ref[...])
    else:
        a2_ref, w1_ref, w2_ref, x_ref, m_ref, g_ref, b_ref, x1_ref, h2_ref = refs
        a2 = a2_ref[...]
    heads = []
    for h in range(N_HEADS):
        hs = slice(h * HEAD_DIM, (h + 1) * HEAD_DIM)
        gx = gate_ref[:, hs]
        gate = gx * _sigmoid(gx) if glu else _sigmoid(gx)
        heads.append((_head_norm(hf_ref[:, hs] + hb_ref[:, hs]) * gate).astype(BF16))
    a1 = jnp.concatenate(heads, axis=1)
    y = _nn(a1, w1_ref[...]) + _nn(a2.astype(BF16), w2_ref[...])
    x1 = _ln_rows(ALPHA * x_ref[...] + m_ref[2:3, :] * y, g_ref[...], b_ref[...])
    x1_ref[...] = x1
    h2_ref[...] = (x1 * (1.0 + m_ref[4:5, :]) + m_ref[3:4, :]).astype(BF16)


def _outproj(scan_parts, a2_parts, w_out_bf, x_all, mods, ln_g, ln_b, n_lat_tiles, glu_params=None):
    t_all, d = x_all.shape
    half = HALF
    row = lambda i: (i, 0)
    const = lambda i: (0, 0)
    a2_parts = list(scan_parts) + list(a2_parts)
    a2_specs = [pl.BlockSpec((ROW_TILE, half), lambda i, c=c: (i, c)) for _, c in a2_parts]
    a2_args = [a for a, _ in a2_parts]
    if glu_params is not None:
        d_skip, glu_w_bf, glu_b = glu_params
        a2_specs += [pl.BlockSpec((1, half), const), pl.BlockSpec((half, half), const),
                     pl.BlockSpec((1, half), const)]
        a2_args += [d_skip.reshape(1, half), glu_w_bf, glu_b.reshape(1, half)]
    return pl.pallas_call(
        functools.partial(_outproj_kernel, glu=glu_params is not None),
        grid=(t_all // ROW_TILE,),
        in_specs=a2_specs + [
            pl.BlockSpec((half, d), const), pl.BlockSpec((half, d), const),
            pl.BlockSpec((ROW_TILE, d), row),
            pl.BlockSpec((None, 8, d), lambda i: (jnp.where(i < n_lat_tiles, 0, 1), 0, 0)),
            pl.BlockSpec((1, d), const), pl.BlockSpec((1, d), const)],
        out_specs=[pl.BlockSpec((ROW_TILE, d), row), pl.BlockSpec((ROW_TILE, d), row)],
        out_shape=[jax.ShapeDtypeStruct((t_all, d), F32), jax.ShapeDtypeStruct((t_all, d), BF16)],
        compiler_params=_cparams(1),
    )(*a2_args, w_out_bf[:half], w_out_bf[half:], x_all, mods, ln_g.reshape(1, d), ln_b.reshape(1, d))


def _ln2_kernel(x_ref, p_ref, m_ref, g_ref, b_ref, *rest):
    x2 = _ln_rows(ALPHA * x_ref[...] + m_ref[5:6, :] * p_ref[...].T, g_ref[...], b_ref[...])
    if len(rest) == 1:
        rest[0][...] = x2
    else:
        mn_ref, o_ref, xm_ref = rest
        o_ref[...] = x2
        xm_ref[...] = _modulated(x2, mn_ref)


def _ln2(x1, peer_out_t, mods, ln_g, ln_b, n_lat_tiles, mods_next=None):
    t_all, d = x1.shape
    row = pl.BlockSpec((ROW_TILE, d), lambda i: (i, 0))
    col = pl.BlockSpec((d, ROW_TILE), lambda i: (0, i))
    vec = pl.BlockSpec((1, d), lambda i: (0, 0))
    nxt = mods_next is not None
    n_rows = t_all if nxt else n_lat_tiles * ROW_TILE
    return pl.pallas_call(
        _ln2_kernel,
        grid=(n_rows // ROW_TILE,),
        in_specs=[row, col, _mods_spec(d, n_lat_tiles), vec, vec] + ([_mods_spec(d, n_lat_tiles)] if nxt else []),
        out_specs=[row, row] if nxt else row,
        out_shape=([jax.ShapeDtypeStruct((t_all, d), F32), jax.ShapeDtypeStruct((t_all, d), BF16)] if nxt
                   else jax.ShapeDtypeStruct((n_rows, d), F32)),
        compiler_params=_cparams(1),
    )(x1, peer_out_t, mods, ln_g.reshape(1, d), ln_b.reshape(1, d), *([mods_next] if nxt else []))


def _scan_block(rev, n_chunks, n_lat_chunks):
    if rev:
        return lambda s: n_chunks - 1 - s
    n_ctx = n_chunks - n_lat_chunks
    return lambda s: jnp.where(s < n_ctx, s + n_lat_chunks, s - n_ctx)


def _scan_mask(rev, n):
    row = lax.broadcasted_iota(jnp.int32, (n, n), 0)
    col = lax.broadcasted_iota(jnp.int32, (n, n), 1)
    return (col >= row) if rev else (col <= row)


def _rope(x, cos, sin):
    lane = lax.broadcasted_iota(jnp.int32, x.shape, 1)
    partner = jnp.where((lane % 64) < 32, pltpu.roll(x, 96, 1), pltpu.roll(x, 32, 1))
    return x * cos + partner * sin


def _mlstm_kernel(*refs):
    n_in = 5
    fw_in, bw_in = refs[:n_in], refs[n_in:2 * n_in]
    bias_ref, biast_ref, out_f, out_b = refs[2 * n_in:2 * n_in + 4]
    state = refs[2 * n_in + 4:]

    @pl.when(pl.program_id(0) == 0)
    def _():
        for r in state:
            r[...] = jnp.zeros_like(r)

    _mlstm_chunk(*fw_in, bias_ref, biast_ref, out_f, *state[:2], rev=False)
    _mlstm_chunk(*bw_in, bias_ref, biast_ref, out_b, *state[2:], rev=True)


def _mlstm_chunk(q_ref, k_ref, v_ref, g_ref, gt_ref, bias_ref, biast_ref, out_ref, cn_ref, m_ref, *, rev):
    n = q_ref.shape[0]
    mask = _scan_mask(rev, n)
    tri = mask.astype(BF16)
    pre_c = g_ref[...] + bias_ref[...]
    pre_r = gt_ref[...] + biast_ref[...]
    b_c = sum(_nn(tri, p) for p in _split3(_log_sigmoid(pre_c)))
    b_r = sum(_nt(p, tri) for p in _split3(_log_sigmoid(pre_r)))
    ones = jnp.ones((n, HEAD_DIM), BF16)
    last = 0 if rev else n - 1
    off = N_HEADS if rev else 0

    for h in range(N_HEADS):
        hs = slice(h * HEAD_DIM, (h + 1) * HEAD_DIM)
        ci, cf = off + h, 2 * N_HEADS + off + h
        bc, br = b_c[:, cf:cf + 1], b_r[cf:cf + 1, :]
        ic, ir = pre_c[:, ci:ci + 1], pre_r[ci:ci + 1, :]
        m_prev = m_ref[h:h + 1, 0:1]
        logw = jnp.where(mask, bc - br + ir, NEG)
        inter = bc + m_prev
        m_t = jnp.maximum(inter, jnp.max(logw, axis=1, keepdims=True))
        w_inter = jnp.exp(inter - m_t)
        qb, kb = q_ref[:, hs], k_ref[:, hs]
        v_ext = jnp.concatenate([v_ref[:, hs].astype(BF16), ones], axis=1)
        s = _nt(qb, kb) * jnp.exp(logw - m_t)
        cn_old = cn_ref[h]
        both = _nn(s.astype(BF16), v_ext) + _nn(qb, cn_old.astype(BF16)) * w_inter
        den = jnp.maximum(jnp.abs(both[:, HEAD_DIM:]), jnp.exp(-m_t))
        out_ref[:, hs] = both[:, :HEAD_DIM] / den
        m_new = m_t[last:last + 1, :]
        b_last = bc[last:last + 1, :]
        w_old = jnp.exp(b_last + m_prev - m_new)
        kw = kb.astype(F32) * jnp.exp(b_last - bc + ic - m_new)
        cn_ref[h] = w_old * cn_old + _tn(kw.astype(BF16), v_ext)
        m_ref[h:h + 1, :] = jnp.broadcast_to(m_new, (1, HEAD_DIM))


def _rope_kernel(q_ref, k_ref, cos_ref, sin_ref, qo_ref, ko_ref):
    cos, sin = cos_ref[...], sin_ref[...]
    for h in range(N_HEADS):
        hs = slice(h * HEAD_DIM, (h + 1) * HEAD_DIM)
        qo_ref[:, hs] = (_rope(q_ref[:, hs], cos, sin) * QK_SCALE).astype(BF16)
        ko_ref[:, hs] = _rope(k_ref[:, hs], cos, sin).astype(BF16)


def _rope_qk(proj, cos_t, sin_t):
    t_all = proj.shape[0]
    tok = pl.BlockSpec((ROW_TILE, 128), lambda i: (i, 0))
    return pl.pallas_call(
        _rope_kernel,
        grid=(t_all // ROW_TILE,),
        in_specs=[pl.BlockSpec((ROW_TILE, HALF), lambda i: (i, 0)),
                  pl.BlockSpec((ROW_TILE, HALF), lambda i: (i, 1)), tok, tok],
        out_specs=[pl.BlockSpec((ROW_TILE, HALF), lambda i: (i, 0))] * 2,
        out_shape=[jax.ShapeDtypeStruct((t_all, HALF), BF16)] * 2,
        compiler_params=_cparams(1),
    )(proj, proj, cos_t, sin_t)


def _mlstm(proj, gates, gates_t, cos_t, sin_t, igate_b, fgate_b, n_lat_chunks):
    t_all = proj.shape[0]
    n = SCAN_CHUNK
    n_chunks = t_all // n
    bias = jnp.zeros((1, 128), F32).at[0, :4 * N_HEADS].set(
        jnp.concatenate([igate_b.reshape(-1), fgate_b.reshape(-1)]))
    q_rot, k_rot = _rope_qk(proj, cos_t, sin_t)

    def dir_specs(rev):
        blk = _scan_block(rev, n_chunks, n_lat_chunks)
        colblk = lambda c: pl.BlockSpec((n, HALF), lambda s, c=c: (blk(s), c))
        ins = [colblk(0), colblk(0), colblk(2), pl.BlockSpec((n, 128), lambda s: (blk(s), 0)),
               pl.BlockSpec((128, n), lambda s: (0, blk(s)))]
        return ins, pl.BlockSpec((n, HALF), lambda s: (blk(s), 0))

    (in_f, out_f), (in_b, out_b) = dir_specs(False), dir_specs(True)
    dir_args = [q_rot, k_rot, proj, gates, gates_t]
    state = [pltpu.VMEM((N_HEADS, HEAD_DIM, 2 * HEAD_DIM), F32), pltpu.VMEM((N_HEADS, HEAD_DIM), F32)]
    return pl.pallas_call(
        _mlstm_kernel,
        grid=(n_chunks,),
        in_specs=in_f + in_b + [pl.BlockSpec((1, 128), lambda s: (0, 0)), pl.BlockSpec((128, 1), lambda s: (0, 0))],
        out_specs=[out_f, out_b],
        out_shape=[jax.ShapeDtypeStruct((t_all, HALF), F32)] * 2,
        scratch_shapes=state + state,
        compiler_params=_cparams(1),
    )(*dir_args, *dir_args, bias, bias.reshape(128, 1))


def _na_kernel(q_ref, kp_ref, kc_ref, kn_ref, vp_ref, vc_ref, vn_ref, kx_ref, vx_ref, bias_ref, o_ref):
    qb = (q_ref[...] * QK_SCALE).astype(BF16)
    kk = jnp.concatenate([kp_ref[...], kc_ref[...], kn_ref[...]], axis=0).astype(BF16)
    vv = jnp.concatenate([vp_ref[...], vc_ref[...], vn_ref[...]], axis=0).astype(BF16)
    s_loc = _nt(qb, kk) + bias_ref[...]
    s_ctx = _nt(qb, kx_ref[...].astype(BF16))
    m = jnp.maximum(jnp.max(s_loc, axis=1, keepdims=True), jnp.max(s_ctx, axis=1, keepdims=True))
    p_loc = jnp.exp(s_loc - m)
    p_ctx = jnp.exp(s_ctx - m)
    l = jnp.sum(p_loc, axis=1, keepdims=True) + jnp.sum(p_ctx, axis=1, keepdims=True)
    o_ref[...] = (_nn(p_loc.astype(BF16), vv) + _nn(p_ctx.astype(BF16), vx_ref[...].astype(BF16))) / l


def _na_bias(rpb, rows):
    qn = NA_QROWS * GRID_W
    kn = 2 * qn
    ql = np.arange(qn)
    kl = np.arange(kn)
    out = []
    nqb = rows // NA_QROWS
    lr, krl = np.arange(NA_QROWS)[:, None], np.arange(2 * NA_QROWS)[None, :]
    dr = np.clip(krl - NA_QROWS // 2 - lr + NA_WIN_ROWS - 1, 0, 2 * NA_WIN_ROWS - 2)
    col = np.arange(GRID_W)
    dc = np.clip(col[None, :] - col[:, None] + NA_WIN_COLS - 1, 0, 2 * NA_WIN_COLS - 2)
    oh_r = jnp.asarray(dr[..., None] == np.arange(2 * NA_WIN_ROWS - 1), F32)
    oh_c = jnp.asarray(dc[..., None] == np.arange(2 * NA_WIN_COLS - 1), F32)
    table = jnp.einsum('hab,lka,qcb->hlqkc', rpb, oh_r, oh_c,
                       precision=lax.Precision.HIGHEST).reshape(rpb.shape[0], qn, kn)
    for qb in (0, 1, nqb - 1):
        qr = NA_QROWS * qb + ql // GRID_W
        qc = ql % GRID_W
        kr = NA_QROWS * qb - NA_QROWS // 2 + kl // GRID_W
        kc = kl % GRID_W
        r0 = np.clip(qr - NA_WIN_ROWS // 2, 0, rows - NA_WIN_ROWS)
        c0 = np.clip(qc - NA_WIN_COLS // 2, 0, GRID_W - NA_WIN_COLS)
        ok = ((kr[None, :] >= r0[:, None]) & (kr[None, :] < r0[:, None] + NA_WIN_ROWS)
              & (kc[None, :] >= c0[:, None]) & (kc[None, :] < c0[:, None] + NA_WIN_COLS)
              & (kr[None, :] >= 0) & (kr[None, :] < rows))
        out.append(jnp.where(jnp.asarray(ok)[None], table, NEG))
    return jnp.stack(out)


def _na(proj, rpb, t_lat, t_ctx):
    rows = t_lat // GRID_W
    qn = NA_QROWS * GRID_W
    hn = qn // 2
    nqb = rows // NA_QROWS
    assert nqb >= 2 and t_ctx == hn
    bias = _na_bias(rpb.astype(F32), rows)
    last_half = 2 * nqb - 1
    ctx_blk = t_lat // hn
    cls = lambda b: jnp.where(b == 0, 0, jnp.where(b == nqb - 1, 2, 1))

    def kv_specs(col0):
        return [pl.BlockSpec((hn, HEAD_DIM), lambda h, b: (jnp.maximum(2 * b - 1, 0), col0 + h)),
                pl.BlockSpec((qn, HEAD_DIM), lambda h, b: (b, col0 + h)),
                pl.BlockSpec((hn, HEAD_DIM), lambda h, b: (jnp.minimum(2 * b + 2, last_half), col0 + h))]

    kcol, vcol = 5 * N_HEADS, 6 * N_HEADS
    return pl.pallas_call(
        _na_kernel,
        grid=(N_HEADS, nqb),
        in_specs=[pl.BlockSpec((qn, HEAD_DIM), lambda h, b: (b, 4 * N_HEADS + h))]
        + kv_specs(kcol) + kv_specs(vcol)
        + [pl.BlockSpec((hn, HEAD_DIM), lambda h, b: (ctx_blk, kcol + h)),
           pl.BlockSpec((hn, HEAD_DIM), lambda h, b: (ctx_blk, vcol + h)),
           pl.BlockSpec((None, None, qn, 2 * qn), lambda h, b: (cls(b), h, 0, 0))],
        out_specs=pl.BlockSpec((qn, HEAD_DIM), lambda h, b: (b, h)),
        out_shape=jax.ShapeDtypeStruct((t_lat, HALF), F32),
        compiler_params=_cparams(2),
    )(proj, proj, proj, proj, proj, proj, proj, proj, proj, bias)


def _ctx_attn_kernel(q_ref, k_ref, v_ref, o_ref):
    qb = (q_ref[...] * QK_SCALE).astype(BF16)
    s = _nt(qb, k_ref[...].astype(BF16))
    p = jnp.exp(s - jnp.max(s, axis=1, keepdims=True))
    o_ref[...] = _nn(p.astype(BF16), v_ref[...].astype(BF16)) / jnp.sum(p, axis=1, keepdims=True)


def _ctx_attn(proj, t_lat, t_ctx):
    blk = t_lat // t_ctx
    spec = lambda c: pl.BlockSpec((t_ctx, HEAD_DIM), lambda h, c=c: (blk, c * N_HEADS + h))
    return pl.pallas_call(
        _ctx_attn_kernel,
        grid=(N_HEADS,),
        in_specs=[spec(4), spec(5), spec(6)],
        out_specs=pl.BlockSpec((t_ctx, HEAD_DIM), lambda h: (0, h)),
        out_shape=jax.ShapeDtypeStruct((t_ctx, HALF), F32),
        compiler_params=_cparams(1),
    )(proj, proj, proj)


def _gla_inter(q_ref, i_ref, f_ref, lb_ref, out_ref, st_ref, k_scr, b_scr, *, rev):
    n = q_ref.shape[0]
    mask = _scan_mask(rev, n)
    tri = mask.astype(BF16)
    last = 0 if rev else n - 1
    x = f_ref[...]
    key = lb_ref[0:1, :] * _sigmoid(-x)
    la, lc = lb_ref[1:2, :], lb_ref[2:3, :] + _log_sigmoid(x)
    logf = jnp.maximum(la, lc) + jnp.log1p(jnp.exp(-jnp.abs(la - lc)))
    b = sum(_nn(tri, p) for p in _split3(logf))
    b_end = b[last:last + 1, :]
    qx = q_ref[...]
    q = qx * _sigmoid(qx)
    qt = q * jnp.exp(b)
    kdec = key * jnp.exp(b_end - b)
    w_end = jnp.exp(b_end)
    k_scr[...] = key
    b_scr[...] = b
    for h in range(N_HEADS):
        hs = slice(h * HEAD_DIM, (h + 1) * HEAD_DIM)
        st = st_ref[h]
        vb = i_ref[:, hs].astype(BF16)
        out_ref[:, hs] = _nt(qt[:, hs].astype(BF16), st.astype(BF16))
        st_ref[h] = st * w_end[:, hs] + _tn(vb, kdec[:, hs].astype(BF16))
    return mask, q, b_end


def _gla_intra_factored(mask, q, b_end, i_ref, out_ref, k_scr, b_scr):
    mid = 0.5 * b_end
    qm = q * jnp.exp(b_scr[...] - mid)
    km = k_scr[...] * jnp.exp(mid - b_scr[...])
    for h in range(N_HEADS):
        hs = slice(h * HEAD_DIM, (h + 1) * HEAD_DIM)
        att = jnp.where(mask, _nt(qm[:, hs].astype(BF16), km[:, hs].astype(BF16)), 0.0)
        out_ref[:, hs] += _nn(att.astype(BF16), i_ref[:, hs].astype(BF16))


def _gla_intra_exact(q, i_ref, out_ref, k_scr, b_scr, *, rev):
    n = q.shape[0]
    t_idx = lax.broadcasted_iota(jnp.int32, (n, 1), 0)
    for h in range(N_HEADS):
        hs = slice(h * HEAD_DIM, (h + 1) * HEAD_DIM)
        q_h = q[:, hs]
        b_h = b_scr[:, hs]

        def body(grp, acc):
            base = pl.multiple_of(grp * 8, 8)
            b8, k8, v8 = b_scr[pl.ds(base, 8), hs], k_scr[pl.ds(base, 8), hs], i_ref[pl.ds(base, 8), hs]
            for r in range(8):
                s = base + r
                e = jnp.exp(jnp.minimum(b_h - b8[r:r + 1, :], 0.0))
                a = jnp.sum(q_h * k8[r:r + 1, :] * e, axis=1, keepdims=True)
                a = jnp.where((t_idx <= s) if rev else (t_idx >= s), a, 0.0)
                acc = acc + a * v8[r:r + 1, :]
            return acc

        out_ref[:, hs] += lax.fori_loop(0, n // 8, body, jnp.zeros((n, HEAD_DIM), F32))


def _gla_kernel(*refs):
    fw_in, bw_in, lb_ref = refs[0:3], refs[3:6], refs[6]
    out_f, out_b = refs[7:9]
    st_f, k_f, b_f, st_b, k_b, b_b = refs[9:]

    @pl.when(pl.program_id(0) == 0)
    def _():
        st_f[...] = jnp.zeros_like(st_f)
        st_b[...] = jnp.zeros_like(st_b)

    mask_f, q_f, end_f = _gla_inter(*fw_in, lb_ref, out_f, st_f, k_f, b_f, rev=False)
    mask_b, q_b, end_b = _gla_inter(*bw_in, lb_ref, out_b, st_b, k_b, b_b, rev=True)
    safe = jnp.minimum(jnp.min(end_f), jnp.min(end_b)) >= 2.0 * GLA_SAFE_LOG

    @pl.when(safe)
    def _():
        _gla_intra_factored(mask_f, q_f, end_f, fw_in[1], out_f, k_f, b_f)
        _gla_intra_factored(mask_b, q_b, end_b, bw_in[1], out_b, k_b, b_b)

    @pl.when(jnp.logical_not(safe))
    def _():
        _gla_intra_exact(q_f, fw_in[1], out_f, k_f, b_f, rev=False)
        _gla_intra_exact(q_b, bw_in[1], out_b, k_b, b_b, rev=True)


def _gla(proj, lb, n_lat_chunks):
    t_all = proj.shape[0]
    n = GLA_CHUNK
    n_chunks = t_all // n

    def dir_specs(rev):
        blk = _scan_block(rev, n_chunks, n_lat_chunks)
        colblk = lambda c: pl.BlockSpec((n, HALF), lambda s, c=c: (blk(s), c))
        return [colblk(0), colblk(1), colblk(3 if rev else 2)], pl.BlockSpec((n, HALF), lambda s: (blk(s), 0))

    (in_f, out_f), (in_b, out_b) = dir_specs(False), dir_specs(True)
    state = [pltpu.VMEM((N_HEADS, HEAD_DIM, HEAD_DIM), F32), pltpu.VMEM((n, HALF), F32),
             pltpu.VMEM((n, HALF), F32)]
    lb = lb.reshape(1, HALF)
    lb_rows = jnp.concatenate([1.0 - lb, jnp.log(lb), jnp.log1p(-lb)], axis=0)
    return pl.pallas_call(
        _gla_kernel,
        grid=(n_chunks,),
        in_specs=in_f + in_b + [pl.BlockSpec((3, HALF), lambda s: (0, 0))],
        out_specs=[out_f, out_b],
        out_shape=[jax.ShapeDtypeStruct((t_all, HALF), F32)] * 2,
        scratch_shapes=state + state,
        compiler_params=_cparams(1),
    )(*([proj] * 6), lb_rows)


def _s5_mats(a_re, a_im, log_dt, b_re, b_im, c_re, c_im):
    hp = lax.Precision.HIGHEST
    n = S5_CHUNK
    dt = jnp.exp(log_dt)[..., None]
    la_re, la_im = a_re * dt, a_im * dt
    mag = jnp.exp(la_re)
    ab_re, ab_im = mag * jnp.cos(la_im), mag * jnp.sin(la_im)
    nr, ni = ab_re - 1.0, ab_im
    den = jnp.square(a_re) + jnp.square(a_im)
    cr = (nr * a_re + ni * a_im) / den
    ci = (ni * a_re - nr * a_im) / den
    bb_re = cr[..., None] * b_re - ci[..., None] * b_im
    bb_im = cr[..., None] * b_im + ci[..., None] * b_re

    def apow(tau):
        tau = jnp.asarray(tau, F32)[None, None, :, None]
        m = jnp.exp(tau * la_re[:, :, None, :])
        return m * jnp.cos(tau * la_im[:, :, None, :]), m * jnp.sin(tau * la_im[:, :, None, :])

    def c_apow(tau):
        pr, pi = apow(tau)
        return (c_re[:, :, None] * pr[:, :, :, None] - c_im[:, :, None] * pi[:, :, :, None],
                c_re[:, :, None] * pi[:, :, :, None] + c_im[:, :, None] * pr[:, :, :, None])

    ca_re, ca_im = c_apow(np.arange(n))
    kern = jnp.einsum('dgtip,dgpj->dgtij', jnp.concatenate([ca_re, -ca_im], axis=-1),
                      jnp.concatenate([bb_re, bb_im], axis=2), precision=hp)
    s_idx, t_idx = np.arange(n)[:, None], np.arange(n)[None, :]
    g = a_re.shape[1]
    lp = n * S5_GROUP

    def toeplitz(kd, lag):
        onehot = jnp.asarray(lag[None] == np.arange(n)[:, None, None], F32)
        return jnp.einsum('lst,glij->gsjti', onehot, kd, precision=hp).reshape(g, lp, lp)

    kmat = toeplitz(kern[0], t_idx - s_idx) + toeplitz(kern[1], s_idx - t_idx)

    def emat(d, tau):
        pr, pi = apow(tau)
        pr, pi = pr[d][:, :, None, :], pi[d][:, :, None, :]
        br, bi = bb_re[d].transpose(0, 2, 1)[:, None], bb_im[d].transpose(0, 2, 1)[:, None]
        return jnp.concatenate([pr * br - pi * bi, pr * bi + pi * br], axis=-1).reshape(g, lp, 2 * S5_STATE)

    def fmat(d, tau):
        fr, fi = c_apow(tau)
        fr, fi = fr[d], fi[d]
        return jnp.concatenate([fr, -fi], axis=-1).transpose(0, 3, 1, 2).reshape(g, 2 * S5_STATE, lp)

    e_all = jnp.concatenate([emat(0, n - 1 - np.arange(n)), emat(1, np.arange(n))], axis=-1)
    f_all = jnp.concatenate([fmat(0, np.arange(n) + 1), fmat(1, n - np.arange(n))], axis=1)
    pr, pi = apow(np.array([n]))
    pr, pi = pr[:, :, 0], pi[:, :, 0]
    ar2 = jnp.concatenate([pr[0], pr[0], pr[1], pr[1]], axis=-1)
    ai2 = jnp.concatenate([-pi[0], pi[0], -pi[1], pi[1]], axis=-1)
    return kmat.astype(BF16), e_all.astype(BF16), f_all.astype(BF16), ar2, ai2


def _s5_state_kernel(u_ref, e_ref, s_ref):
    s_ref[...] = _nn(u_ref[...], e_ref[...])


def _s5_scan_kernel(sf_ref, sb_ref, ar_ref, ai_ref, hf_ref, hb_ref, stf_ref, stb_ref):
    nb = sf_ref.shape[0]
    w = 2 * S5_STATE
    ar, ai = ar_ref[...], ai_ref[...]

    @pl.when(pl.program_id(0) == 0)
    def _():
        stf_ref[...] = jnp.zeros_like(stf_ref)
        stb_ref[...] = jnp.zeros_like(stb_ref)

    def step(h, s, lo):
        return ar[:, lo:lo + w] * h + ai[:, lo:lo + w] * pltpu.roll(h, S5_STATE, 1) + s

    hf, hb = stf_ref[...], stb_ref[...]
    for j in range(nb):
        hf_ref[j] = hf
        hf = step(hf, sf_ref[j], 0)
        hb_ref[nb - 1 - j] = hb
        hb = step(hb, sb_ref[nb - 1 - j], w)
    stf_ref[...] = hf
    stb_ref[...] = hb


def _s5_out_kernel(u_ref, k_ref, h_ref, f_ref, y_ref):
    y_ref[...] = _nn(u_ref[...], k_ref[...]) + _nn(h_ref[...].astype(BF16), f_ref[...])


def _s5(u, mats, n_lat_chunks):
    kmat, emat, fmat, ar2, ai2 = mats
    t_all = u.shape[0]
    n = S5_CHUNK
    nc = t_all // n
    g = S5_GROUPS
    lp = n * S5_GROUP
    w4 = 4 * S5_STATE
    ug = u.astype(BF16).reshape(nc, n, g, S5_GROUP).transpose(2, 0, 1, 3).reshape(g, nc, lp)
    grp = lambda a, b: pl.BlockSpec((None, a, b), lambda i: (i, 0, 0))
    s_end = pl.pallas_call(
        _s5_state_kernel, grid=(g,),
        in_specs=[grp(nc, lp), grp(lp, w4)], out_specs=grp(nc, w4),
        out_shape=jax.ShapeDtypeStruct((g, nc, w4), F32), compiler_params=_cparams(1),
    )(ug, emat)
    nb = S5_SCAN_BLOCK
    w2 = 2 * S5_STATE
    fw_blk = _scan_block(False, nc // nb, n_lat_chunks // nb)
    bw_blk = _scan_block(True, nc // nb, n_lat_chunks // nb)
    s_t = s_end.transpose(1, 0, 2)
    coef = pl.BlockSpec((g, w4), lambda i: (0, 0))
    h_fw, h_bw = pl.pallas_call(
        _s5_scan_kernel, grid=(nc // nb,),
        in_specs=[pl.BlockSpec((nb, g, w2), lambda i: (fw_blk(i), 0, 0)),
                  pl.BlockSpec((nb, g, w2), lambda i: (bw_blk(i), 0, 1)), coef, coef],
        out_specs=[pl.BlockSpec((nb, g, w2), lambda i: (fw_blk(i), 0, 0)),
                   pl.BlockSpec((nb, g, w2), lambda i: (bw_blk(i), 0, 0))],
        out_shape=[jax.ShapeDtypeStruct((nc, g, w2), F32)] * 2,
        scratch_shapes=[pltpu.VMEM((g, w2), F32)] * 2,
        compiler_params=_cparams(1),
    )(s_t, s_t, ar2, ai2)
    h_in = jnp.concatenate([h_fw, h_bw], axis=-1).transpose(1, 0, 2)
    y = pl.pallas_call(
        _s5_out_kernel, grid=(g,),
        in_specs=[grp(nc, lp), grp(lp, lp), grp(nc, w4), grp(w4, lp)], out_specs=grp(nc, lp),
        out_shape=jax.ShapeDtypeStruct((g, nc, lp), F32), compiler_params=_cparams(1),
    )(ug, kmat, h_in, fmat)
    return y.reshape(g, nc, n, S5_GROUP).transpose(1, 2, 0, 3).reshape(t_all, HALF)


def _top_rows(x, k):
    rows = []
    for _ in range(k):
        mx = jnp.max(x, axis=0, keepdims=True)
        rows.append(mx)
        x = jnp.where(x == mx, NEG, x)
    return rows


def _peer_route_kernel(ht_ref, wq_ref, keys_ref, e1_ref, e0_ref, th_ref, top_scr):
    kk = PEER_TOPK
    qt = _nn(wq_ref[...], ht_ref[...])
    for hp in range(2 * PEER_HEADS):
        sc = _nn(keys_ref[hp], qt[hp * HEAD_DIM:(hp + 1) * HEAD_DIM, :].astype(BF16))
        h, p = divmod(hp, 2)
        if p == 0:
            e0_ref[h] = sc
        else:
            e1_ref[h] = sc
        for r, mx in enumerate(_top_rows(sc, kk + 1)):
            top_scr[hp, r:r + 1, :] = mx
        top_scr[hp, kk + 1:, :] = jnp.full((PEER_TOP_ROWS - kk - 1, sc.shape[1]), NEG, F32)

    row = lax.broadcasted_iota(jnp.int32, (PEER_TOP_ROWS, 1), 0)
    for h in range(PEER_HEADS):
        a = top_scr[2 * h]
        b = top_scr[2 * h + 1]
        parts = [a[0:1] + b, a[1:2] + b[0:8], a[2:3] + b[0:8], a[3:4] + b[0:8]]
        parts += [jnp.where(row >= 4, a + b[j:j + 1], NEG) for j in range(3)]
        cand = jnp.maximum(jnp.concatenate(parts, axis=0), NEG)
        top = _top_rows(cand, kk + 1)
        thr = 0.5 * (top[kk - 1] + top[kk])
        z = jnp.sum(jnp.where(cand >= thr, jnp.exp(cand - top[0]), 0.0), axis=0, keepdims=True)
        e0_ref[h] = jnp.exp(e0_ref[h] - a[0:1]) / z
        e1_ref[h] = jnp.exp(e1_ref[h] - b[0:1])
        th_ref[h:h + 1, :] = jnp.exp(thr - top[0]) / z


def _peer_expert_kernel(ht_ref, e1_ref, e0_ref, th_ref, *rest):
    u_refs = rest[:PEER_U_SPLIT]
    vt_ref, o_ref, at0_scr, at1_scr, g0_scr, g1_scr = rest[PEER_U_SPLIT:]
    s = pl.program_id(1)
    n_tiles = pl.num_programs(1) - 2
    ni = u_refs[0].shape[0] // PEER_N_KEYS
    kc = u_refs[0].shape[1]

    @pl.when(s == 0)
    def _():
        o_ref[...] = jnp.zeros_like(o_ref)
        for scr in (at0_scr, at1_scr, g0_scr, g1_scr):
            scr[...] = jnp.zeros_like(scr)

    tile = jnp.clip(s - 1, 0, n_tiles - 1)

    def stages(at_new, at_old, g_new, g_old):
        at_new[...] = sum(_nn(u_k[...], ht_ref[k * kc:(k + 1) * kc, :]) for k, u_k in enumerate(u_refs))
        nj, nc = PEER_GATE_ROWS, 128
        for ii in range(ni):
            i = tile * ni + ii
            e0_rows = [e0_ref[h, pl.ds(i, 1), :] for h in range(PEER_HEADS)]
            for c in range(at_old.shape[1] // nc):
                cs = slice(c * nc, (c + 1) * nc)
                for jb in range(PEER_N_KEYS // nj):
                    js = slice(jb * nj, (jb + 1) * nj)
                    w = None
                    for h in range(PEER_HEADS):
                        p = e1_ref[h, js, cs] * e0_rows[h][:, cs]
                        t = jnp.where(p >= th_ref[h:h + 1, cs], p, 0.0)
                        w = t if w is None else w + t
                    rs = slice(ii * PEER_N_KEYS + jb * nj, ii * PEER_N_KEYS + (jb + 1) * nj)
                    g_new[rs, cs] = (_gelu_tanh(at_old[rs, cs]) * w).astype(BF16)
        o_ref[...] += _nn(vt_ref[...], g_old[...])

    @pl.when(s % 2 == 0)
    def _():
        stages(at0_scr, at1_scr, g1_scr, g0_scr)

    @pl.when(s % 2 == 1)
    def _():
        stages(at1_scr, at0_scr, g0_scr, g1_scr)


def _tok3(t):
    return pl.BlockSpec((PEER_HEADS, PEER_N_KEYS, t), lambda i, *_: (0, 0, i))


def _peer_route(h2t, wq_t_bf, keys_bf):
    d, t_all = h2t.shape
    nh, nk = PEER_HEADS, PEER_N_KEYS
    t1 = PEER_P1_TILE
    return pl.pallas_call(
        _peer_route_kernel,
        grid=(t_all // t1,),
        in_specs=[pl.BlockSpec((d, t1), lambda i: (0, i)),
                  pl.BlockSpec(wq_t_bf.shape, lambda i: (0, 0)),
                  pl.BlockSpec(keys_bf.shape, lambda i: (0, 0, 0))],
        out_specs=[_tok3(t1), _tok3(t1), pl.BlockSpec((nh, t1), lambda i: (0, i))],
        out_shape=[jax.ShapeDtypeStruct((nh, nk, t_all), F32)] * 2 + [jax.ShapeDtypeStruct((nh, t_all), F32)],
        scratch_shapes=[pltpu.VMEM((2 * nh, PEER_TOP_ROWS, t1), F32)],
        compiler_params=_cparams(1),
    )(h2t, wq_t_bf, keys_bf)


def _peer(h2t, wq_t_bf, keys_bf, u_bf, vt_bf):
    t_all = h2t.shape[1]
    h2t = jnp.pad(h2t, ((0, 0), (0, -t_all % PEER_TOK_TILE)))
    return _peer_experts(h2t, *_peer_route(h2t, wq_t_bf, keys_bf), u_bf, vt_bf)


def _peer_tables(u_tab, v_tab):
    return u_tab.astype(BF16), v_tab.T.astype(BF16)


def _peer_experts(h2t, e1, e0, th, u_r, vt_r):
    d, t_all = h2t.shape
    tt = PEER_TOK_TILE
    et = PEER_EXP_TILE
    n_tiles = u_r.shape[0] // et
    kc = d // PEER_U_SPLIT
    tok = pl.BlockSpec((d, tt), lambda i, s: (0, i))
    return pl.pallas_call(
        _peer_expert_kernel,
        grid=(t_all // tt, n_tiles + 2),
        in_specs=[tok, _tok3(tt), _tok3(tt), pl.BlockSpec((PEER_HEADS, tt), lambda i, s: (0, i))]
        + [pl.BlockSpec((et, kc), lambda i, s, k=k: (jnp.minimum(s, n_tiles - 1), k))
           for k in range(PEER_U_SPLIT)]
        + [pl.BlockSpec((d, et), lambda i, s: (0, jnp.clip(s - 2, 0, n_tiles - 1)))],
        out_specs=tok,
        out_shape=jax.ShapeDtypeStruct((d, t_all), F32),
        scratch_shapes=[pltpu.VMEM((et, tt), F32)] * 2 + [pltpu.VMEM((et, tt), BF16)] * 2,
        compiler_params=_cparams(2),
    )(h2t, e1, e0, th, *([u_r] * PEER_U_SPLIT), vt_r)


def _rope_tables(t_lat, t_ctx):
    pos = np.arange(t_lat)
    nfreq = HEAD_DIM // 4
    freqs = ROPE_BASE ** (-jnp.arange(nfreq, dtype=F32) / nfreq)
    ang_r = jnp.asarray(pos // GRID_W, F32)[:, None] * freqs
    ang_c = jnp.asarray(pos % GRID_W, F32)[:, None] * freqs
    cos = jnp.concatenate([jnp.cos(ang_r), jnp.cos(ang_r), jnp.cos(ang_c), jnp.cos(ang_c)], axis=1)
    sin = jnp.concatenate([-jnp.sin(ang_r), jnp.sin(ang_r), -jnp.sin(ang_c), jnp.sin(ang_c)], axis=1)
    cos = jnp.concatenate([cos, jnp.ones((t_ctx, HEAD_DIM), F32)], axis=0)
    sin = jnp.concatenate([sin, jnp.zeros((t_ctx, HEAD_DIM), F32)], axis=0)
    return cos, sin


def _peer_and_norm(x1, h2, mods, ln_g, ln_b, n_lat_tiles, w_q, sub_keys, u_tab, v_tab, mods_next=None):
    keys = sub_keys.reshape(2 * PEER_HEADS, PEER_N_KEYS, -1).astype(BF16)
    pt = _peer(h2.T, w_q.T.astype(BF16), keys, *_peer_tables(u_tab, v_tab))
    return _ln2(x1, pt, mods, ln_g, ln_b, n_lat_tiles, mods_next)


def _layer_ab(x_all, mods, t_lat, t_ctx, w_in, w_out, igate_b, fgate_b, rpb, ln_g, ln_b, xm=None):
    n_lat_tiles = t_lat // ROW_TILE
    if xm is None:
        xm = _modulate(x_all, mods, n_lat_tiles)
    ng = 4 * N_HEADS
    w_main = jnp.concatenate([w_in[:, :4 * HALF], w_in[:, 4 * HALF + ng:]], axis=1).astype(BF16)
    w_gate = jnp.pad(w_in[:, 4 * HALF:4 * HALF + ng], ((0, 0), (0, 128 - ng))).astype(BF16)
    proj = _inproj(xm, w_main)
    gates = _inproj(xm, w_gate)
    cos_t, sin_t = _rope_tables(t_lat, t_ctx)
    h_fw, h_bw = _mlstm(proj, gates, gates.T, cos_t, sin_t, igate_b, fgate_b, t_lat // SCAN_CHUNK)
    y_na = jnp.concatenate([_na(proj, rpb, t_lat, t_ctx), _ctx_attn(proj, t_lat, t_ctx)], axis=0)
    return _outproj([(h_fw, 0), (h_bw, 0), (proj, 3)], [(y_na, 0)], w_out.astype(BF16), x_all, mods,
                    ln_g, ln_b, n_lat_tiles)


def _layer_cd(x_all, mods, t_lat, w_in, w_out, lb, s5p, s5_d, glu_w, glu_b, ln_g, ln_b, xm=None):
    n_lat_tiles = t_lat // ROW_TILE
    if xm is None:
        xm = _modulate(x_all, mods, n_lat_tiles)
    proj = _inproj(xm, w_in.astype(BF16))
    o_fw, o_bw = _gla(proj, lb, t_lat // GLA_CHUNK)
    y5 = _s5(proj[:, 5 * HALF:], _s5_mats(*[p.astype(F32) for p in s5p]), t_lat // S5_CHUNK)
    return _outproj([(o_fw, 0), (o_bw, 0), (proj, 4)], [(y5, 0), (proj, 5)], w_out.astype(BF16), x_all, mods,
                    ln_g, ln_b, n_lat_tiles,
                    glu_params=(s5_d.astype(F32), glu_w.astype(BF16), glu_b.astype(F32)))


def kernel(x, c, ctx, c_ctx, ada_w, ada_b, ln_g, ln_b, ab_w_in, ab_w_out, mlstm_igate_b, mlstm_fgate_b, na_rpb, cd_w_in, cd_w_out, hgrn_lb_logits, s5_a_re, s5_a_im, s5_log_dt, s5_b_re, s5_b_im, s5_c_re, s5_c_im, s5_d, s5_glu_w, s5_glu_b, peer_w_q, peer_sub_keys, peer_u, peer_v):
    t_lat, t_ctx = x.shape[1], ctx.shape[1]
    assert x.shape[0] == 1 and t_ctx == ROW_TILE and t_lat % (NA_QROWS * GRID_W) == 0
    depth = ada_w.shape[0]
    lb_soft = jax.nn.softmax(hgrn_lb_logits.astype(F32), axis=0)
    lower_bounds = jnp.cumsum(lb_soft, axis=0) - lb_soft[0]
    x_all = jnp.concatenate([x[0], ctx[0]], axis=0).astype(F32)
    n_lat_tiles = t_lat // ROW_TILE
    mods_all = [_adaln(c, c_ctx, ada_w, ada_b, l) for l in range(depth)]
    xm = _modulate(x_all, mods_all[0], n_lat_tiles)
    for l in range(depth):
        j = l // 2
        mods = mods_all[l]
        if l % 2 == 0:
            x1, h2 = _layer_ab(x_all, mods, t_lat, t_ctx, ab_w_in[j], ab_w_out[j], mlstm_igate_b[j],
                               mlstm_fgate_b[j], na_rpb[j], ln_g[l, 0], ln_b[l, 0], xm=xm)
        else:
            s5p = (s5_a_re[j], s5_a_im[j], s5_log_dt[j], s5_b_re[j], s5_b_im[j], s5_c_re[j], s5_c_im[j])
            x1, h2 = _layer_cd(x_all, mods, t_lat, cd_w_in[j], cd_w_out[j], lower_bounds[l], s5p, s5_d[j],
                               s5_glu_w[j], s5_glu_b[j], ln_g[l, 0], ln_b[l, 0], xm=xm)
        peer_args = (x1, h2, mods, ln_g[l, 1], ln_b[l, 1], n_lat_tiles, peer_w_q[l], peer_sub_keys[l],
                     peer_u[l], peer_v[l])
        if l + 1 < depth:
            x_all, xm = _peer_and_norm(*peer_args, mods_next=mods_all[l + 1])
        else:
            x_all = _peer_and_norm(*peer_args)
    return x_all[None].astype(x.dtype)
```

```python
import functools
import math

import numpy as np
import jax
import jax.numpy as jnp
from jax import lax
from jax.experimental import pallas as pl
from jax.experimental.pallas import tpu as pltpu

F32 = jnp.float32
BF16 = jnp.bfloat16

HEAD_DIM = 128
GRID_W = 64
N_HEADS = 8
HALF = N_HEADS * HEAD_DIM
NA_WIN_ROWS = 8
NA_WIN_COLS = 16
ROPE_BASE = 10000.0
S5_GROUP = 16
S5_STATE = 64
S5_GROUPS = HALF // S5_GROUP
PEER_HEADS = 8
PEER_N_KEYS = 128
PEER_TOPK = 16
PEER_TOP_ROWS = 24
LN_EPS = 1e-5
DEPTH = 2
ALPHA = (2.0 * DEPTH) ** 0.25
QK_SCALE = HEAD_DIM ** -0.5

ROW_TILE = 256
MM_ROW_TILE = 1280
MM_COL_TILE = 1024
SCAN_CHUNK = 128
GLA_CHUNK = 64
S5_CHUNK = 32
S5_SCAN_BLOCK = 8
NA_QROWS = 8
PEER_P1_TILE = 256
PEER_TOK_TILE = 512
PEER_EXP_TILE = 512
PEER_GATE_ROWS = 64
PEER_U_SPLIT = 4
NEG = -1e30
GLA_SAFE_LOG = -80.0
VMEM_LIMIT = 56 * 1024 * 1024


def _cparams(n_axes):
    return pltpu.CompilerParams(dimension_semantics=("arbitrary",) * n_axes,
                                vmem_limit_bytes=VMEM_LIMIT)


def _nt(a, b):
    return lax.dot_general(a, b, (((1,), (1,)), ((), ())), preferred_element_type=F32)


def _tn(a, b):
    return lax.dot_general(a, b, (((0,), (0,)), ((), ())), preferred_element_type=F32)


def _nn(a, b):
    return jnp.dot(a, b, preferred_element_type=F32)


def _split3(x):
    hi = x.astype(BF16)
    r = x - hi.astype(F32)
    mid = r.astype(BF16)
    lo = (r - mid.astype(F32)).astype(BF16)
    return hi, mid, lo


def _log_sigmoid(x):
    return jnp.minimum(x, 0.0) - jnp.log1p(jnp.exp(-jnp.abs(x)))


def _sigmoid(x):
    return 1.0 / (1.0 + jnp.exp(-x))


def _gelu_tanh(x):
    return 0.5 * x * (1.0 + jnp.tanh(math.sqrt(2.0 / math.pi) * (x + 0.044715 * (x * x * x))))


def _head_norm(x):
    mu = jnp.mean(x, axis=-1, keepdims=True)
    xc = x - mu
    return xc * lax.rsqrt(jnp.mean(xc * xc, axis=-1, keepdims=True) + LN_EPS)


def _adaln_kernel(ct_ref, w_ref, b_ref, o_ref):
    ct = ct_ref[...]
    s = ct * _sigmoid(ct)
    w = w_ref[...]
    o_ref[0:1, :] = jnp.sum(s[:, 0:1] * w, axis=0, keepdims=True) + b_ref[...]
    o_ref[1:2, :] = jnp.sum(s[:, 1:2] * w, axis=0, keepdims=True) + b_ref[...]


def _adaln(c_lat, c_ctx, w_all, b_all, layer):
    n_layers, d, n = w_all.shape
    tn = n // 8
    ct = jnp.stack([c_lat.reshape(d), c_ctx.reshape(d)], axis=1)
    out = pl.pallas_call(
        _adaln_kernel,
        grid=(n // tn,),
        in_specs=[pl.BlockSpec((d, 2), lambda j: (0, 0)),
                  pl.BlockSpec((None, d, tn), lambda j: (layer, 0, j)),
                  pl.BlockSpec((None, 1, tn), lambda j: (layer, 0, j))],
        out_specs=pl.BlockSpec((2, tn), lambda j: (0, j)),
        out_shape=jax.ShapeDtypeStruct((2, n), F32),
        compiler_params=_cparams(1),
    )(ct, w_all, b_all.reshape(n_layers, 1, n))
    return jnp.pad(out.reshape(2, 6, d), ((0, 0), (0, 2), (0, 0)))


def _modulated(x, m_ref):
    return (x * (1.0 + m_ref[1:2, :]) + m_ref[0:1, :]).astype(BF16)


def _modulate_kernel(x_ref, m_ref, o_ref):
    o_ref[...] = _modulated(x_ref[...], m_ref)


def _mods_spec(d, n_lat_tiles):
    return pl.BlockSpec((None, 8, d), lambda i: (jnp.where(i < n_lat_tiles, 0, 1), 0, 0))


def _modulate(x_all, mods, n_lat_tiles):
    t_all, d = x_all.shape
    row = pl.BlockSpec((ROW_TILE, d), lambda i: (i, 0))
    return pl.pallas_call(
        _modulate_kernel,
        grid=(t_all // ROW_TILE,),
        in_specs=[row, _mods_spec(d, n_lat_tiles)],
        out_specs=row,
        out_shape=jax.ShapeDtypeStruct((t_all, d), BF16),
        compiler_params=_cparams(1),
    )(x_all, mods)


def _matmul_kernel(x_ref, w_ref, o_ref):
    o_ref[...] = _nn(x_ref[...], w_ref[...])


def _inproj(xm, w_bf):
    t_all, d = xm.shape
    n = w_bf.shape[1]
    tm = MM_ROW_TILE if t_all % MM_ROW_TILE == 0 else ROW_TILE
    tn = min(n, MM_COL_TILE)
    return pl.pallas_call(
        _matmul_kernel,
        grid=(t_all // tm, n // tn),
        in_specs=[pl.BlockSpec((tm, d), lambda i, j: (i, 0)),
                  pl.BlockSpec((d, tn), lambda i, j: (0, j))],
        out_specs=pl.BlockSpec((tm, tn), lambda i, j: (i, j)),
        out_shape=jax.ShapeDtypeStruct((t_all, n), F32),
        compiler_params=_cparams(2),
    )(xm, w_bf)


def _ln_rows(z, g, b):
    mu = jnp.mean(z, axis=-1, keepdims=True)
    zc = z - mu
    return zc * lax.rsqrt(jnp.mean(zc * zc, axis=-1, keepdims=True) + LN_EPS) * g + b


def _outproj_kernel(*refs, glu):
    hf_ref, hb_ref, gate_ref = refs[:3]
    refs = refs[3:]
    if glu:
        (y5_ref, u5_ref, ds_ref, gw_ref, gb_ref, w1_ref, w2_ref, x_ref, m_ref, g_ref, b_ref,
         x1_ref, h2_ref) = refs
        y = _gelu_tanh(y5_ref[...] + ds_ref[...] * u5_ref[...])
        a2 = y * _sigmoid(_nn(y.astype(BF16), gw_ref[...]) + gb_ref[...])
    else:
        a2_ref, w1_ref, w2_ref, x_ref, m_ref, g_ref, b_ref, x1_ref, h2_ref = refs
        a2 = a2_ref[...]
    heads = []
    for h in range(N_HEADS):
        hs = slice(h * HEAD_DIM, (h + 1) * HEAD_DIM)
        gx = gate_ref[:, hs]
        gate = gx * _sigmoid(gx) if glu else _sigmoid(gx)
        heads.append((_head_norm(hf_ref[:, hs] + hb_ref[:, hs]) * gate).astype(BF16))
    a1 = jnp.concatenate(heads, axis=1)
    y = _nn(a1, w1_ref[...]) + _nn(a2.astype(BF16), w2_ref[...])
    x1 = _ln_rows(ALPHA * x_ref[...] + m_ref[2:3, :] * y, g_ref[...], b_ref[...])
    x1_ref[...] = x1
    h2_ref[...] = (x1 * (1.0 + m_ref[4:5, :]) + m_ref[3:4, :]).astype(BF16)


def _outproj(scan_parts, a2_parts, w_out_bf, x_all, mods, ln_g, ln_b, n_lat_tiles, glu_params=None):
    t_all, d = x_all.shape
    half = HALF
    row = lambda i: (i, 0)
    const = lambda i: (0, 0)
    a2_parts = list(scan_parts) + list(a2_parts)
    a2_specs = [pl.BlockSpec((ROW_TILE, half), lambda i, c=c: (i, c)) for _, c in a2_parts]
    a2_args = [a for a, _ in a2_parts]
    if glu_params is not None:
        d_skip, glu_w_bf, glu_b = glu_params
        a2_specs += [pl.BlockSpec((1, half), const), pl.BlockSpec((half, half), const),
                     pl.BlockSpec((1, half), const)]
        a2_args += [d_skip.reshape(1, half), glu_w_bf, glu_b.reshape(1, half)]
    return pl.pallas_call(
        functools.partial(_outproj_kernel, glu=glu_params is not None),
        grid=(t_all // ROW_TILE,),
        in_specs=a2_specs + [
            pl.BlockSpec((half, d), const), pl.BlockSpec((half, d), const),
            pl.BlockSpec((ROW_TILE, d), row),
            pl.BlockSpec((None, 8, d), lambda i: (jnp.where(i < n_lat_tiles, 0, 1), 0, 0)),
            pl.BlockSpec((1, d), const), pl.BlockSpec((1, d), const)],
        out_specs=[pl.BlockSpec((ROW_TILE, d), row), pl.BlockSpec((ROW_TILE, d), row)],
        out_shape=[jax.ShapeDtypeStruct((t_all, d), F32), jax.ShapeDtypeStruct((t_all, d), BF16)],
        compiler_params=_cparams(1),
    )(*a2_args, w_out_bf[:half], w_out_bf[half:], x_all, mods, ln_g.reshape(1, d), ln_b.reshape(1, d))


def _ln2_kernel(x_ref, p_ref, m_ref, g_ref, b_ref, *rest):
    x2 = _ln_rows(ALPHA * x_ref[...] + m_ref[5:6, :] * p_ref[...].T, g_ref[...], b_ref[...])
    if len(rest) == 1:
        rest[0][...] = x2
    else:
        mn_ref, o_ref, xm_ref = rest
        o_ref[...] = x2
        xm_ref[...] = _modulated(x2, mn_ref)


def _ln2(x1, peer_out_t, mods, ln_g, ln_b, n_lat_tiles, mods_next=None):
    t_all, d = x1.shape
    row = pl.BlockSpec((ROW_TILE, d), lambda i: (i, 0))
    col = pl.BlockSpec((d, ROW_TILE), lambda i: (0, i))
    vec = pl.BlockSpec((1, d), lambda i: (0, 0))
    nxt = mods_next is not None
    n_rows = t_all if nxt else n_lat_tiles * ROW_TILE
    return pl.pallas_call(
        _ln2_kernel,
        grid=(n_rows // ROW_TILE,),
        in_specs=[row, col, _mods_spec(d, n_lat_tiles), vec, vec] + ([_mods_spec(d, n_lat_tiles)] if nxt else []),
        out_specs=[row, row] if nxt else row,
        out_shape=([jax.ShapeDtypeStruct((t_all, d), F32), jax.ShapeDtypeStruct((t_all, d), BF16)] if nxt
                   else jax.ShapeDtypeStruct((n_rows, d), F32)),
        compiler_params=_cparams(1),
    )(x1, peer_out_t, mods, ln_g.reshape(1, d), ln_b.reshape(1, d), *([mods_next] if nxt else []))


def _scan_block(rev, n_chunks, n_lat_chunks):
    if rev:
        return lambda s: n_chunks - 1 - s
    n_ctx = n_chunks - n_lat_chunks
    return lambda s: jnp.where(s < n_ctx, s + n_lat_chunks, s - n_ctx)


def _scan_mask(rev, n):
    row = lax.broadcasted_iota(jnp.int32, (n, n), 0)
    col = lax.broadcasted_iota(jnp.int32, (n, n), 1)
    return (col >= row) if rev else (col <= row)


def _rope(x, cos, sin):
    lane = lax.broadcasted_iota(jnp.int32, x.shape, 1)
    partner = jnp.where((lane % 64) < 32, pltpu.roll(x, 96, 1), pltpu.roll(x, 32, 1))
    return x * cos + partner * sin


def _mlstm_kernel(*refs):
    n_in = 5
    fw_in, bw_in = refs[:n_in], refs[n_in:2 * n_in]
    bias_ref, biast_ref, out_f, out_b = refs[2 * n_in:2 * n_in + 4]
    state = refs[2 * n_in + 4:]

    @pl.when(pl.program_id(0) == 0)
    def _():
        for r in state:
            r[...] = jnp.zeros_like(r)

    _mlstm_chunk(*fw_in, bias_ref, biast_ref, out_f, *state[:2], rev=False)
    _mlstm_chunk(*bw_in, bias_ref, biast_ref, out_b, *state[2:], rev=True)


def _mlstm_chunk(q_ref, k_ref, v_ref, g_ref, gt_ref, bias_ref, biast_ref, out_ref, cn_ref, m_ref, *, rev):
    n = q_ref.shape[0]
    mask = _scan_mask(rev, n)
    tri = mask.astype(BF16)
    pre_c = g_ref[...] + bias_ref[...]
    pre_r = gt_ref[...] + biast_ref[...]
    b_c = sum(_nn(tri, p) for p in _split3(_log_sigmoid(pre_c)))
    b_r = sum(_nt(p, tri) for p in _split3(_log_sigmoid(pre_r)))
    ones = jnp.ones((n, HEAD_DIM), BF16)
    last = 0 if rev else n - 1
    off = N_HEADS if rev else 0

    for h in range(N_HEADS):
        hs = slice(h * HEAD_DIM, (h + 1) * HEAD_DIM)
        ci, cf = off + h, 2 * N_HEADS + off + h
        bc, br = b_c[:, cf:cf + 1], b_r[cf:cf + 1, :]
        ic, ir = pre_c[:, ci:ci + 1], pre_r[ci:ci + 1, :]
        m_prev = m_ref[h:h + 1, 0:1]
        logw = jnp.where(mask, bc - br + ir, NEG)
        inter = bc + m_prev
        m_t = jnp.maximum(inter, jnp.max(logw, axis=1, keepdims=True))
        w_inter = jnp.exp(inter - m_t)
        qb, kb = q_ref[:, hs], k_ref[:, hs]
        v_ext = jnp.concatenate([v_ref[:, hs].astype(BF16), ones], axis=1)
        s = _nt(qb, kb) * jnp.exp(logw - m_t)
        cn_old = cn_ref[h]
        both = _nn(s.astype(BF16), v_ext) + _nn(qb, cn_old.astype(BF16)) * w_inter
        den = jnp.maximum(jnp.abs(both[:, HEAD_DIM:]), jnp.exp(-m_t))
        out_ref[:, hs] = both[:, :HEAD_DIM] / den
        m_new = m_t[last:last + 1, :]
        b_last = bc[last:last + 1, :]
        w_old = jnp.exp(b_last + m_prev - m_new)
        kw = kb.astype(F32) * jnp.exp(b_last - bc + ic - m_new)
        cn_ref[h] = w_old * cn_old + _tn(kw.astype(BF16), v_ext)
        m_ref[h:h + 1, :] = jnp.broadcast_to(m_new, (1, HEAD_DIM))


def _rope_kernel(q_ref, k_ref, cos_ref, sin_ref, qo_ref, ko_ref):
    cos, sin = cos_ref[...], sin_ref[...]
    for h in range(N_HEADS):
        hs = slice(h * HEAD_DIM, (h + 1) * HEAD_DIM)
        qo_ref[:, hs] = (_rope(q_ref[:, hs], cos, sin) * QK_SCALE).astype(BF16)
        ko_ref[:, hs] = _rope(k_ref[:, hs], cos, sin).astype(BF16)


def _rope_qk(proj, cos_t, sin_t):
    t_all = proj.shape[0]
    tok = pl.BlockSpec((ROW_TILE, 128), lambda i: (i, 0))
    return pl.pallas_call(
        _rope_kernel,
        grid=(t_all // ROW_TILE,),
        in_specs=[pl.BlockSpec((ROW_TILE, HALF), lambda i: (i, 0)),
                  pl.BlockSpec((ROW_TILE, HALF), lambda i: (i, 1)), tok, tok],
        out_specs=[pl.BlockSpec((ROW_TILE, HALF), lambda i: (i, 0))] * 2,
        out_shape=[jax.ShapeDtypeStruct((t_all, HALF), BF16)] * 2,
        compiler_params=_cparams(1),
    )(proj, proj, cos_t, sin_t)


def _mlstm(proj, gates, gates_t, cos_t, sin_t, igate_b, fgate_b, n_lat_chunks):
    t_all = proj.shape[0]
    n = SCAN_CHUNK
    n_chunks = t_all // n
    bias = jnp.zeros((1, 128), F32).at[0, :4 * N_HEADS].set(
        jnp.concatenate([igate_b.reshape(-1), fgate_b.reshape(-1)]))
    q_rot, k_rot = _rope_qk(proj, cos_t, sin_t)

    def dir_specs(rev):
        blk = _scan_block(rev, n_chunks, n_lat_chunks)
        colblk = lambda c: pl.BlockSpec((n, HALF), lambda s, c=c: (blk(s), c))
        ins = [colblk(0), colblk(0), colblk(2), pl.BlockSpec((n, 128), lambda s: (blk(s), 0)),
               pl.BlockSpec((128, n), lambda s: (0, blk(s)))]
        return ins, pl.BlockSpec((n, HALF), lambda s: (blk(s), 0))

    (in_f, out_f), (in_b, out_b) = dir_specs(False), dir_specs(True)
    dir_args = [q_rot, k_rot, proj, gates, gates_t]
    state = [pltpu.VMEM((N_HEADS, HEAD_DIM, 2 * HEAD_DIM), F32), pltpu.VMEM((N_HEADS, HEAD_DIM), F32)]
    return pl.pallas_call(
        _mlstm_kernel,
        grid=(n_chunks,),
        in_specs=in_f + in_b + [pl.BlockSpec((1, 128), lambda s: (0, 0)), pl.BlockSpec((128, 1), lambda s: (0, 0))],
        out_specs=[out_f, out_b],
        out_shape=[jax.ShapeDtypeStruct((t_all, HALF), F32)] * 2,
        scratch_shapes=state + state,
        compiler_params=_cparams(1),
    )(*dir_args, *dir_args, bias, bias.reshape(128, 1))


def _na_kernel(q_ref, kp_ref, kc_ref, kn_ref, vp_ref, vc_ref, vn_ref, kx_ref, vx_ref, bias_ref, o_ref):
    qb = (q_ref[...] * QK_SCALE).astype(BF16)
    kk = jnp.concatenate([kp_ref[...], kc_ref[...], kn_ref[...]], axis=0).astype(BF16)
    vv = jnp.concatenate([vp_ref[...], vc_ref[...], vn_ref[...]], axis=0).astype(BF16)
    s_loc = _nt(qb, kk) + bias_ref[...]
    s_ctx = _nt(qb, kx_ref[...].astype(BF16))
    m = jnp.maximum(jnp.max(s_loc, axis=1, keepdims=True), jnp.max(s_ctx, axis=1, keepdims=True))
    p_loc = jnp.exp(s_loc - m)
    p_ctx = jnp.exp(s_ctx - m)
    l = jnp.sum(p_loc, axis=1, keepdims=True) + jnp.sum(p_ctx, axis=1, keepdims=True)
    o_ref[...] = (_nn(p_loc.astype(BF16), vv) + _nn(p_ctx.astype(BF16), vx_ref[...].astype(BF16))) / l


def _na_bias(rpb, rows):
    qn = NA_QROWS * GRID_W
    kn = 2 * qn
    ql = np.arange(qn)
    kl = np.arange(kn)
    out = []
    nqb = rows // NA_QROWS
    lr, krl = np.arange(NA_QROWS)[:, None], np.arange(2 * NA_QROWS)[None, :]
    dr = np.clip(krl - NA_QROWS // 2 - lr + NA_WIN_ROWS - 1, 0, 2 * NA_WIN_ROWS - 2)
    col = np.arange(GRID_W)
    dc = np.clip(col[None, :] - col[:, None] + NA_WIN_COLS - 1, 0, 2 * NA_WIN_COLS - 2)
    oh_r = jnp.asarray(dr[..., None] == np.arange(2 * NA_WIN_ROWS - 1), F32)
    oh_c = jnp.asarray(dc[..., None] == np.arange(2 * NA_WIN_COLS - 1), F32)
    table = jnp.einsum('hab,lka,qcb->hlqkc', rpb, oh_r, oh_c,
                       precision=lax.Precision.HIGHEST).reshape(rpb.shape[0], qn, kn)
    for qb in (0, 1, nqb - 1):
        qr = NA_QROWS * qb + ql // GRID_W
        qc = ql % GRID_W
        kr = NA_QROWS * qb - NA_QROWS // 2 + kl // GRID_W
        kc = kl % GRID_W
        r0 = np.clip(qr - NA_WIN_ROWS // 2, 0, rows - NA_WIN_ROWS)
        c0 = np.clip(qc - NA_WIN_COLS // 2, 0, GRID_W - NA_WIN_COLS)
        ok = ((kr[None, :] >= r0[:, None]) & (kr[None, :] < r0[:, None] + NA_WIN_ROWS)
              & (kc[None, :] >= c0[:, None]) & (kc[None, :] < c0[:, None] + NA_WIN_COLS)
              & (kr[None, :] >= 0) & (kr[None, :] < rows))
        out.append(jnp.where(jnp.asarray(ok)[None], table, NEG))
    return jnp.stack(out)


def _na(proj, rpb, t_lat, t_ctx):
    rows = t_lat // GRID_W
    qn = NA_QROWS * GRID_W
    hn = qn // 2
    nqb = rows // NA_QROWS
    assert nqb >= 2 and t_ctx == hn
    bias = _na_bias(rpb.astype(F32), rows)
    last_half = 2 * nqb - 1
    ctx_blk = t_lat // hn
    cls = lambda b: jnp.where(b == 0, 0, jnp.where(b == nqb - 1, 2, 1))

    def kv_specs(col0):
        return [pl.BlockSpec((hn, HEAD_DIM), lambda h, b: (jnp.maximum(2 * b - 1, 0), col0 + h)),
                pl.BlockSpec((qn, HEAD_DIM), lambda h, b: (b, col0 + h)),
                pl.BlockSpec((hn, HEAD_DIM), lambda h, b: (jnp.minimum(2 * b + 2, last_half), col0 + h))]

    kcol, vcol = 5 * N_HEADS, 6 * N_HEADS
    return pl.pallas_call(
        _na_kernel,
        grid=(N_HEADS, nqb),
        in_specs=[pl.BlockSpec((qn, HEAD_DIM), lambda h, b: (b, 4 * N_HEADS + h))]
        + kv_specs(kcol) + kv_specs(vcol)
        + [pl.BlockSpec((hn, HEAD_DIM), lambda h, b: (ctx_blk, kcol + h)),
           pl.BlockSpec((hn, HEAD_DIM), lambda h, b: (ctx_blk, vcol + h)),
           pl.BlockSpec((None, None, qn, 2 * qn), lambda h, b: (cls(b), h, 0, 0))],
        out_specs=pl.BlockSpec((qn, HEAD_DIM), lambda h, b: (b, h)),
        out_shape=jax.ShapeDtypeStruct((t_lat, HALF), F32),
        compiler_params=_cparams(2),
    )(proj, proj, proj, proj, proj, proj, proj, proj, proj, bias)


def _ctx_attn_kernel(q_ref, k_ref, v_ref, o_ref):
    qb = (q_ref[...] * QK_SCALE).astype(BF16)
    s = _nt(qb, k_ref[...].astype(BF16))
    p = jnp.exp(s - jnp.max(s, axis=1, keepdims=True))
    o_ref[...] = _nn(p.astype(BF16), v_ref[...].astype(BF16)) / jnp.sum(p, axis=1, keepdims=True)


def _ctx_attn(proj, t_lat, t_ctx):
    blk = t_lat // t_ctx
    spec = lambda c: pl.BlockSpec((t_ctx, HEAD_DIM), lambda h, c=c: (blk, c * N_HEADS + h))
    return pl.pallas_call(
        _ctx_attn_kernel,
        grid=(N_HEADS,),
        in_specs=[spec(4), spec(5), spec(6)],
        out_specs=pl.BlockSpec((t_ctx, HEAD_DIM), lambda h: (0, h)),
        out_shape=jax.ShapeDtypeStruct((t_ctx, HALF), F32),
        compiler_params=_cparams(1),
    )(proj, proj, proj)


def _gla_inter(q_ref, i_ref, f_ref, lb_ref, out_ref, st_ref, k_scr, b_scr, *, rev):
    n = q_ref.shape[0]
    mask = _scan_mask(rev, n)
    tri = mask.astype(BF16)
    last = 0 if rev else n - 1
    x = f_ref[...]
    key = lb_ref[0:1, :] * _sigmoid(-x)
    la, lc = lb_ref[1:2, :], lb_ref[2:3, :] + _log_sigmoid(x)
    logf = jnp.maximum(la, lc) + jnp.log1p(jnp.exp(-jnp.abs(la - lc)))
    b = sum(_nn(tri, p) for p in _split3(logf))
    b_end = b[last:last + 1, :]
    qx = q_ref[...]
    q = qx * _sigmoid(qx)
    qt = q * jnp.exp(b)
    kdec = key * jnp.exp(b_end - b)
    w_end = jnp.exp(b_end)
    k_scr[...] = key
    b_scr[...] = b
    for h in range(N_HEADS):
        hs = slice(h * HEAD_DIM, (h + 1) * HEAD_DIM)
        st = st_ref[h]
        vb = i_ref[:, hs].astype(BF16)
        out_ref[:, hs] = _nt(qt[:, hs].astype(BF16), st.astype(BF16))
        st_ref[h] = st * w_end[:, hs] + _tn(vb, kdec[:, hs].astype(BF16))
    return mask, q, b_end


def _gla_intra_factored(mask, q, b_end, i_ref, out_ref, k_scr, b_scr):
    mid = 0.5 * b_end
    qm = q * jnp.exp(b_scr[...] - mid)
    km = k_scr[...] * jnp.exp(mid - b_scr[...])
    for h in range(N_HEADS):
        hs = slice(h * HEAD_DIM, (h + 1) * HEAD_DIM)
        att = jnp.where(mask, _nt(qm[:, hs].astype(BF16), km[:, hs].astype(BF16)), 0.0)
        out_ref[:, hs] += _nn(att.astype(BF16), i_ref[:, hs].astype(BF16))


def _gla_intra_exact(q, i_ref, out_ref, k_scr, b_scr, *, rev):
    n = q.shape[0]
    t_idx = lax.broadcasted_iota(jnp.int32, (n, 1), 0)
    for h in range(N_HEADS):
        hs = slice(h * HEAD_DIM, (h + 1) * HEAD_DIM)
        q_h = q[:, hs]
        b_h = b_scr[:, hs]

        def body(grp, acc):
            base = pl.multiple_of(grp * 8, 8)
            b8, k8, v8 = b_scr[pl.ds(base, 8), hs], k_scr[pl.ds(base, 8), hs], i_ref[pl.ds(base, 8), hs]
            for r in range(8):
                s = base + r
                e = jnp.exp(jnp.minimum(b_h - b8[r:r + 1, :], 0.0))
                a = jnp.sum(q_h * k8[r:r + 1, :] * e, axis=1, keepdims=True)
                a = jnp.where((t_idx <= s) if rev else (t_idx >= s), a, 0.0)
                acc = acc + a * v8[r:r + 1, :]
            return acc

        out_ref[:, hs] += lax.fori_loop(0, n // 8, body, jnp.zeros((n, HEAD_DIM), F32))


def _gla_kernel(*refs):
    fw_in, bw_in, lb_ref = refs[0:3], refs[3:6], refs[6]
    out_f, out_b = refs[7:9]
    st_f, k_f, b_f, st_b, k_b, b_b = refs[9:]

    @pl.when(pl.program_id(0) == 0)
    def _():
        st_f[...] = jnp.zeros_like(st_f)
        st_b[...] = jnp.zeros_like(st_b)

    mask_f, q_f, end_f = _gla_inter(*fw_in, lb_ref, out_f, st_f, k_f, b_f, rev=False)
    mask_b, q_b, end_b = _gla_inter(*bw_in, lb_ref, out_b, st_b, k_b, b_b, rev=True)
    safe = jnp.minimum(jnp.min(end_f), jnp.min(end_b)) >= 2.0 * GLA_SAFE_LOG

    @pl.when(safe)
    def _():
        _gla_intra_factored(mask_f, q_f, end_f, fw_in[1], out_f, k_f, b_f)
        _gla_intra_factored(mask_b, q_b, end_b, bw_in[1], out_b, k_b, b_b)

    @pl.when(jnp.logical_not(safe))
    def _():
        _gla_intra_exact(q_f, fw_in[1], out_f, k_f, b_f, rev=False)
        _gla_intra_exact(q_b, bw_in[1], out_b, k_b, b_b, rev=True)


def _gla(proj, lb, n_lat_chunks):
    t_all = proj.shape[0]
    n = GLA_CHUNK
    n_chunks = t_all // n

    def dir_specs(rev):
        blk = _scan_block(rev, n_chunks, n_lat_chunks)
        colblk = lambda c: pl.BlockSpec((n, HALF), lambda s, c=c: (blk(s), c))
        return [colblk(0), colblk(1), colblk(3 if rev else 2)], pl.BlockSpec((n, HALF), lambda s: (blk(s), 0))

    (in_f, out_f), (in_b, out_b) = dir_specs(False), dir_specs(True)
    state = [pltpu.VMEM((N_HEADS, HEAD_DIM, HEAD_DIM), F32), pltpu.VMEM((n, HALF), F32),
             pltpu.VMEM((n, HALF), F32)]
    lb = lb.reshape(1, HALF)
    lb_rows = jnp.concatenate([1.0 - lb, jnp.log(lb), jnp.log1p(-lb)], axis=0)
    return pl.pallas_call(
        _gla_kernel,
        grid=(n_chunks,),
        in_specs=in_f + in_b + [pl.BlockSpec((3, HALF), lambda s: (0, 0))],
        out_specs=[out_f, out_b],
        out_shape=[jax.ShapeDtypeStruct((t_all, HALF), F32)] * 2,
        scratch_shapes=state + state,
        compiler_params=_cparams(1),
    )(*([proj] * 6), lb_rows)


def _s5_mats(a_re, a_im, log_dt, b_re, b_im, c_re, c_im):
    hp = lax.Precision.HIGHEST
    n = S5_CHUNK
    dt = jnp.exp(log_dt)[..., None]
    la_re, la_im = a_re * dt, a_im * dt
    mag = jnp.exp(la_re)
    ab_re, ab_im = mag * jnp.cos(la_im), mag * jnp.sin(la_im)
    nr, ni = ab_re - 1.0, ab_im
    den = jnp.square(a_re) + jnp.square(a_im)
    cr = (nr * a_re + ni * a_im) / den
    ci = (ni * a_re - nr * a_im) / den
    bb_re = cr[..., None] * b_re - ci[..., None] * b_im
    bb_im = cr[..., None] * b_im + ci[..., None] * b_re

    def apow(tau):
        tau = jnp.asarray(tau, F32)[None, None, :, None]
        m = jnp.exp(tau * la_re[:, :, None, :])
        return m * jnp.cos(tau * la_im[:, :, None, :]), m * jnp.sin(tau * la_im[:, :, None, :])

    def c_apow(tau):
        pr, pi = apow(tau)
        return (c_re[:, :, None] * pr[:, :, :, None] - c_im[:, :, None] * pi[:, :, :, None],
                c_re[:, :, None] * pi[:, :, :, None] + c_im[:, :, None] * pr[:, :, :, None])

    ca_re, ca_im = c_apow(np.arange(n))
    kern = jnp.einsum('dgtip,dgpj->dgtij', jnp.concatenate([ca_re, -ca_im], axis=-1),
                      jnp.concatenate([bb_re, bb_im], axis=2), precision=hp)
    s_idx, t_idx = np.arange(n)[:, None], np.arange(n)[None, :]
    g = a_re.shape[1]
    lp = n * S5_GROUP

    def toeplitz(kd, lag):
        onehot = jnp.asarray(lag[None] == np.arange(n)[:, None, None], F32)
        return jnp.einsum('lst,glij->gsjti', onehot, kd, precision=hp).reshape(g, lp, lp)

    kmat = toeplitz(kern[0], t_idx - s_idx) + toeplitz(kern[1], s_idx - t_idx)

    def emat(d, tau):
        pr, pi = apow(tau)
        pr, pi = pr[d][:, :, None, :], pi[d][:, :, None, :]
        br, bi = bb_re[d].transpose(0, 2, 1)[:, None], bb_im[d].transpose(0, 2, 1)[:, None]
        return jnp.concatenate([pr * br - pi * bi, pr * bi + pi * br], axis=-1).reshape(g, lp, 2 * S5_STATE)

    def fmat(d, tau):
        fr, fi = c_apow(tau)
        fr, fi = fr[d], fi[d]
        return jnp.concatenate([fr, -fi], axis=-1).transpose(0, 3, 1, 2).reshape(g, 2 * S5_STATE, lp)

    e_all = jnp.concatenate([emat(0, n - 1 - np.arange(n)), emat(1, np.arange(n))], axis=-1)
    f_all = jnp.concatenate([fmat(0, np.arange(n) + 1), fmat(1, n - np.arange(n))], axis=1)
    pr, pi = apow(np.array([n]))
    pr, pi = pr[:, :, 0], pi[:, :, 0]
    ar2 = jnp.concatenate([pr[0], pr[0], pr[1], pr[1]], axis=-1)
    ai2 = jnp.concatenate([-pi[0], pi[0], -pi[1], pi[1]], axis=-1)
    return kmat.astype(BF16), e_all.astype(BF16), f_all.astype(BF16), ar2, ai2


def _s5_state_kernel(u_ref, e_ref, s_ref):
    s_ref[...] = _nn(u_ref[...], e_ref[...])


def _s5_scan_kernel(sf_ref, sb_ref, ar_ref, ai_ref, hf_ref, hb_ref, stf_ref, stb_ref):
    nb = sf_ref.shape[0]
    w = 2 * S5_STATE
    ar, ai = ar_ref[...], ai_ref[...]

    @pl.when(pl.program_id(0) == 0)
    def _():
        stf_ref[...] = jnp.zeros_like(stf_ref)
        stb_ref[...] = jnp.zeros_like(stb_ref)

    def step(h, s, lo):
        return ar[:, lo:lo + w] * h + ai[:, lo:lo + w] * pltpu.roll(h, S5_STATE, 1) + s

    hf, hb = stf_ref[...], stb_ref[...]
    for j in range(nb):
        hf_ref[j] = hf
        hf = step(hf, sf_ref[j], 0)
        hb_ref[nb - 1 - j] = hb
        hb = step(hb, sb_ref[nb - 1 - j], w)
    stf_ref[...] = hf
    stb_ref[...] = hb


def _s5_out_kernel(u_ref, k_ref, h_ref, f_ref, y_ref):
    y_ref[...] = _nn(u_ref[...], k_ref[...]) + _nn(h_ref[...].astype(BF16), f_ref[...])


def _s5(u, mats, n_lat_chunks):
    kmat, emat, fmat, ar2, ai2 = mats
    t_all = u.shape[0]
    n = S5_CHUNK
    nc = t_all // n
    g = S5_GROUPS
    lp = n * S5_GROUP
    w4 = 4 * S5_STATE
    ug = u.astype(BF16).reshape(nc, n, g, S5_GROUP).transpose(2, 0, 1, 3).reshape(g, nc, lp)
    grp = lambda a, b: pl.BlockSpec((None, a, b), lambda i: (i, 0, 0))
    s_end = pl.pallas_call(
        _s5_state_kernel, grid=(g,),
        in_specs=[grp(nc, lp), grp(lp, w4)], out_specs=grp(nc, w4),
        out_shape=jax.ShapeDtypeStruct((g, nc, w4), F32), compiler_params=_cparams(1),
    )(ug, emat)
    nb = S5_SCAN_BLOCK
    w2 = 2 * S5_STATE
    fw_blk = _scan_block(False, nc // nb, n_lat_chunks // nb)
    bw_blk = _scan_block(True, nc // nb, n_lat_chunks // nb)
    s_t = s_end.transpose(1, 0, 2)
    coef = pl.BlockSpec((g, w4), lambda i: (0, 0))
    h_fw, h_bw = pl.pallas_call(
        _s5_scan_kernel, grid=(nc // nb,),
        in_specs=[pl.BlockSpec((nb, g, w2), lambda i: (fw_blk(i), 0, 0)),
                  pl.BlockSpec((nb, g, w2), lambda i: (bw_blk(i), 0, 1)), coef, coef],
        out_specs=[pl.BlockSpec((nb, g, w2), lambda i: (fw_blk(i), 0, 0)),
                   pl.BlockSpec((nb, g, w2), lambda i: (bw_blk(i), 0, 0))],
        out_shape=[jax.ShapeDtypeStruct((nc, g, w2), F32)] * 2,
        scratch_shapes=[pltpu.VMEM((g, w2), F32)] * 2,
        compiler_params=_cparams(1),
    )(s_t, s_t, ar2, ai2)
    h_in = jnp.concatenate([h_fw, h_bw], axis=-1).transpose(1, 0, 2)
    y = pl.pallas_call(
        _s5_out_kernel, grid=(g,),
        in_specs=[grp(nc, lp), grp(lp, lp), grp(nc, w4), grp(w4, lp)], out_specs=grp(nc, lp),
        out_shape=jax.ShapeDtypeStruct((g, nc, lp), F32), compiler_params=_cparams(1),
    )(ug, kmat, h_in, fmat)
    return y.reshape(g, nc, n, S5_GROUP).transpose(1, 2, 0, 3).reshape(t_all, HALF)


def _top_rows(x, k):
    rows = []
    for _ in range(k):
        mx = jnp.max(x, axis=0, keepdims=True)
        rows.append(mx)
        x = jnp.where(x == mx, NEG, x)
    return rows


def _peer_route_kernel(ht_ref, wq_ref, keys_ref, e1_ref, e0_ref, th_ref, top_scr):
    kk = PEER_TOPK
    qt = _nn(wq_ref[...], ht_ref[...])
    for hp in range(2 * PEER_HEADS):
        sc = _nn(keys_ref[hp], qt[hp * HEAD_DIM:(hp + 1) * HEAD_DIM, :].astype(BF16))
        h, p = divmod(hp, 2)
        if p == 0:
            e0_ref[h] = sc
        else:
            e1_ref[h] = sc
        for r, mx in enumerate(_top_rows(sc, kk + 1)):
            top_scr[hp, r:r + 1, :] = mx
        top_scr[hp, kk + 1:, :] = jnp.full((PEER_TOP_ROWS - kk - 1, sc.shape[1]), NEG, F32)

    row = lax.broadcasted_iota(jnp.int32, (PEER_TOP_ROWS, 1), 0)
    for h in range(PEER_HEADS):
        a = top_scr[2 * h]
        b = top_scr[2 * h + 1]
        parts = [a[0:1] + b, a[1:2] + b[0:8], a[2:3] + b[0:8], a[3:4] + b[0:8]]
        parts += [jnp.where(row >= 4, a + b[j:j + 1], NEG) for j in range(3)]
        cand = jnp.maximum(jnp.concatenate(parts, axis=0), NEG)
        top = _top_rows(cand, kk + 1)
        thr = 0.5 * (top[kk - 1] + top[kk])
        z = jnp.sum(jnp.where(cand >= thr, jnp.exp(cand - top[0]), 0.0), axis=0, keepdims=True)
        e0_ref[h] = jnp.exp(e0_ref[h] - a[0:1]) / z
        e1_ref[h] = jnp.exp(e1_ref[h] - b[0:1])
        th_ref[h:h + 1, :] = jnp.exp(thr - top[0]) / z


def _peer_expert_kernel(ht_ref, e1_ref, e0_ref, th_ref, *rest, n_tiles):
    u_refs = rest[:PEER_U_SPLIT]
    vt_ref, o_ref, at0_scr, at1_scr, g0_scr, g1_scr = rest[PEER_U_SPLIT:]
    s = pl.program_id(1)
    assert n_tiles % 2 == 0 and n_tiles >= 4
    ni = u_refs[0].shape[0] // PEER_N_KEYS
    kc = u_refs[0].shape[1]
    at_scr, g_scr = (at0_scr, at1_scr), (g0_scr, g1_scr)

    def stages(par, dot1, gate, dot2, first=False):
        if dot1:
            at_scr[par][...] = functools.reduce(
                lambda a, b: a + b, [_nn(u_k[...], ht_ref[k * kc:(k + 1) * kc, :]) for k, u_k in enumerate(u_refs)])
        if gate:
            at_old, g_new = at_scr[1 - par], g_scr[1 - par]
            nj, nc = PEER_GATE_ROWS, 128
            for ii in range(ni):
                i = (s - 1) * ni + ii
                e0_rows = [e0_ref[h, pl.ds(i, 1), :] for h in range(PEER_HEADS)]
                for c in range(at_old.shape[1] // nc):
                    cs = slice(c * nc, (c + 1) * nc)
                    for jb in range(PEER_N_KEYS // nj):
                        js = slice(jb * nj, (jb + 1) * nj)
                        w = None
                        for h in range(PEER_HEADS):
                            p = e1_ref[h, js, cs] * e0_rows[h][:, cs]
                            t = jnp.where(p >= th_ref[h:h + 1, cs], p, 0.0)
                            w = t if w is None else w + t
                        rs = slice(ii * PEER_N_KEYS + jb * nj, ii * PEER_N_KEYS + (jb + 1) * nj)
                        g_new[rs, cs] = (_gelu_tanh(at_old[rs, cs]) * w).astype(BF16)
        if dot2:
            prod = _nn(vt_ref[...], g_scr[par][...])
            if first:
                o_ref[...] = prod
            else:
                o_ref[...] += prod

    mid = jnp.logical_and(s > 2, s < n_tiles)
    pl.when(s == 0)(lambda: stages(0, True, False, False))
    pl.when(s == 1)(lambda: stages(1, True, True, False))
    pl.when(s == 2)(lambda: stages(0, True, True, True, first=True))
    pl.when(jnp.logical_and(mid, s % 2 == 0))(lambda: stages(0, True, True, True))
    pl.when(jnp.logical_and(mid, s % 2 == 1))(lambda: stages(1, True, True, True))
    pl.when(s == n_tiles)(lambda: stages(0, False, True, True))
    pl.when(s == n_tiles + 1)(lambda: stages(1, False, False, True))


def _tok3(t):
    return pl.BlockSpec((PEER_HEADS, PEER_N_KEYS, t), lambda i, *_: (0, 0, i))


def _peer_route(h2t, wq_t_bf, keys_bf):
    d, t_all = h2t.shape
    nh, nk = PEER_HEADS, PEER_N_KEYS
    t1 = PEER_P1_TILE
    return pl.pallas_call(
        _peer_route_kernel,
        grid=(t_all // t1,),
        in_specs=[pl.BlockSpec((d, t1), lambda i: (0, i)),
                  pl.BlockSpec(wq_t_bf.shape, lambda i: (0, 0)),
                  pl.BlockSpec(keys_bf.shape, lambda i: (0, 0, 0))],
        out_specs=[_tok3(t1), _tok3(t1), pl.BlockSpec((nh, t1), lambda i: (0, i))],
        out_shape=[jax.ShapeDtypeStruct((nh, nk, t_all), F32)] * 2 + [jax.ShapeDtypeStruct((nh, t_all), F32)],
        scratch_shapes=[pltpu.VMEM((2 * nh, PEER_TOP_ROWS, t1), F32)],
        compiler_params=_cparams(1),
    )(h2t, wq_t_bf, keys_bf)


def _peer(h2t, wq_t_bf, keys_bf, u_bf, vt_bf):
    t_all = h2t.shape[1]
    h2t = jnp.pad(h2t, ((0, 0), (0, -t_all % PEER_TOK_TILE)))
    return _peer_experts(h2t, *_peer_route(h2t, wq_t_bf, keys_bf), u_bf, vt_bf)


def _peer_tables(u_tab, v_tab):
    return u_tab.astype(BF16), v_tab.T.astype(BF16)


def _peer_experts(h2t, e1, e0, th, u_r, vt_r):
    d, t_all = h2t.shape
    tt = PEER_TOK_TILE
    et = PEER_EXP_TILE
    n_tiles = u_r.shape[0] // et
    kc = d // PEER_U_SPLIT
    tok = pl.BlockSpec((d, tt), lambda i, s: (0, i))
    return pl.pallas_call(
        functools.partial(_peer_expert_kernel, n_tiles=n_tiles),
        grid=(t_all // tt, n_tiles + 2),
        in_specs=[tok, _tok3(tt), _tok3(tt), pl.BlockSpec((PEER_HEADS, tt), lambda i, s: (0, i))]
        + [pl.BlockSpec((et, kc), lambda i, s, k=k: (jnp.minimum(s, n_tiles - 1), k))
           for k in range(PEER_U_SPLIT)]
        + [pl.BlockSpec((d, et), lambda i, s: (0, jnp.clip(s - 2, 0, n_tiles - 1)))],
        out_specs=tok,
        out_shape=jax.ShapeDtypeStruct((d, t_all), F32),
        scratch_shapes=[pltpu.VMEM((et, tt), F32)] * 2 + [pltpu.VMEM((et, tt), BF16)] * 2,
        compiler_params=_cparams(2),
    )(h2t, e1, e0, th, *([u_r] * PEER_U_SPLIT), vt_r)


def _rope_tables(t_lat, t_ctx):
    pos = np.arange(t_lat)
    nfreq = HEAD_DIM // 4
    freqs = ROPE_BASE ** (-jnp.arange(nfreq, dtype=F32) / nfreq)
    ang_r = jnp.asarray(pos // GRID_W, F32)[:, None] * freqs
    ang_c = jnp.asarray(pos % GRID_W, F32)[:, None] * freqs
    cos = jnp.concatenate([jnp.cos(ang_r), jnp.cos(ang_r), jnp.cos(ang_c), jnp.cos(ang_c)], axis=1)
    sin = jnp.concatenate([-jnp.sin(ang_r), jnp.sin(ang_r), -jnp.sin(ang_c), jnp.sin(ang_c)], axis=1)
    cos = jnp.concatenate([cos, jnp.ones((t_ctx, HEAD_DIM), F32)], axis=0)
    sin = jnp.concatenate([sin, jnp.zeros((t_ctx, HEAD_DIM), F32)], axis=0)
    return cos, sin


def _peer_and_norm(x1, h2, mods, ln_g, ln_b, n_lat_tiles, w_q, sub_keys, u_tab, v_tab, mods_next=None):
    keys = sub_keys.reshape(2 * PEER_HEADS, PEER_N_KEYS, -1).astype(BF16)
    pt = _peer(h2.T, w_q.T.astype(BF16), keys, *_peer_tables(u_tab, v_tab))
    return _ln2(x1, pt, mods, ln_g, ln_b, n_lat_tiles, mods_next)


def _layer_ab(x_all, mods, t_lat, t_ctx, w_in, w_out, igate_b, fgate_b, rpb, ln_g, ln_b, xm=None):
    n_lat_tiles = t_lat // ROW_TILE
    if xm is None:
        xm = _modulate(x_all, mods, n_lat_tiles)
    ng = 4 * N_HEADS
    w_main = jnp.concatenate([w_in[:, :4 * HALF], w_in[:, 4 * HALF + ng:]], axis=1).astype(BF16)
    w_gate = jnp.pad(w_in[:, 4 * HALF:4 * HALF + ng], ((0, 0), (0, 128 - ng))).astype(BF16)
    proj = _inproj(xm, w_main)
    gates = _inproj(xm, w_gate)
    cos_t, sin_t = _rope_tables(t_lat, t_ctx)
    h_fw, h_bw = _mlstm(proj, gates, gates.T, cos_t, sin_t, igate_b, fgate_b, t_lat // SCAN_CHUNK)
    y_na = jnp.concatenate([_na(proj, rpb, t_lat, t_ctx), _ctx_attn(proj, t_lat, t_ctx)], axis=0)
    return _outproj([(h_fw, 0), (h_bw, 0), (proj, 3)], [(y_na, 0)], w_out.astype(BF16), x_all, mods,
                    ln_g, ln_b, n_lat_tiles)


def _layer_cd(x_all, mods, t_lat, w_in, w_out, lb, s5p, s5_d, glu_w, glu_b, ln_g, ln_b, xm=None):
    n_lat_tiles = t_lat // ROW_TILE
    if xm is None:
        xm = _modulate(x_all, mods, n_lat_tiles)
    proj = _inproj(xm, w_in.astype(BF16))
    o_fw, o_bw = _gla(proj, lb, t_lat // GLA_CHUNK)
    y5 = _s5(proj[:, 5 * HALF:], _s5_mats(*[p.astype(F32) for p in s5p]), t_lat // S5_CHUNK)
    return _outproj([(o_fw, 0), (o_bw, 0), (proj, 4)], [(y5, 0), (proj, 5)], w_out.astype(BF16), x_all, mods,
                    ln_g, ln_b, n_lat_tiles,
                    glu_params=(s5_d.astype(F32), glu_w.astype(BF16), glu_b.astype(F32)))


def kernel(x, c, ctx, c_ctx, ada_w, ada_b, ln_g, ln_b, ab_w_in, ab_w_out, mlstm_igate_b, mlstm_fgate_b, na_rpb, cd_w_in, cd_w_out, hgrn_lb_logits, s5_a_re, s5_a_im, s5_log_dt, s5_b_re, s5_b_im, s5_c_re, s5_c_im, s5_d, s5_glu_w, s5_glu_b, peer_w_q, peer_sub_keys, peer_u, peer_v):
    t_lat, t_ctx = x.shape[1], ctx.shape[1]
    assert x.shape[0] == 1 and t_ctx == ROW_TILE and t_lat % (NA_QROWS * GRID_W) == 0
    depth = ada_w.shape[0]
    lb_soft = jax.nn.softmax(hgrn_lb_logits.astype(F32), axis=0)
    lower_bounds = jnp.cumsum(lb_soft, axis=0) - lb_soft[0]
    x_all = jnp.concatenate([x[0], ctx[0]], axis=0).astype(F32)
    n_lat_tiles = t_lat // ROW_TILE
    mods_all = [_adaln(c, c_ctx, ada_w, ada_b, l) for l in range(depth)]
    xm = _modulate(x_all, mods_all[0], n_lat_tiles)
    for l in range(depth):
        j = l // 2
        mods = mods_all[l]
        if l % 2 == 0:
            x1, h2 = _layer_ab(x_all, mods, t_lat, t_ctx, ab_w_in[j], ab_w_out[j], mlstm_igate_b[j],
                               mlstm_fgate_b[j], na_rpb[j], ln_g[l, 0], ln_b[l, 0], xm=xm)
        else:
            s5p = (s5_a_re[j], s5_a_im[j], s5_log_dt[j], s5_b_re[j], s5_b_im[j], s5_c_re[j], s5_c_im[j])
            x1, h2 = _layer_cd(x_all, mods, t_lat, cd_w_in[j], cd_w_out[j], lower_bounds[l], s5p, s5_d[j],
                               s5_glu_w[j], s5_glu_b[j], ln_g[l, 0], ln_b[l, 0], xm=xm)
        peer_args = (x1, h2, mods, ln_g[l, 1], ln_b[l, 1], n_lat_tiles, peer_w_q[l], peer_sub_keys[l],
                     peer_u[l], peer_v[l])
        if l + 1 < depth:
            x_all, xm = _peer_and_norm(*peer_args, mods_next=mods_all[l + 1])
        else:
            x_all = _peer_and_norm(*peer_args)
    return x_all[None].astype(x.dtype)
```

```python
import functools
import math

import numpy as np
import jax
import jax.numpy as jnp
from jax import lax
from jax.experimental import pallas as pl
from jax.experimental.pallas import tpu as pltpu

F32 = jnp.float32
BF16 = jnp.bfloat16

HEAD_DIM = 128
GRID_W = 64
N_HEADS = 8
HALF = N_HEADS * HEAD_DIM
NA_WIN_ROWS = 8
NA_WIN_COLS = 16
ROPE_BASE = 10000.0
S5_GROUP = 16
S5_STATE = 64
S5_GROUPS = HALF // S5_GROUP
PEER_HEADS = 8
PEER_N_KEYS = 128
PEER_TOPK = 16
PEER_TOP_ROWS = 24
LN_EPS = 1e-5
DEPTH = 2
ALPHA = (2.0 * DEPTH) ** 0.25
QK_SCALE = HEAD_DIM ** -0.5

ROW_TILE = 256
MM_ROW_TILE = 1280
MM_COL_TILE = 1024
SCAN_CHUNK = 128
GLA_CHUNK = 64
S5_CHUNK = 32
S5_SCAN_BLOCK = 8
NA_QROWS = 8
PEER_P1_TILE = 256
PEER_TOK_TILE = 512
PEER_EXP_TILE = 512
PEER_GATE_ROWS = 64
PEER_U_SPLIT = 4
NEG = -1e30
GLA_SAFE_LOG = -80.0
VMEM_LIMIT = 56 * 1024 * 1024


def _cparams(n_axes):
    return pltpu.CompilerParams(dimension_semantics=("arbitrary",) * n_axes,
                                vmem_limit_bytes=VMEM_LIMIT)


def _nt(a, b):
    return lax.dot_general(a, b, (((1,), (1,)), ((), ())), preferred_element_type=F32)


def _tn(a, b):
    return lax.dot_general(a, b, (((0,), (0,)), ((), ())), preferred_element_type=F32)


def _nn(a, b):
    return jnp.dot(a, b, preferred_element_type=F32)


def _split3(x):
    hi = x.astype(BF16)
    r = x - hi.astype(F32)
    mid = r.astype(BF16)
    lo = (r - mid.astype(F32)).astype(BF16)
    return hi, mid, lo


def _log_sigmoid(x):
    return jnp.minimum(x, 0.0) - jnp.log1p(jnp.exp(-jnp.abs(x)))


def _sigmoid(x):
    return 1.0 / (1.0 + jnp.exp(-x))


def _gelu_tanh(x):
    return 0.5 * x * (1.0 + jnp.tanh(math.sqrt(2.0 / math.pi) * (x + 0.044715 * (x * x * x))))


def _head_norm(x):
    mu = jnp.mean(x, axis=-1, keepdims=True)
    xc = x - mu
    return xc * lax.rsqrt(jnp.mean(xc * xc, axis=-1, keepdims=True) + LN_EPS)


def _adaln_kernel(ct_ref, w_ref, b_ref, o_ref):
    ct = ct_ref[...]
    s = ct * _sigmoid(ct)
    w = w_ref[...]
    o_ref[0:1, :] = jnp.sum(s[:, 0:1] * w, axis=0, keepdims=True) + b_ref[...]
    o_ref[1:2, :] = jnp.sum(s[:, 1:2] * w, axis=0, keepdims=True) + b_ref[...]


def _adaln(c_lat, c_ctx, w_all, b_all, layer):
    n_layers, d, n = w_all.shape
    tn = n // 8
    ct = jnp.stack([c_lat.reshape(d), c_ctx.reshape(d)], axis=1)
    out = pl.pallas_call(
        _adaln_kernel,
        grid=(n // tn,),
        in_specs=[pl.BlockSpec((d, 2), lambda j: (0, 0)),
                  pl.BlockSpec((None, d, tn), lambda j: (layer, 0, j)),
                  pl.BlockSpec((None, 1, tn), lambda j: (layer, 0, j))],
        out_specs=pl.BlockSpec((2, tn), lambda j: (0, j)),
        out_shape=jax.ShapeDtypeStruct((2, n), F32),
        compiler_params=_cparams(1),
    )(ct, w_all, b_all.reshape(n_layers, 1, n))
    return jnp.pad(out.reshape(2, 6, d), ((0, 0), (0, 2), (0, 0)))


def _modulated(x, m_ref):
    return (x * (1.0 + m_ref[1:2, :]) + m_ref[0:1, :]).astype(BF16)


def _modulate_kernel(x_ref, m_ref, o_ref):
    o_ref[...] = _modulated(x_ref[...], m_ref)


def _mods_spec(d, n_lat_tiles):
    return pl.BlockSpec((None, 8, d), lambda i: (jnp.where(i < n_lat_tiles, 0, 1), 0, 0))


def _modulate(x_all, mods, n_lat_tiles):
    t_all, d = x_all.shape
    row = pl.BlockSpec((ROW_TILE, d), lambda i: (i, 0))
    return pl.pallas_call(
        _modulate_kernel,
        grid=(t_all // ROW_TILE,),
        in_specs=[row, _mods_spec(d, n_lat_tiles)],
        out_specs=row,
        out_shape=jax.ShapeDtypeStruct((t_all, d), BF16),
        compiler_params=_cparams(1),
    )(x_all, mods)


def _matmul_kernel(x_ref, w_ref, o_ref):
    o_ref[...] = _nn(x_ref[...], w_ref[...])


def _inproj(xm, w_bf):
    t_all, d = xm.shape
    n = w_bf.shape[1]
    tm = MM_ROW_TILE if t_all % MM_ROW_TILE == 0 else ROW_TILE
    tn = min(n, MM_COL_TILE)
    return pl.pallas_call(
        _matmul_kernel,
        grid=(t_all // tm, n // tn),
        in_specs=[pl.BlockSpec((tm, d), lambda i, j: (i, 0)),
                  pl.BlockSpec((d, tn), lambda i, j: (0, j))],
        out_specs=pl.BlockSpec((tm, tn), lambda i, j: (i, j)),
        out_shape=jax.ShapeDtypeStruct((t_all, n), F32),
        compiler_params=_cparams(2),
    )(xm, w_bf)


def _ln_rows(z, g, b):
    mu = jnp.mean(z, axis=-1, keepdims=True)
    zc = z - mu
    return zc * lax.rsqrt(jnp.mean(zc * zc, axis=-1, keepdims=True) + LN_EPS) * g + b


def _outproj_kernel(*refs, glu, n_row_tiles):
    hf_ref, hb_ref, gate_ref = refs[:3]
    refs = refs[3:]
    if glu:
        (y5_ref, u5_ref, ds_ref, gw_ref, gb_ref, w1_ref, w2_ref, x_ref, m_ref, g_ref, b_ref,
         x1_ref, h2_ref) = refs
        y = _gelu_tanh(y5_ref[...] + ds_ref[...] * u5_ref[...])
        a2 = y * _sigmoid(_nn(y.astype(BF16), gw_ref[...]) + gb_ref[...])
    else:
        a2_ref, w1_ref, w2_ref, x_ref, m_ref, g_ref, b_ref, x1_ref, h2_ref = refs
        a2 = a2_ref[...]
    heads = []
    for h in range(N_HEADS):
        hs = slice(h * HEAD_DIM, (h + 1) * HEAD_DIM)
        gx = gate_ref[:, hs]
        gate = gx * _sigmoid(gx) if glu else _sigmoid(gx)
        heads.append((_head_norm(hf_ref[:, hs] + hb_ref[:, hs]) * gate).astype(BF16))
    a1 = jnp.concatenate(heads, axis=1)
    y = _nn(a1, w1_ref[...]) + _nn(a2.astype(BF16), w2_ref[...])
    x1 = _ln_rows(ALPHA * x_ref[...] + m_ref[2:3, :] * y, g_ref[...], b_ref[...])
    x1_ref[...] = x1
    h2t = (x1 * (1.0 + m_ref[4:5, :]) + m_ref[3:4, :]).T
    h2_ref[...] = jnp.where(pl.program_id(0) < n_row_tiles, h2t, 0.0).astype(BF16)


def _outproj(scan_parts, a2_parts, w_out_bf, x_all, mods, ln_g, ln_b, n_lat_tiles, glu_params=None):
    t_all, d = x_all.shape
    half = HALF
    n_row_tiles = t_all // ROW_TILE
    t_pad = _padded_tokens(t_all)
    row = lambda i: (jnp.minimum(i, n_row_tiles - 1), 0)
    const = lambda i: (0, 0)
    a2_parts = list(scan_parts) + list(a2_parts)
    a2_specs = [pl.BlockSpec((ROW_TILE, half), lambda i, c=c: (jnp.minimum(i, n_row_tiles - 1), c))
                for _, c in a2_parts]
    a2_args = [a for a, _ in a2_parts]
    if glu_params is not None:
        d_skip, glu_w_bf, glu_b = glu_params
        a2_specs += [pl.BlockSpec((1, half), const), pl.BlockSpec((half, half), const),
                     pl.BlockSpec((1, half), const)]
        a2_args += [d_skip.reshape(1, half), glu_w_bf, glu_b.reshape(1, half)]
    return pl.pallas_call(
        functools.partial(_outproj_kernel, glu=glu_params is not None, n_row_tiles=n_row_tiles),
        grid=(t_pad // ROW_TILE,),
        in_specs=a2_specs + [
            pl.BlockSpec((half, d), const), pl.BlockSpec((half, d), const),
            pl.BlockSpec((ROW_TILE, d), row),
            pl.BlockSpec((None, 8, d), lambda i: (jnp.where(i < n_lat_tiles, 0, 1), 0, 0)),
            pl.BlockSpec((1, d), const), pl.BlockSpec((1, d), const)],
        out_specs=[pl.BlockSpec((ROW_TILE, d), row), pl.BlockSpec((d, ROW_TILE), lambda i: (0, i))],
        out_shape=[jax.ShapeDtypeStruct((t_all, d), F32), jax.ShapeDtypeStruct((d, t_pad), BF16)],
        compiler_params=_cparams(1),
    )(*a2_args, w_out_bf[:half], w_out_bf[half:], x_all, mods, ln_g.reshape(1, d), ln_b.reshape(1, d))


def _ln2_kernel(x_ref, p_ref, m_ref, g_ref, b_ref, *rest):
    x2 = _ln_rows(ALPHA * x_ref[...] + m_ref[5:6, :] * p_ref[...].T, g_ref[...], b_ref[...])
    if len(rest) == 1:
        rest[0][...] = x2
    else:
        mn_ref, o_ref, xm_ref = rest
        o_ref[...] = x2
        xm_ref[...] = _modulated(x2, mn_ref)


def _ln2(x1, peer_out_t, mods, ln_g, ln_b, n_lat_tiles, mods_next=None):
    t_all, d = x1.shape
    row = pl.BlockSpec((ROW_TILE, d), lambda i: (i, 0))
    col = pl.BlockSpec((d, ROW_TILE), lambda i: (0, i))
    vec = pl.BlockSpec((1, d), lambda i: (0, 0))
    nxt = mods_next is not None
    n_rows = t_all if nxt else n_lat_tiles * ROW_TILE
    return pl.pallas_call(
        _ln2_kernel,
        grid=(n_rows // ROW_TILE,),
        in_specs=[row, col, _mods_spec(d, n_lat_tiles), vec, vec] + ([_mods_spec(d, n_lat_tiles)] if nxt else []),
        out_specs=[row, row] if nxt else row,
        out_shape=([jax.ShapeDtypeStruct((t_all, d), F32), jax.ShapeDtypeStruct((t_all, d), BF16)] if nxt
                   else jax.ShapeDtypeStruct((n_rows, d), F32)),
        compiler_params=_cparams(1),
    )(x1, peer_out_t, mods, ln_g.reshape(1, d), ln_b.reshape(1, d), *([mods_next] if nxt else []))


def _scan_block(rev, n_chunks, n_lat_chunks):
    if rev:
        return lambda s: n_chunks - 1 - s
    n_ctx = n_chunks - n_lat_chunks
    return lambda s: jnp.where(s < n_ctx, s + n_lat_chunks, s - n_ctx)


def _scan_mask(rev, n):
    row = lax.broadcasted_iota(jnp.int32, (n, n), 0)
    col = lax.broadcasted_iota(jnp.int32, (n, n), 1)
    return (col >= row) if rev else (col <= row)


def _rope(x, cos, sin):
    lane = lax.broadcasted_iota(jnp.int32, x.shape, 1)
    partner = jnp.where((lane % 64) < 32, pltpu.roll(x, 96, 1), pltpu.roll(x, 32, 1))
    return x * cos + partner * sin


def _mlstm_kernel(*refs):
    n_in = 5
    fw_in, bw_in = refs[:n_in], refs[n_in:2 * n_in]
    bias_ref, biast_ref, out_f, out_b = refs[2 * n_in:2 * n_in + 4]
    state = refs[2 * n_in + 4:]

    @pl.when(pl.program_id(0) == 0)
    def _():
        for r in state:
            r[...] = jnp.zeros_like(r)

    _mlstm_chunk(*fw_in, bias_ref, biast_ref, out_f, *state[:2], rev=False)
    _mlstm_chunk(*bw_in, bias_ref, biast_ref, out_b, *state[2:], rev=True)


def _mlstm_chunk(q_ref, k_ref, v_ref, g_ref, gt_ref, bias_ref, biast_ref, out_ref, cn_ref, m_ref, *, rev):
    n = q_ref.shape[0]
    mask = _scan_mask(rev, n)
    tri = mask.astype(BF16)
    pre_c = g_ref[...] + bias_ref[...]
    pre_r = gt_ref[...] + biast_ref[...]
    b_c = sum(_nn(tri, p) for p in _split3(_log_sigmoid(pre_c)))
    b_r = sum(_nt(p, tri) for p in _split3(_log_sigmoid(pre_r)))
    ones = jnp.ones((n, HEAD_DIM), BF16)
    last = 0 if rev else n - 1
    off = N_HEADS if rev else 0

    for h in range(N_HEADS):
        hs = slice(h * HEAD_DIM, (h + 1) * HEAD_DIM)
        ci, cf = off + h, 2 * N_HEADS + off + h
        bc, br = b_c[:, cf:cf + 1], b_r[cf:cf + 1, :]
        ic, ir = pre_c[:, ci:ci + 1], pre_r[ci:ci + 1, :]
        m_prev = m_ref[h:h + 1, 0:1]
        logw = jnp.where(mask, bc - br + ir, NEG)
        inter = bc + m_prev
        m_t = jnp.maximum(inter, jnp.max(logw, axis=1, keepdims=True))
        w_inter = jnp.exp(inter - m_t)
        qb, kb = q_ref[:, hs], k_ref[:, hs]
        v_ext = jnp.concatenate([v_ref[:, hs].astype(BF16), ones], axis=1)
        s = _nt(qb, kb) * jnp.exp(logw - m_t)
        cn_old = cn_ref[h]
        both = _nn(s.astype(BF16), v_ext) + _nn(qb, cn_old.astype(BF16)) * w_inter
        den = jnp.maximum(jnp.abs(both[:, HEAD_DIM:]), jnp.exp(-m_t))
        out_ref[:, hs] = both[:, :HEAD_DIM] / den
        m_new = m_t[last:last + 1, :]
        b_last = bc[last:last + 1, :]
        w_old = jnp.exp(b_last + m_prev - m_new)
        kw = kb.astype(F32) * jnp.exp(b_last - bc + ic - m_new)
        cn_ref[h] = w_old * cn_old + _tn(kw.astype(BF16), v_ext)
        m_ref[h:h + 1, :] = jnp.broadcast_to(m_new, (1, HEAD_DIM))


def _rope_kernel(q_ref, k_ref, cos_ref, sin_ref, qo_ref, ko_ref):
    cos, sin = cos_ref[...], sin_ref[...]
    for h in range(N_HEADS):
        hs = slice(h * HEAD_DIM, (h + 1) * HEAD_DIM)
        qo_ref[:, hs] = (_rope(q_ref[:, hs], cos, sin) * QK_SCALE).astype(BF16)
        ko_ref[:, hs] = _rope(k_ref[:, hs], cos, sin).astype(BF16)


def _rope_qk(proj, cos_t, sin_t):
    t_all = proj.shape[0]
    tok = pl.BlockSpec((ROW_TILE, 128), lambda i: (i, 0))
    return pl.pallas_call(
        _rope_kernel,
        grid=(t_all // ROW_TILE,),
        in_specs=[pl.BlockSpec((ROW_TILE, HALF), lambda i: (i, 0)),
                  pl.BlockSpec((ROW_TILE, HALF), lambda i: (i, 1)), tok, tok],
        out_specs=[pl.BlockSpec((ROW_TILE, HALF), lambda i: (i, 0))] * 2,
        out_shape=[jax.ShapeDtypeStruct((t_all, HALF), BF16)] * 2,
        compiler_params=_cparams(1),
    )(proj, proj, cos_t, sin_t)


def _mlstm(proj, gates, gates_t, cos_t, sin_t, igate_b, fgate_b, n_lat_chunks):
    t_all = proj.shape[0]
    n = SCAN_CHUNK
    n_chunks = t_all // n
    bias = jnp.zeros((1, 128), F32).at[0, :4 * N_HEADS].set(
        jnp.concatenate([igate_b.reshape(-1), fgate_b.reshape(-1)]))
    q_rot, k_rot = _rope_qk(proj, cos_t, sin_t)

    def dir_specs(rev):
        blk = _scan_block(rev, n_chunks, n_lat_chunks)
        colblk = lambda c: pl.BlockSpec((n, HALF), lambda s, c=c: (blk(s), c))
        ins = [colblk(0), colblk(0), colblk(2), pl.BlockSpec((n, 128), lambda s: (blk(s), 0)),
               pl.BlockSpec((128, n), lambda s: (0, blk(s)))]
        return ins, pl.BlockSpec((n, HALF), lambda s: (blk(s), 0))

    (in_f, out_f), (in_b, out_b) = dir_specs(False), dir_specs(True)
    dir_args = [q_rot, k_rot, proj, gates, gates_t]
    state = [pltpu.VMEM((N_HEADS, HEAD_DIM, 2 * HEAD_DIM), F32), pltpu.VMEM((N_HEADS, HEAD_DIM), F32)]
    return pl.pallas_call(
        _mlstm_kernel,
        grid=(n_chunks,),
        in_specs=in_f + in_b + [pl.BlockSpec((1, 128), lambda s: (0, 0)), pl.BlockSpec((128, 1), lambda s: (0, 0))],
        out_specs=[out_f, out_b],
        out_shape=[jax.ShapeDtypeStruct((t_all, HALF), F32)] * 2,
        scratch_shapes=state + state,
        compiler_params=_cparams(1),
    )(*dir_args, *dir_args, bias, bias.reshape(128, 1))


def _na_kernel(q_ref, kp_ref, kc_ref, kn_ref, vp_ref, vc_ref, vn_ref, kx_ref, vx_ref, bias_ref, o_ref):
    qb = (q_ref[...] * QK_SCALE).astype(BF16)
    kk = jnp.concatenate([kp_ref[...], kc_ref[...], kn_ref[...]], axis=0).astype(BF16)
    vv = jnp.concatenate([vp_ref[...], vc_ref[...], vn_ref[...]], axis=0).astype(BF16)
    s_loc = _nt(qb, kk) + bias_ref[...]
    s_ctx = _nt(qb, kx_ref[...].astype(BF16))
    m = jnp.maximum(jnp.max(s_loc, axis=1, keepdims=True), jnp.max(s_ctx, axis=1, keepdims=True))
    p_loc = jnp.exp(s_loc - m)
    p_ctx = jnp.exp(s_ctx - m)
    l = jnp.sum(p_loc, axis=1, keepdims=True) + jnp.sum(p_ctx, axis=1, keepdims=True)
    o_ref[...] = (_nn(p_loc.astype(BF16), vv) + _nn(p_ctx.astype(BF16), vx_ref[...].astype(BF16))) / l


def _na_bias(rpb, rows):
    qn = NA_QROWS * GRID_W
    kn = 2 * qn
    ql = np.arange(qn)
    kl = np.arange(kn)
    out = []
    nqb = rows // NA_QROWS
    lr, krl = np.arange(NA_QROWS)[:, None], np.arange(2 * NA_QROWS)[None, :]
    dr = np.clip(krl - NA_QROWS // 2 - lr + NA_WIN_ROWS - 1, 0, 2 * NA_WIN_ROWS - 2)
    col = np.arange(GRID_W)
    dc = np.clip(col[None, :] - col[:, None] + NA_WIN_COLS - 1, 0, 2 * NA_WIN_COLS - 2)
    oh_r = jnp.asarray(dr[..., None] == np.arange(2 * NA_WIN_ROWS - 1), F32)
    oh_c = jnp.asarray(dc[..., None] == np.arange(2 * NA_WIN_COLS - 1), F32)
    table = jnp.einsum('hab,lka,qcb->hlqkc', rpb, oh_r, oh_c,
                       precision=lax.Precision.HIGHEST).reshape(rpb.shape[0], qn, kn)
    for qb in (0, 1, nqb - 1):
        qr = NA_QROWS * qb + ql // GRID_W
        qc = ql % GRID_W
        kr = NA_QROWS * qb - NA_QROWS // 2 + kl // GRID_W
        kc = kl % GRID_W
        r0 = np.clip(qr - NA_WIN_ROWS // 2, 0, rows - NA_WIN_ROWS)
        c0 = np.clip(qc - NA_WIN_COLS // 2, 0, GRID_W - NA_WIN_COLS)
        ok = ((kr[None, :] >= r0[:, None]) & (kr[None, :] < r0[:, None] + NA_WIN_ROWS)
              & (kc[None, :] >= c0[:, None]) & (kc[None, :] < c0[:, None] + NA_WIN_COLS)
              & (kr[None, :] >= 0) & (kr[None, :] < rows))
        out.append(jnp.where(jnp.asarray(ok)[None], table, NEG))
    return jnp.stack(out)


def _na(proj, rpb, t_lat, t_ctx):
    rows = t_lat // GRID_W
    qn = NA_QROWS * GRID_W
    hn = qn // 2
    nqb = rows // NA_QROWS
    assert nqb >= 2 and t_ctx == hn
    bias = _na_bias(rpb.astype(F32), rows)
    last_half = 2 * nqb - 1
    ctx_blk = t_lat // hn
    cls = lambda b: jnp.where(b == 0, 0, jnp.where(b == nqb - 1, 2, 1))

    def kv_specs(col0):
        return [pl.BlockSpec((hn, HEAD_DIM), lambda h, b: (jnp.maximum(2 * b - 1, 0), col0 + h)),
                pl.BlockSpec((qn, HEAD_DIM), lambda h, b: (b, col0 + h)),
                pl.BlockSpec((hn, HEAD_DIM), lambda h, b: (jnp.minimum(2 * b + 2, last_half), col0 + h))]

    kcol, vcol = 5 * N_HEADS, 6 * N_HEADS
    return pl.pallas_call(
        _na_kernel,
        grid=(N_HEADS, nqb),
        in_specs=[pl.BlockSpec((qn, HEAD_DIM), lambda h, b: (b, 4 * N_HEADS + h))]
        + kv_specs(kcol) + kv_specs(vcol)
        + [pl.BlockSpec((hn, HEAD_DIM), lambda h, b: (ctx_blk, kcol + h)),
           pl.BlockSpec((hn, HEAD_DIM), lambda h, b: (ctx_blk, vcol + h)),
           pl.BlockSpec((None, None, qn, 2 * qn), lambda h, b: (cls(b), h, 0, 0))],
        out_specs=pl.BlockSpec((qn, HEAD_DIM), lambda h, b: (b, h)),
        out_shape=jax.ShapeDtypeStruct((t_lat, HALF), F32),
        compiler_params=_cparams(2),
    )(proj, proj, proj, proj, proj, proj, proj, proj, proj, bias)


def _ctx_attn_kernel(q_ref, k_ref, v_ref, o_ref):
    qb = (q_ref[...] * QK_SCALE).astype(BF16)
    s = _nt(qb, k_ref[...].astype(BF16))
    p = jnp.exp(s - jnp.max(s, axis=1, keepdims=True))
    o_ref[...] = _nn(p.astype(BF16), v_ref[...].astype(BF16)) / jnp.sum(p, axis=1, keepdims=True)


def _ctx_attn(proj, t_lat, t_ctx):
    blk = t_lat // t_ctx
    spec = lambda c: pl.BlockSpec((t_ctx, HEAD_DIM), lambda h, c=c: (blk, c * N_HEADS + h))
    return pl.pallas_call(
        _ctx_attn_kernel,
        grid=(N_HEADS,),
        in_specs=[spec(4), spec(5), spec(6)],
        out_specs=pl.BlockSpec((t_ctx, HEAD_DIM), lambda h: (0, h)),
        out_shape=jax.ShapeDtypeStruct((t_ctx, HALF), F32),
        compiler_params=_cparams(1),
    )(proj, proj, proj)


def _gla_inter(q_ref, i_ref, f_ref, lb_ref, out_ref, st_ref, k_scr, b_scr, *, rev):
    n = q_ref.shape[0]
    mask = _scan_mask(rev, n)
    tri = mask.astype(BF16)
    last = 0 if rev else n - 1
    x = f_ref[...]
    key = lb_ref[0:1, :] * _sigmoid(-x)
    la, lc = lb_ref[1:2, :], lb_ref[2:3, :] + _log_sigmoid(x)
    logf = jnp.maximum(la, lc) + jnp.log1p(jnp.exp(-jnp.abs(la - lc)))
    b = sum(_nn(tri, p) for p in _split3(logf))
    b_end = b[last:last + 1, :]
    qx = q_ref[...]
    q = qx * _sigmoid(qx)
    qt = q * jnp.exp(b)
    kdec = key * jnp.exp(b_end - b)
    w_end = jnp.exp(b_end)
    k_scr[...] = key
    b_scr[...] = b
    for h in range(N_HEADS):
        hs = slice(h * HEAD_DIM, (h + 1) * HEAD_DIM)
        st = st_ref[h]
        vb = i_ref[:, hs].astype(BF16)
        out_ref[:, hs] = _nt(qt[:, hs].astype(BF16), st.astype(BF16))
        st_ref[h] = st * w_end[:, hs] + _tn(vb, kdec[:, hs].astype(BF16))
    return mask, q, b_end


def _gla_intra_factored(mask, q, b_end, i_ref, out_ref, k_scr, b_scr):
    mid = 0.5 * b_end
    qm = q * jnp.exp(b_scr[...] - mid)
    km = k_scr[...] * jnp.exp(mid - b_scr[...])
    for h in range(N_HEADS):
        hs = slice(h * HEAD_DIM, (h + 1) * HEAD_DIM)
        att = jnp.where(mask, _nt(qm[:, hs].astype(BF16), km[:, hs].astype(BF16)), 0.0)
        out_ref[:, hs] += _nn(att.astype(BF16), i_ref[:, hs].astype(BF16))


def _gla_intra_exact(q, i_ref, out_ref, k_scr, b_scr, *, rev):
    n = q.shape[0]
    t_idx = lax.broadcasted_iota(jnp.int32, (n, 1), 0)
    for h in range(N_HEADS):
        hs = slice(h * HEAD_DIM, (h + 1) * HEAD_DIM)
        q_h = q[:, hs]
        b_h = b_scr[:, hs]

        def body(grp, acc):
            base = pl.multiple_of(grp * 8, 8)
            b8, k8, v8 = b_scr[pl.ds(base, 8), hs], k_scr[pl.ds(base, 8), hs], i_ref[pl.ds(base, 8), hs]
            for r in range(8):
                s = base + r
                e = jnp.exp(jnp.minimum(b_h - b8[r:r + 1, :], 0.0))
                a = jnp.sum(q_h * k8[r:r + 1, :] * e, axis=1, keepdims=True)
                a = jnp.where((t_idx <= s) if rev else (t_idx >= s), a, 0.0)
                acc = acc + a * v8[r:r + 1, :]
            return acc

        out_ref[:, hs] += lax.fori_loop(0, n // 8, body, jnp.zeros((n, HEAD_DIM), F32))


def _gla_kernel(*refs):
    fw_in, bw_in, lb_ref = refs[0:3], refs[3:6], refs[6]
    out_f, out_b = refs[7:9]
    st_f, k_f, b_f, st_b, k_b, b_b = refs[9:]

    @pl.when(pl.program_id(0) == 0)
    def _():
        st_f[...] = jnp.zeros_like(st_f)
        st_b[...] = jnp.zeros_like(st_b)

    mask_f, q_f, end_f = _gla_inter(*fw_in, lb_ref, out_f, st_f, k_f, b_f, rev=False)
    mask_b, q_b, end_b = _gla_inter(*bw_in, lb_ref, out_b, st_b, k_b, b_b, rev=True)
    safe = jnp.minimum(jnp.min(end_f), jnp.min(end_b)) >= 2.0 * GLA_SAFE_LOG

    @pl.when(safe)
    def _():
        _gla_intra_factored(mask_f, q_f, end_f, fw_in[1], out_f, k_f, b_f)
        _gla_intra_factored(mask_b, q_b, end_b, bw_in[1], out_b, k_b, b_b)

    @pl.when(jnp.logical_not(safe))
    def _():
        _gla_intra_exact(q_f, fw_in[1], out_f, k_f, b_f, rev=False)
        _gla_intra_exact(q_b, bw_in[1], out_b, k_b, b_b, rev=True)


def _gla(proj, lb, n_lat_chunks):
    t_all = proj.shape[0]
    n = GLA_CHUNK
    n_chunks = t_all // n

    def dir_specs(rev):
        blk = _scan_block(rev, n_chunks, n_lat_chunks)
        colblk = lambda c: pl.BlockSpec((n, HALF), lambda s, c=c: (blk(s), c))
        return [colblk(0), colblk(1), colblk(3 if rev else 2)], pl.BlockSpec((n, HALF), lambda s: (blk(s), 0))

    (in_f, out_f), (in_b, out_b) = dir_specs(False), dir_specs(True)
    state = [pltpu.VMEM((N_HEADS, HEAD_DIM, HEAD_DIM), F32), pltpu.VMEM((n, HALF), F32),
             pltpu.VMEM((n, HALF), F32)]
    lb = lb.reshape(1, HALF)
    lb_rows = jnp.concatenate([1.0 - lb, jnp.log(lb), jnp.log1p(-lb)], axis=0)
    return pl.pallas_call(
        _gla_kernel,
        grid=(n_chunks,),
        in_specs=in_f + in_b + [pl.BlockSpec((3, HALF), lambda s: (0, 0))],
        out_specs=[out_f, out_b],
        out_shape=[jax.ShapeDtypeStruct((t_all, HALF), F32)] * 2,
        scratch_shapes=state + state,
        compiler_params=_cparams(1),
    )(*([proj] * 6), lb_rows)


def _s5_mats(a_re, a_im, log_dt, b_re, b_im, c_re, c_im):
    hp = lax.Precision.HIGHEST
    n = S5_CHUNK
    dt = jnp.exp(log_dt)[..., None]
    la_re, la_im = a_re * dt, a_im * dt
    mag = jnp.exp(la_re)
    ab_re, ab_im = mag * jnp.cos(la_im), mag * jnp.sin(la_im)
    nr, ni = ab_re - 1.0, ab_im
    den = jnp.square(a_re) + jnp.square(a_im)
    cr = (nr * a_re + ni * a_im) / den
    ci = (ni * a_re - nr * a_im) / den
    bb_re = cr[..., None] * b_re - ci[..., None] * b_im
    bb_im = cr[..., None] * b_im + ci[..., None] * b_re

    def apow(tau):
        tau = jnp.asarray(tau, F32)[None, None, :, None]
        m = jnp.exp(tau * la_re[:, :, None, :])
        return m * jnp.cos(tau * la_im[:, :, None, :]), m * jnp.sin(tau * la_im[:, :, None, :])

    def c_apow(tau):
        pr, pi = apow(tau)
        return (c_re[:, :, None] * pr[:, :, :, None] - c_im[:, :, None] * pi[:, :, :, None],
                c_re[:, :, None] * pi[:, :, :, None] + c_im[:, :, None] * pr[:, :, :, None])

    ca_re, ca_im = c_apow(np.arange(n))
    kern = jnp.einsum('dgtip,dgpj->dgtij', jnp.concatenate([ca_re, -ca_im], axis=-1),
                      jnp.concatenate([bb_re, bb_im], axis=2), precision=hp)
    s_idx, t_idx = np.arange(n)[:, None], np.arange(n)[None, :]
    g = a_re.shape[1]
    lp = n * S5_GROUP

    def toeplitz(kd, lag):
        onehot = jnp.asarray(lag[None] == np.arange(n)[:, None, None], F32)
        return jnp.einsum('lst,glij->gsjti', onehot, kd, precision=hp).reshape(g, lp, lp)

    kmat = toeplitz(kern[0], t_idx - s_idx) + toeplitz(kern[1], s_idx - t_idx)

    def emat(d, tau):
        pr, pi = apow(tau)
        pr, pi = pr[d][:, :, None, :], pi[d][:, :, None, :]
        br, bi = bb_re[d].transpose(0, 2, 1)[:, None], bb_im[d].transpose(0, 2, 1)[:, None]
        return jnp.concatenate([pr * br - pi * bi, pr * bi + pi * br], axis=-1).reshape(g, lp, 2 * S5_STATE)

    def fmat(d, tau):
        fr, fi = c_apow(tau)
        fr, fi = fr[d], fi[d]
        return jnp.concatenate([fr, -fi], axis=-1).transpose(0, 3, 1, 2).reshape(g, 2 * S5_STATE, lp)

    e_all = jnp.concatenate([emat(0, n - 1 - np.arange(n)), emat(1, np.arange(n))], axis=-1)
    f_all = jnp.concatenate([fmat(0, np.arange(n) + 1), fmat(1, n - np.arange(n))], axis=1)
    pr, pi = apow(np.array([n]))
    pr, pi = pr[:, :, 0], pi[:, :, 0]
    ar2 = jnp.concatenate([pr[0], pr[0], pr[1], pr[1]], axis=-1)
    ai2 = jnp.concatenate([-pi[0], pi[0], -pi[1], pi[1]], axis=-1)
    return kmat.astype(BF16), e_all.astype(BF16), f_all.astype(BF16), ar2, ai2


def _s5_state_kernel(u_ref, e_ref, s_ref):
    s_ref[...] = _nn(u_ref[...], e_ref[...])


def _s5_scan_kernel(sf_ref, sb_ref, ar_ref, ai_ref, hf_ref, hb_ref, stf_ref, stb_ref):
    nb = sf_ref.shape[0]
    w = 2 * S5_STATE
    ar, ai = ar_ref[...], ai_ref[...]

    @pl.when(pl.program_id(0) == 0)
    def _():
        stf_ref[...] = jnp.zeros_like(stf_ref)
        stb_ref[...] = jnp.zeros_like(stb_ref)

    def step(h, s, lo):
        return ar[:, lo:lo + w] * h + ai[:, lo:lo + w] * pltpu.roll(h, S5_STATE, 1) + s

    hf, hb = stf_ref[...], stb_ref[...]
    for j in range(nb):
        hf_ref[j] = hf
        hf = step(hf, sf_ref[j], 0)
        hb_ref[nb - 1 - j] = hb
        hb = step(hb, sb_ref[nb - 1 - j], w)
    stf_ref[...] = hf
    stb_ref[...] = hb


def _s5_out_kernel(u_ref, k_ref, h_ref, f_ref, y_ref):
    y_ref[...] = _nn(u_ref[...], k_ref[...]) + _nn(h_ref[...].astype(BF16), f_ref[...])


def _s5(u, mats, n_lat_chunks):
    kmat, emat, fmat, ar2, ai2 = mats
    t_all = u.shape[0]
    n = S5_CHUNK
    nc = t_all // n
    g = S5_GROUPS
    lp = n * S5_GROUP
    w4 = 4 * S5_STATE
    ug = u.astype(BF16).reshape(nc, n, g, S5_GROUP).transpose(2, 0, 1, 3).reshape(g, nc, lp)
    grp = lambda a, b: pl.BlockSpec((None, a, b), lambda i: (i, 0, 0))
    s_end = pl.pallas_call(
        _s5_state_kernel, grid=(g,),
        in_specs=[grp(nc, lp), grp(lp, w4)], out_specs=grp(nc, w4),
        out_shape=jax.ShapeDtypeStruct((g, nc, w4), F32), compiler_params=_cparams(1),
    )(ug, emat)
    nb = S5_SCAN_BLOCK
    w2 = 2 * S5_STATE
    fw_blk = _scan_block(False, nc // nb, n_lat_chunks // nb)
    bw_blk = _scan_block(True, nc // nb, n_lat_chunks // nb)
    s_t = s_end.transpose(1, 0, 2)
    coef = pl.BlockSpec((g, w4), lambda i: (0, 0))
    h_fw, h_bw = pl.pallas_call(
        _s5_scan_kernel, grid=(nc // nb,),
        in_specs=[pl.BlockSpec((nb, g, w2), lambda i: (fw_blk(i), 0, 0)),
                  pl.BlockSpec((nb, g, w2), lambda i: (bw_blk(i), 0, 1)), coef, coef],
        out_specs=[pl.BlockSpec((nb, g, w2), lambda i: (fw_blk(i), 0, 0)),
                   pl.BlockSpec((nb, g, w2), lambda i: (bw_blk(i), 0, 0))],
        out_shape=[jax.ShapeDtypeStruct((nc, g, w2), F32)] * 2,
        scratch_shapes=[pltpu.VMEM((g, w2), F32)] * 2,
        compiler_params=_cparams(1),
    )(s_t, s_t, ar2, ai2)
    h_in = jnp.concatenate([h_fw, h_bw], axis=-1).transpose(1, 0, 2)
    y = pl.pallas_call(
        _s5_out_kernel, grid=(g,),
        in_specs=[grp(nc, lp), grp(lp, lp), grp(nc, w4), grp(w4, lp)], out_specs=grp(nc, lp),
        out_shape=jax.ShapeDtypeStruct((g, nc, lp), F32), compiler_params=_cparams(1),
    )(ug, kmat, h_in, fmat)
    return y.reshape(g, nc, n, S5_GROUP).transpose(1, 2, 0, 3).reshape(t_all, HALF)


def _top_rows(x, k):
    rows = []
    for _ in range(k):
        mx = jnp.max(x, axis=0, keepdims=True)
        rows.append(mx)
        x = jnp.where(x == mx, NEG, x)
    return rows


def _peer_route_kernel(ht_ref, wq_ref, keys_ref, e1_ref, e0_ref, th_ref, top_scr):
    kk = PEER_TOPK
    qt = _nn(wq_ref[...], ht_ref[...])
    for hp in range(2 * PEER_HEADS):
        sc = _nn(keys_ref[hp], qt[hp * HEAD_DIM:(hp + 1) * HEAD_DIM, :].astype(BF16))
        h, p = divmod(hp, 2)
        if p == 0:
            e0_ref[h] = sc
        else:
            e1_ref[h] = sc
        for r, mx in enumerate(_top_rows(sc, kk + 1)):
            top_scr[hp, r:r + 1, :] = mx
        top_scr[hp, kk + 1:, :] = jnp.full((PEER_TOP_ROWS - kk - 1, sc.shape[1]), NEG, F32)

    row = lax.broadcasted_iota(jnp.int32, (PEER_TOP_ROWS, 1), 0)
    for h in range(PEER_HEADS):
        a = top_scr[2 * h]
        b = top_scr[2 * h + 1]
        parts = [a[0:1] + b, a[1:2] + b[0:8], a[2:3] + b[0:8], a[3:4] + b[0:8]]
        parts += [jnp.where(row >= 4, a + b[j:j + 1], NEG) for j in range(3)]
        cand = jnp.maximum(jnp.concatenate(parts, axis=0), NEG)
        top = _top_rows(cand, kk + 1)
        thr = 0.5 * (top[kk - 1] + top[kk])
        z = jnp.sum(jnp.where(cand >= thr, jnp.exp(cand - top[0]), 0.0), axis=0, keepdims=True)
        e0_ref[h] = jnp.exp(e0_ref[h] - a[0:1]) / z
        e1_ref[h] = jnp.exp(e1_ref[h] - b[0:1])
        th_ref[h:h + 1, :] = jnp.exp(thr - top[0]) / z


def _peer_expert_kernel(ht_ref, e1_ref, e0_ref, th_ref, *rest, n_tiles):
    u_refs = rest[:PEER_U_SPLIT]
    vt_ref, o_ref, at0_scr, at1_scr, g0_scr, g1_scr = rest[PEER_U_SPLIT:]
    s = pl.program_id(1)
    assert n_tiles % 2 == 0 and n_tiles >= 4
    ni = u_refs[0].shape[0] // PEER_N_KEYS
    kc = u_refs[0].shape[1]
    at_scr, g_scr = (at0_scr, at1_scr), (g0_scr, g1_scr)

    def stages(par, dot1, gate, dot2, first=False):
        if dot1:
            at_scr[par][...] = _nn(jnp.concatenate([u_k[...] for u_k in u_refs], axis=1), ht_ref[...])
        if gate:
            at_old, g_new = at_scr[1 - par], g_scr[1 - par]
            nj, nc = PEER_GATE_ROWS, 128
            for ii in range(ni):
                i = (s - 1) * ni + ii
                e0_rows = [e0_ref[h, pl.ds(i, 1), :] for h in range(PEER_HEADS)]
                for c in range(at_old.shape[1] // nc):
                    cs = slice(c * nc, (c + 1) * nc)
                    for jb in range(PEER_N_KEYS // nj):
                        js = slice(jb * nj, (jb + 1) * nj)
                        w = None
                        for h in range(PEER_HEADS):
                            p = e1_ref[h, js, cs] * e0_rows[h][:, cs]
                            t = jnp.where(p >= th_ref[h:h + 1, cs], p, 0.0)
                            w = t if w is None else w + t
                        rs = slice(ii * PEER_N_KEYS + jb * nj, ii * PEER_N_KEYS + (jb + 1) * nj)
                        g_new[rs, cs] = (_gelu_tanh(at_old[rs, cs]) * w).astype(BF16)
        if dot2:
            prod = _nn(vt_ref[...], g_scr[par][...])
            if first:
                o_ref[...] = prod
            else:
                o_ref[...] += prod

    mid = jnp.logical_and(s > 2, s < n_tiles)
    pl.when(s == 0)(lambda: stages(0, True, False, False))
    pl.when(s == 1)(lambda: stages(1, True, True, False))
    pl.when(s == 2)(lambda: stages(0, True, True, True, first=True))
    pl.when(jnp.logical_and(mid, s % 2 == 0))(lambda: stages(0, True, True, True))
    pl.when(jnp.logical_and(mid, s % 2 == 1))(lambda: stages(1, True, True, True))
    pl.when(s == n_tiles)(lambda: stages(0, False, True, True))
    pl.when(s == n_tiles + 1)(lambda: stages(1, False, False, True))


def _tok3(t):
    return pl.BlockSpec((PEER_HEADS, PEER_N_KEYS, t), lambda i, *_: (0, 0, i))


def _peer_route(h2t, wq_t_bf, keys_bf):
    d, t_all = h2t.shape
    nh, nk = PEER_HEADS, PEER_N_KEYS
    t1 = PEER_P1_TILE
    return pl.pallas_call(
        _peer_route_kernel,
        grid=(t_all // t1,),
        in_specs=[pl.BlockSpec((d, t1), lambda i: (0, i)),
                  pl.BlockSpec(wq_t_bf.shape, lambda i: (0, 0)),
                  pl.BlockSpec(keys_bf.shape, lambda i: (0, 0, 0))],
        out_specs=[_tok3(t1), _tok3(t1), pl.BlockSpec((nh, t1), lambda i: (0, i))],
        out_shape=[jax.ShapeDtypeStruct((nh, nk, t_all), F32)] * 2 + [jax.ShapeDtypeStruct((nh, t_all), F32)],
        scratch_shapes=[pltpu.VMEM((2 * nh, PEER_TOP_ROWS, t1), F32)],
        compiler_params=_cparams(1),
    )(h2t, wq_t_bf, keys_bf)


def _padded_tokens(t_all):
    return t_all + (-t_all % PEER_TOK_TILE)


def _peer(h2t, wq_t_bf, keys_bf, u_r, vt_r, layer=0):
    return _peer_experts(h2t, *_peer_route(h2t, wq_t_bf, keys_bf), u_r, vt_r, layer)


def _peer_tables(u_all, v_all):
    return u_all.astype(BF16), v_all.transpose(0, 2, 1).astype(BF16)


def _peer_experts(h2t, e1, e0, th, u_r, vt_r, layer):
    d, t_all = h2t.shape
    tt = PEER_TOK_TILE
    et = PEER_EXP_TILE
    n_tiles = u_r.shape[1] // et
    kc = d // PEER_U_SPLIT
    tok = pl.BlockSpec((d, tt), lambda i, s: (0, i))
    return pl.pallas_call(
        functools.partial(_peer_expert_kernel, n_tiles=n_tiles),
        grid=(t_all // tt, n_tiles + 2),
        in_specs=[tok, _tok3(tt), _tok3(tt), pl.BlockSpec((PEER_HEADS, tt), lambda i, s: (0, i))]
        + [pl.BlockSpec((None, et, kc), lambda i, s, k=k: (layer, jnp.minimum(s, n_tiles - 1), k))
           for k in range(PEER_U_SPLIT)]
        + [pl.BlockSpec((None, d, et), lambda i, s: (layer, 0, jnp.clip(s - 2, 0, n_tiles - 1)))],
        out_specs=tok,
        out_shape=jax.ShapeDtypeStruct((d, t_all), F32),
        scratch_shapes=[pltpu.VMEM((et, tt), F32)] * 2 + [pltpu.VMEM((et, tt), BF16)] * 2,
        compiler_params=_cparams(2),
    )(h2t, e1, e0, th, *([u_r] * PEER_U_SPLIT), vt_r)


def _rope_tables(t_lat, t_ctx):
    pos = np.arange(t_lat)
    nfreq = HEAD_DIM // 4
    freqs = ROPE_BASE ** (-jnp.arange(nfreq, dtype=F32) / nfreq)
    ang_r = jnp.asarray(pos // GRID_W, F32)[:, None] * freqs
    ang_c = jnp.asarray(pos % GRID_W, F32)[:, None] * freqs
    cos = jnp.concatenate([jnp.cos(ang_r), jnp.cos(ang_r), jnp.cos(ang_c), jnp.cos(ang_c)], axis=1)
    sin = jnp.concatenate([-jnp.sin(ang_r), jnp.sin(ang_r), -jnp.sin(ang_c), jnp.sin(ang_c)], axis=1)
    cos = jnp.concatenate([cos, jnp.ones((t_ctx, HEAD_DIM), F32)], axis=0)
    sin = jnp.concatenate([sin, jnp.zeros((t_ctx, HEAD_DIM), F32)], axis=0)
    return cos, sin


def _peer_and_norm(x1, h2t, mods, ln_g, ln_b, n_lat_tiles, w_q, sub_keys, tables, layer, mods_next=None):
    keys = sub_keys.reshape(2 * PEER_HEADS, PEER_N_KEYS, -1).astype(BF16)
    pt = _peer(h2t, w_q.T.astype(BF16), keys, *tables, layer)
    return _ln2(x1, pt, mods, ln_g, ln_b, n_lat_tiles, mods_next)


def _layer_ab(x_all, mods, t_lat, t_ctx, w_in, w_out, igate_b, fgate_b, rpb, ln_g, ln_b, xm=None):
    n_lat_tiles = t_lat // ROW_TILE
    if xm is None:
        xm = _modulate(x_all, mods, n_lat_tiles)
    ng = 4 * N_HEADS
    w_main = jnp.concatenate([w_in[:, :4 * HALF], w_in[:, 4 * HALF + ng:]], axis=1).astype(BF16)
    w_gate = jnp.pad(w_in[:, 4 * HALF:4 * HALF + ng], ((0, 0), (0, 128 - ng))).astype(BF16)
    proj = _inproj(xm, w_main)
    gates = _inproj(xm, w_gate)
    cos_t, sin_t = _rope_tables(t_lat, t_ctx)
    h_fw, h_bw = _mlstm(proj, gates, gates.T, cos_t, sin_t, igate_b, fgate_b, t_lat // SCAN_CHUNK)
    y_na = jnp.concatenate([_na(proj, rpb, t_lat, t_ctx), _ctx_attn(proj, t_lat, t_ctx)], axis=0)
    return _outproj([(h_fw, 0), (h_bw, 0), (proj, 3)], [(y_na, 0)], w_out.astype(BF16), x_all, mods,
                    ln_g, ln_b, n_lat_tiles)


def _layer_cd(x_all, mods, t_lat, w_in, w_out, lb, s5p, s5_d, glu_w, glu_b, ln_g, ln_b, xm=None):
    n_lat_tiles = t_lat // ROW_TILE
    if xm is None:
        xm = _modulate(x_all, mods, n_lat_tiles)
    proj = _inproj(xm, w_in.astype(BF16))
    o_fw, o_bw = _gla(proj, lb, t_lat // GLA_CHUNK)
    y5 = _s5(proj[:, 5 * HALF:], _s5_mats(*[p.astype(F32) for p in s5p]), t_lat // S5_CHUNK)
    return _outproj([(o_fw, 0), (o_bw, 0), (proj, 4)], [(y5, 0), (proj, 5)], w_out.astype(BF16), x_all, mods,
                    ln_g, ln_b, n_lat_tiles,
                    glu_params=(s5_d.astype(F32), glu_w.astype(BF16), glu_b.astype(F32)))


def kernel(x, c, ctx, c_ctx, ada_w, ada_b, ln_g, ln_b, ab_w_in, ab_w_out, mlstm_igate_b, mlstm_fgate_b, na_rpb, cd_w_in, cd_w_out, hgrn_lb_logits, s5_a_re, s5_a_im, s5_log_dt, s5_b_re, s5_b_im, s5_c_re, s5_c_im, s5_d, s5_glu_w, s5_glu_b, peer_w_q, peer_sub_keys, peer_u, peer_v):
    t_lat, t_ctx = x.shape[1], ctx.shape[1]
    assert x.shape[0] == 1 and t_ctx == ROW_TILE and t_lat % (NA_QROWS * GRID_W) == 0
    depth = ada_w.shape[0]
    lb_soft = jax.nn.softmax(hgrn_lb_logits.astype(F32), axis=0)
    lower_bounds = jnp.cumsum(lb_soft, axis=0) - lb_soft[0]
    x_all = jnp.concatenate([x[0], ctx[0]], axis=0).astype(F32)
    n_lat_tiles = t_lat // ROW_TILE
    mods_all = [_adaln(c, c_ctx, ada_w, ada_b, l) for l in range(depth)]
    xm = _modulate(x_all, mods_all[0], n_lat_tiles)
    tables = _peer_tables(peer_u, peer_v)
    for l in range(depth):
        j = l // 2
        mods = mods_all[l]
        if l % 2 == 0:
            x1, h2t = _layer_ab(x_all, mods, t_lat, t_ctx, ab_w_in[j], ab_w_out[j], mlstm_igate_b[j],
                                mlstm_fgate_b[j], na_rpb[j], ln_g[l, 0], ln_b[l, 0], xm=xm)
        else:
            s5p = (s5_a_re[j], s5_a_im[j], s5_log_dt[j], s5_b_re[j], s5_b_im[j], s5_c_re[j], s5_c_im[j])
            x1, h2t = _layer_cd(x_all, mods, t_lat, cd_w_in[j], cd_w_out[j], lower_bounds[l], s5p, s5_d[j],
                                s5_glu_w[j], s5_glu_b[j], ln_g[l, 0], ln_b[l, 0], xm=xm)
        peer_args = (x1, h2t, mods, ln_g[l, 1], ln_b[l, 1], n_lat_tiles, peer_w_q[l], peer_sub_keys[l],
                     tables, l)
        if l + 1 < depth:
            x_all, xm = _peer_and_norm(*peer_args, mods_next=mods_all[l + 1])
        else:
            x_all = _peer_and_norm(*peer_args)
    return x_all[None].astype(x.dtype)
```

```python
import functools
import math

import numpy as np
import jax
import jax.numpy as jnp
from jax import lax
from jax.experimental import pallas as pl
from jax.experimental.pallas import tpu as pltpu

F32 = jnp.float32
BF16 = jnp.bfloat16

HEAD_DIM = 128
GRID_W = 64
N_HEADS = 8
HALF = N_HEADS * HEAD_DIM
NA_WIN_ROWS = 8
NA_WIN_COLS = 16
ROPE_BASE = 10000.0
S5_GROUP = 16
S5_STATE = 64
S5_GROUPS = HALF // S5_GROUP
PEER_HEADS = 8
PEER_N_KEYS = 128
PEER_TOPK = 16
PEER_TOP_ROWS = 24
LN_EPS = 1e-5
DEPTH = 2
ALPHA = (2.0 * DEPTH) ** 0.25
QK_SCALE = HEAD_DIM ** -0.5

ROW_TILE = 256
MM_ROW_TILE = 1280
MM_COL_TILE = 1024
SCAN_CHUNK = 128
GLA_CHUNK = 64
S5_CHUNK = 32
S5_SCAN_BLOCK = 8
NA_QROWS = 8
PEER_P1_TILE = 256
PEER_TOK_TILE = 512
PEER_EXP_TILE = 512
PEER_GATE_ROWS = 64
PEER_U_SPLIT = 4
NEG = -1e30
GLA_SAFE_LOG = -80.0
VMEM_LIMIT = 56 * 1024 * 1024


def _cparams(n_axes):
    return pltpu.CompilerParams(dimension_semantics=("arbitrary",) * n_axes,
                                vmem_limit_bytes=VMEM_LIMIT)


def _nt(a, b):
    return lax.dot_general(a, b, (((1,), (1,)), ((), ())), preferred_element_type=F32)


def _tn(a, b):
    return lax.dot_general(a, b, (((0,), (0,)), ((), ())), preferred_element_type=F32)


def _nn(a, b):
    return jnp.dot(a, b, preferred_element_type=F32)


def _split3(x):
    hi = x.astype(BF16)
    r = x - hi.astype(F32)
    mid = r.astype(BF16)
    lo = (r - mid.astype(F32)).astype(BF16)
    return hi, mid, lo


def _log_sigmoid(x):
    return jnp.minimum(x, 0.0) - jnp.log1p(jnp.exp(-jnp.abs(x)))


def _sigmoid(x):
    return 1.0 / (1.0 + jnp.exp(-x))


def _gelu_tanh(x):
    return 0.5 * x * (1.0 + jnp.tanh(math.sqrt(2.0 / math.pi) * (x + 0.044715 * (x * x * x))))


def _head_norm(x):
    mu = jnp.mean(x, axis=-1, keepdims=True)
    xc = x - mu
    return xc * lax.rsqrt(jnp.mean(xc * xc, axis=-1, keepdims=True) + LN_EPS)


def _adaln_kernel(ct_ref, w_ref, b_ref, o_ref):
    ct = ct_ref[...]
    s = ct * _sigmoid(ct)
    w = w_ref[...]
    o_ref[0:1, :] = jnp.sum(s[:, 0:1] * w, axis=0, keepdims=True) + b_ref[...]
    o_ref[1:2, :] = jnp.sum(s[:, 1:2] * w, axis=0, keepdims=True) + b_ref[...]


def _adaln(c_lat, c_ctx, w_all, b_all, layer):
    n_layers, d, n = w_all.shape
    tn = n // 8
    ct = jnp.stack([c_lat.reshape(d), c_ctx.reshape(d)], axis=1)
    out = pl.pallas_call(
        _adaln_kernel,
        grid=(n // tn,),
        in_specs=[pl.BlockSpec((d, 2), lambda j: (0, 0)),
                  pl.BlockSpec((None, d, tn), lambda j: (layer, 0, j)),
                  pl.BlockSpec((None, 1, tn), lambda j: (layer, 0, j))],
        out_specs=pl.BlockSpec((2, tn), lambda j: (0, j)),
        out_shape=jax.ShapeDtypeStruct((2, n), F32),
        compiler_params=_cparams(1),
    )(ct, w_all, b_all.reshape(n_layers, 1, n))
    return jnp.pad(out.reshape(2, 6, d), ((0, 0), (0, 2), (0, 0)))


def _modulated(x, m_ref):
    return (x * (1.0 + m_ref[1:2, :]) + m_ref[0:1, :]).astype(BF16)


def _modulate_kernel(x_ref, m_ref, o_ref):
    o_ref[...] = _modulated(x_ref[...], m_ref)


def _mods_spec(d, n_lat_tiles):
    return pl.BlockSpec((None, 8, d), lambda i: (jnp.where(i < n_lat_tiles, 0, 1), 0, 0))


def _modulate(x_all, mods, n_lat_tiles):
    t_all, d = x_all.shape
    row = pl.BlockSpec((ROW_TILE, d), lambda i: (i, 0))
    return pl.pallas_call(
        _modulate_kernel,
        grid=(t_all // ROW_TILE,),
        in_specs=[row, _mods_spec(d, n_lat_tiles)],
        out_specs=row,
        out_shape=jax.ShapeDtypeStruct((t_all, d), BF16),
        compiler_params=_cparams(1),
    )(x_all, mods)


def _matmul_kernel(x_ref, w_ref, o_ref):
    o_ref[...] = _nn(x_ref[...], w_ref[...])


def _inproj(xm, w_bf):
    t_all, d = xm.shape
    n = w_bf.shape[1]
    tm = MM_ROW_TILE if t_all % MM_ROW_TILE == 0 else ROW_TILE
    tn = min(n, MM_COL_TILE)
    return pl.pallas_call(
        _matmul_kernel,
        grid=(t_all // tm, n // tn),
        in_specs=[pl.BlockSpec((tm, d), lambda i, j: (i, 0)),
                  pl.BlockSpec((d, tn), lambda i, j: (0, j))],
        out_specs=pl.BlockSpec((tm, tn), lambda i, j: (i, j)),
        out_shape=jax.ShapeDtypeStruct((t_all, n), F32),
        compiler_params=_cparams(2),
    )(xm, w_bf)


def _ln_rows(z, g, b):
    mu = jnp.mean(z, axis=-1, keepdims=True)
    zc = z - mu
    return zc * lax.rsqrt(jnp.mean(zc * zc, axis=-1, keepdims=True) + LN_EPS) * g + b


def _outproj_kernel(*refs, glu, n_row_tiles):
    hf_ref, hb_ref, gate_ref = refs[:3]
    refs = refs[3:]
    if glu:
        (y5_ref, u5_ref, ds_ref, gw_ref, gb_ref, w1_ref, w2_ref, x_ref, m_ref, g_ref, b_ref,
         x1_ref, h2_ref) = refs
        y = _gelu_tanh(y5_ref[...] + ds_ref[...] * u5_ref[...])
        a2 = y * _sigmoid(_nn(y.astype(BF16), gw_ref[...]) + gb_ref[...])
    else:
        a2_ref, w1_ref, w2_ref, x_ref, m_ref, g_ref, b_ref, x1_ref, h2_ref = refs
        a2 = a2_ref[...]
    heads = []
    for h in range(N_HEADS):
        hs = slice(h * HEAD_DIM, (h + 1) * HEAD_DIM)
        gx = gate_ref[:, hs]
        gate = gx * _sigmoid(gx) if glu else _sigmoid(gx)
        heads.append((_head_norm(hf_ref[:, hs] + hb_ref[:, hs]) * gate).astype(BF16))
    a1 = jnp.concatenate(heads, axis=1)
    y = _nn(a1, w1_ref[...]) + _nn(a2.astype(BF16), w2_ref[...])
    x1 = _ln_rows(ALPHA * x_ref[...] + m_ref[2:3, :] * y, g_ref[...], b_ref[...])
    x1_ref[...] = x1
    h2t = (x1 * (1.0 + m_ref[4:5, :]) + m_ref[3:4, :]).T
    h2_ref[...] = jnp.where(pl.program_id(0) < n_row_tiles, h2t, 0.0).astype(BF16)


def _outproj(scan_parts, a2_parts, w_out_bf, x_all, mods, ln_g, ln_b, n_lat_tiles, glu_params=None):
    t_all, d = x_all.shape
    half = HALF
    n_row_tiles = t_all // ROW_TILE
    t_pad = _padded_tokens(t_all)
    row = lambda i: (jnp.minimum(i, n_row_tiles - 1), 0)
    const = lambda i: (0, 0)
    a2_parts = list(scan_parts) + list(a2_parts)
    a2_specs = [pl.BlockSpec((ROW_TILE, half), lambda i, c=c: (jnp.minimum(i, n_row_tiles - 1), c))
                for _, c in a2_parts]
    a2_args = [a for a, _ in a2_parts]
    if glu_params is not None:
        d_skip, glu_w_bf, glu_b = glu_params
        a2_specs += [pl.BlockSpec((1, half), const), pl.BlockSpec((half, half), const),
                     pl.BlockSpec((1, half), const)]
        a2_args += [d_skip.reshape(1, half), glu_w_bf, glu_b.reshape(1, half)]
    return pl.pallas_call(
        functools.partial(_outproj_kernel, glu=glu_params is not None, n_row_tiles=n_row_tiles),
        grid=(t_pad // ROW_TILE,),
        in_specs=a2_specs + [
            pl.BlockSpec((half, d), const), pl.BlockSpec((half, d), const),
            pl.BlockSpec((ROW_TILE, d), row),
            pl.BlockSpec((None, 8, d), lambda i: (jnp.where(i < n_lat_tiles, 0, 1), 0, 0)),
            pl.BlockSpec((1, d), const), pl.BlockSpec((1, d), const)],
        out_specs=[pl.BlockSpec((ROW_TILE, d), row), pl.BlockSpec((d, ROW_TILE), lambda i: (0, i))],
        out_shape=[jax.ShapeDtypeStruct((t_all, d), F32), jax.ShapeDtypeStruct((d, t_pad), BF16)],
        compiler_params=_cparams(1),
    )(*a2_args, w_out_bf[:half], w_out_bf[half:], x_all, mods, ln_g.reshape(1, d), ln_b.reshape(1, d))


def _ln2_kernel(x_ref, p_ref, m_ref, g_ref, b_ref, *rest):
    x2 = _ln_rows(ALPHA * x_ref[...] + m_ref[5:6, :] * p_ref[...].T, g_ref[...], b_ref[...])
    if len(rest) == 1:
        rest[0][...] = x2
    else:
        mn_ref, o_ref, xm_ref = rest
        o_ref[...] = x2
        xm_ref[...] = _modulated(x2, mn_ref)


def _ln2(x1, peer_out_t, mods, ln_g, ln_b, n_lat_tiles, mods_next=None):
    t_all, d = x1.shape
    row = pl.BlockSpec((ROW_TILE, d), lambda i: (i, 0))
    col = pl.BlockSpec((d, ROW_TILE), lambda i: (0, i))
    vec = pl.BlockSpec((1, d), lambda i: (0, 0))
    nxt = mods_next is not None
    n_rows = t_all if nxt else n_lat_tiles * ROW_TILE
    return pl.pallas_call(
        _ln2_kernel,
        grid=(n_rows // ROW_TILE,),
        in_specs=[row, col, _mods_spec(d, n_lat_tiles), vec, vec] + ([_mods_spec(d, n_lat_tiles)] if nxt else []),
        out_specs=[row, row] if nxt else row,
        out_shape=([jax.ShapeDtypeStruct((t_all, d), F32), jax.ShapeDtypeStruct((t_all, d), BF16)] if nxt
                   else jax.ShapeDtypeStruct((n_rows, d), F32)),
        compiler_params=_cparams(1),
    )(x1, peer_out_t, mods, ln_g.reshape(1, d), ln_b.reshape(1, d), *([mods_next] if nxt else []))


def _scan_block(rev, n_chunks, n_lat_chunks):
    if rev:
        return lambda s: n_chunks - 1 - s
    n_ctx = n_chunks - n_lat_chunks
    return lambda s: jnp.where(s < n_ctx, s + n_lat_chunks, s - n_ctx)


def _scan_mask(rev, n):
    row = lax.broadcasted_iota(jnp.int32, (n, n), 0)
    col = lax.broadcasted_iota(jnp.int32, (n, n), 1)
    return (col >= row) if rev else (col <= row)


def _rope(x, cos, sin):
    lane = lax.broadcasted_iota(jnp.int32, x.shape, 1)
    partner = jnp.where((lane % 64) < 32, pltpu.roll(x, 96, 1), pltpu.roll(x, 32, 1))
    return x * cos + partner * sin


def _mlstm_kernel(*refs):
    n_in = 5
    fw_in, bw_in = refs[:n_in], refs[n_in:2 * n_in]
    bias_ref, biast_ref, out_f, out_b = refs[2 * n_in:2 * n_in + 4]
    state = refs[2 * n_in + 4:]

    @pl.when(pl.program_id(0) == 0)
    def _():
        for r in state:
            r[...] = jnp.zeros_like(r)

    _mlstm_chunk(*fw_in, bias_ref, biast_ref, out_f, *state[:2], rev=False)
    _mlstm_chunk(*bw_in, bias_ref, biast_ref, out_b, *state[2:], rev=True)


def _mlstm_chunk(q_ref, k_ref, v_ref, g_ref, gt_ref, bias_ref, biast_ref, out_ref, cn_ref, m_ref, *, rev):
    n = q_ref.shape[0]
    mask = _scan_mask(rev, n)
    tri = mask.astype(BF16)
    pre_c = g_ref[...] + bias_ref[...]
    pre_r = gt_ref[...] + biast_ref[...]
    b_c = sum(_nn(tri, p) for p in _split3(_log_sigmoid(pre_c)))
    b_r = sum(_nt(p, tri) for p in _split3(_log_sigmoid(pre_r)))
    ones = jnp.ones((n, HEAD_DIM), BF16)
    last = 0 if rev else n - 1
    off = N_HEADS if rev else 0

    for h in range(N_HEADS):
        hs = slice(h * HEAD_DIM, (h + 1) * HEAD_DIM)
        ci, cf = off + h, 2 * N_HEADS + off + h
        bc, br = b_c[:, cf:cf + 1], b_r[cf:cf + 1, :]
        ic, ir = pre_c[:, ci:ci + 1], pre_r[ci:ci + 1, :]
        m_prev = m_ref[h:h + 1, 0:1]
        logw = jnp.where(mask, bc - br + ir, NEG)
        inter = bc + m_prev
        m_t = jnp.maximum(inter, jnp.max(logw, axis=1, keepdims=True))
        w_inter = jnp.exp(inter - m_t)
        qb, kb = q_ref[:, hs], k_ref[:, hs]
        v_ext = jnp.concatenate([v_ref[:, hs].astype(BF16), ones], axis=1)
        s = _nt(qb, kb) * jnp.exp(logw - m_t)
        cn_old = cn_ref[h]
        both = _nn(s.astype(BF16), v_ext) + _nn(qb, cn_old.astype(BF16)) * w_inter
        den = jnp.maximum(jnp.abs(both[:, HEAD_DIM:]), jnp.exp(-m_t))
        out_ref[:, hs] = both[:, :HEAD_DIM] / den
        m_new = m_t[last:last + 1, :]
        b_last = bc[last:last + 1, :]
        w_old = jnp.exp(b_last + m_prev - m_new)
        kw = kb.astype(F32) * jnp.exp(b_last - bc + ic - m_new)
        cn_ref[h] = w_old * cn_old + _tn(kw.astype(BF16), v_ext)
        m_ref[h:h + 1, :] = jnp.broadcast_to(m_new, (1, HEAD_DIM))


def _rope_kernel(q_ref, k_ref, cos_ref, sin_ref, qo_ref, ko_ref):
    cos, sin = cos_ref[...], sin_ref[...]
    for h in range(N_HEADS):
        hs = slice(h * HEAD_DIM, (h + 1) * HEAD_DIM)
        qo_ref[:, hs] = (_rope(q_ref[:, hs], cos, sin) * QK_SCALE).astype(BF16)
        ko_ref[:, hs] = _rope(k_ref[:, hs], cos, sin).astype(BF16)


def _rope_qk(proj, cos_t, sin_t):
    t_all = proj.shape[0]
    tok = pl.BlockSpec((ROW_TILE, 128), lambda i: (i, 0))
    return pl.pallas_call(
        _rope_kernel,
        grid=(t_all // ROW_TILE,),
        in_specs=[pl.BlockSpec((ROW_TILE, HALF), lambda i: (i, 0)),
                  pl.BlockSpec((ROW_TILE, HALF), lambda i: (i, 1)), tok, tok],
        out_specs=[pl.BlockSpec((ROW_TILE, HALF), lambda i: (i, 0))] * 2,
        out_shape=[jax.ShapeDtypeStruct((t_all, HALF), BF16)] * 2,
        compiler_params=_cparams(1),
    )(proj, proj, cos_t, sin_t)


def _mlstm(proj, gates, gates_t, cos_t, sin_t, igate_b, fgate_b, n_lat_chunks):
    t_all = proj.shape[0]
    n = SCAN_CHUNK
    n_chunks = t_all // n
    bias = jnp.zeros((1, 128), F32).at[0, :4 * N_HEADS].set(
        jnp.concatenate([igate_b.reshape(-1), fgate_b.reshape(-1)]))
    q_rot, k_rot = _rope_qk(proj, cos_t, sin_t)

    def dir_specs(rev):
        blk = _scan_block(rev, n_chunks, n_lat_chunks)
        colblk = lambda c: pl.BlockSpec((n, HALF), lambda s, c=c: (blk(s), c))
        ins = [colblk(0), colblk(0), colblk(2), pl.BlockSpec((n, 128), lambda s: (blk(s), 0)),
               pl.BlockSpec((128, n), lambda s: (0, blk(s)))]
        return ins, pl.BlockSpec((n, HALF), lambda s: (blk(s), 0))

    (in_f, out_f), (in_b, out_b) = dir_specs(False), dir_specs(True)
    dir_args = [q_rot, k_rot, proj, gates, gates_t]
    state = [pltpu.VMEM((N_HEADS, HEAD_DIM, 2 * HEAD_DIM), F32), pltpu.VMEM((N_HEADS, HEAD_DIM), F32)]
    return pl.pallas_call(
        _mlstm_kernel,
        grid=(n_chunks,),
        in_specs=in_f + in_b + [pl.BlockSpec((1, 128), lambda s: (0, 0)), pl.BlockSpec((128, 1), lambda s: (0, 0))],
        out_specs=[out_f, out_b],
        out_shape=[jax.ShapeDtypeStruct((t_all, HALF), F32)] * 2,
        scratch_shapes=state + state,
        compiler_params=_cparams(1),
    )(*dir_args, *dir_args, bias, bias.reshape(128, 1))


def _na_kernel(q_ref, kp_ref, kc_ref, kn_ref, vp_ref, vc_ref, vn_ref, kx_ref, vx_ref, bias_ref, o_ref):
    qb = (q_ref[...] * QK_SCALE).astype(BF16)
    kk = jnp.concatenate([kp_ref[...], kc_ref[...], kn_ref[...]], axis=0).astype(BF16)
    vv = jnp.concatenate([vp_ref[...], vc_ref[...], vn_ref[...]], axis=0).astype(BF16)
    s_loc = _nt(qb, kk) + bias_ref[...]
    s_ctx = _nt(qb, kx_ref[...].astype(BF16))
    m = jnp.maximum(jnp.max(s_loc, axis=1, keepdims=True), jnp.max(s_ctx, axis=1, keepdims=True))
    p_loc = jnp.exp(s_loc - m)
    p_ctx = jnp.exp(s_ctx - m)
    l = jnp.sum(p_loc, axis=1, keepdims=True) + jnp.sum(p_ctx, axis=1, keepdims=True)
    o_ref[...] = (_nn(p_loc.astype(BF16), vv) + _nn(p_ctx.astype(BF16), vx_ref[...].astype(BF16))) / l


def _na_bias(rpb, rows):
    qn = NA_QROWS * GRID_W
    kn = 2 * qn
    ql = np.arange(qn)
    kl = np.arange(kn)
    out = []
    nqb = rows // NA_QROWS
    lr, krl = np.arange(NA_QROWS)[:, None], np.arange(2 * NA_QROWS)[None, :]
    dr = np.clip(krl - NA_QROWS // 2 - lr + NA_WIN_ROWS - 1, 0, 2 * NA_WIN_ROWS - 2)
    col = np.arange(GRID_W)
    dc = np.clip(col[None, :] - col[:, None] + NA_WIN_COLS - 1, 0, 2 * NA_WIN_COLS - 2)
    oh_r = jnp.asarray(dr[..., None] == np.arange(2 * NA_WIN_ROWS - 1), F32)
    oh_c = jnp.asarray(dc[..., None] == np.arange(2 * NA_WIN_COLS - 1), F32)
    table = jnp.einsum('hab,lka,qcb->hlqkc', rpb, oh_r, oh_c,
                       precision=lax.Precision.HIGHEST).reshape(rpb.shape[0], qn, kn)
    for qb in (0, 1, nqb - 1):
        qr = NA_QROWS * qb + ql // GRID_W
        qc = ql % GRID_W
        kr = NA_QROWS * qb - NA_QROWS // 2 + kl // GRID_W
        kc = kl % GRID_W
        r0 = np.clip(qr - NA_WIN_ROWS // 2, 0, rows - NA_WIN_ROWS)
        c0 = np.clip(qc - NA_WIN_COLS // 2, 0, GRID_W - NA_WIN_COLS)
        ok = ((kr[None, :] >= r0[:, None]) & (kr[None, :] < r0[:, None] + NA_WIN_ROWS)
              & (kc[None, :] >= c0[:, None]) & (kc[None, :] < c0[:, None] + NA_WIN_COLS)
              & (kr[None, :] >= 0) & (kr[None, :] < rows))
        out.append(jnp.where(jnp.asarray(ok)[None], table, NEG))
    return jnp.stack(out)


def _na(proj, rpb, t_lat, t_ctx):
    rows = t_lat // GRID_W
    qn = NA_QROWS * GRID_W
    hn = qn // 2
    nqb = rows // NA_QROWS
    assert nqb >= 2 and t_ctx == hn
    bias = _na_bias(rpb.astype(F32), rows)
    last_half = 2 * nqb - 1
    ctx_blk = t_lat // hn
    cls = lambda b: jnp.where(b == 0, 0, jnp.where(b == nqb - 1, 2, 1))

    def kv_specs(col0):
        return [pl.BlockSpec((hn, HEAD_DIM), lambda h, b: (jnp.maximum(2 * b - 1, 0), col0 + h)),
                pl.BlockSpec((qn, HEAD_DIM), lambda h, b: (b, col0 + h)),
                pl.BlockSpec((hn, HEAD_DIM), lambda h, b: (jnp.minimum(2 * b + 2, last_half), col0 + h))]

    kcol, vcol = 5 * N_HEADS, 6 * N_HEADS
    return pl.pallas_call(
        _na_kernel,
        grid=(N_HEADS, nqb),
        in_specs=[pl.BlockSpec((qn, HEAD_DIM), lambda h, b: (b, 4 * N_HEADS + h))]
        + kv_specs(kcol) + kv_specs(vcol)
        + [pl.BlockSpec((hn, HEAD_DIM), lambda h, b: (ctx_blk, kcol + h)),
           pl.BlockSpec((hn, HEAD_DIM), lambda h, b: (ctx_blk, vcol + h)),
           pl.BlockSpec((None, None, qn, 2 * qn), lambda h, b: (cls(b), h, 0, 0))],
        out_specs=pl.BlockSpec((qn, HEAD_DIM), lambda h, b: (b, h)),
        out_shape=jax.ShapeDtypeStruct((t_lat, HALF), F32),
        compiler_params=_cparams(2),
    )(proj, proj, proj, proj, proj, proj, proj, proj, proj, bias)


def _ctx_attn_kernel(q_ref, k_ref, v_ref, o_ref):
    qb = (q_ref[...] * QK_SCALE).astype(BF16)
    s = _nt(qb, k_ref[...].astype(BF16))
    p = jnp.exp(s - jnp.max(s, axis=1, keepdims=True))
    o_ref[...] = _nn(p.astype(BF16), v_ref[...].astype(BF16)) / jnp.sum(p, axis=1, keepdims=True)


def _ctx_attn(proj, t_lat, t_ctx):
    blk = t_lat // t_ctx
    spec = lambda c: pl.BlockSpec((t_ctx, HEAD_DIM), lambda h, c=c: (blk, c * N_HEADS + h))
    return pl.pallas_call(
        _ctx_attn_kernel,
        grid=(N_HEADS,),
        in_specs=[spec(4), spec(5), spec(6)],
        out_specs=pl.BlockSpec((t_ctx, HEAD_DIM), lambda h: (0, h)),
        out_shape=jax.ShapeDtypeStruct((t_ctx, HALF), F32),
        compiler_params=_cparams(1),
    )(proj, proj, proj)


def _gla_inter(q_ref, i_ref, f_ref, lb_ref, out_ref, st_ref, k_scr, b_scr, *, rev):
    n = q_ref.shape[0]
    mask = _scan_mask(rev, n)
    tri = mask.astype(BF16)
    last = 0 if rev else n - 1
    x = f_ref[...]
    key = lb_ref[0:1, :] * _sigmoid(-x)
    la, lc = lb_ref[1:2, :], lb_ref[2:3, :] + _log_sigmoid(x)
    logf = jnp.maximum(la, lc) + jnp.log1p(jnp.exp(-jnp.abs(la - lc)))
    b = sum(_nn(tri, p) for p in _split3(logf))
    b_end = b[last:last + 1, :]
    qx = q_ref[...]
    q = qx * _sigmoid(qx)
    qt = q * jnp.exp(b)
    kdec = key * jnp.exp(b_end - b)
    w_end = jnp.exp(b_end)
    k_scr[...] = key
    b_scr[...] = b
    for h in range(N_HEADS):
        hs = slice(h * HEAD_DIM, (h + 1) * HEAD_DIM)
        st = st_ref[h]
        vb = i_ref[:, hs].astype(BF16)
        out_ref[:, hs] = _nt(qt[:, hs].astype(BF16), st.astype(BF16))
        st_ref[h] = st * w_end[:, hs] + _tn(vb, kdec[:, hs].astype(BF16))
    return mask, q, b_end


def _gla_intra_factored(mask, q, b_end, i_ref, out_ref, k_scr, b_scr):
    mid = 0.5 * b_end
    qm = q * jnp.exp(b_scr[...] - mid)
    km = k_scr[...] * jnp.exp(mid - b_scr[...])
    for h in range(N_HEADS):
        hs = slice(h * HEAD_DIM, (h + 1) * HEAD_DIM)
        att = jnp.where(mask, _nt(qm[:, hs].astype(BF16), km[:, hs].astype(BF16)), 0.0)
        out_ref[:, hs] += _nn(att.astype(BF16), i_ref[:, hs].astype(BF16))


def _gla_intra_exact(q, i_ref, out_ref, k_scr, b_scr, *, rev):
    n = q.shape[0]
    t_idx = lax.broadcasted_iota(jnp.int32, (n, 1), 0)
    for h in range(N_HEADS):
        hs = slice(h * HEAD_DIM, (h + 1) * HEAD_DIM)
        q_h = q[:, hs]
        b_h = b_scr[:, hs]

        def body(grp, acc):
            base = pl.multiple_of(grp * 8, 8)
            b8, k8, v8 = b_scr[pl.ds(base, 8), hs], k_scr[pl.ds(base, 8), hs], i_ref[pl.ds(base, 8), hs]
            for r in range(8):
                s = base + r
                e = jnp.exp(jnp.minimum(b_h - b8[r:r + 1, :], 0.0))
                a = jnp.sum(q_h * k8[r:r + 1, :] * e, axis=1, keepdims=True)
                a = jnp.where((t_idx <= s) if rev else (t_idx >= s), a, 0.0)
                acc = acc + a * v8[r:r + 1, :]
            return acc

        out_ref[:, hs] += lax.fori_loop(0, n // 8, body, jnp.zeros((n, HEAD_DIM), F32))


def _gla_kernel(*refs):
    fw_in, bw_in, lb_ref = refs[0:3], refs[3:6], refs[6]
    out_f, out_b = refs[7:9]
    st_f, k_f, b_f, st_b, k_b, b_b = refs[9:]

    @pl.when(pl.program_id(0) == 0)
    def _():
        st_f[...] = jnp.zeros_like(st_f)
        st_b[...] = jnp.zeros_like(st_b)

    mask_f, q_f, end_f = _gla_inter(*fw_in, lb_ref, out_f, st_f, k_f, b_f, rev=False)
    mask_b, q_b, end_b = _gla_inter(*bw_in, lb_ref, out_b, st_b, k_b, b_b, rev=True)
    safe = jnp.minimum(jnp.min(end_f), jnp.min(end_b)) >= 2.0 * GLA_SAFE_LOG

    @pl.when(safe)
    def _():
        _gla_intra_factored(mask_f, q_f, end_f, fw_in[1], out_f, k_f, b_f)
        _gla_intra_factored(mask_b, q_b, end_b, bw_in[1], out_b, k_b, b_b)

    @pl.when(jnp.logical_not(safe))
    def _():
        _gla_intra_exact(q_f, fw_in[1], out_f, k_f, b_f, rev=False)
        _gla_intra_exact(q_b, bw_in[1], out_b, k_b, b_b, rev=True)


def _gla(proj, lb, n_lat_chunks):
    t_all = proj.shape[0]
    n = GLA_CHUNK
    n_chunks = t_all // n

    def dir_specs(rev):
        blk = _scan_block(rev, n_chunks, n_lat_chunks)
        colblk = lambda c: pl.BlockSpec((n, HALF), lambda s, c=c: (blk(s), c))
        return [colblk(0), colblk(1), colblk(3 if rev else 2)], pl.BlockSpec((n, HALF), lambda s: (blk(s), 0))

    (in_f, out_f), (in_b, out_b) = dir_specs(False), dir_specs(True)
    state = [pltpu.VMEM((N_HEADS, HEAD_DIM, HEAD_DIM), F32), pltpu.VMEM((n, HALF), F32),
             pltpu.VMEM((n, HALF), F32)]
    lb = lb.reshape(1, HALF)
    lb_rows = jnp.concatenate([1.0 - lb, jnp.log(lb), jnp.log1p(-lb)], axis=0)
    return pl.pallas_call(
        _gla_kernel,
        grid=(n_chunks,),
        in_specs=in_f + in_b + [pl.BlockSpec((3, HALF), lambda s: (0, 0))],
        out_specs=[out_f, out_b],
        out_shape=[jax.ShapeDtypeStruct((t_all, HALF), F32)] * 2,
        scratch_shapes=state + state,
        compiler_params=_cparams(1),
    )(*([proj] * 6), lb_rows)


def _s5_mats(a_re, a_im, log_dt, b_re, b_im, c_re, c_im):
    hp = lax.Precision.HIGHEST
    n = S5_CHUNK
    dt = jnp.exp(log_dt)[..., None]
    la_re, la_im = a_re * dt, a_im * dt
    mag = jnp.exp(la_re)
    ab_re, ab_im = mag * jnp.cos(la_im), mag * jnp.sin(la_im)
    nr, ni = ab_re - 1.0, ab_im
    den = jnp.square(a_re) + jnp.square(a_im)
    cr = (nr * a_re + ni * a_im) / den
    ci = (ni * a_re - nr * a_im) / den
    bb_re = cr[..., None] * b_re - ci[..., None] * b_im
    bb_im = cr[..., None] * b_im + ci[..., None] * b_re

    def apow(tau):
        tau = jnp.asarray(tau, F32)[None, None, :, None]
        m = jnp.exp(tau * la_re[:, :, None, :])
        return m * jnp.cos(tau * la_im[:, :, None, :]), m * jnp.sin(tau * la_im[:, :, None, :])

    def c_apow(tau):
        pr, pi = apow(tau)
        return (c_re[:, :, None] * pr[:, :, :, None] - c_im[:, :, None] * pi[:, :, :, None],
                c_re[:, :, None] * pi[:, :, :, None] + c_im[:, :, None] * pr[:, :, :, None])

    ca_re, ca_im = c_apow(np.arange(n))
    kern = jnp.einsum('dgtip,dgpj->dgtij', jnp.concatenate([ca_re, -ca_im], axis=-1),
                      jnp.concatenate([bb_re, bb_im], axis=2), precision=hp)
    s_idx, t_idx = np.arange(n)[:, None], np.arange(n)[None, :]
    g = a_re.shape[1]
    lp = n * S5_GROUP

    def toeplitz(kd, lag):
        onehot = jnp.asarray(lag[None] == np.arange(n)[:, None, None], F32)
        return jnp.einsum('lst,glij->gsjti', onehot, kd, precision=hp).reshape(g, lp, lp)

    kmat = toeplitz(kern[0], t_idx - s_idx) + toeplitz(kern[1], s_idx - t_idx)

    def emat(d, tau):
        pr, pi = apow(tau)
        pr, pi = pr[d][:, :, None, :], pi[d][:, :, None, :]
        br, bi = bb_re[d].transpose(0, 2, 1)[:, None], bb_im[d].transpose(0, 2, 1)[:, None]
        return jnp.concatenate([pr * br - pi * bi, pr * bi + pi * br], axis=-1).reshape(g, lp, 2 * S5_STATE)

    def fmat(d, tau):
        fr, fi = c_apow(tau)
        fr, fi = fr[d], fi[d]
        return jnp.concatenate([fr, -fi], axis=-1).transpose(0, 3, 1, 2).reshape(g, 2 * S5_STATE, lp)

    e_all = jnp.concatenate([emat(0, n - 1 - np.arange(n)), emat(1, np.arange(n))], axis=-1)
    f_all = jnp.concatenate([fmat(0, np.arange(n) + 1), fmat(1, n - np.arange(n))], axis=1)
    pr, pi = apow(np.array([n]))
    pr, pi = pr[:, :, 0], pi[:, :, 0]
    ar2 = jnp.concatenate([pr[0], pr[0], pr[1], pr[1]], axis=-1)
    ai2 = jnp.concatenate([-pi[0], pi[0], -pi[1], pi[1]], axis=-1)
    return kmat.astype(BF16), e_all.astype(BF16), f_all.astype(BF16), ar2, ai2


def _s5_state_kernel(u_ref, e_ref, s_ref):
    s_ref[...] = _nn(u_ref[...], e_ref[...])


def _s5_scan_kernel(sf_ref, sb_ref, ar_ref, ai_ref, hf_ref, hb_ref, stf_ref, stb_ref):
    nb = sf_ref.shape[0]
    w = 2 * S5_STATE
    ar, ai = ar_ref[...], ai_ref[...]

    @pl.when(pl.program_id(0) == 0)
    def _():
        stf_ref[...] = jnp.zeros_like(stf_ref)
        stb_ref[...] = jnp.zeros_like(stb_ref)

    def step(h, s, lo):
        return ar[:, lo:lo + w] * h + ai[:, lo:lo + w] * pltpu.roll(h, S5_STATE, 1) + s

    hf, hb = stf_ref[...], stb_ref[...]
    for j in range(nb):
        hf_ref[j] = hf
        hf = step(hf, sf_ref[j], 0)
        hb_ref[nb - 1 - j] = hb
        hb = step(hb, sb_ref[nb - 1 - j], w)
    stf_ref[...] = hf
    stb_ref[...] = hb


def _s5_out_kernel(u_ref, k_ref, h_ref, f_ref, y_ref):
    y_ref[...] = _nn(u_ref[...], k_ref[...]) + _nn(h_ref[...].astype(BF16), f_ref[...])


def _s5(u, mats, n_lat_chunks):
    kmat, emat, fmat, ar2, ai2 = mats
    t_all = u.shape[0]
    n = S5_CHUNK
    nc = t_all // n
    g = S5_GROUPS
    lp = n * S5_GROUP
    w4 = 4 * S5_STATE
    ug = u.astype(BF16).reshape(nc, n, g, S5_GROUP).transpose(2, 0, 1, 3).reshape(g, nc, lp)
    grp = lambda a, b: pl.BlockSpec((None, a, b), lambda i: (i, 0, 0))
    s_end = pl.pallas_call(
        _s5_state_kernel, grid=(g,),
        in_specs=[grp(nc, lp), grp(lp, w4)], out_specs=grp(nc, w4),
        out_shape=jax.ShapeDtypeStruct((g, nc, w4), F32), compiler_params=_cparams(1),
    )(ug, emat)
    nb = S5_SCAN_BLOCK
    w2 = 2 * S5_STATE
    fw_blk = _scan_block(False, nc // nb, n_lat_chunks // nb)
    bw_blk = _scan_block(True, nc // nb, n_lat_chunks // nb)
    s_t = s_end.transpose(1, 0, 2)
    coef = pl.BlockSpec((g, w4), lambda i: (0, 0))
    h_fw, h_bw = pl.pallas_call(
        _s5_scan_kernel, grid=(nc // nb,),
        in_specs=[pl.BlockSpec((nb, g, w2), lambda i: (fw_blk(i), 0, 0)),
                  pl.BlockSpec((nb, g, w2), lambda i: (bw_blk(i), 0, 1)), coef, coef],
        out_specs=[pl.BlockSpec((nb, g, w2), lambda i: (fw_blk(i), 0, 0)),
                   pl.BlockSpec((nb, g, w2), lambda i: (bw_blk(i), 0, 0))],
        out_shape=[jax.ShapeDtypeStruct((nc, g, w2), F32)] * 2,
        scratch_shapes=[pltpu.VMEM((g, w2), F32)] * 2,
        compiler_params=_cparams(1),
    )(s_t, s_t, ar2, ai2)
    h_in = jnp.concatenate([h_fw, h_bw], axis=-1).transpose(1, 0, 2)
    y = pl.pallas_call(
        _s5_out_kernel, grid=(g,),
        in_specs=[grp(nc, lp), grp(lp, lp), grp(nc, w4), grp(w4, lp)], out_specs=grp(nc, lp),
        out_shape=jax.ShapeDtypeStruct((g, nc, lp), F32), compiler_params=_cparams(1),
    )(ug, kmat, h_in, fmat)
    return y.reshape(g, nc, n, S5_GROUP).transpose(1, 2, 0, 3).reshape(t_all, HALF)


def _top_rows(x, k):
    rows = []
    for _ in range(k):
        mx = jnp.max(x, axis=0, keepdims=True)
        rows.append(mx)
        x = jnp.where(x == mx, NEG, x)
    return rows


def _top_rows_sorted(x, k):
    n = x.shape[0] // 8
    cols = [x[8 * i:8 * (i + 1), :] for i in range(n)]
    size = 2
    while size <= n:
        stride = size // 2
        while stride >= 1:
            for i in range(n):
                j = i ^ stride
                if j > i:
                    hi, lo = jnp.maximum(cols[i], cols[j]), jnp.minimum(cols[i], cols[j])
                    cols[i], cols[j] = (hi, lo) if (i & size) == 0 else (lo, hi)
            stride //= 2
        size *= 2
    rows = []
    for r in range(k):
        mx = jnp.max(cols[0], axis=0, keepdims=True)
        rows.append(mx)
        popped = cols[0] == mx
        live = min(n, k - r)
        for i in range(live):
            nxt = cols[i + 1] if i + 1 < n else NEG
            cols[i] = jnp.where(popped, nxt, cols[i])
    return rows


def _peer_route_kernel(ht_ref, wq_ref, keys_ref, e1_ref, e0_ref, th_ref, top_scr):
    kk = PEER_TOPK
    qt = _nn(wq_ref[...], ht_ref[...])
    for hp in range(2 * PEER_HEADS):
        sc = _nn(keys_ref[hp], qt[hp * HEAD_DIM:(hp + 1) * HEAD_DIM, :].astype(BF16))
        h, p = divmod(hp, 2)
        if p == 0:
            e0_ref[h] = sc
        else:
            e1_ref[h] = sc
        for r, mx in enumerate(_top_rows_sorted(sc, kk + 1)):
            top_scr[hp, r:r + 1, :] = mx
        top_scr[hp, kk + 1:, :] = jnp.full((PEER_TOP_ROWS - kk - 1, sc.shape[1]), NEG, F32)

    row = lax.broadcasted_iota(jnp.int32, (PEER_TOP_ROWS, 1), 0)
    for h in range(PEER_HEADS):
        a = top_scr[2 * h]
        b = top_scr[2 * h + 1]
        parts = [a[0:1] + b, a[1:2] + b[0:8], a[2:3] + b[0:8], a[3:4] + b[0:8]]
        parts += [jnp.where(row >= 4, a + b[j:j + 1], NEG) for j in range(3)]
        cand = jnp.maximum(jnp.concatenate(parts, axis=0), NEG)
        top = _top_rows(cand, kk + 1)
        thr = 0.5 * (top[kk - 1] + top[kk])
        z = jnp.sum(jnp.where(cand >= thr, jnp.exp(cand - top[0]), 0.0), axis=0, keepdims=True)
        e0_ref[h] = jnp.exp(e0_ref[h] - a[0:1]) / z
        e1_ref[h] = jnp.exp(e1_ref[h] - b[0:1])
        th_ref[h:h + 1, :] = jnp.exp(thr - top[0]) / z


def _peer_expert_kernel(ht_ref, e1_ref, e0_ref, th_ref, *rest, n_tiles):
    u_refs = rest[:PEER_U_SPLIT]
    vt_ref, o_ref, at0_scr, at1_scr, g0_scr, g1_scr = rest[PEER_U_SPLIT:]
    s = pl.program_id(1)
    assert n_tiles % 2 == 0 and n_tiles >= 4
    ni = u_refs[0].shape[0] // PEER_N_KEYS
    kc = u_refs[0].shape[1]
    at_scr, g_scr = (at0_scr, at1_scr), (g0_scr, g1_scr)

    def stages(par, dot1, gate, dot2, first=False):
        if dot1:
            at_scr[par][...] = _nn(jnp.concatenate([u_k[...] for u_k in u_refs], axis=1), ht_ref[...])
        if gate:
            at_old, g_new = at_scr[1 - par], g_scr[1 - par]
            nj, nc = PEER_GATE_ROWS, 128
            for ii in range(ni):
                i = (s - 1) * ni + ii
                e0_rows = [e0_ref[h, pl.ds(i, 1), :] for h in range(PEER_HEADS)]
                for c in range(at_old.shape[1] // nc):
                    cs = slice(c * nc, (c + 1) * nc)
                    for jb in range(PEER_N_KEYS // nj):
                        js = slice(jb * nj, (jb + 1) * nj)
                        w = None
                        for h in range(PEER_HEADS):
                            p = e1_ref[h, js, cs] * e0_rows[h][:, cs]
                            t = jnp.where(p >= th_ref[h:h + 1, cs], p, 0.0)
                            w = t if w is None else w + t
                        rs = slice(ii * PEER_N_KEYS + jb * nj, ii * PEER_N_KEYS + (jb + 1) * nj)
                        g_new[rs, cs] = (_gelu_tanh(at_old[rs, cs]) * w).astype(BF16)
        if dot2:
            prod = _nn(vt_ref[...], g_scr[par][...])
            if first:
                o_ref[...] = prod
            else:
                o_ref[...] += prod

    mid = jnp.logical_and(s > 2, s < n_tiles)
    pl.when(s == 0)(lambda: stages(0, True, False, False))
    pl.when(s == 1)(lambda: stages(1, True, True, False))
    pl.when(s == 2)(lambda: stages(0, True, True, True, first=True))
    pl.when(jnp.logical_and(mid, s % 2 == 0))(lambda: stages(0, True, True, True))
    pl.when(jnp.logical_and(mid, s % 2 == 1))(lambda: stages(1, True, True, True))
    pl.when(s == n_tiles)(lambda: stages(0, False, True, True))
    pl.when(s == n_tiles + 1)(lambda: stages(1, False, False, True))


def _tok3(t):
    return pl.BlockSpec((PEER_HEADS, PEER_N_KEYS, t), lambda i, *_: (0, 0, i))


def _peer_route(h2t, wq_t_bf, keys_bf):
    d, t_all = h2t.shape
    nh, nk = PEER_HEADS, PEER_N_KEYS
    t1 = PEER_P1_TILE
    return pl.pallas_call(
        _peer_route_kernel,
        grid=(t_all // t1,),
        in_specs=[pl.BlockSpec((d, t1), lambda i: (0, i)),
                  pl.BlockSpec(wq_t_bf.shape, lambda i: (0, 0)),
                  pl.BlockSpec(keys_bf.shape, lambda i: (0, 0, 0))],
        out_specs=[_tok3(t1), _tok3(t1), pl.BlockSpec((nh, t1), lambda i: (0, i))],
        out_shape=[jax.ShapeDtypeStruct((nh, nk, t_all), F32)] * 2 + [jax.ShapeDtypeStruct((nh, t_all), F32)],
        scratch_shapes=[pltpu.VMEM((2 * nh, PEER_TOP_ROWS, t1), F32)],
        compiler_params=_cparams(1),
    )(h2t, wq_t_bf, keys_bf)


def _padded_tokens(t_all):
    return t_all + (-t_all % PEER_TOK_TILE)


def _peer(h2t, wq_t_bf, keys_bf, u_r, vt_r, layer=0):
    return _peer_experts(h2t, *_peer_route(h2t, wq_t_bf, keys_bf), u_r, vt_r, layer)


def _peer_tables(u_all, v_all):
    return u_all.astype(BF16), v_all.transpose(0, 2, 1).astype(BF16)


def _peer_experts(h2t, e1, e0, th, u_r, vt_r, layer):
    d, t_all = h2t.shape
    tt = PEER_TOK_TILE
    et = PEER_EXP_TILE
    n_tiles = u_r.shape[1] // et
    kc = d // PEER_U_SPLIT
    tok = pl.BlockSpec((d, tt), lambda i, s: (0, i))
    return pl.pallas_call(
        functools.partial(_peer_expert_kernel, n_tiles=n_tiles),
        grid=(t_all // tt, n_tiles + 2),
        in_specs=[tok, _tok3(tt), _tok3(tt), pl.BlockSpec((PEER_HEADS, tt), lambda i, s: (0, i))]
        + [pl.BlockSpec((None, et, kc), lambda i, s, k=k: (layer, jnp.minimum(s, n_tiles - 1), k))
           for k in range(PEER_U_SPLIT)]
        + [pl.BlockSpec((None, d, et), lambda i, s: (layer, 0, jnp.clip(s - 2, 0, n_tiles - 1)))],
        out_specs=tok,
        out_shape=jax.ShapeDtypeStruct((d, t_all), F32),
        scratch_shapes=[pltpu.VMEM((et, tt), F32)] * 2 + [pltpu.VMEM((et, tt), BF16)] * 2,
        compiler_params=_cparams(2),
    )(h2t, e1, e0, th, *([u_r] * PEER_U_SPLIT), vt_r)


def _rope_tables(t_lat, t_ctx):
    pos = np.arange(t_lat)
    nfreq = HEAD_DIM // 4
    freqs = ROPE_BASE ** (-jnp.arange(nfreq, dtype=F32) / nfreq)
    ang_r = jnp.asarray(pos // GRID_W, F32)[:, None] * freqs
    ang_c = jnp.asarray(pos % GRID_W, F32)[:, None] * freqs
    cos = jnp.concatenate([jnp.cos(ang_r), jnp.cos(ang_r), jnp.cos(ang_c), jnp.cos(ang_c)], axis=1)
    sin = jnp.concatenate([-jnp.sin(ang_r), jnp.sin(ang_r), -jnp.sin(ang_c), jnp.sin(ang_c)], axis=1)
    cos = jnp.concatenate([cos, jnp.ones((t_ctx, HEAD_DIM), F32)], axis=0)
    sin = jnp.concatenate([sin, jnp.zeros((t_ctx, HEAD_DIM), F32)], axis=0)
    return cos, sin


def _peer_and_norm(x1, h2t, mods, ln_g, ln_b, n_lat_tiles, w_q, sub_keys, tables, layer, mods_next=None):
    keys = sub_keys.reshape(2 * PEER_HEADS, PEER_N_KEYS, -1).astype(BF16)
    pt = _peer(h2t, w_q.T.astype(BF16), keys, *tables, layer)
    return _ln2(x1, pt, mods, ln_g, ln_b, n_lat_tiles, mods_next)


def _layer_ab(x_all, mods, t_lat, t_ctx, w_in, w_out, igate_b, fgate_b, rpb, ln_g, ln_b, xm=None):
    n_lat_tiles = t_lat // ROW_TILE
    if xm is None:
        xm = _modulate(x_all, mods, n_lat_tiles)
    ng = 4 * N_HEADS
    w_main = jnp.concatenate([w_in[:, :4 * HALF], w_in[:, 4 * HALF + ng:]], axis=1).astype(BF16)
    w_gate = jnp.pad(w_in[:, 4 * HALF:4 * HALF + ng], ((0, 0), (0, 128 - ng))).astype(BF16)
    proj = _inproj(xm, w_main)
    gates = _inproj(xm, w_gate)
    cos_t, sin_t = _rope_tables(t_lat, t_ctx)
    h_fw, h_bw = _mlstm(proj, gates, gates.T, cos_t, sin_t, igate_b, fgate_b, t_lat // SCAN_CHUNK)
    y_na = jnp.concatenate([_na(proj, rpb, t_lat, t_ctx), _ctx_attn(proj, t_lat, t_ctx)], axis=0)
    return _outproj([(h_fw, 0), (h_bw, 0), (proj, 3)], [(y_na, 0)], w_out.astype(BF16), x_all, mods,
                    ln_g, ln_b, n_lat_tiles)


def _layer_cd(x_all, mods, t_lat, w_in, w_out, lb, s5p, s5_d, glu_w, glu_b, ln_g, ln_b, xm=None):
    n_lat_tiles = t_lat // ROW_TILE
    if xm is None:
        xm = _modulate(x_all, mods, n_lat_tiles)
    proj = _inproj(xm, w_in.astype(BF16))
    o_fw, o_bw = _gla(proj, lb, t_lat // GLA_CHUNK)
    y5 = _s5(proj[:, 5 * HALF:], _s5_mats(*[p.astype(F32) for p in s5p]), t_lat // S5_CHUNK)
    return _outproj([(o_fw, 0), (o_bw, 0), (proj, 4)], [(y5, 0), (proj, 5)], w_out.astype(BF16), x_all, mods,
                    ln_g, ln_b, n_lat_tiles,
                    glu_params=(s5_d.astype(F32), glu_w.astype(BF16), glu_b.astype(F32)))


def kernel(x, c, ctx, c_ctx, ada_w, ada_b, ln_g, ln_b, ab_w_in, ab_w_out, mlstm_igate_b, mlstm_fgate_b, na_rpb, cd_w_in, cd_w_out, hgrn_lb_logits, s5_a_re, s5_a_im, s5_log_dt, s5_b_re, s5_b_im, s5_c_re, s5_c_im, s5_d, s5_glu_w, s5_glu_b, peer_w_q, peer_sub_keys, peer_u, peer_v):
    t_lat, t_ctx = x.shape[1], ctx.shape[1]
    assert x.shape[0] == 1 and t_ctx == ROW_TILE and t_lat % (NA_QROWS * GRID_W) == 0
    depth = ada_w.shape[0]
    lb_soft = jax.nn.softmax(hgrn_lb_logits.astype(F32), axis=0)
    lower_bounds = jnp.cumsum(lb_soft, axis=0) - lb_soft[0]
    x_all = jnp.concatenate([x[0], ctx[0]], axis=0).astype(F32)
    n_lat_tiles = t_lat // ROW_TILE
    mods_all = [_adaln(c, c_ctx, ada_w, ada_b, l) for l in range(depth)]
    xm = _modulate(x_all, mods_all[0], n_lat_tiles)
    tables = _peer_tables(peer_u, peer_v)
    for l in range(depth):
        j = l // 2
        mods = mods_all[l]
        if l % 2 == 0:
            x1, h2t = _layer_ab(x_all, mods, t_lat, t_ctx, ab_w_in[j], ab_w_out[j], mlstm_igate_b[j],
                                mlstm_fgate_b[j], na_rpb[j], ln_g[l, 0], ln_b[l, 0], xm=xm)
        else:
            s5p = (s5_a_re[j], s5_a_im[j], s5_log_dt[j], s5_b_re[j], s5_b_im[j], s5_c_re[j], s5_c_im[j])
            x1, h2t = _layer_cd(x_all, mods, t_lat, cd_w_in[j], cd_w_out[j], lower_bounds[l], s5p, s5_d[j],
                                s5_glu_w[j], s5_glu_b[j], ln_g[l, 0], ln_b[l, 0], xm=xm)
        peer_args = (x1, h2t, mods, ln_g[l, 1], ln_b[l, 1], n_lat_tiles, peer_w_q[l], peer_sub_keys[l],
                     tables, l)
        if l + 1 < depth:
            x_all, xm = _peer_and_norm(*peer_args, mods_next=mods_all[l + 1])
        else:
            x_all = _peer_and_norm(*peer_args)
    return x_all[None].astype(x.dtype)
```

```python
import functools
import math

import numpy as np
import jax
import jax.numpy as jnp
from jax import lax
from jax.experimental import pallas as pl
from jax.experimental.pallas import tpu as pltpu

F32 = jnp.float32
BF16 = jnp.bfloat16

HEAD_DIM = 128
GRID_W = 64
N_HEADS = 8
HALF = N_HEADS * HEAD_DIM
NA_WIN_ROWS = 8
NA_WIN_COLS = 16
ROPE_BASE = 10000.0
S5_GROUP = 16
S5_STATE = 64
S5_GROUPS = HALF // S5_GROUP
PEER_HEADS = 8
PEER_N_KEYS = 128
PEER_TOPK = 16
PEER_TOP_ROWS = 24
LN_EPS = 1e-5
DEPTH = 2
ALPHA = (2.0 * DEPTH) ** 0.25
QK_SCALE = HEAD_DIM ** -0.5

ROW_TILE = 256
MM_ROW_TILE = 1280
MM_COL_TILE = 1024
SCAN_CHUNK = 128
GLA_CHUNK = 64
S5_CHUNK = 32
S5_SCAN_BLOCK = 8
NA_HEADS = 4
NA_QROWS = 8
PEER_P1_TILE = 256
PEER_TOK_TILE = 512
PEER_EXP_TILE = 512
PEER_GATE_ROWS = 64
PEER_U_SPLIT = 4
NEG = -1e30
GLA_SAFE_LOG = -80.0
VMEM_LIMIT = 56 * 1024 * 1024


def _cparams(n_axes):
    return pltpu.CompilerParams(dimension_semantics=("arbitrary",) * n_axes,
                                vmem_limit_bytes=VMEM_LIMIT)


def _nt(a, b):
    return lax.dot_general(a, b, (((1,), (1,)), ((), ())), preferred_element_type=F32)


def _tn(a, b):
    return lax.dot_general(a, b, (((0,), (0,)), ((), ())), preferred_element_type=F32)


def _nn(a, b):
    return jnp.dot(a, b, preferred_element_type=F32)


def _split3(x):
    hi = x.astype(BF16)
    r = x - hi.astype(F32)
    mid = r.astype(BF16)
    lo = (r - mid.astype(F32)).astype(BF16)
    return hi, mid, lo


def _log_sigmoid(x):
    return jnp.minimum(x, 0.0) - jnp.log1p(jnp.exp(-jnp.abs(x)))


def _sigmoid(x):
    return 1.0 / (1.0 + jnp.exp(-x))


def _gelu_tanh(x):
    return 0.5 * x * (1.0 + jnp.tanh(math.sqrt(2.0 / math.pi) * (x + 0.044715 * (x * x * x))))


def _head_norm(x):
    mu = jnp.mean(x, axis=-1, keepdims=True)
    xc = x - mu
    return xc * lax.rsqrt(jnp.mean(xc * xc, axis=-1, keepdims=True) + LN_EPS)


def _adaln_kernel(ct_ref, w_ref, b_ref, o_ref):
    ct = ct_ref[...]
    s = ct * _sigmoid(ct)
    w = w_ref[...]
    o_ref[0:1, :] = jnp.sum(s[:, 0:1] * w, axis=0, keepdims=True) + b_ref[...]
    o_ref[1:2, :] = jnp.sum(s[:, 1:2] * w, axis=0, keepdims=True) + b_ref[...]


def _adaln(c_lat, c_ctx, w_all, b_all, layer):
    n_layers, d, n = w_all.shape
    tn = n // 8
    ct = jnp.stack([c_lat.reshape(d), c_ctx.reshape(d)], axis=1)
    out = pl.pallas_call(
        _adaln_kernel,
        grid=(n // tn,),
        in_specs=[pl.BlockSpec((d, 2), lambda j: (0, 0)),
                  pl.BlockSpec((None, d, tn), lambda j: (layer, 0, j)),
                  pl.BlockSpec((None, 1, tn), lambda j: (layer, 0, j))],
        out_specs=pl.BlockSpec((2, tn), lambda j: (0, j)),
        out_shape=jax.ShapeDtypeStruct((2, n), F32),
        compiler_params=_cparams(1),
    )(ct, w_all, b_all.reshape(n_layers, 1, n))
    return jnp.pad(out.reshape(2, 6, d), ((0, 0), (0, 2), (0, 0)))


def _modulated(x, m_ref):
    return (x * (1.0 + m_ref[1:2, :]) + m_ref[0:1, :]).astype(BF16)


def _modulate_kernel(x_ref, m_ref, o_ref):
    o_ref[...] = _modulated(x_ref[...], m_ref)


def _mods_spec(d, n_lat_tiles):
    return pl.BlockSpec((None, 8, d), lambda i: (jnp.where(i < n_lat_tiles, 0, 1), 0, 0))


def _modulate(x_all, mods, n_lat_tiles):
    t_all, d = x_all.shape
    row = pl.BlockSpec((ROW_TILE, d), lambda i: (i, 0))
    return pl.pallas_call(
        _modulate_kernel,
        grid=(t_all // ROW_TILE,),
        in_specs=[row, _mods_spec(d, n_lat_tiles)],
        out_specs=row,
        out_shape=jax.ShapeDtypeStruct((t_all, d), BF16),
        compiler_params=_cparams(1),
    )(x_all, mods)


def _matmul_kernel(x_ref, w_ref, o_ref):
    o_ref[...] = _nn(x_ref[...], w_ref[...])


def _inproj(xm, w_bf):
    t_all, d = xm.shape
    n = w_bf.shape[1]
    tm = MM_ROW_TILE if t_all % MM_ROW_TILE == 0 else ROW_TILE
    tn = min(n, MM_COL_TILE)
    return pl.pallas_call(
        _matmul_kernel,
        grid=(t_all // tm, n // tn),
        in_specs=[pl.BlockSpec((tm, d), lambda i, j: (i, 0)),
                  pl.BlockSpec((d, tn), lambda i, j: (0, j))],
        out_specs=pl.BlockSpec((tm, tn), lambda i, j: (i, j)),
        out_shape=jax.ShapeDtypeStruct((t_all, n), F32),
        compiler_params=_cparams(2),
    )(xm, w_bf)


def _ln_rows(z, g, b):
    mu = jnp.mean(z, axis=-1, keepdims=True)
    zc = z - mu
    return zc * lax.rsqrt(jnp.mean(zc * zc, axis=-1, keepdims=True) + LN_EPS) * g + b


def _outproj_kernel(*refs, glu, n_row_tiles):
    hf_ref, hb_ref, gate_ref = refs[:3]
    refs = refs[3:]
    if glu:
        (y5_ref, u5_ref, ds_ref, gw_ref, gb_ref, w1_ref, w2_ref, x_ref, m_ref, g_ref, b_ref,
         x1_ref, h2_ref) = refs
        y = _gelu_tanh(y5_ref[...] + ds_ref[...] * u5_ref[...])
        a2 = y * _sigmoid(_nn(y.astype(BF16), gw_ref[...]) + gb_ref[...])
    else:
        a2_ref, w1_ref, w2_ref, x_ref, m_ref, g_ref, b_ref, x1_ref, h2_ref = refs
        a2 = a2_ref[...]
    heads = []
    for h in range(N_HEADS):
        hs = slice(h * HEAD_DIM, (h + 1) * HEAD_DIM)
        gx = gate_ref[:, hs]
        gate = gx * _sigmoid(gx) if glu else _sigmoid(gx)
        heads.append((_head_norm(hf_ref[:, hs] + hb_ref[:, hs]) * gate).astype(BF16))
    a1 = jnp.concatenate(heads, axis=1)
    y = _nn(a1, w1_ref[...]) + _nn(a2.astype(BF16), w2_ref[...])
    x1 = _ln_rows(ALPHA * x_ref[...] + m_ref[2:3, :] * y, g_ref[...], b_ref[...])
    x1_ref[...] = x1
    h2t = (x1 * (1.0 + m_ref[4:5, :]) + m_ref[3:4, :]).T
    h2_ref[...] = jnp.where(pl.program_id(0) < n_row_tiles, h2t, 0.0).astype(BF16)


def _outproj(scan_parts, a2_parts, w_out_bf, x_all, mods, ln_g, ln_b, n_lat_tiles, glu_params=None):
    t_all, d = x_all.shape
    half = HALF
    n_row_tiles = t_all // ROW_TILE
    t_pad = _padded_tokens(t_all)
    row = lambda i: (jnp.minimum(i, n_row_tiles - 1), 0)
    const = lambda i: (0, 0)
    a2_parts = list(scan_parts) + list(a2_parts)
    a2_specs = [pl.BlockSpec((ROW_TILE, half), lambda i, c=c: (jnp.minimum(i, n_row_tiles - 1), c))
                for _, c in a2_parts]
    a2_args = [a for a, _ in a2_parts]
    if glu_params is not None:
        d_skip, glu_w_bf, glu_b = glu_params
        a2_specs += [pl.BlockSpec((1, half), const), pl.BlockSpec((half, half), const),
                     pl.BlockSpec((1, half), const)]
        a2_args += [d_skip.reshape(1, half), glu_w_bf, glu_b.reshape(1, half)]
    return pl.pallas_call(
        functools.partial(_outproj_kernel, glu=glu_params is not None, n_row_tiles=n_row_tiles),
        grid=(t_pad // ROW_TILE,),
        in_specs=a2_specs + [
            pl.BlockSpec((half, d), const), pl.BlockSpec((half, d), const),
            pl.BlockSpec((ROW_TILE, d), row),
            pl.BlockSpec((None, 8, d), lambda i: (jnp.where(i < n_lat_tiles, 0, 1), 0, 0)),
            pl.BlockSpec((1, d), const), pl.BlockSpec((1, d), const)],
        out_specs=[pl.BlockSpec((ROW_TILE, d), row), pl.BlockSpec((d, ROW_TILE), lambda i: (0, i))],
        out_shape=[jax.ShapeDtypeStruct((t_all, d), F32), jax.ShapeDtypeStruct((d, t_pad), BF16)],
        compiler_params=_cparams(1),
    )(*a2_args, w_out_bf[:half], w_out_bf[half:], x_all, mods, ln_g.reshape(1, d), ln_b.reshape(1, d))


def _ln2_kernel(x_ref, p_ref, m_ref, g_ref, b_ref, *rest):
    x2 = _ln_rows(ALPHA * x_ref[...] + m_ref[5:6, :] * p_ref[...].T, g_ref[...], b_ref[...])
    if len(rest) == 1:
        rest[0][...] = x2
    else:
        mn_ref, o_ref, xm_ref = rest
        o_ref[...] = x2
        xm_ref[...] = _modulated(x2, mn_ref)


def _ln2(x1, peer_out_t, mods, ln_g, ln_b, n_lat_tiles, mods_next=None):
    t_all, d = x1.shape
    row = pl.BlockSpec((ROW_TILE, d), lambda i: (i, 0))
    col = pl.BlockSpec((d, ROW_TILE), lambda i: (0, i))
    vec = pl.BlockSpec((1, d), lambda i: (0, 0))
    nxt = mods_next is not None
    n_rows = t_all if nxt else n_lat_tiles * ROW_TILE
    return pl.pallas_call(
        _ln2_kernel,
        grid=(n_rows // ROW_TILE,),
        in_specs=[row, col, _mods_spec(d, n_lat_tiles), vec, vec] + ([_mods_spec(d, n_lat_tiles)] if nxt else []),
        out_specs=[row, row] if nxt else row,
        out_shape=([jax.ShapeDtypeStruct((t_all, d), F32), jax.ShapeDtypeStruct((t_all, d), BF16)] if nxt
                   else jax.ShapeDtypeStruct((n_rows, d), F32)),
        compiler_params=_cparams(1),
    )(x1, peer_out_t, mods, ln_g.reshape(1, d), ln_b.reshape(1, d), *([mods_next] if nxt else []))


def _scan_block(rev, n_chunks, n_lat_chunks):
    if rev:
        return lambda s: n_chunks - 1 - s
    n_ctx = n_chunks - n_lat_chunks
    return lambda s: jnp.where(s < n_ctx, s + n_lat_chunks, s - n_ctx)


def _scan_mask(rev, n):
    row = lax.broadcasted_iota(jnp.int32, (n, n), 0)
    col = lax.broadcasted_iota(jnp.int32, (n, n), 1)
    return (col >= row) if rev else (col <= row)


def _rope(x, cos, sin):
    lane = lax.broadcasted_iota(jnp.int32, x.shape, 1)
    partner = jnp.where((lane % 64) < 32, pltpu.roll(x, 96, 1), pltpu.roll(x, 32, 1))
    return x * cos + partner * sin


def _mlstm_kernel(*refs):
    n_in = 5
    fw_in, bw_in = refs[:n_in], refs[n_in:2 * n_in]
    bias_ref, biast_ref, out_f, out_b = refs[2 * n_in:2 * n_in + 4]
    state = refs[2 * n_in + 4:]

    @pl.when(pl.program_id(0) == 0)
    def _():
        for r in state:
            r[...] = jnp.zeros_like(r)

    _mlstm_chunk(*fw_in, bias_ref, biast_ref, out_f, *state[:2], rev=False)
    _mlstm_chunk(*bw_in, bias_ref, biast_ref, out_b, *state[2:], rev=True)


def _mlstm_chunk(q_ref, k_ref, v_ref, g_ref, gt_ref, bias_ref, biast_ref, out_ref, cn_ref, m_ref, *, rev):
    n = q_ref.shape[0]
    mask = _scan_mask(rev, n)
    tri = mask.astype(BF16)
    pre_c = g_ref[...] + bias_ref[...]
    pre_r = gt_ref[...] + biast_ref[...]
    b_c = sum(_nn(tri, p) for p in _split3(_log_sigmoid(pre_c)))
    b_r = sum(_nt(p, tri) for p in _split3(_log_sigmoid(pre_r)))
    ones = jnp.ones((n, HEAD_DIM), BF16)
    last = 0 if rev else n - 1
    off = N_HEADS if rev else 0

    for h in range(N_HEADS):
        hs = slice(h * HEAD_DIM, (h + 1) * HEAD_DIM)
        ci, cf = off + h, 2 * N_HEADS + off + h
        bc, br = b_c[:, cf:cf + 1], b_r[cf:cf + 1, :]
        ic, ir = pre_c[:, ci:ci + 1], pre_r[ci:ci + 1, :]
        m_prev = m_ref[h:h + 1, 0:1]
        logw = jnp.where(mask, bc - br + ir, NEG)
        inter = bc + m_prev
        m_t = jnp.maximum(inter, jnp.max(logw, axis=1, keepdims=True))
        w_inter = jnp.exp(inter - m_t)
        qb, kb = q_ref[:, hs], k_ref[:, hs]
        v_ext = jnp.concatenate([v_ref[:, hs].astype(BF16), ones], axis=1)
        s = _nt(qb, kb) * jnp.exp(logw - m_t)
        cn_old = cn_ref[h]
        both = _nn(s.astype(BF16), v_ext) + _nn(qb, cn_old.astype(BF16)) * w_inter
        den = jnp.maximum(jnp.abs(both[:, HEAD_DIM:]), jnp.exp(-m_t))
        out_ref[:, hs] = both[:, :HEAD_DIM] / den
        m_new = m_t[last:last + 1, :]
        b_last = bc[last:last + 1, :]
        w_old = jnp.exp(b_last + m_prev - m_new)
        kw = kb.astype(F32) * jnp.exp(b_last - bc + ic - m_new)
        cn_ref[h] = w_old * cn_old + _tn(kw.astype(BF16), v_ext)
        m_ref[h:h + 1, :] = jnp.broadcast_to(m_new, (1, HEAD_DIM))


def _rope_kernel(q_ref, k_ref, cos_ref, sin_ref, qo_ref, ko_ref):
    cos, sin = cos_ref[...], sin_ref[...]
    for h in range(N_HEADS):
        hs = slice(h * HEAD_DIM, (h + 1) * HEAD_DIM)
        qo_ref[:, hs] = (_rope(q_ref[:, hs], cos, sin) * QK_SCALE).astype(BF16)
        ko_ref[:, hs] = _rope(k_ref[:, hs], cos, sin).astype(BF16)


def _rope_qk(proj, cos_t, sin_t):
    t_all = proj.shape[0]
    tok = pl.BlockSpec((ROW_TILE, 128), lambda i: (i, 0))
    return pl.pallas_call(
        _rope_kernel,
        grid=(t_all // ROW_TILE,),
        in_specs=[pl.BlockSpec((ROW_TILE, HALF), lambda i: (i, 0)),
                  pl.BlockSpec((ROW_TILE, HALF), lambda i: (i, 1)), tok, tok],
        out_specs=[pl.BlockSpec((ROW_TILE, HALF), lambda i: (i, 0))] * 2,
        out_shape=[jax.ShapeDtypeStruct((t_all, HALF), BF16)] * 2,
        compiler_params=_cparams(1),
    )(proj, proj, cos_t, sin_t)


def _mlstm(proj, gates, gates_t, cos_t, sin_t, igate_b, fgate_b, n_lat_chunks):
    t_all = proj.shape[0]
    n = SCAN_CHUNK
    n_chunks = t_all // n
    bias = jnp.zeros((1, 128), F32).at[0, :4 * N_HEADS].set(
        jnp.concatenate([igate_b.reshape(-1), fgate_b.reshape(-1)]))
    q_rot, k_rot = _rope_qk(proj, cos_t, sin_t)

    def dir_specs(rev):
        blk = _scan_block(rev, n_chunks, n_lat_chunks)
        colblk = lambda c: pl.BlockSpec((n, HALF), lambda s, c=c: (blk(s), c))
        ins = [colblk(0), colblk(0), colblk(2), pl.BlockSpec((n, 128), lambda s: (blk(s), 0)),
               pl.BlockSpec((128, n), lambda s: (0, blk(s)))]
        return ins, pl.BlockSpec((n, HALF), lambda s: (blk(s), 0))

    (in_f, out_f), (in_b, out_b) = dir_specs(False), dir_specs(True)
    dir_args = [q_rot, k_rot, proj, gates, gates_t]
    state = [pltpu.VMEM((N_HEADS, HEAD_DIM, 2 * HEAD_DIM), F32), pltpu.VMEM((N_HEADS, HEAD_DIM), F32)]
    return pl.pallas_call(
        _mlstm_kernel,
        grid=(n_chunks,),
        in_specs=in_f + in_b + [pl.BlockSpec((1, 128), lambda s: (0, 0)), pl.BlockSpec((128, 1), lambda s: (0, 0))],
        out_specs=[out_f, out_b],
        out_shape=[jax.ShapeDtypeStruct((t_all, HALF), F32)] * 2,
        scratch_shapes=state + state,
        compiler_params=_cparams(1),
    )(*dir_args, *dir_args, bias, bias.reshape(128, 1))


def _by_head(x):
    return jnp.stack([x[:, h * HEAD_DIM:(h + 1) * HEAD_DIM] for h in range(x.shape[1] // HEAD_DIM)])


def _bqk(a, b):
    return jnp.einsum('hqd,hkd->hqk', a, b, preferred_element_type=F32)


def _bqd(p, v):
    return jnp.einsum('hqk,hkd->hqd', p, v, preferred_element_type=F32)


def _na_kernel(q_ref, kp_ref, kc_ref, kn_ref, vp_ref, vc_ref, vn_ref, kx_ref, vx_ref, bias_ref, o_ref):
    qb = _by_head(q_ref[...] * QK_SCALE).astype(BF16)
    kk = _by_head(jnp.concatenate([kp_ref[...], kc_ref[...], kn_ref[...]], axis=0)).astype(BF16)
    vv = _by_head(jnp.concatenate([vp_ref[...], vc_ref[...], vn_ref[...]], axis=0)).astype(BF16)
    s_loc = _bqk(qb, kk) + bias_ref[...]
    s_ctx = _bqk(qb, _by_head(kx_ref[...]).astype(BF16))
    m = jnp.maximum(jnp.max(s_loc, axis=2, keepdims=True), jnp.max(s_ctx, axis=2, keepdims=True))
    p_loc = jnp.exp(s_loc - m)
    p_ctx = jnp.exp(s_ctx - m)
    l = jnp.sum(p_loc, axis=2, keepdims=True) + jnp.sum(p_ctx, axis=2, keepdims=True)
    out = (_bqd(p_loc.astype(BF16), vv) + _bqd(p_ctx.astype(BF16), _by_head(vx_ref[...]).astype(BF16))) / l
    for h in range(out.shape[0]):
        o_ref[:, h * HEAD_DIM:(h + 1) * HEAD_DIM] = out[h]


def _na_bias(rpb, rows):
    qn = NA_QROWS * GRID_W
    kn = 2 * qn
    ql = np.arange(qn)
    kl = np.arange(kn)
    out = []
    nqb = rows // NA_QROWS
    lr, krl = np.arange(NA_QROWS)[:, None], np.arange(2 * NA_QROWS)[None, :]
    dr = np.clip(krl - NA_QROWS // 2 - lr + NA_WIN_ROWS - 1, 0, 2 * NA_WIN_ROWS - 2)
    col = np.arange(GRID_W)
    dc = np.clip(col[None, :] - col[:, None] + NA_WIN_COLS - 1, 0, 2 * NA_WIN_COLS - 2)
    oh_r = jnp.asarray(dr[..., None] == np.arange(2 * NA_WIN_ROWS - 1), F32)
    oh_c = jnp.asarray(dc[..., None] == np.arange(2 * NA_WIN_COLS - 1), F32)
    table = jnp.einsum('hab,lka,qcb->hlqkc', rpb, oh_r, oh_c,
                       precision=lax.Precision.HIGHEST).reshape(rpb.shape[0], qn, kn)
    for qb in (0, 1, nqb - 1):
        qr = NA_QROWS * qb + ql // GRID_W
        qc = ql % GRID_W
        kr = NA_QROWS * qb - NA_QROWS // 2 + kl // GRID_W
        kc = kl % GRID_W
        r0 = np.clip(qr - NA_WIN_ROWS // 2, 0, rows - NA_WIN_ROWS)
        c0 = np.clip(qc - NA_WIN_COLS // 2, 0, GRID_W - NA_WIN_COLS)
        ok = ((kr[None, :] >= r0[:, None]) & (kr[None, :] < r0[:, None] + NA_WIN_ROWS)
              & (kc[None, :] >= c0[:, None]) & (kc[None, :] < c0[:, None] + NA_WIN_COLS)
              & (kr[None, :] >= 0) & (kr[None, :] < rows))
        out.append(jnp.where(jnp.asarray(ok)[None], table, NEG))
    return jnp.stack(out)


def _na(proj, rpb, t_lat, t_ctx):
    rows = t_lat // GRID_W
    qn = NA_QROWS * GRID_W
    hn = qn // 2
    nqb = rows // NA_QROWS
    assert nqb >= 2 and t_ctx == hn
    bias = _na_bias(rpb.astype(F32), rows)
    last_half = 2 * nqb - 1
    ctx_blk = t_lat // hn
    cls = lambda b: jnp.where(b == 0, 0, jnp.where(b == nqb - 1, 2, 1))

    nh = NA_HEADS
    w = nh * HEAD_DIM
    groups = N_HEADS // nh

    def kv_specs(col0):
        return [pl.BlockSpec((hn, w), lambda h, b: (jnp.maximum(2 * b - 1, 0), col0 + h)),
                pl.BlockSpec((qn, w), lambda h, b: (b, col0 + h)),
                pl.BlockSpec((hn, w), lambda h, b: (jnp.minimum(2 * b + 2, last_half), col0 + h))]

    qcol, kcol, vcol = 4 * groups, 5 * groups, 6 * groups
    return pl.pallas_call(
        _na_kernel,
        grid=(groups, nqb),
        in_specs=[pl.BlockSpec((qn, w), lambda h, b: (b, qcol + h))]
        + kv_specs(kcol) + kv_specs(vcol)
        + [pl.BlockSpec((hn, w), lambda h, b: (ctx_blk, kcol + h)),
           pl.BlockSpec((hn, w), lambda h, b: (ctx_blk, vcol + h)),
           pl.BlockSpec((None, nh, qn, 2 * qn), lambda h, b: (cls(b), h, 0, 0))],
        out_specs=pl.BlockSpec((qn, w), lambda h, b: (b, h)),
        out_shape=jax.ShapeDtypeStruct((t_lat, HALF), F32),
        compiler_params=_cparams(2),
    )(proj, proj, proj, proj, proj, proj, proj, proj, proj, bias)


def _ctx_attn_kernel(q_ref, k_ref, v_ref, o_ref):
    qb = (q_ref[...] * QK_SCALE).astype(BF16)
    s = _nt(qb, k_ref[...].astype(BF16))
    p = jnp.exp(s - jnp.max(s, axis=1, keepdims=True))
    o_ref[...] = _nn(p.astype(BF16), v_ref[...].astype(BF16)) / jnp.sum(p, axis=1, keepdims=True)


def _ctx_attn(proj, t_lat, t_ctx):
    blk = t_lat // t_ctx
    spec = lambda c: pl.BlockSpec((t_ctx, HEAD_DIM), lambda h, c=c: (blk, c * N_HEADS + h))
    return pl.pallas_call(
        _ctx_attn_kernel,
        grid=(N_HEADS,),
        in_specs=[spec(4), spec(5), spec(6)],
        out_specs=pl.BlockSpec((t_ctx, HEAD_DIM), lambda h: (0, h)),
        out_shape=jax.ShapeDtypeStruct((t_ctx, HALF), F32),
        compiler_params=_cparams(1),
    )(proj, proj, proj)


def _gla_inter(q_ref, i_ref, f_ref, lb_ref, out_ref, st_ref, k_scr, b_scr, *, rev):
    n = q_ref.shape[0]
    mask = _scan_mask(rev, n)
    tri = mask.astype(BF16)
    last = 0 if rev else n - 1
    x = f_ref[...]
    key = lb_ref[0:1, :] * _sigmoid(-x)
    la, lc = lb_ref[1:2, :], lb_ref[2:3, :] + _log_sigmoid(x)
    logf = jnp.maximum(la, lc) + jnp.log1p(jnp.exp(-jnp.abs(la - lc)))
    b = sum(_nn(tri, p) for p in _split3(logf))
    b_end = b[last:last + 1, :]
    qx = q_ref[...]
    q = qx * _sigmoid(qx)
    qt = q * jnp.exp(b)
    kdec = key * jnp.exp(b_end - b)
    w_end = jnp.exp(b_end)
    k_scr[...] = key
    b_scr[...] = b
    for h in range(N_HEADS):
        hs = slice(h * HEAD_DIM, (h + 1) * HEAD_DIM)
        st = st_ref[h]
        vb = i_ref[:, hs].astype(BF16)
        out_ref[:, hs] = _nt(qt[:, hs].astype(BF16), st.astype(BF16))
        st_ref[h] = st * w_end[:, hs] + _tn(vb, kdec[:, hs].astype(BF16))
    return mask, q, b_end


def _gla_intra_factored(mask, q, b_end, i_ref, out_ref, k_scr, b_scr):
    mid = 0.5 * b_end
    qm = q * jnp.exp(b_scr[...] - mid)
    km = k_scr[...] * jnp.exp(mid - b_scr[...])
    for h in range(N_HEADS):
        hs = slice(h * HEAD_DIM, (h + 1) * HEAD_DIM)
        att = jnp.where(mask, _nt(qm[:, hs].astype(BF16), km[:, hs].astype(BF16)), 0.0)
        out_ref[:, hs] += _nn(att.astype(BF16), i_ref[:, hs].astype(BF16))


def _gla_intra_exact(q, i_ref, out_ref, k_scr, b_scr, *, rev):
    n = q.shape[0]
    t_idx = lax.broadcasted_iota(jnp.int32, (n, 1), 0)
    for h in range(N_HEADS):
        hs = slice(h * HEAD_DIM, (h + 1) * HEAD_DIM)
        q_h = q[:, hs]
        b_h = b_scr[:, hs]

        def body(grp, acc):
            base = pl.multiple_of(grp * 8, 8)
            b8, k8, v8 = b_scr[pl.ds(base, 8), hs], k_scr[pl.ds(base, 8), hs], i_ref[pl.ds(base, 8), hs]
            for r in range(8):
                s = base + r
                e = jnp.exp(jnp.minimum(b_h - b8[r:r + 1, :], 0.0))
                a = jnp.sum(q_h * k8[r:r + 1, :] * e, axis=1, keepdims=True)
                a = jnp.where((t_idx <= s) if rev else (t_idx >= s), a, 0.0)
                acc = acc + a * v8[r:r + 1, :]
            return acc

        out_ref[:, hs] += lax.fori_loop(0, n // 8, body, jnp.zeros((n, HEAD_DIM), F32))


def _gla_kernel(*refs):
    fw_in, bw_in, lb_ref = refs[0:3], refs[3:6], refs[6]
    out_f, out_b = refs[7:9]
    st_f, k_f, b_f, st_b, k_b, b_b = refs[9:]

    @pl.when(pl.program_id(0) == 0)
    def _():
        st_f[...] = jnp.zeros_like(st_f)
        st_b[...] = jnp.zeros_like(st_b)

    mask_f, q_f, end_f = _gla_inter(*fw_in, lb_ref, out_f, st_f, k_f, b_f, rev=False)
    mask_b, q_b, end_b = _gla_inter(*bw_in, lb_ref, out_b, st_b, k_b, b_b, rev=True)
    safe = jnp.minimum(jnp.min(end_f), jnp.min(end_b)) >= 2.0 * GLA_SAFE_LOG

    @pl.when(safe)
    def _():
        _gla_intra_factored(mask_f, q_f, end_f, fw_in[1], out_f, k_f, b_f)
        _gla_intra_factored(mask_b, q_b, end_b, bw_in[1], out_b, k_b, b_b)

    @pl.when(jnp.logical_not(safe))
    def _():
        _gla_intra_exact(q_f, fw_in[1], out_f, k_f, b_f, rev=False)
        _gla_intra_exact(q_b, bw_in[1], out_b, k_b, b_b, rev=True)


def _gla(proj, lb, n_lat_chunks):
    t_all = proj.shape[0]
    n = GLA_CHUNK
    n_chunks = t_all // n

    def dir_specs(rev):
        blk = _scan_block(rev, n_chunks, n_lat_chunks)
        colblk = lambda c: pl.BlockSpec((n, HALF), lambda s, c=c: (blk(s), c))
        return [colblk(0), colblk(1), colblk(3 if rev else 2)], pl.BlockSpec((n, HALF), lambda s: (blk(s), 0))

    (in_f, out_f), (in_b, out_b) = dir_specs(False), dir_specs(True)
    state = [pltpu.VMEM((N_HEADS, HEAD_DIM, HEAD_DIM), F32), pltpu.VMEM((n, HALF), F32),
             pltpu.VMEM((n, HALF), F32)]
    lb = lb.reshape(1, HALF)
    lb_rows = jnp.concatenate([1.0 - lb, jnp.log(lb), jnp.log1p(-lb)], axis=0)
    return pl.pallas_call(
        _gla_kernel,
        grid=(n_chunks,),
        in_specs=in_f + in_b + [pl.BlockSpec((3, HALF), lambda s: (0, 0))],
        out_specs=[out_f, out_b],
        out_shape=[jax.ShapeDtypeStruct((t_all, HALF), F32)] * 2,
        scratch_shapes=state + state,
        compiler_params=_cparams(1),
    )(*([proj] * 6), lb_rows)


def _s5_mats(a_re, a_im, log_dt, b_re, b_im, c_re, c_im):
    hp = lax.Precision.HIGHEST
    n = S5_CHUNK
    dt = jnp.exp(log_dt)[..., None]
    la_re, la_im = a_re * dt, a_im * dt
    mag = jnp.exp(la_re)
    ab_re, ab_im = mag * jnp.cos(la_im), mag * jnp.sin(la_im)
    nr, ni = ab_re - 1.0, ab_im
    den = jnp.square(a_re) + jnp.square(a_im)
    cr = (nr * a_re + ni * a_im) / den
    ci = (ni * a_re - nr * a_im) / den
    bb_re = cr[..., None] * b_re - ci[..., None] * b_im
    bb_im = cr[..., None] * b_im + ci[..., None] * b_re

    def apow(tau):
        tau = jnp.asarray(tau, F32)[None, None, :, None]
        m = jnp.exp(tau * la_re[:, :, None, :])
        return m * jnp.cos(tau * la_im[:, :, None, :]), m * jnp.sin(tau * la_im[:, :, None, :])

    def c_apow(tau):
        pr, pi = apow(tau)
        return (c_re[:, :, None] * pr[:, :, :, None] - c_im[:, :, None] * pi[:, :, :, None],
                c_re[:, :, None] * pi[:, :, :, None] + c_im[:, :, None] * pr[:, :, :, None])

    ca_re, ca_im = c_apow(np.arange(n))
    kern = jnp.einsum('dgtip,dgpj->dgtij', jnp.concatenate([ca_re, -ca_im], axis=-1),
                      jnp.concatenate([bb_re, bb_im], axis=2), precision=hp)
    s_idx, t_idx = np.arange(n)[:, None], np.arange(n)[None, :]
    g = a_re.shape[1]
    lp = n * S5_GROUP

    def toeplitz(kd, lag):
        onehot = jnp.asarray(lag[None] == np.arange(n)[:, None, None], F32)
        return jnp.einsum('lst,glij->gsjti', onehot, kd, precision=hp).reshape(g, lp, lp)

    kmat = toeplitz(kern[0], t_idx - s_idx) + toeplitz(kern[1], s_idx - t_idx)

    def emat(d, tau):
        pr, pi = apow(tau)
        pr, pi = pr[d][:, :, None, :], pi[d][:, :, None, :]
        br, bi = bb_re[d].transpose(0, 2, 1)[:, None], bb_im[d].transpose(0, 2, 1)[:, None]
        return jnp.concatenate([pr * br - pi * bi, pr * bi + pi * br], axis=-1).reshape(g, lp, 2 * S5_STATE)

    def fmat(d, tau):
        fr, fi = c_apow(tau)
        fr, fi = fr[d], fi[d]
        return jnp.concatenate([fr, -fi], axis=-1).transpose(0, 3, 1, 2).reshape(g, 2 * S5_STATE, lp)

    e_all = jnp.concatenate([emat(0, n - 1 - np.arange(n)), emat(1, np.arange(n))], axis=-1)
    f_all = jnp.concatenate([fmat(0, np.arange(n) + 1), fmat(1, n - np.arange(n))], axis=1)
    pr, pi = apow(np.array([n]))
    pr, pi = pr[:, :, 0], pi[:, :, 0]
    ar2 = jnp.concatenate([pr[0], pr[0], pr[1], pr[1]], axis=-1)
    ai2 = jnp.concatenate([-pi[0], pi[0], -pi[1], pi[1]], axis=-1)
    return kmat.astype(BF16), e_all.astype(BF16), f_all.astype(BF16), ar2, ai2


def _s5_state_kernel(u_ref, e_ref, s_ref):
    s_ref[...] = _nn(u_ref[...], e_ref[...])


def _s5_scan_kernel(sf_ref, sb_ref, ar_ref, ai_ref, hf_ref, hb_ref, stf_ref, stb_ref):
    nb = sf_ref.shape[0]
    w = 2 * S5_STATE
    ar, ai = ar_ref[...], ai_ref[...]

    @pl.when(pl.program_id(0) == 0)
    def _():
        stf_ref[...] = jnp.zeros_like(stf_ref)
        stb_ref[...] = jnp.zeros_like(stb_ref)

    def step(h, s, lo):
        return ar[:, lo:lo + w] * h + ai[:, lo:lo + w] * pltpu.roll(h, S5_STATE, 1) + s

    hf, hb = stf_ref[...], stb_ref[...]
    for j in range(nb):
        hf_ref[j] = hf
        hf = step(hf, sf_ref[j], 0)
        hb_ref[nb - 1 - j] = hb
        hb = step(hb, sb_ref[nb - 1 - j], w)
    stf_ref[...] = hf
    stb_ref[...] = hb


def _s5_out_kernel(u_ref, k_ref, h_ref, f_ref, y_ref):
    y_ref[...] = _nn(u_ref[...], k_ref[...]) + _nn(h_ref[...].astype(BF16), f_ref[...])


def _s5(u, mats, n_lat_chunks):
    kmat, emat, fmat, ar2, ai2 = mats
    t_all = u.shape[0]
    n = S5_CHUNK
    nc = t_all // n
    g = S5_GROUPS
    lp = n * S5_GROUP
    w4 = 4 * S5_STATE
    ug = u.astype(BF16).reshape(nc, n, g, S5_GROUP).transpose(2, 0, 1, 3).reshape(g, nc, lp)
    grp = lambda a, b: pl.BlockSpec((None, a, b), lambda i: (i, 0, 0))
    s_end = pl.pallas_call(
        _s5_state_kernel, grid=(g,),
        in_specs=[grp(nc, lp), grp(lp, w4)], out_specs=grp(nc, w4),
        out_shape=jax.ShapeDtypeStruct((g, nc, w4), F32), compiler_params=_cparams(1),
    )(ug, emat)
    nb = S5_SCAN_BLOCK
    w2 = 2 * S5_STATE
    fw_blk = _scan_block(False, nc // nb, n_lat_chunks // nb)
    bw_blk = _scan_block(True, nc // nb, n_lat_chunks // nb)
    s_t = s_end.transpose(1, 0, 2)
    coef = pl.BlockSpec((g, w4), lambda i: (0, 0))
    h_fw, h_bw = pl.pallas_call(
        _s5_scan_kernel, grid=(nc // nb,),
        in_specs=[pl.BlockSpec((nb, g, w2), lambda i: (fw_blk(i), 0, 0)),
                  pl.BlockSpec((nb, g, w2), lambda i: (bw_blk(i), 0, 1)), coef, coef],
        out_specs=[pl.BlockSpec((nb, g, w2), lambda i: (fw_blk(i), 0, 0)),
                   pl.BlockSpec((nb, g, w2), lambda i: (bw_blk(i), 0, 0))],
        out_shape=[jax.ShapeDtypeStruct((nc, g, w2), F32)] * 2,
        scratch_shapes=[pltpu.VMEM((g, w2), F32)] * 2,
        compiler_params=_cparams(1),
    )(s_t, s_t, ar2, ai2)
    h_in = jnp.concatenate([h_fw, h_bw], axis=-1).transpose(1, 0, 2)
    y = pl.pallas_call(
        _s5_out_kernel, grid=(g,),
        in_specs=[grp(nc, lp), grp(lp, lp), grp(nc, w4), grp(w4, lp)], out_specs=grp(nc, lp),
        out_shape=jax.ShapeDtypeStruct((g, nc, lp), F32), compiler_params=_cparams(1),
    )(ug, kmat, h_in, fmat)
    return y.reshape(g, nc, n, S5_GROUP).transpose(1, 2, 0, 3).reshape(t_all, HALF)


def _top_rows(x, k):
    rows = []
    for _ in range(k):
        mx = jnp.max(x, axis=0, keepdims=True)
        rows.append(mx)
        x = jnp.where(x == mx, NEG, x)
    return rows


def _top_rows_sorted(x, k):
    n = x.shape[0] // 8
    cols = [x[8 * i:8 * (i + 1), :] for i in range(n)]
    size = 2
    while size <= n:
        stride = size // 2
        while stride >= 1:
            for i in range(n):
                j = i ^ stride
                if j > i:
                    hi, lo = jnp.maximum(cols[i], cols[j]), jnp.minimum(cols[i], cols[j])
                    cols[i], cols[j] = (hi, lo) if (i & size) == 0 else (lo, hi)
            stride //= 2
        size *= 2
    rows = []
    for r in range(k):
        mx = jnp.max(cols[0], axis=0, keepdims=True)
        rows.append(mx)
        popped = cols[0] == mx
        live = min(n, k - r)
        for i in range(live):
            nxt = cols[i + 1] if i + 1 < n else NEG
            cols[i] = jnp.where(popped, nxt, cols[i])
    return rows


def _peer_route_kernel(ht_ref, wq_ref, keys_ref, e1_ref, e0_ref, th_ref, top_scr):
    kk = PEER_TOPK
    qt = _nn(wq_ref[...], ht_ref[...])
    for hp in range(2 * PEER_HEADS):
        sc = _nn(keys_ref[hp], qt[hp * HEAD_DIM:(hp + 1) * HEAD_DIM, :].astype(BF16))
        h, p = divmod(hp, 2)
        if p == 0:
            e0_ref[h] = sc
        else:
            e1_ref[h] = sc
        for r, mx in enumerate(_top_rows_sorted(sc, kk + 1)):
            top_scr[hp, r:r + 1, :] = mx
        top_scr[hp, kk + 1:, :] = jnp.full((PEER_TOP_ROWS - kk - 1, sc.shape[1]), NEG, F32)

    row = lax.broadcasted_iota(jnp.int32, (PEER_TOP_ROWS, 1), 0)
    for h in range(PEER_HEADS):
        a = top_scr[2 * h]
        b = top_scr[2 * h + 1]
        parts = [a[0:1] + b, a[1:2] + b[0:8], a[2:3] + b[0:8], a[3:4] + b[0:8]]
        parts += [jnp.where(row >= 4, a + b[j:j + 1], NEG) for j in range(3)]
        cand = jnp.maximum(jnp.concatenate(parts, axis=0), NEG)
        top = _top_rows(cand, kk + 1)
        thr = 0.5 * (top[kk - 1] + top[kk])
        z = jnp.sum(jnp.where(cand >= thr, jnp.exp(cand - top[0]), 0.0), axis=0, keepdims=True)
        e0_ref[h] = jnp.exp(e0_ref[h] - a[0:1]) / z
        e1_ref[h] = jnp.exp(e1_ref[h] - b[0:1])
        th_ref[h:h + 1, :] = jnp.exp(thr - top[0]) / z


def _peer_expert_kernel(ht_ref, e1_ref, e0_ref, th_ref, *rest, n_tiles):
    u_refs = rest[:PEER_U_SPLIT]
    vt_ref, o_ref, at0_scr, at1_scr, g0_scr, g1_scr = rest[PEER_U_SPLIT:]
    s = pl.program_id(1)
    assert n_tiles % 2 == 0 and n_tiles >= 4
    ni = u_refs[0].shape[0] // PEER_N_KEYS
    kc = u_refs[0].shape[1]
    at_scr, g_scr = (at0_scr, at1_scr), (g0_scr, g1_scr)

    def stages(par, dot1, gate, dot2, first=False):
        if dot1:
            at_scr[par][...] = _nn(jnp.concatenate([u_k[...] for u_k in u_refs], axis=1), ht_ref[...])
        if gate:
            at_old, g_new = at_scr[1 - par], g_scr[1 - par]
            nj, nc = PEER_GATE_ROWS, 128
            for ii in range(ni):
                i = (s - 1) * ni + ii
                e0_rows = [e0_ref[h, pl.ds(i, 1), :] for h in range(PEER_HEADS)]
                for c in range(at_old.shape[1] // nc):
                    cs = slice(c * nc, (c + 1) * nc)
                    for jb in range(PEER_N_KEYS // nj):
                        js = slice(jb * nj, (jb + 1) * nj)
                        w = None
                        for h in range(PEER_HEADS):
                            p = e1_ref[h, js, cs] * e0_rows[h][:, cs]
                            t = jnp.where(p >= th_ref[h:h + 1, cs], p, 0.0)
                            w = t if w is None else w + t
                        rs = slice(ii * PEER_N_KEYS + jb * nj, ii * PEER_N_KEYS + (jb + 1) * nj)
                        g_new[rs, cs] = (_gelu_tanh(at_old[rs, cs]) * w).astype(BF16)
        if dot2:
            prod = _nn(vt_ref[...], g_scr[par][...])
            if first:
                o_ref[...] = prod
            else:
                o_ref[...] += prod

    mid = jnp.logical_and(s > 2, s < n_tiles)
    pl.when(s == 0)(lambda: stages(0, True, False, False))
    pl.when(s == 1)(lambda: stages(1, True, True, False))
    pl.when(s == 2)(lambda: stages(0, True, True, True, first=True))
    pl.when(jnp.logical_and(mid, s % 2 == 0))(lambda: stages(0, True, True, True))
    pl.when(jnp.logical_and(mid, s % 2 == 1))(lambda: stages(1, True, True, True))
    pl.when(s == n_tiles)(lambda: stages(0, False, True, True))
    pl.when(s == n_tiles + 1)(lambda: stages(1, False, False, True))


def _tok3(t):
    return pl.BlockSpec((PEER_HEADS, PEER_N_KEYS, t), lambda i, *_: (0, 0, i))


def _peer_route(h2t, wq_t_bf, keys_bf):
    d, t_all = h2t.shape
    nh, nk = PEER_HEADS, PEER_N_KEYS
    t1 = PEER_P1_TILE
    return pl.pallas_call(
        _peer_route_kernel,
        grid=(t_all // t1,),
        in_specs=[pl.BlockSpec((d, t1), lambda i: (0, i)),
                  pl.BlockSpec(wq_t_bf.shape, lambda i: (0, 0)),
                  pl.BlockSpec(keys_bf.shape, lambda i: (0, 0, 0))],
        out_specs=[_tok3(t1), _tok3(t1), pl.BlockSpec((nh, t1), lambda i: (0, i))],
        out_shape=[jax.ShapeDtypeStruct((nh, nk, t_all), F32)] * 2 + [jax.ShapeDtypeStruct((nh, t_all), F32)],
        scratch_shapes=[pltpu.VMEM((2 * nh, PEER_TOP_ROWS, t1), F32)],
        compiler_params=_cparams(1),
    )(h2t, wq_t_bf, keys_bf)


def _padded_tokens(t_all):
    return t_all + (-t_all % PEER_TOK_TILE)


def _peer(h2t, wq_t_bf, keys_bf, u_r, vt_r, layer=0):
    return _peer_experts(h2t, *_peer_route(h2t, wq_t_bf, keys_bf), u_r, vt_r, layer)


def _peer_tables(u_all, v_all):
    return u_all.astype(BF16), v_all.transpose(0, 2, 1).astype(BF16)


def _peer_experts(h2t, e1, e0, th, u_r, vt_r, layer):
    d, t_all = h2t.shape
    tt = PEER_TOK_TILE
    et = PEER_EXP_TILE
    n_tiles = u_r.shape[1] // et
    kc = d // PEER_U_SPLIT
    tok = pl.BlockSpec((d, tt), lambda i, s: (0, i))
    return pl.pallas_call(
        functools.partial(_peer_expert_kernel, n_tiles=n_tiles),
        grid=(t_all // tt, n_tiles + 2),
        in_specs=[tok, _tok3(tt), _tok3(tt), pl.BlockSpec((PEER_HEADS, tt), lambda i, s: (0, i))]
        + [pl.BlockSpec((None, et, kc), lambda i, s, k=k: (layer, jnp.minimum(s, n_tiles - 1), k))
           for k in range(PEER_U_SPLIT)]
        + [pl.BlockSpec((None, d, et), lambda i, s: (layer, 0, jnp.clip(s - 2, 0, n_tiles - 1)))],
        out_specs=tok,
        out_shape=jax.ShapeDtypeStruct((d, t_all), F32),
        scratch_shapes=[pltpu.VMEM((et, tt), F32)] * 2 + [pltpu.VMEM((et, tt), BF16)] * 2,
        compiler_params=_cparams(2),
    )(h2t, e1, e0, th, *([u_r] * PEER_U_SPLIT), vt_r)


def _rope_tables(t_lat, t_ctx):
    pos = np.arange(t_lat)
    nfreq = HEAD_DIM // 4
    freqs = ROPE_BASE ** (-jnp.arange(nfreq, dtype=F32) / nfreq)
    ang_r = jnp.asarray(pos // GRID_W, F32)[:, None] * freqs
    ang_c = jnp.asarray(pos % GRID_W, F32)[:, None] * freqs
    cos = jnp.concatenate([jnp.cos(ang_r), jnp.cos(ang_r), jnp.cos(ang_c), jnp.cos(ang_c)], axis=1)
    sin = jnp.concatenate([-jnp.sin(ang_r), jnp.sin(ang_r), -jnp.sin(ang_c), jnp.sin(ang_c)], axis=1)
    cos = jnp.concatenate([cos, jnp.ones((t_ctx, HEAD_DIM), F32)], axis=0)
    sin = jnp.concatenate([sin, jnp.zeros((t_ctx, HEAD_DIM), F32)], axis=0)
    return cos, sin


def _peer_and_norm(x1, h2t, mods, ln_g, ln_b, n_lat_tiles, w_q, sub_keys, tables, layer, mods_next=None):
    keys = sub_keys.reshape(2 * PEER_HEADS, PEER_N_KEYS, -1).astype(BF16)
    pt = _peer(h2t, w_q.T.astype(BF16), keys, *tables, layer)
    return _ln2(x1, pt, mods, ln_g, ln_b, n_lat_tiles, mods_next)


def _layer_ab(x_all, mods, t_lat, t_ctx, w_in, w_out, igate_b, fgate_b, rpb, ln_g, ln_b, xm=None):
    n_lat_tiles = t_lat // ROW_TILE
    if xm is None:
        xm = _modulate(x_all, mods, n_lat_tiles)
    ng = 4 * N_HEADS
    w_main = jnp.concatenate([w_in[:, :4 * HALF], w_in[:, 4 * HALF + ng:]], axis=1).astype(BF16)
    w_gate = jnp.pad(w_in[:, 4 * HALF:4 * HALF + ng], ((0, 0), (0, 128 - ng))).astype(BF16)
    proj = _inproj(xm, w_main)
    gates = _inproj(xm, w_gate)
    cos_t, sin_t = _rope_tables(t_lat, t_ctx)
    h_fw, h_bw = _mlstm(proj, gates, gates.T, cos_t, sin_t, igate_b, fgate_b, t_lat // SCAN_CHUNK)
    y_na = jnp.concatenate([_na(proj, rpb, t_lat, t_ctx), _ctx_attn(proj, t_lat, t_ctx)], axis=0)
    return _outproj([(h_fw, 0), (h_bw, 0), (proj, 3)], [(y_na, 0)], w_out.astype(BF16), x_all, mods,
                    ln_g, ln_b, n_lat_tiles)


def _layer_cd(x_all, mods, t_lat, w_in, w_out, lb, s5p, s5_d, glu_w, glu_b, ln_g, ln_b, xm=None):
    n_lat_tiles = t_lat // ROW_TILE
    if xm is None:
        xm = _modulate(x_all, mods, n_lat_tiles)
    proj = _inproj(xm, w_in.astype(BF16))
    o_fw, o_bw = _gla(proj, lb, t_lat // GLA_CHUNK)
    y5 = _s5(proj[:, 5 * HALF:], _s5_mats(*[p.astype(F32) for p in s5p]), t_lat // S5_CHUNK)
    return _outproj([(o_fw, 0), (o_bw, 0), (proj, 4)], [(y5, 0), (proj, 5)], w_out.astype(BF16), x_all, mods,
                    ln_g, ln_b, n_lat_tiles,
                    glu_params=(s5_d.astype(F32), glu_w.astype(BF16), glu_b.astype(F32)))


def kernel(x, c, ctx, c_ctx, ada_w, ada_b, ln_g, ln_b, ab_w_in, ab_w_out, mlstm_igate_b, mlstm_fgate_b, na_rpb, cd_w_in, cd_w_out, hgrn_lb_logits, s5_a_re, s5_a_im, s5_log_dt, s5_b_re, s5_b_im, s5_c_re, s5_c_im, s5_d, s5_glu_w, s5_glu_b, peer_w_q, peer_sub_keys, peer_u, peer_v):
    t_lat, t_ctx = x.shape[1], ctx.shape[1]
    assert x.shape[0] == 1 and t_ctx == ROW_TILE and t_lat % (NA_QROWS * GRID_W) == 0
    depth = ada_w.shape[0]
    lb_soft = jax.nn.softmax(hgrn_lb_logits.astype(F32), axis=0)
    lower_bounds = jnp.cumsum(lb_soft, axis=0) - lb_soft[0]
    x_all = jnp.concatenate([x[0], ctx[0]], axis=0).astype(F32)
    n_lat_tiles = t_lat // ROW_TILE
    mods_all = [_adaln(c, c_ctx, ada_w, ada_b, l) for l in range(depth)]
    xm = _modulate(x_all, mods_all[0], n_lat_tiles)
    tables = _peer_tables(peer_u, peer_v)
    for l in range(depth):
        j = l // 2
        mods = mods_all[l]
        if l % 2 == 0:
            x1, h2t = _layer_ab(x_all, mods, t_lat, t_ctx, ab_w_in[j], ab_w_out[j], mlstm_igate_b[j],
                                mlstm_fgate_b[j], na_rpb[j], ln_g[l, 0], ln_b[l, 0], xm=xm)
        else:
            s5p = (s5_a_re[j], s5_a_im[j], s5_log_dt[j], s5_b_re[j], s5_b_im[j], s5_c_re[j], s5_c_im[j])
            x1, h2t = _layer_cd(x_all, mods, t_lat, cd_w_in[j], cd_w_out[j], lower_bounds[l], s5p, s5_d[j],
                                s5_glu_w[j], s5_glu_b[j], ln_g[l, 0], ln_b[l, 0], xm=xm)
        peer_args = (x1, h2t, mods, ln_g[l, 1], ln_b[l, 1], n_lat_tiles, peer_w_q[l], peer_sub_keys[l],
                     tables, l)
        if l + 1 < depth:
            x_all, xm = _peer_and_norm(*peer_args, mods_next=mods_all[l + 1])
        else:
            x_all = _peer_and_norm(*peer_args)
    return x_all[None].astype(x.dtype)
```

```python
import functools
import math

import numpy as np
import jax
import jax.numpy as jnp
from jax import lax
from jax.experimental import pallas as pl
from jax.experimental.pallas import tpu as pltpu

F32 = jnp.float32
BF16 = jnp.bfloat16

HEAD_DIM = 128
GRID_W = 64
N_HEADS = 8
HALF = N_HEADS * HEAD_DIM
NA_WIN_ROWS = 8
NA_WIN_COLS = 16
ROPE_BASE = 10000.0
S5_GROUP = 16
S5_STATE = 64
S5_GROUPS = HALF // S5_GROUP
PEER_HEADS = 8
PEER_N_KEYS = 128
PEER_TOPK = 16
PEER_TOP_ROWS = 24
LN_EPS = 1e-5
DEPTH = 2
ALPHA = (2.0 * DEPTH) ** 0.25
QK_SCALE = HEAD_DIM ** -0.5

ROW_TILE = 256
MM_ROW_TILE = 1280
MM_COL_TILE = 1024
SCAN_CHUNK = 128
GLA_CHUNK = 64
S5_CHUNK = 32
S5_SCAN_BLOCK = 8
NA_HEADS = 4
NA_QROWS = 8
PEER_P1_TILE = 256
PEER_TOK_TILE = 512
PEER_EXP_TILE = 512
PEER_GATE_ROWS = 64
PEER_U_SPLIT = 4
NEG = -1e30
GLA_SAFE_LOG = -80.0
VMEM_LIMIT = 56 * 1024 * 1024


def _cparams(n_axes):
    return pltpu.CompilerParams(dimension_semantics=("arbitrary",) * n_axes,
                                vmem_limit_bytes=VMEM_LIMIT)


def _nt(a, b):
    return lax.dot_general(a, b, (((1,), (1,)), ((), ())), preferred_element_type=F32)


def _tn(a, b):
    return lax.dot_general(a, b, (((0,), (0,)), ((), ())), preferred_element_type=F32)


def _nn(a, b):
    return jnp.dot(a, b, preferred_element_type=F32)


def _split3(x):
    hi = x.astype(BF16)
    r = x - hi.astype(F32)
    mid = r.astype(BF16)
    lo = (r - mid.astype(F32)).astype(BF16)
    return hi, mid, lo


def _log_sigmoid(x):
    return jnp.minimum(x, 0.0) - jnp.log1p(jnp.exp(-jnp.abs(x)))


def _sigmoid(x):
    return 1.0 / (1.0 + jnp.exp(-x))


def _gelu_tanh(x):
    return 0.5 * x * (1.0 + jnp.tanh(math.sqrt(2.0 / math.pi) * (x + 0.044715 * (x * x * x))))


def _head_norm(x):
    mu = jnp.mean(x, axis=-1, keepdims=True)
    xc = x - mu
    return xc * lax.rsqrt(jnp.mean(xc * xc, axis=-1, keepdims=True) + LN_EPS)


def _adaln_kernel(ct_ref, w_ref, b_ref, o_ref):
    ct = ct_ref[...]
    s = ct * _sigmoid(ct)
    w = w_ref[...]
    o_ref[0:1, :] = jnp.sum(s[:, 0:1] * w, axis=0, keepdims=True) + b_ref[...]
    o_ref[1:2, :] = jnp.sum(s[:, 1:2] * w, axis=0, keepdims=True) + b_ref[...]


def _adaln(c_lat, c_ctx, w_all, b_all, layer):
    n_layers, d, n = w_all.shape
    tn = n // 8
    ct = jnp.stack([c_lat.reshape(d), c_ctx.reshape(d)], axis=1)
    out = pl.pallas_call(
        _adaln_kernel,
        grid=(n // tn,),
        in_specs=[pl.BlockSpec((d, 2), lambda j: (0, 0)),
                  pl.BlockSpec((None, d, tn), lambda j: (layer, 0, j)),
                  pl.BlockSpec((None, 1, tn), lambda j: (layer, 0, j))],
        out_specs=pl.BlockSpec((2, tn), lambda j: (0, j)),
        out_shape=jax.ShapeDtypeStruct((2, n), F32),
        compiler_params=_cparams(1),
    )(ct, w_all, b_all.reshape(n_layers, 1, n))
    return jnp.pad(out.reshape(2, 6, d), ((0, 0), (0, 2), (0, 0)))


def _modulated(x, m_ref):
    return (x * (1.0 + m_ref[1:2, :]) + m_ref[0:1, :]).astype(BF16)


def _modulate_kernel(x_ref, m_ref, o_ref):
    o_ref[...] = _modulated(x_ref[...], m_ref)


def _mods_spec(d, n_lat_tiles):
    return pl.BlockSpec((None, 8, d), lambda i: (jnp.where(i < n_lat_tiles, 0, 1), 0, 0))


def _modulate(x_all, mods, n_lat_tiles):
    t_all, d = x_all.shape
    row = pl.BlockSpec((ROW_TILE, d), lambda i: (i, 0))
    return pl.pallas_call(
        _modulate_kernel,
        grid=(t_all // ROW_TILE,),
        in_specs=[row, _mods_spec(d, n_lat_tiles)],
        out_specs=row,
        out_shape=jax.ShapeDtypeStruct((t_all, d), BF16),
        compiler_params=_cparams(1),
    )(x_all, mods)


def _matmul_kernel(x_ref, w_ref, o_ref):
    o_ref[...] = _nn(x_ref[...], w_ref[...])


def _inproj(xm, w_bf):
    t_all, d = xm.shape
    n = w_bf.shape[1]
    tm = MM_ROW_TILE if t_all % MM_ROW_TILE == 0 else ROW_TILE
    tn = min(n, MM_COL_TILE)
    return pl.pallas_call(
        _matmul_kernel,
        grid=(t_all // tm, n // tn),
        in_specs=[pl.BlockSpec((tm, d), lambda i, j: (i, 0)),
                  pl.BlockSpec((d, tn), lambda i, j: (0, j))],
        out_specs=pl.BlockSpec((tm, tn), lambda i, j: (i, j)),
        out_shape=jax.ShapeDtypeStruct((t_all, n), F32),
        compiler_params=_cparams(2),
    )(xm, w_bf)


def _ln_rows(z, g, b):
    mu = jnp.mean(z, axis=-1, keepdims=True)
    zc = z - mu
    return zc * lax.rsqrt(jnp.mean(zc * zc, axis=-1, keepdims=True) + LN_EPS) * g + b


def _outproj_kernel(*refs, glu, n_row_tiles):
    hf_ref, hb_ref, gate_ref = refs[:3]
    refs = refs[3:]
    if glu:
        (y5_ref, u5_ref, ds_ref, gw_ref, gb_ref, w1_ref, w2_ref, x_ref, m_ref, g_ref, b_ref,
         x1_ref, h2_ref) = refs
        y = _gelu_tanh(y5_ref[...] + ds_ref[...] * u5_ref[...])
        a2 = y * _sigmoid(_nn(y.astype(BF16), gw_ref[...]) + gb_ref[...])
    else:
        a2_ref, w1_ref, w2_ref, x_ref, m_ref, g_ref, b_ref, x1_ref, h2_ref = refs
        a2 = a2_ref[...]
    heads = []
    for h in range(N_HEADS):
        hs = slice(h * HEAD_DIM, (h + 1) * HEAD_DIM)
        gx = gate_ref[:, hs]
        gate = gx * _sigmoid(gx) if glu else _sigmoid(gx)
        heads.append((_head_norm(hf_ref[:, hs] + hb_ref[:, hs]) * gate).astype(BF16))
    a1 = jnp.concatenate(heads, axis=1)
    y = _nn(a1, w1_ref[...]) + _nn(a2.astype(BF16), w2_ref[...])
    x1 = _ln_rows(ALPHA * x_ref[...] + m_ref[2:3, :] * y, g_ref[...], b_ref[...])
    x1_ref[...] = x1
    h2t = (x1 * (1.0 + m_ref[4:5, :]) + m_ref[3:4, :]).T
    h2_ref[...] = jnp.where(pl.program_id(0) < n_row_tiles, h2t, 0.0).astype(BF16)


def _outproj(scan_parts, a2_parts, w_out_bf, x_all, mods, ln_g, ln_b, n_lat_tiles, glu_params=None):
    t_all, d = x_all.shape
    half = HALF
    n_row_tiles = t_all // ROW_TILE
    t_pad = _padded_tokens(t_all)
    row = lambda i: (jnp.minimum(i, n_row_tiles - 1), 0)
    const = lambda i: (0, 0)
    a2_parts = list(scan_parts) + list(a2_parts)
    a2_specs = [pl.BlockSpec((ROW_TILE, half), lambda i, c=c: (jnp.minimum(i, n_row_tiles - 1), c))
                for _, c in a2_parts]
    a2_args = [a for a, _ in a2_parts]
    if glu_params is not None:
        d_skip, glu_w_bf, glu_b = glu_params
        a2_specs += [pl.BlockSpec((1, half), const), pl.BlockSpec((half, half), const),
                     pl.BlockSpec((1, half), const)]
        a2_args += [d_skip.reshape(1, half), glu_w_bf, glu_b.reshape(1, half)]
    return pl.pallas_call(
        functools.partial(_outproj_kernel, glu=glu_params is not None, n_row_tiles=n_row_tiles),
        grid=(t_pad // ROW_TILE,),
        in_specs=a2_specs + [
            pl.BlockSpec((half, d), const), pl.BlockSpec((half, d), const),
            pl.BlockSpec((ROW_TILE, d), row),
            pl.BlockSpec((None, 8, d), lambda i: (jnp.where(i < n_lat_tiles, 0, 1), 0, 0)),
            pl.BlockSpec((1, d), const), pl.BlockSpec((1, d), const)],
        out_specs=[pl.BlockSpec((ROW_TILE, d), row), pl.BlockSpec((d, ROW_TILE), lambda i: (0, i))],
        out_shape=[jax.ShapeDtypeStruct((t_all, d), F32), jax.ShapeDtypeStruct((d, t_pad), BF16)],
        compiler_params=_cparams(1),
    )(*a2_args, w_out_bf[:half], w_out_bf[half:], x_all, mods, ln_g.reshape(1, d), ln_b.reshape(1, d))


def _ln2_kernel(x_ref, p_ref, m_ref, g_ref, b_ref, *rest):
    x2 = _ln_rows(ALPHA * x_ref[...] + m_ref[5:6, :] * p_ref[...].T, g_ref[...], b_ref[...])
    if len(rest) == 1:
        rest[0][...] = x2
    else:
        mn_ref, o_ref, xm_ref = rest
        o_ref[...] = x2
        xm_ref[...] = _modulated(x2, mn_ref)


def _ln2(x1, peer_out_t, mods, ln_g, ln_b, n_lat_tiles, mods_next=None):
    t_all, d = x1.shape
    row = pl.BlockSpec((ROW_TILE, d), lambda i: (i, 0))
    col = pl.BlockSpec((d, ROW_TILE), lambda i: (0, i))
    vec = pl.BlockSpec((1, d), lambda i: (0, 0))
    nxt = mods_next is not None
    n_rows = t_all if nxt else n_lat_tiles * ROW_TILE
    return pl.pallas_call(
        _ln2_kernel,
        grid=(n_rows // ROW_TILE,),
        in_specs=[row, col, _mods_spec(d, n_lat_tiles), vec, vec] + ([_mods_spec(d, n_lat_tiles)] if nxt else []),
        out_specs=[row, row] if nxt else row,
        out_shape=([jax.ShapeDtypeStruct((t_all, d), F32), jax.ShapeDtypeStruct((t_all, d), BF16)] if nxt
                   else jax.ShapeDtypeStruct((n_rows, d), F32)),
        compiler_params=_cparams(1),
    )(x1, peer_out_t, mods, ln_g.reshape(1, d), ln_b.reshape(1, d), *([mods_next] if nxt else []))


def _scan_block(rev, n_chunks, n_lat_chunks):
    if rev:
        return lambda s: n_chunks - 1 - s
    n_ctx = n_chunks - n_lat_chunks
    return lambda s: jnp.where(s < n_ctx, s + n_lat_chunks, s - n_ctx)


def _scan_mask(rev, n):
    row = lax.broadcasted_iota(jnp.int32, (n, n), 0)
    col = lax.broadcasted_iota(jnp.int32, (n, n), 1)
    return (col >= row) if rev else (col <= row)


def _rope(x, cos, sin):
    lane = lax.broadcasted_iota(jnp.int32, x.shape, 1)
    partner = jnp.where((lane % 64) < 32, pltpu.roll(x, 96, 1), pltpu.roll(x, 32, 1))
    return x * cos + partner * sin


def _mlstm_kernel(*refs):
    n_in = 5
    fw_in, bw_in = refs[:n_in], refs[n_in:2 * n_in]
    bias_ref, biast_ref, out_f, out_b = refs[2 * n_in:2 * n_in + 4]
    state = refs[2 * n_in + 4:]

    @pl.when(pl.program_id(0) == 0)
    def _():
        for r in state:
            r[...] = jnp.zeros_like(r)

    _mlstm_chunk(*fw_in, bias_ref, biast_ref, out_f, *state[:2], rev=False)
    _mlstm_chunk(*bw_in, bias_ref, biast_ref, out_b, *state[2:], rev=True)


def _mlstm_chunk(q_ref, k_ref, v_ref, g_ref, gt_ref, bias_ref, biast_ref, out_ref, cn_ref, m_ref, *, rev):
    n = q_ref.shape[0]
    mask = _scan_mask(rev, n)
    tri = mask.astype(BF16)
    pre_c = g_ref[...] + bias_ref[...]
    pre_r = gt_ref[...] + biast_ref[...]
    b_c = sum(_nn(tri, p) for p in _split3(_log_sigmoid(pre_c)))
    b_r = sum(_nt(p, tri) for p in _split3(_log_sigmoid(pre_r)))
    last = 0 if rev else n - 1
    off = N_HEADS if rev else 0
    heads = range(N_HEADS)
    bc = jnp.stack([b_c[:, 2 * N_HEADS + off + h:2 * N_HEADS + off + h + 1] for h in heads])
    br = jnp.stack([b_r[2 * N_HEADS + off + h:2 * N_HEADS + off + h + 1, :] for h in heads])
    ic = jnp.stack([pre_c[:, off + h:off + h + 1] for h in heads])
    ir = jnp.stack([pre_r[off + h:off + h + 1, :] for h in heads])
    m_prev = m_ref[:, :, 0:1]
    logw = jnp.where(mask[None], bc - br + ir, NEG)
    inter = bc + m_prev
    m_t = jnp.maximum(inter, jnp.max(logw, axis=2, keepdims=True))
    w_inter = jnp.exp(inter - m_t)
    qb, kb = _by_head(q_ref[...]), _by_head(k_ref[...])
    vb = _by_head(v_ref[...]).astype(BF16)
    v_ext = jnp.concatenate([vb, jnp.ones_like(vb)], axis=2)
    s = _bqk(qb, kb) * jnp.exp(logw - m_t)
    cn_old = cn_ref[...]
    both = _bqd(s.astype(BF16), v_ext) + _bqd(qb, cn_old.astype(BF16)) * w_inter
    den = jnp.maximum(jnp.abs(both[:, :, HEAD_DIM:]), jnp.exp(-m_t))
    out = both[:, :, :HEAD_DIM] / den
    for h in heads:
        out_ref[:, h * HEAD_DIM:(h + 1) * HEAD_DIM] = out[h]
    m_new = m_t[:, last:last + 1, :]
    b_last = bc[:, last:last + 1, :]
    w_old = jnp.exp(b_last + m_prev - m_new)
    kw = kb.astype(F32) * jnp.exp(b_last - bc + ic - m_new)
    cn_ref[...] = w_old * cn_old + jnp.einsum('hkm,hkd->hmd', kw.astype(BF16), v_ext, preferred_element_type=F32)
    m_ref[...] = jnp.broadcast_to(m_new, m_ref.shape)


def _rope_kernel(q_ref, k_ref, cos_ref, sin_ref, qo_ref, ko_ref):
    cos, sin = cos_ref[...], sin_ref[...]
    for h in range(N_HEADS):
        hs = slice(h * HEAD_DIM, (h + 1) * HEAD_DIM)
        qo_ref[:, hs] = (_rope(q_ref[:, hs], cos, sin) * QK_SCALE).astype(BF16)
        ko_ref[:, hs] = _rope(k_ref[:, hs], cos, sin).astype(BF16)


def _rope_qk(proj, cos_t, sin_t):
    t_all = proj.shape[0]
    tok = pl.BlockSpec((ROW_TILE, 128), lambda i: (i, 0))
    return pl.pallas_call(
        _rope_kernel,
        grid=(t_all // ROW_TILE,),
        in_specs=[pl.BlockSpec((ROW_TILE, HALF), lambda i: (i, 0)),
                  pl.BlockSpec((ROW_TILE, HALF), lambda i: (i, 1)), tok, tok],
        out_specs=[pl.BlockSpec((ROW_TILE, HALF), lambda i: (i, 0))] * 2,
        out_shape=[jax.ShapeDtypeStruct((t_all, HALF), BF16)] * 2,
        compiler_params=_cparams(1),
    )(proj, proj, cos_t, sin_t)


def _mlstm(proj, gates, gates_t, cos_t, sin_t, igate_b, fgate_b, n_lat_chunks):
    t_all = proj.shape[0]
    n = SCAN_CHUNK
    n_chunks = t_all // n
    bias = jnp.zeros((1, 128), F32).at[0, :4 * N_HEADS].set(
        jnp.concatenate([igate_b.reshape(-1), fgate_b.reshape(-1)]))
    q_rot, k_rot = _rope_qk(proj, cos_t, sin_t)

    def dir_specs(rev):
        blk = _scan_block(rev, n_chunks, n_lat_chunks)
        colblk = lambda c: pl.BlockSpec((n, HALF), lambda s, c=c: (blk(s), c))
        ins = [colblk(0), colblk(0), colblk(2), pl.BlockSpec((n, 128), lambda s: (blk(s), 0)),
               pl.BlockSpec((128, n), lambda s: (0, blk(s)))]
        return ins, pl.BlockSpec((n, HALF), lambda s: (blk(s), 0))

    (in_f, out_f), (in_b, out_b) = dir_specs(False), dir_specs(True)
    dir_args = [q_rot, k_rot, proj, gates, gates_t]
    state = [pltpu.VMEM((N_HEADS, HEAD_DIM, 2 * HEAD_DIM), F32), pltpu.VMEM((N_HEADS, 1, HEAD_DIM), F32)]
    return pl.pallas_call(
        _mlstm_kernel,
        grid=(n_chunks,),
        in_specs=in_f + in_b + [pl.BlockSpec((1, 128), lambda s: (0, 0)), pl.BlockSpec((128, 1), lambda s: (0, 0))],
        out_specs=[out_f, out_b],
        out_shape=[jax.ShapeDtypeStruct((t_all, HALF), F32)] * 2,
        scratch_shapes=state + state,
        compiler_params=_cparams(1),
    )(*dir_args, *dir_args, bias, bias.reshape(128, 1))


def _by_head(x):
    return jnp.stack([x[:, h * HEAD_DIM:(h + 1) * HEAD_DIM] for h in range(x.shape[1] // HEAD_DIM)])


def _bqk(a, b):
    return jnp.einsum('hqd,hkd->hqk', a, b, preferred_element_type=F32)


def _bqd(p, v):
    return jnp.einsum('hqk,hkd->hqd', p, v, preferred_element_type=F32)


def _na_kernel(q_ref, kp_ref, kc_ref, kn_ref, vp_ref, vc_ref, vn_ref, kx_ref, vx_ref, bias_ref, o_ref):
    qb = _by_head(q_ref[...] * QK_SCALE).astype(BF16)
    kk = _by_head(jnp.concatenate([kp_ref[...], kc_ref[...], kn_ref[...]], axis=0)).astype(BF16)
    vv = _by_head(jnp.concatenate([vp_ref[...], vc_ref[...], vn_ref[...]], axis=0)).astype(BF16)
    s_loc = _bqk(qb, kk) + bias_ref[...]
    s_ctx = _bqk(qb, _by_head(kx_ref[...]).astype(BF16))
    m = jnp.maximum(jnp.max(s_loc, axis=2, keepdims=True), jnp.max(s_ctx, axis=2, keepdims=True))
    p_loc = jnp.exp(s_loc - m)
    p_ctx = jnp.exp(s_ctx - m)
    l = jnp.sum(p_loc, axis=2, keepdims=True) + jnp.sum(p_ctx, axis=2, keepdims=True)
    out = (_bqd(p_loc.astype(BF16), vv) + _bqd(p_ctx.astype(BF16), _by_head(vx_ref[...]).astype(BF16))) / l
    for h in range(out.shape[0]):
        o_ref[:, h * HEAD_DIM:(h + 1) * HEAD_DIM] = out[h]


def _na_bias(rpb, rows):
    qn = NA_QROWS * GRID_W
    kn = 2 * qn
    ql = np.arange(qn)
    kl = np.arange(kn)
    out = []
    nqb = rows // NA_QROWS
    lr, krl = np.arange(NA_QROWS)[:, None], np.arange(2 * NA_QROWS)[None, :]
    dr = np.clip(krl - NA_QROWS // 2 - lr + NA_WIN_ROWS - 1, 0, 2 * NA_WIN_ROWS - 2)
    col = np.arange(GRID_W)
    dc = np.clip(col[None, :] - col[:, None] + NA_WIN_COLS - 1, 0, 2 * NA_WIN_COLS - 2)
    oh_r = jnp.asarray(dr[..., None] == np.arange(2 * NA_WIN_ROWS - 1), F32)
    oh_c = jnp.asarray(dc[..., None] == np.arange(2 * NA_WIN_COLS - 1), F32)
    table = jnp.einsum('hab,lka,qcb->hlqkc', rpb, oh_r, oh_c,
                       precision=lax.Precision.HIGHEST).reshape(rpb.shape[0], qn, kn)
    for qb in (0, 1, nqb - 1):
        qr = NA_QROWS * qb + ql // GRID_W
        qc = ql % GRID_W
        kr = NA_QROWS * qb - NA_QROWS // 2 + kl // GRID_W
        kc = kl % GRID_W
        r0 = np.clip(qr - NA_WIN_ROWS // 2, 0, rows - NA_WIN_ROWS)
        c0 = np.clip(qc - NA_WIN_COLS // 2, 0, GRID_W - NA_WIN_COLS)
        ok = ((kr[None, :] >= r0[:, None]) & (kr[None, :] < r0[:, None] + NA_WIN_ROWS)
              & (kc[None, :] >= c0[:, None]) & (kc[None, :] < c0[:, None] + NA_WIN_COLS)
              & (kr[None, :] >= 0) & (kr[None, :] < rows))
        out.append(jnp.where(jnp.asarray(ok)[None], table, NEG))
    return jnp.stack(out)


def _na(proj, rpb, t_lat, t_ctx):
    rows = t_lat // GRID_W
    qn = NA_QROWS * GRID_W
    hn = qn // 2
    nqb = rows // NA_QROWS
    assert nqb >= 2 and t_ctx == hn
    bias = _na_bias(rpb.astype(F32), rows)
    last_half = 2 * nqb - 1
    ctx_blk = t_lat // hn
    cls = lambda b: jnp.where(b == 0, 0, jnp.where(b == nqb - 1, 2, 1))

    nh = NA_HEADS
    w = nh * HEAD_DIM
    groups = N_HEADS // nh

    def kv_specs(col0):
        return [pl.BlockSpec((hn, w), lambda h, b: (jnp.maximum(2 * b - 1, 0), col0 + h)),
                pl.BlockSpec((qn, w), lambda h, b: (b, col0 + h)),
                pl.BlockSpec((hn, w), lambda h, b: (jnp.minimum(2 * b + 2, last_half), col0 + h))]

    qcol, kcol, vcol = 4 * groups, 5 * groups, 6 * groups
    return pl.pallas_call(
        _na_kernel,
        grid=(groups, nqb),
        in_specs=[pl.BlockSpec((qn, w), lambda h, b: (b, qcol + h))]
        + kv_specs(kcol) + kv_specs(vcol)
        + [pl.BlockSpec((hn, w), lambda h, b: (ctx_blk, kcol + h)),
           pl.BlockSpec((hn, w), lambda h, b: (ctx_blk, vcol + h)),
           pl.BlockSpec((None, nh, qn, 2 * qn), lambda h, b: (cls(b), h, 0, 0))],
        out_specs=pl.BlockSpec((qn, w), lambda h, b: (b, h)),
        out_shape=jax.ShapeDtypeStruct((t_lat, HALF), F32),
        compiler_params=_cparams(2),
    )(proj, proj, proj, proj, proj, proj, proj, proj, proj, bias)


def _ctx_attn_kernel(q_ref, k_ref, v_ref, o_ref):
    qb = (q_ref[...] * QK_SCALE).astype(BF16)
    s = _nt(qb, k_ref[...].astype(BF16))
    p = jnp.exp(s - jnp.max(s, axis=1, keepdims=True))
    o_ref[...] = _nn(p.astype(BF16), v_ref[...].astype(BF16)) / jnp.sum(p, axis=1, keepdims=True)


def _ctx_attn(proj, t_lat, t_ctx):
    blk = t_lat // t_ctx
    spec = lambda c: pl.BlockSpec((t_ctx, HEAD_DIM), lambda h, c=c: (blk, c * N_HEADS + h))
    return pl.pallas_call(
        _ctx_attn_kernel,
        grid=(N_HEADS,),
        in_specs=[spec(4), spec(5), spec(6)],
        out_specs=pl.BlockSpec((t_ctx, HEAD_DIM), lambda h: (0, h)),
        out_shape=jax.ShapeDtypeStruct((t_ctx, HALF), F32),
        compiler_params=_cparams(1),
    )(proj, proj, proj)


def _gla_inter(q_ref, i_ref, f_ref, lb_ref, out_ref, st_ref, k_scr, b_scr, *, rev):
    n = q_ref.shape[0]
    mask = _scan_mask(rev, n)
    tri = mask.astype(BF16)
    last = 0 if rev else n - 1
    x = f_ref[...]
    key = lb_ref[0:1, :] * _sigmoid(-x)
    la, lc = lb_ref[1:2, :], lb_ref[2:3, :] + _log_sigmoid(x)
    logf = jnp.maximum(la, lc) + jnp.log1p(jnp.exp(-jnp.abs(la - lc)))
    b = sum(_nn(tri, p) for p in _split3(logf))
    b_end = b[last:last + 1, :]
    qx = q_ref[...]
    q = qx * _sigmoid(qx)
    qt = q * jnp.exp(b)
    kdec = key * jnp.exp(b_end - b)
    w_end = jnp.exp(b_end)
    k_scr[...] = key
    b_scr[...] = b
    st = st_ref[...]
    vb = _by_head(i_ref[...]).astype(BF16)
    o_inter = _bqk(_by_head(qt).astype(BF16), st.astype(BF16))
    for h in range(N_HEADS):
        out_ref[:, h * HEAD_DIM:(h + 1) * HEAD_DIM] = o_inter[h]
    st_ref[...] = st * _by_head(w_end) + jnp.einsum('hnv,hnk->hvk', vb, _by_head(kdec).astype(BF16),
                                                    preferred_element_type=F32)
    return mask, q, b_end


def _gla_intra_factored(mask, q, b_end, i_ref, out_ref, k_scr, b_scr):
    mid = 0.5 * b_end
    qm = q * jnp.exp(b_scr[...] - mid)
    km = k_scr[...] * jnp.exp(mid - b_scr[...])
    att = jnp.where(mask[None], _bqk(_by_head(qm).astype(BF16), _by_head(km).astype(BF16)), 0.0)
    o_intra = _bqd(att.astype(BF16), _by_head(i_ref[...]).astype(BF16))
    for h in range(N_HEADS):
        out_ref[:, h * HEAD_DIM:(h + 1) * HEAD_DIM] += o_intra[h]


def _gla_intra_exact(q, i_ref, out_ref, k_scr, b_scr, *, rev):
    n = q.shape[0]
    t_idx = lax.broadcasted_iota(jnp.int32, (n, 1), 0)
    for h in range(N_HEADS):
        hs = slice(h * HEAD_DIM, (h + 1) * HEAD_DIM)
        q_h = q[:, hs]
        b_h = b_scr[:, hs]

        def body(grp, acc):
            base = pl.multiple_of(grp * 8, 8)
            b8, k8, v8 = b_scr[pl.ds(base, 8), hs], k_scr[pl.ds(base, 8), hs], i_ref[pl.ds(base, 8), hs]
            for r in range(8):
                s = base + r
                e = jnp.exp(jnp.minimum(b_h - b8[r:r + 1, :], 0.0))
                a = jnp.sum(q_h * k8[r:r + 1, :] * e, axis=1, keepdims=True)
                a = jnp.where((t_idx <= s) if rev else (t_idx >= s), a, 0.0)
                acc = acc + a * v8[r:r + 1, :]
            return acc

        out_ref[:, hs] += lax.fori_loop(0, n // 8, body, jnp.zeros((n, HEAD_DIM), F32))


def _gla_kernel(*refs):
    fw_in, bw_in, lb_ref = refs[0:3], refs[3:6], refs[6]
    out_f, out_b = refs[7:9]
    st_f, k_f, b_f, st_b, k_b, b_b = refs[9:]

    @pl.when(pl.program_id(0) == 0)
    def _():
        st_f[...] = jnp.zeros_like(st_f)
        st_b[...] = jnp.zeros_like(st_b)

    mask_f, q_f, end_f = _gla_inter(*fw_in, lb_ref, out_f, st_f, k_f, b_f, rev=False)
    mask_b, q_b, end_b = _gla_inter(*bw_in, lb_ref, out_b, st_b, k_b, b_b, rev=True)
    safe = jnp.minimum(jnp.min(end_f), jnp.min(end_b)) >= 2.0 * GLA_SAFE_LOG

    @pl.when(safe)
    def _():
        _gla_intra_factored(mask_f, q_f, end_f, fw_in[1], out_f, k_f, b_f)
        _gla_intra_factored(mask_b, q_b, end_b, bw_in[1], out_b, k_b, b_b)

    @pl.when(jnp.logical_not(safe))
    def _():
        _gla_intra_exact(q_f, fw_in[1], out_f, k_f, b_f, rev=False)
        _gla_intra_exact(q_b, bw_in[1], out_b, k_b, b_b, rev=True)


def _gla(proj, lb, n_lat_chunks):
    t_all = proj.shape[0]
    n = GLA_CHUNK
    n_chunks = t_all // n

    def dir_specs(rev):
        blk = _scan_block(rev, n_chunks, n_lat_chunks)
        colblk = lambda c: pl.BlockSpec((n, HALF), lambda s, c=c: (blk(s), c))
        return [colblk(0), colblk(1), colblk(3 if rev else 2)], pl.BlockSpec((n, HALF), lambda s: (blk(s), 0))

    (in_f, out_f), (in_b, out_b) = dir_specs(False), dir_specs(True)
    state = [pltpu.VMEM((N_HEADS, HEAD_DIM, HEAD_DIM), F32), pltpu.VMEM((n, HALF), F32),
             pltpu.VMEM((n, HALF), F32)]
    lb = lb.reshape(1, HALF)
    lb_rows = jnp.concatenate([1.0 - lb, jnp.log(lb), jnp.log1p(-lb)], axis=0)
    return pl.pallas_call(
        _gla_kernel,
        grid=(n_chunks,),
        in_specs=in_f + in_b + [pl.BlockSpec((3, HALF), lambda s: (0, 0))],
        out_specs=[out_f, out_b],
        out_shape=[jax.ShapeDtypeStruct((t_all, HALF), F32)] * 2,
        scratch_shapes=state + state,
        compiler_params=_cparams(1),
    )(*([proj] * 6), lb_rows)


def _s5_mats(a_re, a_im, log_dt, b_re, b_im, c_re, c_im):
    hp = lax.Precision.HIGHEST
    n = S5_CHUNK
    dt = jnp.exp(log_dt)[..., None]
    la_re, la_im = a_re * dt, a_im * dt
    mag = jnp.exp(la_re)
    ab_re, ab_im = mag * jnp.cos(la_im), mag * jnp.sin(la_im)
    nr, ni = ab_re - 1.0, ab_im
    den = jnp.square(a_re) + jnp.square(a_im)
    cr = (nr * a_re + ni * a_im) / den
    ci = (ni * a_re - nr * a_im) / den
    bb_re = cr[..., None] * b_re - ci[..., None] * b_im
    bb_im = cr[..., None] * b_im + ci[..., None] * b_re

    def apow(tau):
        tau = jnp.asarray(tau, F32)[None, None, :, None]
        m = jnp.exp(tau * la_re[:, :, None, :])
        return m * jnp.cos(tau * la_im[:, :, None, :]), m * jnp.sin(tau * la_im[:, :, None, :])

    def c_apow(tau):
        pr, pi = apow(tau)
        return (c_re[:, :, None] * pr[:, :, :, None] - c_im[:, :, None] * pi[:, :, :, None],
                c_re[:, :, None] * pi[:, :, :, None] + c_im[:, :, None] * pr[:, :, :, None])

    ca_re, ca_im = c_apow(np.arange(n))
    kern = jnp.einsum('dgtip,dgpj->dgtij', jnp.concatenate([ca_re, -ca_im], axis=-1),
                      jnp.concatenate([bb_re, bb_im], axis=2), precision=hp)
    s_idx, t_idx = np.arange(n)[:, None], np.arange(n)[None, :]
    g = a_re.shape[1]
    lp = n * S5_GROUP

    def toeplitz(kd, lag):
        onehot = jnp.asarray(lag[None] == np.arange(n)[:, None, None], F32)
        return jnp.einsum('lst,glij->gsjti', onehot, kd, precision=hp).reshape(g, lp, lp)

    kmat = toeplitz(kern[0], t_idx - s_idx) + toeplitz(kern[1], s_idx - t_idx)

    def emat(d, tau):
        pr, pi = apow(tau)
        pr, pi = pr[d][:, :, None, :], pi[d][:, :, None, :]
        br, bi = bb_re[d].transpose(0, 2, 1)[:, None], bb_im[d].transpose(0, 2, 1)[:, None]
        return jnp.concatenate([pr * br - pi * bi, pr * bi + pi * br], axis=-1).reshape(g, lp, 2 * S5_STATE)

    def fmat(d, tau):
        fr, fi = c_apow(tau)
        fr, fi = fr[d], fi[d]
        return jnp.concatenate([fr, -fi], axis=-1).transpose(0, 3, 1, 2).reshape(g, 2 * S5_STATE, lp)

    e_all = jnp.concatenate([emat(0, n - 1 - np.arange(n)), emat(1, np.arange(n))], axis=-1)
    f_all = jnp.concatenate([fmat(0, np.arange(n) + 1), fmat(1, n - np.arange(n))], axis=1)
    pr, pi = apow(np.array([n]))
    pr, pi = pr[:, :, 0], pi[:, :, 0]
    ar2 = jnp.concatenate([pr[0], pr[0], pr[1], pr[1]], axis=-1)
    ai2 = jnp.concatenate([-pi[0], pi[0], -pi[1], pi[1]], axis=-1)
    return kmat.astype(BF16), e_all.astype(BF16), f_all.astype(BF16), ar2, ai2


def _s5_state_kernel(u_ref, e_ref, s_ref):
    s_ref[...] = _nn(u_ref[...], e_ref[...])


def _s5_scan_kernel(sf_ref, sb_ref, ar_ref, ai_ref, hf_ref, hb_ref, stf_ref, stb_ref):
    nb = sf_ref.shape[0]
    w = 2 * S5_STATE
    ar, ai = ar_ref[...], ai_ref[...]

    @pl.when(pl.program_id(0) == 0)
    def _():
        stf_ref[...] = jnp.zeros_like(stf_ref)
        stb_ref[...] = jnp.zeros_like(stb_ref)

    def step(h, s, lo):
        return ar[:, lo:lo + w] * h + ai[:, lo:lo + w] * pltpu.roll(h, S5_STATE, 1) + s

    hf, hb = stf_ref[...], stb_ref[...]
    for j in range(nb):
        hf_ref[j] = hf
        hf = step(hf, sf_ref[j], 0)
        hb_ref[nb - 1 - j] = hb
        hb = step(hb, sb_ref[nb - 1 - j], w)
    stf_ref[...] = hf
    stb_ref[...] = hb


def _s5_out_kernel(u_ref, k_ref, h_ref, f_ref, y_ref):
    y_ref[...] = _nn(u_ref[...], k_ref[...]) + _nn(h_ref[...].astype(BF16), f_ref[...])


def _s5(u, mats, n_lat_chunks):
    kmat, emat, fmat, ar2, ai2 = mats
    t_all = u.shape[0]
    n = S5_CHUNK
    nc = t_all // n
    g = S5_GROUPS
    lp = n * S5_GROUP
    w4 = 4 * S5_STATE
    ug = u.astype(BF16).reshape(nc, n, g, S5_GROUP).transpose(2, 0, 1, 3).reshape(g, nc, lp)
    grp = lambda a, b: pl.BlockSpec((None, a, b), lambda i: (i, 0, 0))
    s_end = pl.pallas_call(
        _s5_state_kernel, grid=(g,),
        in_specs=[grp(nc, lp), grp(lp, w4)], out_specs=grp(nc, w4),
        out_shape=jax.ShapeDtypeStruct((g, nc, w4), F32), compiler_params=_cparams(1),
    )(ug, emat)
    nb = S5_SCAN_BLOCK
    w2 = 2 * S5_STATE
    fw_blk = _scan_block(False, nc // nb, n_lat_chunks // nb)
    bw_blk = _scan_block(True, nc // nb, n_lat_chunks // nb)
    s_t = s_end.transpose(1, 0, 2)
    coef = pl.BlockSpec((g, w4), lambda i: (0, 0))
    h_fw, h_bw = pl.pallas_call(
        _s5_scan_kernel, grid=(nc // nb,),
        in_specs=[pl.BlockSpec((nb, g, w2), lambda i: (fw_blk(i), 0, 0)),
                  pl.BlockSpec((nb, g, w2), lambda i: (bw_blk(i), 0, 1)), coef, coef],
        out_specs=[pl.BlockSpec((nb, g, w2), lambda i: (fw_blk(i), 0, 0)),
                   pl.BlockSpec((nb, g, w2), lambda i: (bw_blk(i), 0, 0))],
        out_shape=[jax.ShapeDtypeStruct((nc, g, w2), F32)] * 2,
        scratch_shapes=[pltpu.VMEM((g, w2), F32)] * 2,
        compiler_params=_cparams(1),
    )(s_t, s_t, ar2, ai2)
    h_in = jnp.concatenate([h_fw, h_bw], axis=-1).transpose(1, 0, 2)
    y = pl.pallas_call(
        _s5_out_kernel, grid=(g,),
        in_specs=[grp(nc, lp), grp(lp, lp), grp(nc, w4), grp(w4, lp)], out_specs=grp(nc, lp),
        out_shape=jax.ShapeDtypeStruct((g, nc, lp), F32), compiler_params=_cparams(1),
    )(ug, kmat, h_in, fmat)
    return y.reshape(g, nc, n, S5_GROUP).transpose(1, 2, 0, 3).reshape(t_all, HALF)


def _top_rows(x, k):
    rows = []
    for _ in range(k):
        mx = jnp.max(x, axis=0, keepdims=True)
        rows.append(mx)
        x = jnp.where(x == mx, NEG, x)
    return rows


def _top_rows_sorted(x, k):
    n = x.shape[0] // 8
    cols = [x[8 * i:8 * (i + 1), :] for i in range(n)]
    size = 2
    while size <= n:
        stride = size // 2
        while stride >= 1:
            for i in range(n):
                j = i ^ stride
                if j > i:
                    hi, lo = jnp.maximum(cols[i], cols[j]), jnp.minimum(cols[i], cols[j])
                    cols[i], cols[j] = (hi, lo) if (i & size) == 0 else (lo, hi)
            stride //= 2
        size *= 2
    rows = []
    for r in range(k):
        mx = jnp.max(cols[0], axis=0, keepdims=True)
        rows.append(mx)
        popped = cols[0] == mx
        live = min(n, k - r)
        for i in range(live):
            nxt = cols[i + 1] if i + 1 < n else NEG
            cols[i] = jnp.where(popped, nxt, cols[i])
    return rows


def _peer_route_kernel(ht_ref, wq_ref, keys_ref, e1_ref, e0_ref, th_ref, top_scr):
    kk = PEER_TOPK
    qt = _nn(wq_ref[...], ht_ref[...])
    for hp in range(2 * PEER_HEADS):
        sc = _nn(keys_ref[hp], qt[hp * HEAD_DIM:(hp + 1) * HEAD_DIM, :].astype(BF16))
        h, p = divmod(hp, 2)
        if p == 0:
            e0_ref[h] = sc
        else:
            e1_ref[h] = sc
        for r, mx in enumerate(_top_rows_sorted(sc, kk + 1)):
            top_scr[hp, r:r + 1, :] = mx
        top_scr[hp, kk + 1:, :] = jnp.full((PEER_TOP_ROWS - kk - 1, sc.shape[1]), NEG, F32)

    row = lax.broadcasted_iota(jnp.int32, (PEER_TOP_ROWS, 1), 0)
    for h in range(PEER_HEADS):
        a = top_scr[2 * h]
        b = top_scr[2 * h + 1]
        parts = [a[0:1] + b, a[1:2] + b[0:8], a[2:3] + b[0:8], a[3:4] + b[0:8]]
        parts += [jnp.where(row >= 4, a + b[j:j + 1], NEG) for j in range(3)]
        cand = jnp.maximum(jnp.concatenate(parts, axis=0), NEG)
        top = _top_rows(cand, kk + 1)
        thr = 0.5 * (top[kk - 1] + top[kk])
        z = jnp.sum(jnp.where(cand >= thr, jnp.exp(cand - top[0]), 0.0), axis=0, keepdims=True)
        e0_ref[h] = jnp.exp(e0_ref[h] - a[0:1]) / z
        e1_ref[h] = jnp.exp(e1_ref[h] - b[0:1])
        th_ref[h:h + 1, :] = jnp.exp(thr - top[0]) / z


def _peer_expert_kernel(ht_ref, e1_ref, e0_ref, th_ref, *rest, n_tiles):
    u_refs = rest[:PEER_U_SPLIT]
    vt_ref, o_ref, at0_scr, at1_scr, g0_scr, g1_scr = rest[PEER_U_SPLIT:]
    s = pl.program_id(1)
    assert n_tiles % 2 == 0 and n_tiles >= 4
    ni = u_refs[0].shape[0] // PEER_N_KEYS
    kc = u_refs[0].shape[1]
    at_scr, g_scr = (at0_scr, at1_scr), (g0_scr, g1_scr)

    def stages(par, dot1, gate, dot2, first=False):
        if dot1:
            at_scr[par][...] = _nn(jnp.concatenate([u_k[...] for u_k in u_refs], axis=1), ht_ref[...])
        if gate:
            at_old, g_new = at_scr[1 - par], g_scr[1 - par]
            nj, nc = PEER_GATE_ROWS, 128
            for ii in range(ni):
                i = (s - 1) * ni + ii
                e0_rows = [e0_ref[h, pl.ds(i, 1), :] for h in range(PEER_HEADS)]
                for c in range(at_old.shape[1] // nc):
                    cs = slice(c * nc, (c + 1) * nc)
                    for jb in range(PEER_N_KEYS // nj):
                        js = slice(jb * nj, (jb + 1) * nj)
                        w = None
                        for h in range(PEER_HEADS):
                            p = e1_ref[h, js, cs] * e0_rows[h][:, cs]
                            t = jnp.where(p >= th_ref[h:h + 1, cs], p, 0.0)
                            w = t if w is None else w + t
                        rs = slice(ii * PEER_N_KEYS + jb * nj, ii * PEER_N_KEYS + (jb + 1) * nj)
                        g_new[rs, cs] = (_gelu_tanh(at_old[rs, cs]) * w).astype(BF16)
        if dot2:
            prod = _nn(vt_ref[...], g_scr[par][...])
            if first:
                o_ref[...] = prod
            else:
                o_ref[...] += prod

    mid = jnp.logical_and(s > 2, s < n_tiles)
    pl.when(s == 0)(lambda: stages(0, True, False, False))
    pl.when(s == 1)(lambda: stages(1, True, True, False))
    pl.when(s == 2)(lambda: stages(0, True, True, True, first=True))
    pl.when(jnp.logical_and(mid, s % 2 == 0))(lambda: stages(0, True, True, True))
    pl.when(jnp.logical_and(mid, s % 2 == 1))(lambda: stages(1, True, True, True))
    pl.when(s == n_tiles)(lambda: stages(0, False, True, True))
    pl.when(s == n_tiles + 1)(lambda: stages(1, False, False, True))


def _tok3(t):
    return pl.BlockSpec((PEER_HEADS, PEER_N_KEYS, t), lambda i, *_: (0, 0, i))


def _peer_route(h2t, wq_t_bf, keys_bf):
    d, t_all = h2t.shape
    nh, nk = PEER_HEADS, PEER_N_KEYS
    t1 = PEER_P1_TILE
    return pl.pallas_call(
        _peer_route_kernel,
        grid=(t_all // t1,),
        in_specs=[pl.BlockSpec((d, t1), lambda i: (0, i)),
                  pl.BlockSpec(wq_t_bf.shape, lambda i: (0, 0)),
                  pl.BlockSpec(keys_bf.shape, lambda i: (0, 0, 0))],
        out_specs=[_tok3(t1), _tok3(t1), pl.BlockSpec((nh, t1), lambda i: (0, i))],
        out_shape=[jax.ShapeDtypeStruct((nh, nk, t_all), F32)] * 2 + [jax.ShapeDtypeStruct((nh, t_all), F32)],
        scratch_shapes=[pltpu.VMEM((2 * nh, PEER_TOP_ROWS, t1), F32)],
        compiler_params=_cparams(1),
    )(h2t, wq_t_bf, keys_bf)


def _padded_tokens(t_all):
    return t_all + (-t_all % PEER_TOK_TILE)


def _peer(h2t, wq_t_bf, keys_bf, u_r, vt_r, layer=0):
    return _peer_experts(h2t, *_peer_route(h2t, wq_t_bf, keys_bf), u_r, vt_r, layer)


def _peer_tables(u_all, v_all):
    return u_all.astype(BF16), v_all.transpose(0, 2, 1).astype(BF16)


def _peer_experts(h2t, e1, e0, th, u_r, vt_r, layer):
    d, t_all = h2t.shape
    tt = PEER_TOK_TILE
    et = PEER_EXP_TILE
    n_tiles = u_r.shape[1] // et
    kc = d // PEER_U_SPLIT
    tok = pl.BlockSpec((d, tt), lambda i, s: (0, i))
    return pl.pallas_call(
        functools.partial(_peer_expert_kernel, n_tiles=n_tiles),
        grid=(t_all // tt, n_tiles + 2),
        in_specs=[tok, _tok3(tt), _tok3(tt), pl.BlockSpec((PEER_HEADS, tt), lambda i, s: (0, i))]
        + [pl.BlockSpec((None, et, kc), lambda i, s, k=k: (layer, jnp.minimum(s, n_tiles - 1), k))
           for k in range(PEER_U_SPLIT)]
        + [pl.BlockSpec((None, d, et), lambda i, s: (layer, 0, jnp.clip(s - 2, 0, n_tiles - 1)))],
        out_specs=tok,
        out_shape=jax.ShapeDtypeStruct((d, t_all), F32),
        scratch_shapes=[pltpu.VMEM((et, tt), F32)] * 2 + [pltpu.VMEM((et, tt), BF16)] * 2,
        compiler_params=_cparams(2),
    )(h2t, e1, e0, th, *([u_r] * PEER_U_SPLIT), vt_r)


def _rope_tables(t_lat, t_ctx):
    pos = np.arange(t_lat)
    nfreq = HEAD_DIM // 4
    freqs = ROPE_BASE ** (-jnp.arange(nfreq, dtype=F32) / nfreq)
    ang_r = jnp.asarray(pos // GRID_W, F32)[:, None] * freqs
    ang_c = jnp.asarray(pos % GRID_W, F32)[:, None] * freqs
    cos = jnp.concatenate([jnp.cos(ang_r), jnp.cos(ang_r), jnp.cos(ang_c), jnp.cos(ang_c)], axis=1)
    sin = jnp.concatenate([-jnp.sin(ang_r), jnp.sin(ang_r), -jnp.sin(ang_c), jnp.sin(ang_c)], axis=1)
    cos = jnp.concatenate([cos, jnp.ones((t_ctx, HEAD_DIM), F32)], axis=0)
    sin = jnp.concatenate([sin, jnp.zeros((t_ctx, HEAD_DIM), F32)], axis=0)
    return cos, sin


def _peer_and_norm(x1, h2t, mods, ln_g, ln_b, n_lat_tiles, w_q, sub_keys, tables, layer, mods_next=None):
    keys = sub_keys.reshape(2 * PEER_HEADS, PEER_N_KEYS, -1).astype(BF16)
    pt = _peer(h2t, w_q.T.astype(BF16), keys, *tables, layer)
    return _ln2(x1, pt, mods, ln_g, ln_b, n_lat_tiles, mods_next)


def _layer_ab(x_all, mods, t_lat, t_ctx, w_in, w_out, igate_b, fgate_b, rpb, ln_g, ln_b, xm=None):
    n_lat_tiles = t_lat // ROW_TILE
    if xm is None:
        xm = _modulate(x_all, mods, n_lat_tiles)
    ng = 4 * N_HEADS
    w_main = jnp.concatenate([w_in[:, :4 * HALF], w_in[:, 4 * HALF + ng:]], axis=1).astype(BF16)
    w_gate = jnp.pad(w_in[:, 4 * HALF:4 * HALF + ng], ((0, 0), (0, 128 - ng))).astype(BF16)
    proj = _inproj(xm, w_main)
    gates = _inproj(xm, w_gate)
    cos_t, sin_t = _rope_tables(t_lat, t_ctx)
    h_fw, h_bw = _mlstm(proj, gates, gates.T, cos_t, sin_t, igate_b, fgate_b, t_lat // SCAN_CHUNK)
    y_na = jnp.concatenate([_na(proj, rpb, t_lat, t_ctx), _ctx_attn(proj, t_lat, t_ctx)], axis=0)
    return _outproj([(h_fw, 0), (h_bw, 0), (proj, 3)], [(y_na, 0)], w_out.astype(BF16), x_all, mods,
                    ln_g, ln_b, n_lat_tiles)


def _layer_cd(x_all, mods, t_lat, w_in, w_out, lb, s5p, s5_d, glu_w, glu_b, ln_g, ln_b, xm=None):
    n_lat_tiles = t_lat // ROW_TILE
    if xm is None:
        xm = _modulate(x_all, mods, n_lat_tiles)
    proj = _inproj(xm, w_in.astype(BF16))
    o_fw, o_bw = _gla(proj, lb, t_lat // GLA_CHUNK)
    y5 = _s5(proj[:, 5 * HALF:], _s5_mats(*[p.astype(F32) for p in s5p]), t_lat // S5_CHUNK)
    return _outproj([(o_fw, 0), (o_bw, 0), (proj, 4)], [(y5, 0), (proj, 5)], w_out.astype(BF16), x_all, mods,
                    ln_g, ln_b, n_lat_tiles,
                    glu_params=(s5_d.astype(F32), glu_w.astype(BF16), glu_b.astype(F32)))


def kernel(x, c, ctx, c_ctx, ada_w, ada_b, ln_g, ln_b, ab_w_in, ab_w_out, mlstm_igate_b, mlstm_fgate_b, na_rpb, cd_w_in, cd_w_out, hgrn_lb_logits, s5_a_re, s5_a_im, s5_log_dt, s5_b_re, s5_b_im, s5_c_re, s5_c_im, s5_d, s5_glu_w, s5_glu_b, peer_w_q, peer_sub_keys, peer_u, peer_v):
    t_lat, t_ctx = x.shape[1], ctx.shape[1]
    assert x.shape[0] == 1 and t_ctx == ROW_TILE and t_lat % (NA_QROWS * GRID_W) == 0
    depth = ada_w.shape[0]
    lb_soft = jax.nn.softmax(hgrn_lb_logits.astype(F32), axis=0)
    lower_bounds = jnp.cumsum(lb_soft, axis=0) - lb_soft[0]
    x_all = jnp.concatenate([x[0], ctx[0]], axis=0).astype(F32)
    n_lat_tiles = t_lat // ROW_TILE
    mods_all = [_adaln(c, c_ctx, ada_w, ada_b, l) for l in range(depth)]
    xm = _modulate(x_all, mods_all[0], n_lat_tiles)
    tables = _peer_tables(peer_u, peer_v)
    for l in range(depth):
        j = l // 2
        mods = mods_all[l]
        if l % 2 == 0:
            x1, h2t = _layer_ab(x_all, mods, t_lat, t_ctx, ab_w_in[j], ab_w_out[j], mlstm_igate_b[j],
                                mlstm_fgate_b[j], na_rpb[j], ln_g[l, 0], ln_b[l, 0], xm=xm)
        else:
            s5p = (s5_a_re[j], s5_a_im[j], s5_log_dt[j], s5_b_re[j], s5_b_im[j], s5_c_re[j], s5_c_im[j])
            x1, h2t = _layer_cd(x_all, mods, t_lat, cd_w_in[j], cd_w_out[j], lower_bounds[l], s5p, s5_d[j],
                                s5_glu_w[j], s5_glu_b[j], ln_g[l, 0], ln_b[l, 0], xm=xm)
        peer_args = (x1, h2t, mods, ln_g[l, 1], ln_b[l, 1], n_lat_tiles, peer_w_q[l], peer_sub_keys[l],
                     tables, l)
        if l + 1 < depth:
            x_all, xm = _peer_and_norm(*peer_args, mods_next=mods_all[l + 1])
        else:
            x_all = _peer_and_norm(*peer_args)
    return x_all[None].astype(x.dtype)
```

```python
import functools
import math

import numpy as np
import jax
import jax.numpy as jnp
from jax import lax
from jax.experimental import pallas as pl
from jax.experimental.pallas import tpu as pltpu

F32 = jnp.float32
BF16 = jnp.bfloat16

HEAD_DIM = 128
GRID_W = 64
N_HEADS = 8
HALF = N_HEADS * HEAD_DIM
NA_WIN_ROWS = 8
NA_WIN_COLS = 16
ROPE_BASE = 10000.0
S5_GROUP = 16
S5_STATE = 64
S5_GROUPS = HALF // S5_GROUP
PEER_HEADS = 8
PEER_N_KEYS = 128
PEER_TOPK = 16
PEER_TOP_ROWS = 24
LN_EPS = 1e-5
DEPTH = 2
ALPHA = (2.0 * DEPTH) ** 0.25
QK_SCALE = HEAD_DIM ** -0.5

ROW_TILE = 256
MM_ROW_TILE = 1280
MM_COL_TILE = 1024
SCAN_CHUNK = 128
GLA_CHUNK = 64
S5_CHUNK = 32
S5_SCAN_BLOCK = 8
NA_HEADS = 4
NA_QROWS = 8
PEER_P1_TILE = 256
PEER_TOK_TILE = 512
PEER_EXP_TILE = 512
PEER_GATE_ROWS = 64
PEER_U_SPLIT = 4
NEG = -1e30
GLA_SAFE_LOG = -80.0
VMEM_LIMIT = 56 * 1024 * 1024


def _cparams(n_axes):
    return pltpu.CompilerParams(dimension_semantics=("arbitrary",) * n_axes,
                                vmem_limit_bytes=VMEM_LIMIT)


def _nt(a, b):
    return lax.dot_general(a, b, (((1,), (1,)), ((), ())), preferred_element_type=F32)


def _tn(a, b):
    return lax.dot_general(a, b, (((0,), (0,)), ((), ())), preferred_element_type=F32)


def _nn(a, b):
    return jnp.dot(a, b, preferred_element_type=F32)


def _split3(x):
    hi = x.astype(BF16)
    r = x - hi.astype(F32)
    mid = r.astype(BF16)
    lo = (r - mid.astype(F32)).astype(BF16)
    return hi, mid, lo


def _log_sigmoid(x):
    return jnp.minimum(x, 0.0) - jnp.log1p(jnp.exp(-jnp.abs(x)))


def _sigmoid(x):
    return 1.0 / (1.0 + jnp.exp(-x))


def _gelu_tanh(x):
    return 0.5 * x * (1.0 + jnp.tanh(math.sqrt(2.0 / math.pi) * (x + 0.044715 * (x * x * x))))


def _head_norm(x):
    mu = jnp.mean(x, axis=-1, keepdims=True)
    xc = x - mu
    return xc * lax.rsqrt(jnp.mean(xc * xc, axis=-1, keepdims=True) + LN_EPS)


def _adaln_kernel(ct_ref, w_ref, b_ref, o_ref):
    ct = ct_ref[...]
    s = ct * _sigmoid(ct)
    w = w_ref[...]
    o_ref[0:1, :] = jnp.sum(s[:, 0:1] * w, axis=0, keepdims=True) + b_ref[...]
    o_ref[1:2, :] = jnp.sum(s[:, 1:2] * w, axis=0, keepdims=True) + b_ref[...]


def _adaln(c_lat, c_ctx, w_all, b_all, layer):
    n_layers, d, n = w_all.shape
    tn = n // 8
    ct = jnp.stack([c_lat.reshape(d), c_ctx.reshape(d)], axis=1)
    out = pl.pallas_call(
        _adaln_kernel,
        grid=(n // tn,),
        in_specs=[pl.BlockSpec((d, 2), lambda j: (0, 0)),
                  pl.BlockSpec((None, d, tn), lambda j: (layer, 0, j)),
                  pl.BlockSpec((None, 1, tn), lambda j: (layer, 0, j))],
        out_specs=pl.BlockSpec((2, tn), lambda j: (0, j)),
        out_shape=jax.ShapeDtypeStruct((2, n), F32),
        compiler_params=_cparams(1),
    )(ct, w_all, b_all.reshape(n_layers, 1, n))
    return jnp.pad(out.reshape(2, 6, d), ((0, 0), (0, 2), (0, 0)))


def _modulated(x, m_ref):
    return (x * (1.0 + m_ref[1:2, :]) + m_ref[0:1, :]).astype(BF16)


def _modulate_kernel(x_ref, m_ref, o_ref):
    o_ref[...] = _modulated(x_ref[...], m_ref)


def _mods_spec(d, n_lat_tiles):
    return pl.BlockSpec((None, 8, d), lambda i: (jnp.where(i < n_lat_tiles, 0, 1), 0, 0))


def _modulate(x_all, mods, n_lat_tiles):
    t_all, d = x_all.shape
    row = pl.BlockSpec((ROW_TILE, d), lambda i: (i, 0))
    return pl.pallas_call(
        _modulate_kernel,
        grid=(t_all // ROW_TILE,),
        in_specs=[row, _mods_spec(d, n_lat_tiles)],
        out_specs=row,
        out_shape=jax.ShapeDtypeStruct((t_all, d), BF16),
        compiler_params=_cparams(1),
    )(x_all, mods)


def _matmul_kernel(x_ref, w_ref, o_ref):
    o_ref[...] = _nn(x_ref[...], w_ref[...]).astype(o_ref.dtype)


def _inproj(xm, w_bf, out_dtype=F32):
    t_all, d = xm.shape
    n = w_bf.shape[1]
    tm = MM_ROW_TILE if t_all % MM_ROW_TILE == 0 else ROW_TILE
    tn = min(n, MM_COL_TILE)
    return pl.pallas_call(
        _matmul_kernel,
        grid=(t_all // tm, n // tn),
        in_specs=[pl.BlockSpec((tm, d), lambda i, j: (i, 0)),
                  pl.BlockSpec((d, tn), lambda i, j: (0, j))],
        out_specs=pl.BlockSpec((tm, tn), lambda i, j: (i, j)),
        out_shape=jax.ShapeDtypeStruct((t_all, n), out_dtype),
        compiler_params=_cparams(2),
    )(xm, w_bf)


def _ln_rows(z, g, b):
    mu = jnp.mean(z, axis=-1, keepdims=True)
    zc = z - mu
    return zc * lax.rsqrt(jnp.mean(zc * zc, axis=-1, keepdims=True) + LN_EPS) * g + b


def _outproj_kernel(*refs, glu, n_row_tiles):
    hf_ref, hb_ref, gate_ref = refs[:3]
    refs = refs[3:]
    if glu:
        (y5_ref, u5_ref, ds_ref, gw_ref, gb_ref, w1_ref, w2_ref, x_ref, m_ref, g_ref, b_ref,
         x1_ref, h2_ref) = refs
        y = _gelu_tanh(y5_ref[...] + ds_ref[...] * u5_ref[...])
        a2 = y * _sigmoid(_nn(y.astype(BF16), gw_ref[...]) + gb_ref[...])
    else:
        a2_ref, w1_ref, w2_ref, x_ref, m_ref, g_ref, b_ref, x1_ref, h2_ref = refs
        a2 = a2_ref[...]
    heads = []
    for h in range(N_HEADS):
        hs = slice(h * HEAD_DIM, (h + 1) * HEAD_DIM)
        gx = gate_ref[:, hs].astype(F32)
        gate = gx * _sigmoid(gx) if glu else _sigmoid(gx)
        heads.append((_head_norm(hf_ref[:, hs] + hb_ref[:, hs]) * gate).astype(BF16))
    a1 = jnp.concatenate(heads, axis=1)
    y = _nn(a1, w1_ref[...]) + _nn(a2.astype(BF16), w2_ref[...])
    x1 = _ln_rows(ALPHA * x_ref[...] + m_ref[2:3, :] * y, g_ref[...], b_ref[...])
    x1_ref[...] = x1
    h2t = (x1 * (1.0 + m_ref[4:5, :]) + m_ref[3:4, :]).T
    h2_ref[...] = jnp.where(pl.program_id(0) < n_row_tiles, h2t, 0.0).astype(BF16)


def _outproj(scan_parts, a2_parts, w_out_bf, x_all, mods, ln_g, ln_b, n_lat_tiles, glu_params=None):
    t_all, d = x_all.shape
    half = HALF
    n_row_tiles = t_all // ROW_TILE
    t_pad = _padded_tokens(t_all)
    row = lambda i: (jnp.minimum(i, n_row_tiles - 1), 0)
    const = lambda i: (0, 0)
    a2_parts = list(scan_parts) + list(a2_parts)
    a2_specs = [pl.BlockSpec((ROW_TILE, half), lambda i, c=c: (jnp.minimum(i, n_row_tiles - 1), c))
                for _, c in a2_parts]
    a2_args = [a for a, _ in a2_parts]
    if glu_params is not None:
        d_skip, glu_w_bf, glu_b = glu_params
        a2_specs += [pl.BlockSpec((1, half), const), pl.BlockSpec((half, half), const),
                     pl.BlockSpec((1, half), const)]
        a2_args += [d_skip.reshape(1, half), glu_w_bf, glu_b.reshape(1, half)]
    return pl.pallas_call(
        functools.partial(_outproj_kernel, glu=glu_params is not None, n_row_tiles=n_row_tiles),
        grid=(t_pad // ROW_TILE,),
        in_specs=a2_specs + [
            pl.BlockSpec((half, d), const), pl.BlockSpec((half, d), const),
            pl.BlockSpec((ROW_TILE, d), row),
            pl.BlockSpec((None, 8, d), lambda i: (jnp.where(i < n_lat_tiles, 0, 1), 0, 0)),
            pl.BlockSpec((1, d), const), pl.BlockSpec((1, d), const)],
        out_specs=[pl.BlockSpec((ROW_TILE, d), row), pl.BlockSpec((d, ROW_TILE), lambda i: (0, i))],
        out_shape=[jax.ShapeDtypeStruct((t_all, d), F32), jax.ShapeDtypeStruct((d, t_pad), BF16)],
        compiler_params=_cparams(1),
    )(*a2_args, w_out_bf[:half], w_out_bf[half:], x_all, mods, ln_g.reshape(1, d), ln_b.reshape(1, d))


def _ln2_kernel(x_ref, p_ref, m_ref, g_ref, b_ref, *rest):
    x2 = _ln_rows(ALPHA * x_ref[...] + m_ref[5:6, :] * p_ref[...].T, g_ref[...], b_ref[...])
    if len(rest) == 1:
        rest[0][...] = x2
    else:
        mn_ref, o_ref, xm_ref = rest
        o_ref[...] = x2
        xm_ref[...] = _modulated(x2, mn_ref)


def _ln2(x1, peer_out_t, mods, ln_g, ln_b, n_lat_tiles, mods_next=None):
    t_all, d = x1.shape
    row = pl.BlockSpec((ROW_TILE, d), lambda i: (i, 0))
    col = pl.BlockSpec((d, ROW_TILE), lambda i: (0, i))
    vec = pl.BlockSpec((1, d), lambda i: (0, 0))
    nxt = mods_next is not None
    n_rows = t_all if nxt else n_lat_tiles * ROW_TILE
    return pl.pallas_call(
        _ln2_kernel,
        grid=(n_rows // ROW_TILE,),
        in_specs=[row, col, _mods_spec(d, n_lat_tiles), vec, vec] + ([_mods_spec(d, n_lat_tiles)] if nxt else []),
        out_specs=[row, row] if nxt else row,
        out_shape=([jax.ShapeDtypeStruct((t_all, d), F32), jax.ShapeDtypeStruct((t_all, d), BF16)] if nxt
                   else jax.ShapeDtypeStruct((n_rows, d), F32)),
        compiler_params=_cparams(1),
    )(x1, peer_out_t, mods, ln_g.reshape(1, d), ln_b.reshape(1, d), *([mods_next] if nxt else []))


def _scan_block(rev, n_chunks, n_lat_chunks):
    if rev:
        return lambda s: n_chunks - 1 - s
    n_ctx = n_chunks - n_lat_chunks
    return lambda s: jnp.where(s < n_ctx, s + n_lat_chunks, s - n_ctx)


def _scan_mask(rev, n):
    row = lax.broadcasted_iota(jnp.int32, (n, n), 0)
    col = lax.broadcasted_iota(jnp.int32, (n, n), 1)
    return (col >= row) if rev else (col <= row)


def _rope(x, cos, sin):
    lane = lax.broadcasted_iota(jnp.int32, x.shape, 1)
    partner = jnp.where((lane % 64) < 32, pltpu.roll(x, 96, 1), pltpu.roll(x, 32, 1))
    return x * cos + partner * sin


def _mlstm_kernel(*refs):
    n_in = 5
    fw_in, bw_in = refs[:n_in], refs[n_in:2 * n_in]
    bias_ref, biast_ref, out_f, out_b = refs[2 * n_in:2 * n_in + 4]
    state = refs[2 * n_in + 4:]

    @pl.when(pl.program_id(0) == 0)
    def _():
        for r in state:
            r[...] = jnp.zeros_like(r)

    _mlstm_chunk(*fw_in, bias_ref, biast_ref, out_f, *state[:2], rev=False)
    _mlstm_chunk(*bw_in, bias_ref, biast_ref, out_b, *state[2:], rev=True)


def _mlstm_chunk(q_ref, k_ref, v_ref, g_ref, gt_ref, bias_ref, biast_ref, out_ref, cn_ref, m_ref, *, rev):
    n = q_ref.shape[0]
    mask = _scan_mask(rev, n)
    tri = mask.astype(BF16)
    pre_c = g_ref[...] + bias_ref[...]
    pre_r = gt_ref[...] + biast_ref[...]
    b_c = sum(_nn(tri, p) for p in _split3(_log_sigmoid(pre_c)))
    b_r = sum(_nt(p, tri) for p in _split3(_log_sigmoid(pre_r)))
    last = 0 if rev else n - 1
    off = N_HEADS if rev else 0
    heads = range(N_HEADS)
    bc = jnp.stack([b_c[:, 2 * N_HEADS + off + h:2 * N_HEADS + off + h + 1] for h in heads])
    br = jnp.stack([b_r[2 * N_HEADS + off + h:2 * N_HEADS + off + h + 1, :] for h in heads])
    ic = jnp.stack([pre_c[:, off + h:off + h + 1] for h in heads])
    ir = jnp.stack([pre_r[off + h:off + h + 1, :] for h in heads])
    m_prev = m_ref[:, :, 0:1]
    logw = jnp.where(mask[None], bc - br + ir, NEG)
    inter = bc + m_prev
    m_t = jnp.maximum(inter, jnp.max(logw, axis=2, keepdims=True))
    w_inter = jnp.exp(inter - m_t)
    qb, kb = _by_head(q_ref[...]), _by_head(k_ref[...])
    vb = _by_head(v_ref[...]).astype(BF16)
    v_ext = jnp.concatenate([vb, jnp.ones_like(vb)], axis=2)
    s = _bqk(qb, kb) * jnp.exp(logw - m_t)
    cn_old = cn_ref[...]
    both = _bqd(s.astype(BF16), v_ext) + _bqd(qb, cn_old.astype(BF16)) * w_inter
    den = jnp.maximum(jnp.abs(both[:, :, HEAD_DIM:]), jnp.exp(-m_t))
    out = both[:, :, :HEAD_DIM] / den
    for h in heads:
        out_ref[:, h * HEAD_DIM:(h + 1) * HEAD_DIM] = out[h]
    m_new = m_t[:, last:last + 1, :]
    b_last = bc[:, last:last + 1, :]
    w_old = jnp.exp(b_last + m_prev - m_new)
    kw = kb.astype(F32) * jnp.exp(b_last - bc + ic - m_new)
    cn_ref[...] = w_old * cn_old + jnp.einsum('hkm,hkd->hmd', kw.astype(BF16), v_ext, preferred_element_type=F32)
    m_ref[...] = jnp.broadcast_to(m_new, m_ref.shape)


def _rope_kernel(q_ref, k_ref, cos_ref, sin_ref, qo_ref, ko_ref):
    cos, sin = cos_ref[...], sin_ref[...]
    for h in range(N_HEADS):
        hs = slice(h * HEAD_DIM, (h + 1) * HEAD_DIM)
        qo_ref[:, hs] = (_rope(q_ref[:, hs].astype(F32), cos, sin) * QK_SCALE).astype(BF16)
        ko_ref[:, hs] = _rope(k_ref[:, hs].astype(F32), cos, sin).astype(BF16)


def _rope_qk(proj, cos_t, sin_t):
    t_all = proj.shape[0]
    tok = pl.BlockSpec((ROW_TILE, 128), lambda i: (i, 0))
    return pl.pallas_call(
        _rope_kernel,
        grid=(t_all // ROW_TILE,),
        in_specs=[pl.BlockSpec((ROW_TILE, HALF), lambda i: (i, 0)),
                  pl.BlockSpec((ROW_TILE, HALF), lambda i: (i, 1)), tok, tok],
        out_specs=[pl.BlockSpec((ROW_TILE, HALF), lambda i: (i, 0))] * 2,
        out_shape=[jax.ShapeDtypeStruct((t_all, HALF), BF16)] * 2,
        compiler_params=_cparams(1),
    )(proj, proj, cos_t, sin_t)


def _mlstm(proj, gates, gates_t, cos_t, sin_t, igate_b, fgate_b, n_lat_chunks):
    t_all = proj.shape[0]
    n = SCAN_CHUNK
    n_chunks = t_all // n
    bias = jnp.zeros((1, 128), F32).at[0, :4 * N_HEADS].set(
        jnp.concatenate([igate_b.reshape(-1), fgate_b.reshape(-1)]))
    q_rot, k_rot = _rope_qk(proj, cos_t, sin_t)

    def dir_specs(rev):
        blk = _scan_block(rev, n_chunks, n_lat_chunks)
        colblk = lambda c: pl.BlockSpec((n, HALF), lambda s, c=c: (blk(s), c))
        ins = [colblk(0), colblk(0), colblk(2), pl.BlockSpec((n, 128), lambda s: (blk(s), 0)),
               pl.BlockSpec((128, n), lambda s: (0, blk(s)))]
        return ins, pl.BlockSpec((n, HALF), lambda s: (blk(s), 0))

    (in_f, out_f), (in_b, out_b) = dir_specs(False), dir_specs(True)
    dir_args = [q_rot, k_rot, proj, gates, gates_t]
    state = [pltpu.VMEM((N_HEADS, HEAD_DIM, 2 * HEAD_DIM), F32), pltpu.VMEM((N_HEADS, 1, HEAD_DIM), F32)]
    return pl.pallas_call(
        _mlstm_kernel,
        grid=(n_chunks,),
        in_specs=in_f + in_b + [pl.BlockSpec((1, 128), lambda s: (0, 0)), pl.BlockSpec((128, 1), lambda s: (0, 0))],
        out_specs=[out_f, out_b],
        out_shape=[jax.ShapeDtypeStruct((t_all, HALF), F32)] * 2,
        scratch_shapes=state + state,
        compiler_params=_cparams(1),
    )(*dir_args, *dir_args, bias, bias.reshape(128, 1))


def _by_head(x):
    return jnp.stack([x[:, h * HEAD_DIM:(h + 1) * HEAD_DIM] for h in range(x.shape[1] // HEAD_DIM)])


def _bqk(a, b):
    return jnp.einsum('hqd,hkd->hqk', a, b, preferred_element_type=F32)


def _bqd(p, v):
    return jnp.einsum('hqk,hkd->hqd', p, v, preferred_element_type=F32)


def _na_kernel(q_ref, kp_ref, kc_ref, kn_ref, vp_ref, vc_ref, vn_ref, kx_ref, vx_ref, bias_ref, o_ref):
    qb = _by_head(q_ref[...].astype(F32) * QK_SCALE).astype(BF16)
    kk = _by_head(jnp.concatenate([kp_ref[...], kc_ref[...], kn_ref[...]], axis=0)).astype(BF16)
    vv = _by_head(jnp.concatenate([vp_ref[...], vc_ref[...], vn_ref[...]], axis=0)).astype(BF16)
    s_loc = _bqk(qb, kk) + bias_ref[...]
    s_ctx = _bqk(qb, _by_head(kx_ref[...]).astype(BF16))
    m = jnp.maximum(jnp.max(s_loc, axis=2, keepdims=True), jnp.max(s_ctx, axis=2, keepdims=True))
    p_loc = jnp.exp(s_loc - m)
    p_ctx = jnp.exp(s_ctx - m)
    l = jnp.sum(p_loc, axis=2, keepdims=True) + jnp.sum(p_ctx, axis=2, keepdims=True)
    out = (_bqd(p_loc.astype(BF16), vv) + _bqd(p_ctx.astype(BF16), _by_head(vx_ref[...]).astype(BF16))) / l
    for h in range(out.shape[0]):
        o_ref[:, h * HEAD_DIM:(h + 1) * HEAD_DIM] = out[h]


def _na_bias(rpb, rows):
    qn = NA_QROWS * GRID_W
    kn = 2 * qn
    ql = np.arange(qn)
    kl = np.arange(kn)
    out = []
    nqb = rows // NA_QROWS
    lr, krl = np.arange(NA_QROWS)[:, None], np.arange(2 * NA_QROWS)[None, :]
    dr = np.clip(krl - NA_QROWS // 2 - lr + NA_WIN_ROWS - 1, 0, 2 * NA_WIN_ROWS - 2)
    col = np.arange(GRID_W)
    dc = np.clip(col[None, :] - col[:, None] + NA_WIN_COLS - 1, 0, 2 * NA_WIN_COLS - 2)
    oh_r = jnp.asarray(dr[..., None] == np.arange(2 * NA_WIN_ROWS - 1), F32)
    oh_c = jnp.asarray(dc[..., None] == np.arange(2 * NA_WIN_COLS - 1), F32)
    table = jnp.einsum('hab,lka,qcb->hlqkc', rpb, oh_r, oh_c,
                       precision=lax.Precision.HIGHEST).reshape(rpb.shape[0], qn, kn)
    for qb in (0, 1, nqb - 1):
        qr = NA_QROWS * qb + ql // GRID_W
        qc = ql % GRID_W
        kr = NA_QROWS * qb - NA_QROWS // 2 + kl // GRID_W
        kc = kl % GRID_W
        r0 = np.clip(qr - NA_WIN_ROWS // 2, 0, rows - NA_WIN_ROWS)
        c0 = np.clip(qc - NA_WIN_COLS // 2, 0, GRID_W - NA_WIN_COLS)
        ok = ((kr[None, :] >= r0[:, None]) & (kr[None, :] < r0[:, None] + NA_WIN_ROWS)
              & (kc[None, :] >= c0[:, None]) & (kc[None, :] < c0[:, None] + NA_WIN_COLS)
              & (kr[None, :] >= 0) & (kr[None, :] < rows))
        out.append(jnp.where(jnp.asarray(ok)[None], table, NEG))
    return jnp.stack(out)


def _na(proj, rpb, t_lat, t_ctx):
    rows = t_lat // GRID_W
    qn = NA_QROWS * GRID_W
    hn = qn // 2
    nqb = rows // NA_QROWS
    assert nqb >= 2 and t_ctx == hn
    bias = _na_bias(rpb.astype(F32), rows)
    last_half = 2 * nqb - 1
    ctx_blk = t_lat // hn
    cls = lambda b: jnp.where(b == 0, 0, jnp.where(b == nqb - 1, 2, 1))

    nh = NA_HEADS
    w = nh * HEAD_DIM
    groups = N_HEADS // nh

    def kv_specs(col0):
        return [pl.BlockSpec((hn, w), lambda h, b: (jnp.maximum(2 * b - 1, 0), col0 + h)),
                pl.BlockSpec((qn, w), lambda h, b: (b, col0 + h)),
                pl.BlockSpec((hn, w), lambda h, b: (jnp.minimum(2 * b + 2, last_half), col0 + h))]

    qcol, kcol, vcol = 4 * groups, 5 * groups, 6 * groups
    return pl.pallas_call(
        _na_kernel,
        grid=(groups, nqb),
        in_specs=[pl.BlockSpec((qn, w), lambda h, b: (b, qcol + h))]
        + kv_specs(kcol) + kv_specs(vcol)
        + [pl.BlockSpec((hn, w), lambda h, b: (ctx_blk, kcol + h)),
           pl.BlockSpec((hn, w), lambda h, b: (ctx_blk, vcol + h)),
           pl.BlockSpec((None, nh, qn, 2 * qn), lambda h, b: (cls(b), h, 0, 0))],
        out_specs=pl.BlockSpec((qn, w), lambda h, b: (b, h)),
        out_shape=jax.ShapeDtypeStruct((t_lat, HALF), F32),
        compiler_params=_cparams(2),
    )(proj, proj, proj, proj, proj, proj, proj, proj, proj, bias)


def _ctx_attn_kernel(q_ref, k_ref, v_ref, o_ref):
    qb = (q_ref[...].astype(F32) * QK_SCALE).astype(BF16)
    s = _nt(qb, k_ref[...].astype(BF16))
    p = jnp.exp(s - jnp.max(s, axis=1, keepdims=True))
    o_ref[...] = _nn(p.astype(BF16), v_ref[...].astype(BF16)) / jnp.sum(p, axis=1, keepdims=True)


def _ctx_attn(proj, t_lat, t_ctx):
    blk = t_lat // t_ctx
    spec = lambda c: pl.BlockSpec((t_ctx, HEAD_DIM), lambda h, c=c: (blk, c * N_HEADS + h))
    return pl.pallas_call(
        _ctx_attn_kernel,
        grid=(N_HEADS,),
        in_specs=[spec(4), spec(5), spec(6)],
        out_specs=pl.BlockSpec((t_ctx, HEAD_DIM), lambda h: (0, h)),
        out_shape=jax.ShapeDtypeStruct((t_ctx, HALF), F32),
        compiler_params=_cparams(1),
    )(proj, proj, proj)


def _gla_inter(q_ref, i_ref, f_ref, lb_ref, out_ref, st_ref, k_scr, b_scr, *, rev):
    n = q_ref.shape[0]
    mask = _scan_mask(rev, n)
    tri = mask.astype(BF16)
    last = 0 if rev else n - 1
    x = f_ref[...]
    key = lb_ref[0:1, :] * _sigmoid(-x)
    la, lc = lb_ref[1:2, :], lb_ref[2:3, :] + _log_sigmoid(x)
    logf = jnp.maximum(la, lc) + jnp.log1p(jnp.exp(-jnp.abs(la - lc)))
    b = sum(_nn(tri, p) for p in _split3(logf))
    b_end = b[last:last + 1, :]
    qx = q_ref[...]
    q = qx * _sigmoid(qx)
    qt = q * jnp.exp(b)
    kdec = key * jnp.exp(b_end - b)
    w_end = jnp.exp(b_end)
    k_scr[...] = key
    b_scr[...] = b
    st = st_ref[...]
    vb = _by_head(i_ref[...]).astype(BF16)
    o_inter = _bqk(_by_head(qt).astype(BF16), st.astype(BF16))
    for h in range(N_HEADS):
        out_ref[:, h * HEAD_DIM:(h + 1) * HEAD_DIM] = o_inter[h]
    st_ref[...] = st * _by_head(w_end) + jnp.einsum('hnv,hnk->hvk', vb, _by_head(kdec).astype(BF16),
                                                    preferred_element_type=F32)
    return mask, q, b_end


def _gla_intra_factored(mask, q, b_end, i_ref, out_ref, k_scr, b_scr):
    mid = 0.5 * b_end
    qm = q * jnp.exp(b_scr[...] - mid)
    km = k_scr[...] * jnp.exp(mid - b_scr[...])
    att = jnp.where(mask[None], _bqk(_by_head(qm).astype(BF16), _by_head(km).astype(BF16)), 0.0)
    o_intra = _bqd(att.astype(BF16), _by_head(i_ref[...]).astype(BF16))
    for h in range(N_HEADS):
        out_ref[:, h * HEAD_DIM:(h + 1) * HEAD_DIM] += o_intra[h]


def _gla_intra_exact(q, i_ref, out_ref, k_scr, b_scr, *, rev):
    n = q.shape[0]
    t_idx = lax.broadcasted_iota(jnp.int32, (n, 1), 0)
    for h in range(N_HEADS):
        hs = slice(h * HEAD_DIM, (h + 1) * HEAD_DIM)
        q_h = q[:, hs]
        b_h = b_scr[:, hs]

        def body(grp, acc):
            base = pl.multiple_of(grp * 8, 8)
            b8, k8, v8 = b_scr[pl.ds(base, 8), hs], k_scr[pl.ds(base, 8), hs], i_ref[pl.ds(base, 8), hs]
            for r in range(8):
                s = base + r
                e = jnp.exp(jnp.minimum(b_h - b8[r:r + 1, :], 0.0))
                a = jnp.sum(q_h * k8[r:r + 1, :] * e, axis=1, keepdims=True)
                a = jnp.where((t_idx <= s) if rev else (t_idx >= s), a, 0.0)
                acc = acc + a * v8[r:r + 1, :]
            return acc

        out_ref[:, hs] += lax.fori_loop(0, n // 8, body, jnp.zeros((n, HEAD_DIM), F32))


def _gla_kernel(*refs):
    fw_in, bw_in, lb_ref = refs[0:3], refs[3:6], refs[6]
    out_f, out_b = refs[7:9]
    st_f, k_f, b_f, st_b, k_b, b_b = refs[9:]

    @pl.when(pl.program_id(0) == 0)
    def _():
        st_f[...] = jnp.zeros_like(st_f)
        st_b[...] = jnp.zeros_like(st_b)

    mask_f, q_f, end_f = _gla_inter(*fw_in, lb_ref, out_f, st_f, k_f, b_f, rev=False)
    mask_b, q_b, end_b = _gla_inter(*bw_in, lb_ref, out_b, st_b, k_b, b_b, rev=True)
    safe = jnp.minimum(jnp.min(end_f), jnp.min(end_b)) >= 2.0 * GLA_SAFE_LOG

    @pl.when(safe)
    def _():
        _gla_intra_factored(mask_f, q_f, end_f, fw_in[1], out_f, k_f, b_f)
        _gla_intra_factored(mask_b, q_b, end_b, bw_in[1], out_b, k_b, b_b)

    @pl.when(jnp.logical_not(safe))
    def _():
        _gla_intra_exact(q_f, fw_in[1], out_f, k_f, b_f, rev=False)
        _gla_intra_exact(q_b, bw_in[1], out_b, k_b, b_b, rev=True)


def _gla(proj, lb, n_lat_chunks):
    t_all = proj.shape[0]
    n = GLA_CHUNK
    n_chunks = t_all // n

    def dir_specs(rev):
        blk = _scan_block(rev, n_chunks, n_lat_chunks)
        colblk = lambda c: pl.BlockSpec((n, HALF), lambda s, c=c: (blk(s), c))
        return [colblk(0), colblk(1), colblk(3 if rev else 2)], pl.BlockSpec((n, HALF), lambda s: (blk(s), 0))

    (in_f, out_f), (in_b, out_b) = dir_specs(False), dir_specs(True)
    state = [pltpu.VMEM((N_HEADS, HEAD_DIM, HEAD_DIM), F32), pltpu.VMEM((n, HALF), F32),
             pltpu.VMEM((n, HALF), F32)]
    lb = lb.reshape(1, HALF)
    lb_rows = jnp.concatenate([1.0 - lb, jnp.log(lb), jnp.log1p(-lb)], axis=0)
    return pl.pallas_call(
        _gla_kernel,
        grid=(n_chunks,),
        in_specs=in_f + in_b + [pl.BlockSpec((3, HALF), lambda s: (0, 0))],
        out_specs=[out_f, out_b],
        out_shape=[jax.ShapeDtypeStruct((t_all, HALF), F32)] * 2,
        scratch_shapes=state + state,
        compiler_params=_cparams(1),
    )(*([proj] * 6), lb_rows)


def _s5_mats(a_re, a_im, log_dt, b_re, b_im, c_re, c_im):
    hp = lax.Precision.HIGHEST
    n = S5_CHUNK
    dt = jnp.exp(log_dt)[..., None]
    la_re, la_im = a_re * dt, a_im * dt
    mag = jnp.exp(la_re)
    ab_re, ab_im = mag * jnp.cos(la_im), mag * jnp.sin(la_im)
    nr, ni = ab_re - 1.0, ab_im
    den = jnp.square(a_re) + jnp.square(a_im)
    cr = (nr * a_re + ni * a_im) / den
    ci = (ni * a_re - nr * a_im) / den
    bb_re = cr[..., None] * b_re - ci[..., None] * b_im
    bb_im = cr[..., None] * b_im + ci[..., None] * b_re

    def apow(tau):
        tau = jnp.asarray(tau, F32)[None, None, :, None]
        m = jnp.exp(tau * la_re[:, :, None, :])
        return m * jnp.cos(tau * la_im[:, :, None, :]), m * jnp.sin(tau * la_im[:, :, None, :])

    def c_apow(tau):
        pr, pi = apow(tau)
        return (c_re[:, :, None] * pr[:, :, :, None] - c_im[:, :, None] * pi[:, :, :, None],
                c_re[:, :, None] * pi[:, :, :, None] + c_im[:, :, None] * pr[:, :, :, None])

    ca_re, ca_im = c_apow(np.arange(n))
    kern = jnp.einsum('dgtip,dgpj->dgtij', jnp.concatenate([ca_re, -ca_im], axis=-1),
                      jnp.concatenate([bb_re, bb_im], axis=2), precision=hp)
    s_idx, t_idx = np.arange(n)[:, None], np.arange(n)[None, :]
    g = a_re.shape[1]
    lp = n * S5_GROUP

    def toeplitz(kd, lag):
        onehot = jnp.asarray(lag[None] == np.arange(n)[:, None, None], F32)
        return jnp.einsum('lst,glij->gsjti', onehot, kd, precision=hp).reshape(g, lp, lp)

    kmat = toeplitz(kern[0], t_idx - s_idx) + toeplitz(kern[1], s_idx - t_idx)

    def emat(d, tau):
        pr, pi = apow(tau)
        pr, pi = pr[d][:, :, None, :], pi[d][:, :, None, :]
        br, bi = bb_re[d].transpose(0, 2, 1)[:, None], bb_im[d].transpose(0, 2, 1)[:, None]
        return jnp.concatenate([pr * br - pi * bi, pr * bi + pi * br], axis=-1).reshape(g, lp, 2 * S5_STATE)

    def fmat(d, tau):
        fr, fi = c_apow(tau)
        fr, fi = fr[d], fi[d]
        return jnp.concatenate([fr, -fi], axis=-1).transpose(0, 3, 1, 2).reshape(g, 2 * S5_STATE, lp)

    e_all = jnp.concatenate([emat(0, n - 1 - np.arange(n)), emat(1, np.arange(n))], axis=-1)
    f_all = jnp.concatenate([fmat(0, np.arange(n) + 1), fmat(1, n - np.arange(n))], axis=1)
    pr, pi = apow(np.array([n]))
    pr, pi = pr[:, :, 0], pi[:, :, 0]
    ar2 = jnp.concatenate([pr[0], pr[0], pr[1], pr[1]], axis=-1)
    ai2 = jnp.concatenate([-pi[0], pi[0], -pi[1], pi[1]], axis=-1)
    return kmat.astype(BF16), e_all.astype(BF16), f_all.astype(BF16), ar2, ai2


def _s5_state_kernel(u_ref, e_ref, s_ref):
    s_ref[...] = _nn(u_ref[...], e_ref[...])


def _s5_scan_kernel(sf_ref, sb_ref, ar_ref, ai_ref, hf_ref, hb_ref, stf_ref, stb_ref):
    nb = sf_ref.shape[0]
    w = 2 * S5_STATE
    ar, ai = ar_ref[...], ai_ref[...]

    @pl.when(pl.program_id(0) == 0)
    def _():
        stf_ref[...] = jnp.zeros_like(stf_ref)
        stb_ref[...] = jnp.zeros_like(stb_ref)

    def step(h, s, lo):
        return ar[:, lo:lo + w] * h + ai[:, lo:lo + w] * pltpu.roll(h, S5_STATE, 1) + s

    hf, hb = stf_ref[...], stb_ref[...]
    for j in range(nb):
        hf_ref[j] = hf
        hf = step(hf, sf_ref[j], 0)
        hb_ref[nb - 1 - j] = hb
        hb = step(hb, sb_ref[nb - 1 - j], w)
    stf_ref[...] = hf
    stb_ref[...] = hb


def _s5_out_kernel(u_ref, k_ref, h_ref, f_ref, y_ref):
    y_ref[...] = _nn(u_ref[...], k_ref[...]) + _nn(h_ref[...].astype(BF16), f_ref[...])


def _s5(u, mats, n_lat_chunks):
    kmat, emat, fmat, ar2, ai2 = mats
    t_all = u.shape[0]
    n = S5_CHUNK
    nc = t_all // n
    g = S5_GROUPS
    lp = n * S5_GROUP
    w4 = 4 * S5_STATE
    ug = u.astype(BF16).reshape(nc, n, g, S5_GROUP).transpose(2, 0, 1, 3).reshape(g, nc, lp)
    grp = lambda a, b: pl.BlockSpec((None, a, b), lambda i: (i, 0, 0))
    s_end = pl.pallas_call(
        _s5_state_kernel, grid=(g,),
        in_specs=[grp(nc, lp), grp(lp, w4)], out_specs=grp(nc, w4),
        out_shape=jax.ShapeDtypeStruct((g, nc, w4), F32), compiler_params=_cparams(1),
    )(ug, emat)
    nb = S5_SCAN_BLOCK
    w2 = 2 * S5_STATE
    fw_blk = _scan_block(False, nc // nb, n_lat_chunks // nb)
    bw_blk = _scan_block(True, nc // nb, n_lat_chunks // nb)
    s_t = s_end.transpose(1, 0, 2)
    coef = pl.BlockSpec((g, w4), lambda i: (0, 0))
    h_fw, h_bw = pl.pallas_call(
        _s5_scan_kernel, grid=(nc // nb,),
        in_specs=[pl.BlockSpec((nb, g, w2), lambda i: (fw_blk(i), 0, 0)),
                  pl.BlockSpec((nb, g, w2), lambda i: (bw_blk(i), 0, 1)), coef, coef],
        out_specs=[pl.BlockSpec((nb, g, w2), lambda i: (fw_blk(i), 0, 0)),
                   pl.BlockSpec((nb, g, w2), lambda i: (bw_blk(i), 0, 0))],
        out_shape=[jax.ShapeDtypeStruct((nc, g, w2), F32)] * 2,
        scratch_shapes=[pltpu.VMEM((g, w2), F32)] * 2,
        compiler_params=_cparams(1),
    )(s_t, s_t, ar2, ai2)
    h_in = jnp.concatenate([h_fw, h_bw], axis=-1).transpose(1, 0, 2)
    y = pl.pallas_call(
        _s5_out_kernel, grid=(g,),
        in_specs=[grp(nc, lp), grp(lp, lp), grp(nc, w4), grp(w4, lp)], out_specs=grp(nc, lp),
        out_shape=jax.ShapeDtypeStruct((g, nc, lp), F32), compiler_params=_cparams(1),
    )(ug, kmat, h_in, fmat)
    return y.reshape(g, nc, n, S5_GROUP).transpose(1, 2, 0, 3).reshape(t_all, HALF)


def _top_rows(x, k):
    rows = []
    for _ in range(k):
        mx = jnp.max(x, axis=0, keepdims=True)
        rows.append(mx)
        x = jnp.where(x == mx, NEG, x)
    return rows


def _top_rows_sorted(x, k):
    n = x.shape[0] // 8
    cols = [x[8 * i:8 * (i + 1), :] for i in range(n)]
    size = 2
    while size <= n:
        stride = size // 2
        while stride >= 1:
            for i in range(n):
                j = i ^ stride
                if j > i:
                    hi, lo = jnp.maximum(cols[i], cols[j]), jnp.minimum(cols[i], cols[j])
                    cols[i], cols[j] = (hi, lo) if (i & size) == 0 else (lo, hi)
            stride //= 2
        size *= 2
    rows = []
    for r in range(k):
        mx = jnp.max(cols[0], axis=0, keepdims=True)
        rows.append(mx)
        popped = cols[0] == mx
        live = min(n, k - r)
        for i in range(live):
            nxt = cols[i + 1] if i + 1 < n else NEG
            cols[i] = jnp.where(popped, nxt, cols[i])
    return rows


def _peer_route_kernel(ht_ref, wq_ref, keys_ref, e1_ref, e0_ref, th_ref, top_scr):
    kk = PEER_TOPK
    qt = _nn(wq_ref[...], ht_ref[...])
    for hp in range(2 * PEER_HEADS):
        sc = _nn(keys_ref[hp], qt[hp * HEAD_DIM:(hp + 1) * HEAD_DIM, :].astype(BF16))
        h, p = divmod(hp, 2)
        if p == 0:
            e0_ref[h] = sc
        else:
            e1_ref[h] = sc
        for r, mx in enumerate(_top_rows_sorted(sc, kk + 1)):
            top_scr[hp, r:r + 1, :] = mx
        top_scr[hp, kk + 1:, :] = jnp.full((PEER_TOP_ROWS - kk - 1, sc.shape[1]), NEG, F32)

    row = lax.broadcasted_iota(jnp.int32, (PEER_TOP_ROWS, 1), 0)
    for h in range(PEER_HEADS):
        a = top_scr[2 * h]
        b = top_scr[2 * h + 1]
        parts = [a[0:1] + b, a[1:2] + b[0:8], a[2:3] + b[0:8], a[3:4] + b[0:8]]
        parts += [jnp.where(row >= 4, a + b[j:j + 1], NEG) for j in range(3)]
        cand = jnp.maximum(jnp.concatenate(parts, axis=0), NEG)
        top = _top_rows(cand, kk + 1)
        thr = 0.5 * (top[kk - 1] + top[kk])
        z = jnp.sum(jnp.where(cand >= thr, jnp.exp(cand - top[0]), 0.0), axis=0, keepdims=True)
        e0_ref[h] = jnp.exp(e0_ref[h] - a[0:1]) / z
        e1_ref[h] = jnp.exp(e1_ref[h] - b[0:1])
        th_ref[h:h + 1, :] = jnp.exp(thr - top[0]) / z


def _peer_expert_kernel(ht_ref, e1_ref, e0_ref, th_ref, *rest, n_tiles):
    u_refs = rest[:PEER_U_SPLIT]
    vt_ref, o_ref, at0_scr, at1_scr, g0_scr, g1_scr = rest[PEER_U_SPLIT:]
    s = pl.program_id(1)
    assert n_tiles % 2 == 0 and n_tiles >= 4
    ni = u_refs[0].shape[0] // PEER_N_KEYS
    kc = u_refs[0].shape[1]
    at_scr, g_scr = (at0_scr, at1_scr), (g0_scr, g1_scr)

    def stages(par, dot1, gate, dot2, first=False):
        if dot1:
            at_scr[par][...] = _nn(jnp.concatenate([u_k[...] for u_k in u_refs], axis=1), ht_ref[...])
        if gate:
            at_old, g_new = at_scr[1 - par], g_scr[1 - par]
            nj, nc = PEER_GATE_ROWS, 128
            for ii in range(ni):
                i = (s - 1) * ni + ii
                e0_rows = [e0_ref[h, pl.ds(i, 1), :] for h in range(PEER_HEADS)]
                for c in range(at_old.shape[1] // nc):
                    cs = slice(c * nc, (c + 1) * nc)
                    for jb in range(PEER_N_KEYS // nj):
                        js = slice(jb * nj, (jb + 1) * nj)
                        w = None
                        for h in range(PEER_HEADS):
                            p = e1_ref[h, js, cs] * e0_rows[h][:, cs]
                            t = jnp.where(p >= th_ref[h:h + 1, cs], p, 0.0)
                            w = t if w is None else w + t
                        rs = slice(ii * PEER_N_KEYS + jb * nj, ii * PEER_N_KEYS + (jb + 1) * nj)
                        g_new[rs, cs] = (_gelu_tanh(at_old[rs, cs]) * w).astype(BF16)
        if dot2:
            prod = _nn(vt_ref[...], g_scr[par][...])
            if first:
                o_ref[...] = prod
            else:
                o_ref[...] += prod

    mid = jnp.logical_and(s > 2, s < n_tiles)
    pl.when(s == 0)(lambda: stages(0, True, False, False))
    pl.when(s == 1)(lambda: stages(1, True, True, False))
    pl.when(s == 2)(lambda: stages(0, True, True, True, first=True))
    pl.when(jnp.logical_and(mid, s % 2 == 0))(lambda: stages(0, True, True, True))
    pl.when(jnp.logical_and(mid, s % 2 == 1))(lambda: stages(1, True, True, True))
    pl.when(s == n_tiles)(lambda: stages(0, False, True, True))
    pl.when(s == n_tiles + 1)(lambda: stages(1, False, False, True))


def _tok3(t):
    return pl.BlockSpec((PEER_HEADS, PEER_N_KEYS, t), lambda i, *_: (0, 0, i))


def _peer_route(h2t, wq_t_bf, keys_bf):
    d, t_all = h2t.shape
    nh, nk = PEER_HEADS, PEER_N_KEYS
    t1 = PEER_P1_TILE
    return pl.pallas_call(
        _peer_route_kernel,
        grid=(t_all // t1,),
        in_specs=[pl.BlockSpec((d, t1), lambda i: (0, i)),
                  pl.BlockSpec(wq_t_bf.shape, lambda i: (0, 0)),
                  pl.BlockSpec(keys_bf.shape, lambda i: (0, 0, 0))],
        out_specs=[_tok3(t1), _tok3(t1), pl.BlockSpec((nh, t1), lambda i: (0, i))],
        out_shape=[jax.ShapeDtypeStruct((nh, nk, t_all), F32)] * 2 + [jax.ShapeDtypeStruct((nh, t_all), F32)],
        scratch_shapes=[pltpu.VMEM((2 * nh, PEER_TOP_ROWS, t1), F32)],
        compiler_params=_cparams(1),
    )(h2t, wq_t_bf, keys_bf)


def _padded_tokens(t_all):
    return t_all + (-t_all % PEER_TOK_TILE)


def _peer(h2t, wq_t_bf, keys_bf, u_r, vt_r, layer=0):
    return _peer_experts(h2t, *_peer_route(h2t, wq_t_bf, keys_bf), u_r, vt_r, layer)


def _peer_tables(u_all, v_all):
    return u_all.astype(BF16), v_all.transpose(0, 2, 1).astype(BF16)


def _peer_experts(h2t, e1, e0, th, u_r, vt_r, layer):
    d, t_all = h2t.shape
    tt = PEER_TOK_TILE
    et = PEER_EXP_TILE
    n_tiles = u_r.shape[1] // et
    kc = d // PEER_U_SPLIT
    tok = pl.BlockSpec((d, tt), lambda i, s: (0, i))
    return pl.pallas_call(
        functools.partial(_peer_expert_kernel, n_tiles=n_tiles),
        grid=(t_all // tt, n_tiles + 2),
        in_specs=[tok, _tok3(tt), _tok3(tt), pl.BlockSpec((PEER_HEADS, tt), lambda i, s: (0, i))]
        + [pl.BlockSpec((None, et, kc), lambda i, s, k=k: (layer, jnp.minimum(s, n_tiles - 1), k))
           for k in range(PEER_U_SPLIT)]
        + [pl.BlockSpec((None, d, et), lambda i, s: (layer, 0, jnp.clip(s - 2, 0, n_tiles - 1)))],
        out_specs=tok,
        out_shape=jax.ShapeDtypeStruct((d, t_all), F32),
        scratch_shapes=[pltpu.VMEM((et, tt), F32)] * 2 + [pltpu.VMEM((et, tt), BF16)] * 2,
        compiler_params=_cparams(2),
    )(h2t, e1, e0, th, *([u_r] * PEER_U_SPLIT), vt_r)


def _rope_tables(t_lat, t_ctx):
    pos = np.arange(t_lat)
    nfreq = HEAD_DIM // 4
    freqs = ROPE_BASE ** (-jnp.arange(nfreq, dtype=F32) / nfreq)
    ang_r = jnp.asarray(pos // GRID_W, F32)[:, None] * freqs
    ang_c = jnp.asarray(pos % GRID_W, F32)[:, None] * freqs
    cos = jnp.concatenate([jnp.cos(ang_r), jnp.cos(ang_r), jnp.cos(ang_c), jnp.cos(ang_c)], axis=1)
    sin = jnp.concatenate([-jnp.sin(ang_r), jnp.sin(ang_r), -jnp.sin(ang_c), jnp.sin(ang_c)], axis=1)
    cos = jnp.concatenate([cos, jnp.ones((t_ctx, HEAD_DIM), F32)], axis=0)
    sin = jnp.concatenate([sin, jnp.zeros((t_ctx, HEAD_DIM), F32)], axis=0)
    return cos, sin


def _peer_and_norm(x1, h2t, mods, ln_g, ln_b, n_lat_tiles, w_q, sub_keys, tables, layer, mods_next=None):
    keys = sub_keys.reshape(2 * PEER_HEADS, PEER_N_KEYS, -1).astype(BF16)
    pt = _peer(h2t, w_q.T.astype(BF16), keys, *tables, layer)
    return _ln2(x1, pt, mods, ln_g, ln_b, n_lat_tiles, mods_next)


def _layer_ab(x_all, mods, t_lat, t_ctx, w_in, w_out, igate_b, fgate_b, rpb, ln_g, ln_b, xm=None):
    n_lat_tiles = t_lat // ROW_TILE
    if xm is None:
        xm = _modulate(x_all, mods, n_lat_tiles)
    ng = 4 * N_HEADS
    w_main = jnp.concatenate([w_in[:, :4 * HALF], w_in[:, 4 * HALF + ng:]], axis=1).astype(BF16)
    w_gate = jnp.pad(w_in[:, 4 * HALF:4 * HALF + ng], ((0, 0), (0, 128 - ng))).astype(BF16)
    proj = _inproj(xm, w_main, BF16)
    gates = _inproj(xm, w_gate)
    cos_t, sin_t = _rope_tables(t_lat, t_ctx)
    h_fw, h_bw = _mlstm(proj, gates, gates.T, cos_t, sin_t, igate_b, fgate_b, t_lat // SCAN_CHUNK)
    y_na = jnp.concatenate([_na(proj, rpb, t_lat, t_ctx), _ctx_attn(proj, t_lat, t_ctx)], axis=0)
    return _outproj([(h_fw, 0), (h_bw, 0), (proj, 3)], [(y_na, 0)], w_out.astype(BF16), x_all, mods,
                    ln_g, ln_b, n_lat_tiles)


def _layer_cd(x_all, mods, t_lat, w_in, w_out, lb, s5p, s5_d, glu_w, glu_b, ln_g, ln_b, xm=None):
    n_lat_tiles = t_lat // ROW_TILE
    if xm is None:
        xm = _modulate(x_all, mods, n_lat_tiles)
    proj = _inproj(xm, w_in.astype(BF16))
    o_fw, o_bw = _gla(proj, lb, t_lat // GLA_CHUNK)
    y5 = _s5(proj[:, 5 * HALF:], _s5_mats(*[p.astype(F32) for p in s5p]), t_lat // S5_CHUNK)
    return _outproj([(o_fw, 0), (o_bw, 0), (proj, 4)], [(y5, 0), (proj, 5)], w_out.astype(BF16), x_all, mods,
                    ln_g, ln_b, n_lat_tiles,
                    glu_params=(s5_d.astype(F32), glu_w.astype(BF16), glu_b.astype(F32)))


def kernel(x, c, ctx, c_ctx, ada_w, ada_b, ln_g, ln_b, ab_w_in, ab_w_out, mlstm_igate_b, mlstm_fgate_b, na_rpb, cd_w_in, cd_w_out, hgrn_lb_logits, s5_a_re, s5_a_im, s5_log_dt, s5_b_re, s5_b_im, s5_c_re, s5_c_im, s5_d, s5_glu_w, s5_glu_b, peer_w_q, peer_sub_keys, peer_u, peer_v):
    t_lat, t_ctx = x.shape[1], ctx.shape[1]
    assert x.shape[0] == 1 and t_ctx == ROW_TILE and t_lat % (NA_QROWS * GRID_W) == 0
    depth = ada_w.shape[0]
    lb_soft = jax.nn.softmax(hgrn_lb_logits.astype(F32), axis=0)
    lower_bounds = jnp.cumsum(lb_soft, axis=0) - lb_soft[0]
    x_all = jnp.concatenate([x[0], ctx[0]], axis=0).astype(F32)
    n_lat_tiles = t_lat // ROW_TILE
    mods_all = [_adaln(c, c_ctx, ada_w, ada_b, l) for l in range(depth)]
    xm = _modulate(x_all, mods_all[0], n_lat_tiles)
    tables = _peer_tables(peer_u, peer_v)
    for l in range(depth):
        j = l // 2
        mods = mods_all[l]
        if l % 2 == 0:
            x1, h2t = _layer_ab(x_all, mods, t_lat, t_ctx, ab_w_in[j], ab_w_out[j], mlstm_igate_b[j],
                                mlstm_fgate_b[j], na_rpb[j], ln_g[l, 0], ln_b[l, 0], xm=xm)
        else:
            s5p = (s5_a_re[j], s5_a_im[j], s5_log_dt[j], s5_b_re[j], s5_b_im[j], s5_c_re[j], s5_c_im[j])
            x1, h2t = _layer_cd(x_all, mods, t_lat, cd_w_in[j], cd_w_out[j], lower_bounds[l], s5p, s5_d[j],
                                s5_glu_w[j], s5_glu_b[j], ln_g[l, 0], ln_b[l, 0], xm=xm)
        peer_args = (x1, h2t, mods, ln_g[l, 1], ln_b[l, 1], n_lat_tiles, peer_w_q[l], peer_sub_keys[l],
                     tables, l)
        if l + 1 < depth:
            x_all, xm = _peer_and_norm(*peer_args, mods_next=mods_all[l + 1])
        else:
            x_all = _peer_and_norm(*peer_args)
    return x_all[None].astype(x.dtype)
```

```python
import functools
import math

import numpy as np
import jax
import jax.numpy as jnp
from jax import lax
from jax.experimental import pallas as pl
from jax.experimental.pallas import tpu as pltpu

F32 = jnp.float32
BF16 = jnp.bfloat16

HEAD_DIM = 128
GRID_W = 64
N_HEADS = 8
HALF = N_HEADS * HEAD_DIM
NA_WIN_ROWS = 8
NA_WIN_COLS = 16
ROPE_BASE = 10000.0
S5_GROUP = 16
S5_STATE = 64
S5_GROUPS = HALF // S5_GROUP
PEER_HEADS = 8
PEER_N_KEYS = 128
PEER_TOPK = 16
PEER_TOP_ROWS = 24
LN_EPS = 1e-5
DEPTH = 2
ALPHA = (2.0 * DEPTH) ** 0.25
QK_SCALE = HEAD_DIM ** -0.5

ROW_TILE = 256
MM_ROW_TILE = 1280
MM_COL_TILE = 1024
SCAN_CHUNK = 128
GLA_CHUNK = 64
S5_CHUNK = 32
S5_SCAN_BLOCK = 8
NA_HEADS = 4
NA_QROWS = 8
PEER_P1_TILE = 256
PEER_TOK_TILE = 512
PEER_EXP_TILE = 512
PEER_GATE_ROWS = 32
PEER_U_SPLIT = 4
NEG = -1e30
GLA_SAFE_LOG = -80.0
VMEM_LIMIT = 56 * 1024 * 1024


def _cparams(n_axes):
    return pltpu.CompilerParams(dimension_semantics=("arbitrary",) * n_axes,
                                vmem_limit_bytes=VMEM_LIMIT)


def _nt(a, b):
    return lax.dot_general(a, b, (((1,), (1,)), ((), ())), preferred_element_type=F32)


def _tn(a, b):
    return lax.dot_general(a, b, (((0,), (0,)), ((), ())), preferred_element_type=F32)


def _nn(a, b):
    return jnp.dot(a, b, preferred_element_type=F32)


def _split3(x):
    hi = x.astype(BF16)
    r = x - hi.astype(F32)
    mid = r.astype(BF16)
    lo = (r - mid.astype(F32)).astype(BF16)
    return hi, mid, lo


def _log_sigmoid(x):
    return jnp.minimum(x, 0.0) - jnp.log1p(jnp.exp(-jnp.abs(x)))


def _sigmoid(x):
    return 1.0 / (1.0 + jnp.exp(-x))


def _gelu_tanh(x):
    return 0.5 * x * (1.0 + jnp.tanh(math.sqrt(2.0 / math.pi) * (x + 0.044715 * (x * x * x))))


def _head_norm(x):
    mu = jnp.mean(x, axis=-1, keepdims=True)
    xc = x - mu
    return xc * lax.rsqrt(jnp.mean(xc * xc, axis=-1, keepdims=True) + LN_EPS)


def _adaln_kernel(ct_ref, w_ref, b_ref, o_ref):
    ct = ct_ref[...]
    s = ct * _sigmoid(ct)
    w = w_ref[...]
    o_ref[0:1, :] = jnp.sum(s[:, 0:1] * w, axis=0, keepdims=True) + b_ref[...]
    o_ref[1:2, :] = jnp.sum(s[:, 1:2] * w, axis=0, keepdims=True) + b_ref[...]


def _adaln(c_lat, c_ctx, w_all, b_all, layer):
    n_layers, d, n = w_all.shape
    tn = n // 8
    ct = jnp.stack([c_lat.reshape(d), c_ctx.reshape(d)], axis=1)
    out = pl.pallas_call(
        _adaln_kernel,
        grid=(n // tn,),
        in_specs=[pl.BlockSpec((d, 2), lambda j: (0, 0)),
                  pl.BlockSpec((None, d, tn), lambda j: (layer, 0, j)),
                  pl.BlockSpec((None, 1, tn), lambda j: (layer, 0, j))],
        out_specs=pl.BlockSpec((2, tn), lambda j: (0, j)),
        out_shape=jax.ShapeDtypeStruct((2, n), F32),
        compiler_params=_cparams(1),
    )(ct, w_all, b_all.reshape(n_layers, 1, n))
    return jnp.pad(out.reshape(2, 6, d), ((0, 0), (0, 2), (0, 0)))


def _modulated(x, m_ref):
    return (x * (1.0 + m_ref[1:2, :]) + m_ref[0:1, :]).astype(BF16)


def _modulate_kernel(x_ref, m_ref, o_ref):
    o_ref[...] = _modulated(x_ref[...], m_ref)


def _mods_spec(d, n_lat_tiles):
    return pl.BlockSpec((None, 8, d), lambda i: (jnp.where(i < n_lat_tiles, 0, 1), 0, 0))


def _modulate(x_all, mods, n_lat_tiles):
    t_all, d = x_all.shape
    row = pl.BlockSpec((ROW_TILE, d), lambda i: (i, 0))
    return pl.pallas_call(
        _modulate_kernel,
        grid=(t_all // ROW_TILE,),
        in_specs=[row, _mods_spec(d, n_lat_tiles)],
        out_specs=row,
        out_shape=jax.ShapeDtypeStruct((t_all, d), BF16),
        compiler_params=_cparams(1),
    )(x_all, mods)


def _matmul_kernel(x_ref, w_ref, o_ref):
    o_ref[...] = _nn(x_ref[...], w_ref[...]).astype(o_ref.dtype)


def _inproj(xm, w_bf, out_dtype=F32):
    t_all, d = xm.shape
    n = w_bf.shape[1]
    tm = MM_ROW_TILE if t_all % MM_ROW_TILE == 0 else ROW_TILE
    tn = min(n, MM_COL_TILE)
    return pl.pallas_call(
        _matmul_kernel,
        grid=(t_all // tm, n // tn),
        in_specs=[pl.BlockSpec((tm, d), lambda i, j: (i, 0)),
                  pl.BlockSpec((d, tn), lambda i, j: (0, j))],
        out_specs=pl.BlockSpec((tm, tn), lambda i, j: (i, j)),
        out_shape=jax.ShapeDtypeStruct((t_all, n), out_dtype),
        compiler_params=_cparams(2),
    )(xm, w_bf)


def _ln_rows(z, g, b):
    mu = jnp.mean(z, axis=-1, keepdims=True)
    zc = z - mu
    return zc * lax.rsqrt(jnp.mean(zc * zc, axis=-1, keepdims=True) + LN_EPS) * g + b


def _outproj_kernel(*refs, glu, n_row_tiles):
    hf_ref, hb_ref, gate_ref = refs[:3]
    refs = refs[3:]
    if glu:
        (y5_ref, u5_ref, ds_ref, gw_ref, gb_ref, w1_ref, w2_ref, x_ref, m_ref, g_ref, b_ref,
         x1_ref, h2_ref) = refs
        y = _gelu_tanh(y5_ref[...] + ds_ref[...] * u5_ref[...])
        a2 = y * _sigmoid(_nn(y.astype(BF16), gw_ref[...]) + gb_ref[...])
    else:
        a2_ref, w1_ref, w2_ref, x_ref, m_ref, g_ref, b_ref, x1_ref, h2_ref = refs
        a2 = a2_ref[...]
    heads = []
    for h in range(N_HEADS):
        hs = slice(h * HEAD_DIM, (h + 1) * HEAD_DIM)
        gx = gate_ref[:, hs].astype(F32)
        gate = gx * _sigmoid(gx) if glu else _sigmoid(gx)
        heads.append((_head_norm(hf_ref[:, hs] + hb_ref[:, hs]) * gate).astype(BF16))
    a1 = jnp.concatenate(heads, axis=1)
    y = _nn(a1, w1_ref[...]) + _nn(a2.astype(BF16), w2_ref[...])
    x1 = _ln_rows(ALPHA * x_ref[...] + m_ref[2:3, :] * y, g_ref[...], b_ref[...])
    x1_ref[...] = x1
    h2t = (x1 * (1.0 + m_ref[4:5, :]) + m_ref[3:4, :]).T
    h2_ref[...] = jnp.where(pl.program_id(0) < n_row_tiles, h2t, 0.0).astype(BF16)


def _outproj(scan_parts, a2_parts, w_out_bf, x_all, mods, ln_g, ln_b, n_lat_tiles, glu_params=None):
    t_all, d = x_all.shape
    half = HALF
    n_row_tiles = t_all // ROW_TILE
    t_pad = _padded_tokens(t_all)
    row = lambda i: (jnp.minimum(i, n_row_tiles - 1), 0)
    const = lambda i: (0, 0)
    a2_parts = list(scan_parts) + list(a2_parts)
    a2_specs = [pl.BlockSpec((ROW_TILE, half), lambda i, c=c: (jnp.minimum(i, n_row_tiles - 1), c))
                for _, c in a2_parts]
    a2_args = [a for a, _ in a2_parts]
    if glu_params is not None:
        d_skip, glu_w_bf, glu_b = glu_params
        a2_specs += [pl.BlockSpec((1, half), const), pl.BlockSpec((half, half), const),
                     pl.BlockSpec((1, half), const)]
        a2_args += [d_skip.reshape(1, half), glu_w_bf, glu_b.reshape(1, half)]
    return pl.pallas_call(
        functools.partial(_outproj_kernel, glu=glu_params is not None, n_row_tiles=n_row_tiles),
        grid=(t_pad // ROW_TILE,),
        in_specs=a2_specs + [
            pl.BlockSpec((half, d), const), pl.BlockSpec((half, d), const),
            pl.BlockSpec((ROW_TILE, d), row),
            pl.BlockSpec((None, 8, d), lambda i: (jnp.where(i < n_lat_tiles, 0, 1), 0, 0)),
            pl.BlockSpec((1, d), const), pl.BlockSpec((1, d), const)],
        out_specs=[pl.BlockSpec((ROW_TILE, d), row), pl.BlockSpec((d, ROW_TILE), lambda i: (0, i))],
        out_shape=[jax.ShapeDtypeStruct((t_all, d), F32), jax.ShapeDtypeStruct((d, t_pad), BF16)],
        compiler_params=_cparams(1),
    )(*a2_args, w_out_bf[:half], w_out_bf[half:], x_all, mods, ln_g.reshape(1, d), ln_b.reshape(1, d))


def _ln2_kernel(x_ref, p_ref, m_ref, g_ref, b_ref, *rest):
    x2 = _ln_rows(ALPHA * x_ref[...] + m_ref[5:6, :] * p_ref[...].T, g_ref[...], b_ref[...])
    if len(rest) == 1:
        rest[0][...] = x2
    else:
        mn_ref, o_ref, xm_ref = rest
        o_ref[...] = x2
        xm_ref[...] = _modulated(x2, mn_ref)


def _ln2(x1, peer_out_t, mods, ln_g, ln_b, n_lat_tiles, mods_next=None):
    t_all, d = x1.shape
    row = pl.BlockSpec((ROW_TILE, d), lambda i: (i, 0))
    col = pl.BlockSpec((d, ROW_TILE), lambda i: (0, i))
    vec = pl.BlockSpec((1, d), lambda i: (0, 0))
    nxt = mods_next is not None
    n_rows = t_all if nxt else n_lat_tiles * ROW_TILE
    return pl.pallas_call(
        _ln2_kernel,
        grid=(n_rows // ROW_TILE,),
        in_specs=[row, col, _mods_spec(d, n_lat_tiles), vec, vec] + ([_mods_spec(d, n_lat_tiles)] if nxt else []),
        out_specs=[row, row] if nxt else row,
        out_shape=([jax.ShapeDtypeStruct((t_all, d), F32), jax.ShapeDtypeStruct((t_all, d), BF16)] if nxt
                   else jax.ShapeDtypeStruct((n_rows, d), F32)),
        compiler_params=_cparams(1),
    )(x1, peer_out_t, mods, ln_g.reshape(1, d), ln_b.reshape(1, d), *([mods_next] if nxt else []))


def _scan_block(rev, n_chunks, n_lat_chunks):
    if rev:
        return lambda s: n_chunks - 1 - s
    n_ctx = n_chunks - n_lat_chunks
    return lambda s: jnp.where(s < n_ctx, s + n_lat_chunks, s - n_ctx)


def _scan_mask(rev, n):
    row = lax.broadcasted_iota(jnp.int32, (n, n), 0)
    col = lax.broadcasted_iota(jnp.int32, (n, n), 1)
    return (col >= row) if rev else (col <= row)


def _rope(x, cos, sin):
    lane = lax.broadcasted_iota(jnp.int32, x.shape, 1)
    partner = jnp.where((lane % 64) < 32, pltpu.roll(x, 96, 1), pltpu.roll(x, 32, 1))
    return x * cos + partner * sin


def _mlstm_kernel(*refs):
    n_in = 5
    fw_in, bw_in = refs[:n_in], refs[n_in:2 * n_in]
    bias_ref, biast_ref, out_f, out_b = refs[2 * n_in:2 * n_in + 4]
    state = refs[2 * n_in + 4:]

    @pl.when(pl.program_id(0) == 0)
    def _():
        for r in state:
            r[...] = jnp.zeros_like(r)

    _mlstm_chunk(*fw_in, bias_ref, biast_ref, out_f, *state[:2], rev=False)
    _mlstm_chunk(*bw_in, bias_ref, biast_ref, out_b, *state[2:], rev=True)


def _mlstm_chunk(q_ref, k_ref, v_ref, g_ref, gt_ref, bias_ref, biast_ref, out_ref, cn_ref, m_ref, *, rev):
    n = q_ref.shape[0]
    mask = _scan_mask(rev, n)
    tri = mask.astype(BF16)
    pre_c = g_ref[...] + bias_ref[...]
    pre_r = gt_ref[...] + biast_ref[...]
    b_c = sum(_nn(tri, p) for p in _split3(_log_sigmoid(pre_c)))
    b_r = sum(_nt(p, tri) for p in _split3(_log_sigmoid(pre_r)))
    last = 0 if rev else n - 1
    off = N_HEADS if rev else 0
    heads = range(N_HEADS)
    bc = jnp.stack([b_c[:, 2 * N_HEADS + off + h:2 * N_HEADS + off + h + 1] for h in heads])
    br = jnp.stack([b_r[2 * N_HEADS + off + h:2 * N_HEADS + off + h + 1, :] for h in heads])
    ic = jnp.stack([pre_c[:, off + h:off + h + 1] for h in heads])
    ir = jnp.stack([pre_r[off + h:off + h + 1, :] for h in heads])
    m_prev = m_ref[:, :, 0:1]
    logw = jnp.where(mask[None], bc - br + ir, NEG)
    inter = bc + m_prev
    m_t = jnp.maximum(inter, jnp.max(logw, axis=2, keepdims=True))
    w_inter = jnp.exp(inter - m_t)
    qb, kb = _by_head(q_ref[...]), _by_head(k_ref[...])
    vb = _by_head(v_ref[...]).astype(BF16)
    v_ext = jnp.concatenate([vb, jnp.ones_like(vb)], axis=2)
    s = _bqk(qb, kb) * jnp.exp(logw - m_t)
    cn_old = cn_ref[...]
    both = _bqd(s.astype(BF16), v_ext) + _bqd(qb, cn_old.astype(BF16)) * w_inter
    den = jnp.maximum(jnp.abs(both[:, :, HEAD_DIM:]), jnp.exp(-m_t))
    out = both[:, :, :HEAD_DIM] / den
    for h in heads:
        out_ref[:, h * HEAD_DIM:(h + 1) * HEAD_DIM] = out[h]
    m_new = m_t[:, last:last + 1, :]
    b_last = bc[:, last:last + 1, :]
    w_old = jnp.exp(b_last + m_prev - m_new)
    kw = kb.astype(F32) * jnp.exp(b_last - bc + ic - m_new)
    cn_ref[...] = w_old * cn_old + jnp.einsum('hkm,hkd->hmd', kw.astype(BF16), v_ext, preferred_element_type=F32)
    m_ref[...] = jnp.broadcast_to(m_new, m_ref.shape)


def _rope_kernel(q_ref, k_ref, cos_ref, sin_ref, qo_ref, ko_ref):
    cos, sin = cos_ref[...], sin_ref[...]
    for h in range(N_HEADS):
        hs = slice(h * HEAD_DIM, (h + 1) * HEAD_DIM)
        qo_ref[:, hs] = (_rope(q_ref[:, hs].astype(F32), cos, sin) * QK_SCALE).astype(BF16)
        ko_ref[:, hs] = _rope(k_ref[:, hs].astype(F32), cos, sin).astype(BF16)


def _rope_qk(proj, cos_t, sin_t):
    t_all = proj.shape[0]
    tok = pl.BlockSpec((ROW_TILE, 128), lambda i: (i, 0))
    return pl.pallas_call(
        _rope_kernel,
        grid=(t_all // ROW_TILE,),
        in_specs=[pl.BlockSpec((ROW_TILE, HALF), lambda i: (i, 0)),
                  pl.BlockSpec((ROW_TILE, HALF), lambda i: (i, 1)), tok, tok],
        out_specs=[pl.BlockSpec((ROW_TILE, HALF), lambda i: (i, 0))] * 2,
        out_shape=[jax.ShapeDtypeStruct((t_all, HALF), BF16)] * 2,
        compiler_params=_cparams(1),
    )(proj, proj, cos_t, sin_t)


def _mlstm(proj, gates, gates_t, cos_t, sin_t, igate_b, fgate_b, n_lat_chunks):
    t_all = proj.shape[0]
    n = SCAN_CHUNK
    n_chunks = t_all // n
    bias = jnp.zeros((1, 128), F32).at[0, :4 * N_HEADS].set(
        jnp.concatenate([igate_b.reshape(-1), fgate_b.reshape(-1)]))
    q_rot, k_rot = _rope_qk(proj, cos_t, sin_t)

    def dir_specs(rev):
        blk = _scan_block(rev, n_chunks, n_lat_chunks)
        colblk = lambda c: pl.BlockSpec((n, HALF), lambda s, c=c: (blk(s), c))
        ins = [colblk(0), colblk(0), colblk(2), pl.BlockSpec((n, 128), lambda s: (blk(s), 0)),
               pl.BlockSpec((128, n), lambda s: (0, blk(s)))]
        return ins, pl.BlockSpec((n, HALF), lambda s: (blk(s), 0))

    (in_f, out_f), (in_b, out_b) = dir_specs(False), dir_specs(True)
    dir_args = [q_rot, k_rot, proj, gates, gates_t]
    state = [pltpu.VMEM((N_HEADS, HEAD_DIM, 2 * HEAD_DIM), F32), pltpu.VMEM((N_HEADS, 1, HEAD_DIM), F32)]
    return pl.pallas_call(
        _mlstm_kernel,
        grid=(n_chunks,),
        in_specs=in_f + in_b + [pl.BlockSpec((1, 128), lambda s: (0, 0)), pl.BlockSpec((128, 1), lambda s: (0, 0))],
        out_specs=[out_f, out_b],
        out_shape=[jax.ShapeDtypeStruct((t_all, HALF), F32)] * 2,
        scratch_shapes=state + state,
        compiler_params=_cparams(1),
    )(*dir_args, *dir_args, bias, bias.reshape(128, 1))


def _by_head(x):
    return jnp.stack([x[:, h * HEAD_DIM:(h + 1) * HEAD_DIM] for h in range(x.shape[1] // HEAD_DIM)])


def _bqk(a, b):
    return jnp.einsum('hqd,hkd->hqk', a, b, preferred_element_type=F32)


def _bqd(p, v):
    return jnp.einsum('hqk,hkd->hqd', p, v, preferred_element_type=F32)


def _na_kernel(q_ref, kp_ref, kc_ref, kn_ref, vp_ref, vc_ref, vn_ref, kx_ref, vx_ref, bias_ref, o_ref):
    qb = _by_head(q_ref[...].astype(F32) * QK_SCALE).astype(BF16)
    kk = _by_head(jnp.concatenate([kp_ref[...], kc_ref[...], kn_ref[...]], axis=0)).astype(BF16)
    vv = _by_head(jnp.concatenate([vp_ref[...], vc_ref[...], vn_ref[...]], axis=0)).astype(BF16)
    s_loc = _bqk(qb, kk) + bias_ref[...]
    s_ctx = _bqk(qb, _by_head(kx_ref[...]).astype(BF16))
    m = jnp.maximum(jnp.max(s_loc, axis=2, keepdims=True), jnp.max(s_ctx, axis=2, keepdims=True))
    p_loc = jnp.exp(s_loc - m)
    p_ctx = jnp.exp(s_ctx - m)
    l = jnp.sum(p_loc, axis=2, keepdims=True) + jnp.sum(p_ctx, axis=2, keepdims=True)
    out = (_bqd(p_loc.astype(BF16), vv) + _bqd(p_ctx.astype(BF16), _by_head(vx_ref[...]).astype(BF16))) / l
    for h in range(out.shape[0]):
        o_ref[:, h * HEAD_DIM:(h + 1) * HEAD_DIM] = out[h]


def _na_bias(rpb, rows):
    qn = NA_QROWS * GRID_W
    kn = 2 * qn
    ql = np.arange(qn)
    kl = np.arange(kn)
    out = []
    nqb = rows // NA_QROWS
    lr, krl = np.arange(NA_QROWS)[:, None], np.arange(2 * NA_QROWS)[None, :]
    dr = np.clip(krl - NA_QROWS // 2 - lr + NA_WIN_ROWS - 1, 0, 2 * NA_WIN_ROWS - 2)
    col = np.arange(GRID_W)
    dc = np.clip(col[None, :] - col[:, None] + NA_WIN_COLS - 1, 0, 2 * NA_WIN_COLS - 2)
    oh_r = jnp.asarray(dr[..., None] == np.arange(2 * NA_WIN_ROWS - 1), F32)
    oh_c = jnp.asarray(dc[..., None] == np.arange(2 * NA_WIN_COLS - 1), F32)
    table = jnp.einsum('hab,lka,qcb->hlqkc', rpb, oh_r, oh_c,
                       precision=lax.Precision.HIGHEST).reshape(rpb.shape[0], qn, kn)
    for qb in (0, 1, nqb - 1):
        qr = NA_QROWS * qb + ql // GRID_W
        qc = ql % GRID_W
        kr = NA_QROWS * qb - NA_QROWS // 2 + kl // GRID_W
        kc = kl % GRID_W
        r0 = np.clip(qr - NA_WIN_ROWS // 2, 0, rows - NA_WIN_ROWS)
        c0 = np.clip(qc - NA_WIN_COLS // 2, 0, GRID_W - NA_WIN_COLS)
        ok = ((kr[None, :] >= r0[:, None]) & (kr[None, :] < r0[:, None] + NA_WIN_ROWS)
              & (kc[None, :] >= c0[:, None]) & (kc[None, :] < c0[:, None] + NA_WIN_COLS)
              & (kr[None, :] >= 0) & (kr[None, :] < rows))
        out.append(jnp.where(jnp.asarray(ok)[None], table, NEG))
    return jnp.stack(out)


def _na(proj, rpb, t_lat, t_ctx):
    rows = t_lat // GRID_W
    qn = NA_QROWS * GRID_W
    hn = qn // 2
    nqb = rows // NA_QROWS
    assert nqb >= 2 and t_ctx == hn
    bias = _na_bias(rpb.astype(F32), rows)
    last_half = 2 * nqb - 1
    ctx_blk = t_lat // hn
    cls = lambda b: jnp.where(b == 0, 0, jnp.where(b == nqb - 1, 2, 1))

    nh = NA_HEADS
    w = nh * HEAD_DIM
    groups = N_HEADS // nh

    def kv_specs(col0):
        return [pl.BlockSpec((hn, w), lambda h, b: (jnp.maximum(2 * b - 1, 0), col0 + h)),
                pl.BlockSpec((qn, w), lambda h, b: (b, col0 + h)),
                pl.BlockSpec((hn, w), lambda h, b: (jnp.minimum(2 * b + 2, last_half), col0 + h))]

    qcol, kcol, vcol = 4 * groups, 5 * groups, 6 * groups
    return pl.pallas_call(
        _na_kernel,
        grid=(groups, nqb),
        in_specs=[pl.BlockSpec((qn, w), lambda h, b: (b, qcol + h))]
        + kv_specs(kcol) + kv_specs(vcol)
        + [pl.BlockSpec((hn, w), lambda h, b: (ctx_blk, kcol + h)),
           pl.BlockSpec((hn, w), lambda h, b: (ctx_blk, vcol + h)),
           pl.BlockSpec((None, nh, qn, 2 * qn), lambda h, b: (cls(b), h, 0, 0))],
        out_specs=pl.BlockSpec((qn, w), lambda h, b: (b, h)),
        out_shape=jax.ShapeDtypeStruct((t_lat, HALF), F32),
        compiler_params=_cparams(2),
    )(proj, proj, proj, proj, proj, proj, proj, proj, proj, bias)


def _ctx_attn_kernel(q_ref, k_ref, v_ref, o_ref):
    qb = (q_ref[...].astype(F32) * QK_SCALE).astype(BF16)
    s = _nt(qb, k_ref[...].astype(BF16))
    p = jnp.exp(s - jnp.max(s, axis=1, keepdims=True))
    o_ref[...] = _nn(p.astype(BF16), v_ref[...].astype(BF16)) / jnp.sum(p, axis=1, keepdims=True)


def _ctx_attn(proj, t_lat, t_ctx):
    blk = t_lat // t_ctx
    spec = lambda c: pl.BlockSpec((t_ctx, HEAD_DIM), lambda h, c=c: (blk, c * N_HEADS + h))
    return pl.pallas_call(
        _ctx_attn_kernel,
        grid=(N_HEADS,),
        in_specs=[spec(4), spec(5), spec(6)],
        out_specs=pl.BlockSpec((t_ctx, HEAD_DIM), lambda h: (0, h)),
        out_shape=jax.ShapeDtypeStruct((t_ctx, HALF), F32),
        compiler_params=_cparams(1),
    )(proj, proj, proj)


def _gla_inter(q_ref, i_ref, f_ref, lb_ref, out_ref, st_ref, k_scr, b_scr, *, rev):
    n = q_ref.shape[0]
    mask = _scan_mask(rev, n)
    tri = mask.astype(BF16)
    last = 0 if rev else n - 1
    x = f_ref[...]
    key = lb_ref[0:1, :] * _sigmoid(-x)
    la, lc = lb_ref[1:2, :], lb_ref[2:3, :] + _log_sigmoid(x)
    logf = jnp.maximum(la, lc) + jnp.log1p(jnp.exp(-jnp.abs(la - lc)))
    b = sum(_nn(tri, p) for p in _split3(logf))
    b_end = b[last:last + 1, :]
    qx = q_ref[...]
    q = qx * _sigmoid(qx)
    qt = q * jnp.exp(b)
    kdec = key * jnp.exp(b_end - b)
    w_end = jnp.exp(b_end)
    k_scr[...] = key
    b_scr[...] = b
    st = st_ref[...]
    vb = _by_head(i_ref[...]).astype(BF16)
    o_inter = _bqk(_by_head(qt).astype(BF16), st.astype(BF16))
    for h in range(N_HEADS):
        out_ref[:, h * HEAD_DIM:(h + 1) * HEAD_DIM] = o_inter[h]
    st_ref[...] = st * _by_head(w_end) + jnp.einsum('hnv,hnk->hvk', vb, _by_head(kdec).astype(BF16),
                                                    preferred_element_type=F32)
    return mask, q, b_end


def _gla_intra_factored(mask, q, b_end, i_ref, out_ref, k_scr, b_scr):
    mid = 0.5 * b_end
    qm = q * jnp.exp(b_scr[...] - mid)
    km = k_scr[...] * jnp.exp(mid - b_scr[...])
    att = jnp.where(mask[None], _bqk(_by_head(qm).astype(BF16), _by_head(km).astype(BF16)), 0.0)
    o_intra = _bqd(att.astype(BF16), _by_head(i_ref[...]).astype(BF16))
    for h in range(N_HEADS):
        out_ref[:, h * HEAD_DIM:(h + 1) * HEAD_DIM] += o_intra[h]


def _gla_intra_exact(q, i_ref, out_ref, k_scr, b_scr, *, rev):
    n = q.shape[0]
    t_idx = lax.broadcasted_iota(jnp.int32, (n, 1), 0)
    for h in range(N_HEADS):
        hs = slice(h * HEAD_DIM, (h + 1) * HEAD_DIM)
        q_h = q[:, hs]
        b_h = b_scr[:, hs]

        def body(grp, acc):
            base = pl.multiple_of(grp * 8, 8)
            b8, k8, v8 = b_scr[pl.ds(base, 8), hs], k_scr[pl.ds(base, 8), hs], i_ref[pl.ds(base, 8), hs]
            for r in range(8):
                s = base + r
                e = jnp.exp(jnp.minimum(b_h - b8[r:r + 1, :], 0.0))
                a = jnp.sum(q_h * k8[r:r + 1, :] * e, axis=1, keepdims=True)
                a = jnp.where((t_idx <= s) if rev else (t_idx >= s), a, 0.0)
                acc = acc + a * v8[r:r + 1, :]
            return acc

        out_ref[:, hs] += lax.fori_loop(0, n // 8, body, jnp.zeros((n, HEAD_DIM), F32))


def _gla_kernel(*refs):
    fw_in, bw_in, lb_ref = refs[0:3], refs[3:6], refs[6]
    out_f, out_b = refs[7:9]
    st_f, k_f, b_f, st_b, k_b, b_b = refs[9:]

    @pl.when(pl.program_id(0) == 0)
    def _():
        st_f[...] = jnp.zeros_like(st_f)
        st_b[...] = jnp.zeros_like(st_b)

    mask_f, q_f, end_f = _gla_inter(*fw_in, lb_ref, out_f, st_f, k_f, b_f, rev=False)
    mask_b, q_b, end_b = _gla_inter(*bw_in, lb_ref, out_b, st_b, k_b, b_b, rev=True)
    safe = jnp.minimum(jnp.min(end_f), jnp.min(end_b)) >= 2.0 * GLA_SAFE_LOG

    @pl.when(safe)
    def _():
        _gla_intra_factored(mask_f, q_f, end_f, fw_in[1], out_f, k_f, b_f)
        _gla_intra_factored(mask_b, q_b, end_b, bw_in[1], out_b, k_b, b_b)

    @pl.when(jnp.logical_not(safe))
    def _():
        _gla_intra_exact(q_f, fw_in[1], out_f, k_f, b_f, rev=False)
        _gla_intra_exact(q_b, bw_in[1], out_b, k_b, b_b, rev=True)


def _gla(proj, lb, n_lat_chunks):
    t_all = proj.shape[0]
    n = GLA_CHUNK
    n_chunks = t_all // n

    def dir_specs(rev):
        blk = _scan_block(rev, n_chunks, n_lat_chunks)
        colblk = lambda c: pl.BlockSpec((n, HALF), lambda s, c=c: (blk(s), c))
        return [colblk(0), colblk(1), colblk(3 if rev else 2)], pl.BlockSpec((n, HALF), lambda s: (blk(s), 0))

    (in_f, out_f), (in_b, out_b) = dir_specs(False), dir_specs(True)
    state = [pltpu.VMEM((N_HEADS, HEAD_DIM, HEAD_DIM), F32), pltpu.VMEM((n, HALF), F32),
             pltpu.VMEM((n, HALF), F32)]
    lb = lb.reshape(1, HALF)
    lb_rows = jnp.concatenate([1.0 - lb, jnp.log(lb), jnp.log1p(-lb)], axis=0)
    return pl.pallas_call(
        _gla_kernel,
        grid=(n_chunks,),
        in_specs=in_f + in_b + [pl.BlockSpec((3, HALF), lambda s: (0, 0))],
        out_specs=[out_f, out_b],
        out_shape=[jax.ShapeDtypeStruct((t_all, HALF), F32)] * 2,
        scratch_shapes=state + state,
        compiler_params=_cparams(1),
    )(*([proj] * 6), lb_rows)


def _s5_mats(a_re, a_im, log_dt, b_re, b_im, c_re, c_im):
    hp = lax.Precision.HIGHEST
    n = S5_CHUNK
    dt = jnp.exp(log_dt)[..., None]
    la_re, la_im = a_re * dt, a_im * dt
    mag = jnp.exp(la_re)
    ab_re, ab_im = mag * jnp.cos(la_im), mag * jnp.sin(la_im)
    nr, ni = ab_re - 1.0, ab_im
    den = jnp.square(a_re) + jnp.square(a_im)
    cr = (nr * a_re + ni * a_im) / den
    ci = (ni * a_re - nr * a_im) / den
    bb_re = cr[..., None] * b_re - ci[..., None] * b_im
    bb_im = cr[..., None] * b_im + ci[..., None] * b_re

    def apow(tau):
        tau = jnp.asarray(tau, F32)[None, None, :, None]
        m = jnp.exp(tau * la_re[:, :, None, :])
        return m * jnp.cos(tau * la_im[:, :, None, :]), m * jnp.sin(tau * la_im[:, :, None, :])

    def c_apow(tau):
        pr, pi = apow(tau)
        return (c_re[:, :, None] * pr[:, :, :, None] - c_im[:, :, None] * pi[:, :, :, None],
                c_re[:, :, None] * pi[:, :, :, None] + c_im[:, :, None] * pr[:, :, :, None])

    ca_re, ca_im = c_apow(np.arange(n))
    kern = jnp.einsum('dgtip,dgpj->dgtij', jnp.concatenate([ca_re, -ca_im], axis=-1),
                      jnp.concatenate([bb_re, bb_im], axis=2), precision=hp)
    s_idx, t_idx = np.arange(n)[:, None], np.arange(n)[None, :]
    g = a_re.shape[1]
    lp = n * S5_GROUP

    def toeplitz(kd, lag):
        onehot = jnp.asarray(lag[None] == np.arange(n)[:, None, None], F32)
        return jnp.einsum('lst,glij->gsjti', onehot, kd, precision=hp).reshape(g, lp, lp)

    kmat = toeplitz(kern[0], t_idx - s_idx) + toeplitz(kern[1], s_idx - t_idx)

    def emat(d, tau):
        pr, pi = apow(tau)
        pr, pi = pr[d][:, :, None, :], pi[d][:, :, None, :]
        br, bi = bb_re[d].transpose(0, 2, 1)[:, None], bb_im[d].transpose(0, 2, 1)[:, None]
        return jnp.concatenate([pr * br - pi * bi, pr * bi + pi * br], axis=-1).reshape(g, lp, 2 * S5_STATE)

    def fmat(d, tau):
        fr, fi = c_apow(tau)
        fr, fi = fr[d], fi[d]
        return jnp.concatenate([fr, -fi], axis=-1).transpose(0, 3, 1, 2).reshape(g, 2 * S5_STATE, lp)

    e_all = jnp.concatenate([emat(0, n - 1 - np.arange(n)), emat(1, np.arange(n))], axis=-1)
    f_all = jnp.concatenate([fmat(0, np.arange(n) + 1), fmat(1, n - np.arange(n))], axis=1)
    pr, pi = apow(np.array([n]))
    pr, pi = pr[:, :, 0], pi[:, :, 0]
    ar2 = jnp.concatenate([pr[0], pr[0], pr[1], pr[1]], axis=-1)
    ai2 = jnp.concatenate([-pi[0], pi[0], -pi[1], pi[1]], axis=-1)
    return kmat.astype(BF16), e_all.astype(BF16), f_all.astype(BF16), ar2, ai2


def _s5_state_kernel(u_ref, e_ref, s_ref):
    s_ref[...] = _nn(u_ref[...], e_ref[...])


def _s5_scan_kernel(sf_ref, sb_ref, ar_ref, ai_ref, hf_ref, hb_ref, stf_ref, stb_ref):
    nb = sf_ref.shape[0]
    w = 2 * S5_STATE
    ar, ai = ar_ref[...], ai_ref[...]

    @pl.when(pl.program_id(0) == 0)
    def _():
        stf_ref[...] = jnp.zeros_like(stf_ref)
        stb_ref[...] = jnp.zeros_like(stb_ref)

    def step(h, s, lo):
        return ar[:, lo:lo + w] * h + ai[:, lo:lo + w] * pltpu.roll(h, S5_STATE, 1) + s

    hf, hb = stf_ref[...], stb_ref[...]
    for j in range(nb):
        hf_ref[j] = hf
        hf = step(hf, sf_ref[j], 0)
        hb_ref[nb - 1 - j] = hb
        hb = step(hb, sb_ref[nb - 1 - j], w)
    stf_ref[...] = hf
    stb_ref[...] = hb


def _s5_out_kernel(u_ref, k_ref, h_ref, f_ref, y_ref):
    y_ref[...] = _nn(u_ref[...], k_ref[...]) + _nn(h_ref[...].astype(BF16), f_ref[...])


def _s5(u, mats, n_lat_chunks):
    kmat, emat, fmat, ar2, ai2 = mats
    t_all = u.shape[0]
    n = S5_CHUNK
    nc = t_all // n
    g = S5_GROUPS
    lp = n * S5_GROUP
    w4 = 4 * S5_STATE
    ug = u.astype(BF16).reshape(nc, n, g, S5_GROUP).transpose(2, 0, 1, 3).reshape(g, nc, lp)
    grp = lambda a, b: pl.BlockSpec((None, a, b), lambda i: (i, 0, 0))
    s_end = pl.pallas_call(
        _s5_state_kernel, grid=(g,),
        in_specs=[grp(nc, lp), grp(lp, w4)], out_specs=grp(nc, w4),
        out_shape=jax.ShapeDtypeStruct((g, nc, w4), F32), compiler_params=_cparams(1),
    )(ug, emat)
    nb = S5_SCAN_BLOCK
    w2 = 2 * S5_STATE
    fw_blk = _scan_block(False, nc // nb, n_lat_chunks // nb)
    bw_blk = _scan_block(True, nc // nb, n_lat_chunks // nb)
    s_t = s_end.transpose(1, 0, 2)
    coef = pl.BlockSpec((g, w4), lambda i: (0, 0))
    h_fw, h_bw = pl.pallas_call(
        _s5_scan_kernel, grid=(nc // nb,),
        in_specs=[pl.BlockSpec((nb, g, w2), lambda i: (fw_blk(i), 0, 0)),
                  pl.BlockSpec((nb, g, w2), lambda i: (bw_blk(i), 0, 1)), coef, coef],
        out_specs=[pl.BlockSpec((nb, g, w2), lambda i: (fw_blk(i), 0, 0)),
                   pl.BlockSpec((nb, g, w2), lambda i: (bw_blk(i), 0, 0))],
        out_shape=[jax.ShapeDtypeStruct((nc, g, w2), F32)] * 2,
        scratch_shapes=[pltpu.VMEM((g, w2), F32)] * 2,
        compiler_params=_cparams(1),
    )(s_t, s_t, ar2, ai2)
    h_in = jnp.concatenate([h_fw, h_bw], axis=-1).transpose(1, 0, 2)
    y = pl.pallas_call(
        _s5_out_kernel, grid=(g,),
        in_specs=[grp(nc, lp), grp(lp, lp), grp(nc, w4), grp(w4, lp)], out_specs=grp(nc, lp),
        out_shape=jax.ShapeDtypeStruct((g, nc, lp), F32), compiler_params=_cparams(1),
    )(ug, kmat, h_in, fmat)
    return y.reshape(g, nc, n, S5_GROUP).transpose(1, 2, 0, 3).reshape(t_all, HALF)


def _top_rows_sorted(x, k):
    n = x.shape[0] // 8
    cols = [x[8 * i:8 * (i + 1), :] for i in range(n)]
    size = 2
    while size <= n:
        stride = size // 2
        while stride >= 1:
            for i in range(n):
                j = i ^ stride
                if j > i:
                    hi, lo = jnp.maximum(cols[i], cols[j]), jnp.minimum(cols[i], cols[j])
                    cols[i], cols[j] = (hi, lo) if (i & size) == 0 else (lo, hi)
            stride //= 2
        size *= 2
    rows = []
    for r in range(k):
        mx = jnp.max(cols[0], axis=0, keepdims=True)
        rows.append(mx)
        popped = cols[0] == mx
        live = min(n, k - r)
        for i in range(live):
            nxt = cols[i + 1] if i + 1 < n else NEG
            cols[i] = jnp.where(popped, nxt, cols[i])
    return rows


def _peer_route_kernel(ht_ref, wq_ref, keys_ref, e1_ref, e0_ref, th_ref, top_scr):
    kk = PEER_TOPK
    qt = _nn(wq_ref[...], ht_ref[...])
    for hp in range(2 * PEER_HEADS):
        sc = _nn(keys_ref[hp], qt[hp * HEAD_DIM:(hp + 1) * HEAD_DIM, :].astype(BF16))
        h, p = divmod(hp, 2)
        if p == 0:
            e0_ref[h] = sc
        else:
            e1_ref[h] = sc
        for r, mx in enumerate(_top_rows_sorted(sc, kk + 1)):
            top_scr[hp, r:r + 1, :] = mx
        top_scr[hp, kk + 1:, :] = jnp.full((PEER_TOP_ROWS - kk - 1, sc.shape[1]), NEG, F32)

    row = lax.broadcasted_iota(jnp.int32, (PEER_TOP_ROWS, 1), 0)
    for h in range(PEER_HEADS):
        a = top_scr[2 * h]
        b = top_scr[2 * h + 1]
        parts = [a[0:1] + b, a[1:2] + b[0:8], a[2:3] + b[0:8], a[3:4] + b[0:8]]
        parts += [jnp.where(row >= 4, a + b[j:j + 1], NEG) for j in range(3)]
        cand = jnp.maximum(jnp.concatenate(parts, axis=0), NEG)
        pad = jnp.full((PEER_N_KEYS - cand.shape[0], cand.shape[1]), NEG, F32)
        top = _top_rows_sorted(jnp.concatenate([cand, pad], axis=0), kk + 1)
        thr = 0.5 * (top[kk - 1] + top[kk])
        z = jnp.sum(jnp.where(cand >= thr, jnp.exp(cand - top[0]), 0.0), axis=0, keepdims=True)
        e0_ref[h] = jnp.exp(e0_ref[h] - a[0:1]) / z
        e1_ref[h] = jnp.exp(e1_ref[h] - b[0:1])
        th_ref[h:h + 1, :] = jnp.exp(thr - top[0]) / z


def _peer_expert_kernel(ht_ref, e1_ref, e0_ref, th_ref, *rest, n_tiles):
    u_refs = rest[:PEER_U_SPLIT]
    vt_ref, o_ref, at0_scr, at1_scr, g0_scr, g1_scr = rest[PEER_U_SPLIT:]
    s = pl.program_id(1)
    assert n_tiles % 2 == 0 and n_tiles >= 4
    ni = u_refs[0].shape[0] // PEER_N_KEYS
    kc = u_refs[0].shape[1]
    at_scr, g_scr = (at0_scr, at1_scr), (g0_scr, g1_scr)

    def stages(par, dot1, gate, dot2, first=False):
        if dot1:
            at_scr[par][...] = _nn(jnp.concatenate([u_k[...] for u_k in u_refs], axis=1), ht_ref[...])
        if gate:
            at_old, g_new = at_scr[1 - par], g_scr[1 - par]
            nj, nc = PEER_GATE_ROWS, 128
            for ii in range(ni):
                i = (s - 1) * ni + ii
                e0_rows = [e0_ref[h, pl.ds(i, 1), :] for h in range(PEER_HEADS)]
                for c in range(at_old.shape[1] // nc):
                    cs = slice(c * nc, (c + 1) * nc)
                    for jb in range(PEER_N_KEYS // nj):
                        js = slice(jb * nj, (jb + 1) * nj)
                        w = None
                        for h in range(PEER_HEADS):
                            p = e1_ref[h, js, cs] * e0_rows[h][:, cs]
                            t = jnp.where(p >= th_ref[h:h + 1, cs], p, 0.0)
                            w = t if w is None else w + t
                        rs = slice(ii * PEER_N_KEYS + jb * nj, ii * PEER_N_KEYS + (jb + 1) * nj)
                        g_new[rs, cs] = (_gelu_tanh(at_old[rs, cs]) * w).astype(BF16)
        if dot2:
            prod = _nn(vt_ref[...], g_scr[par][...])
            if first:
                o_ref[...] = prod
            else:
                o_ref[...] += prod

    mid = jnp.logical_and(s > 2, s < n_tiles)
    pl.when(s == 0)(lambda: stages(0, True, False, False))
    pl.when(s == 1)(lambda: stages(1, True, True, False))
    pl.when(s == 2)(lambda: stages(0, True, True, True, first=True))
    pl.when(jnp.logical_and(mid, s % 2 == 0))(lambda: stages(0, True, True, True))
    pl.when(jnp.logical_and(mid, s % 2 == 1))(lambda: stages(1, True, True, True))
    pl.when(s == n_tiles)(lambda: stages(0, False, True, True))
    pl.when(s == n_tiles + 1)(lambda: stages(1, False, False, True))


def _tok3(t):
    return pl.BlockSpec((PEER_HEADS, PEER_N_KEYS, t), lambda i, *_: (0, 0, i))


def _peer_route(h2t, wq_t_bf, keys_bf):
    d, t_all = h2t.shape
    nh, nk = PEER_HEADS, PEER_N_KEYS
    t1 = PEER_P1_TILE
    return pl.pallas_call(
        _peer_route_kernel,
        grid=(t_all // t1,),
        in_specs=[pl.BlockSpec((d, t1), lambda i: (0, i)),
                  pl.BlockSpec(wq_t_bf.shape, lambda i: (0, 0)),
                  pl.BlockSpec(keys_bf.shape, lambda i: (0, 0, 0))],
        out_specs=[_tok3(t1), _tok3(t1), pl.BlockSpec((nh, t1), lambda i: (0, i))],
        out_shape=[jax.ShapeDtypeStruct((nh, nk, t_all), F32)] * 2 + [jax.ShapeDtypeStruct((nh, t_all), F32)],
        scratch_shapes=[pltpu.VMEM((2 * nh, PEER_TOP_ROWS, t1), F32)],
        compiler_params=_cparams(1),
    )(h2t, wq_t_bf, keys_bf)


def _padded_tokens(t_all):
    return t_all + (-t_all % PEER_TOK_TILE)


def _peer(h2t, wq_t_bf, keys_bf, u_r, vt_r, layer=0):
    return _peer_experts(h2t, *_peer_route(h2t, wq_t_bf, keys_bf), u_r, vt_r, layer)


def _peer_tables(u_all, v_all):
    return u_all.astype(BF16), v_all.transpose(0, 2, 1).astype(BF16)


def _peer_experts(h2t, e1, e0, th, u_r, vt_r, layer):
    d, t_all = h2t.shape
    tt = PEER_TOK_TILE
    et = PEER_EXP_TILE
    n_tiles = u_r.shape[1] // et
    kc = d // PEER_U_SPLIT
    tok = pl.BlockSpec((d, tt), lambda i, s: (0, i))
    return pl.pallas_call(
        functools.partial(_peer_expert_kernel, n_tiles=n_tiles),
        grid=(t_all // tt, n_tiles + 2),
        in_specs=[tok, _tok3(tt), _tok3(tt), pl.BlockSpec((PEER_HEADS, tt), lambda i, s: (0, i))]
        + [pl.BlockSpec((None, et, kc), lambda i, s, k=k: (layer, jnp.minimum(s, n_tiles - 1), k))
           for k in range(PEER_U_SPLIT)]
        + [pl.BlockSpec((None, d, et), lambda i, s: (layer, 0, jnp.clip(s - 2, 0, n_tiles - 1)))],
        out_specs=tok,
        out_shape=jax.ShapeDtypeStruct((d, t_all), F32),
        scratch_shapes=[pltpu.VMEM((et, tt), F32)] * 2 + [pltpu.VMEM((et, tt), BF16)] * 2,
        compiler_params=_cparams(2),
    )(h2t, e1, e0, th, *([u_r] * PEER_U_SPLIT), vt_r)


def _rope_tables(t_lat, t_ctx):
    pos = np.arange(t_lat)
    nfreq = HEAD_DIM // 4
    freqs = ROPE_BASE ** (-jnp.arange(nfreq, dtype=F32) / nfreq)
    ang_r = jnp.asarray(pos // GRID_W, F32)[:, None] * freqs
    ang_c = jnp.asarray(pos % GRID_W, F32)[:, None] * freqs
    cos = jnp.concatenate([jnp.cos(ang_r), jnp.cos(ang_r), jnp.cos(ang_c), jnp.cos(ang_c)], axis=1)
    sin = jnp.concatenate([-jnp.sin(ang_r), jnp.sin(ang_r), -jnp.sin(ang_c), jnp.sin(ang_c)], axis=1)
    cos = jnp.concatenate([cos, jnp.ones((t_ctx, HEAD_DIM), F32)], axis=0)
    sin = jnp.concatenate([sin, jnp.zeros((t_ctx, HEAD_DIM), F32)], axis=0)
    return cos, sin


def _peer_and_norm(x1, h2t, mods, ln_g, ln_b, n_lat_tiles, w_q, sub_keys, tables, layer, mods_next=None):
    keys = sub_keys.reshape(2 * PEER_HEADS, PEER_N_KEYS, -1).astype(BF16)
    pt = _peer(h2t, w_q.T.astype(BF16), keys, *tables, layer)
    return _ln2(x1, pt, mods, ln_g, ln_b, n_lat_tiles, mods_next)


def _layer_ab(x_all, mods, t_lat, t_ctx, w_in, w_out, igate_b, fgate_b, rpb, ln_g, ln_b, xm=None):
    n_lat_tiles = t_lat // ROW_TILE
    if xm is None:
        xm = _modulate(x_all, mods, n_lat_tiles)
    ng = 4 * N_HEADS
    w_main = jnp.concatenate([w_in[:, :4 * HALF], w_in[:, 4 * HALF + ng:]], axis=1).astype(BF16)
    w_gate = jnp.pad(w_in[:, 4 * HALF:4 * HALF + ng], ((0, 0), (0, 128 - ng))).astype(BF16)
    proj = _inproj(xm, w_main, BF16)
    gates = _inproj(xm, w_gate)
    cos_t, sin_t = _rope_tables(t_lat, t_ctx)
    h_fw, h_bw = _mlstm(proj, gates, gates.T, cos_t, sin_t, igate_b, fgate_b, t_lat // SCAN_CHUNK)
    y_na = jnp.concatenate([_na(proj, rpb, t_lat, t_ctx), _ctx_attn(proj, t_lat, t_ctx)], axis=0)
    return _outproj([(h_fw, 0), (h_bw, 0), (proj, 3)], [(y_na, 0)], w_out.astype(BF16), x_all, mods,
                    ln_g, ln_b, n_lat_tiles)


def _layer_cd(x_all, mods, t_lat, w_in, w_out, lb, s5p, s5_d, glu_w, glu_b, ln_g, ln_b, xm=None):
    n_lat_tiles = t_lat // ROW_TILE
    if xm is None:
        xm = _modulate(x_all, mods, n_lat_tiles)
    proj = _inproj(xm, w_in.astype(BF16))
    o_fw, o_bw = _gla(proj, lb, t_lat // GLA_CHUNK)
    y5 = _s5(proj[:, 5 * HALF:], _s5_mats(*[p.astype(F32) for p in s5p]), t_lat // S5_CHUNK)
    return _outproj([(o_fw, 0), (o_bw, 0), (proj, 4)], [(y5, 0), (proj, 5)], w_out.astype(BF16), x_all, mods,
                    ln_g, ln_b, n_lat_tiles,
                    glu_params=(s5_d.astype(F32), glu_w.astype(BF16), glu_b.astype(F32)))


def kernel(x, c, ctx, c_ctx, ada_w, ada_b, ln_g, ln_b, ab_w_in, ab_w_out, mlstm_igate_b, mlstm_fgate_b, na_rpb, cd_w_in, cd_w_out, hgrn_lb_logits, s5_a_re, s5_a_im, s5_log_dt, s5_b_re, s5_b_im, s5_c_re, s5_c_im, s5_d, s5_glu_w, s5_glu_b, peer_w_q, peer_sub_keys, peer_u, peer_v):
    t_lat, t_ctx = x.shape[1], ctx.shape[1]
    assert x.shape[0] == 1 and t_ctx == ROW_TILE and t_lat % (NA_QROWS * GRID_W) == 0
    depth = ada_w.shape[0]
    lb_soft = jax.nn.softmax(hgrn_lb_logits.astype(F32), axis=0)
    lower_bounds = jnp.cumsum(lb_soft, axis=0) - lb_soft[0]
    x_all = jnp.concatenate([x[0], ctx[0]], axis=0).astype(F32)
    n_lat_tiles = t_lat // ROW_TILE
    mods_all = [_adaln(c, c_ctx, ada_w, ada_b, l) for l in range(depth)]
    xm = _modulate(x_all, mods_all[0], n_lat_tiles)
    tables = _peer_tables(peer_u, peer_v)
    for l in range(depth):
        j = l // 2
        mods = mods_all[l]
        if l % 2 == 0:
            x1, h2t = _layer_ab(x_all, mods, t_lat, t_ctx, ab_w_in[j], ab_w_out[j], mlstm_igate_b[j],
                                mlstm_fgate_b[j], na_rpb[j], ln_g[l, 0], ln_b[l, 0], xm=xm)
        else:
            s5p = (s5_a_re[j], s5_a_im[j], s5_log_dt[j], s5_b_re[j], s5_b_im[j], s5_c_re[j], s5_c_im[j])
            x1, h2t = _layer_cd(x_all, mods, t_lat, cd_w_in[j], cd_w_out[j], lower_bounds[l], s5p, s5_d[j],
                                s5_glu_w[j], s5_glu_b[j], ln_g[l, 0], ln_b[l, 0], xm=xm)
        peer_args = (x1, h2t, mods, ln_g[l, 1], ln_b[l, 1], n_lat_tiles, peer_w_q[l], peer_sub_keys[l],
                     tables, l)
        if l + 1 < depth:
            x_all, xm = _peer_and_norm(*peer_args, mods_next=mods_all[l + 1])
        else:
            x_all = _peer_and_norm(*peer_args)
    return x_all[None].astype(x.dtype)
```

```python
import functools
import math

import numpy as np
import jax
import jax.numpy as jnp
from jax import lax
from jax.experimental import pallas as pl
from jax.experimental.pallas import tpu as pltpu

F32 = jnp.float32
BF16 = jnp.bfloat16

HEAD_DIM = 128
GRID_W = 64
N_HEADS = 8
HALF = N_HEADS * HEAD_DIM
NA_WIN_ROWS = 8
NA_WIN_COLS = 16
ROPE_BASE = 10000.0
S5_GROUP = 16
S5_STATE = 64
S5_GROUPS = HALF // S5_GROUP
PEER_HEADS = 8
PEER_N_KEYS = 128
PEER_TOPK = 16
PEER_TOP_ROWS = 24
LN_EPS = 1e-5
DEPTH = 2
ALPHA = (2.0 * DEPTH) ** 0.25
QK_SCALE = HEAD_DIM ** -0.5

ROW_TILE = 256
MM_ROW_TILE = 1280
MM_COL_TILE = 1024
SCAN_CHUNK = 128
GLA_CHUNK = 64
S5_CHUNK = 32
S5_SCAN_BLOCK = 8
NA_HEADS = 4
NA_QROWS = 8
PEER_P1_TILE = 256
PEER_TOK_TILE = 512
PEER_EXP_TILE = 512
PEER_GATE_ROWS = 32
PEER_U_SPLIT = 4
NEG = -1e30
GLA_SAFE_LOG = -80.0
VMEM_LIMIT = 56 * 1024 * 1024


def _cparams(n_axes):
    return pltpu.CompilerParams(dimension_semantics=("arbitrary",) * n_axes,
                                vmem_limit_bytes=VMEM_LIMIT)


def _nt(a, b):
    return lax.dot_general(a, b, (((1,), (1,)), ((), ())), preferred_element_type=F32)


def _nn(a, b):
    return jnp.dot(a, b, preferred_element_type=F32)


def _split3(x):
    hi = x.astype(BF16)
    r = x - hi.astype(F32)
    mid = r.astype(BF16)
    lo = (r - mid.astype(F32)).astype(BF16)
    return hi, mid, lo


def _log_sigmoid(x):
    return jnp.minimum(x, 0.0) - jnp.log1p(jnp.exp(-jnp.abs(x)))


def _sigmoid(x):
    return 1.0 / (1.0 + jnp.exp(-x))


def _gelu_tanh(x):
    return 0.5 * x * (1.0 + jnp.tanh(math.sqrt(2.0 / math.pi) * (x + 0.044715 * (x * x * x))))


def _head_norm(x):
    mu = jnp.mean(x, axis=-1, keepdims=True)
    xc = x - mu
    return xc * lax.rsqrt(jnp.mean(xc * xc, axis=-1, keepdims=True) + LN_EPS)


def _adaln_kernel(ct_ref, w_ref, b_ref, o_ref):
    ct = ct_ref[...]
    s = ct * _sigmoid(ct)
    w = w_ref[...]
    o_ref[0:1, :] = jnp.sum(s[:, 0:1] * w, axis=0, keepdims=True) + b_ref[...]
    o_ref[1:2, :] = jnp.sum(s[:, 1:2] * w, axis=0, keepdims=True) + b_ref[...]


def _adaln(c_lat, c_ctx, w_all, b_all, layer):
    n_layers, d, n = w_all.shape
    tn = n // 8
    ct = jnp.stack([c_lat.reshape(d), c_ctx.reshape(d)], axis=1)
    out = pl.pallas_call(
        _adaln_kernel,
        grid=(n // tn,),
        in_specs=[pl.BlockSpec((d, 2), lambda j: (0, 0)),
                  pl.BlockSpec((None, d, tn), lambda j: (layer, 0, j)),
                  pl.BlockSpec((None, 1, tn), lambda j: (layer, 0, j))],
        out_specs=pl.BlockSpec((2, tn), lambda j: (0, j)),
        out_shape=jax.ShapeDtypeStruct((2, n), F32),
        compiler_params=_cparams(1),
    )(ct, w_all, b_all.reshape(n_layers, 1, n))
    return jnp.pad(out.reshape(2, 6, d), ((0, 0), (0, 2), (0, 0)))


def _modulated(x, m_ref):
    return (x * (1.0 + m_ref[1:2, :]) + m_ref[0:1, :]).astype(BF16)


def _modulate_kernel(x_ref, m_ref, o_ref):
    o_ref[...] = _modulated(x_ref[...], m_ref)


def _mods_spec(d, n_lat_tiles):
    return pl.BlockSpec((None, 8, d), lambda i: (jnp.where(i < n_lat_tiles, 0, 1), 0, 0))


def _modulate(x_all, mods, n_lat_tiles):
    t_all, d = x_all.shape
    row = pl.BlockSpec((ROW_TILE, d), lambda i: (i, 0))
    return pl.pallas_call(
        _modulate_kernel,
        grid=(t_all // ROW_TILE,),
        in_specs=[row, _mods_spec(d, n_lat_tiles)],
        out_specs=row,
        out_shape=jax.ShapeDtypeStruct((t_all, d), BF16),
        compiler_params=_cparams(1),
    )(x_all, mods)


def _matmul_kernel(x_ref, w_ref, o_ref):
    o_ref[...] = _nn(x_ref[...], w_ref[...]).astype(o_ref.dtype)


def _inproj(xm, w_bf, out_dtype=F32):
    t_all, d = xm.shape
    n = w_bf.shape[1]
    tm = MM_ROW_TILE if t_all % MM_ROW_TILE == 0 else ROW_TILE
    tn = min(n, MM_COL_TILE)
    return pl.pallas_call(
        _matmul_kernel,
        grid=(t_all // tm, n // tn),
        in_specs=[pl.BlockSpec((tm, d), lambda i, j: (i, 0)),
                  pl.BlockSpec((d, tn), lambda i, j: (0, j))],
        out_specs=pl.BlockSpec((tm, tn), lambda i, j: (i, j)),
        out_shape=jax.ShapeDtypeStruct((t_all, n), out_dtype),
        compiler_params=_cparams(2),
    )(xm, w_bf)


def _ln_rows(z, g, b):
    mu = jnp.mean(z, axis=-1, keepdims=True)
    zc = z - mu
    return zc * lax.rsqrt(jnp.mean(zc * zc, axis=-1, keepdims=True) + LN_EPS) * g + b


def _outproj_kernel(*refs, glu, n_row_tiles):
    hf_ref, hb_ref, gate_ref = refs[:3]
    refs = refs[3:]
    if glu:
        (y5_ref, u5_ref, ds_ref, gw_ref, gb_ref, w1_ref, w2_ref, x_ref, m_ref, g_ref, b_ref,
         x1_ref, h2_ref) = refs
        y = _gelu_tanh(y5_ref[...] + ds_ref[...] * u5_ref[...])
        a2 = y * _sigmoid(_nn(y.astype(BF16), gw_ref[...]) + gb_ref[...])
    else:
        a2_ref, w1_ref, w2_ref, x_ref, m_ref, g_ref, b_ref, x1_ref, h2_ref = refs
        a2 = a2_ref[...]
    heads = []
    for h in range(N_HEADS):
        hs = slice(h * HEAD_DIM, (h + 1) * HEAD_DIM)
        gx = gate_ref[:, hs].astype(F32)
        gate = gx * _sigmoid(gx) if glu else _sigmoid(gx)
        heads.append((_head_norm(hf_ref[:, hs] + hb_ref[:, hs]) * gate).astype(BF16))
    a1 = jnp.concatenate(heads, axis=1)
    y = _nn(a1, w1_ref[...]) + _nn(a2.astype(BF16), w2_ref[...])
    x1 = _ln_rows(ALPHA * x_ref[...] + m_ref[2:3, :] * y, g_ref[...], b_ref[...])
    x1_ref[...] = x1
    h2t = (x1 * (1.0 + m_ref[4:5, :]) + m_ref[3:4, :]).T
    h2_ref[...] = jnp.where(pl.program_id(0) < n_row_tiles, h2t, 0.0).astype(BF16)


def _outproj(scan_parts, a2_parts, w_out_bf, x_all, mods, ln_g, ln_b, n_lat_tiles, glu_params=None):
    t_all, d = x_all.shape
    half = HALF
    n_row_tiles = t_all // ROW_TILE
    t_pad = _padded_tokens(t_all)
    row = lambda i: (jnp.minimum(i, n_row_tiles - 1), 0)
    const = lambda i: (0, 0)
    a2_parts = list(scan_parts) + list(a2_parts)
    a2_specs = [pl.BlockSpec((ROW_TILE, half), lambda i, c=c: (jnp.minimum(i, n_row_tiles - 1), c))
                for _, c in a2_parts]
    a2_args = [a for a, _ in a2_parts]
    if glu_params is not None:
        d_skip, glu_w_bf, glu_b = glu_params
        a2_specs += [pl.BlockSpec((1, half), const), pl.BlockSpec((half, half), const),
                     pl.BlockSpec((1, half), const)]
        a2_args += [d_skip.reshape(1, half), glu_w_bf, glu_b.reshape(1, half)]
    return pl.pallas_call(
        functools.partial(_outproj_kernel, glu=glu_params is not None, n_row_tiles=n_row_tiles),
        grid=(t_pad // ROW_TILE,),
        in_specs=a2_specs + [
            pl.BlockSpec((half, d), const), pl.BlockSpec((half, d), const),
            pl.BlockSpec((ROW_TILE, d), row),
            pl.BlockSpec((None, 8, d), lambda i: (jnp.where(i < n_lat_tiles, 0, 1), 0, 0)),
            pl.BlockSpec((1, d), const), pl.BlockSpec((1, d), const)],
        out_specs=[pl.BlockSpec((ROW_TILE, d), row), pl.BlockSpec((d, ROW_TILE), lambda i: (0, i))],
        out_shape=[jax.ShapeDtypeStruct((t_all, d), F32), jax.ShapeDtypeStruct((d, t_pad), BF16)],
        compiler_params=_cparams(1),
    )(*a2_args, w_out_bf[:half], w_out_bf[half:], x_all, mods, ln_g.reshape(1, d), ln_b.reshape(1, d))


def _ln2_kernel(x_ref, p_ref, m_ref, g_ref, b_ref, *rest):
    x2 = _ln_rows(ALPHA * x_ref[...] + m_ref[5:6, :] * p_ref[...].T, g_ref[...], b_ref[...])
    if len(rest) == 1:
        rest[0][...] = x2
    else:
        mn_ref, o_ref, xm_ref = rest
        o_ref[...] = x2
        xm_ref[...] = _modulated(x2, mn_ref)


def _ln2(x1, peer_out_t, mods, ln_g, ln_b, n_lat_tiles, mods_next=None):
    t_all, d = x1.shape
    row = pl.BlockSpec((ROW_TILE, d), lambda i: (i, 0))
    col = pl.BlockSpec((d, ROW_TILE), lambda i: (0, i))
    vec = pl.BlockSpec((1, d), lambda i: (0, 0))
    nxt = mods_next is not None
    n_rows = t_all if nxt else n_lat_tiles * ROW_TILE
    return pl.pallas_call(
        _ln2_kernel,
        grid=(n_rows // ROW_TILE,),
        in_specs=[row, col, _mods_spec(d, n_lat_tiles), vec, vec] + ([_mods_spec(d, n_lat_tiles)] if nxt else []),
        out_specs=[row, row] if nxt else row,
        out_shape=([jax.ShapeDtypeStruct((t_all, d), F32), jax.ShapeDtypeStruct((t_all, d), BF16)] if nxt
                   else jax.ShapeDtypeStruct((n_rows, d), F32)),
        compiler_params=_cparams(1),
    )(x1, peer_out_t, mods, ln_g.reshape(1, d), ln_b.reshape(1, d), *([mods_next] if nxt else []))


def _scan_block(rev, n_chunks, n_lat_chunks):
    if rev:
        return lambda s: n_chunks - 1 - s
    n_ctx = n_chunks - n_lat_chunks
    return lambda s: jnp.where(s < n_ctx, s + n_lat_chunks, s - n_ctx)


def _scan_mask(rev, n):
    row = lax.broadcasted_iota(jnp.int32, (n, n), 0)
    col = lax.broadcasted_iota(jnp.int32, (n, n), 1)
    return (col >= row) if rev else (col <= row)


def _rope(x, cos, sin):
    lane = lax.broadcasted_iota(jnp.int32, x.shape, 1)
    partner = jnp.where((lane % 64) < 32, pltpu.roll(x, 96, 1), pltpu.roll(x, 32, 1))
    return x * cos + partner * sin


def _mlstm_kernel(*refs):
    n_in = 5
    fw_in, bw_in = refs[:n_in], refs[n_in:2 * n_in]
    bias_ref, biast_ref, out_f, out_b = refs[2 * n_in:2 * n_in + 4]
    state = refs[2 * n_in + 4:]

    @pl.when(pl.program_id(0) == 0)
    def _():
        for r in state:
            r[...] = jnp.zeros_like(r)

    _mlstm_chunk(*fw_in, bias_ref, biast_ref, out_f, *state[:2], rev=False)
    _mlstm_chunk(*bw_in, bias_ref, biast_ref, out_b, *state[2:], rev=True)


def _mlstm_chunk(q_ref, k_ref, v_ref, g_ref, gt_ref, bias_ref, biast_ref, out_ref, cn_ref, m_ref, *, rev):
    n = q_ref.shape[0]
    mask = _scan_mask(rev, n)
    tri = mask.astype(BF16)
    pre_c = g_ref[...] + bias_ref[...]
    pre_r = gt_ref[...] + biast_ref[...]
    b_c = sum(_nn(tri, p) for p in _split3(_log_sigmoid(pre_c)))
    b_r = sum(_nt(p, tri) for p in _split3(_log_sigmoid(pre_r)))
    last = 0 if rev else n - 1
    off = N_HEADS if rev else 0
    heads = range(N_HEADS)
    bc = jnp.stack([b_c[:, 2 * N_HEADS + off + h:2 * N_HEADS + off + h + 1] for h in heads])
    br = jnp.stack([b_r[2 * N_HEADS + off + h:2 * N_HEADS + off + h + 1, :] for h in heads])
    ic = jnp.stack([pre_c[:, off + h:off + h + 1] for h in heads])
    ir = jnp.stack([pre_r[off + h:off + h + 1, :] for h in heads])
    m_prev = m_ref[:, :, 0:1]
    logw = jnp.where(mask[None], bc - br + ir, NEG)
    inter = bc + m_prev
    m_t = jnp.maximum(inter, jnp.max(logw, axis=2, keepdims=True))
    w_inter = jnp.exp(inter - m_t)
    qb, kb = _by_head(q_ref[...]), _by_head(k_ref[...])
    vb = _by_head(v_ref[...]).astype(BF16)
    v_ext = jnp.concatenate([vb, jnp.ones_like(vb)], axis=2)
    s = _bqk(qb, kb) * jnp.exp(logw - m_t)
    cn_old = cn_ref[...]
    both = _bqd(s.astype(BF16), v_ext) + _bqd(qb, cn_old.astype(BF16)) * w_inter
    den = jnp.maximum(jnp.abs(both[:, :, HEAD_DIM:]), jnp.exp(-m_t))
    out = both[:, :, :HEAD_DIM] / den
    for h in heads:
        out_ref[:, h * HEAD_DIM:(h + 1) * HEAD_DIM] = out[h]
    m_new = m_t[:, last:last + 1, :]
    b_last = bc[:, last:last + 1, :]
    w_old = jnp.exp(b_last + m_prev - m_new)
    kw = kb.astype(F32) * jnp.exp(b_last - bc + ic - m_new)
    cn_ref[...] = w_old * cn_old + jnp.einsum('hkm,hkd->hmd', kw.astype(BF16), v_ext, preferred_element_type=F32)
    m_ref[...] = jnp.broadcast_to(m_new, m_ref.shape)


def _rope_kernel(q_ref, k_ref, cos_ref, sin_ref, qo_ref, ko_ref):
    cos, sin = cos_ref[...], sin_ref[...]
    for h in range(N_HEADS):
        hs = slice(h * HEAD_DIM, (h + 1) * HEAD_DIM)
        qo_ref[:, hs] = (_rope(q_ref[:, hs].astype(F32), cos, sin) * QK_SCALE).astype(BF16)
        ko_ref[:, hs] = _rope(k_ref[:, hs].astype(F32), cos, sin).astype(BF16)


def _rope_qk(proj, cos_t, sin_t):
    t_all = proj.shape[0]
    tok = pl.BlockSpec((ROW_TILE, 128), lambda i: (i, 0))
    return pl.pallas_call(
        _rope_kernel,
        grid=(t_all // ROW_TILE,),
        in_specs=[pl.BlockSpec((ROW_TILE, HALF), lambda i: (i, 0)),
                  pl.BlockSpec((ROW_TILE, HALF), lambda i: (i, 1)), tok, tok],
        out_specs=[pl.BlockSpec((ROW_TILE, HALF), lambda i: (i, 0))] * 2,
        out_shape=[jax.ShapeDtypeStruct((t_all, HALF), BF16)] * 2,
        compiler_params=_cparams(1),
    )(proj, proj, cos_t, sin_t)


def _mlstm(proj, gates, gates_t, cos_t, sin_t, igate_b, fgate_b, n_lat_chunks):
    t_all = proj.shape[0]
    n = SCAN_CHUNK
    n_chunks = t_all // n
    bias = jnp.zeros((1, 128), F32).at[0, :4 * N_HEADS].set(
        jnp.concatenate([igate_b.reshape(-1), fgate_b.reshape(-1)]))
    q_rot, k_rot = _rope_qk(proj, cos_t, sin_t)

    def dir_specs(rev):
        blk = _scan_block(rev, n_chunks, n_lat_chunks)
        colblk = lambda c: pl.BlockSpec((n, HALF), lambda s, c=c: (blk(s), c))
        ins = [colblk(0), colblk(0), colblk(2), pl.BlockSpec((n, 128), lambda s: (blk(s), 0)),
               pl.BlockSpec((128, n), lambda s: (0, blk(s)))]
        return ins, pl.BlockSpec((n, HALF), lambda s: (blk(s), 0))

    (in_f, out_f), (in_b, out_b) = dir_specs(False), dir_specs(True)
    dir_args = [q_rot, k_rot, proj, gates, gates_t]
    state = [pltpu.VMEM((N_HEADS, HEAD_DIM, 2 * HEAD_DIM), F32), pltpu.VMEM((N_HEADS, 1, HEAD_DIM), F32)]
    return pl.pallas_call(
        _mlstm_kernel,
        grid=(n_chunks,),
        in_specs=in_f + in_b + [pl.BlockSpec((1, 128), lambda s: (0, 0)), pl.BlockSpec((128, 1), lambda s: (0, 0))],
        out_specs=[out_f, out_b],
        out_shape=[jax.ShapeDtypeStruct((t_all, HALF), F32)] * 2,
        scratch_shapes=state + state,
        compiler_params=_cparams(1),
    )(*dir_args, *dir_args, bias, bias.reshape(128, 1))


def _by_head(x):
    return jnp.stack([x[:, h * HEAD_DIM:(h + 1) * HEAD_DIM] for h in range(x.shape[1] // HEAD_DIM)])


def _bqk(a, b):
    return jnp.einsum('hqd,hkd->hqk', a, b, preferred_element_type=F32)


def _bqd(p, v):
    return jnp.einsum('hqk,hkd->hqd', p, v, preferred_element_type=F32)


def _na_kernel(q_ref, kp_ref, kc_ref, kn_ref, vp_ref, vc_ref, vn_ref, kx_ref, vx_ref, bias_ref, o_ref):
    qb = _by_head(q_ref[...].astype(F32) * QK_SCALE).astype(BF16)
    kk = _by_head(jnp.concatenate([kp_ref[...], kc_ref[...], kn_ref[...]], axis=0)).astype(BF16)
    vv = _by_head(jnp.concatenate([vp_ref[...], vc_ref[...], vn_ref[...]], axis=0)).astype(BF16)
    s_loc = _bqk(qb, kk) + bias_ref[...]
    s_ctx = _bqk(qb, _by_head(kx_ref[...]).astype(BF16))
    m = jnp.maximum(jnp.max(s_loc, axis=2, keepdims=True), jnp.max(s_ctx, axis=2, keepdims=True))
    p_loc = jnp.exp(s_loc - m)
    p_ctx = jnp.exp(s_ctx - m)
    l = jnp.sum(p_loc, axis=2, keepdims=True) + jnp.sum(p_ctx, axis=2, keepdims=True)
    out = (_bqd(p_loc.astype(BF16), vv) + _bqd(p_ctx.astype(BF16), _by_head(vx_ref[...]).astype(BF16))) / l
    for h in range(out.shape[0]):
        o_ref[:, h * HEAD_DIM:(h + 1) * HEAD_DIM] = out[h]


def _na_bias(rpb, rows):
    qn = NA_QROWS * GRID_W
    kn = 2 * qn
    ql = np.arange(qn)
    kl = np.arange(kn)
    out = []
    nqb = rows // NA_QROWS
    lr, krl = np.arange(NA_QROWS)[:, None], np.arange(2 * NA_QROWS)[None, :]
    dr = np.clip(krl - NA_QROWS // 2 - lr + NA_WIN_ROWS - 1, 0, 2 * NA_WIN_ROWS - 2)
    col = np.arange(GRID_W)
    dc = np.clip(col[None, :] - col[:, None] + NA_WIN_COLS - 1, 0, 2 * NA_WIN_COLS - 2)
    oh_r = jnp.asarray(dr[..., None] == np.arange(2 * NA_WIN_ROWS - 1), F32)
    oh_c = jnp.asarray(dc[..., None] == np.arange(2 * NA_WIN_COLS - 1), F32)
    table = jnp.einsum('hab,lka,qcb->hlqkc', rpb, oh_r, oh_c,
                       precision=lax.Precision.HIGHEST).reshape(rpb.shape[0], qn, kn)
    for qb in (0, 1, nqb - 1):
        qr = NA_QROWS * qb + ql // GRID_W
        qc = ql % GRID_W
        kr = NA_QROWS * qb - NA_QROWS // 2 + kl // GRID_W
        kc = kl % GRID_W
        r0 = np.clip(qr - NA_WIN_ROWS // 2, 0, rows - NA_WIN_ROWS)
        c0 = np.clip(qc - NA_WIN_COLS // 2, 0, GRID_W - NA_WIN_COLS)
        ok = ((kr[None, :] >= r0[:, None]) & (kr[None, :] < r0[:, None] + NA_WIN_ROWS)
              & (kc[None, :] >= c0[:, None]) & (kc[None, :] < c0[:, None] + NA_WIN_COLS)
              & (kr[None, :] >= 0) & (kr[None, :] < rows))
        out.append(jnp.where(jnp.asarray(ok)[None], table, NEG))
    return jnp.stack(out)


def _na(proj, rpb, t_lat, t_ctx):
    rows = t_lat // GRID_W
    qn = NA_QROWS * GRID_W
    hn = qn // 2
    nqb = rows // NA_QROWS
    assert nqb >= 2 and t_ctx == hn
    bias = _na_bias(rpb.astype(F32), rows)
    last_half = 2 * nqb - 1
    ctx_blk = t_lat // hn
    cls = lambda b: jnp.where(b == 0, 0, jnp.where(b == nqb - 1, 2, 1))

    nh = NA_HEADS
    w = nh * HEAD_DIM
    groups = N_HEADS // nh

    def kv_specs(col0):
        return [pl.BlockSpec((hn, w), lambda h, b: (jnp.maximum(2 * b - 1, 0), col0 + h)),
                pl.BlockSpec((qn, w), lambda h, b: (b, col0 + h)),
                pl.BlockSpec((hn, w), lambda h, b: (jnp.minimum(2 * b + 2, last_half), col0 + h))]

    qcol, kcol, vcol = 4 * groups, 5 * groups, 6 * groups
    return pl.pallas_call(
        _na_kernel,
        grid=(groups, nqb),
        in_specs=[pl.BlockSpec((qn, w), lambda h, b: (b, qcol + h))]
        + kv_specs(kcol) + kv_specs(vcol)
        + [pl.BlockSpec((hn, w), lambda h, b: (ctx_blk, kcol + h)),
           pl.BlockSpec((hn, w), lambda h, b: (ctx_blk, vcol + h)),
           pl.BlockSpec((None, nh, qn, 2 * qn), lambda h, b: (cls(b), h, 0, 0))],
        out_specs=pl.BlockSpec((qn, w), lambda h, b: (b, h)),
        out_shape=jax.ShapeDtypeStruct((t_lat, HALF), F32),
        compiler_params=_cparams(2),
    )(proj, proj, proj, proj, proj, proj, proj, proj, proj, bias)


def _ctx_attn_kernel(q_ref, k_ref, v_ref, o_ref):
    qb = (q_ref[...].astype(F32) * QK_SCALE).astype(BF16)
    s = _nt(qb, k_ref[...].astype(BF16))
    p = jnp.exp(s - jnp.max(s, axis=1, keepdims=True))
    o_ref[...] = _nn(p.astype(BF16), v_ref[...].astype(BF16)) / jnp.sum(p, axis=1, keepdims=True)


def _ctx_attn(proj, t_lat, t_ctx):
    blk = t_lat // t_ctx
    spec = lambda c: pl.BlockSpec((t_ctx, HEAD_DIM), lambda h, c=c: (blk, c * N_HEADS + h))
    return pl.pallas_call(
        _ctx_attn_kernel,
        grid=(N_HEADS,),
        in_specs=[spec(4), spec(5), spec(6)],
        out_specs=pl.BlockSpec((t_ctx, HEAD_DIM), lambda h: (0, h)),
        out_shape=jax.ShapeDtypeStruct((t_ctx, HALF), F32),
        compiler_params=_cparams(1),
    )(proj, proj, proj)


def _gla_inter(q_ref, i_ref, f_ref, lb_ref, out_ref, st_ref, k_scr, b_scr, *, rev):
    n = q_ref.shape[0]
    mask = _scan_mask(rev, n)
    tri = mask.astype(BF16)
    last = 0 if rev else n - 1
    x = f_ref[...]
    key = lb_ref[0:1, :] * _sigmoid(-x)
    la, lc = lb_ref[1:2, :], lb_ref[2:3, :] + _log_sigmoid(x)
    logf = jnp.maximum(la, lc) + jnp.log1p(jnp.exp(-jnp.abs(la - lc)))
    b = sum(_nn(tri, p) for p in _split3(logf))
    b_end = b[last:last + 1, :]
    qx = q_ref[...]
    q = qx * _sigmoid(qx)
    qt = q * jnp.exp(b)
    kdec = key * jnp.exp(b_end - b)
    w_end = jnp.exp(b_end)
    k_scr[...] = key
    b_scr[...] = b
    st = st_ref[...]
    vb = _by_head(i_ref[...]).astype(BF16)
    o_inter = _bqk(_by_head(qt).astype(BF16), st.astype(BF16))
    for h in range(N_HEADS):
        out_ref[:, h * HEAD_DIM:(h + 1) * HEAD_DIM] = o_inter[h]
    st_ref[...] = st * _by_head(w_end) + jnp.einsum('hnv,hnk->hvk', vb, _by_head(kdec).astype(BF16),
                                                    preferred_element_type=F32)
    return mask, q, b_end


def _gla_intra_factored(mask, q, b_end, i_ref, out_ref, k_scr, b_scr):
    mid = 0.5 * b_end
    qm = q * jnp.exp(b_scr[...] - mid)
    km = k_scr[...] * jnp.exp(mid - b_scr[...])
    att = jnp.where(mask[None], _bqk(_by_head(qm).astype(BF16), _by_head(km).astype(BF16)), 0.0)
    o_intra = _bqd(att.astype(BF16), _by_head(i_ref[...]).astype(BF16))
    for h in range(N_HEADS):
        out_ref[:, h * HEAD_DIM:(h + 1) * HEAD_DIM] += o_intra[h]


def _gla_intra_exact(q, i_ref, out_ref, k_scr, b_scr, *, rev):
    n = q.shape[0]
    t_idx = lax.broadcasted_iota(jnp.int32, (n, 1), 0)
    for h in range(N_HEADS):
        hs = slice(h * HEAD_DIM, (h + 1) * HEAD_DIM)
        q_h = q[:, hs]
        b_h = b_scr[:, hs]

        def body(grp, acc):
            base = pl.multiple_of(grp * 8, 8)
            b8, k8, v8 = b_scr[pl.ds(base, 8), hs], k_scr[pl.ds(base, 8), hs], i_ref[pl.ds(base, 8), hs]
            for r in range(8):
                s = base + r
                e = jnp.exp(jnp.minimum(b_h - b8[r:r + 1, :], 0.0))
                a = jnp.sum(q_h * k8[r:r + 1, :] * e, axis=1, keepdims=True)
                a = jnp.where((t_idx <= s) if rev else (t_idx >= s), a, 0.0)
                acc = acc + a * v8[r:r + 1, :]
            return acc

        out_ref[:, hs] += lax.fori_loop(0, n // 8, body, jnp.zeros((n, HEAD_DIM), F32))


def _gla_kernel(*refs):
    fw_in, bw_in, lb_ref = refs[0:3], refs[3:6], refs[6]
    out_f, out_b = refs[7:9]
    st_f, k_f, b_f, st_b, k_b, b_b = refs[9:]

    @pl.when(pl.program_id(0) == 0)
    def _():
        st_f[...] = jnp.zeros_like(st_f)
        st_b[...] = jnp.zeros_like(st_b)

    mask_f, q_f, end_f = _gla_inter(*fw_in, lb_ref, out_f, st_f, k_f, b_f, rev=False)
    mask_b, q_b, end_b = _gla_inter(*bw_in, lb_ref, out_b, st_b, k_b, b_b, rev=True)
    safe = jnp.minimum(jnp.min(end_f), jnp.min(end_b)) >= 2.0 * GLA_SAFE_LOG

    @pl.when(safe)
    def _():
        _gla_intra_factored(mask_f, q_f, end_f, fw_in[1], out_f, k_f, b_f)
        _gla_intra_factored(mask_b, q_b, end_b, bw_in[1], out_b, k_b, b_b)

    @pl.when(jnp.logical_not(safe))
    def _():
        _gla_intra_exact(q_f, fw_in[1], out_f, k_f, b_f, rev=False)
        _gla_intra_exact(q_b, bw_in[1], out_b, k_b, b_b, rev=True)


def _gla(proj, lb, n_lat_chunks):
    t_all = proj.shape[0]
    n = GLA_CHUNK
    n_chunks = t_all // n

    def dir_specs(rev):
        blk = _scan_block(rev, n_chunks, n_lat_chunks)
        colblk = lambda c: pl.BlockSpec((n, HALF), lambda s, c=c: (blk(s), c))
        return [colblk(0), colblk(1), colblk(3 if rev else 2)], pl.BlockSpec((n, HALF), lambda s: (blk(s), 0))

    (in_f, out_f), (in_b, out_b) = dir_specs(False), dir_specs(True)
    state = [pltpu.VMEM((N_HEADS, HEAD_DIM, HEAD_DIM), F32), pltpu.VMEM((n, HALF), F32),
             pltpu.VMEM((n, HALF), F32)]
    lb = lb.reshape(1, HALF)
    lb_rows = jnp.concatenate([1.0 - lb, jnp.log(lb), jnp.log1p(-lb)], axis=0)
    return pl.pallas_call(
        _gla_kernel,
        grid=(n_chunks,),
        in_specs=in_f + in_b + [pl.BlockSpec((3, HALF), lambda s: (0, 0))],
        out_specs=[out_f, out_b],
        out_shape=[jax.ShapeDtypeStruct((t_all, HALF), F32)] * 2,
        scratch_shapes=state + state,
        compiler_params=_cparams(1),
    )(*([proj] * 6), lb_rows)


def _s5_mats(a_re, a_im, log_dt, b_re, b_im, c_re, c_im):
    hp = lax.Precision.HIGHEST
    n = S5_CHUNK
    dt = jnp.exp(log_dt)[..., None]
    la_re, la_im = a_re * dt, a_im * dt
    mag = jnp.exp(la_re)
    ab_re, ab_im = mag * jnp.cos(la_im), mag * jnp.sin(la_im)
    nr, ni = ab_re - 1.0, ab_im
    den = jnp.square(a_re) + jnp.square(a_im)
    cr = (nr * a_re + ni * a_im) / den
    ci = (ni * a_re - nr * a_im) / den
    bb_re = cr[..., None] * b_re - ci[..., None] * b_im
    bb_im = cr[..., None] * b_im + ci[..., None] * b_re

    def apow(tau):
        tau = jnp.asarray(tau, F32)[None, None, :, None]
        m = jnp.exp(tau * la_re[:, :, None, :])
        return m * jnp.cos(tau * la_im[:, :, None, :]), m * jnp.sin(tau * la_im[:, :, None, :])

    def c_apow(tau):
        pr, pi = apow(tau)
        return (c_re[:, :, None] * pr[:, :, :, None] - c_im[:, :, None] * pi[:, :, :, None],
                c_re[:, :, None] * pi[:, :, :, None] + c_im[:, :, None] * pr[:, :, :, None])

    ca_re, ca_im = c_apow(np.arange(n))
    kern = jnp.einsum('dgtip,dgpj->dgtij', jnp.concatenate([ca_re, -ca_im], axis=-1),
                      jnp.concatenate([bb_re, bb_im], axis=2), precision=hp)
    s_idx, t_idx = np.arange(n)[:, None], np.arange(n)[None, :]
    g = a_re.shape[1]
    lp = n * S5_GROUP

    def toeplitz(kd, lag):
        onehot = jnp.asarray(lag[None] == np.arange(n)[:, None, None], F32)
        return jnp.einsum('lst,glij->gsjti', onehot, kd, precision=hp).reshape(g, lp, lp)

    kmat = toeplitz(kern[0], t_idx - s_idx) + toeplitz(kern[1], s_idx - t_idx)

    def emat(d, tau):
        pr, pi = apow(tau)
        pr, pi = pr[d][:, :, None, :], pi[d][:, :, None, :]
        br, bi = bb_re[d].transpose(0, 2, 1)[:, None], bb_im[d].transpose(0, 2, 1)[:, None]
        return jnp.concatenate([pr * br - pi * bi, pr * bi + pi * br], axis=-1).reshape(g, lp, 2 * S5_STATE)

    def fmat(d, tau):
        fr, fi = c_apow(tau)
        fr, fi = fr[d], fi[d]
        return jnp.concatenate([fr, -fi], axis=-1).transpose(0, 3, 1, 2).reshape(g, 2 * S5_STATE, lp)

    e_all = jnp.concatenate([emat(0, n - 1 - np.arange(n)), emat(1, np.arange(n))], axis=-1)
    f_all = jnp.concatenate([fmat(0, np.arange(n) + 1), fmat(1, n - np.arange(n))], axis=1)
    pr, pi = apow(np.array([n]))
    pr, pi = pr[:, :, 0], pi[:, :, 0]
    ar2 = jnp.concatenate([pr[0], pr[0], pr[1], pr[1]], axis=-1)
    ai2 = jnp.concatenate([-pi[0], pi[0], -pi[1], pi[1]], axis=-1)
    return kmat.astype(BF16), e_all.astype(BF16), f_all.astype(BF16), ar2, ai2


def _s5_state_kernel(u_ref, e_ref, s_ref):
    s_ref[...] = _nn(u_ref[...], e_ref[...])


def _s5_scan_kernel(sf_ref, sb_ref, ar_ref, ai_ref, hf_ref, hb_ref, stf_ref, stb_ref):
    nb = sf_ref.shape[0]
    w = 2 * S5_STATE
    ar, ai = ar_ref[...], ai_ref[...]

    @pl.when(pl.program_id(0) == 0)
    def _():
        stf_ref[...] = jnp.zeros_like(stf_ref)
        stb_ref[...] = jnp.zeros_like(stb_ref)

    def step(h, s, lo):
        return ar[:, lo:lo + w] * h + ai[:, lo:lo + w] * pltpu.roll(h, S5_STATE, 1) + s

    hf, hb = stf_ref[...], stb_ref[...]
    for j in range(nb):
        hf_ref[j] = hf
        hf = step(hf, sf_ref[j], 0)
        hb_ref[nb - 1 - j] = hb
        hb = step(hb, sb_ref[nb - 1 - j], w)
    stf_ref[...] = hf
    stb_ref[...] = hb


def _s5_out_kernel(u_ref, k_ref, h_ref, f_ref, y_ref):
    y_ref[...] = _nn(u_ref[...], k_ref[...]) + _nn(h_ref[...].astype(BF16), f_ref[...])


def _s5(u, mats, n_lat_chunks):
    kmat, emat, fmat, ar2, ai2 = mats
    t_all = u.shape[0]
    n = S5_CHUNK
    nc = t_all // n
    g = S5_GROUPS
    lp = n * S5_GROUP
    w4 = 4 * S5_STATE
    ug = u.astype(BF16).reshape(nc, n, g, S5_GROUP).transpose(2, 0, 1, 3).reshape(g, nc, lp)
    grp = lambda a, b: pl.BlockSpec((None, a, b), lambda i: (i, 0, 0))
    s_end = pl.pallas_call(
        _s5_state_kernel, grid=(g,),
        in_specs=[grp(nc, lp), grp(lp, w4)], out_specs=grp(nc, w4),
        out_shape=jax.ShapeDtypeStruct((g, nc, w4), F32), compiler_params=_cparams(1),
    )(ug, emat)
    nb = S5_SCAN_BLOCK
    w2 = 2 * S5_STATE
    fw_blk = _scan_block(False, nc // nb, n_lat_chunks // nb)
    bw_blk = _scan_block(True, nc // nb, n_lat_chunks // nb)
    s_t = s_end.transpose(1, 0, 2)
    coef = pl.BlockSpec((g, w4), lambda i: (0, 0))
    h_fw, h_bw = pl.pallas_call(
        _s5_scan_kernel, grid=(nc // nb,),
        in_specs=[pl.BlockSpec((nb, g, w2), lambda i: (fw_blk(i), 0, 0)),
                  pl.BlockSpec((nb, g, w2), lambda i: (bw_blk(i), 0, 1)), coef, coef],
        out_specs=[pl.BlockSpec((nb, g, w2), lambda i: (fw_blk(i), 0, 0)),
                   pl.BlockSpec((nb, g, w2), lambda i: (bw_blk(i), 0, 0))],
        out_shape=[jax.ShapeDtypeStruct((nc, g, w2), F32)] * 2,
        scratch_shapes=[pltpu.VMEM((g, w2), F32)] * 2,
        compiler_params=_cparams(1),
    )(s_t, s_t, ar2, ai2)
    h_in = jnp.concatenate([h_fw, h_bw], axis=-1).transpose(1, 0, 2)
    y = pl.pallas_call(
        _s5_out_kernel, grid=(g,),
        in_specs=[grp(nc, lp), grp(lp, lp), grp(nc, w4), grp(w4, lp)], out_specs=grp(nc, lp),
        out_shape=jax.ShapeDtypeStruct((g, nc, lp), F32), compiler_params=_cparams(1),
    )(ug, kmat, h_in, fmat)
    return y.reshape(g, nc, n, S5_GROUP).transpose(1, 2, 0, 3).reshape(t_all, HALF)


def _top_rows_sorted(x, k):
    n = x.shape[0] // 8
    cols = [x[8 * i:8 * (i + 1), :] for i in range(n)]
    size = 2
    while size <= n:
        stride = size // 2
        while stride >= 1:
            for i in range(n):
                j = i ^ stride
                if j > i:
                    hi, lo = jnp.maximum(cols[i], cols[j]), jnp.minimum(cols[i], cols[j])
                    cols[i], cols[j] = (hi, lo) if (i & size) == 0 else (lo, hi)
            stride //= 2
        size *= 2
    rows = []
    for r in range(k):
        mx = jnp.max(cols[0], axis=0, keepdims=True)
        rows.append(mx)
        popped = cols[0] == mx
        live = min(n, k - r)
        for i in range(live):
            nxt = cols[i + 1] if i + 1 < n else NEG
            cols[i] = jnp.where(popped, nxt, cols[i])
    return rows


def _peer_route_kernel(ht_ref, wq_ref, keys_ref, e1_ref, e0_ref, th_ref, top_scr):
    kk = PEER_TOPK
    qt = _nn(wq_ref[...], ht_ref[...])
    for hp in range(2 * PEER_HEADS):
        sc = _nn(keys_ref[hp], qt[hp * HEAD_DIM:(hp + 1) * HEAD_DIM, :].astype(BF16))
        h, p = divmod(hp, 2)
        if p == 0:
            e0_ref[h] = sc
        else:
            e1_ref[h] = sc
        for r, mx in enumerate(_top_rows_sorted(sc, kk + 1)):
            top_scr[hp, r:r + 1, :] = mx
        top_scr[hp, kk + 1:, :] = jnp.full((PEER_TOP_ROWS - kk - 1, sc.shape[1]), NEG, F32)

    row = lax.broadcasted_iota(jnp.int32, (PEER_TOP_ROWS, 1), 0)
    for h in range(PEER_HEADS):
        a = top_scr[2 * h]
        b = top_scr[2 * h + 1]
        parts = [a[0:1] + b, a[1:2] + b[0:8], a[2:3] + b[0:8], a[3:4] + b[0:8]]
        parts += [jnp.where(row >= 4, a + b[j:j + 1], NEG) for j in range(3)]
        cand = jnp.maximum(jnp.concatenate(parts, axis=0), NEG)
        pad = jnp.full((PEER_N_KEYS - cand.shape[0], cand.shape[1]), NEG, F32)
        top = _top_rows_sorted(jnp.concatenate([cand, pad], axis=0), kk + 1)
        thr = 0.5 * (top[kk - 1] + top[kk])
        z = jnp.sum(jnp.where(cand >= thr, jnp.exp(cand - top[0]), 0.0), axis=0, keepdims=True)
        e0_ref[h] = jnp.exp(e0_ref[h] - a[0:1]) / z
        e1_ref[h] = jnp.exp(e1_ref[h] - b[0:1])
        th_ref[h:h + 1, :] = jnp.exp(thr - top[0]) / z


def _peer_expert_kernel(ht_ref, e1_ref, e0_ref, th_ref, *rest, n_tiles):
    u_refs = rest[:PEER_U_SPLIT]
    vt_ref, o_ref, at0_scr, at1_scr, g0_scr, g1_scr = rest[PEER_U_SPLIT:]
    s = pl.program_id(1)
    assert n_tiles % 2 == 0 and n_tiles >= 4
    ni = u_refs[0].shape[0] // PEER_N_KEYS
    kc = u_refs[0].shape[1]
    at_scr, g_scr = (at0_scr, at1_scr), (g0_scr, g1_scr)

    def stages(par, dot1, gate, dot2, first=False):
        if dot1:
            at_scr[par][...] = _nn(jnp.concatenate([u_k[...] for u_k in u_refs], axis=1), ht_ref[...])
        if gate:
            at_old, g_new = at_scr[1 - par], g_scr[1 - par]
            nj, nc = PEER_GATE_ROWS, 128
            for ii in range(ni):
                i = (s - 1) * ni + ii
                e0_rows = [e0_ref[h, pl.ds(i, 1), :] for h in range(PEER_HEADS)]
                for c in range(at_old.shape[1] // nc):
                    cs = slice(c * nc, (c + 1) * nc)
                    for jb in range(PEER_N_KEYS // nj):
                        js = slice(jb * nj, (jb + 1) * nj)
                        w = None
                        for h in range(PEER_HEADS):
                            p = e1_ref[h, js, cs] * e0_rows[h][:, cs]
                            t = jnp.where(p >= th_ref[h:h + 1, cs], p, 0.0)
                            w = t if w is None else w + t
                        rs = slice(ii * PEER_N_KEYS + jb * nj, ii * PEER_N_KEYS + (jb + 1) * nj)
                        g_new[rs, cs] = (_gelu_tanh(at_old[rs, cs]) * w).astype(BF16)
        if dot2:
            prod = _nn(vt_ref[...], g_scr[par][...])
            if first:
                o_ref[...] = prod
            else:
                o_ref[...] += prod

    mid = jnp.logical_and(s > 2, s < n_tiles)
    pl.when(s == 0)(lambda: stages(0, True, False, False))
    pl.when(s == 1)(lambda: stages(1, True, True, False))
    pl.when(s == 2)(lambda: stages(0, True, True, True, first=True))
    pl.when(jnp.logical_and(mid, s % 2 == 0))(lambda: stages(0, True, True, True))
    pl.when(jnp.logical_and(mid, s % 2 == 1))(lambda: stages(1, True, True, True))
    pl.when(s == n_tiles)(lambda: stages(0, False, True, True))
    pl.when(s == n_tiles + 1)(lambda: stages(1, False, False, True))


def _tok3(t):
    return pl.BlockSpec((PEER_HEADS, PEER_N_KEYS, t), lambda i, *_: (0, 0, i))


def _peer_route(h2t, wq_t_bf, keys_bf):
    d, t_all = h2t.shape
    nh, nk = PEER_HEADS, PEER_N_KEYS
    t1 = PEER_P1_TILE
    return pl.pallas_call(
        _peer_route_kernel,
        grid=(t_all // t1,),
        in_specs=[pl.BlockSpec((d, t1), lambda i: (0, i)),
                  pl.BlockSpec(wq_t_bf.shape, lambda i: (0, 0)),
                  pl.BlockSpec(keys_bf.shape, lambda i: (0, 0, 0))],
        out_specs=[_tok3(t1), _tok3(t1), pl.BlockSpec((nh, t1), lambda i: (0, i))],
        out_shape=[jax.ShapeDtypeStruct((nh, nk, t_all), F32)] * 2 + [jax.ShapeDtypeStruct((nh, t_all), F32)],
        scratch_shapes=[pltpu.VMEM((2 * nh, PEER_TOP_ROWS, t1), F32)],
        compiler_params=_cparams(1),
    )(h2t, wq_t_bf, keys_bf)


def _padded_tokens(t_all):
    return t_all + (-t_all % PEER_TOK_TILE)


def _peer(h2t, wq_t_bf, keys_bf, u_r, vt_r, layer=0):
    return _peer_experts(h2t, *_peer_route(h2t, wq_t_bf, keys_bf), u_r, vt_r, layer)


def _peer_tables(u_all, v_all):
    return u_all.astype(BF16), v_all.transpose(0, 2, 1).astype(BF16)


def _peer_experts(h2t, e1, e0, th, u_r, vt_r, layer):
    d, t_all = h2t.shape
    tt = PEER_TOK_TILE
    et = PEER_EXP_TILE
    n_tiles = u_r.shape[1] // et
    kc = d // PEER_U_SPLIT
    tok = pl.BlockSpec((d, tt), lambda i, s: (0, i))
    return pl.pallas_call(
        functools.partial(_peer_expert_kernel, n_tiles=n_tiles),
        grid=(t_all // tt, n_tiles + 2),
        in_specs=[tok, _tok3(tt), _tok3(tt), pl.BlockSpec((PEER_HEADS, tt), lambda i, s: (0, i))]
        + [pl.BlockSpec((None, et, kc), lambda i, s, k=k: (layer, jnp.minimum(s, n_tiles - 1), k))
           for k in range(PEER_U_SPLIT)]
        + [pl.BlockSpec((None, d, et), lambda i, s: (layer, 0, jnp.clip(s - 2, 0, n_tiles - 1)))],
        out_specs=tok,
        out_shape=jax.ShapeDtypeStruct((d, t_all), F32),
        scratch_shapes=[pltpu.VMEM((et, tt), F32)] * 2 + [pltpu.VMEM((et, tt), BF16)] * 2,
        compiler_params=_cparams(2),
    )(h2t, e1, e0, th, *([u_r] * PEER_U_SPLIT), vt_r)


def _rope_tables(t_lat, t_ctx):
    pos = np.arange(t_lat)
    nfreq = HEAD_DIM // 4
    freqs = ROPE_BASE ** (-jnp.arange(nfreq, dtype=F32) / nfreq)
    ang_r = jnp.asarray(pos // GRID_W, F32)[:, None] * freqs
    ang_c = jnp.asarray(pos % GRID_W, F32)[:, None] * freqs
    cos = jnp.concatenate([jnp.cos(ang_r), jnp.cos(ang_r), jnp.cos(ang_c), jnp.cos(ang_c)], axis=1)
    sin = jnp.concatenate([-jnp.sin(ang_r), jnp.sin(ang_r), -jnp.sin(ang_c), jnp.sin(ang_c)], axis=1)
    cos = jnp.concatenate([cos, jnp.ones((t_ctx, HEAD_DIM), F32)], axis=0)
    sin = jnp.concatenate([sin, jnp.zeros((t_ctx, HEAD_DIM), F32)], axis=0)
    return cos, sin


def _peer_and_norm(x1, h2t, mods, ln_g, ln_b, n_lat_tiles, w_q, sub_keys, tables, layer, mods_next=None):
    keys = sub_keys.reshape(2 * PEER_HEADS, PEER_N_KEYS, -1).astype(BF16)
    pt = _peer(h2t, w_q.T.astype(BF16), keys, *tables, layer)
    return _ln2(x1, pt, mods, ln_g, ln_b, n_lat_tiles, mods_next)


def _layer_ab(x_all, mods, t_lat, t_ctx, w_in, w_out, igate_b, fgate_b, rpb, ln_g, ln_b, xm=None):
    n_lat_tiles = t_lat // ROW_TILE
    if xm is None:
        xm = _modulate(x_all, mods, n_lat_tiles)
    ng = 4 * N_HEADS
    w_main = jnp.concatenate([w_in[:, :4 * HALF], w_in[:, 4 * HALF + ng:]], axis=1).astype(BF16)
    w_gate = jnp.pad(w_in[:, 4 * HALF:4 * HALF + ng], ((0, 0), (0, 128 - ng))).astype(BF16)
    proj = _inproj(xm, w_main, BF16)
    gates = _inproj(xm, w_gate)
    cos_t, sin_t = _rope_tables(t_lat, t_ctx)
    h_fw, h_bw = _mlstm(proj, gates, gates.T, cos_t, sin_t, igate_b, fgate_b, t_lat // SCAN_CHUNK)
    y_na = jnp.concatenate([_na(proj, rpb, t_lat, t_ctx), _ctx_attn(proj, t_lat, t_ctx)], axis=0)
    return _outproj([(h_fw, 0), (h_bw, 0), (proj, 3)], [(y_na, 0)], w_out.astype(BF16), x_all, mods,
                    ln_g, ln_b, n_lat_tiles)


def _layer_cd(x_all, mods, t_lat, w_in, w_out, lb, s5p, s5_d, glu_w, glu_b, ln_g, ln_b, xm=None):
    n_lat_tiles = t_lat // ROW_TILE
    if xm is None:
        xm = _modulate(x_all, mods, n_lat_tiles)
    proj = _inproj(xm, w_in.astype(BF16))
    o_fw, o_bw = _gla(proj, lb, t_lat // GLA_CHUNK)
    y5 = _s5(proj[:, 5 * HALF:], _s5_mats(*[p.astype(F32) for p in s5p]), t_lat // S5_CHUNK)
    return _outproj([(o_fw, 0), (o_bw, 0), (proj, 4)], [(y5, 0), (proj, 5)], w_out.astype(BF16), x_all, mods,
                    ln_g, ln_b, n_lat_tiles,
                    glu_params=(s5_d.astype(F32), glu_w.astype(BF16), glu_b.astype(F32)))


def kernel(x, c, ctx, c_ctx, ada_w, ada_b, ln_g, ln_b, ab_w_in, ab_w_out, mlstm_igate_b, mlstm_fgate_b, na_rpb, cd_w_in, cd_w_out, hgrn_lb_logits, s5_a_re, s5_a_im, s5_log_dt, s5_b_re, s5_b_im, s5_c_re, s5_c_im, s5_d, s5_glu_w, s5_glu_b, peer_w_q, peer_sub_keys, peer_u, peer_v):
    t_lat, t_ctx = x.shape[1], ctx.shape[1]
    assert x.shape[0] == 1 and t_ctx == ROW_TILE and t_lat % (NA_QROWS * GRID_W) == 0
    depth = ada_w.shape[0]
    lb_soft = jax.nn.softmax(hgrn_lb_logits.astype(F32), axis=0)
    lower_bounds = jnp.cumsum(lb_soft, axis=0) - lb_soft[0]
    x_all = jnp.concatenate([x[0], ctx[0]], axis=0).astype(F32)
    n_lat_tiles = t_lat // ROW_TILE
    mods_all = [_adaln(c, c_ctx, ada_w, ada_b, l) for l in range(depth)]
    xm = _modulate(x_all, mods_all[0], n_lat_tiles)
    tables = _peer_tables(peer_u, peer_v)
    for l in range(depth):
        j = l // 2
        mods = mods_all[l]
        if l % 2 == 0:
            x1, h2t = _layer_ab(x_all, mods, t_lat, t_ctx, ab_w_in[j], ab_w_out[j], mlstm_igate_b[j],
                                mlstm_fgate_b[j], na_rpb[j], ln_g[l, 0], ln_b[l, 0], xm=xm)
        else:
            s5p = (s5_a_re[j], s5_a_im[j], s5_log_dt[j], s5_b_re[j], s5_b_im[j], s5_c_re[j], s5_c_im[j])
            x1, h2t = _layer_cd(x_all, mods, t_lat, cd_w_in[j], cd_w_out[j], lower_bounds[l], s5p, s5_d[j],
                                s5_glu_w[j], s5_glu_b[j], ln_g[l, 0], ln_b[l, 0], xm=xm)
        peer_args = (x1, h2t, mods, ln_g[l, 1], ln_b[l, 1], n_lat_tiles, peer_w_q[l], peer_sub_keys[l],
                     tables, l)
        if l + 1 < depth:
            x_all, xm = _peer_and_norm(*peer_args, mods_next=mods_all[l + 1])
        else:
            x_all = _peer_and_norm(*peer_args)
    return x_all[None].astype(x.dtype)
```

```python
import functools
import math

import numpy as np
import jax
import jax.numpy as jnp
from jax import lax
from jax.experimental import pallas as pl
from jax.experimental.pallas import tpu as pltpu

F32 = jnp.float32
BF16 = jnp.bfloat16

HEAD_DIM = 128
GRID_W = 64
N_HEADS = 8
HALF = N_HEADS * HEAD_DIM
NA_WIN_ROWS = 8
NA_WIN_COLS = 16
ROPE_BASE = 10000.0
S5_GROUP = 16
S5_STATE = 64
S5_GROUPS = HALF // S5_GROUP
PEER_HEADS = 8
PEER_N_KEYS = 128
PEER_TOPK = 16
PEER_TOP_ROWS = 24
LN_EPS = 1e-5
DEPTH = 2
ALPHA = (2.0 * DEPTH) ** 0.25
QK_SCALE = HEAD_DIM ** -0.5

ROW_TILE = 256
MM_ROW_TILE = 1280
MM_COL_TILE = 1024
SCAN_CHUNK = 256
GLA_CHUNK = 64
S5_CHUNK = 32
S5_SCAN_BLOCK = 8
NA_HEADS = 4
NA_QROWS = 8
PEER_P1_TILE = 256
PEER_TOK_TILE = 512
PEER_EXP_TILE = 512
PEER_GATE_ROWS = 32
PEER_U_SPLIT = 4
NEG = -1e30
GLA_SAFE_LOG = -80.0
VMEM_LIMIT = 56 * 1024 * 1024


def _cparams(n_axes):
    return pltpu.CompilerParams(dimension_semantics=("arbitrary",) * n_axes,
                                vmem_limit_bytes=VMEM_LIMIT)


def _nt(a, b):
    return lax.dot_general(a, b, (((1,), (1,)), ((), ())), preferred_element_type=F32)


def _nn(a, b):
    return jnp.dot(a, b, preferred_element_type=F32)


def _split3(x):
    hi = x.astype(BF16)
    r = x - hi.astype(F32)
    mid = r.astype(BF16)
    lo = (r - mid.astype(F32)).astype(BF16)
    return hi, mid, lo


def _log_sigmoid(x):
    return jnp.minimum(x, 0.0) - jnp.log1p(jnp.exp(-jnp.abs(x)))


def _sigmoid(x):
    return 1.0 / (1.0 + jnp.exp(-x))


def _gelu_tanh(x):
    return 0.5 * x * (1.0 + jnp.tanh(math.sqrt(2.0 / math.pi) * (x + 0.044715 * (x * x * x))))


def _head_norm(x):
    mu = jnp.mean(x, axis=-1, keepdims=True)
    xc = x - mu
    return xc * lax.rsqrt(jnp.mean(xc * xc, axis=-1, keepdims=True) + LN_EPS)


def _adaln_kernel(ct_ref, w_ref, b_ref, o_ref):
    ct = ct_ref[...]
    s = ct * _sigmoid(ct)
    w = w_ref[...]
    o_ref[0:1, :] = jnp.sum(s[:, 0:1] * w, axis=0, keepdims=True) + b_ref[...]
    o_ref[1:2, :] = jnp.sum(s[:, 1:2] * w, axis=0, keepdims=True) + b_ref[...]


def _adaln(c_lat, c_ctx, w_all, b_all, layer):
    n_layers, d, n = w_all.shape
    tn = n // 8
    ct = jnp.stack([c_lat.reshape(d), c_ctx.reshape(d)], axis=1)
    out = pl.pallas_call(
        _adaln_kernel,
        grid=(n // tn,),
        in_specs=[pl.BlockSpec((d, 2), lambda j: (0, 0)),
                  pl.BlockSpec((None, d, tn), lambda j: (layer, 0, j)),
                  pl.BlockSpec((None, 1, tn), lambda j: (layer, 0, j))],
        out_specs=pl.BlockSpec((2, tn), lambda j: (0, j)),
        out_shape=jax.ShapeDtypeStruct((2, n), F32),
        compiler_params=_cparams(1),
    )(ct, w_all, b_all.reshape(n_layers, 1, n))
    return jnp.pad(out.reshape(2, 6, d), ((0, 0), (0, 2), (0, 0)))


def _modulated(x, m_ref):
    return (x * (1.0 + m_ref[1:2, :]) + m_ref[0:1, :]).astype(BF16)


def _modulate_kernel(x_ref, m_ref, o_ref):
    o_ref[...] = _modulated(x_ref[...], m_ref)


def _mods_spec(d, n_lat_tiles):
    return pl.BlockSpec((None, 8, d), lambda i: (jnp.where(i < n_lat_tiles, 0, 1), 0, 0))


def _modulate(x_all, mods, n_lat_tiles):
    t_all, d = x_all.shape
    row = pl.BlockSpec((ROW_TILE, d), lambda i: (i, 0))
    return pl.pallas_call(
        _modulate_kernel,
        grid=(t_all // ROW_TILE,),
        in_specs=[row, _mods_spec(d, n_lat_tiles)],
        out_specs=row,
        out_shape=jax.ShapeDtypeStruct((t_all, d), BF16),
        compiler_params=_cparams(1),
    )(x_all, mods)


def _matmul_kernel(x_ref, w_ref, o_ref):
    o_ref[...] = _nn(x_ref[...], w_ref[...]).astype(o_ref.dtype)


def _inproj(xm, w_bf, out_dtype=F32):
    t_all, d = xm.shape
    n = w_bf.shape[1]
    tm = MM_ROW_TILE if t_all % MM_ROW_TILE == 0 else ROW_TILE
    tn = min(n, MM_COL_TILE)
    return pl.pallas_call(
        _matmul_kernel,
        grid=(t_all // tm, n // tn),
        in_specs=[pl.BlockSpec((tm, d), lambda i, j: (i, 0)),
                  pl.BlockSpec((d, tn), lambda i, j: (0, j))],
        out_specs=pl.BlockSpec((tm, tn), lambda i, j: (i, j)),
        out_shape=jax.ShapeDtypeStruct((t_all, n), out_dtype),
        compiler_params=_cparams(2),
    )(xm, w_bf)


def _ln_rows(z, g, b):
    mu = jnp.mean(z, axis=-1, keepdims=True)
    zc = z - mu
    return zc * lax.rsqrt(jnp.mean(zc * zc, axis=-1, keepdims=True) + LN_EPS) * g + b


def _outproj_kernel(*refs, glu, n_row_tiles):
    hf_ref, hb_ref, gate_ref = refs[:3]
    refs = refs[3:]
    if glu:
        (y5_ref, u5_ref, ds_ref, gw_ref, gb_ref, w1_ref, w2_ref, x_ref, m_ref, g_ref, b_ref,
         x1_ref, h2_ref) = refs
        y = _gelu_tanh(y5_ref[...] + ds_ref[...] * u5_ref[...])
        a2 = y * _sigmoid(_nn(y.astype(BF16), gw_ref[...]) + gb_ref[...])
    else:
        a2_ref, w1_ref, w2_ref, x_ref, m_ref, g_ref, b_ref, x1_ref, h2_ref = refs
        a2 = a2_ref[...]
    heads = []
    for h in range(N_HEADS):
        hs = slice(h * HEAD_DIM, (h + 1) * HEAD_DIM)
        gx = gate_ref[:, hs].astype(F32)
        gate = gx * _sigmoid(gx) if glu else _sigmoid(gx)
        heads.append((_head_norm(hf_ref[:, hs] + hb_ref[:, hs]) * gate).astype(BF16))
    a1 = jnp.concatenate(heads, axis=1)
    y = _nn(a1, w1_ref[...]) + _nn(a2.astype(BF16), w2_ref[...])
    x1 = _ln_rows(ALPHA * x_ref[...] + m_ref[2:3, :] * y, g_ref[...], b_ref[...])
    x1_ref[...] = x1
    h2t = (x1 * (1.0 + m_ref[4:5, :]) + m_ref[3:4, :]).T
    h2_ref[...] = jnp.where(pl.program_id(0) < n_row_tiles, h2t, 0.0).astype(BF16)


def _outproj(scan_parts, a2_parts, w_out_bf, x_all, mods, ln_g, ln_b, n_lat_tiles, glu_params=None):
    t_all, d = x_all.shape
    half = HALF
    n_row_tiles = t_all // ROW_TILE
    t_pad = _padded_tokens(t_all)
    row = lambda i: (jnp.minimum(i, n_row_tiles - 1), 0)
    const = lambda i: (0, 0)
    a2_parts = list(scan_parts) + list(a2_parts)
    a2_specs = [pl.BlockSpec((ROW_TILE, half), lambda i, c=c: (jnp.minimum(i, n_row_tiles - 1), c))
                for _, c in a2_parts]
    a2_args = [a for a, _ in a2_parts]
    if glu_params is not None:
        d_skip, glu_w_bf, glu_b = glu_params
        a2_specs += [pl.BlockSpec((1, half), const), pl.BlockSpec((half, half), const),
                     pl.BlockSpec((1, half), const)]
        a2_args += [d_skip.reshape(1, half), glu_w_bf, glu_b.reshape(1, half)]
    return pl.pallas_call(
        functools.partial(_outproj_kernel, glu=glu_params is not None, n_row_tiles=n_row_tiles),
        grid=(t_pad // ROW_TILE,),
        in_specs=a2_specs + [
            pl.BlockSpec((half, d), const), pl.BlockSpec((half, d), const),
            pl.BlockSpec((ROW_TILE, d), row),
            pl.BlockSpec((None, 8, d), lambda i: (jnp.where(i < n_lat_tiles, 0, 1), 0, 0)),
            pl.BlockSpec((1, d), const), pl.BlockSpec((1, d), const)],
        out_specs=[pl.BlockSpec((ROW_TILE, d), row), pl.BlockSpec((d, ROW_TILE), lambda i: (0, i))],
        out_shape=[jax.ShapeDtypeStruct((t_all, d), F32), jax.ShapeDtypeStruct((d, t_pad), BF16)],
        compiler_params=_cparams(1),
    )(*a2_args, w_out_bf[:half], w_out_bf[half:], x_all, mods, ln_g.reshape(1, d), ln_b.reshape(1, d))


def _ln2_kernel(x_ref, p_ref, m_ref, g_ref, b_ref, *rest):
    x2 = _ln_rows(ALPHA * x_ref[...] + m_ref[5:6, :] * p_ref[...].T, g_ref[...], b_ref[...])
    if len(rest) == 1:
        rest[0][...] = x2
    else:
        mn_ref, o_ref, xm_ref = rest
        o_ref[...] = x2
        xm_ref[...] = _modulated(x2, mn_ref)


def _ln2(x1, peer_out_t, mods, ln_g, ln_b, n_lat_tiles, mods_next=None):
    t_all, d = x1.shape
    row = pl.BlockSpec((ROW_TILE, d), lambda i: (i, 0))
    col = pl.BlockSpec((d, ROW_TILE), lambda i: (0, i))
    vec = pl.BlockSpec((1, d), lambda i: (0, 0))
    nxt = mods_next is not None
    n_rows = t_all if nxt else n_lat_tiles * ROW_TILE
    return pl.pallas_call(
        _ln2_kernel,
        grid=(n_rows // ROW_TILE,),
        in_specs=[row, col, _mods_spec(d, n_lat_tiles), vec, vec] + ([_mods_spec(d, n_lat_tiles)] if nxt else []),
        out_specs=[row, row] if nxt else row,
        out_shape=([jax.ShapeDtypeStruct((t_all, d), F32), jax.ShapeDtypeStruct((t_all, d), BF16)] if nxt
                   else jax.ShapeDtypeStruct((n_rows, d), F32)),
        compiler_params=_cparams(1),
    )(x1, peer_out_t, mods, ln_g.reshape(1, d), ln_b.reshape(1, d), *([mods_next] if nxt else []))


def _scan_block(rev, n_chunks, n_lat_chunks):
    if rev:
        return lambda s: n_chunks - 1 - s
    n_ctx = n_chunks - n_lat_chunks
    return lambda s: jnp.where(s < n_ctx, s + n_lat_chunks, s - n_ctx)


def _scan_mask(rev, n):
    row = lax.broadcasted_iota(jnp.int32, (n, n), 0)
    col = lax.broadcasted_iota(jnp.int32, (n, n), 1)
    return (col >= row) if rev else (col <= row)


def _rope(x, cos, sin):
    lane = lax.broadcasted_iota(jnp.int32, x.shape, 1)
    partner = jnp.where((lane % 64) < 32, pltpu.roll(x, 96, 1), pltpu.roll(x, 32, 1))
    return x * cos + partner * sin


def _mlstm_kernel(*refs):
    n_in = 5
    fw_in, bw_in = refs[:n_in], refs[n_in:2 * n_in]
    bias_ref, biast_ref, out_f, out_b = refs[2 * n_in:2 * n_in + 4]
    state = refs[2 * n_in + 4:]

    @pl.when(pl.program_id(0) == 0)
    def _():
        for r in state:
            r[...] = jnp.zeros_like(r)

    _mlstm_chunk(*fw_in, bias_ref, biast_ref, out_f, *state[:2], rev=False)
    _mlstm_chunk(*bw_in, bias_ref, biast_ref, out_b, *state[2:], rev=True)


def _mlstm_chunk(q_ref, k_ref, v_ref, g_ref, gt_ref, bias_ref, biast_ref, out_ref, cn_ref, m_ref, *, rev):
    n = q_ref.shape[0]
    mask = _scan_mask(rev, n)
    tri = mask.astype(BF16)
    pre_c = g_ref[...] + bias_ref[...]
    pre_r = gt_ref[...] + biast_ref[...]
    b_c = sum(_nn(tri, p) for p in _split3(_log_sigmoid(pre_c)))
    b_r = sum(_nt(p, tri) for p in _split3(_log_sigmoid(pre_r)))
    last = 0 if rev else n - 1
    off = N_HEADS if rev else 0
    heads = range(N_HEADS)
    bc = jnp.stack([b_c[:, 2 * N_HEADS + off + h:2 * N_HEADS + off + h + 1] for h in heads])
    br = jnp.stack([b_r[2 * N_HEADS + off + h:2 * N_HEADS + off + h + 1, :] for h in heads])
    ic = jnp.stack([pre_c[:, off + h:off + h + 1] for h in heads])
    ir = jnp.stack([pre_r[off + h:off + h + 1, :] for h in heads])
    m_prev = m_ref[:, :, 0:1]
    logw = jnp.where(mask[None], bc - br + ir, NEG)
    inter = bc + m_prev
    m_t = jnp.maximum(inter, jnp.max(logw, axis=2, keepdims=True))
    w_inter = jnp.exp(inter - m_t)
    qb, kb = _by_head(q_ref[...]), _by_head(k_ref[...])
    vb = _by_head(v_ref[...]).astype(BF16)
    v_ext = jnp.concatenate([vb, jnp.ones_like(vb)], axis=2)
    s = _bqk(qb, kb) * jnp.exp(logw - m_t)
    cn_old = cn_ref[...]
    both = _bqd(s.astype(BF16), v_ext) + _bqd(qb, cn_old.astype(BF16)) * w_inter
    den = jnp.maximum(jnp.abs(both[:, :, HEAD_DIM:]), jnp.exp(-m_t))
    out = both[:, :, :HEAD_DIM] / den
    for h in heads:
        out_ref[:, h * HEAD_DIM:(h + 1) * HEAD_DIM] = out[h]
    m_new = m_t[:, last:last + 1, :]
    b_last = bc[:, last:last + 1, :]
    w_old = jnp.exp(b_last + m_prev - m_new)
    kw = kb.astype(F32) * jnp.exp(b_last - bc + ic - m_new)
    cn_ref[...] = w_old * cn_old + jnp.einsum('hkm,hkd->hmd', kw.astype(BF16), v_ext, preferred_element_type=F32)
    m_ref[...] = jnp.broadcast_to(m_new, m_ref.shape)


def _rope_kernel(q_ref, k_ref, cos_ref, sin_ref, qo_ref, ko_ref):
    cos, sin = cos_ref[...], sin_ref[...]
    for h in range(N_HEADS):
        hs = slice(h * HEAD_DIM, (h + 1) * HEAD_DIM)
        qo_ref[:, hs] = (_rope(q_ref[:, hs].astype(F32), cos, sin) * QK_SCALE).astype(BF16)
        ko_ref[:, hs] = _rope(k_ref[:, hs].astype(F32), cos, sin).astype(BF16)


def _rope_qk(proj, cos_t, sin_t):
    t_all = proj.shape[0]
    tok = pl.BlockSpec((ROW_TILE, 128), lambda i: (i, 0))
    return pl.pallas_call(
        _rope_kernel,
        grid=(t_all // ROW_TILE,),
        in_specs=[pl.BlockSpec((ROW_TILE, HALF), lambda i: (i, 0)),
                  pl.BlockSpec((ROW_TILE, HALF), lambda i: (i, 1)), tok, tok],
        out_specs=[pl.BlockSpec((ROW_TILE, HALF), lambda i: (i, 0))] * 2,
        out_shape=[jax.ShapeDtypeStruct((t_all, HALF), BF16)] * 2,
        compiler_params=_cparams(1),
    )(proj, proj, cos_t, sin_t)


def _mlstm(proj, gates, gates_t, cos_t, sin_t, igate_b, fgate_b, n_lat_chunks):
    t_all = proj.shape[0]
    n = SCAN_CHUNK
    n_chunks = t_all // n
    bias = jnp.zeros((1, 128), F32).at[0, :4 * N_HEADS].set(
        jnp.concatenate([igate_b.reshape(-1), fgate_b.reshape(-1)]))
    q_rot, k_rot = _rope_qk(proj, cos_t, sin_t)

    def dir_specs(rev):
        blk = _scan_block(rev, n_chunks, n_lat_chunks)
        colblk = lambda c: pl.BlockSpec((n, HALF), lambda s, c=c: (blk(s), c))
        ins = [colblk(0), colblk(0), colblk(2), pl.BlockSpec((n, 128), lambda s: (blk(s), 0)),
               pl.BlockSpec((128, n), lambda s: (0, blk(s)))]
        return ins, pl.BlockSpec((n, HALF), lambda s: (blk(s), 0))

    (in_f, out_f), (in_b, out_b) = dir_specs(False), dir_specs(True)
    dir_args = [q_rot, k_rot, proj, gates, gates_t]
    state = [pltpu.VMEM((N_HEADS, HEAD_DIM, 2 * HEAD_DIM), F32), pltpu.VMEM((N_HEADS, 1, HEAD_DIM), F32)]
    return pl.pallas_call(
        _mlstm_kernel,
        grid=(n_chunks,),
        in_specs=in_f + in_b + [pl.BlockSpec((1, 128), lambda s: (0, 0)), pl.BlockSpec((128, 1), lambda s: (0, 0))],
        out_specs=[out_f, out_b],
        out_shape=[jax.ShapeDtypeStruct((t_all, HALF), F32)] * 2,
        scratch_shapes=state + state,
        compiler_params=_cparams(1),
    )(*dir_args, *dir_args, bias, bias.reshape(128, 1))


def _by_head(x):
    return jnp.stack([x[:, h * HEAD_DIM:(h + 1) * HEAD_DIM] for h in range(x.shape[1] // HEAD_DIM)])


def _bqk(a, b):
    return jnp.einsum('hqd,hkd->hqk', a, b, preferred_element_type=F32)


def _bqd(p, v):
    return jnp.einsum('hqk,hkd->hqd', p, v, preferred_element_type=F32)


def _na_kernel(q_ref, kp_ref, kc_ref, kn_ref, vp_ref, vc_ref, vn_ref, kx_ref, vx_ref, bias_ref, o_ref):
    qb = _by_head(q_ref[...].astype(F32) * QK_SCALE).astype(BF16)
    kk = _by_head(jnp.concatenate([kp_ref[...], kc_ref[...], kn_ref[...]], axis=0)).astype(BF16)
    vv = _by_head(jnp.concatenate([vp_ref[...], vc_ref[...], vn_ref[...]], axis=0)).astype(BF16)
    s_loc = _bqk(qb, kk) + bias_ref[...]
    s_ctx = _bqk(qb, _by_head(kx_ref[...]).astype(BF16))
    m = jnp.maximum(jnp.max(s_loc, axis=2, keepdims=True), jnp.max(s_ctx, axis=2, keepdims=True))
    p_loc = jnp.exp(s_loc - m)
    p_ctx = jnp.exp(s_ctx - m)
    l = jnp.sum(p_loc, axis=2, keepdims=True) + jnp.sum(p_ctx, axis=2, keepdims=True)
    out = (_bqd(p_loc.astype(BF16), vv) + _bqd(p_ctx.astype(BF16), _by_head(vx_ref[...]).astype(BF16))) / l
    for h in range(out.shape[0]):
        o_ref[:, h * HEAD_DIM:(h + 1) * HEAD_DIM] = out[h]


def _na_bias(rpb, rows):
    qn = NA_QROWS * GRID_W
    kn = 2 * qn
    ql = np.arange(qn)
    kl = np.arange(kn)
    out = []
    nqb = rows // NA_QROWS
    lr, krl = np.arange(NA_QROWS)[:, None], np.arange(2 * NA_QROWS)[None, :]
    dr = np.clip(krl - NA_QROWS // 2 - lr + NA_WIN_ROWS - 1, 0, 2 * NA_WIN_ROWS - 2)
    col = np.arange(GRID_W)
    dc = np.clip(col[None, :] - col[:, None] + NA_WIN_COLS - 1, 0, 2 * NA_WIN_COLS - 2)
    oh_r = jnp.asarray(dr[..., None] == np.arange(2 * NA_WIN_ROWS - 1), F32)
    oh_c = jnp.asarray(dc[..., None] == np.arange(2 * NA_WIN_COLS - 1), F32)
    table = jnp.einsum('hab,lka,qcb->hlqkc', rpb, oh_r, oh_c,
                       precision=lax.Precision.HIGHEST).reshape(rpb.shape[0], qn, kn)
    for qb in (0, 1, nqb - 1):
        qr = NA_QROWS * qb + ql // GRID_W
        qc = ql % GRID_W
        kr = NA_QROWS * qb - NA_QROWS // 2 + kl // GRID_W
        kc = kl % GRID_W
        r0 = np.clip(qr - NA_WIN_ROWS // 2, 0, rows - NA_WIN_ROWS)
        c0 = np.clip(qc - NA_WIN_COLS // 2, 0, GRID_W - NA_WIN_COLS)
        ok = ((kr[None, :] >= r0[:, None]) & (kr[None, :] < r0[:, None] + NA_WIN_ROWS)
              & (kc[None, :] >= c0[:, None]) & (kc[None, :] < c0[:, None] + NA_WIN_COLS)
              & (kr[None, :] >= 0) & (kr[None, :] < rows))
        out.append(jnp.where(jnp.asarray(ok)[None], table, NEG))
    return jnp.stack(out)


def _na(proj, rpb, t_lat, t_ctx):
    rows = t_lat // GRID_W
    qn = NA_QROWS * GRID_W
    hn = qn // 2
    nqb = rows // NA_QROWS
    assert nqb >= 2 and t_ctx == hn
    bias = _na_bias(rpb.astype(F32), rows)
    last_half = 2 * nqb - 1
    ctx_blk = t_lat // hn
    cls = lambda b: jnp.where(b == 0, 0, jnp.where(b == nqb - 1, 2, 1))

    nh = NA_HEADS
    w = nh * HEAD_DIM
    groups = N_HEADS // nh

    def kv_specs(col0):
        return [pl.BlockSpec((hn, w), lambda h, b: (jnp.maximum(2 * b - 1, 0), col0 + h)),
                pl.BlockSpec((qn, w), lambda h, b: (b, col0 + h)),
                pl.BlockSpec((hn, w), lambda h, b: (jnp.minimum(2 * b + 2, last_half), col0 + h))]

    qcol, kcol, vcol = 4 * groups, 5 * groups, 6 * groups
    return pl.pallas_call(
        _na_kernel,
        grid=(groups, nqb),
        in_specs=[pl.BlockSpec((qn, w), lambda h, b: (b, qcol + h))]
        + kv_specs(kcol) + kv_specs(vcol)
        + [pl.BlockSpec((hn, w), lambda h, b: (ctx_blk, kcol + h)),
           pl.BlockSpec((hn, w), lambda h, b: (ctx_blk, vcol + h)),
           pl.BlockSpec((None, nh, qn, 2 * qn), lambda h, b: (cls(b), h, 0, 0))],
        out_specs=pl.BlockSpec((qn, w), lambda h, b: (b, h)),
        out_shape=jax.ShapeDtypeStruct((t_lat, HALF), F32),
        compiler_params=_cparams(2),
    )(proj, proj, proj, proj, proj, proj, proj, proj, proj, bias)


def _ctx_attn_kernel(q_ref, k_ref, v_ref, o_ref):
    qb = (q_ref[...].astype(F32) * QK_SCALE).astype(BF16)
    s = _nt(qb, k_ref[...].astype(BF16))
    p = jnp.exp(s - jnp.max(s, axis=1, keepdims=True))
    o_ref[...] = _nn(p.astype(BF16), v_ref[...].astype(BF16)) / jnp.sum(p, axis=1, keepdims=True)


def _ctx_attn(proj, t_lat, t_ctx):
    blk = t_lat // t_ctx
    spec = lambda c: pl.BlockSpec((t_ctx, HEAD_DIM), lambda h, c=c: (blk, c * N_HEADS + h))
    return pl.pallas_call(
        _ctx_attn_kernel,
        grid=(N_HEADS,),
        in_specs=[spec(4), spec(5), spec(6)],
        out_specs=pl.BlockSpec((t_ctx, HEAD_DIM), lambda h: (0, h)),
        out_shape=jax.ShapeDtypeStruct((t_ctx, HALF), F32),
        compiler_params=_cparams(1),
    )(proj, proj, proj)


def _gla_inter(q_ref, i_ref, f_ref, lb_ref, out_ref, st_ref, k_scr, b_scr, *, rev):
    n = q_ref.shape[0]
    mask = _scan_mask(rev, n)
    tri = mask.astype(BF16)
    last = 0 if rev else n - 1
    x = f_ref[...]
    key = lb_ref[0:1, :] * _sigmoid(-x)
    la, lc = lb_ref[1:2, :], lb_ref[2:3, :] + _log_sigmoid(x)
    logf = jnp.maximum(la, lc) + jnp.log1p(jnp.exp(-jnp.abs(la - lc)))
    b = sum(_nn(tri, p) for p in _split3(logf))
    b_end = b[last:last + 1, :]
    qx = q_ref[...]
    q = qx * _sigmoid(qx)
    qt = q * jnp.exp(b)
    kdec = key * jnp.exp(b_end - b)
    w_end = jnp.exp(b_end)
    k_scr[...] = key
    b_scr[...] = b
    st = st_ref[...]
    vb = _by_head(i_ref[...]).astype(BF16)
    o_inter = _bqk(_by_head(qt).astype(BF16), st.astype(BF16))
    for h in range(N_HEADS):
        out_ref[:, h * HEAD_DIM:(h + 1) * HEAD_DIM] = o_inter[h]
    st_ref[...] = st * _by_head(w_end) + jnp.einsum('hnv,hnk->hvk', vb, _by_head(kdec).astype(BF16),
                                                    preferred_element_type=F32)
    return mask, q, b_end


def _gla_intra_factored(mask, q, b_end, i_ref, out_ref, k_scr, b_scr):
    mid = 0.5 * b_end
    qm = q * jnp.exp(b_scr[...] - mid)
    km = k_scr[...] * jnp.exp(mid - b_scr[...])
    att = jnp.where(mask[None], _bqk(_by_head(qm).astype(BF16), _by_head(km).astype(BF16)), 0.0)
    o_intra = _bqd(att.astype(BF16), _by_head(i_ref[...]).astype(BF16))
    for h in range(N_HEADS):
        out_ref[:, h * HEAD_DIM:(h + 1) * HEAD_DIM] += o_intra[h]


def _gla_intra_exact(q, i_ref, out_ref, k_scr, b_scr, *, rev):
    n = q.shape[0]
    t_idx = lax.broadcasted_iota(jnp.int32, (n, 1), 0)
    for h in range(N_HEADS):
        hs = slice(h * HEAD_DIM, (h + 1) * HEAD_DIM)
        q_h = q[:, hs]
        b_h = b_scr[:, hs]

        def body(grp, acc):
            base = pl.multiple_of(grp * 8, 8)
            b8, k8, v8 = b_scr[pl.ds(base, 8), hs], k_scr[pl.ds(base, 8), hs], i_ref[pl.ds(base, 8), hs]
            for r in range(8):
                s = base + r
                e = jnp.exp(jnp.minimum(b_h - b8[r:r + 1, :], 0.0))
                a = jnp.sum(q_h * k8[r:r + 1, :] * e, axis=1, keepdims=True)
                a = jnp.where((t_idx <= s) if rev else (t_idx >= s), a, 0.0)
                acc = acc + a * v8[r:r + 1, :]
            return acc

        out_ref[:, hs] += lax.fori_loop(0, n // 8, body, jnp.zeros((n, HEAD_DIM), F32))


def _gla_kernel(*refs):
    fw_in, bw_in, lb_ref = refs[0:3], refs[3:6], refs[6]
    out_f, out_b = refs[7:9]
    st_f, k_f, b_f, st_b, k_b, b_b = refs[9:]

    @pl.when(pl.program_id(0) == 0)
    def _():
        st_f[...] = jnp.zeros_like(st_f)
        st_b[...] = jnp.zeros_like(st_b)

    mask_f, q_f, end_f = _gla_inter(*fw_in, lb_ref, out_f, st_f, k_f, b_f, rev=False)
    mask_b, q_b, end_b = _gla_inter(*bw_in, lb_ref, out_b, st_b, k_b, b_b, rev=True)
    safe = jnp.minimum(jnp.min(end_f), jnp.min(end_b)) >= 2.0 * GLA_SAFE_LOG

    @pl.when(safe)
    def _():
        _gla_intra_factored(mask_f, q_f, end_f, fw_in[1], out_f, k_f, b_f)
        _gla_intra_factored(mask_b, q_b, end_b, bw_in[1], out_b, k_b, b_b)

    @pl.when(jnp.logical_not(safe))
    def _():
        _gla_intra_exact(q_f, fw_in[1], out_f, k_f, b_f, rev=False)
        _gla_intra_exact(q_b, bw_in[1], out_b, k_b, b_b, rev=True)


def _gla(proj, lb, n_lat_chunks):
    t_all = proj.shape[0]
    n = GLA_CHUNK
    n_chunks = t_all // n

    def dir_specs(rev):
        blk = _scan_block(rev, n_chunks, n_lat_chunks)
        colblk = lambda c: pl.BlockSpec((n, HALF), lambda s, c=c: (blk(s), c))
        return [colblk(0), colblk(1), colblk(3 if rev else 2)], pl.BlockSpec((n, HALF), lambda s: (blk(s), 0))

    (in_f, out_f), (in_b, out_b) = dir_specs(False), dir_specs(True)
    state = [pltpu.VMEM((N_HEADS, HEAD_DIM, HEAD_DIM), F32), pltpu.VMEM((n, HALF), F32),
             pltpu.VMEM((n, HALF), F32)]
    lb = lb.reshape(1, HALF)
    lb_rows = jnp.concatenate([1.0 - lb, jnp.log(lb), jnp.log1p(-lb)], axis=0)
    return pl.pallas_call(
        _gla_kernel,
        grid=(n_chunks,),
        in_specs=in_f + in_b + [pl.BlockSpec((3, HALF), lambda s: (0, 0))],
        out_specs=[out_f, out_b],
        out_shape=[jax.ShapeDtypeStruct((t_all, HALF), F32)] * 2,
        scratch_shapes=state + state,
        compiler_params=_cparams(1),
    )(*([proj] * 6), lb_rows)


def _s5_mats(a_re, a_im, log_dt, b_re, b_im, c_re, c_im):
    hp = lax.Precision.HIGHEST
    n = S5_CHUNK
    dt = jnp.exp(log_dt)[..., None]
    la_re, la_im = a_re * dt, a_im * dt
    mag = jnp.exp(la_re)
    ab_re, ab_im = mag * jnp.cos(la_im), mag * jnp.sin(la_im)
    nr, ni = ab_re - 1.0, ab_im
    den = jnp.square(a_re) + jnp.square(a_im)
    cr = (nr * a_re + ni * a_im) / den
    ci = (ni * a_re - nr * a_im) / den
    bb_re = cr[..., None] * b_re - ci[..., None] * b_im
    bb_im = cr[..., None] * b_im + ci[..., None] * b_re

    def apow(tau):
        tau = jnp.asarray(tau, F32)[None, None, :, None]
        m = jnp.exp(tau * la_re[:, :, None, :])
        return m * jnp.cos(tau * la_im[:, :, None, :]), m * jnp.sin(tau * la_im[:, :, None, :])

    def c_apow(tau):
        pr, pi = apow(tau)
        return (c_re[:, :, None] * pr[:, :, :, None] - c_im[:, :, None] * pi[:, :, :, None],
                c_re[:, :, None] * pi[:, :, :, None] + c_im[:, :, None] * pr[:, :, :, None])

    ca_re, ca_im = c_apow(np.arange(n))
    kern = jnp.einsum('dgtip,dgpj->dgtij', jnp.concatenate([ca_re, -ca_im], axis=-1),
                      jnp.concatenate([bb_re, bb_im], axis=2), precision=hp)
    s_idx, t_idx = np.arange(n)[:, None], np.arange(n)[None, :]
    g = a_re.shape[1]
    lp = n * S5_GROUP

    def toeplitz(kd, lag):
        onehot = jnp.asarray(lag[None] == np.arange(n)[:, None, None], F32)
        return jnp.einsum('lst,glij->gsjti', onehot, kd, precision=hp).reshape(g, lp, lp)

    kmat = toeplitz(kern[0], t_idx - s_idx) + toeplitz(kern[1], s_idx - t_idx)

    def emat(d, tau):
        pr, pi = apow(tau)
        pr, pi = pr[d][:, :, None, :], pi[d][:, :, None, :]
        br, bi = bb_re[d].transpose(0, 2, 1)[:, None], bb_im[d].transpose(0, 2, 1)[:, None]
        return jnp.concatenate([pr * br - pi * bi, pr * bi + pi * br], axis=-1).reshape(g, lp, 2 * S5_STATE)

    def fmat(d, tau):
        fr, fi = c_apow(tau)
        fr, fi = fr[d], fi[d]
        return jnp.concatenate([fr, -fi], axis=-1).transpose(0, 3, 1, 2).reshape(g, 2 * S5_STATE, lp)

    e_all = jnp.concatenate([emat(0, n - 1 - np.arange(n)), emat(1, np.arange(n))], axis=-1)
    f_all = jnp.concatenate([fmat(0, np.arange(n) + 1), fmat(1, n - np.arange(n))], axis=1)
    pr, pi = apow(np.array([n]))
    pr, pi = pr[:, :, 0], pi[:, :, 0]
    ar2 = jnp.concatenate([pr[0], pr[0], pr[1], pr[1]], axis=-1)
    ai2 = jnp.concatenate([-pi[0], pi[0], -pi[1], pi[1]], axis=-1)
    return kmat.astype(BF16), e_all.astype(BF16), f_all.astype(BF16), ar2, ai2


def _s5_state_kernel(u_ref, e_ref, s_ref):
    s_ref[...] = _nn(u_ref[...], e_ref[...])


def _s5_scan_kernel(sf_ref, sb_ref, ar_ref, ai_ref, hf_ref, hb_ref, stf_ref, stb_ref):
    nb = sf_ref.shape[0]
    w = 2 * S5_STATE
    ar, ai = ar_ref[...], ai_ref[...]

    @pl.when(pl.program_id(0) == 0)
    def _():
        stf_ref[...] = jnp.zeros_like(stf_ref)
        stb_ref[...] = jnp.zeros_like(stb_ref)

    def step(h, s, lo):
        return ar[:, lo:lo + w] * h + ai[:, lo:lo + w] * pltpu.roll(h, S5_STATE, 1) + s

    hf, hb = stf_ref[...], stb_ref[...]
    for j in range(nb):
        hf_ref[j] = hf
        hf = step(hf, sf_ref[j], 0)
        hb_ref[nb - 1 - j] = hb
        hb = step(hb, sb_ref[nb - 1 - j], w)
    stf_ref[...] = hf
    stb_ref[...] = hb


def _s5_out_kernel(u_ref, k_ref, h_ref, f_ref, y_ref):
    y_ref[...] = _nn(u_ref[...], k_ref[...]) + _nn(h_ref[...].astype(BF16), f_ref[...])


def _s5(u, mats, n_lat_chunks):
    kmat, emat, fmat, ar2, ai2 = mats
    t_all = u.shape[0]
    n = S5_CHUNK
    nc = t_all // n
    g = S5_GROUPS
    lp = n * S5_GROUP
    w4 = 4 * S5_STATE
    ug = u.astype(BF16).reshape(nc, n, g, S5_GROUP).transpose(2, 0, 1, 3).reshape(g, nc, lp)
    grp = lambda a, b: pl.BlockSpec((None, a, b), lambda i: (i, 0, 0))
    s_end = pl.pallas_call(
        _s5_state_kernel, grid=(g,),
        in_specs=[grp(nc, lp), grp(lp, w4)], out_specs=grp(nc, w4),
        out_shape=jax.ShapeDtypeStruct((g, nc, w4), F32), compiler_params=_cparams(1),
    )(ug, emat)
    nb = S5_SCAN_BLOCK
    w2 = 2 * S5_STATE
    fw_blk = _scan_block(False, nc // nb, n_lat_chunks // nb)
    bw_blk = _scan_block(True, nc // nb, n_lat_chunks // nb)
    s_t = s_end.transpose(1, 0, 2)
    coef = pl.BlockSpec((g, w4), lambda i: (0, 0))
    h_fw, h_bw = pl.pallas_call(
        _s5_scan_kernel, grid=(nc // nb,),
        in_specs=[pl.BlockSpec((nb, g, w2), lambda i: (fw_blk(i), 0, 0)),
                  pl.BlockSpec((nb, g, w2), lambda i: (bw_blk(i), 0, 1)), coef, coef],
        out_specs=[pl.BlockSpec((nb, g, w2), lambda i: (fw_blk(i), 0, 0)),
                   pl.BlockSpec((nb, g, w2), lambda i: (bw_blk(i), 0, 0))],
        out_shape=[jax.ShapeDtypeStruct((nc, g, w2), F32)] * 2,
        scratch_shapes=[pltpu.VMEM((g, w2), F32)] * 2,
        compiler_params=_cparams(1),
    )(s_t, s_t, ar2, ai2)
    h_in = jnp.concatenate([h_fw, h_bw], axis=-1).transpose(1, 0, 2)
    y = pl.pallas_call(
        _s5_out_kernel, grid=(g,),
        in_specs=[grp(nc, lp), grp(lp, lp), grp(nc, w4), grp(w4, lp)], out_specs=grp(nc, lp),
        out_shape=jax.ShapeDtypeStruct((g, nc, lp), F32), compiler_params=_cparams(1),
    )(ug, kmat, h_in, fmat)
    return y.reshape(g, nc, n, S5_GROUP).transpose(1, 2, 0, 3).reshape(t_all, HALF)


def _top_rows_sorted(x, k):
    n = x.shape[0] // 8
    cols = [x[8 * i:8 * (i + 1), :] for i in range(n)]
    size = 2
    while size <= n:
        stride = size // 2
        while stride >= 1:
            for i in range(n):
                j = i ^ stride
                if j > i:
                    hi, lo = jnp.maximum(cols[i], cols[j]), jnp.minimum(cols[i], cols[j])
                    cols[i], cols[j] = (hi, lo) if (i & size) == 0 else (lo, hi)
            stride //= 2
        size *= 2
    rows = []
    for r in range(k):
        mx = jnp.max(cols[0], axis=0, keepdims=True)
        rows.append(mx)
        popped = cols[0] == mx
        live = min(n, k - r)
        for i in range(live):
            nxt = cols[i + 1] if i + 1 < n else NEG
            cols[i] = jnp.where(popped, nxt, cols[i])
    return rows


def _peer_route_kernel(ht_ref, wq_ref, keys_ref, e1_ref, e0_ref, th_ref, top_scr):
    kk = PEER_TOPK
    qt = _nn(wq_ref[...], ht_ref[...])
    for hp in range(2 * PEER_HEADS):
        sc = _nn(keys_ref[hp], qt[hp * HEAD_DIM:(hp + 1) * HEAD_DIM, :].astype(BF16))
        h, p = divmod(hp, 2)
        if p == 0:
            e0_ref[h] = sc
        else:
            e1_ref[h] = sc
        for r, mx in enumerate(_top_rows_sorted(sc, kk + 1)):
            top_scr[hp, r:r + 1, :] = mx
        top_scr[hp, kk + 1:, :] = jnp.full((PEER_TOP_ROWS - kk - 1, sc.shape[1]), NEG, F32)

    row = lax.broadcasted_iota(jnp.int32, (PEER_TOP_ROWS, 1), 0)
    for h in range(PEER_HEADS):
        a = top_scr[2 * h]
        b = top_scr[2 * h + 1]
        parts = [a[0:1] + b, a[1:2] + b[0:8], a[2:3] + b[0:8], a[3:4] + b[0:8]]
        parts += [jnp.where(row >= 4, a + b[j:j + 1], NEG) for j in range(3)]
        cand = jnp.maximum(jnp.concatenate(parts, axis=0), NEG)
        pad = jnp.full((PEER_N_KEYS - cand.shape[0], cand.shape[1]), NEG, F32)
        top = _top_rows_sorted(jnp.concatenate([cand, pad], axis=0), kk + 1)
        thr = 0.5 * (top[kk - 1] + top[kk])
        z = jnp.sum(jnp.where(cand >= thr, jnp.exp(cand - top[0]), 0.0), axis=0, keepdims=True)
        e0_ref[h] = jnp.exp(e0_ref[h] - a[0:1]) / z
        e1_ref[h] = jnp.exp(e1_ref[h] - b[0:1])
        th_ref[h:h + 1, :] = jnp.exp(thr - top[0]) / z


def _peer_expert_kernel(ht_ref, e1_ref, e0_ref, th_ref, *rest, n_tiles):
    u_refs = rest[:PEER_U_SPLIT]
    vt_ref, o_ref, at0_scr, at1_scr, g0_scr, g1_scr = rest[PEER_U_SPLIT:]
    s = pl.program_id(1)
    assert n_tiles % 2 == 0 and n_tiles >= 4
    ni = u_refs[0].shape[0] // PEER_N_KEYS
    kc = u_refs[0].shape[1]
    at_scr, g_scr = (at0_scr, at1_scr), (g0_scr, g1_scr)

    def stages(par, dot1, gate, dot2, first=False):
        if dot1:
            at_scr[par][...] = _nn(jnp.concatenate([u_k[...] for u_k in u_refs], axis=1), ht_ref[...])
        if gate:
            at_old, g_new = at_scr[1 - par], g_scr[1 - par]
            nj, nc = PEER_GATE_ROWS, 128
            for ii in range(ni):
                i = (s - 1) * ni + ii
                e0_rows = [e0_ref[h, pl.ds(i, 1), :] for h in range(PEER_HEADS)]
                for c in range(at_old.shape[1] // nc):
                    cs = slice(c * nc, (c + 1) * nc)
                    for jb in range(PEER_N_KEYS // nj):
                        js = slice(jb * nj, (jb + 1) * nj)
                        w = None
                        for h in range(PEER_HEADS):
                            p = e1_ref[h, js, cs] * e0_rows[h][:, cs]
                            t = jnp.where(p >= th_ref[h:h + 1, cs], p, 0.0)
                            w = t if w is None else w + t
                        rs = slice(ii * PEER_N_KEYS + jb * nj, ii * PEER_N_KEYS + (jb + 1) * nj)
                        g_new[rs, cs] = (_gelu_tanh(at_old[rs, cs]) * w).astype(BF16)
        if dot2:
            prod = _nn(vt_ref[...], g_scr[par][...])
            if first:
                o_ref[...] = prod
            else:
                o_ref[...] += prod

    mid = jnp.logical_and(s > 2, s < n_tiles)
    pl.when(s == 0)(lambda: stages(0, True, False, False))
    pl.when(s == 1)(lambda: stages(1, True, True, False))
    pl.when(s == 2)(lambda: stages(0, True, True, True, first=True))
    pl.when(jnp.logical_and(mid, s % 2 == 0))(lambda: stages(0, True, True, True))
    pl.when(jnp.logical_and(mid, s % 2 == 1))(lambda: stages(1, True, True, True))
    pl.when(s == n_tiles)(lambda: stages(0, False, True, True))
    pl.when(s == n_tiles + 1)(lambda: stages(1, False, False, True))


def _tok3(t):
    return pl.BlockSpec((PEER_HEADS, PEER_N_KEYS, t), lambda i, *_: (0, 0, i))


def _peer_route(h2t, wq_t_bf, keys_bf):
    d, t_all = h2t.shape
    nh, nk = PEER_HEADS, PEER_N_KEYS
    t1 = PEER_P1_TILE
    return pl.pallas_call(
        _peer_route_kernel,
        grid=(t_all // t1,),
        in_specs=[pl.BlockSpec((d, t1), lambda i: (0, i)),
                  pl.BlockSpec(wq_t_bf.shape, lambda i: (0, 0)),
                  pl.BlockSpec(keys_bf.shape, lambda i: (0, 0, 0))],
        out_specs=[_tok3(t1), _tok3(t1), pl.BlockSpec((nh, t1), lambda i: (0, i))],
        out_shape=[jax.ShapeDtypeStruct((nh, nk, t_all), F32)] * 2 + [jax.ShapeDtypeStruct((nh, t_all), F32)],
        scratch_shapes=[pltpu.VMEM((2 * nh, PEER_TOP_ROWS, t1), F32)],
        compiler_params=_cparams(1),
    )(h2t, wq_t_bf, keys_bf)


def _padded_tokens(t_all):
    return t_all + (-t_all % PEER_TOK_TILE)


def _peer(h2t, wq_t_bf, keys_bf, u_r, vt_r, layer=0):
    return _peer_experts(h2t, *_peer_route(h2t, wq_t_bf, keys_bf), u_r, vt_r, layer)


def _peer_tables(u_all, v_all):
    return u_all.astype(BF16), v_all.transpose(0, 2, 1).astype(BF16)


def _peer_experts(h2t, e1, e0, th, u_r, vt_r, layer):
    d, t_all = h2t.shape
    tt = PEER_TOK_TILE
    et = PEER_EXP_TILE
    n_tiles = u_r.shape[1] // et
    kc = d // PEER_U_SPLIT
    tok = pl.BlockSpec((d, tt), lambda i, s: (0, i))
    return pl.pallas_call(
        functools.partial(_peer_expert_kernel, n_tiles=n_tiles),
        grid=(t_all // tt, n_tiles + 2),
        in_specs=[tok, _tok3(tt), _tok3(tt), pl.BlockSpec((PEER_HEADS, tt), lambda i, s: (0, i))]
        + [pl.BlockSpec((None, et, kc), lambda i, s, k=k: (layer, jnp.minimum(s, n_tiles - 1), k))
           for k in range(PEER_U_SPLIT)]
        + [pl.BlockSpec((None, d, et), lambda i, s: (layer, 0, jnp.clip(s - 2, 0, n_tiles - 1)))],
        out_specs=tok,
        out_shape=jax.ShapeDtypeStruct((d, t_all), F32),
        scratch_shapes=[pltpu.VMEM((et, tt), F32)] * 2 + [pltpu.VMEM((et, tt), BF16)] * 2,
        compiler_params=_cparams(2),
    )(h2t, e1, e0, th, *([u_r] * PEER_U_SPLIT), vt_r)


def _rope_tables(t_lat, t_ctx):
    pos = np.arange(t_lat)
    nfreq = HEAD_DIM // 4
    freqs = ROPE_BASE ** (-jnp.arange(nfreq, dtype=F32) / nfreq)
    ang_r = jnp.asarray(pos // GRID_W, F32)[:, None] * freqs
    ang_c = jnp.asarray(pos % GRID_W, F32)[:, None] * freqs
    cos = jnp.concatenate([jnp.cos(ang_r), jnp.cos(ang_r), jnp.cos(ang_c), jnp.cos(ang_c)], axis=1)
    sin = jnp.concatenate([-jnp.sin(ang_r), jnp.sin(ang_r), -jnp.sin(ang_c), jnp.sin(ang_c)], axis=1)
    cos = jnp.concatenate([cos, jnp.ones((t_ctx, HEAD_DIM), F32)], axis=0)
    sin = jnp.concatenate([sin, jnp.zeros((t_ctx, HEAD_DIM), F32)], axis=0)
    return cos, sin


def _peer_and_norm(x1, h2t, mods, ln_g, ln_b, n_lat_tiles, w_q, sub_keys, tables, layer, mods_next=None):
    keys = sub_keys.reshape(2 * PEER_HEADS, PEER_N_KEYS, -1).astype(BF16)
    pt = _peer(h2t, w_q.T.astype(BF16), keys, *tables, layer)
    return _ln2(x1, pt, mods, ln_g, ln_b, n_lat_tiles, mods_next)


def _layer_ab(x_all, mods, t_lat, t_ctx, w_in, w_out, igate_b, fgate_b, rpb, ln_g, ln_b, xm=None):
    n_lat_tiles = t_lat // ROW_TILE
    if xm is None:
        xm = _modulate(x_all, mods, n_lat_tiles)
    ng = 4 * N_HEADS
    w_main = jnp.concatenate([w_in[:, :4 * HALF], w_in[:, 4 * HALF + ng:]], axis=1).astype(BF16)
    w_gate = jnp.pad(w_in[:, 4 * HALF:4 * HALF + ng], ((0, 0), (0, 128 - ng))).astype(BF16)
    proj = _inproj(xm, w_main, BF16)
    gates = _inproj(xm, w_gate)
    cos_t, sin_t = _rope_tables(t_lat, t_ctx)
    h_fw, h_bw = _mlstm(proj, gates, gates.T, cos_t, sin_t, igate_b, fgate_b, t_lat // SCAN_CHUNK)
    y_na = jnp.concatenate([_na(proj, rpb, t_lat, t_ctx), _ctx_attn(proj, t_lat, t_ctx)], axis=0)
    return _outproj([(h_fw, 0), (h_bw, 0), (proj, 3)], [(y_na, 0)], w_out.astype(BF16), x_all, mods,
                    ln_g, ln_b, n_lat_tiles)


def _layer_cd(x_all, mods, t_lat, w_in, w_out, lb, s5p, s5_d, glu_w, glu_b, ln_g, ln_b, xm=None):
    n_lat_tiles = t_lat // ROW_TILE
    if xm is None:
        xm = _modulate(x_all, mods, n_lat_tiles)
    proj = _inproj(xm, w_in.astype(BF16))
    o_fw, o_bw = _gla(proj, lb, t_lat // GLA_CHUNK)
    y5 = _s5(proj[:, 5 * HALF:], _s5_mats(*[p.astype(F32) for p in s5p]), t_lat // S5_CHUNK)
    return _outproj([(o_fw, 0), (o_bw, 0), (proj, 4)], [(y5, 0), (proj, 5)], w_out.astype(BF16), x_all, mods,
                    ln_g, ln_b, n_lat_tiles,
                    glu_params=(s5_d.astype(F32), glu_w.astype(BF16), glu_b.astype(F32)))


def kernel(x, c, ctx, c_ctx, ada_w, ada_b, ln_g, ln_b, ab_w_in, ab_w_out, mlstm_igate_b, mlstm_fgate_b, na_rpb, cd_w_in, cd_w_out, hgrn_lb_logits, s5_a_re, s5_a_im, s5_log_dt, s5_b_re, s5_b_im, s5_c_re, s5_c_im, s5_d, s5_glu_w, s5_glu_b, peer_w_q, peer_sub_keys, peer_u, peer_v):
    t_lat, t_ctx = x.shape[1], ctx.shape[1]
    assert x.shape[0] == 1 and t_ctx == ROW_TILE and t_lat % (NA_QROWS * GRID_W) == 0
    depth = ada_w.shape[0]
    lb_soft = jax.nn.softmax(hgrn_lb_logits.astype(F32), axis=0)
    lower_bounds = jnp.cumsum(lb_soft, axis=0) - lb_soft[0]
    x_all = jnp.concatenate([x[0], ctx[0]], axis=0).astype(F32)
    n_lat_tiles = t_lat // ROW_TILE
    mods_all = [_adaln(c, c_ctx, ada_w, ada_b, l) for l in range(depth)]
    xm = _modulate(x_all, mods_all[0], n_lat_tiles)
    tables = _peer_tables(peer_u, peer_v)
    for l in range(depth):
        j = l // 2
        mods = mods_all[l]
        if l % 2 == 0:
            x1, h2t = _layer_ab(x_all, mods, t_lat, t_ctx, ab_w_in[j], ab_w_out[j], mlstm_igate_b[j],
                                mlstm_fgate_b[j], na_rpb[j], ln_g[l, 0], ln_b[l, 0], xm=xm)
        else:
            s5p = (s5_a_re[j], s5_a_im[j], s5_log_dt[j], s5_b_re[j], s5_b_im[j], s5_c_re[j], s5_c_im[j])
            x1, h2t = _layer_cd(x_all, mods, t_lat, cd_w_in[j], cd_w_out[j], lower_bounds[l], s5p, s5_d[j],
                                s5_glu_w[j], s5_glu_b[j], ln_g[l, 0], ln_b[l, 0], xm=xm)
        peer_args = (x1, h2t, mods, ln_g[l, 1], ln_b[l, 1], n_lat_tiles, peer_w_q[l], peer_sub_keys[l],
                     tables, l)
        if l + 1 < depth:
            x_all, xm = _peer_and_norm(*peer_args, mods_next=mods_all[l + 1])
        else:
            x_all = _peer_and_norm(*peer_args)
    return x_all[None].astype(x.dtype)
```
